```python
import jax, jax.numpy as jnp
from jax import lax
import numpy as np

D_MODEL = 1024
BATCH = 8
SEQ = 4096
DEPTH = 1

CONV_WIDTH = 1024
CONV_KERNEL = 31
N_HEADS = 8
QK_NOPE_DIM = 128
QK_ROPE_DIM = 64
V_DIM = 128
Q_LORA_RANK = 256
KV_LORA_RANK = 256
ATTN_WIDTH = N_HEADS * V_DIM
ROPE_THETA = 10000.0
Q_BLOCK = 128
EPS = 1e-6

IN_SIZES = (
    CONV_WIDTH,
    CONV_WIDTH,
    CONV_WIDTH,
    Q_LORA_RANK,
    KV_LORA_RANK,
    QK_ROPE_DIM,
    ATTN_WIDTH,
    D_MODEL,
    D_MODEL,
)
IN_COLS = 3 * CONV_WIDTH + Q_LORA_RANK + KV_LORA_RANK + QK_ROPE_DIM + ATTN_WIDTH + 2 * D_MODEL

kernel_name = 'hybrid_conformer_mla_block'


def rms_norm(x, w):
    xf = x.astype(jnp.float32)
    y = xf * lax.rsqrt(jnp.mean(xf * xf, axis=-1, keepdims=True) + EPS)
    return (y * w.astype(jnp.float32)).astype(x.dtype)


def layer_norm(x, w, b):
    xf = x.astype(jnp.float32)
    mu = jnp.mean(xf, axis=-1, keepdims=True)
    xc = xf - mu
    var = jnp.mean(xc * xc, axis=-1, keepdims=True)
    y = xc * lax.rsqrt(var + EPS)
    return (y * w.astype(jnp.float32) + b.astype(jnp.float32)).astype(x.dtype)


def apply_rope(x, cos, sin):
    x1, x2 = jnp.split(x, 2, axis=-1)
    return jnp.concatenate([x1 * cos - x2 * sin, x1 * sin + x2 * cos], axis=-1).astype(x.dtype)


def causal_depthwise_conv(u, w, b):
    k, ch = w.shape
    out = lax.conv_general_dilated(
        u, w[:, None, :].astype(u.dtype), window_strides=(1,), padding=[(k - 1, 0)],
        dimension_numbers=('NWC', 'WIO', 'NWC'), feature_group_count=ch)
    return out + b


def mla_attention(q_nope, q_rope, k_nope, k_rope, v):
    b, s, h, _ = q_nope.shape
    nb = s // Q_BLOCK
    scale = (QK_NOPE_DIM + QK_ROPE_DIM) ** -0.5
    qn = q_nope.reshape(b, nb, Q_BLOCK, h, QK_NOPE_DIM).transpose(1, 0, 2, 3, 4)
    qr = q_rope.reshape(b, nb, Q_BLOCK, h, QK_ROPE_DIM).transpose(1, 0, 2, 3, 4)
    key_idx = jnp.arange(s)

    def block(args):
        qn_b, qr_b, i = args
        scores = (jnp.einsum('bqhd,bkhd->bhqk', qn_b, k_nope)
                  + jnp.einsum('bqhd,bkd->bhqk', qr_b, k_rope)).astype(jnp.float32) * scale
        q_idx = i * Q_BLOCK + jnp.arange(Q_BLOCK)
        mask = key_idx[None, :] <= q_idx[:, None]
        scores = jnp.where(mask[None, None], scores, -jnp.inf)
        p = jax.nn.softmax(scores, axis=-1).astype(v.dtype)
        return jnp.einsum('bhqk,bkhd->bqhd', p, v)

    out = lax.map(block, (qn, qr, jnp.arange(nb)))
    return out.transpose(1, 0, 2, 3, 4).reshape(b, s, h * V_DIM)


def _fwd_setup_inputs(seed: int = 0) -> dict:
    key = jax.random.key(seed)
    ks = jax.random.split(key, 24)
    f32 = jnp.float32

    def nrm(k, shape, fan_in, mult=1.0):
        return jax.random.normal(k, shape, f32) * (mult * fan_in ** -0.5)

    def gain(k, shape):
        return 1.0 + 0.02 * jax.random.normal(k, shape, f32)

    x = jax.random.normal(ks[0], (BATCH, SEQ, D_MODEL), f32)
    c = jax.random.normal(ks[1], (BATCH, D_MODEL), f32)
    offsets = jax.random.randint(ks[2], (BATCH, 1), 0, 2048, jnp.int32)
    positions = (offsets + jnp.arange(SEQ, dtype=jnp.int32)[None, :]).astype(jnp.int32)
    return {
        'x': x,
        'c': c,
        'positions': positions,
        'w_ada': nrm(ks[3], (DEPTH, D_MODEL, 3 * D_MODEL), D_MODEL, 0.5),
        'b_ada': 0.02 * jax.random.normal(ks[4], (DEPTH, 3 * D_MODEL), f32),
        'norm_w': gain(ks[5], (DEPTH, D_MODEL)),
        'w_in': nrm(ks[6], (DEPTH, D_MODEL, IN_COLS), D_MODEL),
        'conv_w': nrm(ks[7], (DEPTH, CONV_KERNEL, CONV_WIDTH), CONV_KERNEL),
        'conv_b': 0.02 * jax.random.normal(ks[8], (DEPTH, CONV_WIDTH), f32),
        'conv_ln_w': gain(ks[9], (DEPTH, CONV_WIDTH)),
        'conv_ln_b': 0.02 * jax.random.normal(ks[10], (DEPTH, CONV_WIDTH), f32),
        'w_conv_out': nrm(ks[11], (DEPTH, CONV_WIDTH, D_MODEL), CONV_WIDTH),
        'q_norm_w': gain(ks[12], (DEPTH, Q_LORA_RANK)),
        'w_uq': nrm(ks[13], (DEPTH, Q_LORA_RANK, N_HEADS * (QK_NOPE_DIM + QK_ROPE_DIM)), Q_LORA_RANK),
        'kv_norm_w': gain(ks[14], (DEPTH, KV_LORA_RANK)),
        'w_ukv': nrm(ks[15], (DEPTH, KV_LORA_RANK, N_HEADS * (QK_NOPE_DIM + V_DIM)), KV_LORA_RANK),
        'w_attn_out': nrm(ks[16], (DEPTH, ATTN_WIDTH, D_MODEL), ATTN_WIDTH),
        'w_out': nrm(ks[17], (DEPTH, D_MODEL, D_MODEL), D_MODEL),
        'final_norm_w': gain(ks[18], (D_MODEL,)),
    }


def _fwd_reference(x, c, positions, w_ada, b_ada, norm_w, w_in, conv_w, conv_b, conv_ln_w,
              conv_ln_b, w_conv_out, q_norm_w, w_uq, kv_norm_w, w_ukv, w_attn_out,
              w_out, final_norm_w):
    b, s, _ = x.shape
    split_points = []
    acc = 0
    for sz in IN_SIZES[:-1]:
        acc += sz
        split_points.append(acc)

    inv_freq = ROPE_THETA ** (-jnp.arange(0, QK_ROPE_DIM, 2, dtype=jnp.float32) / QK_ROPE_DIM)
    ang = positions.astype(jnp.float32)[..., None] * inv_freq
    cos, sin = jnp.cos(ang), jnp.sin(ang)

    c_act = jax.nn.silu(c)
    for l in range(DEPTH):
        shift, scale, gate = jnp.split(c_act @ w_ada[l] + b_ada[l], 3, axis=-1)
        h = rms_norm(x, norm_w[l]) * (1.0 + scale[:, None, :]) + shift[:, None, :]

        proj = h @ w_in[l]
        a_val, a_glu, a_gate, cq, ckv, k_pe, b_gate, g_a, g_b = jnp.split(proj, split_points, axis=-1)

        u = a_val * jax.nn.sigmoid(a_glu)
        u = causal_depthwise_conv(u, conv_w[l], conv_b[l])
        u = jax.nn.silu(layer_norm(u, conv_ln_w[l], conv_ln_b[l]))
        y_a = (u * jax.nn.silu(a_gate)) @ w_conv_out[l]

        q = (rms_norm(cq, q_norm_w[l]) @ w_uq[l]).reshape(b, s, N_HEADS, QK_NOPE_DIM + QK_ROPE_DIM)
        q_nope, q_rope = q[..., :QK_NOPE_DIM], q[..., QK_NOPE_DIM:]
        kv = (rms_norm(ckv, kv_norm_w[l]) @ w_ukv[l]).reshape(b, s, N_HEADS, QK_NOPE_DIM + V_DIM)
        k_nope, v = kv[..., :QK_NOPE_DIM], kv[..., QK_NOPE_DIM:]
        q_rope = apply_rope(q_rope, cos[:, :, None, :], sin[:, :, None, :])
        k_rope = apply_rope(k_pe, cos, sin)
        o = mla_attention(q_nope, q_rope, k_nope, k_rope, v)
        y_b = (o * jax.nn.silu(b_gate)) @ w_attn_out[l]

        merged = jax.nn.sigmoid(g_a) * y_a + jax.nn.sigmoid(g_b) * y_b
        x = x + gate[:, None, :] * (merged @ w_out[l])

    return rms_norm(x, final_norm_w)


import jax as _jax
import jax.numpy as _jnp

TWIN_FORMAT = 'train_step'
FWD_PARAMS = ['x', 'c', 'positions', 'w_ada', 'b_ada', 'norm_w', 'w_in', 'conv_w', 'conv_b', 'conv_ln_w', 'conv_ln_b', 'w_conv_out', 'q_norm_w', 'w_uq', 'kv_norm_w', 'w_ukv', 'w_attn_out', 'w_out', 'final_norm_w']
TWIN_WEIGHTS = ['w_ada', 'b_ada', 'norm_w', 'w_in', 'conv_w', 'conv_b', 'conv_ln_w', 'conv_ln_b', 'w_conv_out', 'q_norm_w', 'w_uq', 'kv_norm_w', 'w_ukv', 'w_attn_out', 'w_out', 'final_norm_w']
TWIN_DIFF_INPUT = 'x'
TWIN_INPUTS = ['x', 'c', 'positions', 'w_ada', 'b_ada', 'norm_w', 'w_in', 'conv_w', 'conv_b', 'conv_ln_w', 'conv_ln_b', 'w_conv_out', 'q_norm_w', 'w_uq', 'kv_norm_w', 'w_ukv', 'w_attn_out', 'w_out', 'final_norm_w', 'loss_target', 'm_w_ada', 'm_b_ada', 'm_norm_w', 'm_w_in', 'm_conv_w', 'm_conv_b', 'm_conv_ln_w', 'm_conv_ln_b', 'm_w_conv_out', 'm_q_norm_w', 'm_w_uq', 'm_kv_norm_w', 'm_w_ukv', 'm_w_attn_out', 'm_w_out', 'm_final_norm_w', 'v_w_ada', 'v_b_ada', 'v_norm_w', 'v_w_in', 'v_conv_w', 'v_conv_b', 'v_conv_ln_w', 'v_conv_ln_b', 'v_w_conv_out', 'v_q_norm_w', 'v_w_uq', 'v_kv_norm_w', 'v_w_ukv', 'v_w_attn_out', 'v_w_out', 'v_final_norm_w']
TWIN_OUTPUTS = ['loss', 'grad_x', 'grad_w_ada', 'grad_b_ada', 'grad_norm_w', 'grad_w_in', 'grad_conv_w', 'grad_conv_b', 'grad_conv_ln_w', 'grad_conv_ln_b', 'grad_w_conv_out', 'grad_q_norm_w', 'grad_w_uq', 'grad_kv_norm_w', 'grad_w_ukv', 'grad_w_attn_out', 'grad_w_out', 'grad_final_norm_w', 'delta_w_ada', 'delta_b_ada', 'delta_norm_w', 'delta_w_in', 'delta_conv_w', 'delta_conv_b', 'delta_conv_ln_w', 'delta_conv_ln_b', 'delta_w_conv_out', 'delta_q_norm_w', 'delta_w_uq', 'delta_kv_norm_w', 'delta_w_ukv', 'delta_w_attn_out', 'delta_w_out', 'delta_final_norm_w', 'new_m_w_ada', 'new_m_b_ada', 'new_m_norm_w', 'new_m_w_in', 'new_m_conv_w', 'new_m_conv_b', 'new_m_conv_ln_w', 'new_m_conv_ln_b', 'new_m_w_conv_out', 'new_m_q_norm_w', 'new_m_w_uq', 'new_m_kv_norm_w', 'new_m_w_ukv', 'new_m_w_attn_out', 'new_m_w_out', 'new_m_final_norm_w', 'new_v_w_ada', 'new_v_b_ada', 'new_v_norm_w', 'new_v_w_in', 'new_v_conv_w', 'new_v_conv_b', 'new_v_conv_ln_w', 'new_v_conv_ln_b', 'new_v_w_conv_out', 'new_v_q_norm_w', 'new_v_w_uq', 'new_v_kv_norm_w', 'new_v_w_ukv', 'new_v_w_attn_out', 'new_v_w_out', 'new_v_final_norm_w']
TWIN_LEAF_KINDS = {'loss': 'loss', 'grad_x': 'grad_x', 'grad_w_ada': 'grad_w', 'grad_b_ada': 'grad_w', 'grad_norm_w': 'grad_w', 'grad_w_in': 'grad_w', 'grad_conv_w': 'grad_w', 'grad_conv_b': 'grad_w', 'grad_conv_ln_w': 'grad_w', 'grad_conv_ln_b': 'grad_w', 'grad_w_conv_out': 'grad_w', 'grad_q_norm_w': 'grad_w', 'grad_w_uq': 'grad_w', 'grad_kv_norm_w': 'grad_w', 'grad_w_ukv': 'grad_w', 'grad_w_attn_out': 'grad_w', 'grad_w_out': 'grad_w', 'grad_final_norm_w': 'grad_w', 'delta_w_ada': 'delta_w', 'delta_b_ada': 'delta_w', 'delta_norm_w': 'delta_w', 'delta_w_in': 'delta_w', 'delta_conv_w': 'delta_w', 'delta_conv_b': 'delta_w', 'delta_conv_ln_w': 'delta_w', 'delta_conv_ln_b': 'delta_w', 'delta_w_conv_out': 'delta_w', 'delta_q_norm_w': 'delta_w', 'delta_w_uq': 'delta_w', 'delta_kv_norm_w': 'delta_w', 'delta_w_ukv': 'delta_w', 'delta_w_attn_out': 'delta_w', 'delta_w_out': 'delta_w', 'delta_final_norm_w': 'delta_w', 'new_m_w_ada': 'new_m', 'new_m_b_ada': 'new_m', 'new_m_norm_w': 'new_m', 'new_m_w_in': 'new_m', 'new_m_conv_w': 'new_m', 'new_m_conv_b': 'new_m', 'new_m_conv_ln_w': 'new_m', 'new_m_conv_ln_b': 'new_m', 'new_m_w_conv_out': 'new_m', 'new_m_q_norm_w': 'new_m', 'new_m_w_uq': 'new_m', 'new_m_kv_norm_w': 'new_m', 'new_m_w_ukv': 'new_m', 'new_m_w_attn_out': 'new_m', 'new_m_w_out': 'new_m', 'new_m_final_norm_w': 'new_m', 'new_v_w_ada': 'new_v', 'new_v_b_ada': 'new_v', 'new_v_norm_w': 'new_v', 'new_v_w_in': 'new_v', 'new_v_conv_w': 'new_v', 'new_v_conv_b': 'new_v', 'new_v_conv_ln_w': 'new_v', 'new_v_conv_ln_b': 'new_v', 'new_v_w_conv_out': 'new_v', 'new_v_q_norm_w': 'new_v', 'new_v_w_uq': 'new_v', 'new_v_kv_norm_w': 'new_v', 'new_v_w_ukv': 'new_v', 'new_v_w_attn_out': 'new_v', 'new_v_w_out': 'new_v', 'new_v_final_norm_w': 'new_v'}


def _forward(args):
    return _fwd_reference(*[args[k] for k in FWD_PARAMS])


def _output_shape():
    def fwd():
        inp = _fwd_setup_inputs(0)
        return _fwd_reference(*[inp[k] for k in FWD_PARAMS])
    out = _jax.eval_shape(fwd)
    return out.shape, out.dtype

N_MICROBATCH = 1
ADAM_LR = 0.001
ADAM_B1 = 0.9
ADAM_B2 = 0.999
ADAM_EPS = 1e-08
ADAM_WD = 0.01
ADAM_STEP = 10
PER_EXAMPLE_BATCH_AXIS = {'x': 0, 'c': 0, 'positions': 0, 'loss_target': 0}
SHARED_INPUTS = []
_WEIGHT_DTYPES = {'w_ada': _jnp.float32, 'b_ada': _jnp.float32, 'norm_w': _jnp.float32, 'w_in': _jnp.float32, 'conv_w': _jnp.float32, 'conv_b': _jnp.float32, 'conv_ln_w': _jnp.float32, 'conv_ln_b': _jnp.float32, 'w_conv_out': _jnp.float32, 'q_norm_w': _jnp.float32, 'w_uq': _jnp.float32, 'kv_norm_w': _jnp.float32, 'w_ukv': _jnp.float32, 'w_attn_out': _jnp.float32, 'w_out': _jnp.float32, 'final_norm_w': _jnp.float32}
MOMENT_SCALE = {'w_ada': 1.896416e-02, 'b_ada': 3.140904e-02, 'norm_w': 1.866853e-02, 'w_in': 8.357197e-03, 'conv_w': 1.201337e-02, 'conv_b': 2.239675e-02, 'conv_ln_w': 1.407663e-02, 'conv_ln_b': 1.213207e-02, 'w_conv_out': 1.157047e-02, 'q_norm_w': 6.844030e-03, 'w_uq': 2.781546e-03, 'kv_norm_w': 1.568786e-02, 'w_ukv': 5.111653e-03, 'w_attn_out': 6.777557e-03, 'w_out': 1.341560e-02, 'final_norm_w': 3.201216e+01}


def _to_microbatches(a, axis):
    t = _jnp.moveaxis(a, axis, 0)
    t = t.reshape((N_MICROBATCH, t.shape[0] // N_MICROBATCH) + t.shape[1:])
    return _jnp.moveaxis(t, 1, axis + 1)


def setup_inputs(seed: int = 0) -> dict:
    inp = _fwd_setup_inputs(seed)
    key = _jax.random.fold_in(_jax.random.key(seed), 7919)
    shape, _ = _output_shape()
    out = dict(inp)
    out["loss_target"] = _jax.random.normal(_jax.random.fold_in(key, 0), shape, _jnp.float32)
    for i, name in enumerate(TWIN_WEIGHTS):
        w = inp[name].astype(_jnp.float32)
        if MOMENT_SCALE is None:
            s = _jnp.sqrt(_jnp.mean(_jnp.square(w)) + 1e-30)
        else:
            s = MOMENT_SCALE[name]
        km, kv = _jax.random.split(_jax.random.fold_in(key, i + 1))
        out[name] = w
        out["m_" + name] = s * _jax.random.normal(km, w.shape, _jnp.float32)
        out["v_" + name] = (s * s) * _jax.random.uniform(kv, w.shape, _jnp.float32, 0.5, 1.5)
    if N_MICROBATCH > 1:
        for name, axis in PER_EXAMPLE_BATCH_AXIS.items():
            out[name] = _to_microbatches(out[name], axis)
    return {'x': out['x'], 'c': out['c'], 'positions': out['positions'], 'w_ada': out['w_ada'], 'b_ada': out['b_ada'], 'norm_w': out['norm_w'], 'w_in': out['w_in'], 'conv_w': out['conv_w'], 'conv_b': out['conv_b'], 'conv_ln_w': out['conv_ln_w'], 'conv_ln_b': out['conv_ln_b'], 'w_conv_out': out['w_conv_out'], 'q_norm_w': out['q_norm_w'], 'w_uq': out['w_uq'], 'kv_norm_w': out['kv_norm_w'], 'w_ukv': out['w_ukv'], 'w_attn_out': out['w_attn_out'], 'w_out': out['w_out'], 'final_norm_w': out['final_norm_w'], 'loss_target': out['loss_target'], 'm_w_ada': out['m_w_ada'], 'm_b_ada': out['m_b_ada'], 'm_norm_w': out['m_norm_w'], 'm_w_in': out['m_w_in'], 'm_conv_w': out['m_conv_w'], 'm_conv_b': out['m_conv_b'], 'm_conv_ln_w': out['m_conv_ln_w'], 'm_conv_ln_b': out['m_conv_ln_b'], 'm_w_conv_out': out['m_w_conv_out'], 'm_q_norm_w': out['m_q_norm_w'], 'm_w_uq': out['m_w_uq'], 'm_kv_norm_w': out['m_kv_norm_w'], 'm_w_ukv': out['m_w_ukv'], 'm_w_attn_out': out['m_w_attn_out'], 'm_w_out': out['m_w_out'], 'm_final_norm_w': out['m_final_norm_w'], 'v_w_ada': out['v_w_ada'], 'v_b_ada': out['v_b_ada'], 'v_norm_w': out['v_norm_w'], 'v_w_in': out['v_w_in'], 'v_conv_w': out['v_conv_w'], 'v_conv_b': out['v_conv_b'], 'v_conv_ln_w': out['v_conv_ln_w'], 'v_conv_ln_b': out['v_conv_ln_b'], 'v_w_conv_out': out['v_w_conv_out'], 'v_q_norm_w': out['v_q_norm_w'], 'v_w_uq': out['v_w_uq'], 'v_kv_norm_w': out['v_kv_norm_w'], 'v_w_ukv': out['v_w_ukv'], 'v_w_attn_out': out['v_w_attn_out'], 'v_w_out': out['v_w_out'], 'v_final_norm_w': out['v_final_norm_w']}


def _loss(weights, diff, rest, loss_target):
    with _jax.named_scope("forward"):
        args = {**rest, TWIN_DIFF_INPUT: diff, **{k: w.astype(_WEIGHT_DTYPES[k]) for k, w in weights.items()}}
        y = _forward(args)
    with _jax.named_scope("loss_head"):
        err = _jnp.square(y.astype(_jnp.float32) - loss_target)
        return 0.5 * _jnp.sum(_jnp.mean(err, axis=-1)) if err.ndim else 0.5 * err


def _adamw(w, g, m, v):
    m = ADAM_B1 * m + (1.0 - ADAM_B1) * g
    v = ADAM_B2 * v + (1.0 - ADAM_B2) * _jnp.square(g)
    m_hat = m / (1.0 - ADAM_B1 ** ADAM_STEP)
    v_hat = v / (1.0 - ADAM_B2 ** ADAM_STEP)
    delta = -ADAM_LR * (m_hat / (_jnp.sqrt(v_hat) + ADAM_EPS) + ADAM_WD * w)
    return delta, m, v


def reference(x, c, positions, w_ada, b_ada, norm_w, w_in, conv_w, conv_b, conv_ln_w, conv_ln_b, w_conv_out, q_norm_w, w_uq, kv_norm_w, w_ukv, w_attn_out, w_out, final_norm_w, loss_target, m_w_ada, m_b_ada, m_norm_w, m_w_in, m_conv_w, m_conv_b, m_conv_ln_w, m_conv_ln_b, m_w_conv_out, m_q_norm_w, m_w_uq, m_kv_norm_w, m_w_ukv, m_w_attn_out, m_w_out, m_final_norm_w, v_w_ada, v_b_ada, v_norm_w, v_w_in, v_conv_w, v_conv_b, v_conv_ln_w, v_conv_ln_b, v_w_conv_out, v_q_norm_w, v_w_uq, v_kv_norm_w, v_w_ukv, v_w_attn_out, v_w_out, v_final_norm_w):
    given = dict(x=x, c=c, positions=positions, w_ada=w_ada, b_ada=b_ada, norm_w=norm_w, w_in=w_in, conv_w=conv_w, conv_b=conv_b, conv_ln_w=conv_ln_w, conv_ln_b=conv_ln_b, w_conv_out=w_conv_out, q_norm_w=q_norm_w, w_uq=w_uq, kv_norm_w=kv_norm_w, w_ukv=w_ukv, w_attn_out=w_attn_out, w_out=w_out, final_norm_w=final_norm_w, loss_target=loss_target, m_w_ada=m_w_ada, m_b_ada=m_b_ada, m_norm_w=m_norm_w, m_w_in=m_w_in, m_conv_w=m_conv_w, m_conv_b=m_conv_b, m_conv_ln_w=m_conv_ln_w, m_conv_ln_b=m_conv_ln_b, m_w_conv_out=m_w_conv_out, m_q_norm_w=m_q_norm_w, m_w_uq=m_w_uq, m_kv_norm_w=m_kv_norm_w, m_w_ukv=m_w_ukv, m_w_attn_out=m_w_attn_out, m_w_out=m_w_out, m_final_norm_w=m_final_norm_w, v_w_ada=v_w_ada, v_b_ada=v_b_ada, v_norm_w=v_norm_w, v_w_in=v_w_in, v_conv_w=v_conv_w, v_conv_b=v_conv_b, v_conv_ln_w=v_conv_ln_w, v_conv_ln_b=v_conv_ln_b, v_w_conv_out=v_w_conv_out, v_q_norm_w=v_q_norm_w, v_w_uq=v_w_uq, v_kv_norm_w=v_kv_norm_w, v_w_ukv=v_w_ukv, v_w_attn_out=v_w_attn_out, v_w_out=v_w_out, v_final_norm_w=v_final_norm_w)
    weights = {n: given[n] for n in TWIN_WEIGHTS}
    shared = {n: given[n] for n in SHARED_INPUTS}
    per_example = {n: given[n] for n in ['x', 'c', 'positions']}
    grad_fn = _jax.value_and_grad(_loss, argnums=(0, 1))

    def one_microbatch(ex, loss_target):
        ex = dict(ex)
        diff = ex.pop(TWIN_DIFF_INPUT)
        return grad_fn(weights, diff, {**shared, **ex}, loss_target)

    if N_MICROBATCH == 1:
        loss, (grad_w, grad_x) = one_microbatch(per_example, given["loss_target"])
    else:
        def body(carry, xs):
            loss_sum, grad_sum = carry
            l_k, (gw_k, gx_k) = one_microbatch(xs[0], xs[1])
            with _jax.named_scope("update"):
                return (loss_sum + l_k, _jax.tree.map(_jnp.add, grad_sum, gw_k)), gx_k

        init = (_jnp.zeros((), _jnp.float32), _jax.tree.map(_jnp.zeros_like, weights))
        (loss, grad_w), grad_x = _jax.lax.scan(body, init, (per_example, given["loss_target"]))
    with _jax.named_scope("update"):
        delta_w, new_m, new_v = {}, {}, {}
        for n in TWIN_WEIGHTS:
            delta_w[n], new_m[n], new_v[n] = _adamw(weights[n], grad_w[n], given["m_" + n], given["v_" + n])
    return (loss, grad_x, *[grad_w[n] for n in TWIN_WEIGHTS], *[delta_w[n] for n in TWIN_WEIGHTS],
            *[new_m[n] for n in TWIN_WEIGHTS], *[new_v[n] for n in TWIN_WEIGHTS])
```

```python
import functools

import numpy as np
import jax
import jax.numpy as jnp
from jax import lax
from jax.experimental import pallas as pl
from jax.experimental.pallas import tpu as pltpu

F32 = jnp.float32
BF16 = jnp.bfloat16
MESH = pl.DeviceIdType.MESH

D = 1024
H = 8
DN = 128
DR = 64
RQ = 256
KC = 31
HALO = 32
EPS = 1e-6
ROPE_THETA = 10000.0
N_CHIP = 4
N_DEV = 8
LANE = 128
VMEM_BIG = 56 * 1024 * 1024

ADAM_LR = 0.001
ADAM_B1 = 0.9
ADAM_B2 = 0.999
ADAM_EPS = 1e-08
ADAM_WD = 0.01
ADAM_STEP = 10

A_COLS = 3 * D
L_COLS_RAW = RQ + RQ + DR
L_COLS = 640
G_COLS = 3 * D
IN_COLS = A_COLS + L_COLS_RAW + G_COLS

SHARD_SHAPES = ((D, IN_COLS // N_CHIP), (RQ, 384), (RQ, 512), (D // N_CHIP, D), (D // N_CHIP, D), (D // N_CHIP, D))
SHARD_SIZES = tuple(a * b for a, b in SHARD_SHAPES)
PACK_ROWS = sum(SHARD_SIZES) // LANE
HALF_ROWS = PACK_ROWS // 2


def _params(sem=None, vmem=None):
    kw = {}
    if sem is not None:
        kw["dimension_semantics"] = sem
    if vmem is not None:
        kw["vmem_limit_bytes"] = vmem
    return pltpu.CompilerParams(**kw)


def _dot(a, b):
    return jnp.dot(a, b, preferred_element_type=F32)


def _dot_nt(a, b):
    return lax.dot_general(a, b, (((1,), (1,)), ((), ())), preferred_element_type=F32)


def _dot_tn(a, b):
    return lax.dot_general(a, b, (((0,), (0,)), ((), ())), preferred_element_type=F32)


def _colsum(v):
    return jnp.sum(v, axis=0, keepdims=True)


def _rowmean(v):
    return jnp.mean(v, axis=-1, keepdims=True)


def _sigmoid(v):
    return jax.nn.sigmoid(v)


def _dsilu(v, s):
    return s * (1.0 + v * (1.0 - s))


def _swap_halves(v, first_half):
    return jnp.where(first_half, pltpu.roll(v, 96, 1), pltpu.roll(v, 32, 1))


def _first_half_mask(rows):
    lane = lax.broadcasted_iota(jnp.int32, (rows, LANE), 1)
    return (lane % 64) < 32


def _adaln_norm(x, norm_w, shift, scale, ts):
    s = x.shape[0]

    def body(x_ref, nw_ref, sh_ref, sc_ref, h_ref):
        xv = x_ref[...]
        r = lax.rsqrt(_rowmean(xv * xv) + EPS)
        y = xv * r * nw_ref[...]
        h_ref[...] = (y * (1.0 + sc_ref[...]) + sh_ref[...]).astype(BF16)

    row = pl.BlockSpec((ts, D), lambda i: (i, 0))
    vec = pl.BlockSpec((1, D), lambda i: (0, 0))
    return pl.pallas_call(
        body, grid=(s // ts,), in_specs=[row, vec, vec, vec], out_specs=row,
        out_shape=jax.ShapeDtypeStruct((s, D), BF16), name="adaln_norm",
        compiler_params=_params(("parallel",)))(x, norm_w, shift, scale)


def _mm_nn(a, b, tm, tn, name):
    m, k = a.shape
    n = b.shape[1]

    def body(a_ref, b_ref, o_ref):
        o_ref[...] = _dot(a_ref[...], b_ref[...])

    return pl.pallas_call(
        body, grid=(n // tn, m // tm),
        in_specs=[pl.BlockSpec((tm, k), lambda j, i: (i, 0)), pl.BlockSpec((k, tn), lambda j, i: (0, j))],
        out_specs=pl.BlockSpec((tm, tn), lambda j, i: (i, j)),
        out_shape=jax.ShapeDtypeStruct((m, n), F32), name=name,
        compiler_params=_params(("parallel", "parallel")))(a, b)


def _mm_tn(a, b, tm, tn, name):
    m, k = a.shape
    n = b.shape[1]

    def body(a_ref, b_ref, o_ref):
        @pl.when(pl.program_id(1) == 0)
        def _():
            o_ref[...] = jnp.zeros_like(o_ref)
        o_ref[...] += _dot_tn(a_ref[...], b_ref[...])

    return pl.pallas_call(
        body, grid=(n // tn, m // tm),
        in_specs=[pl.BlockSpec((tm, k), lambda j, i: (i, 0)), pl.BlockSpec((tm, tn), lambda j, i: (i, j))],
        out_specs=pl.BlockSpec((k, tn), lambda j, i: (0, j)),
        out_shape=jax.ShapeDtypeStruct((k, n), F32), name=name,
        compiler_params=_params(("parallel", "arbitrary")))(a, b)


def _conv_taps(win_ref, w_ref, rows, chunk, offset_of_tap):
    pieces = []
    for c0 in range(0, rows, chunk):
        acc = None
        for j in range(KC):
            term = w_ref[j:j + 1, :] * win_ref[pl.ds(c0 + offset_of_tap(j), chunk), :]
            acc = term if acc is None else acc + term
        pieces.append(acc)
    return pieces


def _conv_fwd(proj_a, conv_w, conv_b, ln_w, ln_b, ts, chunk):
    s = proj_a.shape[0]

    def body(av_ref, al_ref, ag_ref, w_ref, b_ref, lw_ref, lb_ref, u0_ref, u1_ref, za_ref, win_ref):
        @pl.when(pl.program_id(0) == 0)
        def _():
            win_ref[0:HALO, :] = jnp.zeros((HALO, D), F32)

        u0 = av_ref[...] * _sigmoid(al_ref[...])
        u0_ref[...] = u0
        win_ref[HALO:HALO + ts, :] = u0
        pieces = _conv_taps(win_ref, w_ref, ts, chunk, lambda j: HALO - (KC - 1) + j)
        for n, acc in enumerate(pieces):
            u1_ref[n * chunk:(n + 1) * chunk, :] = acc + b_ref[...]
        win_ref[0:HALO, :] = win_ref[ts:ts + HALO, :]

        u1 = u1_ref[...]
        xc = u1 - _rowmean(u1)
        rstd = lax.rsqrt(_rowmean(xc * xc) + EPS)
        u2 = xc * rstd * lw_ref[...] + lb_ref[...]
        u3 = u2 * _sigmoid(u2)
        ag = ag_ref[...]
        za_ref[...] = (u3 * (ag * _sigmoid(ag))).astype(BF16)

    col = lambda c: pl.BlockSpec((ts, D), lambda i, c=c: (i, c))
    row = pl.BlockSpec((ts, D), lambda i: (i, 0))
    vec = pl.BlockSpec((1, D), lambda i: (0, 0))
    return pl.pallas_call(
        body, grid=(s // ts,),
        in_specs=[col(0), col(1), col(2), pl.BlockSpec((HALO, D), lambda i: (0, 0)), vec, vec, vec],
        out_specs=[row, row, row],
        out_shape=[jax.ShapeDtypeStruct((s, D), F32), jax.ShapeDtypeStruct((s, D), F32),
                   jax.ShapeDtypeStruct((s, D), BF16)],
        scratch_shapes=[pltpu.VMEM((ts + HALO, D), F32)], name="conv_fwd",
        compiler_params=_params(("arbitrary",)))(proj_a, proj_a, proj_a, conv_w, conv_b, ln_w, ln_b)


def _conv_bwd(dza, proj_a, u0, u1, conv_w, ln_w, ln_b, ts, chunk):
    s = dza.shape[0]
    nt = s // ts
    per = ts // HALO

    def body(dza_ref, av_ref, al_ref, ag_ref, u0_ref, u0p_ref, u1_ref, w_ref, lw_ref, lb_ref,
             dpa_ref, gw_ref, gv_ref, dwin_ref, uwin_ref, du0_ref, gwp_ref):
        step = pl.program_id(0)
        tile = nt - 1 - step

        @pl.when(step == 0)
        def _():
            dwin_ref[ts:ts + HALO, :] = jnp.zeros((HALO, D), F32)
            gwp_ref[...] = jnp.zeros_like(gwp_ref)
            gv_ref[...] = jnp.zeros_like(gv_ref)

        ag = ag_ref[...]
        sg = _sigmoid(ag)
        u1 = u1_ref[...]
        xc = u1 - _rowmean(u1)
        rstd = lax.rsqrt(_rowmean(xc * xc) + EPS)
        xh = xc * rstd
        u2 = xh * lw_ref[...] + lb_ref[...]
        s2 = _sigmoid(u2)
        dz = dza_ref[...]
        du3 = dz * (ag * sg)
        dpa_ref[:, 2 * D:3 * D] = (dz * (u2 * s2) * _dsilu(ag, sg)).astype(BF16)
        du2 = du3 * _dsilu(u2, s2)
        gv_ref[0:1, :] += _colsum(du2 * xh)
        gv_ref[1:2, :] += _colsum(du2)
        dxh = du2 * lw_ref[...]
        du1 = rstd * (dxh - _rowmean(dxh) - xh * _rowmean(dxh * xh))
        gv_ref[2:3, :] += _colsum(du1)
        dwin_ref[0:ts, :] = du1

        uwin_ref[0:HALO, :] = jnp.where(tile == 0, 0.0, u0p_ref[...])
        uwin_ref[HALO:HALO + ts, :] = u0_ref[...]

        pieces = _conv_taps(dwin_ref, w_ref, ts, chunk, lambda j: (KC - 1) - j)
        for n, acc in enumerate(pieces):
            du0_ref[n * chunk:(n + 1) * chunk, :] = acc
        for c0 in range(0, ts, chunk):
            dchunk = dwin_ref[c0:c0 + chunk, :]
            for j in range(KC):
                prod = dchunk * uwin_ref[pl.ds(c0 + HALO - (KC - 1) + j, chunk), :]
                gwp_ref[8 * j:8 * j + 8, :] += jnp.sum(prod.reshape(chunk // 8, 8, D), axis=0)
        dwin_ref[ts:ts + HALO, :] = dwin_ref[0:HALO, :]

        du0 = du0_ref[...]
        al = al_ref[...]
        sl = _sigmoid(al)
        dpa_ref[:, 0:D] = (du0 * sl).astype(BF16)
        dpa_ref[:, D:2 * D] = (du0 * av_ref[...] * sl * (1.0 - sl)).astype(BF16)

        @pl.when(step == nt - 1)
        def _():
            for j in range(KC):
                gw_ref[j:j + 1, :] = _colsum(gwp_ref[8 * j:8 * j + 8, :])
            gw_ref[KC:HALO, :] = jnp.zeros((HALO - KC, D), F32)

    rev = lambda i: nt - 1 - i
    col = lambda c: pl.BlockSpec((ts, D), lambda i, c=c: (rev(i), c))
    row = pl.BlockSpec((ts, D), lambda i: (rev(i), 0))
    vec = pl.BlockSpec((1, D), lambda i: (0, 0))
    halo = pl.BlockSpec((HALO, D), lambda i: (jnp.maximum(rev(i) * per - 1, 0), 0))
    return pl.pallas_call(
        body, grid=(nt,),
        in_specs=[row, col(0), col(1), col(2), row, halo, row, pl.BlockSpec((HALO, D), lambda i: (0, 0)), vec, vec],
        out_specs=[pl.BlockSpec((ts, A_COLS), lambda i: (rev(i), 0)),
                   pl.BlockSpec((HALO, D), lambda i: (0, 0)), pl.BlockSpec((8, D), lambda i: (0, 0))],
        out_shape=[jax.ShapeDtypeStruct((s, A_COLS), BF16), jax.ShapeDtypeStruct((HALO, D), F32),
                   jax.ShapeDtypeStruct((8, D), F32)],
        scratch_shapes=[pltpu.VMEM((ts + HALO, D), F32), pltpu.VMEM((ts + HALO, D), F32),
                        pltpu.VMEM((ts, D), F32), pltpu.VMEM((8 * HALO, D), F32)],
        name="conv_bwd", compiler_params=_params(("arbitrary",), VMEM_BIG))(
            dza, proj_a, proj_a, proj_a, u0, u0, u1, conv_w, ln_w, ln_b)


def _mla_prep(proj_l, q_norm_w, kv_norm_w, w_uq2, w_ukv, cos_t, sin_t, ts):
    s = proj_l.shape[0]

    def body(pl_ref, qw_ref, kw_ref, wq_ref, wkv_ref, c_ref, s_ref, qn_ref, kvn_ref, q_ref, k_ref, v_ref):
        first = _first_half_mask(ts)
        cs = c_ref[...]
        sn = s_ref[...]

        def rms(v, w):
            return v * lax.rsqrt(_rowmean(v * v) + EPS) * w

        def rope(v):
            return v * cs + _swap_halves(v, first) * sn

        qn = rms(pl_ref[:, 0:RQ], qw_ref[...]).astype(BF16)
        kvn = rms(pl_ref[:, RQ:2 * RQ], kw_ref[...]).astype(BF16)
        qn_ref[...] = qn
        kvn_ref[...] = kvn
        q = _dot(qn, wq_ref[...])
        kv = _dot(kvn, wkv_ref[...])
        kr = rope(pl_ref[:, 2 * RQ:2 * RQ + LANE]).astype(BF16)
        for h in range(H):
            q_ref[h, :, 0:DN] = q[:, DN * h:DN * (h + 1)].astype(BF16)
            q_ref[h, :, DN:2 * DN] = rope(q[:, H * DN + LANE * h:H * DN + LANE * (h + 1)]).astype(BF16)
            k_ref[h, :, 0:DN] = kv[:, 2 * DN * h:2 * DN * h + DN].astype(BF16)
            k_ref[h, :, DN:2 * DN] = kr
            v_ref[h] = kv[:, 2 * DN * h + DN:2 * DN * (h + 1)].astype(BF16)

    const = lambda shape: pl.BlockSpec(shape, lambda i: (0,) * len(shape))
    rowb = lambda w: pl.BlockSpec((ts, w), lambda i: (i, 0))
    head = lambda w: pl.BlockSpec((H, ts, w), lambda i: (0, i, 0))
    return pl.pallas_call(
        body, grid=(s // ts,),
        in_specs=[rowb(L_COLS), const((1, RQ)), const((1, RQ)), const((RQ, 2 * H * DN)), const((RQ, 2 * H * DN)),
                  rowb(LANE), rowb(LANE)],
        out_specs=[rowb(RQ), rowb(RQ), head(2 * DN), head(2 * DN), head(DN)],
        out_shape=[jax.ShapeDtypeStruct((s, RQ), BF16), jax.ShapeDtypeStruct((s, RQ), BF16),
                   jax.ShapeDtypeStruct((H, s, 2 * DN), BF16), jax.ShapeDtypeStruct((H, s, 2 * DN), BF16),
                   jax.ShapeDtypeStruct((H, s, DN), BF16)],
        name="mla_prep", compiler_params=_params(("parallel",)))(
            proj_l, q_norm_w, kv_norm_w, w_uq2, w_ukv, cos_t, sin_t)


def _mla_prep_bwd(dq, dk, dv, proj_l, qn, kvn, q_norm_w, kv_norm_w, w_uq2, w_ukv, cos_t, sin_t, ts):
    s = proj_l.shape[0]

    def body(dq_ref, dk_ref, dv_ref, pl_ref, qn_ref, kvn_ref, qw_ref, kw_ref, wq_ref, wkv_ref, c_ref, s_ref,
             dpl_ref, gwq_ref, gwkv_ref, gv_ref, dq2_ref, dkv2_ref):
        @pl.when(pl.program_id(0) == 0)
        def _():
            gwq_ref[...] = jnp.zeros_like(gwq_ref)
            gwkv_ref[...] = jnp.zeros_like(gwkv_ref)
            gv_ref[...] = jnp.zeros_like(gv_ref)

        first = _first_half_mask(ts)
        cs = c_ref[...]
        sn = s_ref[...]

        def rope_bwd(g):
            return g * cs + _swap_halves(g * sn, first)

        def rms_bwd(v, w, dy):
            r = lax.rsqrt(_rowmean(v * v) + EPS)
            vh = v * r
            dvh = dy * w
            return r * (dvh - vh * _rowmean(dvh * vh)), _colsum(dy * vh)

        dkr = None
        for h in range(H):
            dq2_ref[:, DN * h:DN * (h + 1)] = dq_ref[h, :, 0:DN].astype(BF16)
            dq2_ref[:, H * DN + LANE * h:H * DN + LANE * (h + 1)] = rope_bwd(dq_ref[h, :, DN:2 * DN]).astype(BF16)
            dkv2_ref[:, 2 * DN * h:2 * DN * h + DN] = dk_ref[h, :, 0:DN].astype(BF16)
            dkv2_ref[:, 2 * DN * h + DN:2 * DN * (h + 1)] = dv_ref[h].astype(BF16)
            part = dk_ref[h, :, DN:2 * DN]
            dkr = part if dkr is None else dkr + part

        dq2 = dq2_ref[...]
        dkv2 = dkv2_ref[...]
        gwq_ref[...] += _dot_tn(qn_ref[...], dq2)
        gwkv_ref[...] += _dot_tn(kvn_ref[...], dkv2)
        dcq, gq = rms_bwd(pl_ref[:, 0:RQ], qw_ref[...], _dot_nt(dq2, wq_ref[...]))
        dckv, gkv = rms_bwd(pl_ref[:, RQ:2 * RQ], kw_ref[...], _dot_nt(dkv2, wkv_ref[...]))
        gv_ref[0:1, :] += gq
        gv_ref[1:2, :] += gkv
        dpl_ref[:, 0:RQ] = dcq.astype(BF16)
        dpl_ref[:, RQ:2 * RQ] = dckv.astype(BF16)
        dpl_ref[:, 2 * RQ:2 * RQ + LANE] = rope_bwd(dkr).astype(BF16)

    const = lambda shape: pl.BlockSpec(shape, lambda i: (0,) * len(shape))
    rowb = lambda w: pl.BlockSpec((ts, w), lambda i: (i, 0))
    head = lambda w: pl.BlockSpec((H, ts, w), lambda i: (0, i, 0))
    return pl.pallas_call(
        body, grid=(s // ts,),
        in_specs=[head(2 * DN), head(2 * DN), head(DN), rowb(L_COLS), rowb(RQ), rowb(RQ), const((1, RQ)),
                  const((1, RQ)), const((RQ, 2 * H * DN)), const((RQ, 2 * H * DN)), rowb(LANE), rowb(LANE)],
        out_specs=[rowb(L_COLS), const((RQ, 2 * H * DN)), const((RQ, 2 * H * DN)), const((8, RQ))],
        out_shape=[jax.ShapeDtypeStruct((s, L_COLS), BF16), jax.ShapeDtypeStruct((RQ, 2 * H * DN), F32),
                   jax.ShapeDtypeStruct((RQ, 2 * H * DN), F32), jax.ShapeDtypeStruct((8, RQ), F32)],
        scratch_shapes=[pltpu.VMEM((ts, 2 * H * DN), BF16), pltpu.VMEM((ts, 2 * H * DN), BF16)],
        name="mla_prep_bwd", compiler_params=_params(("arbitrary",), VMEM_BIG))(
            dq, dk, dv, proj_l, qn, kvn, q_norm_w, kv_norm_w, w_uq2, w_ukv, cos_t, sin_t)


def _causal_pairs(n, by_key):
    if by_key:
        pairs = [(i, j) for j in range(n) for i in range(j, n)]
    else:
        pairs = [(i, j) for i in range(n) for j in range(i + 1)]
    return (jnp.asarray(np.array([p[0] for p in pairs], np.int32)),
            jnp.asarray(np.array([p[1] for p in pairs], np.int32)))


def _causal_mask(i, j, t):
    rows = i * t + lax.broadcasted_iota(jnp.int32, (t, t), 0)
    cols = j * t + lax.broadcasted_iota(jnp.int32, (t, t), 1)
    return cols <= rows


def _attn_fwd(q, k, v, t):
    s = q.shape[1]
    n = s // t
    scale = float((DN + DR) ** -0.5)
    qi, ki = _causal_pairs(n, by_key=False)

    def body(qi_ref, ki_ref, q_ref, k_ref, v_ref, o_ref, lse_ref, m_sc, l_sc, acc_sc):
        p = pl.program_id(1)
        i = qi_ref[p]
        j = ki_ref[p]

        @pl.when(j == 0)
        def _():
            m_sc[...] = jnp.full_like(m_sc, -jnp.inf)
            l_sc[...] = jnp.zeros_like(l_sc)
            acc_sc[...] = jnp.zeros_like(acc_sc)

        sc = _dot_nt(q_ref[0], k_ref[0]) * scale
        sc = jnp.where(_causal_mask(i, j, t), sc, -jnp.inf)
        m_prev = m_sc[...]
        m_new = jnp.maximum(m_prev, jnp.max(sc, axis=-1, keepdims=True))
        alpha = jnp.exp(m_prev - m_new)
        pe = jnp.exp(sc - m_new)
        l_sc[...] = alpha * l_sc[...] + jnp.sum(pe, axis=-1, keepdims=True)
        acc_sc[...] = alpha * acc_sc[...] + _dot(pe.astype(BF16), v_ref[0])
        m_sc[...] = m_new

        @pl.when(j == i)
        def _():
            o_ref[...] = acc_sc[...] / l_sc[...]
            lse_ref[0] = m_sc[...] + jnp.log(l_sc[...])

    grid_spec = pltpu.PrefetchScalarGridSpec(
        num_scalar_prefetch=2, grid=(H, int(qi.shape[0])),
        in_specs=[pl.BlockSpec((1, t, 2 * DN), lambda h, p, qi, ki: (h, qi[p], 0)),
                  pl.BlockSpec((1, t, 2 * DN), lambda h, p, qi, ki: (h, ki[p], 0)),
                  pl.BlockSpec((1, t, DN), lambda h, p, qi, ki: (h, ki[p], 0))],
        out_specs=[pl.BlockSpec((t, DN), lambda h, p, qi, ki: (qi[p], h)),
                   pl.BlockSpec((1, t, 1), lambda h, p, qi, ki: (h, qi[p], 0))],
        scratch_shapes=[pltpu.VMEM((t, 1), F32), pltpu.VMEM((t, 1), F32), pltpu.VMEM((t, DN), F32)])
    return pl.pallas_call(
        body, grid_spec=grid_spec,
        out_shape=[jax.ShapeDtypeStruct((s, H * DN), F32), jax.ShapeDtypeStruct((H, s, 1), F32)],
        name="attn_fwd", compiler_params=_params(("parallel", "arbitrary")))(qi, ki, q, k, v)


def _attn_bwd(q, k, v, do, lse, delta, t):
    s = q.shape[1]
    n = s // t
    scale = float((DN + DR) ** -0.5)
    qi, ki = _causal_pairs(n, by_key=True)

    def body(qi_ref, ki_ref, q_ref, k_ref, v_ref, do_ref, lse_ref, dl_ref, dq_ref, dk_ref, dv_ref, dk_sc, dv_sc):
        p = pl.program_id(1)
        i = qi_ref[p]
        j = ki_ref[p]

        @pl.when(p == 0)
        def _():
            dq_ref[...] = jnp.zeros_like(dq_ref)

        @pl.when(i == j)
        def _():
            dk_sc[...] = jnp.zeros_like(dk_sc)
            dv_sc[...] = jnp.zeros_like(dv_sc)

        qv = q_ref[0]
        kv = k_ref[0]
        dov = do_ref[...]
        sc = _dot_nt(qv, kv) * scale
        sc = jnp.where(_causal_mask(i, j, t), sc, -jnp.inf)
        pr = jnp.exp(sc - lse_ref[0])
        dp = _dot_nt(dov, v_ref[0])
        ds = (pr * (dp - dl_ref[0]) * scale).astype(BF16)
        dv_sc[...] += _dot_tn(pr.astype(BF16), dov)
        dk_sc[...] += _dot_tn(ds, qv)
        rows = pl.ds(pl.multiple_of(i * t, t), t)
        dq_ref[0, rows, :] += _dot(ds, kv)

        @pl.when(i == n - 1)
        def _():
            dk_ref[0] = dk_sc[...]
            dv_ref[0] = dv_sc[...]

    grid_spec = pltpu.PrefetchScalarGridSpec(
        num_scalar_prefetch=2, grid=(H, int(qi.shape[0])),
        in_specs=[pl.BlockSpec((1, t, 2 * DN), lambda h, p, qi, ki: (h, qi[p], 0)),
                  pl.BlockSpec((1, t, 2 * DN), lambda h, p, qi, ki: (h, ki[p], 0)),
                  pl.BlockSpec((1, t, DN), lambda h, p, qi, ki: (h, ki[p], 0)),
                  pl.BlockSpec((t, DN), lambda h, p, qi, ki: (qi[p], h)),
                  pl.BlockSpec((1, t, 1), lambda h, p, qi, ki: (h, qi[p], 0)),
                  pl.BlockSpec((1, t, 1), lambda h, p, qi, ki: (h, qi[p], 0))],
        out_specs=[pl.BlockSpec((1, s, 2 * DN), lambda h, p, qi, ki: (h, 0, 0)),
                   pl.BlockSpec((1, t, 2 * DN), lambda h, p, qi, ki: (h, ki[p], 0)),
                   pl.BlockSpec((1, t, DN), lambda h, p, qi, ki: (h, ki[p], 0))],
        scratch_shapes=[pltpu.VMEM((t, 2 * DN), F32), pltpu.VMEM((t, DN), F32)])
    return pl.pallas_call(
        body, grid_spec=grid_spec,
        out_shape=[jax.ShapeDtypeStruct((H, s, 2 * DN), F32), jax.ShapeDtypeStruct((H, s, 2 * DN), F32),
                   jax.ShapeDtypeStruct((H, s, DN), F32)],
        name="attn_bwd", compiler_params=_params(("parallel", "arbitrary"), VMEM_BIG))(
            qi, ki, q, k, v, do, lse, delta)


def _middle(za, o, proj_g, x, tgt, gate, fnw, wco, wao, wo, ts):
    s = x.shape[0]
    inv_d = 1.0 / D

    def body(za_ref, o_ref, bg_ref, ga_ref, gb_ref, x_ref, t_ref, gate_ref, fnw_ref, wco_ref, wao_ref, wo_ref,
             dx2_ref, dza_ref, do_ref, dl_ref, dpg_ref, zb_ref, mg_ref, dmo_ref, dya_ref, dyb_ref, vec_ref):
        @pl.when(pl.program_id(0) == 0)
        def _():
            vec_ref[...] = jnp.zeros_like(vec_ref)

        ov = o_ref[...]
        bg = bg_ref[...]
        sb = _sigmoid(bg)
        silu_b = bg * sb
        zb = (ov * silu_b).astype(BF16)
        zb_ref[...] = zb
        ya = _dot(za_ref[...], wco_ref[...])
        yb = _dot(zb, wao_ref[...])
        sa = _sigmoid(ga_ref[...])
        sg = _sigmoid(gb_ref[...])
        mg = (sa * ya + sg * yb).astype(BF16)
        mg_ref[...] = mg
        mo = _dot(mg, wo_ref[...])
        gate_v = gate_ref[...]
        x2 = x_ref[...] + gate_v * mo
        r = lax.rsqrt(_rowmean(x2 * x2) + EPS)
        xh = x2 * r
        fw = fnw_ref[...]
        e = xh * fw - t_ref[...]
        vec_ref[2:3, :] += _colsum(e * e)
        dy = e * inv_d
        vec_ref[0:1, :] += _colsum(dy * xh)
        dxh = dy * fw
        dx2 = r * (dxh - xh * _rowmean(dxh * xh))
        dx2_ref[...] = dx2
        vec_ref[1:2, :] += _colsum(dx2 * mo)
        dmo = (gate_v * dx2).astype(BF16)
        dmo_ref[...] = dmo
        dmg = _dot_nt(dmo, wo_ref[...])
        dya = (sa * dmg).astype(BF16)
        dyb = (sg * dmg).astype(BF16)
        dya_ref[...] = dya
        dyb_ref[...] = dyb
        dpg_ref[:, D:2 * D] = (dmg * ya * (sa * (1.0 - sa))).astype(BF16)
        dpg_ref[:, 2 * D:3 * D] = (dmg * yb * (sg * (1.0 - sg))).astype(BF16)
        dza_ref[...] = _dot_nt(dya, wco_ref[...])
        dzb = _dot_nt(dyb, wao_ref[...])
        dov = dzb * silu_b
        do_ref[...] = dov.astype(BF16)
        dpg_ref[:, 0:D] = (dzb * ov * _dsilu(bg, sb)).astype(BF16)
        dprod = dov * ov
        for h in range(H):
            dl_ref[h] = jnp.sum(dprod[:, DN * h:DN * (h + 1)], axis=-1, keepdims=True)

    col = lambda c: pl.BlockSpec((ts, D), lambda i, c=c: (i, c))
    row = pl.BlockSpec((ts, D), lambda i: (i, 0))
    vec = pl.BlockSpec((1, D), lambda i: (0, 0))
    wsp = pl.BlockSpec((D, D), lambda i: (0, 0))
    bf = jax.ShapeDtypeStruct((s, D), BF16)
    ff = jax.ShapeDtypeStruct((s, D), F32)
    return pl.pallas_call(
        body, grid=(s // ts,),
        in_specs=[row, row, col(0), col(1), col(2), row, row, vec, vec, wsp, wsp, wsp],
        out_specs=[row, row, row, pl.BlockSpec((H, ts, 1), lambda i: (0, i, 0)),
                   pl.BlockSpec((ts, G_COLS), lambda i: (i, 0)), row, row, row, row, row,
                   pl.BlockSpec((8, D), lambda i: (0, 0))],
        out_shape=[ff, ff, bf, jax.ShapeDtypeStruct((H, s, 1), F32), jax.ShapeDtypeStruct((s, G_COLS), BF16),
                   bf, bf, bf, bf, bf, jax.ShapeDtypeStruct((8, D), F32)],
        name="middle", compiler_params=_params(("arbitrary",), VMEM_BIG))(
            za, o, proj_g, proj_g, proj_g, x, tgt, gate, fnw, wco, wao, wo)


def _input_bwd(dpa, dpl, dpg, wa, wl, wg, x, dx2, norm_w, scale, ts):
    s = x.shape[0]

    def body(dpa_ref, dpl_ref, dpg_ref, wa_ref, wl_ref, wg_ref, x_ref, dx2_ref, nw_ref, sc_ref, gx_ref, gv_ref):
        @pl.when(pl.program_id(0) == 0)
        def _():
            gv_ref[...] = jnp.zeros_like(gv_ref)

        dh = (_dot_nt(dpa_ref[...], wa_ref[...]) + _dot_nt(dpl_ref[...], wl_ref[...])
              + _dot_nt(dpg_ref[...], wg_ref[...]))
        xv = x_ref[...]
        r = lax.rsqrt(_rowmean(xv * xv) + EPS)
        xh = xv * r
        nw = nw_ref[...]
        gv_ref[0:1, :] += _colsum(dh)
        gv_ref[1:2, :] += _colsum(dh * (xh * nw))
        dy = dh * (1.0 + sc_ref[...])
        gv_ref[2:3, :] += _colsum(dy * xh)
        dxh = dy * nw
        gx_ref[...] = dx2_ref[...] + r * (dxh - xh * _rowmean(dxh * xh))

    const = lambda shape: pl.BlockSpec(shape, lambda i: (0, 0))
    rowb = lambda w: pl.BlockSpec((ts, w), lambda i: (i, 0))
    return pl.pallas_call(
        body, grid=(s // ts,),
        in_specs=[rowb(A_COLS), rowb(L_COLS), rowb(G_COLS), const((D, A_COLS)), const((D, L_COLS)),
                  const((D, G_COLS)), rowb(D), rowb(D), const((1, D)), const((1, D))],
        out_specs=[rowb(D), const((8, D))],
        out_shape=[jax.ShapeDtypeStruct((s, D), F32), jax.ShapeDtypeStruct((8, D), F32)],
        name="input_bwd", compiler_params=_params(("arbitrary",), VMEM_BIG))(
            dpa, dpl, dpg, wa, wl, wg, x, dx2, norm_w, scale)


def _adamw(w, g, m, v, tr, name):
    rows, cols = w.shape
    c1 = 1.0 - ADAM_B1 ** ADAM_STEP
    c2 = 1.0 - ADAM_B2 ** ADAM_STEP

    def body(w_ref, g_ref, m_ref, v_ref, d_ref, nm_ref, nv_ref):
        gv = g_ref[...]
        nm = ADAM_B1 * m_ref[...] + (1.0 - ADAM_B1) * gv
        nv = ADAM_B2 * v_ref[...] + (1.0 - ADAM_B2) * (gv * gv)
        nm_ref[...] = nm
        nv_ref[...] = nv
        d_ref[...] = -ADAM_LR * ((nm / c1) / (jnp.sqrt(nv / c2) + ADAM_EPS) + ADAM_WD * w_ref[...])

    blk = pl.BlockSpec((tr, cols), lambda i: (i, 0))
    shp = jax.ShapeDtypeStruct((rows, cols), F32)
    return pl.pallas_call(
        body, grid=(rows // tr,), in_specs=[blk] * 4, out_specs=[blk] * 3, out_shape=[shp] * 3, name=name,
        compiler_params=_params(("parallel",)))(w, g, m, v)


def _ada_fwd(c_all, w_ada_shard, b_ada_shard):
    def body(c_ref, w_ref, b_ref, o_ref):
        cv = c_ref[...]
        o_ref[...] = jnp.dot(cv * _sigmoid(cv), w_ref[...], preferred_element_type=F32,
                             precision=lax.Precision.HIGHEST) + b_ref[...]

    return pl.pallas_call(
        body, out_shape=jax.ShapeDtypeStruct((N_DEV, w_ada_shard.shape[1]), F32), name="ada_fwd")(
            c_all, w_ada_shard, b_ada_shard)


def _ada_bwd(c_all_t, dmod_shard):
    def body(c_ref, d_ref, o_ref):
        cv = c_ref[...]
        o_ref[...] = jnp.dot(cv * _sigmoid(cv), d_ref[...], preferred_element_type=F32,
                             precision=lax.Precision.HIGHEST)

    return pl.pallas_call(
        body, out_shape=jax.ShapeDtypeStruct((D, dmod_shard.shape[1]), F32), name="ada_bwd")(c_all_t, dmod_shard)


def _sum_slabs(stack, tr, name):
    n, rows, cols = stack.shape

    def body(s_ref, o_ref):
        acc = s_ref[0]
        for k in range(1, n):
            acc = acc + s_ref[k]
        o_ref[...] = acc

    return pl.pallas_call(
        body, grid=(rows // tr,), in_specs=[pl.BlockSpec((n, tr, cols), lambda i: (0, i, 0))],
        out_specs=pl.BlockSpec((tr, cols), lambda i: (i, 0)), out_shape=jax.ShapeDtypeStruct((rows, cols), F32),
        name=name, compiler_params=_params(("parallel",)))(stack)


def _add_own_half(full, other, core, tr):
    n, rows, cols = other.shape
    per = rows // tr

    def body(c_ref, f_ref, o_ref, out_ref):
        out_ref[...] = f_ref[...] + o_ref[...]

    grid_spec = pltpu.PrefetchScalarGridSpec(
        num_scalar_prefetch=1, grid=(n, per),
        in_specs=[pl.BlockSpec((1, tr, cols), lambda k, i, c: (k, c[0] * per + i, 0)),
                  pl.BlockSpec((1, tr, cols), lambda k, i, c: (k, i, 0))],
        out_specs=pl.BlockSpec((1, tr, cols), lambda k, i, c: (k, i, 0)))
    return pl.pallas_call(
        body, grid_spec=grid_spec, out_shape=jax.ShapeDtypeStruct((n, rows, cols), F32), name="add_own_half",
        compiler_params=_params(("parallel", "parallel")))(core, full, other)


def _coords():
    return lax.axis_index("x"), lax.axis_index("y"), lax.axis_index("c")


def _allgather8(block, src_rows, vmem, name):
    n = block.shape[1]
    m = src_rows
    sliced = block.shape[0] != m

    def body(x_ref, out_ref, send_sems, recv_sems, local_sem):
        x, y, c = _coords()
        me, sibling = (x, y, c), (x, y, 1 - c)
        chips = [(1 - x, y), (x, 1 - y), (1 - x, 1 - y)]
        src = x_ref.at[pl.ds(pl.multiple_of(c * m, 16), m), :] if sliced else x_ref

        def rows(px, py, pc):
            return out_ref.at[pl.ds(pl.multiple_of((4 * px + 2 * py + pc) * m, 8), m), :]

        def copy(k, blk, to, source=None):
            return pltpu.make_async_remote_copy(
                src_ref=rows(*blk) if source is None else source, dst_ref=rows(*blk),
                send_sem=send_sems.at[k], recv_sem=recv_sems.at[k], device_id=to, device_id_type=MESH)

        mine = pltpu.make_async_copy(src, rows(*me), local_sem)
        mine.start()
        first = [copy(0, me, sibling, source=src)]
        first += [copy(1 + j, me, (*chip, c), source=src) for j, chip in enumerate(chips)]
        for cp in first:
            cp.start()
        passed = [copy(4 + j, (*chip, c), sibling) for j, chip in enumerate(chips)]
        for j, chip in enumerate(chips):
            copy(1 + j, (*chip, c), me).wait_recv()
            passed[j].start()
        copy(0, sibling, me).wait_recv()
        for j, chip in enumerate(chips):
            copy(4 + j, (*chip, 1 - c), me).wait_recv()
        for cp in first + passed:
            cp.wait_send()
        mine.wait()

    space = pltpu.VMEM if vmem else pl.ANY
    return pl.pallas_call(
        body, out_shape=jax.ShapeDtypeStruct((N_DEV * m, n), block.dtype),
        in_specs=[pl.BlockSpec(memory_space=space)], out_specs=pl.BlockSpec(memory_space=space),
        scratch_shapes=[pltpu.SemaphoreType.DMA((7,)), pltpu.SemaphoreType.DMA((7,)), pltpu.SemaphoreType.DMA],
        name=name)(block)


def _swap_halves_with_sibling(full):
    n, rows2, cols = full.shape
    rows = rows2 // 2

    def body(f_ref, got_ref, send_sems, recv_sems):
        x, y, c = _coords()
        copies = []
        for k in range(n):
            src = f_ref.at[k, pl.ds(pl.multiple_of((1 - c) * rows, 8), rows), :]
            copies.append(pltpu.make_async_remote_copy(
                src_ref=src, dst_ref=got_ref.at[k], send_sem=send_sems.at[k], recv_sem=recv_sems.at[k],
                device_id=(x, y, 1 - c), device_id_type=MESH))
        for cp in copies:
            cp.start()
        for cp in copies:
            cp.wait()

    return pl.pallas_call(
        body, out_shape=jax.ShapeDtypeStruct((n, rows, cols), full.dtype),
        in_specs=[pl.BlockSpec(memory_space=pl.ANY)], out_specs=pl.BlockSpec(memory_space=pl.ANY),
        scratch_shapes=[pltpu.SemaphoreType.DMA((n,)), pltpu.SemaphoreType.DMA((n,))],
        name="rs_pair_swap")(full)


def _scatter_to_chips(part):
    n, rows, cols = part.shape

    def body(p_ref, got_ref, send_sems, recv_sems, local_sem):
        x, y, c = _coords()
        my_chip = 2 * x + y
        chips = [(1 - x, y), (x, 1 - y), (1 - x, 1 - y)]
        mine = pltpu.make_async_copy(p_ref.at[my_chip], got_ref.at[my_chip], local_sem)
        mine.start()
        copies = []
        for j, (px, py) in enumerate(chips):
            copies.append(pltpu.make_async_remote_copy(
                src_ref=p_ref.at[2 * px + py], dst_ref=got_ref.at[my_chip], send_sem=send_sems.at[j],
                recv_sem=recv_sems.at[j], device_id=(px, py, c), device_id_type=MESH))
        for cp in copies:
            cp.start()
        for j, (px, py) in enumerate(chips):
            pltpu.make_async_remote_copy(
                src_ref=p_ref.at[my_chip], dst_ref=got_ref.at[2 * px + py], send_sem=send_sems.at[j],
                recv_sem=recv_sems.at[j], device_id=(px, py, c), device_id_type=MESH).wait_recv()
        for cp in copies:
            cp.wait_send()
        mine.wait()

    return pl.pallas_call(
        body, out_shape=jax.ShapeDtypeStruct((n, rows, cols), part.dtype),
        in_specs=[pl.BlockSpec(memory_space=pl.ANY)], out_specs=pl.BlockSpec(memory_space=pl.ANY),
        scratch_shapes=[pltpu.SemaphoreType.DMA((3,)), pltpu.SemaphoreType.DMA((3,)), pltpu.SemaphoreType.DMA],
        name="rs_chip_scatter")(part)


def _join_halves_with_sibling(half):
    rows, cols = half.shape

    def body(h_ref, out_ref, send_sem, recv_sem, local_sem):
        x, y, c = _coords()
        my_rows = pl.ds(pl.multiple_of(c * rows, 8), rows)
        mine = pltpu.make_async_copy(h_ref, out_ref.at[my_rows, :], local_sem)
        mine.start()
        push = pltpu.make_async_remote_copy(
            src_ref=h_ref, dst_ref=out_ref.at[my_rows, :], send_sem=send_sem, recv_sem=recv_sem,
            device_id=(x, y, 1 - c), device_id_type=MESH)
        push.start()
        other_rows = pl.ds(pl.multiple_of((1 - c) * rows, 8), rows)
        pltpu.make_async_remote_copy(
            src_ref=h_ref, dst_ref=out_ref.at[other_rows, :], send_sem=send_sem, recv_sem=recv_sem,
            device_id=(x, y, 1 - c), device_id_type=MESH).wait_recv()
        push.wait_send()
        mine.wait()

    return pl.pallas_call(
        body, out_shape=jax.ShapeDtypeStruct((2 * rows, cols), half.dtype),
        in_specs=[pl.BlockSpec(memory_space=pl.ANY)], out_specs=pl.BlockSpec(memory_space=pl.ANY),
        scratch_shapes=[pltpu.SemaphoreType.DMA, pltpu.SemaphoreType.DMA, pltpu.SemaphoreType.DMA],
        name="rs_pair_join")(half)


def _pack_shard(parts):
    flat = jnp.concatenate([p.reshape(-1) for p in parts])
    return flat.reshape(PACK_ROWS, LANE)


def _unpack_shard(flat2d):
    flat = flat2d.reshape(-1)
    out, pos = [], 0
    for shape, size in zip(SHARD_SHAPES, SHARD_SIZES):
        out.append(flat[pos:pos + size].reshape(shape))
        pos += size
    return out


def _uq_to_padded(w_uq):
    per = w_uq.reshape(RQ, H, DN + DR)
    nope = per[:, :, :DN].reshape(RQ, H * DN)
    rope = jnp.pad(per[:, :, DN:], ((0, 0), (0, 0), (0, LANE - DR))).reshape(RQ, H * LANE)
    return jnp.concatenate([nope, rope], axis=1)


def _uq_from_padded(g):
    nope = g[:, :H * DN].reshape(RQ, H, DN)
    rope = g[:, H * DN:].reshape(RQ, H, LANE)[:, :, :DR]
    return jnp.concatenate([nope, rope], axis=2).reshape(RQ, H * (DN + DR))


def _rope_tables(positions):
    inv_freq = ROPE_THETA ** (-jnp.arange(0, DR, 2, dtype=F32) / DR)
    ang = positions.astype(F32)[:, None] * inv_freq
    cos, sin = jnp.cos(ang), jnp.sin(ang)
    return jnp.tile(cos, (1, 4)), jnp.tile(jnp.concatenate([-sin, sin], axis=1), (1, 2))


def _local_step(x, tgt, cos_t, sin_t, mod, weights, small, tiles):
    ts, tm, t_attn, chunk = tiles
    wa, wl, wg, w_uq2, w_ukv, wco, wao, wo, conv_w = weights
    norm_w, conv_b, ln_w, ln_b, q_norm_w, kv_norm_w, fnw = small
    shift, scale, gate = mod[:, 0:D], mod[:, D:2 * D], mod[:, 2 * D:3 * D]

    h = _adaln_norm(x, norm_w, shift, scale, ts)
    proj_a = _mm_nn(h, wa, tm, D, "proj_a")
    proj_l = _mm_nn(h, wl, tm, L_COLS, "proj_l")
    proj_g = _mm_nn(h, wg, tm, D, "proj_g")
    u0, u1, za = _conv_fwd(proj_a, conv_w, conv_b, ln_w, ln_b, ts, chunk)
    qn, kvn, q, k, v = _mla_prep(proj_l, q_norm_w, kv_norm_w, w_uq2, w_ukv, cos_t, sin_t, ts)
    o, lse = _attn_fwd(q, k, v, t_attn)
    (dx2, dza, do, delta, dpg, zb, mg, dmo, dya, dyb, vec_mid) = _middle(
        za, o, proj_g, x, tgt, gate, fnw, wco, wao, wo, ts)
    g_wo = _mm_tn(mg, dmo, tm, D, "grad_w_out")
    g_wco = _mm_tn(za, dya, tm, D, "grad_w_conv_out")
    g_wao = _mm_tn(zb, dyb, tm, D, "grad_w_attn_out")
    dq, dk, dv = _attn_bwd(q, k, v, do, lse, delta, t_attn)
    dpl, g_wuq2, g_wukv, vec_mla = _mla_prep_bwd(
        dq, dk, dv, proj_l, qn, kvn, q_norm_w, kv_norm_w, w_uq2, w_ukv, cos_t, sin_t, ts)
    dpa, g_conv_w, vec_conv = _conv_bwd(dza, proj_a, u0, u1, conv_w, ln_w, ln_b, ts, chunk)
    grad_x, vec_in = _input_bwd(dpa, dpl, dpg, wa, wl, wg, x, dx2, norm_w, scale, ts)
    g_wa = _mm_tn(h, dpa, tm, D, "grad_w_in_a")
    g_wl = _mm_tn(h, dpl, tm, L_COLS, "grad_w_in_l")
    g_wg = _mm_tn(h, dpg, tm, D, "grad_w_in_g")

    dmod = jnp.concatenate([vec_in[0:1], vec_in[1:2], vec_mid[1:2]], axis=1)
    sums = dict(dmod=dmod, norm_w=vec_in[2:3], conv_b=vec_conv[2:3], ln_w=vec_conv[0:1], ln_b=vec_conv[1:2],
                q_norm_w=vec_mla[0:1], kv_norm_w=vec_mla[1:2], final_norm_w=vec_mid[0:1], loss=vec_mid[2:3],
                conv_w=g_conv_w)
    grads = dict(wa=g_wa, wl=g_wl, wg=g_wg, w_uq2=g_wuq2, w_ukv=g_wukv, wco=g_wco, wao=g_wao, wo=g_wo)
    return grad_x, grads, sums


SMALL_ORDER = (("dmod", 3 * D), ("norm_w", D), ("conv_b", D), ("ln_w", D), ("ln_b", D), ("q_norm_w", RQ),
               ("kv_norm_w", RQ), ("final_norm_w", D), ("loss", D), ("conv_w", HALO * D))
SMALL_ROWS = 336


def kernel(x, c, positions, w_ada, b_ada, norm_w, w_in, conv_w, conv_b, conv_ln_w, conv_ln_b, w_conv_out, q_norm_w, w_uq, kv_norm_w, w_ukv, w_attn_out, w_out, final_norm_w, loss_target, m_w_ada, m_b_ada, m_norm_w, m_w_in, m_conv_w, m_conv_b, m_conv_ln_w, m_conv_ln_b, m_w_conv_out, m_q_norm_w, m_w_uq, m_kv_norm_w, m_w_ukv, m_w_attn_out, m_w_out, m_final_norm_w, v_w_ada, v_b_ada, v_norm_w, v_w_in, v_conv_w, v_conv_b, v_conv_ln_w, v_conv_ln_b, v_w_conv_out, v_q_norm_w, v_w_uq, v_kv_norm_w, v_w_ukv, v_w_attn_out, v_w_out, v_final_norm_w):
    ix, iy, ic = _coords()
    chip = 2 * ix + iy
    dev = 4 * ix + 2 * iy + ic
    s = x.shape[1]
    tiles = (256, 512, 512, 32)

    conv_w_pad = jnp.pad(conv_w[0], ((0, HALO - KC), (0, 0)))
    small_in = jnp.concatenate([c.reshape(8, LANE), conv_w_pad.reshape(64, LANE)], axis=0)
    small_all = _allgather8(small_in, 72, True, "gather_c_conv").reshape(N_DEV, 72, LANE)
    c_all = small_all[:, 0:8].reshape(N_DEV, D)
    conv_full = jnp.concatenate(
        [small_all[2 * k, 8:72].reshape(HALO, D // N_CHIP) for k in range(N_CHIP)], axis=1)

    shard = _pack_shard([w_in[0], w_uq[0], w_ukv[0], w_conv_out[0], w_attn_out[0], w_out[0]]).astype(BF16)
    gathered = _allgather8(shard, HALF_ROWS, False, "gather_weights").reshape(N_CHIP, PACK_ROWS, LANE)
    per_chip = [_unpack_shard(gathered[k]) for k in range(N_CHIP)]
    w_in_f = jnp.concatenate([p[0] for p in per_chip], axis=1)
    w_uq_f = jnp.concatenate([p[1] for p in per_chip], axis=1)
    w_ukv_f = jnp.concatenate([p[2] for p in per_chip], axis=1)
    wco = jnp.concatenate([p[3] for p in per_chip], axis=0)
    wao = jnp.concatenate([p[4] for p in per_chip], axis=0)
    wo = jnp.concatenate([p[5] for p in per_chip], axis=0)
    wa = w_in_f[:, 0:A_COLS]
    wl = jnp.pad(w_in_f[:, A_COLS:A_COLS + L_COLS_RAW], ((0, 0), (0, L_COLS - L_COLS_RAW)))
    wg = w_in_f[:, A_COLS + L_COLS_RAW:]
    weights = (wa, wl, wg, _uq_to_padded(w_uq_f), w_ukv_f, wco, wao, wo, conv_full)

    ada_cols = w_ada.shape[2]
    b_shard = lax.dynamic_slice(b_ada, (0, chip * ada_cols), (1, ada_cols))
    mod_part = _ada_fwd(c_all, w_ada[0], b_shard)
    mod_all = _allgather8(mod_part, N_DEV, True, "gather_mod").reshape(N_DEV, N_DEV, ada_cols)
    mod = jnp.concatenate(
        [lax.dynamic_slice(mod_all[2 * k], (dev, 0), (1, ada_cols)) for k in range(N_CHIP)], axis=1)

    cos_t, sin_t = _rope_tables(positions[0])
    small = (norm_w, conv_b, conv_ln_w, conv_ln_b, q_norm_w, kv_norm_w, final_norm_w.reshape(1, D))
    grad_x, grads, sums = _local_step(x[0], loss_target[0], cos_t, sin_t, mod, weights, small, tiles)

    small_flat = jnp.concatenate([sums[name].reshape(-1) for name, _ in SMALL_ORDER])
    small_flat = jnp.pad(small_flat, (0, SMALL_ROWS * LANE - small_flat.shape[0]))
    small_g = _allgather8(small_flat.reshape(SMALL_ROWS, LANE), SMALL_ROWS, True, "gather_small_grads")
    small_g = small_g.reshape(N_DEV, SMALL_ROWS, LANE)
    small_sum = _sum_slabs(small_g, SMALL_ROWS, "sum_small_grads").reshape(-1)
    tot, pos = {}, 0
    for name, size in SMALL_ORDER:
        tot[name] = small_sum[pos:pos + size]
        pos += size
    loss = (0.5 / D) * jnp.sum(tot["loss"])
    dmod_all = small_g.reshape(N_DEV, -1)[:, 0:3 * D]
    g_b_ada = tot["dmod"].reshape(1, 3 * D)
    dmod_shard = lax.dynamic_slice(dmod_all, (0, chip * ada_cols), (N_DEV, ada_cols))
    g_w_ada = _ada_bwd(c_all.T, dmod_shard).reshape(1, D, ada_cols)
    g_conv_w = lax.dynamic_slice(tot["conv_w"].reshape(HALO, D), (0, chip * (D // N_CHIP)), (KC, D // N_CHIP))
    g_conv_w = g_conv_w.reshape(1, KC, D // N_CHIP)

    g_w_in = jnp.concatenate([grads["wa"], grads["wl"][:, 0:L_COLS_RAW], grads["wg"]], axis=1)
    g_w_uq = _uq_from_padded(grads["w_uq2"])
    nin, nuq, nukv, nr = IN_COLS // N_CHIP, 384, 512, D // N_CHIP
    stack = jnp.stack([
        _pack_shard([g_w_in[:, k * nin:(k + 1) * nin], g_w_uq[:, k * nuq:(k + 1) * nuq],
                     grads["w_ukv"][:, k * nukv:(k + 1) * nukv], grads["wco"][k * nr:(k + 1) * nr],
                     grads["wao"][k * nr:(k + 1) * nr], grads["wo"][k * nr:(k + 1) * nr]])
        for k in range(N_CHIP)])
    from_sibling = _swap_halves_with_sibling(stack)
    chip_sum = _add_own_half(stack, from_sibling, ic.reshape(1).astype(jnp.int32), HALF_ROWS // 8)
    arrived = _scatter_to_chips(chip_sum)
    my_half = _sum_slabs(arrived, HALF_ROWS // 8, "sum_chip_slabs")
    g_shard = _unpack_shard(_join_halves_with_sibling(my_half))
    g_w_in_s, g_w_uq_s, g_w_ukv_s, g_wco_s, g_wao_s, g_wo_s = g_shard

    def big(w, g, m, v, tr, name):
        d, nm, nv = _adamw(w[0], g, m[0], v[0], tr, name)
        return g[None], d[None], nm[None], nv[None]

    vec_names = ("b_ada", "norm_w", "conv_b", "conv_ln_w", "conv_ln_b", "q_norm_w", "kv_norm_w", "final_norm_w")
    vec_w = (b_ada, norm_w, conv_b, conv_ln_w, conv_ln_b, q_norm_w, kv_norm_w, final_norm_w)
    vec_m = (m_b_ada, m_norm_w, m_conv_b, m_conv_ln_w, m_conv_ln_b, m_q_norm_w, m_kv_norm_w, m_final_norm_w)
    vec_v = (v_b_ada, v_norm_w, v_conv_b, v_conv_ln_w, v_conv_ln_b, v_q_norm_w, v_kv_norm_w, v_final_norm_w)
    vec_g = (g_b_ada, tot["norm_w"], tot["conv_b"], tot["ln_w"], tot["ln_b"], tot["q_norm_w"], tot["kv_norm_w"],
             tot["final_norm_w"])
    vec_g = tuple(g.reshape(w.shape) for g, w in zip(vec_g, vec_w))
    cat = lambda arrs: jnp.concatenate([a.reshape(-1) for a in arrs]).reshape(-1, LANE)
    vd, vnm, vnv = _adamw(cat(vec_w), cat(vec_g), cat(vec_m), cat(vec_v), cat(vec_w).shape[0], "adamw_vectors")

    def split(packed):
        flat, out, pos = packed.reshape(-1), [], 0
        for w in vec_w:
            out.append(flat[pos:pos + w.size].reshape(w.shape))
            pos += w.size
        return out

    res = {}
    for name, g, d, nm, nv in zip(vec_names, vec_g, split(vd), split(vnm), split(vnv)):
        res[name] = (g, d, nm, nv)
    res["w_ada"] = big(w_ada, g_w_ada[0], m_w_ada, v_w_ada, 256, "adamw_w_ada")
    res["w_in"] = big(w_in, g_w_in_s, m_w_in, v_w_in, 256, "adamw_w_in")
    res["conv_w"] = big(conv_w, g_conv_w[0], m_conv_w, v_conv_w, KC, "adamw_conv_w")
    res["w_conv_out"] = big(w_conv_out, g_wco_s, m_w_conv_out, v_w_conv_out, 256, "adamw_w_conv_out")
    res["w_uq"] = big(w_uq, g_w_uq_s, m_w_uq, v_w_uq, 256, "adamw_w_uq")
    res["w_ukv"] = big(w_ukv, g_w_ukv_s, m_w_ukv, v_w_ukv, 256, "adamw_w_ukv")
    res["w_attn_out"] = big(w_attn_out, g_wao_s, m_w_attn_out, v_w_attn_out, 256, "adamw_w_attn_out")
    res["w_out"] = big(w_out, g_wo_s, m_w_out, v_w_out, 256, "adamw_w_out")

    order = ("w_ada", "b_ada", "norm_w", "w_in", "conv_w", "conv_b", "conv_ln_w", "conv_ln_b", "w_conv_out",
             "q_norm_w", "w_uq", "kv_norm_w", "w_ukv", "w_attn_out", "w_out", "final_norm_w")
    outs = [loss, grad_x[None]]
    for slot in range(4):
        outs += [res[name][slot] for name in order]
    return tuple(outs)
```

```python
import functools

import numpy as np
import jax
import jax.numpy as jnp
from jax import lax
from jax.experimental import pallas as pl
from jax.experimental.pallas import tpu as pltpu

F32 = jnp.float32
BF16 = jnp.bfloat16
MESH = pl.DeviceIdType.MESH

D = 1024
H = 8
DN = 128
DR = 64
RQ = 256
KC = 31
HALO = 32
EPS = 1e-6
ROPE_THETA = 10000.0
N_CHIP = 4
N_DEV = 8
LANE = 128
VMEM_BIG = 56 * 1024 * 1024

ADAM_LR = 0.001
ADAM_B1 = 0.9
ADAM_B2 = 0.999
ADAM_EPS = 1e-08
ADAM_WD = 0.01
ADAM_STEP = 10

A_COLS = 3 * D
L_COLS_RAW = RQ + RQ + DR
L_COLS = 640
G_COLS = 3 * D
IN_COLS = A_COLS + L_COLS_RAW + G_COLS


def _params(sem=None, vmem=None):
    kw = {}
    if sem is not None:
        kw["dimension_semantics"] = sem
    if vmem is not None:
        kw["vmem_limit_bytes"] = vmem
    return pltpu.CompilerParams(**kw)


def _dot(a, b):
    return jnp.dot(a, b, preferred_element_type=F32)


def _dot_nt(a, b):
    return lax.dot_general(a, b, (((1,), (1,)), ((), ())), preferred_element_type=F32)


def _dot_tn(a, b):
    return lax.dot_general(a, b, (((0,), (0,)), ((), ())), preferred_element_type=F32)


def _colsum(v):
    return jnp.sum(v, axis=0, keepdims=True)


def _rowmean(v):
    return jnp.mean(v, axis=-1, keepdims=True)


def _sigmoid(v):
    return jax.nn.sigmoid(v)


def _dsilu(v, s):
    return s * (1.0 + v * (1.0 - s))


def _swap_halves(v, first_half):
    return jnp.where(first_half, pltpu.roll(v, 96, 1), pltpu.roll(v, 32, 1))


def _first_half_mask(rows):
    lane = lax.broadcasted_iota(jnp.int32, (rows, LANE), 1)
    return (lane % 64) < 32


def _adaln_norm(x, norm_w, shift, scale, ts):
    s = x.shape[0]

    def body(x_ref, nw_ref, sh_ref, sc_ref, h_ref):
        xv = x_ref[...]
        r = lax.rsqrt(_rowmean(xv * xv) + EPS)
        y = xv * r * nw_ref[...]
        h_ref[...] = (y * (1.0 + sc_ref[...]) + sh_ref[...]).astype(BF16)

    row = pl.BlockSpec((ts, D), lambda i: (i, 0))
    vec = pl.BlockSpec((1, D), lambda i: (0, 0))
    return pl.pallas_call(
        body, grid=(s // ts,), in_specs=[row, vec, vec, vec], out_specs=row,
        out_shape=jax.ShapeDtypeStruct((s, D), BF16), name="adaln_norm",
        compiler_params=_params(("parallel",)))(x, norm_w, shift, scale)


def _mm_nn(a, b, tm, tn, name):
    m, k = a.shape
    n = b.shape[1]

    def body(a_ref, b_ref, o_ref):
        o_ref[...] = _dot(a_ref[...], b_ref[...])

    return pl.pallas_call(
        body, grid=(n // tn, m // tm),
        in_specs=[pl.BlockSpec((tm, k), lambda j, i: (i, 0)), pl.BlockSpec((k, tn), lambda j, i: (0, j))],
        out_specs=pl.BlockSpec((tm, tn), lambda j, i: (i, j)),
        out_shape=jax.ShapeDtypeStruct((m, n), F32), name=name,
        compiler_params=_params(("parallel", "parallel")))(a, b)


def _mm_tn(a, b, tm, tn, name):
    m, k = a.shape
    n = b.shape[1]

    def body(a_ref, b_ref, o_ref):
        @pl.when(pl.program_id(1) == 0)
        def _():
            o_ref[...] = jnp.zeros_like(o_ref)
        o_ref[...] += _dot_tn(a_ref[...], b_ref[...])

    return pl.pallas_call(
        body, grid=(n // tn, m // tm),
        in_specs=[pl.BlockSpec((tm, k), lambda j, i: (i, 0)), pl.BlockSpec((tm, tn), lambda j, i: (i, j))],
        out_specs=pl.BlockSpec((k, tn), lambda j, i: (0, j)),
        out_shape=jax.ShapeDtypeStruct((k, n), F32), name=name,
        compiler_params=_params(("parallel", "arbitrary")))(a, b)


def _conv_taps(win_ref, w_ref, rows, chunk, offset_of_tap):
    pieces = []
    for c0 in range(0, rows, chunk):
        acc = None
        for j in range(KC):
            term = w_ref[j:j + 1, :] * win_ref[pl.ds(c0 + offset_of_tap(j), chunk), :]
            acc = term if acc is None else acc + term
        pieces.append(acc)
    return pieces


def _conv_fwd(proj_a, conv_w, conv_b, ln_w, ln_b, ts, chunk):
    s = proj_a.shape[0]

    def body(av_ref, al_ref, ag_ref, w_ref, b_ref, lw_ref, lb_ref, u0_ref, u1_ref, za_ref, win_ref):
        @pl.when(pl.program_id(0) == 0)
        def _():
            win_ref[0:HALO, :] = jnp.zeros((HALO, D), F32)

        u0 = av_ref[...] * _sigmoid(al_ref[...])
        u0_ref[...] = u0
        win_ref[HALO:HALO + ts, :] = u0
        pieces = _conv_taps(win_ref, w_ref, ts, chunk, lambda j: HALO - (KC - 1) + j)
        for n, acc in enumerate(pieces):
            u1_ref[n * chunk:(n + 1) * chunk, :] = acc + b_ref[...]
        win_ref[0:HALO, :] = win_ref[ts:ts + HALO, :]

        u1 = u1_ref[...]
        xc = u1 - _rowmean(u1)
        rstd = lax.rsqrt(_rowmean(xc * xc) + EPS)
        u2 = xc * rstd * lw_ref[...] + lb_ref[...]
        u3 = u2 * _sigmoid(u2)
        ag = ag_ref[...]
        za_ref[...] = (u3 * (ag * _sigmoid(ag))).astype(BF16)

    col = lambda c: pl.BlockSpec((ts, D), lambda i, c=c: (i, c))
    row = pl.BlockSpec((ts, D), lambda i: (i, 0))
    vec = pl.BlockSpec((1, D), lambda i: (0, 0))
    return pl.pallas_call(
        body, grid=(s // ts,),
        in_specs=[col(0), col(1), col(2), pl.BlockSpec((HALO, D), lambda i: (0, 0)), vec, vec, vec],
        out_specs=[row, row, row],
        out_shape=[jax.ShapeDtypeStruct((s, D), F32), jax.ShapeDtypeStruct((s, D), F32),
                   jax.ShapeDtypeStruct((s, D), BF16)],
        scratch_shapes=[pltpu.VMEM((ts + HALO, D), F32)], name="conv_fwd",
        compiler_params=_params(("arbitrary",)))(proj_a, proj_a, proj_a, conv_w, conv_b, ln_w, ln_b)


def _conv_bwd(dza, proj_a, u0, u1, conv_w, ln_w, ln_b, ts, chunk):
    s = dza.shape[0]
    nt = s // ts
    per = ts // HALO

    def body(dza_ref, av_ref, al_ref, ag_ref, u0_ref, u0p_ref, u1_ref, w_ref, lw_ref, lb_ref,
             dpa_ref, gw_ref, gv_ref, dwin_ref, uwin_ref, du0_ref, gwp_ref):
        step = pl.program_id(0)
        tile = nt - 1 - step

        @pl.when(step == 0)
        def _():
            dwin_ref[ts:ts + HALO, :] = jnp.zeros((HALO, D), F32)
            gwp_ref[...] = jnp.zeros_like(gwp_ref)
            gv_ref[...] = jnp.zeros_like(gv_ref)

        ag = ag_ref[...]
        sg = _sigmoid(ag)
        u1 = u1_ref[...]
        xc = u1 - _rowmean(u1)
        rstd = lax.rsqrt(_rowmean(xc * xc) + EPS)
        xh = xc * rstd
        u2 = xh * lw_ref[...] + lb_ref[...]
        s2 = _sigmoid(u2)
        dz = dza_ref[...]
        du3 = dz * (ag * sg)
        dpa_ref[:, 2 * D:3 * D] = (dz * (u2 * s2) * _dsilu(ag, sg)).astype(BF16)
        du2 = du3 * _dsilu(u2, s2)
        gv_ref[0:1, :] += _colsum(du2 * xh)
        gv_ref[1:2, :] += _colsum(du2)
        dxh = du2 * lw_ref[...]
        du1 = rstd * (dxh - _rowmean(dxh) - xh * _rowmean(dxh * xh))
        gv_ref[2:3, :] += _colsum(du1)
        dwin_ref[0:ts, :] = du1

        uwin_ref[0:HALO, :] = jnp.where(tile == 0, 0.0, u0p_ref[...])
        uwin_ref[HALO:HALO + ts, :] = u0_ref[...]

        pieces = _conv_taps(dwin_ref, w_ref, ts, chunk, lambda j: (KC - 1) - j)
        for n, acc in enumerate(pieces):
            du0_ref[n * chunk:(n + 1) * chunk, :] = acc
        for c0 in range(0, ts, chunk):
            dchunk = dwin_ref[c0:c0 + chunk, :]
            for j in range(KC):
                prod = dchunk * uwin_ref[pl.ds(c0 + HALO - (KC - 1) + j, chunk), :]
                gwp_ref[8 * j:8 * j + 8, :] += jnp.sum(prod.reshape(chunk // 8, 8, D), axis=0)
        dwin_ref[ts:ts + HALO, :] = dwin_ref[0:HALO, :]

        du0 = du0_ref[...]
        al = al_ref[...]
        sl = _sigmoid(al)
        dpa_ref[:, 0:D] = (du0 * sl).astype(BF16)
        dpa_ref[:, D:2 * D] = (du0 * av_ref[...] * sl * (1.0 - sl)).astype(BF16)

        @pl.when(step == nt - 1)
        def _():
            for j in range(KC):
                gw_ref[j:j + 1, :] = _colsum(gwp_ref[8 * j:8 * j + 8, :])
            gw_ref[KC:HALO, :] = jnp.zeros((HALO - KC, D), F32)

    rev = lambda i: nt - 1 - i
    col = lambda c: pl.BlockSpec((ts, D), lambda i, c=c: (rev(i), c))
    row = pl.BlockSpec((ts, D), lambda i: (rev(i), 0))
    vec = pl.BlockSpec((1, D), lambda i: (0, 0))
    halo = pl.BlockSpec((HALO, D), lambda i: (jnp.maximum(rev(i) * per - 1, 0), 0))
    return pl.pallas_call(
        body, grid=(nt,),
        in_specs=[row, col(0), col(1), col(2), row, halo, row, pl.BlockSpec((HALO, D), lambda i: (0, 0)), vec, vec],
        out_specs=[pl.BlockSpec((ts, A_COLS), lambda i: (rev(i), 0)),
                   pl.BlockSpec((HALO, D), lambda i: (0, 0)), pl.BlockSpec((8, D), lambda i: (0, 0))],
        out_shape=[jax.ShapeDtypeStruct((s, A_COLS), BF16), jax.ShapeDtypeStruct((HALO, D), F32),
                   jax.ShapeDtypeStruct((8, D), F32)],
        scratch_shapes=[pltpu.VMEM((ts + HALO, D), F32), pltpu.VMEM((ts + HALO, D), F32),
                        pltpu.VMEM((ts, D), F32), pltpu.VMEM((8 * HALO, D), F32)],
        name="conv_bwd", compiler_params=_params(("arbitrary",), VMEM_BIG))(
            dza, proj_a, proj_a, proj_a, u0, u0, u1, conv_w, ln_w, ln_b)


def _mla_prep(proj_l, q_norm_w, kv_norm_w, w_uq2, w_ukv, cos_t, sin_t, ts):
    s = proj_l.shape[0]

    def body(pl_ref, qw_ref, kw_ref, wq_ref, wkv_ref, c_ref, s_ref, qn_ref, kvn_ref, q_ref, k_ref, v_ref):
        first = _first_half_mask(ts)
        cs = c_ref[...]
        sn = s_ref[...]

        def rms(v, w):
            return v * lax.rsqrt(_rowmean(v * v) + EPS) * w

        def rope(v):
            return v * cs + _swap_halves(v, first) * sn

        qn = rms(pl_ref[:, 0:RQ], qw_ref[...]).astype(BF16)
        kvn = rms(pl_ref[:, RQ:2 * RQ], kw_ref[...]).astype(BF16)
        qn_ref[...] = qn
        kvn_ref[...] = kvn
        q = _dot(qn, wq_ref[...])
        kv = _dot(kvn, wkv_ref[...])
        kr = rope(pl_ref[:, 2 * RQ:2 * RQ + LANE]).astype(BF16)
        for h in range(H):
            q_ref[h, :, 0:DN] = q[:, DN * h:DN * (h + 1)].astype(BF16)
            q_ref[h, :, DN:2 * DN] = rope(q[:, H * DN + LANE * h:H * DN + LANE * (h + 1)]).astype(BF16)
            k_ref[h, :, 0:DN] = kv[:, 2 * DN * h:2 * DN * h + DN].astype(BF16)
            k_ref[h, :, DN:2 * DN] = kr
            v_ref[h] = kv[:, 2 * DN * h + DN:2 * DN * (h + 1)].astype(BF16)

    const = lambda shape: pl.BlockSpec(shape, lambda i: (0,) * len(shape))
    rowb = lambda w: pl.BlockSpec((ts, w), lambda i: (i, 0))
    head = lambda w: pl.BlockSpec((H, ts, w), lambda i: (0, i, 0))
    return pl.pallas_call(
        body, grid=(s // ts,),
        in_specs=[rowb(L_COLS), const((1, RQ)), const((1, RQ)), const((RQ, 2 * H * DN)), const((RQ, 2 * H * DN)),
                  rowb(LANE), rowb(LANE)],
        out_specs=[rowb(RQ), rowb(RQ), head(2 * DN), head(2 * DN), head(DN)],
        out_shape=[jax.ShapeDtypeStruct((s, RQ), BF16), jax.ShapeDtypeStruct((s, RQ), BF16),
                   jax.ShapeDtypeStruct((H, s, 2 * DN), BF16), jax.ShapeDtypeStruct((H, s, 2 * DN), BF16),
                   jax.ShapeDtypeStruct((H, s, DN), BF16)],
        name="mla_prep", compiler_params=_params(("parallel",)))(
            proj_l, q_norm_w, kv_norm_w, w_uq2, w_ukv, cos_t, sin_t)


def _mla_prep_bwd(dq, dk, dv, proj_l, qn, kvn, q_norm_w, kv_norm_w, w_uq2, w_ukv, cos_t, sin_t, ts):
    s = proj_l.shape[0]

    def body(dq_ref, dk_ref, dv_ref, pl_ref, qn_ref, kvn_ref, qw_ref, kw_ref, wq_ref, wkv_ref, c_ref, s_ref,
             dpl_ref, gwq_ref, gwkv_ref, gv_ref, dq2_ref, dkv2_ref):
        @pl.when(pl.program_id(0) == 0)
        def _():
            gwq_ref[...] = jnp.zeros_like(gwq_ref)
            gwkv_ref[...] = jnp.zeros_like(gwkv_ref)
            gv_ref[...] = jnp.zeros_like(gv_ref)

        first = _first_half_mask(ts)
        cs = c_ref[...]
        sn = s_ref[...]

        def rope_bwd(g):
            return g * cs + _swap_halves(g * sn, first)

        def rms_bwd(v, w, dy):
            r = lax.rsqrt(_rowmean(v * v) + EPS)
            vh = v * r
            dvh = dy * w
            return r * (dvh - vh * _rowmean(dvh * vh)), _colsum(dy * vh)

        dkr = None
        for h in range(H):
            dq2_ref[:, DN * h:DN * (h + 1)] = dq_ref[h, :, 0:DN].astype(BF16)
            dq2_ref[:, H * DN + LANE * h:H * DN + LANE * (h + 1)] = rope_bwd(dq_ref[h, :, DN:2 * DN]).astype(BF16)
            dkv2_ref[:, 2 * DN * h:2 * DN * h + DN] = dk_ref[h, :, 0:DN].astype(BF16)
            dkv2_ref[:, 2 * DN * h + DN:2 * DN * (h + 1)] = dv_ref[h].astype(BF16)
            part = dk_ref[h, :, DN:2 * DN]
            dkr = part if dkr is None else dkr + part

        dq2 = dq2_ref[...]
        dkv2 = dkv2_ref[...]
        gwq_ref[...] += _dot_tn(qn_ref[...], dq2)
        gwkv_ref[...] += _dot_tn(kvn_ref[...], dkv2)
        dcq, gq = rms_bwd(pl_ref[:, 0:RQ], qw_ref[...], _dot_nt(dq2, wq_ref[...]))
        dckv, gkv = rms_bwd(pl_ref[:, RQ:2 * RQ], kw_ref[...], _dot_nt(dkv2, wkv_ref[...]))
        gv_ref[0:1, :] += gq
        gv_ref[1:2, :] += gkv
        dpl_ref[:, 0:RQ] = dcq.astype(BF16)
        dpl_ref[:, RQ:2 * RQ] = dckv.astype(BF16)
        dpl_ref[:, 2 * RQ:2 * RQ + LANE] = rope_bwd(dkr).astype(BF16)

    const = lambda shape: pl.BlockSpec(shape, lambda i: (0,) * len(shape))
    rowb = lambda w: pl.BlockSpec((ts, w), lambda i: (i, 0))
    head = lambda w: pl.BlockSpec((H, ts, w), lambda i: (0, i, 0))
    return pl.pallas_call(
        body, grid=(s // ts,),
        in_specs=[head(2 * DN), head(2 * DN), head(DN), rowb(L_COLS), rowb(RQ), rowb(RQ), const((1, RQ)),
                  const((1, RQ)), const((RQ, 2 * H * DN)), const((RQ, 2 * H * DN)), rowb(LANE), rowb(LANE)],
        out_specs=[rowb(L_COLS), const((RQ, 2 * H * DN)), const((RQ, 2 * H * DN)), const((8, RQ))],
        out_shape=[jax.ShapeDtypeStruct((s, L_COLS), BF16), jax.ShapeDtypeStruct((RQ, 2 * H * DN), F32),
                   jax.ShapeDtypeStruct((RQ, 2 * H * DN), F32), jax.ShapeDtypeStruct((8, RQ), F32)],
        scratch_shapes=[pltpu.VMEM((ts, 2 * H * DN), BF16), pltpu.VMEM((ts, 2 * H * DN), BF16)],
        name="mla_prep_bwd", compiler_params=_params(("arbitrary",), VMEM_BIG))(
            dq, dk, dv, proj_l, qn, kvn, q_norm_w, kv_norm_w, w_uq2, w_ukv, cos_t, sin_t)


def _causal_pairs(n, by_key):
    if by_key:
        pairs = [(i, j) for j in range(n) for i in range(j, n)]
    else:
        pairs = [(i, j) for i in range(n) for j in range(i + 1)]
    return (jnp.asarray(np.array([p[0] for p in pairs], np.int32)),
            jnp.asarray(np.array([p[1] for p in pairs], np.int32)))


def _causal_mask(i, j, t):
    rows = i * t + lax.broadcasted_iota(jnp.int32, (t, t), 0)
    cols = j * t + lax.broadcasted_iota(jnp.int32, (t, t), 1)
    return cols <= rows


def _attn_fwd(q, k, v, t):
    s = q.shape[1]
    n = s // t
    scale = float((DN + DR) ** -0.5)
    qi, ki = _causal_pairs(n, by_key=False)

    def body(qi_ref, ki_ref, q_ref, k_ref, v_ref, o_ref, lse_ref, m_sc, l_sc, acc_sc):
        p = pl.program_id(1)
        i = qi_ref[p]
        j = ki_ref[p]

        @pl.when(j == 0)
        def _():
            m_sc[...] = jnp.full_like(m_sc, -jnp.inf)
            l_sc[...] = jnp.zeros_like(l_sc)
            acc_sc[...] = jnp.zeros_like(acc_sc)

        sc = _dot_nt(q_ref[0], k_ref[0]) * scale
        sc = jnp.where(_causal_mask(i, j, t), sc, -jnp.inf)
        m_prev = m_sc[...]
        m_new = jnp.maximum(m_prev, jnp.max(sc, axis=-1, keepdims=True))
        alpha = jnp.exp(m_prev - m_new)
        pe = jnp.exp(sc - m_new)
        l_sc[...] = alpha * l_sc[...] + jnp.sum(pe, axis=-1, keepdims=True)
        acc_sc[...] = alpha * acc_sc[...] + _dot(pe.astype(BF16), v_ref[0])
        m_sc[...] = m_new

        @pl.when(j == i)
        def _():
            o_ref[...] = acc_sc[...] / l_sc[...]
            lse_ref[0] = m_sc[...] + jnp.log(l_sc[...])

    grid_spec = pltpu.PrefetchScalarGridSpec(
        num_scalar_prefetch=2, grid=(H, int(qi.shape[0])),
        in_specs=[pl.BlockSpec((1, t, 2 * DN), lambda h, p, qi, ki: (h, qi[p], 0)),
                  pl.BlockSpec((1, t, 2 * DN), lambda h, p, qi, ki: (h, ki[p], 0)),
                  pl.BlockSpec((1, t, DN), lambda h, p, qi, ki: (h, ki[p], 0))],
        out_specs=[pl.BlockSpec((t, DN), lambda h, p, qi, ki: (qi[p], h)),
                   pl.BlockSpec((1, t, 1), lambda h, p, qi, ki: (h, qi[p], 0))],
        scratch_shapes=[pltpu.VMEM((t, 1), F32), pltpu.VMEM((t, 1), F32), pltpu.VMEM((t, DN), F32)])
    return pl.pallas_call(
        body, grid_spec=grid_spec,
        out_shape=[jax.ShapeDtypeStruct((s, H * DN), F32), jax.ShapeDtypeStruct((H, s, 1), F32)],
        name="attn_fwd", compiler_params=_params(("parallel", "arbitrary")))(qi, ki, q, k, v)


def _attn_bwd(q, k, v, do, lse, delta, t):
    s = q.shape[1]
    n = s // t
    scale = float((DN + DR) ** -0.5)
    qi, ki = _causal_pairs(n, by_key=True)

    def body(qi_ref, ki_ref, q_ref, k_ref, v_ref, do_ref, lse_ref, dl_ref, dq_ref, dk_ref, dv_ref, dk_sc, dv_sc):
        p = pl.program_id(1)
        i = qi_ref[p]
        j = ki_ref[p]

        @pl.when(p == 0)
        def _():
            dq_ref[...] = jnp.zeros_like(dq_ref)

        @pl.when(i == j)
        def _():
            dk_sc[...] = jnp.zeros_like(dk_sc)
            dv_sc[...] = jnp.zeros_like(dv_sc)

        qv = q_ref[0]
        kv = k_ref[0]
        dov = do_ref[...]
        sc = _dot_nt(qv, kv) * scale
        sc = jnp.where(_causal_mask(i, j, t), sc, -jnp.inf)
        pr = jnp.exp(sc - lse_ref[0])
        dp = _dot_nt(dov, v_ref[0])
        ds = (pr * (dp - dl_ref[0]) * scale).astype(BF16)
        dv_sc[...] += _dot_tn(pr.astype(BF16), dov)
        dk_sc[...] += _dot_tn(ds, qv)
        rows = pl.ds(pl.multiple_of(i * t, t), t)
        dq_ref[0, rows, :] += _dot(ds, kv)

        @pl.when(i == n - 1)
        def _():
            dk_ref[0] = dk_sc[...]
            dv_ref[0] = dv_sc[...]

    grid_spec = pltpu.PrefetchScalarGridSpec(
        num_scalar_prefetch=2, grid=(H, int(qi.shape[0])),
        in_specs=[pl.BlockSpec((1, t, 2 * DN), lambda h, p, qi, ki: (h, qi[p], 0)),
                  pl.BlockSpec((1, t, 2 * DN), lambda h, p, qi, ki: (h, ki[p], 0)),
                  pl.BlockSpec((1, t, DN), lambda h, p, qi, ki: (h, ki[p], 0)),
                  pl.BlockSpec((t, DN), lambda h, p, qi, ki: (qi[p], h)),
                  pl.BlockSpec((1, t, 1), lambda h, p, qi, ki: (h, qi[p], 0)),
                  pl.BlockSpec((1, t, 1), lambda h, p, qi, ki: (h, qi[p], 0))],
        out_specs=[pl.BlockSpec((1, s, 2 * DN), lambda h, p, qi, ki: (h, 0, 0)),
                   pl.BlockSpec((1, t, 2 * DN), lambda h, p, qi, ki: (h, ki[p], 0)),
                   pl.BlockSpec((1, t, DN), lambda h, p, qi, ki: (h, ki[p], 0))],
        scratch_shapes=[pltpu.VMEM((t, 2 * DN), F32), pltpu.VMEM((t, DN), F32)])
    return pl.pallas_call(
        body, grid_spec=grid_spec,
        out_shape=[jax.ShapeDtypeStruct((H, s, 2 * DN), F32), jax.ShapeDtypeStruct((H, s, 2 * DN), F32),
                   jax.ShapeDtypeStruct((H, s, DN), F32)],
        name="attn_bwd", compiler_params=_params(("parallel", "arbitrary"), VMEM_BIG))(
            qi, ki, q, k, v, do, lse, delta)


def _middle(za, o, proj_g, x, tgt, gate, fnw, wco, wao, wo, ts):
    s = x.shape[0]
    inv_d = 1.0 / D

    def body(za_ref, o_ref, bg_ref, ga_ref, gb_ref, x_ref, t_ref, gate_ref, fnw_ref, wco_ref, wao_ref, wo_ref,
             dx2_ref, dza_ref, do_ref, dl_ref, dpg_ref, zb_ref, mg_ref, dmo_ref, dya_ref, dyb_ref, vec_ref):
        @pl.when(pl.program_id(0) == 0)
        def _():
            vec_ref[...] = jnp.zeros_like(vec_ref)

        ov = o_ref[...]
        bg = bg_ref[...]
        sb = _sigmoid(bg)
        silu_b = bg * sb
        zb = (ov * silu_b).astype(BF16)
        zb_ref[...] = zb
        ya = _dot(za_ref[...], wco_ref[...])
        yb = _dot(zb, wao_ref[...])
        sa = _sigmoid(ga_ref[...])
        sg = _sigmoid(gb_ref[...])
        mg = (sa * ya + sg * yb).astype(BF16)
        mg_ref[...] = mg
        mo = _dot(mg, wo_ref[...])
        gate_v = gate_ref[...]
        x2 = x_ref[...] + gate_v * mo
        r = lax.rsqrt(_rowmean(x2 * x2) + EPS)
        xh = x2 * r
        fw = fnw_ref[...]
        e = xh * fw - t_ref[...]
        vec_ref[2:3, :] += _colsum(e * e)
        dy = e * inv_d
        vec_ref[0:1, :] += _colsum(dy * xh)
        dxh = dy * fw
        dx2 = r * (dxh - xh * _rowmean(dxh * xh))
        dx2_ref[...] = dx2
        vec_ref[1:2, :] += _colsum(dx2 * mo)
        dmo = (gate_v * dx2).astype(BF16)
        dmo_ref[...] = dmo
        dmg = _dot_nt(dmo, wo_ref[...])
        dya = (sa * dmg).astype(BF16)
        dyb = (sg * dmg).astype(BF16)
        dya_ref[...] = dya
        dyb_ref[...] = dyb
        dpg_ref[:, D:2 * D] = (dmg * ya * (sa * (1.0 - sa))).astype(BF16)
        dpg_ref[:, 2 * D:3 * D] = (dmg * yb * (sg * (1.0 - sg))).astype(BF16)
        dza_ref[...] = _dot_nt(dya, wco_ref[...])
        dzb = _dot_nt(dyb, wao_ref[...])
        dov = dzb * silu_b
        do_ref[...] = dov.astype(BF16)
        dpg_ref[:, 0:D] = (dzb * ov * _dsilu(bg, sb)).astype(BF16)
        dprod = dov * ov
        for h in range(H):
            dl_ref[h] = jnp.sum(dprod[:, DN * h:DN * (h + 1)], axis=-1, keepdims=True)

    col = lambda c: pl.BlockSpec((ts, D), lambda i, c=c: (i, c))
    row = pl.BlockSpec((ts, D), lambda i: (i, 0))
    vec = pl.BlockSpec((1, D), lambda i: (0, 0))
    wsp = pl.BlockSpec((D, D), lambda i: (0, 0))
    bf = jax.ShapeDtypeStruct((s, D), BF16)
    ff = jax.ShapeDtypeStruct((s, D), F32)
    return pl.pallas_call(
        body, grid=(s // ts,),
        in_specs=[row, row, col(0), col(1), col(2), row, row, vec, vec, wsp, wsp, wsp],
        out_specs=[row, row, row, pl.BlockSpec((H, ts, 1), lambda i: (0, i, 0)),
                   pl.BlockSpec((ts, G_COLS), lambda i: (i, 0)), row, row, row, row, row,
                   pl.BlockSpec((8, D), lambda i: (0, 0))],
        out_shape=[ff, ff, bf, jax.ShapeDtypeStruct((H, s, 1), F32), jax.ShapeDtypeStruct((s, G_COLS), BF16),
                   bf, bf, bf, bf, bf, jax.ShapeDtypeStruct((8, D), F32)],
        name="middle", compiler_params=_params(("arbitrary",), VMEM_BIG))(
            za, o, proj_g, proj_g, proj_g, x, tgt, gate, fnw, wco, wao, wo)


def _input_bwd(dpa, dpl, dpg, wa, wl, wg, x, dx2, norm_w, scale, ts):
    s = x.shape[0]

    def body(dpa_ref, dpl_ref, dpg_ref, wa_ref, wl_ref, wg_ref, x_ref, dx2_ref, nw_ref, sc_ref, gx_ref, gv_ref):
        @pl.when(pl.program_id(0) == 0)
        def _():
            gv_ref[...] = jnp.zeros_like(gv_ref)

        dh = (_dot_nt(dpa_ref[...], wa_ref[...]) + _dot_nt(dpl_ref[...], wl_ref[...])
              + _dot_nt(dpg_ref[...], wg_ref[...]))
        xv = x_ref[...]
        r = lax.rsqrt(_rowmean(xv * xv) + EPS)
        xh = xv * r
        nw = nw_ref[...]
        gv_ref[0:1, :] += _colsum(dh)
        gv_ref[1:2, :] += _colsum(dh * (xh * nw))
        dy = dh * (1.0 + sc_ref[...])
        gv_ref[2:3, :] += _colsum(dy * xh)
        dxh = dy * nw
        gx_ref[...] = dx2_ref[...] + r * (dxh - xh * _rowmean(dxh * xh))

    const = lambda shape: pl.BlockSpec(shape, lambda i: (0, 0))
    rowb = lambda w: pl.BlockSpec((ts, w), lambda i: (i, 0))
    return pl.pallas_call(
        body, grid=(s // ts,),
        in_specs=[rowb(A_COLS), rowb(L_COLS), rowb(G_COLS), const((D, A_COLS)), const((D, L_COLS)),
                  const((D, G_COLS)), rowb(D), rowb(D), const((1, D)), const((1, D))],
        out_specs=[rowb(D), const((8, D))],
        out_shape=[jax.ShapeDtypeStruct((s, D), F32), jax.ShapeDtypeStruct((8, D), F32)],
        name="input_bwd", compiler_params=_params(("arbitrary",), VMEM_BIG))(
            dpa, dpl, dpg, wa, wl, wg, x, dx2, norm_w, scale)


def _adamw(w, g, m, v, tr, name):
    rows, cols = w.shape
    c1 = 1.0 - ADAM_B1 ** ADAM_STEP
    c2 = 1.0 - ADAM_B2 ** ADAM_STEP

    def body(w_ref, g_ref, m_ref, v_ref, d_ref, nm_ref, nv_ref):
        gv = g_ref[...]
        nm = ADAM_B1 * m_ref[...] + (1.0 - ADAM_B1) * gv
        nv = ADAM_B2 * v_ref[...] + (1.0 - ADAM_B2) * (gv * gv)
        nm_ref[...] = nm
        nv_ref[...] = nv
        d_ref[...] = -ADAM_LR * ((nm / c1) / (jnp.sqrt(nv / c2) + ADAM_EPS) + ADAM_WD * w_ref[...])

    blk = pl.BlockSpec((tr, cols), lambda i: (i, 0))
    shp = jax.ShapeDtypeStruct((rows, cols), F32)
    return pl.pallas_call(
        body, grid=(rows // tr,), in_specs=[blk] * 4, out_specs=[blk] * 3, out_shape=[shp] * 3, name=name,
        compiler_params=_params(("parallel",)))(w, g, m, v)


def _ada_fwd(c_all, w_ada_shard, b_ada_shard):
    def body(c_ref, w_ref, b_ref, o_ref):
        cv = c_ref[...]
        o_ref[...] = jnp.dot(cv * _sigmoid(cv), w_ref[...], preferred_element_type=F32,
                             precision=lax.Precision.HIGHEST) + b_ref[...]

    return pl.pallas_call(
        body, out_shape=jax.ShapeDtypeStruct((N_DEV, w_ada_shard.shape[1]), F32), name="ada_fwd")(
            c_all, w_ada_shard, b_ada_shard)


def _ada_bwd(c_all_t, dmod_shard):
    def body(c_ref, d_ref, o_ref):
        cv = c_ref[...]
        o_ref[...] = jnp.dot(cv * _sigmoid(cv), d_ref[...], preferred_element_type=F32,
                             precision=lax.Precision.HIGHEST)

    return pl.pallas_call(
        body, out_shape=jax.ShapeDtypeStruct((D, dmod_shard.shape[1]), F32), name="ada_bwd")(c_all_t, dmod_shard)


def _sum_slabs(stack, tr, name):
    n, rows, cols = stack.shape

    def body(s_ref, o_ref):
        acc = s_ref[0]
        for k in range(1, n):
            acc = acc + s_ref[k]
        o_ref[...] = acc

    return pl.pallas_call(
        body, grid=(rows // tr,), in_specs=[pl.BlockSpec((n, tr, cols), lambda i: (0, i, 0))],
        out_specs=pl.BlockSpec((tr, cols), lambda i: (i, 0)), out_shape=jax.ShapeDtypeStruct((rows, cols), F32),
        name=name, compiler_params=_params(("parallel",)))(stack)


def _add_own_half(full, other, core, tr, name):
    n, rows, cols = other.shape
    per = rows // tr

    def body(c_ref, f_ref, o_ref, out_ref):
        out_ref[...] = f_ref[...] + o_ref[...]

    grid_spec = pltpu.PrefetchScalarGridSpec(
        num_scalar_prefetch=1, grid=(n, per),
        in_specs=[pl.BlockSpec((1, tr, cols), lambda k, i, c: (k, c[0] * per + i, 0)),
                  pl.BlockSpec((1, tr, cols), lambda k, i, c: (k, i, 0))],
        out_specs=pl.BlockSpec((1, tr, cols), lambda k, i, c: (k, i, 0)))
    return pl.pallas_call(
        body, grid_spec=grid_spec, out_shape=jax.ShapeDtypeStruct((n, rows, cols), F32), name=name,
        compiler_params=_params(("parallel", "parallel")))(core, full, other)


def _coords():
    return lax.axis_index("x"), lax.axis_index("y"), lax.axis_index("c")


def _allgather8(block, src_rows, vmem, name):
    n = block.shape[1]
    m = src_rows
    sliced = block.shape[0] != m

    def body(x_ref, out_ref, send_sems, recv_sems, local_sem):
        x, y, c = _coords()
        me, sibling = (x, y, c), (x, y, 1 - c)
        chips = [(1 - x, y), (x, 1 - y), (1 - x, 1 - y)]
        src = x_ref.at[pl.ds(pl.multiple_of(c * m, 16), m), :] if sliced else x_ref

        def rows(px, py, pc):
            return out_ref.at[pl.ds(pl.multiple_of((4 * px + 2 * py + pc) * m, 8), m), :]

        def copy(k, blk, to, source=None):
            return pltpu.make_async_remote_copy(
                src_ref=rows(*blk) if source is None else source, dst_ref=rows(*blk),
                send_sem=send_sems.at[k], recv_sem=recv_sems.at[k], device_id=to, device_id_type=MESH)

        mine = pltpu.make_async_copy(src, rows(*me), local_sem)
        mine.start()
        first = [copy(0, me, sibling, source=src)]
        first += [copy(1 + j, me, (*chip, c), source=src) for j, chip in enumerate(chips)]
        for cp in first:
            cp.start()
        passed = [copy(4 + j, (*chip, c), sibling) for j, chip in enumerate(chips)]
        for j, chip in enumerate(chips):
            copy(1 + j, (*chip, c), me).wait_recv()
            passed[j].start()
        copy(0, sibling, me).wait_recv()
        for j, chip in enumerate(chips):
            copy(4 + j, (*chip, 1 - c), me).wait_recv()
        for cp in first + passed:
            cp.wait_send()
        mine.wait()

    space = pltpu.VMEM if vmem else pl.ANY
    return pl.pallas_call(
        body, out_shape=jax.ShapeDtypeStruct((N_DEV * m, n), block.dtype),
        in_specs=[pl.BlockSpec(memory_space=space)], out_specs=pl.BlockSpec(memory_space=space),
        scratch_shapes=[pltpu.SemaphoreType.DMA((7,)), pltpu.SemaphoreType.DMA((7,)), pltpu.SemaphoreType.DMA],
        name=name)(block)


HBM_REF = pl.BlockSpec(memory_space=pl.ANY)


def _gather_weights(shards):
    n = len(shards)
    halves = [a.shape[0] // 2 for a in shards]

    def body(*refs):
        x_refs, out_refs = refs[:n], refs[n:2 * n]
        send_sems, recv_sems, local_sems = refs[2 * n:]
        x, y, c = _coords()
        me, sibling = (x, y, c), (x, y, 1 - c)
        chips = [(1 - x, y), (x, 1 - y), (1 - x, 1 - y)]

        def src(a):
            return x_refs[a].at[pl.ds(pl.multiple_of(c * halves[a], 16), halves[a]), :]

        def blk(a, px, py, pc):
            return out_refs[a].at[4 * px + 2 * py + pc]

        def copy(a, k, who, to, source=None):
            return pltpu.make_async_remote_copy(
                src_ref=blk(a, *who) if source is None else source, dst_ref=blk(a, *who),
                send_sem=send_sems.at[7 * a + k], recv_sem=recv_sems.at[7 * a + k], device_id=to,
                device_id_type=MESH)

        mine = [pltpu.make_async_copy(src(a), blk(a, *me), local_sems.at[a]) for a in range(n)]
        for cp in mine:
            cp.start()
        started = []
        for a in range(n):
            started.append(copy(a, 0, me, sibling, source=src(a)))
            started += [copy(a, 1 + j, me, (*chip, c), source=src(a)) for j, chip in enumerate(chips)]
        for cp in started:
            cp.start()
        for j, chip in enumerate(chips):
            for a in range(n):
                copy(a, 1 + j, (*chip, c), me).wait_recv()
                onward = copy(a, 4 + j, (*chip, c), sibling)
                onward.start()
                started.append(onward)
        for a in range(n):
            copy(a, 0, sibling, me).wait_recv()
        for j, chip in enumerate(chips):
            for a in range(n):
                copy(a, 4 + j, (*chip, 1 - c), me).wait_recv()
        for cp in started:
            cp.wait_send()
        for cp in mine:
            cp.wait()

    outs = pl.pallas_call(
        body, out_shape=[jax.ShapeDtypeStruct((N_DEV, h, a.shape[1]), a.dtype) for a, h in zip(shards, halves)],
        in_specs=[HBM_REF] * n, out_specs=[HBM_REF] * n,
        scratch_shapes=[pltpu.SemaphoreType.DMA((7 * n,)), pltpu.SemaphoreType.DMA((7 * n,)),
                        pltpu.SemaphoreType.DMA((n,))],
        name="gather_weights")(*shards)
    return [o.reshape(N_CHIP, a.shape[0], a.shape[1]) for o, a in zip(outs, shards)]


def _swap_halves_with_sibling(fulls):
    n = len(fulls)
    halves = [a.shape[1] // 2 for a in fulls]

    def body(*refs):
        f_refs, got_refs = refs[:n], refs[n:2 * n]
        send_sems, recv_sems = refs[2 * n:]
        x, y, c = _coords()
        copies = []
        for a in range(n):
            src = f_refs[a].at[:, pl.ds(pl.multiple_of((1 - c) * halves[a], 8), halves[a]), :]
            copies.append(pltpu.make_async_remote_copy(
                src_ref=src, dst_ref=got_refs[a], send_sem=send_sems.at[a], recv_sem=recv_sems.at[a],
                device_id=(x, y, 1 - c), device_id_type=MESH))
        for cp in copies:
            cp.start()
        for cp in copies:
            cp.wait()

    return pl.pallas_call(
        body, out_shape=[jax.ShapeDtypeStruct((a.shape[0], h, a.shape[2]), a.dtype) for a, h in zip(fulls, halves)],
        in_specs=[HBM_REF] * n, out_specs=[HBM_REF] * n,
        scratch_shapes=[pltpu.SemaphoreType.DMA((n,)), pltpu.SemaphoreType.DMA((n,))],
        name="rs_pair_swap")(*fulls)


def _scatter_to_chips(parts):
    n = len(parts)

    def body(*refs):
        p_refs, got_refs = refs[:n], refs[n:2 * n]
        send_sems, recv_sems, local_sems = refs[2 * n:]
        x, y, c = _coords()
        my_chip = 2 * x + y
        chips = [(1 - x, y), (x, 1 - y), (1 - x, 1 - y)]
        mine = [pltpu.make_async_copy(p_refs[a].at[my_chip], got_refs[a].at[my_chip], local_sems.at[a])
                for a in range(n)]
        for cp in mine:
            cp.start()

        def copy(a, j, src_slab, dst_slab):
            px, py = chips[j]
            return pltpu.make_async_remote_copy(
                src_ref=p_refs[a].at[src_slab], dst_ref=got_refs[a].at[dst_slab], send_sem=send_sems.at[3 * a + j],
                recv_sem=recv_sems.at[3 * a + j], device_id=(px, py, c), device_id_type=MESH)

        sends = [copy(a, j, 2 * px + py, my_chip) for a in range(n) for j, (px, py) in enumerate(chips)]
        for cp in sends:
            cp.start()
        for a in range(n):
            for j, (px, py) in enumerate(chips):
                copy(a, j, my_chip, 2 * px + py).wait_recv()
        for cp in sends:
            cp.wait_send()
        for cp in mine:
            cp.wait()

    return pl.pallas_call(
        body, out_shape=[jax.ShapeDtypeStruct(a.shape, a.dtype) for a in parts],
        in_specs=[HBM_REF] * n, out_specs=[HBM_REF] * n,
        scratch_shapes=[pltpu.SemaphoreType.DMA((3 * n,)), pltpu.SemaphoreType.DMA((3 * n,)),
                        pltpu.SemaphoreType.DMA((n,))],
        name="rs_chip_scatter")(*parts)


def _join_halves_with_sibling(halves):
    n = len(halves)

    def body(*refs):
        h_refs, out_refs = refs[:n], refs[n:2 * n]
        send_sems, recv_sems, local_sems = refs[2 * n:]
        x, y, c = _coords()
        mine, pushes = [], []
        for a in range(n):
            rows = halves[a].shape[0]
            my_rows = pl.ds(pl.multiple_of(c * rows, 8), rows)
            mine.append(pltpu.make_async_copy(h_refs[a], out_refs[a].at[my_rows, :], local_sems.at[a]))
            pushes.append(pltpu.make_async_remote_copy(
                src_ref=h_refs[a], dst_ref=out_refs[a].at[my_rows, :], send_sem=send_sems.at[a],
                recv_sem=recv_sems.at[a], device_id=(x, y, 1 - c), device_id_type=MESH))
        for cp in mine + pushes:
            cp.start()
        for a in range(n):
            rows = halves[a].shape[0]
            other_rows = pl.ds(pl.multiple_of((1 - c) * rows, 8), rows)
            pltpu.make_async_remote_copy(
                src_ref=h_refs[a], dst_ref=out_refs[a].at[other_rows, :], send_sem=send_sems.at[a],
                recv_sem=recv_sems.at[a], device_id=(x, y, 1 - c), device_id_type=MESH).wait_recv()
        for cp in pushes:
            cp.wait_send()
        for cp in mine:
            cp.wait()

    return pl.pallas_call(
        body, out_shape=[jax.ShapeDtypeStruct((2 * a.shape[0], a.shape[1]), a.dtype) for a in halves],
        in_specs=[HBM_REF] * n, out_specs=[HBM_REF] * n,
        scratch_shapes=[pltpu.SemaphoreType.DMA((n,)), pltpu.SemaphoreType.DMA((n,)), pltpu.SemaphoreType.DMA((n,))],
        name="rs_pair_join")(*halves)


def _cols_to_slabs(g):
    rows, cols = g.shape
    return g.reshape(rows, N_CHIP, cols // N_CHIP).transpose(1, 0, 2)


def _slabs_to_cols(w):
    n, rows, cols = w.shape
    return w.transpose(1, 0, 2).reshape(rows, n * cols)


def _uq_to_padded(w_uq):
    per = w_uq.reshape(RQ, H, DN + DR)
    nope = per[:, :, :DN].reshape(RQ, H * DN)
    rope = jnp.pad(per[:, :, DN:], ((0, 0), (0, 0), (0, LANE - DR))).reshape(RQ, H * LANE)
    return jnp.concatenate([nope, rope], axis=1)


def _uq_from_padded(g):
    nope = g[:, :H * DN].reshape(RQ, H, DN)
    rope = g[:, H * DN:].reshape(RQ, H, LANE)[:, :, :DR]
    return jnp.concatenate([nope, rope], axis=2).reshape(RQ, H * (DN + DR))


def _rope_tables(positions):
    inv_freq = ROPE_THETA ** (-jnp.arange(0, DR, 2, dtype=F32) / DR)
    ang = positions.astype(F32)[:, None] * inv_freq
    cos, sin = jnp.cos(ang), jnp.sin(ang)
    return jnp.tile(cos, (1, 4)), jnp.tile(jnp.concatenate([-sin, sin], axis=1), (1, 2))


def _local_step(x, tgt, cos_t, sin_t, mod, weights, small, tiles):
    ts, tm, t_attn, chunk = tiles
    wa, wl, wg, w_uq2, w_ukv, wco, wao, wo, conv_w = weights
    norm_w, conv_b, ln_w, ln_b, q_norm_w, kv_norm_w, fnw = small
    shift, scale, gate = mod[:, 0:D], mod[:, D:2 * D], mod[:, 2 * D:3 * D]

    h = _adaln_norm(x, norm_w, shift, scale, ts)
    proj_a = _mm_nn(h, wa, tm, D, "proj_a")
    proj_l = _mm_nn(h, wl, tm, L_COLS, "proj_l")
    proj_g = _mm_nn(h, wg, tm, D, "proj_g")
    u0, u1, za = _conv_fwd(proj_a, conv_w, conv_b, ln_w, ln_b, ts, chunk)
    qn, kvn, q, k, v = _mla_prep(proj_l, q_norm_w, kv_norm_w, w_uq2, w_ukv, cos_t, sin_t, ts)
    o, lse = _attn_fwd(q, k, v, t_attn)
    (dx2, dza, do, delta, dpg, zb, mg, dmo, dya, dyb, vec_mid) = _middle(
        za, o, proj_g, x, tgt, gate, fnw, wco, wao, wo, ts)
    g_wo = _mm_tn(mg, dmo, tm, D, "grad_w_out")
    g_wco = _mm_tn(za, dya, tm, D, "grad_w_conv_out")
    g_wao = _mm_tn(zb, dyb, tm, D, "grad_w_attn_out")
    dq, dk, dv = _attn_bwd(q, k, v, do, lse, delta, t_attn)
    dpl, g_wuq2, g_wukv, vec_mla = _mla_prep_bwd(
        dq, dk, dv, proj_l, qn, kvn, q_norm_w, kv_norm_w, w_uq2, w_ukv, cos_t, sin_t, ts)
    dpa, g_conv_w, vec_conv = _conv_bwd(dza, proj_a, u0, u1, conv_w, ln_w, ln_b, ts, chunk)
    grad_x, vec_in = _input_bwd(dpa, dpl, dpg, wa, wl, wg, x, dx2, norm_w, scale, ts)
    g_wa = _mm_tn(h, dpa, tm, D, "grad_w_in_a")
    g_wl = _mm_tn(h, dpl, tm, L_COLS, "grad_w_in_l")
    g_wg = _mm_tn(h, dpg, tm, D, "grad_w_in_g")

    dmod = jnp.concatenate([vec_in[0:1], vec_in[1:2], vec_mid[1:2]], axis=1)
    sums = dict(dmod=dmod, norm_w=vec_in[2:3], conv_b=vec_conv[2:3], ln_w=vec_conv[0:1], ln_b=vec_conv[1:2],
                q_norm_w=vec_mla[0:1], kv_norm_w=vec_mla[1:2], final_norm_w=vec_mid[0:1], loss=vec_mid[2:3],
                conv_w=g_conv_w)
    grads = dict(wa=g_wa, wl=g_wl, wg=g_wg, w_uq2=g_wuq2, w_ukv=g_wukv, wco=g_wco, wao=g_wao, wo=g_wo)
    return grad_x, grads, sums


SMALL_ORDER = (("dmod", 3 * D), ("norm_w", D), ("conv_b", D), ("ln_w", D), ("ln_b", D), ("q_norm_w", RQ),
               ("kv_norm_w", RQ), ("final_norm_w", D), ("loss", D), ("conv_w", HALO * D))
SMALL_ROWS = 336


def kernel(x, c, positions, w_ada, b_ada, norm_w, w_in, conv_w, conv_b, conv_ln_w, conv_ln_b, w_conv_out, q_norm_w, w_uq, kv_norm_w, w_ukv, w_attn_out, w_out, final_norm_w, loss_target, m_w_ada, m_b_ada, m_norm_w, m_w_in, m_conv_w, m_conv_b, m_conv_ln_w, m_conv_ln_b, m_w_conv_out, m_q_norm_w, m_w_uq, m_kv_norm_w, m_w_ukv, m_w_attn_out, m_w_out, m_final_norm_w, v_w_ada, v_b_ada, v_norm_w, v_w_in, v_conv_w, v_conv_b, v_conv_ln_w, v_conv_ln_b, v_w_conv_out, v_q_norm_w, v_w_uq, v_kv_norm_w, v_w_ukv, v_w_attn_out, v_w_out, v_final_norm_w):
    ix, iy, ic = _coords()
    chip = 2 * ix + iy
    dev = 4 * ix + 2 * iy + ic
    s = x.shape[1]
    tiles = (256, 512, 512, 32)

    conv_w_pad = jnp.pad(conv_w[0], ((0, HALO - KC), (0, 0)))
    small_in = jnp.concatenate([c.reshape(8, LANE), conv_w_pad.reshape(64, LANE)], axis=0)
    small_all = _allgather8(small_in, 72, True, "gather_c_conv").reshape(N_DEV, 72, LANE)
    c_all = small_all[:, 0:8].reshape(N_DEV, D)
    conv_full = jnp.concatenate(
        [small_all[2 * k, 8:72].reshape(HALO, D // N_CHIP) for k in range(N_CHIP)], axis=1)

    shards = [w[0].astype(BF16) for w in (w_in, w_uq, w_ukv, w_conv_out, w_attn_out, w_out)]
    g_in, g_uq, g_ukv, g_co, g_ao, g_o = _gather_weights(shards)
    w_in_f, w_uq_f, w_ukv_f = _slabs_to_cols(g_in), _slabs_to_cols(g_uq), _slabs_to_cols(g_ukv)
    wco, wao, wo = g_co.reshape(D, D), g_ao.reshape(D, D), g_o.reshape(D, D)
    wa = w_in_f[:, 0:A_COLS]
    wl = jnp.pad(w_in_f[:, A_COLS:A_COLS + L_COLS_RAW], ((0, 0), (0, L_COLS - L_COLS_RAW)))
    wg = w_in_f[:, A_COLS + L_COLS_RAW:]
    weights = (wa, wl, wg, _uq_to_padded(w_uq_f), w_ukv_f, wco, wao, wo, conv_full)

    ada_cols = w_ada.shape[2]
    b_shard = lax.dynamic_slice(b_ada, (0, chip * ada_cols), (1, ada_cols))
    mod_part = _ada_fwd(c_all, w_ada[0], b_shard)
    mod_all = _allgather8(mod_part, N_DEV, True, "gather_mod").reshape(N_DEV, N_DEV, ada_cols)
    mod = jnp.concatenate(
        [lax.dynamic_slice(mod_all[2 * k], (dev, 0), (1, ada_cols)) for k in range(N_CHIP)], axis=1)

    cos_t, sin_t = _rope_tables(positions[0])
    small = (norm_w, conv_b, conv_ln_w, conv_ln_b, q_norm_w, kv_norm_w, final_norm_w.reshape(1, D))
    grad_x, grads, sums = _local_step(x[0], loss_target[0], cos_t, sin_t, mod, weights, small, tiles)

    small_flat = jnp.concatenate([sums[name].reshape(-1) for name, _ in SMALL_ORDER])
    small_flat = jnp.pad(small_flat, (0, SMALL_ROWS * LANE - small_flat.shape[0]))
    small_g = _allgather8(small_flat.reshape(SMALL_ROWS, LANE), SMALL_ROWS, True, "gather_small_grads")
    small_g = small_g.reshape(N_DEV, SMALL_ROWS, LANE)
    small_sum = _sum_slabs(small_g, SMALL_ROWS, "sum_small_grads").reshape(-1)
    tot, pos = {}, 0
    for name, size in SMALL_ORDER:
        tot[name] = small_sum[pos:pos + size]
        pos += size
    loss = (0.5 / D) * jnp.sum(tot["loss"])
    dmod_all = small_g.reshape(N_DEV, -1)[:, 0:3 * D]
    g_b_ada = tot["dmod"].reshape(1, 3 * D)
    dmod_shard = lax.dynamic_slice(dmod_all, (0, chip * ada_cols), (N_DEV, ada_cols))
    g_w_ada = _ada_bwd(c_all.T, dmod_shard).reshape(1, D, ada_cols)
    g_conv_w = lax.dynamic_slice(tot["conv_w"].reshape(HALO, D), (0, chip * (D // N_CHIP)), (KC, D // N_CHIP))
    g_conv_w = g_conv_w.reshape(1, KC, D // N_CHIP)

    g_w_in = jnp.concatenate([grads["wa"], grads["wl"][:, 0:L_COLS_RAW], grads["wg"]], axis=1)
    nr = D // N_CHIP
    fulls = [_cols_to_slabs(g_w_in), _cols_to_slabs(_uq_from_padded(grads["w_uq2"])), _cols_to_slabs(grads["w_ukv"]),
             grads["wco"].reshape(N_CHIP, nr, D), grads["wao"].reshape(N_CHIP, nr, D),
             grads["wo"].reshape(N_CHIP, nr, D)]
    from_sibling = _swap_halves_with_sibling(fulls)
    core = ic.reshape(1).astype(jnp.int32)
    chip_sums = [_add_own_half(f, o, core, min(256, o.shape[1]), f"add_own_half_{n}")
                 for n, (f, o) in enumerate(zip(fulls, from_sibling))]
    arrived = _scatter_to_chips(chip_sums)
    my_halves = [_sum_slabs(a, min(128, a.shape[1]), f"sum_chip_slabs_{n}") for n, a in enumerate(arrived)]
    g_w_in_s, g_w_uq_s, g_w_ukv_s, g_wco_s, g_wao_s, g_wo_s = _join_halves_with_sibling(my_halves)

    def big(w, g, m, v, tr, name):
        d, nm, nv = _adamw(w[0], g, m[0], v[0], tr, name)
        return g[None], d[None], nm[None], nv[None]

    vec_names = ("b_ada", "norm_w", "conv_b", "conv_ln_w", "conv_ln_b", "q_norm_w", "kv_norm_w", "final_norm_w")
    vec_w = (b_ada, norm_w, conv_b, conv_ln_w, conv_ln_b, q_norm_w, kv_norm_w, final_norm_w)
    vec_m = (m_b_ada, m_norm_w, m_conv_b, m_conv_ln_w, m_conv_ln_b, m_q_norm_w, m_kv_norm_w, m_final_norm_w)
    vec_v = (v_b_ada, v_norm_w, v_conv_b, v_conv_ln_w, v_conv_ln_b, v_q_norm_w, v_kv_norm_w, v_final_norm_w)
    vec_g = (g_b_ada, tot["norm_w"], tot["conv_b"], tot["ln_w"], tot["ln_b"], tot["q_norm_w"], tot["kv_norm_w"],
             tot["final_norm_w"])
    vec_g = tuple(g.reshape(w.shape) for g, w in zip(vec_g, vec_w))
    cat = lambda arrs: jnp.concatenate([a.reshape(-1) for a in arrs]).reshape(-1, LANE)
    vd, vnm, vnv = _adamw(cat(vec_w), cat(vec_g), cat(vec_m), cat(vec_v), cat(vec_w).shape[0], "adamw_vectors")

    def split(packed):
        flat, out, pos = packed.reshape(-1), [], 0
        for w in vec_w:
            out.append(flat[pos:pos + w.size].reshape(w.shape))
            pos += w.size
        return out

    res = {}
    for name, g, d, nm, nv in zip(vec_names, vec_g, split(vd), split(vnm), split(vnv)):
        res[name] = (g, d, nm, nv)
    res["w_ada"] = big(w_ada, g_w_ada[0], m_w_ada, v_w_ada, 256, "adamw_w_ada")
    res["w_in"] = big(w_in, g_w_in_s, m_w_in, v_w_in, 256, "adamw_w_in")
    res["conv_w"] = big(conv_w, g_conv_w[0], m_conv_w, v_conv_w, KC, "adamw_conv_w")
    res["w_conv_out"] = big(w_conv_out, g_wco_s, m_w_conv_out, v_w_conv_out, 256, "adamw_w_conv_out")
    res["w_uq"] = big(w_uq, g_w_uq_s, m_w_uq, v_w_uq, 256, "adamw_w_uq")
    res["w_ukv"] = big(w_ukv, g_w_ukv_s, m_w_ukv, v_w_ukv, 256, "adamw_w_ukv")
    res["w_attn_out"] = big(w_attn_out, g_wao_s, m_w_attn_out, v_w_attn_out, 256, "adamw_w_attn_out")
    res["w_out"] = big(w_out, g_wo_s, m_w_out, v_w_out, 256, "adamw_w_out")

    order = ("w_ada", "b_ada", "norm_w", "w_in", "conv_w", "conv_b", "conv_ln_w", "conv_ln_b", "w_conv_out",
             "q_norm_w", "w_uq", "kv_norm_w", "w_ukv", "w_attn_out", "w_out", "final_norm_w")
    outs = [loss, grad_x[None]]
    for slot in range(4):
        outs += [res[name][slot] for name in order]
    return tuple(outs)
```

```python
import functools

import numpy as np
import jax
import jax.numpy as jnp
from jax import lax
from jax.experimental import pallas as pl
from jax.experimental.pallas import tpu as pltpu

F32 = jnp.float32
BF16 = jnp.bfloat16
MESH = pl.DeviceIdType.MESH

D = 1024
H = 8
DN = 128
DR = 64
RQ = 256
KC = 31
HALO = 32
EPS = 1e-6
ROPE_THETA = 10000.0
N_CHIP = 4
N_DEV = 8
LANE = 128
VMEM_BIG = 56 * 1024 * 1024

ADAM_LR = 0.001
ADAM_B1 = 0.9
ADAM_B2 = 0.999
ADAM_EPS = 1e-08
ADAM_WD = 0.01
ADAM_STEP = 10

A_COLS = 3 * D
L_COLS_RAW = RQ + RQ + DR
L_COLS = 640
G_COLS = 3 * D
IN_COLS = A_COLS + L_COLS_RAW + G_COLS


def _params(sem=None, vmem=None):
    kw = {}
    if sem is not None:
        kw["dimension_semantics"] = sem
    if vmem is not None:
        kw["vmem_limit_bytes"] = vmem
    return pltpu.CompilerParams(**kw)


def _dot(a, b):
    return jnp.dot(a, b, preferred_element_type=F32)


def _dot_nt(a, b):
    return lax.dot_general(a, b, (((1,), (1,)), ((), ())), preferred_element_type=F32)


def _dot_tn(a, b):
    return lax.dot_general(a, b, (((0,), (0,)), ((), ())), preferred_element_type=F32)


def _colsum(v):
    return jnp.sum(v, axis=0, keepdims=True)


def _rowmean(v):
    return jnp.mean(v, axis=-1, keepdims=True)


def _sigmoid(v):
    return jax.nn.sigmoid(v)


def _dsilu(v, s):
    return s * (1.0 + v * (1.0 - s))


def _swap_halves(v, first_half):
    return jnp.where(first_half, pltpu.roll(v, 96, 1), pltpu.roll(v, 32, 1))


def _first_half_mask(rows):
    lane = lax.broadcasted_iota(jnp.int32, (rows, LANE), 1)
    return (lane % 64) < 32


def _adaln_norm(x, norm_w, shift, scale, ts):
    s = x.shape[0]

    def body(x_ref, nw_ref, sh_ref, sc_ref, h_ref):
        xv = x_ref[...]
        r = lax.rsqrt(_rowmean(xv * xv) + EPS)
        y = xv * r * nw_ref[...]
        h_ref[...] = (y * (1.0 + sc_ref[...]) + sh_ref[...]).astype(BF16)

    row = pl.BlockSpec((ts, D), lambda i: (i, 0))
    vec = pl.BlockSpec((1, D), lambda i: (0, 0))
    return pl.pallas_call(
        body, grid=(s // ts,), in_specs=[row, vec, vec, vec], out_specs=row,
        out_shape=jax.ShapeDtypeStruct((s, D), BF16), name="adaln_norm",
        compiler_params=_params(("parallel",)))(x, norm_w, shift, scale)


def _mm_nn(a, b, tm, tn, name):
    m, k = a.shape
    n = b.shape[1]

    def body(a_ref, b_ref, o_ref):
        o_ref[...] = _dot(a_ref[...], b_ref[...])

    return pl.pallas_call(
        body, grid=(n // tn, m // tm),
        in_specs=[pl.BlockSpec((tm, k), lambda j, i: (i, 0)), pl.BlockSpec((k, tn), lambda j, i: (0, j))],
        out_specs=pl.BlockSpec((tm, tn), lambda j, i: (i, j)),
        out_shape=jax.ShapeDtypeStruct((m, n), F32), name=name,
        compiler_params=_params(("parallel", "parallel")))(a, b)


def _mm_tn(a, b, tm, tn, name):
    m, k = a.shape
    n = b.shape[1]

    def body(a_ref, b_ref, o_ref):
        @pl.when(pl.program_id(1) == 0)
        def _():
            o_ref[...] = jnp.zeros_like(o_ref)
        o_ref[...] += _dot_tn(a_ref[...], b_ref[...])

    return pl.pallas_call(
        body, grid=(n // tn, m // tm),
        in_specs=[pl.BlockSpec((tm, k), lambda j, i: (i, 0)), pl.BlockSpec((tm, tn), lambda j, i: (i, j))],
        out_specs=pl.BlockSpec((k, tn), lambda j, i: (0, j)),
        out_shape=jax.ShapeDtypeStruct((k, n), F32), name=name,
        compiler_params=_params(("parallel", "arbitrary")))(a, b)


def _shifted_copies(win_ref, sh_ref, rows):
    for p in range(1, 8):
        sh_ref[p - 1, 0:rows, :] = win_ref[pl.ds(p, rows), :]


def _tap_rows(win_ref, sh_ref, start, rows):
    p = start % 8
    if p == 0:
        return win_ref[pl.ds(start, rows), :]
    return sh_ref[p - 1, pl.ds(start - p, rows), :]


def _conv_taps(win_ref, sh_ref, w_ref, rows, chunk, offset_of_tap):
    pieces = []
    for c0 in range(0, rows, chunk):
        acc = None
        for j in range(KC):
            term = w_ref[j:j + 1, :] * _tap_rows(win_ref, sh_ref, c0 + offset_of_tap(j), chunk)
            acc = term if acc is None else acc + term
        pieces.append(acc)
    return pieces


def _conv_fwd(proj_a, conv_w, conv_b, ln_w, ln_b, ts, chunk):
    s = proj_a.shape[0]

    def body(av_ref, al_ref, ag_ref, w_ref, b_ref, lw_ref, lb_ref, u0_ref, u1_ref, za_ref, win_ref, sh_ref):
        @pl.when(pl.program_id(0) == 0)
        def _():
            win_ref[0:HALO, :] = jnp.zeros((HALO, D), F32)

        u0 = av_ref[...] * _sigmoid(al_ref[...])
        u0_ref[...] = u0
        win_ref[HALO:HALO + ts, :] = u0
        _shifted_copies(win_ref, sh_ref, ts + HALO - 8)
        pieces = _conv_taps(win_ref, sh_ref, w_ref, ts, chunk, lambda j: HALO - (KC - 1) + j)
        for n, acc in enumerate(pieces):
            u1_ref[n * chunk:(n + 1) * chunk, :] = acc + b_ref[...]
        win_ref[0:HALO, :] = win_ref[ts:ts + HALO, :]

        u1 = u1_ref[...]
        xc = u1 - _rowmean(u1)
        rstd = lax.rsqrt(_rowmean(xc * xc) + EPS)
        u2 = xc * rstd * lw_ref[...] + lb_ref[...]
        u3 = u2 * _sigmoid(u2)
        ag = ag_ref[...]
        za_ref[...] = (u3 * (ag * _sigmoid(ag))).astype(BF16)

    col = lambda c: pl.BlockSpec((ts, D), lambda i, c=c: (i, c))
    row = pl.BlockSpec((ts, D), lambda i: (i, 0))
    vec = pl.BlockSpec((1, D), lambda i: (0, 0))
    return pl.pallas_call(
        body, grid=(s // ts,),
        in_specs=[col(0), col(1), col(2), pl.BlockSpec((HALO, D), lambda i: (0, 0)), vec, vec, vec],
        out_specs=[row, row, row],
        out_shape=[jax.ShapeDtypeStruct((s, D), F32), jax.ShapeDtypeStruct((s, D), F32),
                   jax.ShapeDtypeStruct((s, D), BF16)],
        scratch_shapes=[pltpu.VMEM((ts + HALO, D), F32), pltpu.VMEM((7, ts + HALO, D), F32)], name="conv_fwd",
        compiler_params=_params(("arbitrary",), VMEM_BIG))(proj_a, proj_a, proj_a, conv_w, conv_b, ln_w, ln_b)


def _conv_bwd(dza, proj_a, u0, u1, conv_w, ln_w, ln_b, ts, chunk):
    s = dza.shape[0]
    nt = s // ts
    per = ts // HALO

    def body(dza_ref, av_ref, al_ref, ag_ref, u0_ref, u0p_ref, u1_ref, w_ref, lw_ref, lb_ref,
             dpa_ref, gw_ref, gv_ref, dwin_ref, uwin_ref, du0_ref, gwp_ref, dsh_ref, ush_ref):
        step = pl.program_id(0)
        tile = nt - 1 - step

        @pl.when(step == 0)
        def _():
            dwin_ref[ts:ts + HALO, :] = jnp.zeros((HALO, D), F32)
            gwp_ref[...] = jnp.zeros_like(gwp_ref)
            gv_ref[...] = jnp.zeros_like(gv_ref)

        ag = ag_ref[...]
        sg = _sigmoid(ag)
        u1 = u1_ref[...]
        xc = u1 - _rowmean(u1)
        rstd = lax.rsqrt(_rowmean(xc * xc) + EPS)
        xh = xc * rstd
        u2 = xh * lw_ref[...] + lb_ref[...]
        s2 = _sigmoid(u2)
        dz = dza_ref[...]
        du3 = dz * (ag * sg)
        dpa_ref[:, 2 * D:3 * D] = (dz * (u2 * s2) * _dsilu(ag, sg)).astype(BF16)
        du2 = du3 * _dsilu(u2, s2)
        gv_ref[0:1, :] += _colsum(du2 * xh)
        gv_ref[1:2, :] += _colsum(du2)
        dxh = du2 * lw_ref[...]
        du1 = rstd * (dxh - _rowmean(dxh) - xh * _rowmean(dxh * xh))
        gv_ref[2:3, :] += _colsum(du1)
        dwin_ref[0:ts, :] = du1

        uwin_ref[0:HALO, :] = jnp.where(tile == 0, 0.0, u0p_ref[...])
        uwin_ref[HALO:HALO + ts, :] = u0_ref[...]

        _shifted_copies(dwin_ref, dsh_ref, ts + HALO - 8)
        _shifted_copies(uwin_ref, ush_ref, ts + HALO - 8)
        pieces = _conv_taps(dwin_ref, dsh_ref, w_ref, ts, chunk, lambda j: (KC - 1) - j)
        for n, acc in enumerate(pieces):
            du0_ref[n * chunk:(n + 1) * chunk, :] = acc
        for c0 in range(0, ts, chunk):
            dchunk = dwin_ref[c0:c0 + chunk, :]
            for j in range(KC):
                prod = dchunk * _tap_rows(uwin_ref, ush_ref, c0 + HALO - (KC - 1) + j, chunk)
                gwp_ref[8 * j:8 * j + 8, :] += jnp.sum(prod.reshape(chunk // 8, 8, D), axis=0)
        dwin_ref[ts:ts + HALO, :] = dwin_ref[0:HALO, :]

        du0 = du0_ref[...]
        al = al_ref[...]
        sl = _sigmoid(al)
        dpa_ref[:, 0:D] = (du0 * sl).astype(BF16)
        dpa_ref[:, D:2 * D] = (du0 * av_ref[...] * sl * (1.0 - sl)).astype(BF16)

        @pl.when(step == nt - 1)
        def _():
            for j in range(KC):
                gw_ref[j:j + 1, :] = _colsum(gwp_ref[8 * j:8 * j + 8, :])
            gw_ref[KC:HALO, :] = jnp.zeros((HALO - KC, D), F32)

    rev = lambda i: nt - 1 - i
    col = lambda c: pl.BlockSpec((ts, D), lambda i, c=c: (rev(i), c))
    row = pl.BlockSpec((ts, D), lambda i: (rev(i), 0))
    vec = pl.BlockSpec((1, D), lambda i: (0, 0))
    halo = pl.BlockSpec((HALO, D), lambda i: (jnp.maximum(rev(i) * per - 1, 0), 0))
    return pl.pallas_call(
        body, grid=(nt,),
        in_specs=[row, col(0), col(1), col(2), row, halo, row, pl.BlockSpec((HALO, D), lambda i: (0, 0)), vec, vec],
        out_specs=[pl.BlockSpec((ts, A_COLS), lambda i: (rev(i), 0)),
                   pl.BlockSpec((HALO, D), lambda i: (0, 0)), pl.BlockSpec((8, D), lambda i: (0, 0))],
        out_shape=[jax.ShapeDtypeStruct((s, A_COLS), BF16), jax.ShapeDtypeStruct((HALO, D), F32),
                   jax.ShapeDtypeStruct((8, D), F32)],
        scratch_shapes=[pltpu.VMEM((ts + HALO, D), F32), pltpu.VMEM((ts + HALO, D), F32),
                        pltpu.VMEM((ts, D), F32), pltpu.VMEM((8 * HALO, D), F32),
                        pltpu.VMEM((7, ts + HALO, D), F32), pltpu.VMEM((7, ts + HALO, D), F32)],
        name="conv_bwd", compiler_params=_params(("arbitrary",), VMEM_BIG))(
            dza, proj_a, proj_a, proj_a, u0, u0, u1, conv_w, ln_w, ln_b)


def _mla_prep(proj_l, q_norm_w, kv_norm_w, w_uq2, w_ukv, cos_t, sin_t, ts):
    s = proj_l.shape[0]

    def body(pl_ref, qw_ref, kw_ref, wq_ref, wkv_ref, c_ref, s_ref, qn_ref, kvn_ref, q_ref, k_ref, v_ref):
        first = _first_half_mask(ts)
        cs = c_ref[...]
        sn = s_ref[...]

        def rms(v, w):
            return v * lax.rsqrt(_rowmean(v * v) + EPS) * w

        def rope(v):
            return v * cs + _swap_halves(v, first) * sn

        qn = rms(pl_ref[:, 0:RQ], qw_ref[...]).astype(BF16)
        kvn = rms(pl_ref[:, RQ:2 * RQ], kw_ref[...]).astype(BF16)
        qn_ref[...] = qn
        kvn_ref[...] = kvn
        q = _dot(qn, wq_ref[...])
        kv = _dot(kvn, wkv_ref[...])
        kr = rope(pl_ref[:, 2 * RQ:2 * RQ + LANE]).astype(BF16)
        for h in range(H):
            q_ref[h, :, 0:DN] = q[:, DN * h:DN * (h + 1)].astype(BF16)
            q_ref[h, :, DN:2 * DN] = rope(q[:, H * DN + LANE * h:H * DN + LANE * (h + 1)]).astype(BF16)
            k_ref[h, :, 0:DN] = kv[:, 2 * DN * h:2 * DN * h + DN].astype(BF16)
            k_ref[h, :, DN:2 * DN] = kr
            v_ref[h, :, 0:DN] = kv[:, 2 * DN * h + DN:2 * DN * (h + 1)].astype(BF16)
            v_ref[h, :, DN:2 * DN] = jnp.ones((ts, DN), BF16)

    const = lambda shape: pl.BlockSpec(shape, lambda i: (0,) * len(shape))
    rowb = lambda w: pl.BlockSpec((ts, w), lambda i: (i, 0))
    head = lambda w: pl.BlockSpec((H, ts, w), lambda i: (0, i, 0))
    return pl.pallas_call(
        body, grid=(s // ts,),
        in_specs=[rowb(L_COLS), const((1, RQ)), const((1, RQ)), const((RQ, 2 * H * DN)), const((RQ, 2 * H * DN)),
                  rowb(LANE), rowb(LANE)],
        out_specs=[rowb(RQ), rowb(RQ), head(2 * DN), head(2 * DN), head(2 * DN)],
        out_shape=[jax.ShapeDtypeStruct((s, RQ), BF16), jax.ShapeDtypeStruct((s, RQ), BF16),
                   jax.ShapeDtypeStruct((H, s, 2 * DN), BF16), jax.ShapeDtypeStruct((H, s, 2 * DN), BF16),
                   jax.ShapeDtypeStruct((H, s, 2 * DN), BF16)],
        name="mla_prep", compiler_params=_params(("parallel",)))(
            proj_l, q_norm_w, kv_norm_w, w_uq2, w_ukv, cos_t, sin_t)


def _mla_prep_bwd(dq, dk, dv, proj_l, qn, kvn, q_norm_w, kv_norm_w, w_uq2, w_ukv, cos_t, sin_t, ts):
    s = proj_l.shape[0]

    def body(dq_ref, dk_ref, dv_ref, pl_ref, qn_ref, kvn_ref, qw_ref, kw_ref, wq_ref, wkv_ref, c_ref, s_ref,
             dpl_ref, gwq_ref, gwkv_ref, gv_ref, dq2_ref, dkv2_ref):
        @pl.when(pl.program_id(0) == 0)
        def _():
            gwq_ref[...] = jnp.zeros_like(gwq_ref)
            gwkv_ref[...] = jnp.zeros_like(gwkv_ref)
            gv_ref[...] = jnp.zeros_like(gv_ref)

        first = _first_half_mask(ts)
        cs = c_ref[...]
        sn = s_ref[...]

        def rope_bwd(g):
            return g * cs + _swap_halves(g * sn, first)

        def rms_bwd(v, w, dy):
            r = lax.rsqrt(_rowmean(v * v) + EPS)
            vh = v * r
            dvh = dy * w
            return r * (dvh - vh * _rowmean(dvh * vh)), _colsum(dy * vh)

        dkr = None
        for h in range(H):
            dq2_ref[:, DN * h:DN * (h + 1)] = dq_ref[h, :, 0:DN].astype(BF16)
            dq2_ref[:, H * DN + LANE * h:H * DN + LANE * (h + 1)] = rope_bwd(dq_ref[h, :, DN:2 * DN]).astype(BF16)
            dkv2_ref[:, 2 * DN * h:2 * DN * h + DN] = dk_ref[h, :, 0:DN].astype(BF16)
            dkv2_ref[:, 2 * DN * h + DN:2 * DN * (h + 1)] = dv_ref[h].astype(BF16)
            part = dk_ref[h, :, DN:2 * DN]
            dkr = part if dkr is None else dkr + part

        dq2 = dq2_ref[...]
        dkv2 = dkv2_ref[...]
        gwq_ref[...] += _dot_tn(qn_ref[...], dq2)
        gwkv_ref[...] += _dot_tn(kvn_ref[...], dkv2)
        dcq, gq = rms_bwd(pl_ref[:, 0:RQ], qw_ref[...], _dot_nt(dq2, wq_ref[...]))
        dckv, gkv = rms_bwd(pl_ref[:, RQ:2 * RQ], kw_ref[...], _dot_nt(dkv2, wkv_ref[...]))
        gv_ref[0:1, :] += gq
        gv_ref[1:2, :] += gkv
        dpl_ref[:, 0:RQ] = dcq.astype(BF16)
        dpl_ref[:, RQ:2 * RQ] = dckv.astype(BF16)
        dpl_ref[:, 2 * RQ:2 * RQ + LANE] = rope_bwd(dkr).astype(BF16)

    const = lambda shape: pl.BlockSpec(shape, lambda i: (0,) * len(shape))
    rowb = lambda w: pl.BlockSpec((ts, w), lambda i: (i, 0))
    head = lambda w: pl.BlockSpec((H, ts, w), lambda i: (0, i, 0))
    return pl.pallas_call(
        body, grid=(s // ts,),
        in_specs=[head(2 * DN), head(2 * DN), head(DN), rowb(L_COLS), rowb(RQ), rowb(RQ), const((1, RQ)),
                  const((1, RQ)), const((RQ, 2 * H * DN)), const((RQ, 2 * H * DN)), rowb(LANE), rowb(LANE)],
        out_specs=[rowb(L_COLS), const((RQ, 2 * H * DN)), const((RQ, 2 * H * DN)), const((8, RQ))],
        out_shape=[jax.ShapeDtypeStruct((s, L_COLS), BF16), jax.ShapeDtypeStruct((RQ, 2 * H * DN), F32),
                   jax.ShapeDtypeStruct((RQ, 2 * H * DN), F32), jax.ShapeDtypeStruct((8, RQ), F32)],
        scratch_shapes=[pltpu.VMEM((ts, 2 * H * DN), BF16), pltpu.VMEM((ts, 2 * H * DN), BF16)],
        name="mla_prep_bwd", compiler_params=_params(("arbitrary",), VMEM_BIG))(
            dq, dk, dv, proj_l, qn, kvn, q_norm_w, kv_norm_w, w_uq2, w_ukv, cos_t, sin_t)


def _causal_pairs(n, by_key):
    if by_key:
        pairs = [(i, j) for j in range(n) for i in range(j, n)]
    else:
        pairs = [(i, j) for i in range(n) for j in range(i + 1)]
    return (jnp.asarray(np.array([p[0] for p in pairs], np.int32)),
            jnp.asarray(np.array([p[1] for p in pairs], np.int32)))


ATT_HEADS = 2
ATT_ROWS = 64


def _diag_width(r0, t):
    return min(t, -(-(r0 + ATT_ROWS) // LANE) * LANE)


def _diag_mask_rows(r0, width):
    rows = r0 + lax.broadcasted_iota(jnp.int32, (ATT_ROWS, width), 0)
    cols = lax.broadcasted_iota(jnp.int32, (ATT_ROWS, width), 1)
    return cols <= rows


def _diag_mask(t):
    return lax.broadcasted_iota(jnp.int32, (t, t), 1) <= lax.broadcasted_iota(jnp.int32, (t, t), 0)


def _attn_fwd(q, k, v, t):
    s = q.shape[1]
    n = s // t
    scale = float((DN + DR) ** -0.5)
    qi, ki = _causal_pairs(n, by_key=False)

    def body(qi_ref, ki_ref, q_ref, k_ref, v_ref, o_ref, lse_ref, m_sc, acc_sc, s_sc, p_sc):
        p = pl.program_id(1)
        i = qi_ref[p]
        j = ki_ref[p]

        @pl.when(j == 0)
        def _():
            m_sc[...] = jnp.full_like(m_sc, -jnp.inf)
            acc_sc[...] = jnp.zeros_like(acc_sc)

        def step(diag):
            for h in range(ATT_HEADS):
                sc = _dot_nt(q_ref[h], k_ref[h])
                if diag:
                    sc = jnp.where(_diag_mask(t), sc, -jnp.inf)
                s_sc[h] = sc
            for h in range(ATT_HEADS):
                m_prev = m_sc[h]
                m_new = jnp.maximum(m_prev, jnp.max(s_sc[h], axis=-1, keepdims=True) * scale)
                m_sc[h] = m_new
                acc_sc[h] = jnp.exp(m_prev - m_new) * acc_sc[h]
                for r0 in range(0, t, ATT_ROWS):
                    rows = slice(r0, r0 + ATT_ROWS)
                    p_sc[h, rows, :] = jnp.exp(s_sc[h, rows, :] * scale - m_new[rows]).astype(BF16)
            for h in range(ATT_HEADS):
                acc_sc[h] += _dot(p_sc[h], v_ref[h])

        @pl.when(j < i)
        def _():
            step(False)

        @pl.when(j == i)
        def _():
            step(True)
            for h in range(ATT_HEADS):
                l = acc_sc[h, :, DN:2 * DN]
                o_ref[:, DN * h:DN * (h + 1)] = acc_sc[h, :, 0:DN] / l
                lse_ref[h] = m_sc[h] + jnp.log(l[:, 0:1])

    hb = ATT_HEADS
    grid_spec = pltpu.PrefetchScalarGridSpec(
        num_scalar_prefetch=2, grid=(H // hb, int(qi.shape[0])),
        in_specs=[pl.BlockSpec((hb, t, 2 * DN), lambda h, p, qi, ki: (h, qi[p], 0)),
                  pl.BlockSpec((hb, t, 2 * DN), lambda h, p, qi, ki: (h, ki[p], 0)),
                  pl.BlockSpec((hb, t, 2 * DN), lambda h, p, qi, ki: (h, ki[p], 0))],
        out_specs=[pl.BlockSpec((t, hb * DN), lambda h, p, qi, ki: (qi[p], h)),
                   pl.BlockSpec((hb, t, 1), lambda h, p, qi, ki: (h, qi[p], 0))],
        scratch_shapes=[pltpu.VMEM((hb, t, 1), F32), pltpu.VMEM((hb, t, 2 * DN), F32),
                        pltpu.VMEM((hb, t, t), F32), pltpu.VMEM((hb, t, t), BF16)])
    return pl.pallas_call(
        body, grid_spec=grid_spec,
        out_shape=[jax.ShapeDtypeStruct((s, H * DN), F32), jax.ShapeDtypeStruct((H, s, 1), F32)],
        name="attn_fwd", compiler_params=_params(("parallel", "arbitrary"), VMEM_BIG))(qi, ki, q, k, v)


def _attn_bwd(q, k, v, do, lse, delta, t):
    s = q.shape[1]
    n = s // t
    scale = float((DN + DR) ** -0.5)
    qi, ki = _causal_pairs(n, by_key=True)

    def body(qi_ref, ki_ref, q_ref, k_ref, v_ref, do_ref, lse_ref, dl_ref, dq_ref, dk_ref, dv_ref,
             dk_sc, dv_sc, s_sc, dp_sc, p_sc, ds_sc):
        p = pl.program_id(1)
        i = qi_ref[p]
        j = ki_ref[p]

        @pl.when(p == 0)
        def _():
            dq_ref[...] = jnp.zeros_like(dq_ref)

        @pl.when(i == j)
        def _():
            dk_sc[...] = jnp.zeros_like(dk_sc)
            dv_sc[...] = jnp.zeros_like(dv_sc)

        def step(diag):
            for h in range(ATT_HEADS):
                s_sc[h] = _dot_nt(q_ref[h], k_ref[h])
                dp_sc[h] = _dot_nt(do_ref[:, DN * h:DN * (h + 1)], v_ref[h, :, 0:DN])
            for h in range(ATT_HEADS):
                for r0 in range(0, t, ATT_ROWS):
                    rows = slice(r0, r0 + ATT_ROWS)
                    width = _diag_width(r0, t) if diag else t
                    sc = s_sc[h, rows, 0:width] * scale
                    if diag:
                        sc = jnp.where(_diag_mask_rows(r0, width), sc, -jnp.inf)
                    pr = jnp.exp(sc - lse_ref[h, rows, :])
                    ds = pr * (dp_sc[h, rows, 0:width] - dl_ref[h, rows, :]) * scale
                    p_sc[h, rows, 0:width] = pr.astype(BF16)
                    ds_sc[h, rows, 0:width] = ds.astype(BF16)
                    if width < t:
                        p_sc[h, rows, width:t] = jnp.zeros((ATT_ROWS, t - width), BF16)
                        ds_sc[h, rows, width:t] = jnp.zeros((ATT_ROWS, t - width), BF16)
            q_rows = pl.ds(pl.multiple_of(i * t, t), t)
            for h in range(ATT_HEADS):
                dv_sc[h] += _dot_tn(p_sc[h], do_ref[:, DN * h:DN * (h + 1)])
                dk_sc[h] += _dot_tn(ds_sc[h], q_ref[h])
                dq_ref[h, q_rows, :] += _dot(ds_sc[h], k_ref[h])

        @pl.when(i > j)
        def _():
            step(False)

        @pl.when(i == j)
        def _():
            step(True)

        @pl.when(i == n - 1)
        def _():
            dk_ref[...] = dk_sc[...]
            dv_ref[...] = dv_sc[...]

    hb = ATT_HEADS
    grid_spec = pltpu.PrefetchScalarGridSpec(
        num_scalar_prefetch=2, grid=(H // hb, int(qi.shape[0])),
        in_specs=[pl.BlockSpec((hb, t, 2 * DN), lambda h, p, qi, ki: (h, qi[p], 0)),
                  pl.BlockSpec((hb, t, 2 * DN), lambda h, p, qi, ki: (h, ki[p], 0)),
                  pl.BlockSpec((hb, t, 2 * DN), lambda h, p, qi, ki: (h, ki[p], 0)),
                  pl.BlockSpec((t, hb * DN), lambda h, p, qi, ki: (qi[p], h)),
                  pl.BlockSpec((hb, t, 1), lambda h, p, qi, ki: (h, qi[p], 0)),
                  pl.BlockSpec((hb, t, 1), lambda h, p, qi, ki: (h, qi[p], 0))],
        out_specs=[pl.BlockSpec((hb, s, 2 * DN), lambda h, p, qi, ki: (h, 0, 0)),
                   pl.BlockSpec((hb, t, 2 * DN), lambda h, p, qi, ki: (h, ki[p], 0)),
                   pl.BlockSpec((hb, t, DN), lambda h, p, qi, ki: (h, ki[p], 0))],
        scratch_shapes=[pltpu.VMEM((hb, t, 2 * DN), F32), pltpu.VMEM((hb, t, DN), F32),
                        pltpu.VMEM((hb, t, t), F32), pltpu.VMEM((hb, t, t), F32),
                        pltpu.VMEM((hb, t, t), BF16), pltpu.VMEM((hb, t, t), BF16)])
    return pl.pallas_call(
        body, grid_spec=grid_spec,
        out_shape=[jax.ShapeDtypeStruct((H, s, 2 * DN), F32), jax.ShapeDtypeStruct((H, s, 2 * DN), F32),
                   jax.ShapeDtypeStruct((H, s, DN), F32)],
        name="attn_bwd", compiler_params=_params(("parallel", "arbitrary"), VMEM_BIG))(
            qi, ki, q, k, v, do, lse, delta)


def _middle(za, o, proj_g, x, tgt, gate, fnw, wco, wao, wo, ts):
    s = x.shape[0]
    inv_d = 1.0 / D

    def body(za_ref, o_ref, bg_ref, ga_ref, gb_ref, x_ref, t_ref, gate_ref, fnw_ref, wco_ref, wao_ref, wo_ref,
             dx2_ref, dza_ref, do_ref, dl_ref, dpg_ref, zb_ref, mg_ref, dmo_ref, dya_ref, dyb_ref, vec_ref):
        @pl.when(pl.program_id(0) == 0)
        def _():
            vec_ref[...] = jnp.zeros_like(vec_ref)

        ov = o_ref[...]
        bg = bg_ref[...]
        sb = _sigmoid(bg)
        silu_b = bg * sb
        zb = (ov * silu_b).astype(BF16)
        zb_ref[...] = zb
        ya = _dot(za_ref[...], wco_ref[...])
        yb = _dot(zb, wao_ref[...])
        sa = _sigmoid(ga_ref[...])
        sg = _sigmoid(gb_ref[...])
        mg = (sa * ya + sg * yb).astype(BF16)
        mg_ref[...] = mg
        mo = _dot(mg, wo_ref[...])
        gate_v = gate_ref[...]
        x2 = x_ref[...] + gate_v * mo
        r = lax.rsqrt(_rowmean(x2 * x2) + EPS)
        xh = x2 * r
        fw = fnw_ref[...]
        e = xh * fw - t_ref[...]
        vec_ref[2:3, :] += _colsum(e * e)
        dy = e * inv_d
        vec_ref[0:1, :] += _colsum(dy * xh)
        dxh = dy * fw
        dx2 = r * (dxh - xh * _rowmean(dxh * xh))
        dx2_ref[...] = dx2
        vec_ref[1:2, :] += _colsum(dx2 * mo)
        dmo = (gate_v * dx2).astype(BF16)
        dmo_ref[...] = dmo
        dmg = _dot_nt(dmo, wo_ref[...])
        dya = (sa * dmg).astype(BF16)
        dyb = (sg * dmg).astype(BF16)
        dya_ref[...] = dya
        dyb_ref[...] = dyb
        dpg_ref[:, D:2 * D] = (dmg * ya * (sa * (1.0 - sa))).astype(BF16)
        dpg_ref[:, 2 * D:3 * D] = (dmg * yb * (sg * (1.0 - sg))).astype(BF16)
        dza_ref[...] = _dot_nt(dya, wco_ref[...])
        dzb = _dot_nt(dyb, wao_ref[...])
        dov = dzb * silu_b
        do_ref[...] = dov.astype(BF16)
        dpg_ref[:, 0:D] = (dzb * ov * _dsilu(bg, sb)).astype(BF16)
        dprod = dov * ov
        for h in range(H):
            dl_ref[h] = jnp.sum(dprod[:, DN * h:DN * (h + 1)], axis=-1, keepdims=True)

    col = lambda c: pl.BlockSpec((ts, D), lambda i, c=c: (i, c))
    row = pl.BlockSpec((ts, D), lambda i: (i, 0))
    vec = pl.BlockSpec((1, D), lambda i: (0, 0))
    wsp = pl.BlockSpec((D, D), lambda i: (0, 0))
    bf = jax.ShapeDtypeStruct((s, D), BF16)
    ff = jax.ShapeDtypeStruct((s, D), F32)
    return pl.pallas_call(
        body, grid=(s // ts,),
        in_specs=[row, row, col(0), col(1), col(2), row, row, vec, vec, wsp, wsp, wsp],
        out_specs=[row, row, row, pl.BlockSpec((H, ts, 1), lambda i: (0, i, 0)),
                   pl.BlockSpec((ts, G_COLS), lambda i: (i, 0)), row, row, row, row, row,
                   pl.BlockSpec((8, D), lambda i: (0, 0))],
        out_shape=[ff, ff, bf, jax.ShapeDtypeStruct((H, s, 1), F32), jax.ShapeDtypeStruct((s, G_COLS), BF16),
                   bf, bf, bf, bf, bf, jax.ShapeDtypeStruct((8, D), F32)],
        name="middle", compiler_params=_params(("arbitrary",), VMEM_BIG))(
            za, o, proj_g, proj_g, proj_g, x, tgt, gate, fnw, wco, wao, wo)


def _input_bwd(dpa, dpl, dpg, wa, wl, wg, x, dx2, norm_w, scale, ts):
    s = x.shape[0]

    def body(dpa_ref, dpl_ref, dpg_ref, wa_ref, wl_ref, wg_ref, x_ref, dx2_ref, nw_ref, sc_ref, gx_ref, gv_ref):
        @pl.when(pl.program_id(0) == 0)
        def _():
            gv_ref[...] = jnp.zeros_like(gv_ref)

        dh = (_dot_nt(dpa_ref[...], wa_ref[...]) + _dot_nt(dpl_ref[...], wl_ref[...])
              + _dot_nt(dpg_ref[...], wg_ref[...]))
        xv = x_ref[...]
        r = lax.rsqrt(_rowmean(xv * xv) + EPS)
        xh = xv * r
        nw = nw_ref[...]
        gv_ref[0:1, :] += _colsum(dh)
        gv_ref[1:2, :] += _colsum(dh * (xh * nw))
        dy = dh * (1.0 + sc_ref[...])
        gv_ref[2:3, :] += _colsum(dy * xh)
        dxh = dy * nw
        gx_ref[...] = dx2_ref[...] + r * (dxh - xh * _rowmean(dxh * xh))

    const = lambda shape: pl.BlockSpec(shape, lambda i: (0, 0))
    rowb = lambda w: pl.BlockSpec((ts, w), lambda i: (i, 0))
    return pl.pallas_call(
        body, grid=(s // ts,),
        in_specs=[rowb(A_COLS), rowb(L_COLS), rowb(G_COLS), const((D, A_COLS)), const((D, L_COLS)),
                  const((D, G_COLS)), rowb(D), rowb(D), const((1, D)), const((1, D))],
        out_specs=[rowb(D), const((8, D))],
        out_shape=[jax.ShapeDtypeStruct((s, D), F32), jax.ShapeDtypeStruct((8, D), F32)],
        name="input_bwd", compiler_params=_params(("arbitrary",), VMEM_BIG))(
            dpa, dpl, dpg, wa, wl, wg, x, dx2, norm_w, scale)


def _adamw(w, g, m, v, tr, name):
    rows, cols = w.shape
    c1 = 1.0 - ADAM_B1 ** ADAM_STEP
    c2 = 1.0 - ADAM_B2 ** ADAM_STEP

    def body(w_ref, g_ref, m_ref, v_ref, d_ref, nm_ref, nv_ref):
        gv = g_ref[...]
        nm = ADAM_B1 * m_ref[...] + (1.0 - ADAM_B1) * gv
        nv = ADAM_B2 * v_ref[...] + (1.0 - ADAM_B2) * (gv * gv)
        nm_ref[...] = nm
        nv_ref[...] = nv
        d_ref[...] = -ADAM_LR * ((nm / c1) / (jnp.sqrt(nv / c2) + ADAM_EPS) + ADAM_WD * w_ref[...])

    blk = pl.BlockSpec((tr, cols), lambda i: (i, 0))
    shp = jax.ShapeDtypeStruct((rows, cols), F32)
    return pl.pallas_call(
        body, grid=(rows // tr,), in_specs=[blk] * 4, out_specs=[blk] * 3, out_shape=[shp] * 3, name=name,
        compiler_params=_params(("parallel",)))(w, g, m, v)


def _ada_fwd(c_all, w_ada_shard, b_ada_shard):
    def body(c_ref, w_ref, b_ref, o_ref):
        cv = c_ref[...]
        o_ref[...] = jnp.dot(cv * _sigmoid(cv), w_ref[...], preferred_element_type=F32,
                             precision=lax.Precision.HIGHEST) + b_ref[...]

    return pl.pallas_call(
        body, out_shape=jax.ShapeDtypeStruct((N_DEV, w_ada_shard.shape[1]), F32), name="ada_fwd")(
            c_all, w_ada_shard, b_ada_shard)


def _ada_bwd(c_all_t, dmod_shard):
    def body(c_ref, d_ref, o_ref):
        cv = c_ref[...]
        o_ref[...] = jnp.dot(cv * _sigmoid(cv), d_ref[...], preferred_element_type=F32,
                             precision=lax.Precision.HIGHEST)

    return pl.pallas_call(
        body, out_shape=jax.ShapeDtypeStruct((D, dmod_shard.shape[1]), F32), name="ada_bwd")(c_all_t, dmod_shard)


def _sum_slabs(stack, tr, name):
    n, rows, cols = stack.shape

    def body(s_ref, o_ref):
        acc = s_ref[0]
        for k in range(1, n):
            acc = acc + s_ref[k]
        o_ref[...] = acc

    return pl.pallas_call(
        body, grid=(rows // tr,), in_specs=[pl.BlockSpec((n, tr, cols), lambda i: (0, i, 0))],
        out_specs=pl.BlockSpec((tr, cols), lambda i: (i, 0)), out_shape=jax.ShapeDtypeStruct((rows, cols), F32),
        name=name, compiler_params=_params(("parallel",)))(stack)


def _add_own_half(full, other, core, tr, name):
    n, rows, cols = other.shape
    per = rows // tr

    def body(c_ref, f_ref, o_ref, out_ref):
        out_ref[...] = f_ref[...] + o_ref[...]

    grid_spec = pltpu.PrefetchScalarGridSpec(
        num_scalar_prefetch=1, grid=(n, per),
        in_specs=[pl.BlockSpec((1, tr, cols), lambda k, i, c: (k, c[0] * per + i, 0)),
                  pl.BlockSpec((1, tr, cols), lambda k, i, c: (k, i, 0))],
        out_specs=pl.BlockSpec((1, tr, cols), lambda k, i, c: (k, i, 0)))
    return pl.pallas_call(
        body, grid_spec=grid_spec, out_shape=jax.ShapeDtypeStruct((n, rows, cols), F32), name=name,
        compiler_params=_params(("parallel", "parallel")))(core, full, other)


def _coords():
    return lax.axis_index("x"), lax.axis_index("y"), lax.axis_index("c")


def _allgather8(block, src_rows, vmem, name):
    n = block.shape[1]
    m = src_rows
    sliced = block.shape[0] != m

    def body(x_ref, out_ref, send_sems, recv_sems, local_sem):
        x, y, c = _coords()
        me, sibling = (x, y, c), (x, y, 1 - c)
        chips = [(1 - x, y), (x, 1 - y), (1 - x, 1 - y)]
        src = x_ref.at[pl.ds(pl.multiple_of(c * m, 16), m), :] if sliced else x_ref

        def rows(px, py, pc):
            return out_ref.at[pl.ds(pl.multiple_of((4 * px + 2 * py + pc) * m, 8), m), :]

        def copy(k, blk, to, source=None):
            return pltpu.make_async_remote_copy(
                src_ref=rows(*blk) if source is None else source, dst_ref=rows(*blk),
                send_sem=send_sems.at[k], recv_sem=recv_sems.at[k], device_id=to, device_id_type=MESH)

        mine = pltpu.make_async_copy(src, rows(*me), local_sem)
        mine.start()
        first = [copy(0, me, sibling, source=src)]
        first += [copy(1 + j, me, (*chip, c), source=src) for j, chip in enumerate(chips)]
        for cp in first:
            cp.start()
        passed = [copy(4 + j, (*chip, c), sibling) for j, chip in enumerate(chips)]
        for j, chip in enumerate(chips):
            copy(1 + j, (*chip, c), me).wait_recv()
            passed[j].start()
        copy(0, sibling, me).wait_recv()
        for j, chip in enumerate(chips):
            copy(4 + j, (*chip, 1 - c), me).wait_recv()
        for cp in first + passed:
            cp.wait_send()
        mine.wait()

    space = pltpu.VMEM if vmem else pl.ANY
    return pl.pallas_call(
        body, out_shape=jax.ShapeDtypeStruct((N_DEV * m, n), block.dtype),
        in_specs=[pl.BlockSpec(memory_space=space)], out_specs=pl.BlockSpec(memory_space=space),
        scratch_shapes=[pltpu.SemaphoreType.DMA((7,)), pltpu.SemaphoreType.DMA((7,)), pltpu.SemaphoreType.DMA],
        name=name)(block)


HBM_REF = pl.BlockSpec(memory_space=pl.ANY)


def _gather_weights(shards):
    n = len(shards)
    halves = [a.shape[0] // 2 for a in shards]

    def body(*refs):
        x_refs, out_refs = refs[:n], refs[n:2 * n]
        send_sems, recv_sems, local_sems = refs[2 * n:]
        x, y, c = _coords()
        me, sibling = (x, y, c), (x, y, 1 - c)
        chips = [(1 - x, y), (x, 1 - y), (1 - x, 1 - y)]

        def src(a):
            return x_refs[a].at[pl.ds(pl.multiple_of(c * halves[a], 16), halves[a]), :]

        def blk(a, px, py, pc):
            return out_refs[a].at[4 * px + 2 * py + pc]

        def copy(a, k, who, to, source=None):
            return pltpu.make_async_remote_copy(
                src_ref=blk(a, *who) if source is None else source, dst_ref=blk(a, *who),
                send_sem=send_sems.at[7 * a + k], recv_sem=recv_sems.at[7 * a + k], device_id=to,
                device_id_type=MESH)

        mine = [pltpu.make_async_copy(src(a), blk(a, *me), local_sems.at[a]) for a in range(n)]
        for cp in mine:
            cp.start()
        started = []
        for a in range(n):
            started.append(copy(a, 0, me, sibling, source=src(a)))
            started += [copy(a, 1 + j, me, (*chip, c), source=src(a)) for j, chip in enumerate(chips)]
        for cp in started:
            cp.start()
        for j, chip in enumerate(chips):
            for a in range(n):
                copy(a, 1 + j, (*chip, c), me).wait_recv()
                onward = copy(a, 4 + j, (*chip, c), sibling)
                onward.start()
                started.append(onward)
        for a in range(n):
            copy(a, 0, sibling, me).wait_recv()
        for j, chip in enumerate(chips):
            for a in range(n):
                copy(a, 4 + j, (*chip, 1 - c), me).wait_recv()
        for cp in started:
            cp.wait_send()
        for cp in mine:
            cp.wait()

    outs = pl.pallas_call(
        body, out_shape=[jax.ShapeDtypeStruct((N_DEV, h, a.shape[1]), a.dtype) for a, h in zip(shards, halves)],
        in_specs=[HBM_REF] * n, out_specs=[HBM_REF] * n,
        scratch_shapes=[pltpu.SemaphoreType.DMA((7 * n,)), pltpu.SemaphoreType.DMA((7 * n,)),
                        pltpu.SemaphoreType.DMA((n,))],
        name="gather_weights")(*shards)
    return [o.reshape(N_CHIP, a.shape[0], a.shape[1]) for o, a in zip(outs, shards)]


def _swap_halves_with_sibling(fulls):
    n = len(fulls)
    halves = [a.shape[1] // 2 for a in fulls]

    def body(*refs):
        f_refs, got_refs = refs[:n], refs[n:2 * n]
        send_sems, recv_sems = refs[2 * n:]
        x, y, c = _coords()
        copies = []
        for a in range(n):
            src = f_refs[a].at[:, pl.ds(pl.multiple_of((1 - c) * halves[a], 8), halves[a]), :]
            copies.append(pltpu.make_async_remote_copy(
                src_ref=src, dst_ref=got_refs[a], send_sem=send_sems.at[a], recv_sem=recv_sems.at[a],
                device_id=(x, y, 1 - c), device_id_type=MESH))
        for cp in copies:
            cp.start()
        for cp in copies:
            cp.wait()

    return pl.pallas_call(
        body, out_shape=[jax.ShapeDtypeStruct((a.shape[0], h, a.shape[2]), a.dtype) for a, h in zip(fulls, halves)],
        in_specs=[HBM_REF] * n, out_specs=[HBM_REF] * n,
        scratch_shapes=[pltpu.SemaphoreType.DMA((n,)), pltpu.SemaphoreType.DMA((n,))],
        name="rs_pair_swap")(*fulls)


def _scatter_to_chips(parts):
    n = len(parts)

    def body(*refs):
        p_refs, got_refs = refs[:n], refs[n:2 * n]
        send_sems, recv_sems, local_sems = refs[2 * n:]
        x, y, c = _coords()
        my_chip = 2 * x + y
        chips = [(1 - x, y), (x, 1 - y), (1 - x, 1 - y)]
        mine = [pltpu.make_async_copy(p_refs[a].at[my_chip], got_refs[a].at[my_chip], local_sems.at[a])
                for a in range(n)]
        for cp in mine:
            cp.start()

        def copy(a, j, src_slab, dst_slab):
            px, py = chips[j]
            return pltpu.make_async_remote_copy(
                src_ref=p_refs[a].at[src_slab], dst_ref=got_refs[a].at[dst_slab], send_sem=send_sems.at[3 * a + j],
                recv_sem=recv_sems.at[3 * a + j], device_id=(px, py, c), device_id_type=MESH)

        sends = [copy(a, j, 2 * px + py, my_chip) for a in range(n) for j, (px, py) in enumerate(chips)]
        for cp in sends:
            cp.start()
        for a in range(n):
            for j, (px, py) in enumerate(chips):
                copy(a, j, my_chip, 2 * px + py).wait_recv()
        for cp in sends:
            cp.wait_send()
        for cp in mine:
            cp.wait()

    return pl.pallas_call(
        body, out_shape=[jax.ShapeDtypeStruct(a.shape, a.dtype) for a in parts],
        in_specs=[HBM_REF] * n, out_specs=[HBM_REF] * n,
        scratch_shapes=[pltpu.SemaphoreType.DMA((3 * n,)), pltpu.SemaphoreType.DMA((3 * n,)),
                        pltpu.SemaphoreType.DMA((n,))],
        name="rs_chip_scatter")(*parts)


def _join_halves_with_sibling(halves):
    n = len(halves)

    def body(*refs):
        h_refs, out_refs = refs[:n], refs[n:2 * n]
        send_sems, recv_sems, local_sems = refs[2 * n:]
        x, y, c = _coords()
        mine, pushes = [], []
        for a in range(n):
            rows = halves[a].shape[0]
            my_rows = pl.ds(pl.multiple_of(c * rows, 8), rows)
            mine.append(pltpu.make_async_copy(h_refs[a], out_refs[a].at[my_rows, :], local_sems.at[a]))
            pushes.append(pltpu.make_async_remote_copy(
                src_ref=h_refs[a], dst_ref=out_refs[a].at[my_rows, :], send_sem=send_sems.at[a],
                recv_sem=recv_sems.at[a], device_id=(x, y, 1 - c), device_id_type=MESH))
        for cp in mine + pushes:
            cp.start()
        for a in range(n):
            rows = halves[a].shape[0]
            other_rows = pl.ds(pl.multiple_of((1 - c) * rows, 8), rows)
            pltpu.make_async_remote_copy(
                src_ref=h_refs[a], dst_ref=out_refs[a].at[other_rows, :], send_sem=send_sems.at[a],
                recv_sem=recv_sems.at[a], device_id=(x, y, 1 - c), device_id_type=MESH).wait_recv()
        for cp in pushes:
            cp.wait_send()
        for cp in mine:
            cp.wait()

    return pl.pallas_call(
        body, out_shape=[jax.ShapeDtypeStruct((2 * a.shape[0], a.shape[1]), a.dtype) for a in halves],
        in_specs=[HBM_REF] * n, out_specs=[HBM_REF] * n,
        scratch_shapes=[pltpu.SemaphoreType.DMA((n,)), pltpu.SemaphoreType.DMA((n,)), pltpu.SemaphoreType.DMA((n,))],
        name="rs_pair_join")(*halves)


def _cols_to_slabs(g):
    rows, cols = g.shape
    return g.reshape(rows, N_CHIP, cols // N_CHIP).transpose(1, 0, 2)


def _slabs_to_cols(w):
    n, rows, cols = w.shape
    return w.transpose(1, 0, 2).reshape(rows, n * cols)


def _uq_to_padded(w_uq):
    per = w_uq.reshape(RQ, H, DN + DR)
    nope = per[:, :, :DN].reshape(RQ, H * DN)
    rope = jnp.pad(per[:, :, DN:], ((0, 0), (0, 0), (0, LANE - DR))).reshape(RQ, H * LANE)
    return jnp.concatenate([nope, rope], axis=1)


def _uq_from_padded(g):
    nope = g[:, :H * DN].reshape(RQ, H, DN)
    rope = g[:, H * DN:].reshape(RQ, H, LANE)[:, :, :DR]
    return jnp.concatenate([nope, rope], axis=2).reshape(RQ, H * (DN + DR))


def _rope_tables(positions):
    inv_freq = ROPE_THETA ** (-jnp.arange(0, DR, 2, dtype=F32) / DR)
    ang = positions.astype(F32)[:, None] * inv_freq
    cos, sin = jnp.cos(ang), jnp.sin(ang)
    return jnp.tile(cos, (1, 4)), jnp.tile(jnp.concatenate([-sin, sin], axis=1), (1, 2))


def _local_step(x, tgt, cos_t, sin_t, mod, weights, small, tiles):
    ts, tm, t_attn, chunk = tiles
    wa, wl, wg, w_uq2, w_ukv, wco, wao, wo, conv_w = weights
    norm_w, conv_b, ln_w, ln_b, q_norm_w, kv_norm_w, fnw = small
    shift, scale, gate = mod[:, 0:D], mod[:, D:2 * D], mod[:, 2 * D:3 * D]

    h = _adaln_norm(x, norm_w, shift, scale, ts)
    proj_a = _mm_nn(h, wa, tm, D, "proj_a")
    proj_l = _mm_nn(h, wl, tm, L_COLS, "proj_l")
    proj_g = _mm_nn(h, wg, tm, D, "proj_g")
    u0, u1, za = _conv_fwd(proj_a, conv_w, conv_b, ln_w, ln_b, ts, chunk)
    qn, kvn, q, k, v = _mla_prep(proj_l, q_norm_w, kv_norm_w, w_uq2, w_ukv, cos_t, sin_t, ts)
    o, lse = _attn_fwd(q, k, v, t_attn)
    (dx2, dza, do, delta, dpg, zb, mg, dmo, dya, dyb, vec_mid) = _middle(
        za, o, proj_g, x, tgt, gate, fnw, wco, wao, wo, ts)
    g_wo = _mm_tn(mg, dmo, tm, D, "grad_w_out")
    g_wco = _mm_tn(za, dya, tm, D, "grad_w_conv_out")
    g_wao = _mm_tn(zb, dyb, tm, D, "grad_w_attn_out")
    dq, dk, dv = _attn_bwd(q, k, v, do, lse, delta, t_attn)
    dpl, g_wuq2, g_wukv, vec_mla = _mla_prep_bwd(
        dq, dk, dv, proj_l, qn, kvn, q_norm_w, kv_norm_w, w_uq2, w_ukv, cos_t, sin_t, ts)
    dpa, g_conv_w, vec_conv = _conv_bwd(dza, proj_a, u0, u1, conv_w, ln_w, ln_b, ts, chunk)
    grad_x, vec_in = _input_bwd(dpa, dpl, dpg, wa, wl, wg, x, dx2, norm_w, scale, ts)
    g_wa = _mm_tn(h, dpa, tm, D, "grad_w_in_a")
    g_wl = _mm_tn(h, dpl, tm, L_COLS, "grad_w_in_l")
    g_wg = _mm_tn(h, dpg, tm, D, "grad_w_in_g")

    dmod = jnp.concatenate([vec_in[0:1], vec_in[1:2], vec_mid[1:2]], axis=1)
    sums = dict(dmod=dmod, norm_w=vec_in[2:3], conv_b=vec_conv[2:3], ln_w=vec_conv[0:1], ln_b=vec_conv[1:2],
                q_norm_w=vec_mla[0:1], kv_norm_w=vec_mla[1:2], final_norm_w=vec_mid[0:1], loss=vec_mid[2:3],
                conv_w=g_conv_w)
    grads = dict(wa=g_wa, wl=g_wl, wg=g_wg, w_uq2=g_wuq2, w_ukv=g_wukv, wco=g_wco, wao=g_wao, wo=g_wo)
    return grad_x, grads, sums


SMALL_ORDER = (("dmod", 3 * D), ("norm_w", D), ("conv_b", D), ("ln_w", D), ("ln_b", D), ("q_norm_w", RQ),
               ("kv_norm_w", RQ), ("final_norm_w", D), ("loss", D), ("conv_w", HALO * D))
SMALL_ROWS = 336


def kernel(x, c, positions, w_ada, b_ada, norm_w, w_in, conv_w, conv_b, conv_ln_w, conv_ln_b, w_conv_out, q_norm_w, w_uq, kv_norm_w, w_ukv, w_attn_out, w_out, final_norm_w, loss_target, m_w_ada, m_b_ada, m_norm_w, m_w_in, m_conv_w, m_conv_b, m_conv_ln_w, m_conv_ln_b, m_w_conv_out, m_q_norm_w, m_w_uq, m_kv_norm_w, m_w_ukv, m_w_attn_out, m_w_out, m_final_norm_w, v_w_ada, v_b_ada, v_norm_w, v_w_in, v_conv_w, v_conv_b, v_conv_ln_w, v_conv_ln_b, v_w_conv_out, v_q_norm_w, v_w_uq, v_kv_norm_w, v_w_ukv, v_w_attn_out, v_w_out, v_final_norm_w):
    ix, iy, ic = _coords()
    chip = 2 * ix + iy
    dev = 4 * ix + 2 * iy + ic
    s = x.shape[1]
    tiles = (256, 512, 512, 32)

    conv_w_pad = jnp.pad(conv_w[0], ((0, HALO - KC), (0, 0)))
    small_in = jnp.concatenate([c.reshape(8, LANE), conv_w_pad.reshape(64, LANE)], axis=0)
    small_all = _allgather8(small_in, 72, True, "gather_c_conv").reshape(N_DEV, 72, LANE)
    c_all = small_all[:, 0:8].reshape(N_DEV, D)
    conv_full = jnp.concatenate(
        [small_all[2 * k, 8:72].reshape(HALO, D // N_CHIP) for k in range(N_CHIP)], axis=1)

    shards = [w[0].astype(BF16) for w in (w_in, w_uq, w_ukv, w_conv_out, w_attn_out, w_out)]
    g_in, g_uq, g_ukv, g_co, g_ao, g_o = _gather_weights(shards)
    w_in_f, w_uq_f, w_ukv_f = _slabs_to_cols(g_in), _slabs_to_cols(g_uq), _slabs_to_cols(g_ukv)
    wco, wao, wo = g_co.reshape(D, D), g_ao.reshape(D, D), g_o.reshape(D, D)
    wa = w_in_f[:, 0:A_COLS]
    wl = jnp.pad(w_in_f[:, A_COLS:A_COLS + L_COLS_RAW], ((0, 0), (0, L_COLS - L_COLS_RAW)))
    wg = w_in_f[:, A_COLS + L_COLS_RAW:]
    weights = (wa, wl, wg, _uq_to_padded(w_uq_f), w_ukv_f, wco, wao, wo, conv_full)

    ada_cols = w_ada.shape[2]
    b_shard = lax.dynamic_slice(b_ada, (0, chip * ada_cols), (1, ada_cols))
    mod_part = _ada_fwd(c_all, w_ada[0], b_shard)
    mod_all = _allgather8(mod_part, N_DEV, True, "gather_mod").reshape(N_DEV, N_DEV, ada_cols)
    mod = jnp.concatenate(
        [lax.dynamic_slice(mod_all[2 * k], (dev, 0), (1, ada_cols)) for k in range(N_CHIP)], axis=1)

    cos_t, sin_t = _rope_tables(positions[0])
    small = (norm_w, conv_b, conv_ln_w, conv_ln_b, q_norm_w, kv_norm_w, final_norm_w.reshape(1, D))
    grad_x, grads, sums = _local_step(x[0], loss_target[0], cos_t, sin_t, mod, weights, small, tiles)

    small_flat = jnp.concatenate([sums[name].reshape(-1) for name, _ in SMALL_ORDER])
    small_flat = jnp.pad(small_flat, (0, SMALL_ROWS * LANE - small_flat.shape[0]))
    small_g = _allgather8(small_flat.reshape(SMALL_ROWS, LANE), SMALL_ROWS, True, "gather_small_grads")
    small_g = small_g.reshape(N_DEV, SMALL_ROWS, LANE)
    small_sum = _sum_slabs(small_g, SMALL_ROWS, "sum_small_grads").reshape(-1)
    tot, pos = {}, 0
    for name, size in SMALL_ORDER:
        tot[name] = small_sum[pos:pos + size]
        pos += size
    loss = (0.5 / D) * jnp.sum(tot["loss"])
    dmod_all = small_g.reshape(N_DEV, -1)[:, 0:3 * D]
    g_b_ada = tot["dmod"].reshape(1, 3 * D)
    dmod_shard = lax.dynamic_slice(dmod_all, (0, chip * ada_cols), (N_DEV, ada_cols))
    g_w_ada = _ada_bwd(c_all.T, dmod_shard).reshape(1, D, ada_cols)
    g_conv_w = lax.dynamic_slice(tot["conv_w"].reshape(HALO, D), (0, chip * (D // N_CHIP)), (KC, D // N_CHIP))
    g_conv_w = g_conv_w.reshape(1, KC, D // N_CHIP)

    g_w_in = jnp.concatenate([grads["wa"], grads["wl"][:, 0:L_COLS_RAW], grads["wg"]], axis=1)
    nr = D // N_CHIP
    fulls = [_cols_to_slabs(g_w_in), _cols_to_slabs(_uq_from_padded(grads["w_uq2"])), _cols_to_slabs(grads["w_ukv"]),
             grads["wco"].reshape(N_CHIP, nr, D), grads["wao"].reshape(N_CHIP, nr, D),
             grads["wo"].reshape(N_CHIP, nr, D)]
    from_sibling = _swap_halves_with_sibling(fulls)
    core = ic.reshape(1).astype(jnp.int32)
    chip_sums = [_add_own_half(f, o, core, min(256, o.shape[1]), f"add_own_half_{n}")
                 for n, (f, o) in enumerate(zip(fulls, from_sibling))]
    arrived = _scatter_to_chips(chip_sums)
    my_halves = [_sum_slabs(a, min(128, a.shape[1]), f"sum_chip_slabs_{n}") for n, a in enumerate(arrived)]
    g_w_in_s, g_w_uq_s, g_w_ukv_s, g_wco_s, g_wao_s, g_wo_s = _join_halves_with_sibling(my_halves)

    def big(w, g, m, v, tr, name):
        d, nm, nv = _adamw(w[0], g, m[0], v[0], tr, name)
        return g[None], d[None], nm[None], nv[None]

    vec_names = ("b_ada", "norm_w", "conv_b", "conv_ln_w", "conv_ln_b", "q_norm_w", "kv_norm_w", "final_norm_w")
    vec_w = (b_ada, norm_w, conv_b, conv_ln_w, conv_ln_b, q_norm_w, kv_norm_w, final_norm_w)
    vec_m = (m_b_ada, m_norm_w, m_conv_b, m_conv_ln_w, m_conv_ln_b, m_q_norm_w, m_kv_norm_w, m_final_norm_w)
    vec_v = (v_b_ada, v_norm_w, v_conv_b, v_conv_ln_w, v_conv_ln_b, v_q_norm_w, v_kv_norm_w, v_final_norm_w)
    vec_g = (g_b_ada, tot["norm_w"], tot["conv_b"], tot["ln_w"], tot["ln_b"], tot["q_norm_w"], tot["kv_norm_w"],
             tot["final_norm_w"])
    vec_g = tuple(g.reshape(w.shape) for g, w in zip(vec_g, vec_w))
    cat = lambda arrs: jnp.concatenate([a.reshape(-1) for a in arrs]).reshape(-1, LANE)
    vd, vnm, vnv = _adamw(cat(vec_w), cat(vec_g), cat(vec_m), cat(vec_v), cat(vec_w).shape[0], "adamw_vectors")

    def split(packed):
        flat, out, pos = packed.reshape(-1), [], 0
        for w in vec_w:
            out.append(flat[pos:pos + w.size].reshape(w.shape))
            pos += w.size
        return out

    res = {}
    for name, g, d, nm, nv in zip(vec_names, vec_g, split(vd), split(vnm), split(vnv)):
        res[name] = (g, d, nm, nv)
    res["w_ada"] = big(w_ada, g_w_ada[0], m_w_ada, v_w_ada, 256, "adamw_w_ada")
    res["w_in"] = big(w_in, g_w_in_s, m_w_in, v_w_in, 256, "adamw_w_in")
    res["conv_w"] = big(conv_w, g_conv_w[0], m_conv_w, v_conv_w, KC, "adamw_conv_w")
    res["w_conv_out"] = big(w_conv_out, g_wco_s, m_w_conv_out, v_w_conv_out, 256, "adamw_w_conv_out")
    res["w_uq"] = big(w_uq, g_w_uq_s, m_w_uq, v_w_uq, 256, "adamw_w_uq")
    res["w_ukv"] = big(w_ukv, g_w_ukv_s, m_w_ukv, v_w_ukv, 256, "adamw_w_ukv")
    res["w_attn_out"] = big(w_attn_out, g_wao_s, m_w_attn_out, v_w_attn_out, 256, "adamw_w_attn_out")
    res["w_out"] = big(w_out, g_wo_s, m_w_out, v_w_out, 256, "adamw_w_out")

    order = ("w_ada", "b_ada", "norm_w", "w_in", "conv_w", "conv_b", "conv_ln_w", "conv_ln_b", "w_conv_out",
             "q_norm_w", "w_uq", "kv_norm_w", "w_ukv", "w_attn_out", "w_out", "final_norm_w")
    outs = [loss, grad_x[None]]
    for slot in range(4):
        outs += [res[name][slot] for name in order]
    return tuple(outs)
```

```python
import functools

import numpy as np
import jax
import jax.numpy as jnp
from jax import lax
from jax.experimental import pallas as pl
from jax.experimental.pallas import tpu as pltpu

F32 = jnp.float32
BF16 = jnp.bfloat16
MESH = pl.DeviceIdType.MESH

D = 1024
H = 8
DN = 128
DR = 64
RQ = 256
KC = 31
HALO = 32
EPS = 1e-6
ROPE_THETA = 10000.0
N_CHIP = 4
N_DEV = 8
LANE = 128
VMEM_BIG = 56 * 1024 * 1024

ADAM_LR = 0.001
ADAM_B1 = 0.9
ADAM_B2 = 0.999
ADAM_EPS = 1e-08
ADAM_WD = 0.01
ADAM_STEP = 10

A_COLS = 3 * D
L_COLS_RAW = RQ + RQ + DR
L_COLS = 640
G_COLS = 3 * D
IN_COLS = A_COLS + L_COLS_RAW + G_COLS


def _params(sem=None, vmem=None):
    kw = {}
    if sem is not None:
        kw["dimension_semantics"] = sem
    if vmem is not None:
        kw["vmem_limit_bytes"] = vmem
    return pltpu.CompilerParams(**kw)


def _dot(a, b):
    return jnp.dot(a, b, preferred_element_type=F32)


def _dot_nt(a, b):
    return lax.dot_general(a, b, (((1,), (1,)), ((), ())), preferred_element_type=F32)


def _dot_tn(a, b):
    return lax.dot_general(a, b, (((0,), (0,)), ((), ())), preferred_element_type=F32)


def _colsum(v):
    return jnp.sum(v, axis=0, keepdims=True)


def _rowmean(v):
    return jnp.mean(v, axis=-1, keepdims=True)


def _sigmoid(v):
    return jax.nn.sigmoid(v)


def _dsilu(v, s):
    return s * (1.0 + v * (1.0 - s))


def _swap_halves(v, first_half):
    return jnp.where(first_half, pltpu.roll(v, 96, 1), pltpu.roll(v, 32, 1))


def _first_half_mask(rows):
    lane = lax.broadcasted_iota(jnp.int32, (rows, LANE), 1)
    return (lane % 64) < 32


def _adaln_norm(x, norm_w, shift, scale, ts):
    s = x.shape[0]

    def body(x_ref, nw_ref, sh_ref, sc_ref, h_ref):
        xv = x_ref[...]
        r = lax.rsqrt(_rowmean(xv * xv) + EPS)
        y = xv * r * nw_ref[...]
        h_ref[...] = (y * (1.0 + sc_ref[...]) + sh_ref[...]).astype(BF16)

    row = pl.BlockSpec((ts, D), lambda i: (i, 0))
    vec = pl.BlockSpec((1, D), lambda i: (0, 0))
    return pl.pallas_call(
        body, grid=(s // ts,), in_specs=[row, vec, vec, vec], out_specs=row,
        out_shape=jax.ShapeDtypeStruct((s, D), BF16), name="adaln_norm",
        compiler_params=_params(("parallel",)))(x, norm_w, shift, scale)


def _mm_nn(a, b, tm, tn, name):
    m, k = a.shape
    n = b.shape[1]

    def body(a_ref, b_ref, o_ref):
        o_ref[...] = _dot(a_ref[...], b_ref[...])

    return pl.pallas_call(
        body, grid=(n // tn, m // tm),
        in_specs=[pl.BlockSpec((tm, k), lambda j, i: (i, 0)), pl.BlockSpec((k, tn), lambda j, i: (0, j))],
        out_specs=pl.BlockSpec((tm, tn), lambda j, i: (i, j)),
        out_shape=jax.ShapeDtypeStruct((m, n), F32), name=name,
        compiler_params=_params(("parallel", "parallel")))(a, b)


def _mm_tn(a, b, tm, tn, name):
    m, k = a.shape
    n = b.shape[1]

    def body(a_ref, b_ref, o_ref):
        @pl.when(pl.program_id(1) == 0)
        def _():
            o_ref[...] = jnp.zeros_like(o_ref)
        o_ref[...] += _dot_tn(a_ref[...], b_ref[...])

    return pl.pallas_call(
        body, grid=(n // tn, m // tm),
        in_specs=[pl.BlockSpec((tm, k), lambda j, i: (i, 0)), pl.BlockSpec((tm, tn), lambda j, i: (i, j))],
        out_specs=pl.BlockSpec((k, tn), lambda j, i: (0, j)),
        out_shape=jax.ShapeDtypeStruct((k, n), F32), name=name,
        compiler_params=_params(("parallel", "arbitrary")))(a, b)


def _shifted_copies(win_ref, sh_ref, rows):
    for p in range(1, 8):
        sh_ref[p - 1, 0:rows, :] = win_ref[pl.ds(p, rows), :]


def _tap_rows(win_ref, sh_ref, start, rows):
    p = start % 8
    if p == 0:
        return win_ref[pl.ds(start, rows), :]
    return sh_ref[p - 1, pl.ds(start - p, rows), :]


def _conv_taps(win_ref, sh_ref, w_ref, rows, chunk, offset_of_tap):
    pieces = []
    for c0 in range(0, rows, chunk):
        acc = None
        for j in range(KC):
            term = w_ref[j:j + 1, :] * _tap_rows(win_ref, sh_ref, c0 + offset_of_tap(j), chunk)
            acc = term if acc is None else acc + term
        pieces.append(acc)
    return pieces


def _conv_fwd(proj_a, conv_w, conv_b, ln_w, ln_b, ts, chunk):
    s = proj_a.shape[0]

    def body(av_ref, al_ref, ag_ref, w_ref, b_ref, lw_ref, lb_ref, u0_ref, u1_ref, za_ref, win_ref, sh_ref):
        @pl.when(pl.program_id(0) == 0)
        def _():
            win_ref[0:HALO, :] = jnp.zeros((HALO, D), F32)

        u0 = av_ref[...] * _sigmoid(al_ref[...])
        u0_ref[...] = u0
        win_ref[HALO:HALO + ts, :] = u0
        _shifted_copies(win_ref, sh_ref, ts + HALO - 8)
        pieces = _conv_taps(win_ref, sh_ref, w_ref, ts, chunk, lambda j: HALO - (KC - 1) + j)
        for n, acc in enumerate(pieces):
            u1_ref[n * chunk:(n + 1) * chunk, :] = acc + b_ref[...]
        win_ref[0:HALO, :] = win_ref[ts:ts + HALO, :]

        u1 = u1_ref[...]
        xc = u1 - _rowmean(u1)
        rstd = lax.rsqrt(_rowmean(xc * xc) + EPS)
        u2 = xc * rstd * lw_ref[...] + lb_ref[...]
        u3 = u2 * _sigmoid(u2)
        ag = ag_ref[...]
        za_ref[...] = (u3 * (ag * _sigmoid(ag))).astype(BF16)

    col = lambda c: pl.BlockSpec((ts, D), lambda i, c=c: (i, c))
    row = pl.BlockSpec((ts, D), lambda i: (i, 0))
    vec = pl.BlockSpec((1, D), lambda i: (0, 0))
    return pl.pallas_call(
        body, grid=(s // ts,),
        in_specs=[col(0), col(1), col(2), pl.BlockSpec((HALO, D), lambda i: (0, 0)), vec, vec, vec],
        out_specs=[row, row, row],
        out_shape=[jax.ShapeDtypeStruct((s, D), F32), jax.ShapeDtypeStruct((s, D), F32),
                   jax.ShapeDtypeStruct((s, D), BF16)],
        scratch_shapes=[pltpu.VMEM((ts + HALO, D), F32), pltpu.VMEM((7, ts + HALO, D), F32)], name="conv_fwd",
        compiler_params=_params(("arbitrary",), VMEM_BIG))(proj_a, proj_a, proj_a, conv_w, conv_b, ln_w, ln_b)


def _conv_bwd(dza, proj_a, u0, u1, conv_w, ln_w, ln_b, ts, chunk):
    s = dza.shape[0]
    nt = s // ts
    per = ts // HALO

    def body(dza_ref, av_ref, al_ref, ag_ref, u0_ref, u0p_ref, u1_ref, w_ref, lw_ref, lb_ref,
             dpa_ref, gw_ref, gv_ref, dwin_ref, uwin_ref, du0_ref, gwp_ref, dsh_ref, ush_ref):
        step = pl.program_id(0)
        tile = nt - 1 - step

        @pl.when(step == 0)
        def _():
            dwin_ref[ts:ts + HALO, :] = jnp.zeros((HALO, D), F32)
            gwp_ref[...] = jnp.zeros_like(gwp_ref)
            gv_ref[...] = jnp.zeros_like(gv_ref)

        ag = ag_ref[...]
        sg = _sigmoid(ag)
        u1 = u1_ref[...]
        xc = u1 - _rowmean(u1)
        rstd = lax.rsqrt(_rowmean(xc * xc) + EPS)
        xh = xc * rstd
        u2 = xh * lw_ref[...] + lb_ref[...]
        s2 = _sigmoid(u2)
        dz = dza_ref[...]
        du3 = dz * (ag * sg)
        dpa_ref[:, 2 * D:3 * D] = (dz * (u2 * s2) * _dsilu(ag, sg)).astype(BF16)
        du2 = du3 * _dsilu(u2, s2)
        gv_ref[0:1, :] += _colsum(du2 * xh)
        gv_ref[1:2, :] += _colsum(du2)
        dxh = du2 * lw_ref[...]
        du1 = rstd * (dxh - _rowmean(dxh) - xh * _rowmean(dxh * xh))
        gv_ref[2:3, :] += _colsum(du1)
        dwin_ref[0:ts, :] = du1

        uwin_ref[0:HALO, :] = jnp.where(tile == 0, 0.0, u0p_ref[...])
        uwin_ref[HALO:HALO + ts, :] = u0_ref[...]

        _shifted_copies(dwin_ref, dsh_ref, ts + HALO - 8)
        _shifted_copies(uwin_ref, ush_ref, ts + HALO - 8)
        pieces = _conv_taps(dwin_ref, dsh_ref, w_ref, ts, chunk, lambda j: (KC - 1) - j)
        for n, acc in enumerate(pieces):
            du0_ref[n * chunk:(n + 1) * chunk, :] = acc
        for c0 in range(0, ts, chunk):
            dchunk = dwin_ref[c0:c0 + chunk, :]
            for j in range(KC):
                prod = dchunk * _tap_rows(uwin_ref, ush_ref, c0 + HALO - (KC - 1) + j, chunk)
                gwp_ref[8 * j:8 * j + 8, :] += jnp.sum(prod.reshape(chunk // 8, 8, D), axis=0)
        dwin_ref[ts:ts + HALO, :] = dwin_ref[0:HALO, :]

        du0 = du0_ref[...]
        al = al_ref[...]
        sl = _sigmoid(al)
        dpa_ref[:, 0:D] = (du0 * sl).astype(BF16)
        dpa_ref[:, D:2 * D] = (du0 * av_ref[...] * sl * (1.0 - sl)).astype(BF16)

        @pl.when(step == nt - 1)
        def _():
            for j in range(KC):
                gw_ref[j:j + 1, :] = _colsum(gwp_ref[8 * j:8 * j + 8, :])
            gw_ref[KC:HALO, :] = jnp.zeros((HALO - KC, D), F32)

    rev = lambda i: nt - 1 - i
    col = lambda c: pl.BlockSpec((ts, D), lambda i, c=c: (rev(i), c))
    row = pl.BlockSpec((ts, D), lambda i: (rev(i), 0))
    vec = pl.BlockSpec((1, D), lambda i: (0, 0))
    halo = pl.BlockSpec((HALO, D), lambda i: (jnp.maximum(rev(i) * per - 1, 0), 0))
    return pl.pallas_call(
        body, grid=(nt,),
        in_specs=[row, col(0), col(1), col(2), row, halo, row, pl.BlockSpec((HALO, D), lambda i: (0, 0)), vec, vec],
        out_specs=[pl.BlockSpec((ts, A_COLS), lambda i: (rev(i), 0)),
                   pl.BlockSpec((HALO, D), lambda i: (0, 0)), pl.BlockSpec((8, D), lambda i: (0, 0))],
        out_shape=[jax.ShapeDtypeStruct((s, A_COLS), BF16), jax.ShapeDtypeStruct((HALO, D), F32),
                   jax.ShapeDtypeStruct((8, D), F32)],
        scratch_shapes=[pltpu.VMEM((ts + HALO, D), F32), pltpu.VMEM((ts + HALO, D), F32),
                        pltpu.VMEM((ts, D), F32), pltpu.VMEM((8 * HALO, D), F32),
                        pltpu.VMEM((7, ts + HALO, D), F32), pltpu.VMEM((7, ts + HALO, D), F32)],
        name="conv_bwd", compiler_params=_params(("arbitrary",), VMEM_BIG))(
            dza, proj_a, proj_a, proj_a, u0, u0, u1, conv_w, ln_w, ln_b)


def _mla_prep(proj_l, q_norm_w, kv_norm_w, w_uq2, w_ukv, cos_t, sin_t, ts):
    s = proj_l.shape[0]

    def body(pl_ref, qw_ref, kw_ref, wq_ref, wkv_ref, c_ref, s_ref, qn_ref, kvn_ref, q_ref, k_ref, v_ref):
        first = _first_half_mask(ts)
        cs = c_ref[...]
        sn = s_ref[...]

        def rms(v, w):
            return v * lax.rsqrt(_rowmean(v * v) + EPS) * w

        def rope(v):
            return v * cs + _swap_halves(v, first) * sn

        qn = rms(pl_ref[:, 0:RQ], qw_ref[...]).astype(BF16)
        kvn = rms(pl_ref[:, RQ:2 * RQ], kw_ref[...]).astype(BF16)
        qn_ref[...] = qn
        kvn_ref[...] = kvn
        q = _dot(qn, wq_ref[...])
        kv = _dot(kvn, wkv_ref[...])
        kr = rope(pl_ref[:, 2 * RQ:2 * RQ + LANE]).astype(BF16)
        for h in range(H):
            q_ref[h, :, 0:DN] = q[:, DN * h:DN * (h + 1)].astype(BF16)
            q_ref[h, :, DN:2 * DN] = rope(q[:, H * DN + LANE * h:H * DN + LANE * (h + 1)]).astype(BF16)
            k_ref[h, :, 0:DN] = kv[:, 2 * DN * h:2 * DN * h + DN].astype(BF16)
            k_ref[h, :, DN:2 * DN] = kr
            v_ref[h, :, 0:DN] = kv[:, 2 * DN * h + DN:2 * DN * (h + 1)].astype(BF16)
            v_ref[h, :, DN:2 * DN] = jnp.ones((ts, DN), BF16)

    const = lambda shape: pl.BlockSpec(shape, lambda i: (0,) * len(shape))
    rowb = lambda w: pl.BlockSpec((ts, w), lambda i: (i, 0))
    head = lambda w: pl.BlockSpec((H, ts, w), lambda i: (0, i, 0))
    return pl.pallas_call(
        body, grid=(s // ts,),
        in_specs=[rowb(L_COLS), const((1, RQ)), const((1, RQ)), const((RQ, 2 * H * DN)), const((RQ, 2 * H * DN)),
                  rowb(LANE), rowb(LANE)],
        out_specs=[rowb(RQ), rowb(RQ), head(2 * DN), head(2 * DN), head(2 * DN)],
        out_shape=[jax.ShapeDtypeStruct((s, RQ), BF16), jax.ShapeDtypeStruct((s, RQ), BF16),
                   jax.ShapeDtypeStruct((H, s, 2 * DN), BF16), jax.ShapeDtypeStruct((H, s, 2 * DN), BF16),
                   jax.ShapeDtypeStruct((H, s, 2 * DN), BF16)],
        name="mla_prep", compiler_params=_params(("parallel",)))(
            proj_l, q_norm_w, kv_norm_w, w_uq2, w_ukv, cos_t, sin_t)


def _mla_prep_bwd(dq, dk, dv, proj_l, qn, kvn, q_norm_w, kv_norm_w, w_uq2, w_ukv, cos_t, sin_t, ts):
    s = proj_l.shape[0]

    def body(dq_ref, dk_ref, dv_ref, pl_ref, qn_ref, kvn_ref, qw_ref, kw_ref, wq_ref, wkv_ref, c_ref, s_ref,
             dpl_ref, gwq_ref, gwkv_ref, gv_ref, dq2_ref, dkv2_ref):
        @pl.when(pl.program_id(0) == 0)
        def _():
            gwq_ref[...] = jnp.zeros_like(gwq_ref)
            gwkv_ref[...] = jnp.zeros_like(gwkv_ref)
            gv_ref[...] = jnp.zeros_like(gv_ref)

        first = _first_half_mask(ts)
        cs = c_ref[...]
        sn = s_ref[...]

        def rope_bwd(g):
            return g * cs + _swap_halves(g * sn, first)

        def rms_bwd(v, w, dy):
            r = lax.rsqrt(_rowmean(v * v) + EPS)
            vh = v * r
            dvh = dy * w
            return r * (dvh - vh * _rowmean(dvh * vh)), _colsum(dy * vh)

        dkr = None
        for h in range(H):
            dq2_ref[:, DN * h:DN * (h + 1)] = dq_ref[h, :, 0:DN].astype(BF16)
            dq2_ref[:, H * DN + LANE * h:H * DN + LANE * (h + 1)] = rope_bwd(dq_ref[h, :, DN:2 * DN]).astype(BF16)
            dkv2_ref[:, 2 * DN * h:2 * DN * h + DN] = dk_ref[h, :, 0:DN].astype(BF16)
            dkv2_ref[:, 2 * DN * h + DN:2 * DN * (h + 1)] = dv_ref[h].astype(BF16)
            part = dk_ref[h, :, DN:2 * DN]
            dkr = part if dkr is None else dkr + part

        dq2 = dq2_ref[...]
        dkv2 = dkv2_ref[...]
        gwq_ref[...] += _dot_tn(qn_ref[...], dq2)
        gwkv_ref[...] += _dot_tn(kvn_ref[...], dkv2)
        dcq, gq = rms_bwd(pl_ref[:, 0:RQ], qw_ref[...], _dot_nt(dq2, wq_ref[...]))
        dckv, gkv = rms_bwd(pl_ref[:, RQ:2 * RQ], kw_ref[...], _dot_nt(dkv2, wkv_ref[...]))
        gv_ref[0:1, :] += gq
        gv_ref[1:2, :] += gkv
        dpl_ref[:, 0:RQ] = dcq.astype(BF16)
        dpl_ref[:, RQ:2 * RQ] = dckv.astype(BF16)
        dpl_ref[:, 2 * RQ:2 * RQ + LANE] = rope_bwd(dkr).astype(BF16)

    const = lambda shape: pl.BlockSpec(shape, lambda i: (0,) * len(shape))
    rowb = lambda w: pl.BlockSpec((ts, w), lambda i: (i, 0))
    head = lambda w: pl.BlockSpec((H, ts, w), lambda i: (0, i, 0))
    return pl.pallas_call(
        body, grid=(s // ts,),
        in_specs=[head(2 * DN), head(2 * DN), head(DN), rowb(L_COLS), rowb(RQ), rowb(RQ), const((1, RQ)),
                  const((1, RQ)), const((RQ, 2 * H * DN)), const((RQ, 2 * H * DN)), rowb(LANE), rowb(LANE)],
        out_specs=[rowb(L_COLS), const((RQ, 2 * H * DN)), const((RQ, 2 * H * DN)), const((8, RQ))],
        out_shape=[jax.ShapeDtypeStruct((s, L_COLS), BF16), jax.ShapeDtypeStruct((RQ, 2 * H * DN), F32),
                   jax.ShapeDtypeStruct((RQ, 2 * H * DN), F32), jax.ShapeDtypeStruct((8, RQ), F32)],
        scratch_shapes=[pltpu.VMEM((ts, 2 * H * DN), BF16), pltpu.VMEM((ts, 2 * H * DN), BF16)],
        name="mla_prep_bwd", compiler_params=_params(("arbitrary",), VMEM_BIG))(
            dq, dk, dv, proj_l, qn, kvn, q_norm_w, kv_norm_w, w_uq2, w_ukv, cos_t, sin_t)


def _causal_pairs(n, by_key):
    if by_key:
        pairs = [(i, j) for j in range(n) for i in range(j, n)]
    else:
        pairs = [(i, j) for i in range(n) for j in range(i + 1)]
    return (jnp.asarray(np.array([p[0] for p in pairs], np.int32)),
            jnp.asarray(np.array([p[1] for p in pairs], np.int32)))


ATT_HEADS = 2
ATT_ROWS = 64


def _diag_width(r0, t):
    return min(t, -(-(r0 + ATT_ROWS) // LANE) * LANE)


def _diag_mask_rows(r0, width):
    rows = r0 + lax.broadcasted_iota(jnp.int32, (ATT_ROWS, width), 0)
    cols = lax.broadcasted_iota(jnp.int32, (ATT_ROWS, width), 1)
    return cols <= rows


def _diag_mask(t):
    return lax.broadcasted_iota(jnp.int32, (t, t), 1) <= lax.broadcasted_iota(jnp.int32, (t, t), 0)


def _attn_fwd(q, k, v, t):
    s = q.shape[1]
    n = s // t
    scale = float((DN + DR) ** -0.5)
    qi, ki = _causal_pairs(n, by_key=False)

    def body(qi_ref, ki_ref, q_ref, k_ref, v_ref, o_ref, lse_ref, m_sc, acc_sc, s_sc, p_sc):
        p = pl.program_id(1)
        i = qi_ref[p]
        j = ki_ref[p]

        @pl.when(j == 0)
        def _():
            m_sc[...] = jnp.full_like(m_sc, -jnp.inf)
            acc_sc[...] = jnp.zeros_like(acc_sc)

        def step(diag):
            for h in range(ATT_HEADS):
                sc = _dot_nt(q_ref[h], k_ref[h])
                if diag:
                    sc = jnp.where(_diag_mask(t), sc, -jnp.inf)
                s_sc[h] = sc
            for h in range(ATT_HEADS):
                m_prev = m_sc[h]
                m_new = jnp.maximum(m_prev, jnp.max(s_sc[h], axis=-1, keepdims=True) * scale)
                m_sc[h] = m_new
                acc_sc[h] = jnp.exp(m_prev - m_new) * acc_sc[h]
                for r0 in range(0, t, ATT_ROWS):
                    rows = slice(r0, r0 + ATT_ROWS)
                    p_sc[h, rows, :] = jnp.exp(s_sc[h, rows, :] * scale - m_new[rows]).astype(BF16)
            for h in range(ATT_HEADS):
                acc_sc[h] += _dot(p_sc[h], v_ref[h])

        @pl.when(j < i)
        def _():
            step(False)

        @pl.when(j == i)
        def _():
            step(True)
            for h in range(ATT_HEADS):
                l = acc_sc[h, :, DN:2 * DN]
                o_ref[:, DN * h:DN * (h + 1)] = acc_sc[h, :, 0:DN] / l
                lse_ref[h] = m_sc[h] + jnp.log(l[:, 0:1])

    hb = ATT_HEADS
    grid_spec = pltpu.PrefetchScalarGridSpec(
        num_scalar_prefetch=2, grid=(H // hb, int(qi.shape[0])),
        in_specs=[pl.BlockSpec((hb, t, 2 * DN), lambda h, p, qi, ki: (h, qi[p], 0)),
                  pl.BlockSpec((hb, t, 2 * DN), lambda h, p, qi, ki: (h, ki[p], 0)),
                  pl.BlockSpec((hb, t, 2 * DN), lambda h, p, qi, ki: (h, ki[p], 0))],
        out_specs=[pl.BlockSpec((t, hb * DN), lambda h, p, qi, ki: (qi[p], h)),
                   pl.BlockSpec((hb, t, 1), lambda h, p, qi, ki: (h, qi[p], 0))],
        scratch_shapes=[pltpu.VMEM((hb, t, 1), F32), pltpu.VMEM((hb, t, 2 * DN), F32),
                        pltpu.VMEM((hb, t, t), F32), pltpu.VMEM((hb, t, t), BF16)])
    return pl.pallas_call(
        body, grid_spec=grid_spec,
        out_shape=[jax.ShapeDtypeStruct((s, H * DN), F32), jax.ShapeDtypeStruct((H, s, 1), F32)],
        name="attn_fwd", compiler_params=_params(("parallel", "arbitrary"), VMEM_BIG))(qi, ki, q, k, v)


def _attn_bwd(q, k, v, do, lse, delta, t):
    s = q.shape[1]
    n = s // t
    scale = float((DN + DR) ** -0.5)
    qi, ki = _causal_pairs(n, by_key=True)

    def body(qi_ref, ki_ref, q_ref, k_ref, v_ref, do_ref, lse_ref, dl_ref, dq_ref, dk_ref, dv_ref,
             dk_sc, dv_sc, s_sc, dp_sc, p_sc, ds_sc):
        p = pl.program_id(1)
        i = qi_ref[p]
        j = ki_ref[p]

        @pl.when(p == 0)
        def _():
            dq_ref[...] = jnp.zeros_like(dq_ref)

        @pl.when(i == j)
        def _():
            dk_sc[...] = jnp.zeros_like(dk_sc)
            dv_sc[...] = jnp.zeros_like(dv_sc)

        def step(diag):
            for h in range(ATT_HEADS):
                s_sc[h] = _dot_nt(q_ref[h], k_ref[h])
                dp_sc[h] = _dot_nt(do_ref[:, DN * h:DN * (h + 1)], v_ref[h, :, 0:DN])
            for h in range(ATT_HEADS):
                for r0 in range(0, t, ATT_ROWS):
                    rows = slice(r0, r0 + ATT_ROWS)
                    width = _diag_width(r0, t) if diag else t
                    sc = s_sc[h, rows, 0:width] * scale
                    if diag:
                        sc = jnp.where(_diag_mask_rows(r0, width), sc, -jnp.inf)
                    pr = jnp.exp(sc - lse_ref[h, rows, :])
                    ds = pr * (dp_sc[h, rows, 0:width] - dl_ref[h, rows, :]) * scale
                    p_sc[h, rows, 0:width] = pr.astype(BF16)
                    ds_sc[h, rows, 0:width] = ds.astype(BF16)
                    if width < t:
                        p_sc[h, rows, width:t] = jnp.zeros((ATT_ROWS, t - width), BF16)
                        ds_sc[h, rows, width:t] = jnp.zeros((ATT_ROWS, t - width), BF16)
            q_rows = pl.ds(pl.multiple_of(i * t, t), t)
            for h in range(ATT_HEADS):
                dv_sc[h] += _dot_tn(p_sc[h], do_ref[:, DN * h:DN * (h + 1)])
                dk_sc[h] += _dot_tn(ds_sc[h], q_ref[h])
                dq_ref[h, q_rows, :] += _dot(ds_sc[h], k_ref[h])

        @pl.when(i > j)
        def _():
            step(False)

        @pl.when(i == j)
        def _():
            step(True)

        @pl.when(i == n - 1)
        def _():
            dk_ref[...] = dk_sc[...]
            dv_ref[...] = dv_sc[...]

    hb = ATT_HEADS
    grid_spec = pltpu.PrefetchScalarGridSpec(
        num_scalar_prefetch=2, grid=(H // hb, int(qi.shape[0])),
        in_specs=[pl.BlockSpec((hb, t, 2 * DN), lambda h, p, qi, ki: (h, qi[p], 0)),
                  pl.BlockSpec((hb, t, 2 * DN), lambda h, p, qi, ki: (h, ki[p], 0)),
                  pl.BlockSpec((hb, t, 2 * DN), lambda h, p, qi, ki: (h, ki[p], 0)),
                  pl.BlockSpec((t, hb * DN), lambda h, p, qi, ki: (qi[p], h)),
                  pl.BlockSpec((hb, t, 1), lambda h, p, qi, ki: (h, qi[p], 0)),
                  pl.BlockSpec((hb, t, 1), lambda h, p, qi, ki: (h, qi[p], 0))],
        out_specs=[pl.BlockSpec((hb, s, 2 * DN), lambda h, p, qi, ki: (h, 0, 0)),
                   pl.BlockSpec((hb, t, 2 * DN), lambda h, p, qi, ki: (h, ki[p], 0)),
                   pl.BlockSpec((hb, t, DN), lambda h, p, qi, ki: (h, ki[p], 0))],
        scratch_shapes=[pltpu.VMEM((hb, t, 2 * DN), F32), pltpu.VMEM((hb, t, DN), F32),
                        pltpu.VMEM((hb, t, t), F32), pltpu.VMEM((hb, t, t), F32),
                        pltpu.VMEM((hb, t, t), BF16), pltpu.VMEM((hb, t, t), BF16)])
    return pl.pallas_call(
        body, grid_spec=grid_spec,
        out_shape=[jax.ShapeDtypeStruct((H, s, 2 * DN), F32), jax.ShapeDtypeStruct((H, s, 2 * DN), F32),
                   jax.ShapeDtypeStruct((H, s, DN), F32)],
        name="attn_bwd", compiler_params=_params(("parallel", "arbitrary"), VMEM_BIG))(
            qi, ki, q, k, v, do, lse, delta)


def _middle(za, o, proj_g, x, tgt, gate, fnw, wco, wao, wo, ts):
    s = x.shape[0]
    inv_d = 1.0 / D

    def body(za_ref, o_ref, bg_ref, ga_ref, gb_ref, x_ref, t_ref, gate_ref, fnw_ref, wco_ref, wao_ref, wo_ref,
             dx2_ref, dza_ref, do_ref, dl_ref, dpg_ref, zb_ref, mg_ref, dmo_ref, dya_ref, dyb_ref, vec_ref):
        @pl.when(pl.program_id(0) == 0)
        def _():
            vec_ref[...] = jnp.zeros_like(vec_ref)

        ov = o_ref[...]
        bg = bg_ref[...]
        sb = _sigmoid(bg)
        silu_b = bg * sb
        zb = (ov * silu_b).astype(BF16)
        zb_ref[...] = zb
        ya = _dot(za_ref[...], wco_ref[...])
        yb = _dot(zb, wao_ref[...])
        sa = _sigmoid(ga_ref[...])
        sg = _sigmoid(gb_ref[...])
        mg = (sa * ya + sg * yb).astype(BF16)
        mg_ref[...] = mg
        mo = _dot(mg, wo_ref[...])
        gate_v = gate_ref[...]
        x2 = x_ref[...] + gate_v * mo
        r = lax.rsqrt(_rowmean(x2 * x2) + EPS)
        xh = x2 * r
        fw = fnw_ref[...]
        e = xh * fw - t_ref[...]
        vec_ref[2:3, :] += _colsum(e * e)
        dy = e * inv_d
        vec_ref[0:1, :] += _colsum(dy * xh)
        dxh = dy * fw
        dx2 = r * (dxh - xh * _rowmean(dxh * xh))
        dx2_ref[...] = dx2
        vec_ref[1:2, :] += _colsum(dx2 * mo)
        dmo = (gate_v * dx2).astype(BF16)
        dmo_ref[...] = dmo
        dmg = _dot_nt(dmo, wo_ref[...])
        dya = (sa * dmg).astype(BF16)
        dyb = (sg * dmg).astype(BF16)
        dya_ref[...] = dya
        dyb_ref[...] = dyb
        dpg_ref[:, D:2 * D] = (dmg * ya * (sa * (1.0 - sa))).astype(BF16)
        dpg_ref[:, 2 * D:3 * D] = (dmg * yb * (sg * (1.0 - sg))).astype(BF16)
        dza_ref[...] = _dot_nt(dya, wco_ref[...])
        dzb = _dot_nt(dyb, wao_ref[...])
        dov = dzb * silu_b
        do_ref[...] = dov.astype(BF16)
        dpg_ref[:, 0:D] = (dzb * ov * _dsilu(bg, sb)).astype(BF16)
        dprod = dov * ov
        for h in range(H):
            dl_ref[h] = jnp.sum(dprod[:, DN * h:DN * (h + 1)], axis=-1, keepdims=True)

    col = lambda c: pl.BlockSpec((ts, D), lambda i, c=c: (i, c))
    row = pl.BlockSpec((ts, D), lambda i: (i, 0))
    vec = pl.BlockSpec((1, D), lambda i: (0, 0))
    wsp = pl.BlockSpec((D, D), lambda i: (0, 0))
    bf = jax.ShapeDtypeStruct((s, D), BF16)
    ff = jax.ShapeDtypeStruct((s, D), F32)
    return pl.pallas_call(
        body, grid=(s // ts,),
        in_specs=[row, row, col(0), col(1), col(2), row, row, vec, vec, wsp, wsp, wsp],
        out_specs=[row, row, row, pl.BlockSpec((H, ts, 1), lambda i: (0, i, 0)),
                   pl.BlockSpec((ts, G_COLS), lambda i: (i, 0)), row, row, row, row, row,
                   pl.BlockSpec((8, D), lambda i: (0, 0))],
        out_shape=[ff, ff, bf, jax.ShapeDtypeStruct((H, s, 1), F32), jax.ShapeDtypeStruct((s, G_COLS), BF16),
                   bf, bf, bf, bf, bf, jax.ShapeDtypeStruct((8, D), F32)],
        name="middle", compiler_params=_params(("arbitrary",), VMEM_BIG))(
            za, o, proj_g, proj_g, proj_g, x, tgt, gate, fnw, wco, wao, wo)


def _input_bwd(dpa, dpl, dpg, wa, wl, wg, x, dx2, norm_w, scale, ts):
    s = x.shape[0]

    def body(dpa_ref, dpl_ref, dpg_ref, wa_ref, wl_ref, wg_ref, x_ref, dx2_ref, nw_ref, sc_ref, gx_ref, gv_ref):
        @pl.when(pl.program_id(0) == 0)
        def _():
            gv_ref[...] = jnp.zeros_like(gv_ref)

        dh = (_dot_nt(dpa_ref[...], wa_ref[...]) + _dot_nt(dpl_ref[...], wl_ref[...])
              + _dot_nt(dpg_ref[...], wg_ref[...]))
        xv = x_ref[...]
        r = lax.rsqrt(_rowmean(xv * xv) + EPS)
        xh = xv * r
        nw = nw_ref[...]
        gv_ref[0:1, :] += _colsum(dh)
        gv_ref[1:2, :] += _colsum(dh * (xh * nw))
        dy = dh * (1.0 + sc_ref[...])
        gv_ref[2:3, :] += _colsum(dy * xh)
        dxh = dy * nw
        gx_ref[...] = dx2_ref[...] + r * (dxh - xh * _rowmean(dxh * xh))

    const = lambda shape: pl.BlockSpec(shape, lambda i: (0, 0))
    rowb = lambda w: pl.BlockSpec((ts, w), lambda i: (i, 0))
    return pl.pallas_call(
        body, grid=(s // ts,),
        in_specs=[rowb(A_COLS), rowb(L_COLS), rowb(G_COLS), const((D, A_COLS)), const((D, L_COLS)),
                  const((D, G_COLS)), rowb(D), rowb(D), const((1, D)), const((1, D))],
        out_specs=[rowb(D), const((8, D))],
        out_shape=[jax.ShapeDtypeStruct((s, D), F32), jax.ShapeDtypeStruct((8, D), F32)],
        name="input_bwd", compiler_params=_params(("arbitrary",), VMEM_BIG))(
            dpa, dpl, dpg, wa, wl, wg, x, dx2, norm_w, scale)


def _adamw(w, g, m, v, tr, name):
    rows, cols = w.shape
    c1 = 1.0 - ADAM_B1 ** ADAM_STEP
    c2 = 1.0 - ADAM_B2 ** ADAM_STEP

    def body(w_ref, g_ref, m_ref, v_ref, d_ref, nm_ref, nv_ref):
        gv = g_ref[...]
        nm = ADAM_B1 * m_ref[...] + (1.0 - ADAM_B1) * gv
        nv = ADAM_B2 * v_ref[...] + (1.0 - ADAM_B2) * (gv * gv)
        nm_ref[...] = nm
        nv_ref[...] = nv
        d_ref[...] = -ADAM_LR * ((nm / c1) / (jnp.sqrt(nv / c2) + ADAM_EPS) + ADAM_WD * w_ref[...])

    blk = pl.BlockSpec((tr, cols), lambda i: (i, 0))
    shp = jax.ShapeDtypeStruct((rows, cols), F32)
    return pl.pallas_call(
        body, grid=(rows // tr,), in_specs=[blk] * 4, out_specs=[blk] * 3, out_shape=[shp] * 3, name=name,
        compiler_params=_params(("parallel",)))(w, g, m, v)


def _ada_fwd(c_all, w_ada_shard, b_ada_shard):
    def body(c_ref, w_ref, b_ref, o_ref):
        cv = c_ref[...]
        o_ref[...] = jnp.dot(cv * _sigmoid(cv), w_ref[...], preferred_element_type=F32,
                             precision=lax.Precision.HIGHEST) + b_ref[...]

    return pl.pallas_call(
        body, out_shape=jax.ShapeDtypeStruct((N_DEV, w_ada_shard.shape[1]), F32), name="ada_fwd")(
            c_all, w_ada_shard, b_ada_shard)


def _ada_bwd(c_all_t, dmod_shard):
    def body(c_ref, d_ref, o_ref):
        cv = c_ref[...]
        o_ref[...] = jnp.dot(cv * _sigmoid(cv), d_ref[...], preferred_element_type=F32,
                             precision=lax.Precision.HIGHEST)

    return pl.pallas_call(
        body, out_shape=jax.ShapeDtypeStruct((D, dmod_shard.shape[1]), F32), name="ada_bwd")(c_all_t, dmod_shard)


def _sum_slabs(stack, tr, name):
    n, rows, cols = stack.shape

    def body(s_ref, o_ref):
        acc = s_ref[0]
        for k in range(1, n):
            acc = acc + s_ref[k]
        o_ref[...] = acc

    return pl.pallas_call(
        body, grid=(rows // tr,), in_specs=[pl.BlockSpec((n, tr, cols), lambda i: (0, i, 0))],
        out_specs=pl.BlockSpec((tr, cols), lambda i: (i, 0)), out_shape=jax.ShapeDtypeStruct((rows, cols), F32),
        name=name, compiler_params=_params(("parallel",)))(stack)


def _sum_chip_slabs(arrived, part, place, tr, name):
    n, rows, cols = arrived.shape
    per = rows // tr

    def body(place_ref, a_ref, p_ref, o_ref):
        acc = p_ref[0].astype(F32)
        for k in range(n):
            acc = acc + a_ref[k].astype(F32)
        o_ref[...] = acc

    grid_spec = pltpu.PrefetchScalarGridSpec(
        num_scalar_prefetch=1, grid=(per,),
        in_specs=[pl.BlockSpec((n, tr, cols), lambda i, pc: (0, i, 0)),
                  pl.BlockSpec((1, tr, cols), lambda i, pc: (pc[0], i, 0))],
        out_specs=pl.BlockSpec((tr, cols), lambda i, pc: (pc[1] * per + i, 0)))
    return pl.pallas_call(
        body, grid_spec=grid_spec, out_shape=jax.ShapeDtypeStruct((2 * rows, cols), F32), name=name,
        compiler_params=_params(("parallel",)))(place, arrived, part)


def _add_own_half(full, other, core, tr, name):
    n, rows, cols = other.shape
    per = rows // tr

    def body(c_ref, f_ref, o_ref, out_ref):
        out_ref[...] = (f_ref[...] + o_ref[...]).astype(BF16)

    grid_spec = pltpu.PrefetchScalarGridSpec(
        num_scalar_prefetch=1, grid=(n, per),
        in_specs=[pl.BlockSpec((1, tr, cols), lambda k, i, c: (k, c[0] * per + i, 0)),
                  pl.BlockSpec((1, tr, cols), lambda k, i, c: (k, i, 0))],
        out_specs=pl.BlockSpec((1, tr, cols), lambda k, i, c: (k, i, 0)))
    return pl.pallas_call(
        body, grid_spec=grid_spec, out_shape=jax.ShapeDtypeStruct((n, rows, cols), BF16), name=name,
        compiler_params=_params(("parallel", "parallel")))(core, full, other)


def _coords():
    return lax.axis_index("x"), lax.axis_index("y"), lax.axis_index("c")


def _allgather8(block, src_rows, vmem, name):
    n = block.shape[1]
    m = src_rows
    sliced = block.shape[0] != m

    def body(x_ref, out_ref, send_sems, recv_sems, local_sem):
        x, y, c = _coords()
        me, sibling = (x, y, c), (x, y, 1 - c)
        chips = [(1 - x, y), (x, 1 - y), (1 - x, 1 - y)]
        src = x_ref.at[pl.ds(pl.multiple_of(c * m, 16), m), :] if sliced else x_ref

        def rows(px, py, pc):
            return out_ref.at[pl.ds(pl.multiple_of((4 * px + 2 * py + pc) * m, 8), m), :]

        def copy(k, blk, to, source=None):
            return pltpu.make_async_remote_copy(
                src_ref=rows(*blk) if source is None else source, dst_ref=rows(*blk),
                send_sem=send_sems.at[k], recv_sem=recv_sems.at[k], device_id=to, device_id_type=MESH)

        mine = pltpu.make_async_copy(src, rows(*me), local_sem)
        mine.start()
        first = [copy(0, me, sibling, source=src)]
        first += [copy(1 + j, me, (*chip, c), source=src) for j, chip in enumerate(chips)]
        for cp in first:
            cp.start()
        passed = [copy(4 + j, (*chip, c), sibling) for j, chip in enumerate(chips)]
        for j, chip in enumerate(chips):
            copy(1 + j, (*chip, c), me).wait_recv()
            passed[j].start()
        copy(0, sibling, me).wait_recv()
        for j, chip in enumerate(chips):
            copy(4 + j, (*chip, 1 - c), me).wait_recv()
        for cp in first + passed:
            cp.wait_send()
        mine.wait()

    space = pltpu.VMEM if vmem else pl.ANY
    return pl.pallas_call(
        body, out_shape=jax.ShapeDtypeStruct((N_DEV * m, n), block.dtype),
        in_specs=[pl.BlockSpec(memory_space=space)], out_specs=pl.BlockSpec(memory_space=space),
        scratch_shapes=[pltpu.SemaphoreType.DMA((7,)), pltpu.SemaphoreType.DMA((7,)), pltpu.SemaphoreType.DMA],
        name=name)(block)


HBM_REF = pl.BlockSpec(memory_space=pl.ANY)


def _gather_weights(shards):
    n = len(shards)
    halves = [a.shape[0] // 2 for a in shards]

    def body(*refs):
        x_refs, out_refs = refs[:n], refs[n:2 * n]
        send_sems, recv_sems, local_sems = refs[2 * n:]
        x, y, c = _coords()
        me, sibling = (x, y, c), (x, y, 1 - c)
        chips = [(1 - x, y), (x, 1 - y), (1 - x, 1 - y)]

        def src(a):
            return x_refs[a].at[pl.ds(pl.multiple_of(c * halves[a], 16), halves[a]), :]

        def blk(a, px, py, pc):
            return out_refs[a].at[4 * px + 2 * py + pc]

        def copy(a, k, who, to, source=None):
            return pltpu.make_async_remote_copy(
                src_ref=blk(a, *who) if source is None else source, dst_ref=blk(a, *who),
                send_sem=send_sems.at[7 * a + k], recv_sem=recv_sems.at[7 * a + k], device_id=to,
                device_id_type=MESH)

        mine = [pltpu.make_async_copy(src(a), blk(a, *me), local_sems.at[a]) for a in range(n)]
        for cp in mine:
            cp.start()
        started = []
        for a in range(n):
            started.append(copy(a, 0, me, sibling, source=src(a)))
            started += [copy(a, 1 + j, me, (*chip, c), source=src(a)) for j, chip in enumerate(chips)]
        for cp in started:
            cp.start()
        for j, chip in enumerate(chips):
            for a in range(n):
                copy(a, 1 + j, (*chip, c), me).wait_recv()
                onward = copy(a, 4 + j, (*chip, c), sibling)
                onward.start()
                started.append(onward)
        for a in range(n):
            copy(a, 0, sibling, me).wait_recv()
        for j, chip in enumerate(chips):
            for a in range(n):
                copy(a, 4 + j, (*chip, 1 - c), me).wait_recv()
        for cp in started:
            cp.wait_send()
        for cp in mine:
            cp.wait()

    outs = pl.pallas_call(
        body, out_shape=[jax.ShapeDtypeStruct((N_DEV, h, a.shape[1]), a.dtype) for a, h in zip(shards, halves)],
        in_specs=[HBM_REF] * n, out_specs=[HBM_REF] * n,
        scratch_shapes=[pltpu.SemaphoreType.DMA((7 * n,)), pltpu.SemaphoreType.DMA((7 * n,)),
                        pltpu.SemaphoreType.DMA((n,))],
        name="gather_weights")(*shards)
    return [o.reshape(N_CHIP, a.shape[0], a.shape[1]) for o, a in zip(outs, shards)]


def _swap_halves_with_sibling(fulls):
    n = len(fulls)
    halves = [a.shape[1] // 2 for a in fulls]

    def body(*refs):
        f_refs, got_refs = refs[:n], refs[n:2 * n]
        send_sems, recv_sems = refs[2 * n:]
        x, y, c = _coords()
        copies = []
        for a in range(n):
            src = f_refs[a].at[:, pl.ds(pl.multiple_of((1 - c) * halves[a], 8), halves[a]), :]
            copies.append(pltpu.make_async_remote_copy(
                src_ref=src, dst_ref=got_refs[a], send_sem=send_sems.at[a], recv_sem=recv_sems.at[a],
                device_id=(x, y, 1 - c), device_id_type=MESH))
        for cp in copies:
            cp.start()
        for cp in copies:
            cp.wait()

    return pl.pallas_call(
        body, out_shape=[jax.ShapeDtypeStruct((a.shape[0], h, a.shape[2]), a.dtype) for a, h in zip(fulls, halves)],
        in_specs=[HBM_REF] * n, out_specs=[HBM_REF] * n,
        scratch_shapes=[pltpu.SemaphoreType.DMA((n,)), pltpu.SemaphoreType.DMA((n,))],
        name="rs_pair_swap")(*fulls)


def _scatter_to_chips(parts):
    n = len(parts)

    def body(*refs):
        p_refs, got_refs = refs[:n], refs[n:2 * n]
        send_sems, recv_sems = refs[2 * n:]
        x, y, c = _coords()
        chips = [(1 - x, y), (x, 1 - y), (1 - x, 1 - y)]

        def copy(a, j):
            px, py = chips[j]
            return pltpu.make_async_remote_copy(
                src_ref=p_refs[a].at[2 * px + py], dst_ref=got_refs[a].at[j], send_sem=send_sems.at[3 * a + j],
                recv_sem=recv_sems.at[3 * a + j], device_id=(px, py, c), device_id_type=MESH)

        copies = [copy(a, j) for a in range(n) for j in range(3)]
        for cp in copies:
            cp.start()
        for cp in copies:
            cp.wait()

    return pl.pallas_call(
        body, out_shape=[jax.ShapeDtypeStruct((3,) + a.shape[1:], a.dtype) for a in parts],
        in_specs=[HBM_REF] * n, out_specs=[HBM_REF] * n,
        scratch_shapes=[pltpu.SemaphoreType.DMA((3 * n,)), pltpu.SemaphoreType.DMA((3 * n,))],
        name="rs_chip_scatter")(*parts)


def _join_halves_with_sibling(wholes):
    n = len(wholes)

    def body(*refs):
        out_refs = refs[n:2 * n]
        send_sems, recv_sems = refs[2 * n:]
        x, y, c = _coords()

        def push(a, core):
            rows = wholes[a].shape[0] // 2
            half = out_refs[a].at[pl.ds(pl.multiple_of(core * rows, 8), rows), :]
            return pltpu.make_async_remote_copy(
                src_ref=half, dst_ref=half, send_sem=send_sems.at[a], recv_sem=recv_sems.at[a],
                device_id=(x, y, 1 - c), device_id_type=MESH)

        for a in range(n):
            push(a, c).start()
        for a in range(n):
            push(a, 1 - c).wait_recv()
        for a in range(n):
            push(a, c).wait_send()

    return pl.pallas_call(
        body, out_shape=[jax.ShapeDtypeStruct(a.shape, a.dtype) for a in wholes],
        in_specs=[HBM_REF] * n, out_specs=[HBM_REF] * n, input_output_aliases={a: a for a in range(n)},
        scratch_shapes=[pltpu.SemaphoreType.DMA((n,)), pltpu.SemaphoreType.DMA((n,))],
        name="rs_pair_join")(*wholes)


def _cols_to_slabs(g):
    rows, cols = g.shape
    return g.reshape(rows, N_CHIP, cols // N_CHIP).transpose(1, 0, 2)


def _slabs_to_cols(w):
    n, rows, cols = w.shape
    return w.transpose(1, 0, 2).reshape(rows, n * cols)


def _uq_to_padded(w_uq):
    per = w_uq.reshape(RQ, H, DN + DR)
    nope = per[:, :, :DN].reshape(RQ, H * DN)
    rope = jnp.pad(per[:, :, DN:], ((0, 0), (0, 0), (0, LANE - DR))).reshape(RQ, H * LANE)
    return jnp.concatenate([nope, rope], axis=1)


def _uq_from_padded(g):
    nope = g[:, :H * DN].reshape(RQ, H, DN)
    rope = g[:, H * DN:].reshape(RQ, H, LANE)[:, :, :DR]
    return jnp.concatenate([nope, rope], axis=2).reshape(RQ, H * (DN + DR))


def _rope_tables(positions):
    inv_freq = ROPE_THETA ** (-jnp.arange(0, DR, 2, dtype=F32) / DR)
    ang = positions.astype(F32)[:, None] * inv_freq
    cos, sin = jnp.cos(ang), jnp.sin(ang)
    return jnp.tile(cos, (1, 4)), jnp.tile(jnp.concatenate([-sin, sin], axis=1), (1, 2))


def _local_step(x, tgt, cos_t, sin_t, mod, weights, small, tiles):
    ts, tm, t_attn, chunk = tiles
    wa, wl, wg, w_uq2, w_ukv, wco, wao, wo, conv_w = weights
    norm_w, conv_b, ln_w, ln_b, q_norm_w, kv_norm_w, fnw = small
    shift, scale, gate = mod[:, 0:D], mod[:, D:2 * D], mod[:, 2 * D:3 * D]

    h = _adaln_norm(x, norm_w, shift, scale, ts)
    proj_a = _mm_nn(h, wa, tm, D, "proj_a")
    proj_l = _mm_nn(h, wl, tm, L_COLS, "proj_l")
    proj_g = _mm_nn(h, wg, tm, D, "proj_g")
    u0, u1, za = _conv_fwd(proj_a, conv_w, conv_b, ln_w, ln_b, ts, chunk)
    qn, kvn, q, k, v = _mla_prep(proj_l, q_norm_w, kv_norm_w, w_uq2, w_ukv, cos_t, sin_t, ts)
    o, lse = _attn_fwd(q, k, v, t_attn)
    (dx2, dza, do, delta, dpg, zb, mg, dmo, dya, dyb, vec_mid) = _middle(
        za, o, proj_g, x, tgt, gate, fnw, wco, wao, wo, ts)
    g_wo = _mm_tn(mg, dmo, tm, D, "grad_w_out")
    g_wco = _mm_tn(za, dya, tm, D, "grad_w_conv_out")
    g_wao = _mm_tn(zb, dyb, tm, D, "grad_w_attn_out")
    dq, dk, dv = _attn_bwd(q, k, v, do, lse, delta, t_attn)
    dpl, g_wuq2, g_wukv, vec_mla = _mla_prep_bwd(
        dq, dk, dv, proj_l, qn, kvn, q_norm_w, kv_norm_w, w_uq2, w_ukv, cos_t, sin_t, ts)
    dpa, g_conv_w, vec_conv = _conv_bwd(dza, proj_a, u0, u1, conv_w, ln_w, ln_b, ts, chunk)
    grad_x, vec_in = _input_bwd(dpa, dpl, dpg, wa, wl, wg, x, dx2, norm_w, scale, ts)
    g_wa = _mm_tn(h, dpa, tm, D, "grad_w_in_a")
    g_wl = _mm_tn(h, dpl, tm, L_COLS, "grad_w_in_l")
    g_wg = _mm_tn(h, dpg, tm, D, "grad_w_in_g")

    dmod = jnp.concatenate([vec_in[0:1], vec_in[1:2], vec_mid[1:2]], axis=1)
    sums = dict(dmod=dmod, norm_w=vec_in[2:3], conv_b=vec_conv[2:3], ln_w=vec_conv[0:1], ln_b=vec_conv[1:2],
                q_norm_w=vec_mla[0:1], kv_norm_w=vec_mla[1:2], final_norm_w=vec_mid[0:1], loss=vec_mid[2:3],
                conv_w=g_conv_w)
    grads = dict(wa=g_wa, wl=g_wl, wg=g_wg, w_uq2=g_wuq2, w_ukv=g_wukv, wco=g_wco, wao=g_wao, wo=g_wo)
    return grad_x, grads, sums


SMALL_ORDER = (("dmod", 3 * D), ("norm_w", D), ("conv_b", D), ("ln_w", D), ("ln_b", D), ("q_norm_w", RQ),
               ("kv_norm_w", RQ), ("final_norm_w", D), ("loss", D), ("conv_w", HALO * D))
SMALL_ROWS = 336


def kernel(x, c, positions, w_ada, b_ada, norm_w, w_in, conv_w, conv_b, conv_ln_w, conv_ln_b, w_conv_out, q_norm_w, w_uq, kv_norm_w, w_ukv, w_attn_out, w_out, final_norm_w, loss_target, m_w_ada, m_b_ada, m_norm_w, m_w_in, m_conv_w, m_conv_b, m_conv_ln_w, m_conv_ln_b, m_w_conv_out, m_q_norm_w, m_w_uq, m_kv_norm_w, m_w_ukv, m_w_attn_out, m_w_out, m_final_norm_w, v_w_ada, v_b_ada, v_norm_w, v_w_in, v_conv_w, v_conv_b, v_conv_ln_w, v_conv_ln_b, v_w_conv_out, v_q_norm_w, v_w_uq, v_kv_norm_w, v_w_ukv, v_w_attn_out, v_w_out, v_final_norm_w):
    ix, iy, ic = _coords()
    chip = 2 * ix + iy
    dev = 4 * ix + 2 * iy + ic
    s = x.shape[1]
    tiles = (256, 512, 512, 32)

    conv_w_pad = jnp.pad(conv_w[0], ((0, HALO - KC), (0, 0)))
    small_in = jnp.concatenate([c.reshape(8, LANE), conv_w_pad.reshape(64, LANE)], axis=0)
    small_all = _allgather8(small_in, 72, True, "gather_c_conv").reshape(N_DEV, 72, LANE)
    c_all = small_all[:, 0:8].reshape(N_DEV, D)
    conv_full = jnp.concatenate(
        [small_all[2 * k, 8:72].reshape(HALO, D // N_CHIP) for k in range(N_CHIP)], axis=1)

    shards = [w[0].astype(BF16) for w in (w_in, w_uq, w_ukv, w_conv_out, w_attn_out, w_out)]
    g_in, g_uq, g_ukv, g_co, g_ao, g_o = _gather_weights(shards)
    w_in_f, w_uq_f, w_ukv_f = _slabs_to_cols(g_in), _slabs_to_cols(g_uq), _slabs_to_cols(g_ukv)
    wco, wao, wo = g_co.reshape(D, D), g_ao.reshape(D, D), g_o.reshape(D, D)
    wa = w_in_f[:, 0:A_COLS]
    wl = jnp.pad(w_in_f[:, A_COLS:A_COLS + L_COLS_RAW], ((0, 0), (0, L_COLS - L_COLS_RAW)))
    wg = w_in_f[:, A_COLS + L_COLS_RAW:]
    weights = (wa, wl, wg, _uq_to_padded(w_uq_f), w_ukv_f, wco, wao, wo, conv_full)

    ada_cols = w_ada.shape[2]
    b_shard = lax.dynamic_slice(b_ada, (0, chip * ada_cols), (1, ada_cols))
    mod_part = _ada_fwd(c_all, w_ada[0], b_shard)
    mod_all = _allgather8(mod_part, N_DEV, True, "gather_mod").reshape(N_DEV, N_DEV, ada_cols)
    mod = jnp.concatenate(
        [lax.dynamic_slice(mod_all[2 * k], (dev, 0), (1, ada_cols)) for k in range(N_CHIP)], axis=1)

    cos_t, sin_t = _rope_tables(positions[0])
    small = (norm_w, conv_b, conv_ln_w, conv_ln_b, q_norm_w, kv_norm_w, final_norm_w.reshape(1, D))
    grad_x, grads, sums = _local_step(x[0], loss_target[0], cos_t, sin_t, mod, weights, small, tiles)

    small_flat = jnp.concatenate([sums[name].reshape(-1) for name, _ in SMALL_ORDER])
    small_flat = jnp.pad(small_flat, (0, SMALL_ROWS * LANE - small_flat.shape[0]))
    small_g = _allgather8(small_flat.reshape(SMALL_ROWS, LANE), SMALL_ROWS, True, "gather_small_grads")
    small_g = small_g.reshape(N_DEV, SMALL_ROWS, LANE)
    small_sum = _sum_slabs(small_g, SMALL_ROWS, "sum_small_grads").reshape(-1)
    tot, pos = {}, 0
    for name, size in SMALL_ORDER:
        tot[name] = small_sum[pos:pos + size]
        pos += size
    loss = (0.5 / D) * jnp.sum(tot["loss"])
    dmod_all = small_g.reshape(N_DEV, -1)[:, 0:3 * D]
    g_b_ada = tot["dmod"].reshape(1, 3 * D)
    dmod_shard = lax.dynamic_slice(dmod_all, (0, chip * ada_cols), (N_DEV, ada_cols))
    g_w_ada = _ada_bwd(c_all.T, dmod_shard).reshape(1, D, ada_cols)
    g_conv_w = lax.dynamic_slice(tot["conv_w"].reshape(HALO, D), (0, chip * (D // N_CHIP)), (KC, D // N_CHIP))
    g_conv_w = g_conv_w.reshape(1, KC, D // N_CHIP)

    g_w_in = jnp.concatenate([grads["wa"], grads["wl"][:, 0:L_COLS_RAW], grads["wg"]], axis=1)
    nr = D // N_CHIP
    fulls = [_cols_to_slabs(g_w_in), _cols_to_slabs(_uq_from_padded(grads["w_uq2"])), _cols_to_slabs(grads["w_ukv"]),
             grads["wco"].reshape(N_CHIP, nr, D), grads["wao"].reshape(N_CHIP, nr, D),
             grads["wo"].reshape(N_CHIP, nr, D)]
    from_sibling = _swap_halves_with_sibling(fulls)
    core = ic.reshape(1).astype(jnp.int32)
    chip_sums = [_add_own_half(f, o, core, min(256, o.shape[1]), f"add_own_half_{n}")
                 for n, (f, o) in enumerate(zip(fulls, from_sibling))]
    arrived = _scatter_to_chips(chip_sums)
    place = jnp.stack([chip, ic]).astype(jnp.int32)
    wholes = [_sum_chip_slabs(a, p, place, min(128, a.shape[1]), f"sum_chip_slabs_{n}")
              for n, (a, p) in enumerate(zip(arrived, chip_sums))]
    g_w_in_s, g_w_uq_s, g_w_ukv_s, g_wco_s, g_wao_s, g_wo_s = _join_halves_with_sibling(wholes)

    def big(w, g, m, v, tr, name):
        d, nm, nv = _adamw(w[0], g, m[0], v[0], tr, name)
        return g[None], d[None], nm[None], nv[None]

    vec_names = ("b_ada", "norm_w", "conv_b", "conv_ln_w", "conv_ln_b", "q_norm_w", "kv_norm_w", "final_norm_w")
    vec_w = (b_ada, norm_w, conv_b, conv_ln_w, conv_ln_b, q_norm_w, kv_norm_w, final_norm_w)
    vec_m = (m_b_ada, m_norm_w, m_conv_b, m_conv_ln_w, m_conv_ln_b, m_q_norm_w, m_kv_norm_w, m_final_norm_w)
    vec_v = (v_b_ada, v_norm_w, v_conv_b, v_conv_ln_w, v_conv_ln_b, v_q_norm_w, v_kv_norm_w, v_final_norm_w)
    vec_g = (g_b_ada, tot["norm_w"], tot["conv_b"], tot["ln_w"], tot["ln_b"], tot["q_norm_w"], tot["kv_norm_w"],
             tot["final_norm_w"])
    vec_g = tuple(g.reshape(w.shape) for g, w in zip(vec_g, vec_w))
    cat = lambda arrs: jnp.concatenate([a.reshape(-1) for a in arrs]).reshape(-1, LANE)
    vd, vnm, vnv = _adamw(cat(vec_w), cat(vec_g), cat(vec_m), cat(vec_v), cat(vec_w).shape[0], "adamw_vectors")

    def split(packed):
        flat, out, pos = packed.reshape(-1), [], 0
        for w in vec_w:
            out.append(flat[pos:pos + w.size].reshape(w.shape))
            pos += w.size
        return out

    res = {}
    for name, g, d, nm, nv in zip(vec_names, vec_g, split(vd), split(vnm), split(vnv)):
        res[name] = (g, d, nm, nv)
    res["w_ada"] = big(w_ada, g_w_ada[0], m_w_ada, v_w_ada, 256, "adamw_w_ada")
    res["w_in"] = big(w_in, g_w_in_s, m_w_in, v_w_in, 256, "adamw_w_in")
    res["conv_w"] = big(conv_w, g_conv_w[0], m_conv_w, v_conv_w, KC, "adamw_conv_w")
    res["w_conv_out"] = big(w_conv_out, g_wco_s, m_w_conv_out, v_w_conv_out, 256, "adamw_w_conv_out")
    res["w_uq"] = big(w_uq, g_w_uq_s, m_w_uq, v_w_uq, 256, "adamw_w_uq")
    res["w_ukv"] = big(w_ukv, g_w_ukv_s, m_w_ukv, v_w_ukv, 256, "adamw_w_ukv")
    res["w_attn_out"] = big(w_attn_out, g_wao_s, m_w_attn_out, v_w_attn_out, 256, "adamw_w_attn_out")
    res["w_out"] = big(w_out, g_wo_s, m_w_out, v_w_out, 256, "adamw_w_out")

    order = ("w_ada", "b_ada", "norm_w", "w_in", "conv_w", "conv_b", "conv_ln_w", "conv_ln_b", "w_conv_out",
             "q_norm_w", "w_uq", "kv_norm_w", "w_ukv", "w_attn_out", "w_out", "final_norm_w")
    outs = [loss, grad_x[None]]
    for slot in range(4):
        outs += [res[name][slot] for name in order]
    return tuple(outs)
```

```python
import functools

import numpy as np
import jax
import jax.numpy as jnp
from jax import lax
from jax.experimental import pallas as pl
from jax.experimental.pallas import tpu as pltpu

F32 = jnp.float32
BF16 = jnp.bfloat16
MESH = pl.DeviceIdType.MESH

D = 1024
H = 8
DN = 128
DR = 64
RQ = 256
KC = 31
HALO = 32
EPS = 1e-6
ROPE_THETA = 10000.0
N_CHIP = 4
N_DEV = 8
LANE = 128
VMEM_BIG = 56 * 1024 * 1024

ADAM_LR = 0.001
ADAM_B1 = 0.9
ADAM_B2 = 0.999
ADAM_EPS = 1e-08
ADAM_WD = 0.01
ADAM_STEP = 10

A_COLS = 3 * D
L_COLS_RAW = RQ + RQ + DR
L_COLS = 640
G_COLS = 3 * D
IN_COLS = A_COLS + L_COLS_RAW + G_COLS


def _params(sem=None, vmem=None):
    kw = {}
    if sem is not None:
        kw["dimension_semantics"] = sem
    if vmem is not None:
        kw["vmem_limit_bytes"] = vmem
    return pltpu.CompilerParams(**kw)


def _dot(a, b):
    return jnp.dot(a, b, preferred_element_type=F32)


def _dot_nt(a, b):
    return lax.dot_general(a, b, (((1,), (1,)), ((), ())), preferred_element_type=F32)


def _dot_tn(a, b):
    return lax.dot_general(a, b, (((0,), (0,)), ((), ())), preferred_element_type=F32)


def _colsum(v):
    return jnp.sum(v, axis=0, keepdims=True)


def _rowmean(v):
    return jnp.mean(v, axis=-1, keepdims=True)


def _sigmoid(v):
    return jax.nn.sigmoid(v)


def _dsilu(v, s):
    return s * (1.0 + v * (1.0 - s))


def _swap_halves(v, first_half):
    return jnp.where(first_half, pltpu.roll(v, 96, 1), pltpu.roll(v, 32, 1))


def _first_half_mask(rows):
    lane = lax.broadcasted_iota(jnp.int32, (rows, LANE), 1)
    return (lane % 64) < 32


def _adaln_norm(x, norm_w, shift, scale, ts):
    s = x.shape[0]

    def body(x_ref, nw_ref, sh_ref, sc_ref, h_ref):
        xv = x_ref[...]
        r = lax.rsqrt(_rowmean(xv * xv) + EPS)
        y = xv * r * nw_ref[...]
        h_ref[...] = (y * (1.0 + sc_ref[...]) + sh_ref[...]).astype(BF16)

    row = pl.BlockSpec((ts, D), lambda i: (i, 0))
    vec = pl.BlockSpec((1, D), lambda i: (0, 0))
    return pl.pallas_call(
        body, grid=(s // ts,), in_specs=[row, vec, vec, vec], out_specs=row,
        out_shape=jax.ShapeDtypeStruct((s, D), BF16), name="adaln_norm",
        compiler_params=_params(("parallel",)))(x, norm_w, shift, scale)


def _mm_nn(a, b, tm, tn, name):
    m, k = a.shape
    n = b.shape[1]

    def body(a_ref, b_ref, o_ref):
        o_ref[...] = _dot(a_ref[...], b_ref[...])

    return pl.pallas_call(
        body, grid=(n // tn, m // tm),
        in_specs=[pl.BlockSpec((tm, k), lambda j, i: (i, 0)), pl.BlockSpec((k, tn), lambda j, i: (0, j))],
        out_specs=pl.BlockSpec((tm, tn), lambda j, i: (i, j)),
        out_shape=jax.ShapeDtypeStruct((m, n), F32), name=name,
        compiler_params=_params(("parallel", "parallel"), VMEM_BIG))(a, b)


def _mm_tn(a, b, tm, tn, name):
    m, k = a.shape
    n = b.shape[1]

    def body(a_ref, b_ref, o_ref):
        @pl.when(pl.program_id(1) == 0)
        def _():
            o_ref[...] = jnp.zeros_like(o_ref)
        o_ref[...] += _dot_tn(a_ref[...], b_ref[...])

    return pl.pallas_call(
        body, grid=(n // tn, m // tm),
        in_specs=[pl.BlockSpec((tm, k), lambda j, i: (i, 0)), pl.BlockSpec((tm, tn), lambda j, i: (i, j))],
        out_specs=pl.BlockSpec((k, tn), lambda j, i: (0, j)),
        out_shape=jax.ShapeDtypeStruct((k, n), F32), name=name,
        compiler_params=_params(("parallel", "arbitrary"), VMEM_BIG))(a, b)


def _shifted_copies(win_ref, sh_ref, rows):
    for p in range(1, 8):
        sh_ref[p - 1, 0:rows, :] = win_ref[pl.ds(p, rows), :]


def _tap_rows(win_ref, sh_ref, start, rows):
    p = start % 8
    if p == 0:
        return win_ref[pl.ds(start, rows), :]
    return sh_ref[p - 1, pl.ds(start - p, rows), :]


def _conv_taps(win_ref, sh_ref, w_ref, rows, chunk, offset_of_tap):
    pieces = []
    for c0 in range(0, rows, chunk):
        acc = None
        for j in range(KC):
            term = w_ref[j:j + 1, :] * _tap_rows(win_ref, sh_ref, c0 + offset_of_tap(j), chunk)
            acc = term if acc is None else acc + term
        pieces.append(acc)
    return pieces


def _conv_fwd(proj_a, conv_w, conv_b, ln_w, ln_b, ts, chunk):
    s = proj_a.shape[0]

    def body(av_ref, al_ref, ag_ref, w_ref, b_ref, lw_ref, lb_ref, u0_ref, u1_ref, za_ref, win_ref, sh_ref):
        @pl.when(pl.program_id(0) == 0)
        def _():
            win_ref[0:HALO, :] = jnp.zeros((HALO, D), F32)

        u0 = av_ref[...] * _sigmoid(al_ref[...])
        u0_ref[...] = u0
        win_ref[HALO:HALO + ts, :] = u0
        _shifted_copies(win_ref, sh_ref, ts + HALO - 8)
        pieces = _conv_taps(win_ref, sh_ref, w_ref, ts, chunk, lambda j: HALO - (KC - 1) + j)
        for n, acc in enumerate(pieces):
            u1_ref[n * chunk:(n + 1) * chunk, :] = acc + b_ref[...]
        win_ref[0:HALO, :] = win_ref[ts:ts + HALO, :]

        u1 = u1_ref[...]
        xc = u1 - _rowmean(u1)
        rstd = lax.rsqrt(_rowmean(xc * xc) + EPS)
        u2 = xc * rstd * lw_ref[...] + lb_ref[...]
        u3 = u2 * _sigmoid(u2)
        ag = ag_ref[...]
        za_ref[...] = (u3 * (ag * _sigmoid(ag))).astype(BF16)

    col = lambda c: pl.BlockSpec((ts, D), lambda i, c=c: (i, c))
    row = pl.BlockSpec((ts, D), lambda i: (i, 0))
    vec = pl.BlockSpec((1, D), lambda i: (0, 0))
    return pl.pallas_call(
        body, grid=(s // ts,),
        in_specs=[col(0), col(1), col(2), pl.BlockSpec((HALO, D), lambda i: (0, 0)), vec, vec, vec],
        out_specs=[row, row, row],
        out_shape=[jax.ShapeDtypeStruct((s, D), F32), jax.ShapeDtypeStruct((s, D), F32),
                   jax.ShapeDtypeStruct((s, D), BF16)],
        scratch_shapes=[pltpu.VMEM((ts + HALO, D), F32), pltpu.VMEM((7, ts + HALO, D), F32)], name="conv_fwd",
        compiler_params=_params(("arbitrary",), VMEM_BIG))(proj_a, proj_a, proj_a, conv_w, conv_b, ln_w, ln_b)


def _conv_bwd(dza, proj_a, u0, u1, conv_w, ln_w, ln_b, ts, chunk):
    s = dza.shape[0]
    nt = s // ts
    per = ts // HALO

    def body(dza_ref, av_ref, al_ref, ag_ref, u0_ref, u0p_ref, u1_ref, w_ref, lw_ref, lb_ref,
             dpa_ref, gw_ref, gv_ref, dwin_ref, uwin_ref, du0_ref, gwp_ref, dsh_ref, ush_ref):
        step = pl.program_id(0)
        tile = nt - 1 - step

        @pl.when(step == 0)
        def _():
            dwin_ref[ts:ts + HALO, :] = jnp.zeros((HALO, D), F32)
            gwp_ref[...] = jnp.zeros_like(gwp_ref)
            gv_ref[...] = jnp.zeros_like(gv_ref)

        ag = ag_ref[...]
        sg = _sigmoid(ag)
        u1 = u1_ref[...]
        xc = u1 - _rowmean(u1)
        rstd = lax.rsqrt(_rowmean(xc * xc) + EPS)
        xh = xc * rstd
        u2 = xh * lw_ref[...] + lb_ref[...]
        s2 = _sigmoid(u2)
        dz = dza_ref[...]
        du3 = dz * (ag * sg)
        dpa_ref[:, 2 * D:3 * D] = (dz * (u2 * s2) * _dsilu(ag, sg)).astype(BF16)
        du2 = du3 * _dsilu(u2, s2)
        gv_ref[0:1, :] += _colsum(du2 * xh)
        gv_ref[1:2, :] += _colsum(du2)
        dxh = du2 * lw_ref[...]
        du1 = rstd * (dxh - _rowmean(dxh) - xh * _rowmean(dxh * xh))
        gv_ref[2:3, :] += _colsum(du1)
        dwin_ref[0:ts, :] = du1

        uwin_ref[0:HALO, :] = jnp.where(tile == 0, 0.0, u0p_ref[...])
        uwin_ref[HALO:HALO + ts, :] = u0_ref[...]

        _shifted_copies(dwin_ref, dsh_ref, ts + HALO - 8)
        _shifted_copies(uwin_ref, ush_ref, ts + HALO - 8)
        pieces = _conv_taps(dwin_ref, dsh_ref, w_ref, ts, chunk, lambda j: (KC - 1) - j)
        for n, acc in enumerate(pieces):
            du0_ref[n * chunk:(n + 1) * chunk, :] = acc
        for c0 in range(0, ts, chunk):
            dchunk = dwin_ref[c0:c0 + chunk, :]
            for j in range(KC):
                prod = dchunk * _tap_rows(uwin_ref, ush_ref, c0 + HALO - (KC - 1) + j, chunk)
                gwp_ref[8 * j:8 * j + 8, :] += jnp.sum(prod.reshape(chunk // 8, 8, D), axis=0)
        dwin_ref[ts:ts + HALO, :] = dwin_ref[0:HALO, :]

        du0 = du0_ref[...]
        al = al_ref[...]
        sl = _sigmoid(al)
        dpa_ref[:, 0:D] = (du0 * sl).astype(BF16)
        dpa_ref[:, D:2 * D] = (du0 * av_ref[...] * sl * (1.0 - sl)).astype(BF16)

        @pl.when(step == nt - 1)
        def _():
            for j in range(KC):
                gw_ref[j:j + 1, :] = _colsum(gwp_ref[8 * j:8 * j + 8, :])
            gw_ref[KC:HALO, :] = jnp.zeros((HALO - KC, D), F32)

    rev = lambda i: nt - 1 - i
    col = lambda c: pl.BlockSpec((ts, D), lambda i, c=c: (rev(i), c))
    row = pl.BlockSpec((ts, D), lambda i: (rev(i), 0))
    vec = pl.BlockSpec((1, D), lambda i: (0, 0))
    halo = pl.BlockSpec((HALO, D), lambda i: (jnp.maximum(rev(i) * per - 1, 0), 0))
    return pl.pallas_call(
        body, grid=(nt,),
        in_specs=[row, col(0), col(1), col(2), row, halo, row, pl.BlockSpec((HALO, D), lambda i: (0, 0)), vec, vec],
        out_specs=[pl.BlockSpec((ts, A_COLS), lambda i: (rev(i), 0)),
                   pl.BlockSpec((HALO, D), lambda i: (0, 0)), pl.BlockSpec((8, D), lambda i: (0, 0))],
        out_shape=[jax.ShapeDtypeStruct((s, A_COLS), BF16), jax.ShapeDtypeStruct((HALO, D), F32),
                   jax.ShapeDtypeStruct((8, D), F32)],
        scratch_shapes=[pltpu.VMEM((ts + HALO, D), F32), pltpu.VMEM((ts + HALO, D), F32),
                        pltpu.VMEM((ts, D), F32), pltpu.VMEM((8 * HALO, D), F32),
                        pltpu.VMEM((7, ts + HALO, D), F32), pltpu.VMEM((7, ts + HALO, D), F32)],
        name="conv_bwd", compiler_params=_params(("arbitrary",), VMEM_BIG))(
            dza, proj_a, proj_a, proj_a, u0, u0, u1, conv_w, ln_w, ln_b)


def _mla_prep(proj_l, q_norm_w, kv_norm_w, w_uq2, w_ukv, cos_t, sin_t, ts):
    s = proj_l.shape[0]

    def body(pl_ref, qw_ref, kw_ref, wq_ref, wkv_ref, c_ref, s_ref, qn_ref, kvn_ref, q_ref, k_ref, v_ref):
        first = _first_half_mask(ts)
        cs = c_ref[...]
        sn = s_ref[...]

        def rms(v, w):
            return v * lax.rsqrt(_rowmean(v * v) + EPS) * w

        def rope(v):
            return v * cs + _swap_halves(v, first) * sn

        qn = rms(pl_ref[:, 0:RQ], qw_ref[...]).astype(BF16)
        kvn = rms(pl_ref[:, RQ:2 * RQ], kw_ref[...]).astype(BF16)
        qn_ref[...] = qn
        kvn_ref[...] = kvn
        q = _dot(qn, wq_ref[...])
        kv = _dot(kvn, wkv_ref[...])
        kr = rope(pl_ref[:, 2 * RQ:2 * RQ + LANE]).astype(BF16)
        for h in range(H):
            q_ref[h, :, 0:DN] = q[:, DN * h:DN * (h + 1)].astype(BF16)
            q_ref[h, :, DN:2 * DN] = rope(q[:, H * DN + LANE * h:H * DN + LANE * (h + 1)]).astype(BF16)
            k_ref[h, :, 0:DN] = kv[:, 2 * DN * h:2 * DN * h + DN].astype(BF16)
            k_ref[h, :, DN:2 * DN] = kr
            v_ref[h, :, 0:DN] = kv[:, 2 * DN * h + DN:2 * DN * (h + 1)].astype(BF16)
            v_ref[h, :, DN:2 * DN] = jnp.ones((ts, DN), BF16)

    const = lambda shape: pl.BlockSpec(shape, lambda i: (0,) * len(shape))
    rowb = lambda w: pl.BlockSpec((ts, w), lambda i: (i, 0))
    head = lambda w: pl.BlockSpec((H, ts, w), lambda i: (0, i, 0))
    return pl.pallas_call(
        body, grid=(s // ts,),
        in_specs=[rowb(L_COLS), const((1, RQ)), const((1, RQ)), const((RQ, 2 * H * DN)), const((RQ, 2 * H * DN)),
                  rowb(LANE), rowb(LANE)],
        out_specs=[rowb(RQ), rowb(RQ), head(2 * DN), head(2 * DN), head(2 * DN)],
        out_shape=[jax.ShapeDtypeStruct((s, RQ), BF16), jax.ShapeDtypeStruct((s, RQ), BF16),
                   jax.ShapeDtypeStruct((H, s, 2 * DN), BF16), jax.ShapeDtypeStruct((H, s, 2 * DN), BF16),
                   jax.ShapeDtypeStruct((H, s, 2 * DN), BF16)],
        name="mla_prep", compiler_params=_params(("parallel",)))(
            proj_l, q_norm_w, kv_norm_w, w_uq2, w_ukv, cos_t, sin_t)


def _mla_prep_bwd(dq, dk, dv, proj_l, qn, kvn, q_norm_w, kv_norm_w, w_uq2, w_ukv, cos_t, sin_t, ts):
    s = proj_l.shape[0]

    def body(dq_ref, dk_ref, dv_ref, pl_ref, qn_ref, kvn_ref, qw_ref, kw_ref, wq_ref, wkv_ref, c_ref, s_ref,
             dpl_ref, gwq_ref, gwkv_ref, gv_ref, dq2_ref, dkv2_ref):
        @pl.when(pl.program_id(0) == 0)
        def _():
            gwq_ref[...] = jnp.zeros_like(gwq_ref)
            gwkv_ref[...] = jnp.zeros_like(gwkv_ref)
            gv_ref[...] = jnp.zeros_like(gv_ref)

        first = _first_half_mask(ts)
        cs = c_ref[...]
        sn = s_ref[...]

        def rope_bwd(g):
            return g * cs + _swap_halves(g * sn, first)

        def rms_bwd(v, w, dy):
            r = lax.rsqrt(_rowmean(v * v) + EPS)
            vh = v * r
            dvh = dy * w
            return r * (dvh - vh * _rowmean(dvh * vh)), _colsum(dy * vh)

        dkr = None
        for h in range(H):
            dq2_ref[:, DN * h:DN * (h + 1)] = dq_ref[h, :, 0:DN].astype(BF16)
            dq2_ref[:, H * DN + LANE * h:H * DN + LANE * (h + 1)] = rope_bwd(dq_ref[h, :, DN:2 * DN]).astype(BF16)
            dkv2_ref[:, 2 * DN * h:2 * DN * h + DN] = dk_ref[h, :, 0:DN].astype(BF16)
            dkv2_ref[:, 2 * DN * h + DN:2 * DN * (h + 1)] = dv_ref[h].astype(BF16)
            part = dk_ref[h, :, DN:2 * DN]
            dkr = part if dkr is None else dkr + part

        dq2 = dq2_ref[...]
        dkv2 = dkv2_ref[...]
        gwq_ref[...] += _dot_tn(qn_ref[...], dq2)
        gwkv_ref[...] += _dot_tn(kvn_ref[...], dkv2)
        dcq, gq = rms_bwd(pl_ref[:, 0:RQ], qw_ref[...], _dot_nt(dq2, wq_ref[...]))
        dckv, gkv = rms_bwd(pl_ref[:, RQ:2 * RQ], kw_ref[...], _dot_nt(dkv2, wkv_ref[...]))
        gv_ref[0:1, :] += gq
        gv_ref[1:2, :] += gkv
        dpl_ref[:, 0:RQ] = dcq.astype(BF16)
        dpl_ref[:, RQ:2 * RQ] = dckv.astype(BF16)
        dpl_ref[:, 2 * RQ:2 * RQ + LANE] = rope_bwd(dkr).astype(BF16)

    const = lambda shape: pl.BlockSpec(shape, lambda i: (0,) * len(shape))
    rowb = lambda w: pl.BlockSpec((ts, w), lambda i: (i, 0))
    head = lambda w: pl.BlockSpec((H, ts, w), lambda i: (0, i, 0))
    return pl.pallas_call(
        body, grid=(s // ts,),
        in_specs=[head(2 * DN), head(2 * DN), head(DN), rowb(L_COLS), rowb(RQ), rowb(RQ), const((1, RQ)),
                  const((1, RQ)), const((RQ, 2 * H * DN)), const((RQ, 2 * H * DN)), rowb(LANE), rowb(LANE)],
        out_specs=[rowb(L_COLS), const((RQ, 2 * H * DN)), const((RQ, 2 * H * DN)), const((8, RQ))],
        out_shape=[jax.ShapeDtypeStruct((s, L_COLS), BF16), jax.ShapeDtypeStruct((RQ, 2 * H * DN), F32),
                   jax.ShapeDtypeStruct((RQ, 2 * H * DN), F32), jax.ShapeDtypeStruct((8, RQ), F32)],
        scratch_shapes=[pltpu.VMEM((ts, 2 * H * DN), BF16), pltpu.VMEM((ts, 2 * H * DN), BF16)],
        name="mla_prep_bwd", compiler_params=_params(("arbitrary",), VMEM_BIG))(
            dq, dk, dv, proj_l, qn, kvn, q_norm_w, kv_norm_w, w_uq2, w_ukv, cos_t, sin_t)


def _causal_pairs(n, by_key):
    if by_key:
        pairs = [(i, j) for j in range(n) for i in range(j, n)]
    else:
        pairs = [(i, j) for i in range(n) for j in range(i + 1)]
    return (jnp.asarray(np.array([p[0] for p in pairs], np.int32)),
            jnp.asarray(np.array([p[1] for p in pairs], np.int32)))


ATT_HEADS = 2
ATT_ROWS = 64


def _diag_width(r0, t):
    return min(t, -(-(r0 + ATT_ROWS) // LANE) * LANE)


def _diag_mask_rows(r0, width):
    rows = r0 + lax.broadcasted_iota(jnp.int32, (ATT_ROWS, width), 0)
    cols = lax.broadcasted_iota(jnp.int32, (ATT_ROWS, width), 1)
    return cols <= rows


def _diag_mask(t):
    return lax.broadcasted_iota(jnp.int32, (t, t), 1) <= lax.broadcasted_iota(jnp.int32, (t, t), 0)


def _attn_fwd(q, k, v, t):
    s = q.shape[1]
    n = s // t
    scale = float((DN + DR) ** -0.5)
    qi, ki = _causal_pairs(n, by_key=False)

    def body(qi_ref, ki_ref, q_ref, k_ref, v_ref, o_ref, lse_ref, *scratch):
        per_head = [scratch[4 * h:4 * h + 4] for h in range(ATT_HEADS)]
        p = pl.program_id(1)
        i = qi_ref[p]
        j = ki_ref[p]

        @pl.when(j == 0)
        def _():
            for m_sc, acc_sc, _, _ in per_head:
                m_sc[...] = jnp.full_like(m_sc, -jnp.inf)
                acc_sc[...] = jnp.zeros_like(acc_sc)

        def scores(h, diag):
            sc = _dot_nt(q_ref[h], k_ref[h])
            if diag:
                sc = jnp.where(_diag_mask(t), sc, -jnp.inf)
            per_head[h][2][...] = sc

        def softmax(h):
            m_sc, acc_sc, s_sc, p_sc = per_head[h]
            m_prev = m_sc[...]
            m_new = jnp.maximum(m_prev, jnp.max(s_sc[...], axis=-1, keepdims=True) * scale)
            m_sc[...] = m_new
            acc_sc[...] = jnp.exp(m_prev - m_new) * acc_sc[...]
            for r0 in range(0, t, ATT_ROWS):
                rows = slice(r0, r0 + ATT_ROWS)
                p_sc[rows, :] = jnp.exp(s_sc[rows, :] * scale - m_sc[rows, :]).astype(BF16)

        def values(h):
            _, acc_sc, _, p_sc = per_head[h]
            acc_sc[...] += _dot(p_sc[...], v_ref[h])

        def step(diag):
            scores(0, diag)
            for h in range(ATT_HEADS):
                if h + 1 < ATT_HEADS:
                    scores(h + 1, diag)
                softmax(h)
                values(h)

        @pl.when(j < i)
        def _():
            step(False)

        @pl.when(j == i)
        def _():
            step(True)
            for h, (m_sc, acc_sc, _, _) in enumerate(per_head):
                l = acc_sc[:, DN:2 * DN]
                o_ref[:, DN * h:DN * (h + 1)] = acc_sc[:, 0:DN] / l
                lse_ref[h] = m_sc[...] + jnp.log(l[:, 0:1])

    hb = ATT_HEADS
    grid_spec = pltpu.PrefetchScalarGridSpec(
        num_scalar_prefetch=2, grid=(H // hb, int(qi.shape[0])),
        in_specs=[pl.BlockSpec((hb, t, 2 * DN), lambda h, p, qi, ki: (h, qi[p], 0)),
                  pl.BlockSpec((hb, t, 2 * DN), lambda h, p, qi, ki: (h, ki[p], 0)),
                  pl.BlockSpec((hb, t, 2 * DN), lambda h, p, qi, ki: (h, ki[p], 0))],
        out_specs=[pl.BlockSpec((t, hb * DN), lambda h, p, qi, ki: (qi[p], h)),
                   pl.BlockSpec((hb, t, 1), lambda h, p, qi, ki: (h, qi[p], 0))],
        scratch_shapes=[pltpu.VMEM((t, 1), F32), pltpu.VMEM((t, 2 * DN), F32),
                        pltpu.VMEM((t, t), F32), pltpu.VMEM((t, t), BF16)] * hb)
    return pl.pallas_call(
        body, grid_spec=grid_spec,
        out_shape=[jax.ShapeDtypeStruct((s, H * DN), F32), jax.ShapeDtypeStruct((H, s, 1), F32)],
        name="attn_fwd", compiler_params=_params(("parallel", "arbitrary"), VMEM_BIG))(qi, ki, q, k, v)


def _attn_bwd(q, k, v, do, lse, delta, t):
    s = q.shape[1]
    n = s // t
    scale = float((DN + DR) ** -0.5)
    qi, ki = _causal_pairs(n, by_key=True)

    def body(qi_ref, ki_ref, q_ref, k_ref, v_ref, do_ref, lse_ref, dl_ref, dq_ref, dk_ref, dv_ref,
             dk_sc, dv_sc, s_sc, dp_sc, p_sc, ds_sc):
        p = pl.program_id(1)
        i = qi_ref[p]
        j = ki_ref[p]

        @pl.when(p == 0)
        def _():
            dq_ref[...] = jnp.zeros_like(dq_ref)

        @pl.when(i == j)
        def _():
            dk_sc[...] = jnp.zeros_like(dk_sc)
            dv_sc[...] = jnp.zeros_like(dv_sc)

        def step(diag):
            for h in range(ATT_HEADS):
                s_sc[h] = _dot_nt(q_ref[h], k_ref[h])
                dp_sc[h] = _dot_nt(do_ref[:, DN * h:DN * (h + 1)], v_ref[h, :, 0:DN])
            for h in range(ATT_HEADS):
                for r0 in range(0, t, ATT_ROWS):
                    rows = slice(r0, r0 + ATT_ROWS)
                    width = _diag_width(r0, t) if diag else t
                    sc = s_sc[h, rows, 0:width] * scale
                    if diag:
                        sc = jnp.where(_diag_mask_rows(r0, width), sc, -jnp.inf)
                    pr = jnp.exp(sc - lse_ref[h, rows, :])
                    ds = pr * (dp_sc[h, rows, 0:width] - dl_ref[h, rows, :]) * scale
                    p_sc[h, rows, 0:width] = pr.astype(BF16)
                    ds_sc[h, rows, 0:width] = ds.astype(BF16)
                    if width < t:
                        p_sc[h, rows, width:t] = jnp.zeros((ATT_ROWS, t - width), BF16)
                        ds_sc[h, rows, width:t] = jnp.zeros((ATT_ROWS, t - width), BF16)
            q_rows = pl.ds(pl.multiple_of(i * t, t), t)
            for h in range(ATT_HEADS):
                dv_sc[h] += _dot_tn(p_sc[h], do_ref[:, DN * h:DN * (h + 1)])
                dk_sc[h] += _dot_tn(ds_sc[h], q_ref[h])
                dq_ref[h, q_rows, :] += _dot(ds_sc[h], k_ref[h])

        @pl.when(i > j)
        def _():
            step(False)

        @pl.when(i == j)
        def _():
            step(True)

        @pl.when(i == n - 1)
        def _():
            dk_ref[...] = dk_sc[...]
            dv_ref[...] = dv_sc[...]

    hb = ATT_HEADS
    grid_spec = pltpu.PrefetchScalarGridSpec(
        num_scalar_prefetch=2, grid=(H // hb, int(qi.shape[0])),
        in_specs=[pl.BlockSpec((hb, t, 2 * DN), lambda h, p, qi, ki: (h, qi[p], 0)),
                  pl.BlockSpec((hb, t, 2 * DN), lambda h, p, qi, ki: (h, ki[p], 0)),
                  pl.BlockSpec((hb, t, 2 * DN), lambda h, p, qi, ki: (h, ki[p], 0)),
                  pl.BlockSpec((t, hb * DN), lambda h, p, qi, ki: (qi[p], h)),
                  pl.BlockSpec((hb, t, 1), lambda h, p, qi, ki: (h, qi[p], 0)),
                  pl.BlockSpec((hb, t, 1), lambda h, p, qi, ki: (h, qi[p], 0))],
        out_specs=[pl.BlockSpec((hb, s, 2 * DN), lambda h, p, qi, ki: (h, 0, 0)),
                   pl.BlockSpec((hb, t, 2 * DN), lambda h, p, qi, ki: (h, ki[p], 0)),
                   pl.BlockSpec((hb, t, DN), lambda h, p, qi, ki: (h, ki[p], 0))],
        scratch_shapes=[pltpu.VMEM((hb, t, 2 * DN), F32), pltpu.VMEM((hb, t, DN), F32),
                        pltpu.VMEM((hb, t, t), F32), pltpu.VMEM((hb, t, t), F32),
                        pltpu.VMEM((hb, t, t), BF16), pltpu.VMEM((hb, t, t), BF16)])
    return pl.pallas_call(
        body, grid_spec=grid_spec,
        out_shape=[jax.ShapeDtypeStruct((H, s, 2 * DN), F32), jax.ShapeDtypeStruct((H, s, 2 * DN), F32),
                   jax.ShapeDtypeStruct((H, s, DN), F32)],
        name="attn_bwd", compiler_params=_params(("parallel", "arbitrary"), VMEM_BIG))(
            qi, ki, q, k, v, do, lse, delta)


def _middle(za, o, proj_g, x, tgt, gate, fnw, wco, wao, wo, ts):
    s = x.shape[0]
    inv_d = 1.0 / D

    def body(za_ref, o_ref, bg_ref, ga_ref, gb_ref, x_ref, t_ref, gate_ref, fnw_ref, wco_ref, wao_ref, wo_ref,
             dx2_ref, dza_ref, do_ref, dl_ref, dpg_ref, zb_ref, mg_ref, dmo_ref, dya_ref, dyb_ref, vec_ref):
        @pl.when(pl.program_id(0) == 0)
        def _():
            vec_ref[...] = jnp.zeros_like(vec_ref)

        ov = o_ref[...]
        bg = bg_ref[...]
        sb = _sigmoid(bg)
        silu_b = bg * sb
        zb = (ov * silu_b).astype(BF16)
        zb_ref[...] = zb
        ya = _dot(za_ref[...], wco_ref[...])
        yb = _dot(zb, wao_ref[...])
        sa = _sigmoid(ga_ref[...])
        sg = _sigmoid(gb_ref[...])
        mg = (sa * ya + sg * yb).astype(BF16)
        mg_ref[...] = mg
        mo = _dot(mg, wo_ref[...])
        gate_v = gate_ref[...]
        x2 = x_ref[...] + gate_v * mo
        r = lax.rsqrt(_rowmean(x2 * x2) + EPS)
        xh = x2 * r
        fw = fnw_ref[...]
        e = xh * fw - t_ref[...]
        vec_ref[2:3, :] += _colsum(e * e)
        dy = e * inv_d
        vec_ref[0:1, :] += _colsum(dy * xh)
        dxh = dy * fw
        dx2 = r * (dxh - xh * _rowmean(dxh * xh))
        dx2_ref[...] = dx2
        vec_ref[1:2, :] += _colsum(dx2 * mo)
        dmo = (gate_v * dx2).astype(BF16)
        dmo_ref[...] = dmo
        dmg = _dot_nt(dmo, wo_ref[...])
        dya = (sa * dmg).astype(BF16)
        dyb = (sg * dmg).astype(BF16)
        dya_ref[...] = dya
        dyb_ref[...] = dyb
        dpg_ref[:, D:2 * D] = (dmg * ya * (sa * (1.0 - sa))).astype(BF16)
        dpg_ref[:, 2 * D:3 * D] = (dmg * yb * (sg * (1.0 - sg))).astype(BF16)
        dza_ref[...] = _dot_nt(dya, wco_ref[...])
        dzb = _dot_nt(dyb, wao_ref[...])
        dov = dzb * silu_b
        do_ref[...] = dov.astype(BF16)
        dpg_ref[:, 0:D] = (dzb * ov * _dsilu(bg, sb)).astype(BF16)
        dprod = dov * ov
        for h in range(H):
            dl_ref[h] = jnp.sum(dprod[:, DN * h:DN * (h + 1)], axis=-1, keepdims=True)

    col = lambda c: pl.BlockSpec((ts, D), lambda i, c=c: (i, c))
    row = pl.BlockSpec((ts, D), lambda i: (i, 0))
    vec = pl.BlockSpec((1, D), lambda i: (0, 0))
    wsp = pl.BlockSpec((D, D), lambda i: (0, 0))
    bf = jax.ShapeDtypeStruct((s, D), BF16)
    ff = jax.ShapeDtypeStruct((s, D), F32)
    return pl.pallas_call(
        body, grid=(s // ts,),
        in_specs=[row, row, col(0), col(1), col(2), row, row, vec, vec, wsp, wsp, wsp],
        out_specs=[row, row, row, pl.BlockSpec((H, ts, 1), lambda i: (0, i, 0)),
                   pl.BlockSpec((ts, G_COLS), lambda i: (i, 0)), row, row, row, row, row,
                   pl.BlockSpec((8, D), lambda i: (0, 0))],
        out_shape=[ff, ff, bf, jax.ShapeDtypeStruct((H, s, 1), F32), jax.ShapeDtypeStruct((s, G_COLS), BF16),
                   bf, bf, bf, bf, bf, jax.ShapeDtypeStruct((8, D), F32)],
        name="middle", compiler_params=_params(("arbitrary",), VMEM_BIG))(
            za, o, proj_g, proj_g, proj_g, x, tgt, gate, fnw, wco, wao, wo)


def _input_bwd(dpa, dpl, dpg, wa, wl, wg, x, dx2, norm_w, scale, ts):
    s = x.shape[0]

    def body(dpa_ref, dpl_ref, dpg_ref, wa_ref, wl_ref, wg_ref, x_ref, dx2_ref, nw_ref, sc_ref, gx_ref, gv_ref):
        @pl.when(pl.program_id(0) == 0)
        def _():
            gv_ref[...] = jnp.zeros_like(gv_ref)

        dh = (_dot_nt(dpa_ref[...], wa_ref[...]) + _dot_nt(dpl_ref[...], wl_ref[...])
              + _dot_nt(dpg_ref[...], wg_ref[...]))
        xv = x_ref[...]
        r = lax.rsqrt(_rowmean(xv * xv) + EPS)
        xh = xv * r
        nw = nw_ref[...]
        gv_ref[0:1, :] += _colsum(dh)
        gv_ref[1:2, :] += _colsum(dh * (xh * nw))
        dy = dh * (1.0 + sc_ref[...])
        gv_ref[2:3, :] += _colsum(dy * xh)
        dxh = dy * nw
        gx_ref[...] = dx2_ref[...] + r * (dxh - xh * _rowmean(dxh * xh))

    const = lambda shape: pl.BlockSpec(shape, lambda i: (0, 0))
    rowb = lambda w: pl.BlockSpec((ts, w), lambda i: (i, 0))
    return pl.pallas_call(
        body, grid=(s // ts,),
        in_specs=[rowb(A_COLS), rowb(L_COLS), rowb(G_COLS), const((D, A_COLS)), const((D, L_COLS)),
                  const((D, G_COLS)), rowb(D), rowb(D), const((1, D)), const((1, D))],
        out_specs=[rowb(D), const((8, D))],
        out_shape=[jax.ShapeDtypeStruct((s, D), F32), jax.ShapeDtypeStruct((8, D), F32)],
        name="input_bwd", compiler_params=_params(("arbitrary",), VMEM_BIG))(
            dpa, dpl, dpg, wa, wl, wg, x, dx2, norm_w, scale)


def _adamw(w, g, m, v, tr, name):
    rows, cols = w.shape
    c1 = 1.0 - ADAM_B1 ** ADAM_STEP
    c2 = 1.0 - ADAM_B2 ** ADAM_STEP

    def body(w_ref, g_ref, m_ref, v_ref, d_ref, nm_ref, nv_ref):
        gv = g_ref[...]
        nm = ADAM_B1 * m_ref[...] + (1.0 - ADAM_B1) * gv
        nv = ADAM_B2 * v_ref[...] + (1.0 - ADAM_B2) * (gv * gv)
        nm_ref[...] = nm
        nv_ref[...] = nv
        d_ref[...] = -ADAM_LR * ((nm / c1) / (jnp.sqrt(nv / c2) + ADAM_EPS) + ADAM_WD * w_ref[...])

    blk = pl.BlockSpec((tr, cols), lambda i: (i, 0))
    shp = jax.ShapeDtypeStruct((rows, cols), F32)
    return pl.pallas_call(
        body, grid=(rows // tr,), in_specs=[blk] * 4, out_specs=[blk] * 3, out_shape=[shp] * 3, name=name,
        compiler_params=_params(("parallel",)))(w, g, m, v)


def _ada_fwd(c_all, w_ada_shard, b_ada_shard):
    def body(c_ref, w_ref, b_ref, o_ref):
        cv = c_ref[...]
        o_ref[...] = jnp.dot(cv * _sigmoid(cv), w_ref[...], preferred_element_type=F32,
                             precision=lax.Precision.HIGHEST) + b_ref[...]

    return pl.pallas_call(
        body, out_shape=jax.ShapeDtypeStruct((N_DEV, w_ada_shard.shape[1]), F32), name="ada_fwd")(
            c_all, w_ada_shard, b_ada_shard)


def _ada_bwd(c_all_t, dmod_shard):
    def body(c_ref, d_ref, o_ref):
        cv = c_ref[...]
        o_ref[...] = jnp.dot(cv * _sigmoid(cv), d_ref[...], preferred_element_type=F32,
                             precision=lax.Precision.HIGHEST)

    return pl.pallas_call(
        body, out_shape=jax.ShapeDtypeStruct((D, dmod_shard.shape[1]), F32), name="ada_bwd")(c_all_t, dmod_shard)


def _sum_slabs(stack, tr, name):
    n, rows, cols = stack.shape

    def body(s_ref, o_ref):
        acc = s_ref[0]
        for k in range(1, n):
            acc = acc + s_ref[k]
        o_ref[...] = acc

    return pl.pallas_call(
        body, grid=(rows // tr,), in_specs=[pl.BlockSpec((n, tr, cols), lambda i: (0, i, 0))],
        out_specs=pl.BlockSpec((tr, cols), lambda i: (i, 0)), out_shape=jax.ShapeDtypeStruct((rows, cols), F32),
        name=name, compiler_params=_params(("parallel",)))(stack)


def _sum_chip_slabs(arrived, part, place, tr, name):
    n, rows, cols = arrived.shape
    per = rows // tr

    def body(place_ref, a_ref, p_ref, o_ref):
        acc = p_ref[0].astype(F32)
        for k in range(n):
            acc = acc + a_ref[k].astype(F32)
        o_ref[...] = acc

    grid_spec = pltpu.PrefetchScalarGridSpec(
        num_scalar_prefetch=1, grid=(per,),
        in_specs=[pl.BlockSpec((n, tr, cols), lambda i, pc: (0, i, 0)),
                  pl.BlockSpec((1, tr, cols), lambda i, pc: (pc[0], i, 0))],
        out_specs=pl.BlockSpec((tr, cols), lambda i, pc: (pc[1] * per + i, 0)))
    return pl.pallas_call(
        body, grid_spec=grid_spec, out_shape=jax.ShapeDtypeStruct((2 * rows, cols), F32), name=name,
        compiler_params=_params(("parallel",)))(place, arrived, part)


def _add_own_half(full, other, core, tr, name):
    n, rows, cols = other.shape
    per = rows // tr

    def body(c_ref, f_ref, o_ref, out_ref):
        out_ref[...] = (f_ref[...] + o_ref[...]).astype(BF16)

    grid_spec = pltpu.PrefetchScalarGridSpec(
        num_scalar_prefetch=1, grid=(n, per),
        in_specs=[pl.BlockSpec((1, tr, cols), lambda k, i, c: (k, c[0] * per + i, 0)),
                  pl.BlockSpec((1, tr, cols), lambda k, i, c: (k, i, 0))],
        out_specs=pl.BlockSpec((1, tr, cols), lambda k, i, c: (k, i, 0)))
    return pl.pallas_call(
        body, grid_spec=grid_spec, out_shape=jax.ShapeDtypeStruct((n, rows, cols), BF16), name=name,
        compiler_params=_params(("parallel", "parallel")))(core, full, other)


def _coords():
    return lax.axis_index("x"), lax.axis_index("y"), lax.axis_index("c")


def _allgather8(block, src_rows, vmem, name):
    n = block.shape[1]
    m = src_rows
    sliced = block.shape[0] != m

    def body(x_ref, out_ref, send_sems, recv_sems, local_sem):
        x, y, c = _coords()
        me, sibling = (x, y, c), (x, y, 1 - c)
        chips = [(1 - x, y), (x, 1 - y), (1 - x, 1 - y)]
        src = x_ref.at[pl.ds(pl.multiple_of(c * m, 16), m), :] if sliced else x_ref

        def rows(px, py, pc):
            return out_ref.at[pl.ds(pl.multiple_of((4 * px + 2 * py + pc) * m, 8), m), :]

        def copy(k, blk, to, source=None):
            return pltpu.make_async_remote_copy(
                src_ref=rows(*blk) if source is None else source, dst_ref=rows(*blk),
                send_sem=send_sems.at[k], recv_sem=recv_sems.at[k], device_id=to, device_id_type=MESH)

        mine = pltpu.make_async_copy(src, rows(*me), local_sem)
        mine.start()
        first = [copy(0, me, sibling, source=src)]
        first += [copy(1 + j, me, (*chip, c), source=src) for j, chip in enumerate(chips)]
        for cp in first:
            cp.start()
        passed = [copy(4 + j, (*chip, c), sibling) for j, chip in enumerate(chips)]
        for j, chip in enumerate(chips):
            copy(1 + j, (*chip, c), me).wait_recv()
            passed[j].start()
        copy(0, sibling, me).wait_recv()
        for j, chip in enumerate(chips):
            copy(4 + j, (*chip, 1 - c), me).wait_recv()
        for cp in first + passed:
            cp.wait_send()
        mine.wait()

    space = pltpu.VMEM if vmem else pl.ANY
    return pl.pallas_call(
        body, out_shape=jax.ShapeDtypeStruct((N_DEV * m, n), block.dtype),
        in_specs=[pl.BlockSpec(memory_space=space)], out_specs=pl.BlockSpec(memory_space=space),
        scratch_shapes=[pltpu.SemaphoreType.DMA((7,)), pltpu.SemaphoreType.DMA((7,)), pltpu.SemaphoreType.DMA],
        name=name)(block)


HBM_REF = pl.BlockSpec(memory_space=pl.ANY)


def _gather_weights(shards):
    n = len(shards)
    halves = [a.shape[0] // 2 for a in shards]

    def body(*refs):
        x_refs, out_refs = refs[:n], refs[n:2 * n]
        send_sems, recv_sems, local_sems = refs[2 * n:]
        x, y, c = _coords()
        me, sibling = (x, y, c), (x, y, 1 - c)
        chips = [(1 - x, y), (x, 1 - y), (1 - x, 1 - y)]

        def src(a):
            return x_refs[a].at[pl.ds(pl.multiple_of(c * halves[a], 16), halves[a]), :]

        def blk(a, px, py, pc):
            return out_refs[a].at[4 * px + 2 * py + pc]

        def copy(a, k, who, to, source=None):
            return pltpu.make_async_remote_copy(
                src_ref=blk(a, *who) if source is None else source, dst_ref=blk(a, *who),
                send_sem=send_sems.at[7 * a + k], recv_sem=recv_sems.at[7 * a + k], device_id=to,
                device_id_type=MESH)

        mine = [pltpu.make_async_copy(src(a), blk(a, *me), local_sems.at[a]) for a in range(n)]
        for cp in mine:
            cp.start()
        started = []
        for a in range(n):
            started.append(copy(a, 0, me, sibling, source=src(a)))
            started += [copy(a, 1 + j, me, (*chip, c), source=src(a)) for j, chip in enumerate(chips)]
        for cp in started:
            cp.start()
        for j, chip in enumerate(chips):
            for a in range(n):
                copy(a, 1 + j, (*chip, c), me).wait_recv()
                onward = copy(a, 4 + j, (*chip, c), sibling)
                onward.start()
                started.append(onward)
        for a in range(n):
            copy(a, 0, sibling, me).wait_recv()
        for j, chip in enumerate(chips):
            for a in range(n):
                copy(a, 4 + j, (*chip, 1 - c), me).wait_recv()
        for cp in started:
            cp.wait_send()
        for cp in mine:
            cp.wait()

    outs = pl.pallas_call(
        body, out_shape=[jax.ShapeDtypeStruct((N_DEV, h, a.shape[1]), a.dtype) for a, h in zip(shards, halves)],
        in_specs=[HBM_REF] * n, out_specs=[HBM_REF] * n,
        scratch_shapes=[pltpu.SemaphoreType.DMA((7 * n,)), pltpu.SemaphoreType.DMA((7 * n,)),
                        pltpu.SemaphoreType.DMA((n,))],
        name="gather_weights")(*shards)
    return [o.reshape(N_CHIP, a.shape[0], a.shape[1]) for o, a in zip(outs, shards)]


def _swap_halves_with_sibling(fulls):
    n = len(fulls)
    halves = [a.shape[1] // 2 for a in fulls]

    def body(*refs):
        f_refs, got_refs = refs[:n], refs[n:2 * n]
        send_sems, recv_sems = refs[2 * n:]
        x, y, c = _coords()
        copies = []
        for a in range(n):
            src = f_refs[a].at[:, pl.ds(pl.multiple_of((1 - c) * halves[a], 8), halves[a]), :]
            copies.append(pltpu.make_async_remote_copy(
                src_ref=src, dst_ref=got_refs[a], send_sem=send_sems.at[a], recv_sem=recv_sems.at[a],
                device_id=(x, y, 1 - c), device_id_type=MESH))
        for cp in copies:
            cp.start()
        for cp in copies:
            cp.wait()

    return pl.pallas_call(
        body, out_shape=[jax.ShapeDtypeStruct((a.shape[0], h, a.shape[2]), a.dtype) for a, h in zip(fulls, halves)],
        in_specs=[HBM_REF] * n, out_specs=[HBM_REF] * n,
        scratch_shapes=[pltpu.SemaphoreType.DMA((n,)), pltpu.SemaphoreType.DMA((n,))],
        name="rs_pair_swap")(*fulls)


def _scatter_to_chips(parts):
    n = len(parts)

    def body(*refs):
        p_refs, got_refs = refs[:n], refs[n:2 * n]
        send_sems, recv_sems = refs[2 * n:]
        x, y, c = _coords()
        chips = [(1 - x, y), (x, 1 - y), (1 - x, 1 - y)]

        def copy(a, j):
            px, py = chips[j]
            return pltpu.make_async_remote_copy(
                src_ref=p_refs[a].at[2 * px + py], dst_ref=got_refs[a].at[j], send_sem=send_sems.at[3 * a + j],
                recv_sem=recv_sems.at[3 * a + j], device_id=(px, py, c), device_id_type=MESH)

        copies = [copy(a, j) for a in range(n) for j in range(3)]
        for cp in copies:
            cp.start()
        for cp in copies:
            cp.wait()

    return pl.pallas_call(
        body, out_shape=[jax.ShapeDtypeStruct((3,) + a.shape[1:], a.dtype) for a in parts],
        in_specs=[HBM_REF] * n, out_specs=[HBM_REF] * n,
        scratch_shapes=[pltpu.SemaphoreType.DMA((3 * n,)), pltpu.SemaphoreType.DMA((3 * n,))],
        name="rs_chip_scatter")(*parts)


def _join_halves_with_sibling(wholes):
    n = len(wholes)

    def body(*refs):
        out_refs = refs[n:2 * n]
        send_sems, recv_sems = refs[2 * n:]
        x, y, c = _coords()

        def push(a, core):
            rows = wholes[a].shape[0] // 2
            half = out_refs[a].at[pl.ds(pl.multiple_of(core * rows, 8), rows), :]
            return pltpu.make_async_remote_copy(
                src_ref=half, dst_ref=half, send_sem=send_sems.at[a], recv_sem=recv_sems.at[a],
                device_id=(x, y, 1 - c), device_id_type=MESH)

        for a in range(n):
            push(a, c).start()
        for a in range(n):
            push(a, 1 - c).wait_recv()
        for a in range(n):
            push(a, c).wait_send()

    return pl.pallas_call(
        body, out_shape=[jax.ShapeDtypeStruct(a.shape, a.dtype) for a in wholes],
        in_specs=[HBM_REF] * n, out_specs=[HBM_REF] * n, input_output_aliases={a: a for a in range(n)},
        scratch_shapes=[pltpu.SemaphoreType.DMA((n,)), pltpu.SemaphoreType.DMA((n,))],
        name="rs_pair_join")(*wholes)


def _cols_to_slabs(g):
    rows, cols = g.shape
    return g.reshape(rows, N_CHIP, cols // N_CHIP).transpose(1, 0, 2)


def _slabs_to_cols(w):
    n, rows, cols = w.shape
    return w.transpose(1, 0, 2).reshape(rows, n * cols)


def _uq_to_padded(w_uq):
    per = w_uq.reshape(RQ, H, DN + DR)
    nope = per[:, :, :DN].reshape(RQ, H * DN)
    rope = jnp.pad(per[:, :, DN:], ((0, 0), (0, 0), (0, LANE - DR))).reshape(RQ, H * LANE)
    return jnp.concatenate([nope, rope], axis=1)


def _uq_from_padded(g):
    nope = g[:, :H * DN].reshape(RQ, H, DN)
    rope = g[:, H * DN:].reshape(RQ, H, LANE)[:, :, :DR]
    return jnp.concatenate([nope, rope], axis=2).reshape(RQ, H * (DN + DR))


def _rope_tables(positions):
    inv_freq = ROPE_THETA ** (-jnp.arange(0, DR, 2, dtype=F32) / DR)
    ang = positions.astype(F32)[:, None] * inv_freq
    cos, sin = jnp.cos(ang), jnp.sin(ang)
    return jnp.tile(cos, (1, 4)), jnp.tile(jnp.concatenate([-sin, sin], axis=1), (1, 2))


def _local_step(x, tgt, cos_t, sin_t, mod, weights, small, tiles):
    ts, ts_in, tm_nn, tm_tn, t_attn, chunk = tiles
    wa, wl, wg, w_uq2, w_ukv, wco, wao, wo, conv_w = weights
    norm_w, conv_b, ln_w, ln_b, q_norm_w, kv_norm_w, fnw = small
    shift, scale, gate = mod[:, 0:D], mod[:, D:2 * D], mod[:, 2 * D:3 * D]

    h = _adaln_norm(x, norm_w, shift, scale, ts)
    proj_a = _mm_nn(h, wa, tm_nn, D, "proj_a")
    proj_l = _mm_nn(h, wl, tm_nn, L_COLS, "proj_l")
    proj_g = _mm_nn(h, wg, tm_nn, D, "proj_g")
    u0, u1, za = _conv_fwd(proj_a, conv_w, conv_b, ln_w, ln_b, ts, chunk)
    qn, kvn, q, k, v = _mla_prep(proj_l, q_norm_w, kv_norm_w, w_uq2, w_ukv, cos_t, sin_t, ts)
    o, lse = _attn_fwd(q, k, v, t_attn)
    (dx2, dza, do, delta, dpg, zb, mg, dmo, dya, dyb, vec_mid) = _middle(
        za, o, proj_g, x, tgt, gate, fnw, wco, wao, wo, ts)
    g_wo = _mm_tn(mg, dmo, tm_tn, D, "grad_w_out")
    g_wco = _mm_tn(za, dya, tm_tn, D, "grad_w_conv_out")
    g_wao = _mm_tn(zb, dyb, tm_tn, D, "grad_w_attn_out")
    dq, dk, dv = _attn_bwd(q, k, v, do, lse, delta, t_attn)
    dpl, g_wuq2, g_wukv, vec_mla = _mla_prep_bwd(
        dq, dk, dv, proj_l, qn, kvn, q_norm_w, kv_norm_w, w_uq2, w_ukv, cos_t, sin_t, ts)
    dpa, g_conv_w, vec_conv = _conv_bwd(dza, proj_a, u0, u1, conv_w, ln_w, ln_b, ts, chunk)
    grad_x, vec_in = _input_bwd(dpa, dpl, dpg, wa, wl, wg, x, dx2, norm_w, scale, ts_in)
    g_wa = _mm_tn(h, dpa, tm_tn, D, "grad_w_in_a")
    g_wl = _mm_tn(h, dpl, tm_tn, L_COLS, "grad_w_in_l")
    g_wg = _mm_tn(h, dpg, tm_tn, D, "grad_w_in_g")

    dmod = jnp.concatenate([vec_in[0:1], vec_in[1:2], vec_mid[1:2]], axis=1)
    sums = dict(dmod=dmod, norm_w=vec_in[2:3], conv_b=vec_conv[2:3], ln_w=vec_conv[0:1], ln_b=vec_conv[1:2],
                q_norm_w=vec_mla[0:1], kv_norm_w=vec_mla[1:2], final_norm_w=vec_mid[0:1], loss=vec_mid[2:3],
                conv_w=g_conv_w)
    grads = dict(wa=g_wa, wl=g_wl, wg=g_wg, w_uq2=g_wuq2, w_ukv=g_wukv, wco=g_wco, wao=g_wao, wo=g_wo)
    return grad_x, grads, sums


SMALL_ORDER = (("dmod", 3 * D), ("norm_w", D), ("conv_b", D), ("ln_w", D), ("ln_b", D), ("q_norm_w", RQ),
               ("kv_norm_w", RQ), ("final_norm_w", D), ("loss", D), ("conv_w", HALO * D))
SMALL_ROWS = 336


def kernel(x, c, positions, w_ada, b_ada, norm_w, w_in, conv_w, conv_b, conv_ln_w, conv_ln_b, w_conv_out, q_norm_w, w_uq, kv_norm_w, w_ukv, w_attn_out, w_out, final_norm_w, loss_target, m_w_ada, m_b_ada, m_norm_w, m_w_in, m_conv_w, m_conv_b, m_conv_ln_w, m_conv_ln_b, m_w_conv_out, m_q_norm_w, m_w_uq, m_kv_norm_w, m_w_ukv, m_w_attn_out, m_w_out, m_final_norm_w, v_w_ada, v_b_ada, v_norm_w, v_w_in, v_conv_w, v_conv_b, v_conv_ln_w, v_conv_ln_b, v_w_conv_out, v_q_norm_w, v_w_uq, v_kv_norm_w, v_w_ukv, v_w_attn_out, v_w_out, v_final_norm_w):
    ix, iy, ic = _coords()
    chip = 2 * ix + iy
    dev = 4 * ix + 2 * iy + ic
    s = x.shape[1]
    tiles = (256, 512, 1024, 2048, 512, 32)

    conv_w_pad = jnp.pad(conv_w[0], ((0, HALO - KC), (0, 0)))
    small_in = jnp.concatenate([c.reshape(8, LANE), conv_w_pad.reshape(64, LANE)], axis=0)
    small_all = _allgather8(small_in, 72, True, "gather_c_conv").reshape(N_DEV, 72, LANE)
    c_all = small_all[:, 0:8].reshape(N_DEV, D)
    conv_full = jnp.concatenate(
        [small_all[2 * k, 8:72].reshape(HALO, D // N_CHIP) for k in range(N_CHIP)], axis=1)

    shards = [w[0].astype(BF16) for w in (w_in, w_uq, w_ukv, w_conv_out, w_attn_out, w_out)]
    g_in, g_uq, g_ukv, g_co, g_ao, g_o = _gather_weights(shards)
    w_in_f, w_uq_f, w_ukv_f = _slabs_to_cols(g_in), _slabs_to_cols(g_uq), _slabs_to_cols(g_ukv)
    wco, wao, wo = g_co.reshape(D, D), g_ao.reshape(D, D), g_o.reshape(D, D)
    wa = w_in_f[:, 0:A_COLS]
    wl = jnp.pad(w_in_f[:, A_COLS:A_COLS + L_COLS_RAW], ((0, 0), (0, L_COLS - L_COLS_RAW)))
    wg = w_in_f[:, A_COLS + L_COLS_RAW:]
    weights = (wa, wl, wg, _uq_to_padded(w_uq_f), w_ukv_f, wco, wao, wo, conv_full)

    ada_cols = w_ada.shape[2]
    b_shard = lax.dynamic_slice(b_ada, (0, chip * ada_cols), (1, ada_cols))
    mod_part = _ada_fwd(c_all, w_ada[0], b_shard)
    mod_all = _allgather8(mod_part, N_DEV, True, "gather_mod").reshape(N_DEV, N_DEV, ada_cols)
    mod = jnp.concatenate(
        [lax.dynamic_slice(mod_all[2 * k], (dev, 0), (1, ada_cols)) for k in range(N_CHIP)], axis=1)

    cos_t, sin_t = _rope_tables(positions[0])
    small = (norm_w, conv_b, conv_ln_w, conv_ln_b, q_norm_w, kv_norm_w, final_norm_w.reshape(1, D))
    grad_x, grads, sums = _local_step(x[0], loss_target[0], cos_t, sin_t, mod, weights, small, tiles)

    small_flat = jnp.concatenate([sums[name].reshape(-1) for name, _ in SMALL_ORDER])
    small_flat = jnp.pad(small_flat, (0, SMALL_ROWS * LANE - small_flat.shape[0]))
    small_g = _allgather8(small_flat.reshape(SMALL_ROWS, LANE), SMALL_ROWS, True, "gather_small_grads")
    small_g = small_g.reshape(N_DEV, SMALL_ROWS, LANE)
    small_sum = _sum_slabs(small_g, SMALL_ROWS, "sum_small_grads").reshape(-1)
    tot, pos = {}, 0
    for name, size in SMALL_ORDER:
        tot[name] = small_sum[pos:pos + size]
        pos += size
    loss = (0.5 / D) * jnp.sum(tot["loss"])
    dmod_all = small_g.reshape(N_DEV, -1)[:, 0:3 * D]
    g_b_ada = tot["dmod"].reshape(1, 3 * D)
    dmod_shard = lax.dynamic_slice(dmod_all, (0, chip * ada_cols), (N_DEV, ada_cols))
    g_w_ada = _ada_bwd(c_all.T, dmod_shard).reshape(1, D, ada_cols)
    g_conv_w = lax.dynamic_slice(tot["conv_w"].reshape(HALO, D), (0, chip * (D // N_CHIP)), (KC, D // N_CHIP))
    g_conv_w = g_conv_w.reshape(1, KC, D // N_CHIP)

    g_w_in = jnp.concatenate([grads["wa"], grads["wl"][:, 0:L_COLS_RAW], grads["wg"]], axis=1)
    nr = D // N_CHIP
    fulls = [_cols_to_slabs(g_w_in), _cols_to_slabs(_uq_from_padded(grads["w_uq2"])), _cols_to_slabs(grads["w_ukv"]),
             grads["wco"].reshape(N_CHIP, nr, D), grads["wao"].reshape(N_CHIP, nr, D),
             grads["wo"].reshape(N_CHIP, nr, D)]
    from_sibling = _swap_halves_with_sibling(fulls)
    core = ic.reshape(1).astype(jnp.int32)
    chip_sums = [_add_own_half(f, o, core, min(256, o.shape[1]), f"add_own_half_{n}")
                 for n, (f, o) in enumerate(zip(fulls, from_sibling))]
    arrived = _scatter_to_chips(chip_sums)
    place = jnp.stack([chip, ic]).astype(jnp.int32)
    wholes = [_sum_chip_slabs(a, p, place, min(128, a.shape[1]), f"sum_chip_slabs_{n}")
              for n, (a, p) in enumerate(zip(arrived, chip_sums))]
    g_w_in_s, g_w_uq_s, g_w_ukv_s, g_wco_s, g_wao_s, g_wo_s = _join_halves_with_sibling(wholes)

    def big(w, g, m, v, tr, name):
        d, nm, nv = _adamw(w[0], g, m[0], v[0], tr, name)
        return g[None], d[None], nm[None], nv[None]

    vec_names = ("b_ada", "norm_w", "conv_b", "conv_ln_w", "conv_ln_b", "q_norm_w", "kv_norm_w", "final_norm_w")
    vec_w = (b_ada, norm_w, conv_b, conv_ln_w, conv_ln_b, q_norm_w, kv_norm_w, final_norm_w)
    vec_m = (m_b_ada, m_norm_w, m_conv_b, m_conv_ln_w, m_conv_ln_b, m_q_norm_w, m_kv_norm_w, m_final_norm_w)
    vec_v = (v_b_ada, v_norm_w, v_conv_b, v_conv_ln_w, v_conv_ln_b, v_q_norm_w, v_kv_norm_w, v_final_norm_w)
    vec_g = (g_b_ada, tot["norm_w"], tot["conv_b"], tot["ln_w"], tot["ln_b"], tot["q_norm_w"], tot["kv_norm_w"],
             tot["final_norm_w"])
    vec_g = tuple(g.reshape(w.shape) for g, w in zip(vec_g, vec_w))
    cat = lambda arrs: jnp.concatenate([a.reshape(-1) for a in arrs]).reshape(-1, LANE)
    vd, vnm, vnv = _adamw(cat(vec_w), cat(vec_g), cat(vec_m), cat(vec_v), cat(vec_w).shape[0], "adamw_vectors")

    def split(packed):
        flat, out, pos = packed.reshape(-1), [], 0
        for w in vec_w:
            out.append(flat[pos:pos + w.size].reshape(w.shape))
            pos += w.size
        return out

    res = {}
    for name, g, d, nm, nv in zip(vec_names, vec_g, split(vd), split(vnm), split(vnv)):
        res[name] = (g, d, nm, nv)
    res["w_ada"] = big(w_ada, g_w_ada[0], m_w_ada, v_w_ada, 256, "adamw_w_ada")
    res["w_in"] = big(w_in, g_w_in_s, m_w_in, v_w_in, 256, "adamw_w_in")
    res["conv_w"] = big(conv_w, g_conv_w[0], m_conv_w, v_conv_w, KC, "adamw_conv_w")
    res["w_conv_out"] = big(w_conv_out, g_wco_s, m_w_conv_out, v_w_conv_out, 256, "adamw_w_conv_out")
    res["w_uq"] = big(w_uq, g_w_uq_s, m_w_uq, v_w_uq, 256, "adamw_w_uq")
    res["w_ukv"] = big(w_ukv, g_w_ukv_s, m_w_ukv, v_w_ukv, 256, "adamw_w_ukv")
    res["w_attn_out"] = big(w_attn_out, g_wao_s, m_w_attn_out, v_w_attn_out, 256, "adamw_w_attn_out")
    res["w_out"] = big(w_out, g_wo_s, m_w_out, v_w_out, 256, "adamw_w_out")

    order = ("w_ada", "b_ada", "norm_w", "w_in", "conv_w", "conv_b", "conv_ln_w", "conv_ln_b", "w_conv_out",
             "q_norm_w", "w_uq", "kv_norm_w", "w_ukv", "w_attn_out", "w_out", "final_norm_w")
    outs = [loss, grad_x[None]]
    for slot in range(4):
        outs += [res[name][slot] for name in order]
    return tuple(outs)
```

```python
import functools

import numpy as np
import jax
import jax.numpy as jnp
from jax import lax
from jax.experimental import pallas as pl
from jax.experimental.pallas import tpu as pltpu

F32 = jnp.float32
BF16 = jnp.bfloat16
MESH = pl.DeviceIdType.MESH

D = 1024
H = 8
DN = 128
DR = 64
RQ = 256
KC = 31
HALO = 32
EPS = 1e-6
ROPE_THETA = 10000.0
N_CHIP = 4
N_DEV = 8
LANE = 128
VMEM_BIG = 56 * 1024 * 1024

ADAM_LR = 0.001
ADAM_B1 = 0.9
ADAM_B2 = 0.999
ADAM_EPS = 1e-08
ADAM_WD = 0.01
ADAM_STEP = 10

A_COLS = 3 * D
L_COLS_RAW = RQ + RQ + DR
L_COLS = 640
G_COLS = 3 * D
IN_COLS = A_COLS + L_COLS_RAW + G_COLS


def _params(sem=None, vmem=None):
    kw = {}
    if sem is not None:
        kw["dimension_semantics"] = sem
    if vmem is not None:
        kw["vmem_limit_bytes"] = vmem
    return pltpu.CompilerParams(**kw)


def _dot(a, b):
    return jnp.dot(a, b, preferred_element_type=F32)


def _dot_nt(a, b):
    return lax.dot_general(a, b, (((1,), (1,)), ((), ())), preferred_element_type=F32)


def _dot_tn(a, b):
    return lax.dot_general(a, b, (((0,), (0,)), ((), ())), preferred_element_type=F32)


def _colsum(v):
    return jnp.sum(v, axis=0, keepdims=True)


def _rowmean(v):
    return jnp.mean(v, axis=-1, keepdims=True)


def _sigmoid(v):
    return jax.nn.sigmoid(v)


def _dsilu(v, s):
    return s * (1.0 + v * (1.0 - s))


def _swap_halves(v, first_half):
    return jnp.where(first_half, pltpu.roll(v, 96, 1), pltpu.roll(v, 32, 1))


def _first_half_mask(rows):
    lane = lax.broadcasted_iota(jnp.int32, (rows, LANE), 1)
    return (lane % 64) < 32


def _adaln_norm(x, norm_w, shift, scale, ts):
    s = x.shape[0]

    def body(x_ref, nw_ref, sh_ref, sc_ref, h_ref):
        xv = x_ref[...]
        r = lax.rsqrt(_rowmean(xv * xv) + EPS)
        y = xv * r * nw_ref[...]
        h_ref[...] = (y * (1.0 + sc_ref[...]) + sh_ref[...]).astype(BF16)

    row = pl.BlockSpec((ts, D), lambda i: (i, 0))
    vec = pl.BlockSpec((1, D), lambda i: (0, 0))
    return pl.pallas_call(
        body, grid=(s // ts,), in_specs=[row, vec, vec, vec], out_specs=row,
        out_shape=jax.ShapeDtypeStruct((s, D), BF16), name="adaln_norm",
        compiler_params=_params(("parallel",)))(x, norm_w, shift, scale)


def _mm_nn(a, b, tm, tn, name):
    m, k = a.shape
    n = b.shape[1]

    def body(a_ref, b_ref, o_ref):
        o_ref[...] = _dot(a_ref[...], b_ref[...])

    return pl.pallas_call(
        body, grid=(n // tn, m // tm),
        in_specs=[pl.BlockSpec((tm, k), lambda j, i: (i, 0)), pl.BlockSpec((k, tn), lambda j, i: (0, j))],
        out_specs=pl.BlockSpec((tm, tn), lambda j, i: (i, j)),
        out_shape=jax.ShapeDtypeStruct((m, n), F32), name=name,
        compiler_params=_params(("parallel", "parallel"), VMEM_BIG))(a, b)


def _mm_tn(a, b, tm, tn, name):
    m, k = a.shape
    n = b.shape[1]

    def body(a_ref, b_ref, o_ref):
        @pl.when(pl.program_id(1) == 0)
        def _():
            o_ref[...] = jnp.zeros_like(o_ref)
        o_ref[...] += _dot_tn(a_ref[...], b_ref[...])

    return pl.pallas_call(
        body, grid=(n // tn, m // tm),
        in_specs=[pl.BlockSpec((tm, k), lambda j, i: (i, 0)), pl.BlockSpec((tm, tn), lambda j, i: (i, j))],
        out_specs=pl.BlockSpec((k, tn), lambda j, i: (0, j)),
        out_shape=jax.ShapeDtypeStruct((k, n), F32), name=name,
        compiler_params=_params(("parallel", "arbitrary"), VMEM_BIG))(a, b)


def _coords():
    return lax.axis_index("x"), lax.axis_index("y"), lax.axis_index("c")


HBM_REF = pl.BlockSpec(memory_space=pl.ANY)


def _chip_scatter_copies(p_refs, got_refs, send_sems, recv_sems):
    x, y, c = _coords()
    copies = []
    for a in range(len(p_refs)):
        for j, (px, py) in enumerate([(1 - x, y), (x, 1 - y), (1 - x, 1 - y)]):
            copies.append(pltpu.make_async_remote_copy(
                src_ref=p_refs[a].at[2 * px + py], dst_ref=got_refs[a].at[j], send_sem=send_sems.at[3 * a + j],
                recv_sem=recv_sems.at[3 * a + j], device_id=(px, py, c), device_id_type=MESH))
    return copies


def _scatter_alongside(body, n_in, n_out, n_parts, last_step):
    def wrapped(*refs):
        ins, parts = refs[:n_in], refs[n_in:n_in + n_parts]
        rest = refs[n_in + n_parts:]
        outs, got = rest[:n_out], rest[n_out:n_out + n_parts]
        scratch, (send_sems, recv_sems) = rest[n_out + n_parts:-2], rest[-2:]

        @pl.when(pl.program_id(0) == 0)
        def _():
            for cp in _chip_scatter_copies(parts, got, send_sems, recv_sems):
                cp.start()

        body(*ins, *outs, *scratch)

        @pl.when(pl.program_id(0) == last_step)
        def _():
            for cp in _chip_scatter_copies(parts, got, send_sems, recv_sems):
                cp.wait()

    return wrapped


def _scatter_operands(parts):
    n = len(parts)
    shapes = [jax.ShapeDtypeStruct((3,) + a.shape[1:], a.dtype) for a in parts]
    sems = [pltpu.SemaphoreType.DMA((3 * n,)), pltpu.SemaphoreType.DMA((3 * n,))]
    return [HBM_REF] * n, [HBM_REF] * n, shapes, sems


def _shifted_copies(win_ref, sh_ref, rows):
    for p in range(1, 8):
        sh_ref[p - 1, 0:rows, :] = win_ref[pl.ds(p, rows), :]


def _tap_rows(win_ref, sh_ref, start, rows):
    p = start % 8
    if p == 0:
        return win_ref[pl.ds(start, rows), :]
    return sh_ref[p - 1, pl.ds(start - p, rows), :]


def _conv_taps(win_ref, sh_ref, w_ref, rows, chunk, offset_of_tap):
    pieces = []
    for c0 in range(0, rows, chunk):
        acc = None
        for j in range(KC):
            term = w_ref[j:j + 1, :] * _tap_rows(win_ref, sh_ref, c0 + offset_of_tap(j), chunk)
            acc = term if acc is None else acc + term
        pieces.append(acc)
    return pieces


def _conv_fwd(proj_a, conv_w, conv_b, ln_w, ln_b, ts, chunk):
    s = proj_a.shape[0]

    def body(av_ref, al_ref, ag_ref, w_ref, b_ref, lw_ref, lb_ref, u0_ref, u1_ref, za_ref, win_ref, sh_ref):
        @pl.when(pl.program_id(0) == 0)
        def _():
            win_ref[0:HALO, :] = jnp.zeros((HALO, D), F32)

        u0 = av_ref[...] * _sigmoid(al_ref[...])
        u0_ref[...] = u0
        win_ref[HALO:HALO + ts, :] = u0
        _shifted_copies(win_ref, sh_ref, ts + HALO - 8)
        pieces = _conv_taps(win_ref, sh_ref, w_ref, ts, chunk, lambda j: HALO - (KC - 1) + j)
        for n, acc in enumerate(pieces):
            u1_ref[n * chunk:(n + 1) * chunk, :] = acc + b_ref[...]
        win_ref[0:HALO, :] = win_ref[ts:ts + HALO, :]

        u1 = u1_ref[...]
        xc = u1 - _rowmean(u1)
        rstd = lax.rsqrt(_rowmean(xc * xc) + EPS)
        u2 = xc * rstd * lw_ref[...] + lb_ref[...]
        u3 = u2 * _sigmoid(u2)
        ag = ag_ref[...]
        za_ref[...] = (u3 * (ag * _sigmoid(ag))).astype(BF16)

    col = lambda c: pl.BlockSpec((ts, D), lambda i, c=c: (i, c))
    row = pl.BlockSpec((ts, D), lambda i: (i, 0))
    vec = pl.BlockSpec((1, D), lambda i: (0, 0))
    return pl.pallas_call(
        body, grid=(s // ts,),
        in_specs=[col(0), col(1), col(2), pl.BlockSpec((HALO, D), lambda i: (0, 0)), vec, vec, vec],
        out_specs=[row, row, row],
        out_shape=[jax.ShapeDtypeStruct((s, D), F32), jax.ShapeDtypeStruct((s, D), F32),
                   jax.ShapeDtypeStruct((s, D), BF16)],
        scratch_shapes=[pltpu.VMEM((ts + HALO, D), F32), pltpu.VMEM((7, ts + HALO, D), F32)], name="conv_fwd",
        compiler_params=_params(("arbitrary",), VMEM_BIG))(proj_a, proj_a, proj_a, conv_w, conv_b, ln_w, ln_b)


def _conv_bwd(dza, proj_a, u0, u1, conv_w, ln_w, ln_b, ts, chunk, parts):
    s = dza.shape[0]
    nt = s // ts
    per = ts // HALO

    def body(dza_ref, av_ref, al_ref, ag_ref, u0_ref, u0p_ref, u1_ref, w_ref, lw_ref, lb_ref,
             dpa_ref, gw_ref, gv_ref, dwin_ref, uwin_ref, du0_ref, gwp_ref, dsh_ref, ush_ref):
        step = pl.program_id(0)
        tile = nt - 1 - step

        @pl.when(step == 0)
        def _():
            dwin_ref[ts:ts + HALO, :] = jnp.zeros((HALO, D), F32)
            gwp_ref[...] = jnp.zeros_like(gwp_ref)
            gv_ref[...] = jnp.zeros_like(gv_ref)

        ag = ag_ref[...]
        sg = _sigmoid(ag)
        u1 = u1_ref[...]
        xc = u1 - _rowmean(u1)
        rstd = lax.rsqrt(_rowmean(xc * xc) + EPS)
        xh = xc * rstd
        u2 = xh * lw_ref[...] + lb_ref[...]
        s2 = _sigmoid(u2)
        dz = dza_ref[...]
        du3 = dz * (ag * sg)
        dpa_ref[:, 2 * D:3 * D] = (dz * (u2 * s2) * _dsilu(ag, sg)).astype(BF16)
        du2 = du3 * _dsilu(u2, s2)
        gv_ref[0:1, :] += _colsum(du2 * xh)
        gv_ref[1:2, :] += _colsum(du2)
        dxh = du2 * lw_ref[...]
        du1 = rstd * (dxh - _rowmean(dxh) - xh * _rowmean(dxh * xh))
        gv_ref[2:3, :] += _colsum(du1)
        dwin_ref[0:ts, :] = du1

        uwin_ref[0:HALO, :] = jnp.where(tile == 0, 0.0, u0p_ref[...])
        uwin_ref[HALO:HALO + ts, :] = u0_ref[...]

        _shifted_copies(dwin_ref, dsh_ref, ts + HALO - 8)
        _shifted_copies(uwin_ref, ush_ref, ts + HALO - 8)
        pieces = _conv_taps(dwin_ref, dsh_ref, w_ref, ts, chunk, lambda j: (KC - 1) - j)
        for n, acc in enumerate(pieces):
            du0_ref[n * chunk:(n + 1) * chunk, :] = acc
        for c0 in range(0, ts, chunk):
            dchunk = dwin_ref[c0:c0 + chunk, :]
            for j in range(KC):
                prod = dchunk * _tap_rows(uwin_ref, ush_ref, c0 + HALO - (KC - 1) + j, chunk)
                gwp_ref[8 * j:8 * j + 8, :] += jnp.sum(prod.reshape(chunk // 8, 8, D), axis=0)
        dwin_ref[ts:ts + HALO, :] = dwin_ref[0:HALO, :]

        du0 = du0_ref[...]
        al = al_ref[...]
        sl = _sigmoid(al)
        dpa_ref[:, 0:D] = (du0 * sl).astype(BF16)
        dpa_ref[:, D:2 * D] = (du0 * av_ref[...] * sl * (1.0 - sl)).astype(BF16)

        @pl.when(step == nt - 1)
        def _():
            for j in range(KC):
                gw_ref[j:j + 1, :] = _colsum(gwp_ref[8 * j:8 * j + 8, :])
            gw_ref[KC:HALO, :] = jnp.zeros((HALO - KC, D), F32)

    rev = lambda i: nt - 1 - i
    col = lambda c: pl.BlockSpec((ts, D), lambda i, c=c: (rev(i), c))
    row = pl.BlockSpec((ts, D), lambda i: (rev(i), 0))
    vec = pl.BlockSpec((1, D), lambda i: (0, 0))
    halo = pl.BlockSpec((HALO, D), lambda i: (jnp.maximum(rev(i) * per - 1, 0), 0))
    side_in, side_out, side_shapes, side_sems = _scatter_operands(parts)
    outs = pl.pallas_call(
        _scatter_alongside(body, 10, 3, len(parts), nt - 1), grid=(nt,),
        in_specs=[row, col(0), col(1), col(2), row, halo, row, pl.BlockSpec((HALO, D), lambda i: (0, 0)), vec, vec]
        + side_in,
        out_specs=[pl.BlockSpec((ts, A_COLS), lambda i: (rev(i), 0)),
                   pl.BlockSpec((HALO, D), lambda i: (0, 0)), pl.BlockSpec((8, D), lambda i: (0, 0))] + side_out,
        out_shape=[jax.ShapeDtypeStruct((s, A_COLS), BF16), jax.ShapeDtypeStruct((HALO, D), F32),
                   jax.ShapeDtypeStruct((8, D), F32)] + side_shapes,
        scratch_shapes=[pltpu.VMEM((ts + HALO, D), F32), pltpu.VMEM((ts + HALO, D), F32),
                        pltpu.VMEM((ts, D), F32), pltpu.VMEM((8 * HALO, D), F32),
                        pltpu.VMEM((7, ts + HALO, D), F32), pltpu.VMEM((7, ts + HALO, D), F32)] + side_sems,
        name="conv_bwd", compiler_params=_params(("arbitrary",), VMEM_BIG))(
            dza, proj_a, proj_a, proj_a, u0, u0, u1, conv_w, ln_w, ln_b, *parts)
    return outs[0], outs[1], outs[2], list(outs[3:])


def _mla_prep(proj_l, q_norm_w, kv_norm_w, w_uq2, w_ukv, cos_t, sin_t, ts):
    s = proj_l.shape[0]

    def body(pl_ref, qw_ref, kw_ref, wq_ref, wkv_ref, c_ref, s_ref, qn_ref, kvn_ref, q_ref, k_ref, v_ref):
        first = _first_half_mask(ts)
        cs = c_ref[...]
        sn = s_ref[...]

        def rms(v, w):
            return v * lax.rsqrt(_rowmean(v * v) + EPS) * w

        def rope(v):
            return v * cs + _swap_halves(v, first) * sn

        qn = rms(pl_ref[:, 0:RQ], qw_ref[...]).astype(BF16)
        kvn = rms(pl_ref[:, RQ:2 * RQ], kw_ref[...]).astype(BF16)
        qn_ref[...] = qn
        kvn_ref[...] = kvn
        q = _dot(qn, wq_ref[...])
        kv = _dot(kvn, wkv_ref[...])
        kr = rope(pl_ref[:, 2 * RQ:2 * RQ + LANE]).astype(BF16)
        for h in range(H):
            q_ref[h, :, 0:DN] = q[:, DN * h:DN * (h + 1)].astype(BF16)
            q_ref[h, :, DN:2 * DN] = rope(q[:, H * DN + LANE * h:H * DN + LANE * (h + 1)]).astype(BF16)
            k_ref[h, :, 0:DN] = kv[:, 2 * DN * h:2 * DN * h + DN].astype(BF16)
            k_ref[h, :, DN:2 * DN] = kr
            v_ref[h, :, 0:DN] = kv[:, 2 * DN * h + DN:2 * DN * (h + 1)].astype(BF16)
            v_ref[h, :, DN:2 * DN] = jnp.ones((ts, DN), BF16)

    const = lambda shape: pl.BlockSpec(shape, lambda i: (0,) * len(shape))
    rowb = lambda w: pl.BlockSpec((ts, w), lambda i: (i, 0))
    head = lambda w: pl.BlockSpec((H, ts, w), lambda i: (0, i, 0))
    return pl.pallas_call(
        body, grid=(s // ts,),
        in_specs=[rowb(L_COLS), const((1, RQ)), const((1, RQ)), const((RQ, 2 * H * DN)), const((RQ, 2 * H * DN)),
                  rowb(LANE), rowb(LANE)],
        out_specs=[rowb(RQ), rowb(RQ), head(2 * DN), head(2 * DN), head(2 * DN)],
        out_shape=[jax.ShapeDtypeStruct((s, RQ), BF16), jax.ShapeDtypeStruct((s, RQ), BF16),
                   jax.ShapeDtypeStruct((H, s, 2 * DN), BF16), jax.ShapeDtypeStruct((H, s, 2 * DN), BF16),
                   jax.ShapeDtypeStruct((H, s, 2 * DN), BF16)],
        name="mla_prep", compiler_params=_params(("parallel",)))(
            proj_l, q_norm_w, kv_norm_w, w_uq2, w_ukv, cos_t, sin_t)


def _mla_prep_bwd(dq, dk, dv, proj_l, qn, kvn, q_norm_w, kv_norm_w, w_uq2, w_ukv, cos_t, sin_t, ts):
    s = proj_l.shape[0]

    def body(dq_ref, dk_ref, dv_ref, pl_ref, qn_ref, kvn_ref, qw_ref, kw_ref, wq_ref, wkv_ref, c_ref, s_ref,
             dpl_ref, gwq_ref, gwkv_ref, gv_ref, dq2_ref, dkv2_ref):
        @pl.when(pl.program_id(0) == 0)
        def _():
            gwq_ref[...] = jnp.zeros_like(gwq_ref)
            gwkv_ref[...] = jnp.zeros_like(gwkv_ref)
            gv_ref[...] = jnp.zeros_like(gv_ref)

        first = _first_half_mask(ts)
        cs = c_ref[...]
        sn = s_ref[...]

        def rope_bwd(g):
            return g * cs + _swap_halves(g * sn, first)

        def rms_bwd(v, w, dy):
            r = lax.rsqrt(_rowmean(v * v) + EPS)
            vh = v * r
            dvh = dy * w
            return r * (dvh - vh * _rowmean(dvh * vh)), _colsum(dy * vh)

        dkr = None
        for h in range(H):
            dq2_ref[:, DN * h:DN * (h + 1)] = dq_ref[h, :, 0:DN].astype(BF16)
            dq2_ref[:, H * DN + LANE * h:H * DN + LANE * (h + 1)] = rope_bwd(dq_ref[h, :, DN:2 * DN]).astype(BF16)
            dkv2_ref[:, 2 * DN * h:2 * DN * h + DN] = dk_ref[h, :, 0:DN].astype(BF16)
            dkv2_ref[:, 2 * DN * h + DN:2 * DN * (h + 1)] = dv_ref[h].astype(BF16)
            part = dk_ref[h, :, DN:2 * DN]
            dkr = part if dkr is None else dkr + part

        dq2 = dq2_ref[...]
        dkv2 = dkv2_ref[...]
        gwq_ref[...] += _dot_tn(qn_ref[...], dq2)
        gwkv_ref[...] += _dot_tn(kvn_ref[...], dkv2)
        dcq, gq = rms_bwd(pl_ref[:, 0:RQ], qw_ref[...], _dot_nt(dq2, wq_ref[...]))
        dckv, gkv = rms_bwd(pl_ref[:, RQ:2 * RQ], kw_ref[...], _dot_nt(dkv2, wkv_ref[...]))
        gv_ref[0:1, :] += gq
        gv_ref[1:2, :] += gkv
        dpl_ref[:, 0:RQ] = dcq.astype(BF16)
        dpl_ref[:, RQ:2 * RQ] = dckv.astype(BF16)
        dpl_ref[:, 2 * RQ:2 * RQ + LANE] = rope_bwd(dkr).astype(BF16)

    const = lambda shape: pl.BlockSpec(shape, lambda i: (0,) * len(shape))
    rowb = lambda w: pl.BlockSpec((ts, w), lambda i: (i, 0))
    head = lambda w: pl.BlockSpec((H, ts, w), lambda i: (0, i, 0))
    return pl.pallas_call(
        body, grid=(s // ts,),
        in_specs=[head(2 * DN), head(2 * DN), head(DN), rowb(L_COLS), rowb(RQ), rowb(RQ), const((1, RQ)),
                  const((1, RQ)), const((RQ, 2 * H * DN)), const((RQ, 2 * H * DN)), rowb(LANE), rowb(LANE)],
        out_specs=[rowb(L_COLS), const((RQ, 2 * H * DN)), const((RQ, 2 * H * DN)), const((8, RQ))],
        out_shape=[jax.ShapeDtypeStruct((s, L_COLS), BF16), jax.ShapeDtypeStruct((RQ, 2 * H * DN), F32),
                   jax.ShapeDtypeStruct((RQ, 2 * H * DN), F32), jax.ShapeDtypeStruct((8, RQ), F32)],
        scratch_shapes=[pltpu.VMEM((ts, 2 * H * DN), BF16), pltpu.VMEM((ts, 2 * H * DN), BF16)],
        name="mla_prep_bwd", compiler_params=_params(("arbitrary",), VMEM_BIG))(
            dq, dk, dv, proj_l, qn, kvn, q_norm_w, kv_norm_w, w_uq2, w_ukv, cos_t, sin_t)


def _causal_pairs(n, by_key):
    if by_key:
        pairs = [(i, j) for j in range(n) for i in range(j, n)]
    else:
        pairs = [(i, j) for i in range(n) for j in range(i + 1)]
    return (jnp.asarray(np.array([p[0] for p in pairs], np.int32)),
            jnp.asarray(np.array([p[1] for p in pairs], np.int32)))


ATT_HEADS = 2
ATT_ROWS = 64


def _diag_width(r0, t):
    return min(t, -(-(r0 + ATT_ROWS) // LANE) * LANE)


def _diag_mask_rows(r0, width):
    rows = r0 + lax.broadcasted_iota(jnp.int32, (ATT_ROWS, width), 0)
    cols = lax.broadcasted_iota(jnp.int32, (ATT_ROWS, width), 1)
    return cols <= rows


def _diag_mask(t):
    return lax.broadcasted_iota(jnp.int32, (t, t), 1) <= lax.broadcasted_iota(jnp.int32, (t, t), 0)


def _attn_fwd(q, k, v, t):
    s = q.shape[1]
    n = s // t
    scale = float((DN + DR) ** -0.5)
    qi, ki = _causal_pairs(n, by_key=False)

    def body(qi_ref, ki_ref, q_ref, k_ref, v_ref, o_ref, lse_ref, *scratch):
        per_head = [scratch[4 * h:4 * h + 4] for h in range(ATT_HEADS)]
        p = pl.program_id(1)
        i = qi_ref[p]
        j = ki_ref[p]

        @pl.when(j == 0)
        def _():
            for m_sc, acc_sc, _, _ in per_head:
                m_sc[...] = jnp.full_like(m_sc, -jnp.inf)
                acc_sc[...] = jnp.zeros_like(acc_sc)

        def scores(h, diag):
            sc = _dot_nt(q_ref[h], k_ref[h])
            if diag:
                sc = jnp.where(_diag_mask(t), sc, -jnp.inf)
            per_head[h][2][...] = sc

        def softmax(h):
            m_sc, acc_sc, s_sc, p_sc = per_head[h]
            m_prev = m_sc[...]
            m_new = jnp.maximum(m_prev, jnp.max(s_sc[...], axis=-1, keepdims=True) * scale)
            m_sc[...] = m_new
            acc_sc[...] = jnp.exp(m_prev - m_new) * acc_sc[...]
            for r0 in range(0, t, ATT_ROWS):
                rows = slice(r0, r0 + ATT_ROWS)
                p_sc[rows, :] = jnp.exp(s_sc[rows, :] * scale - m_sc[rows, :]).astype(BF16)

        def values(h):
            _, acc_sc, _, p_sc = per_head[h]
            acc_sc[...] += _dot(p_sc[...], v_ref[h])

        def step(diag):
            scores(0, diag)
            for h in range(ATT_HEADS):
                if h + 1 < ATT_HEADS:
                    scores(h + 1, diag)
                softmax(h)
                values(h)

        @pl.when(j < i)
        def _():
            step(False)

        @pl.when(j == i)
        def _():
            step(True)
            for h, (m_sc, acc_sc, _, _) in enumerate(per_head):
                l = acc_sc[:, DN:2 * DN]
                o_ref[:, DN * h:DN * (h + 1)] = acc_sc[:, 0:DN] / l
                lse_ref[h] = m_sc[...] + jnp.log(l[:, 0:1])

    hb = ATT_HEADS
    grid_spec = pltpu.PrefetchScalarGridSpec(
        num_scalar_prefetch=2, grid=(H // hb, int(qi.shape[0])),
        in_specs=[pl.BlockSpec((hb, t, 2 * DN), lambda h, p, qi, ki: (h, qi[p], 0)),
                  pl.BlockSpec((hb, t, 2 * DN), lambda h, p, qi, ki: (h, ki[p], 0)),
                  pl.BlockSpec((hb, t, 2 * DN), lambda h, p, qi, ki: (h, ki[p], 0))],
        out_specs=[pl.BlockSpec((t, hb * DN), lambda h, p, qi, ki: (qi[p], h)),
                   pl.BlockSpec((hb, t, 1), lambda h, p, qi, ki: (h, qi[p], 0))],
        scratch_shapes=[pltpu.VMEM((t, 1), F32), pltpu.VMEM((t, 2 * DN), F32),
                        pltpu.VMEM((t, t), F32), pltpu.VMEM((t, t), BF16)] * hb)
    return pl.pallas_call(
        body, grid_spec=grid_spec,
        out_shape=[jax.ShapeDtypeStruct((s, H * DN), F32), jax.ShapeDtypeStruct((H, s, 1), F32)],
        name="attn_fwd", compiler_params=_params(("parallel", "arbitrary"), VMEM_BIG))(qi, ki, q, k, v)


def _attn_bwd(q, k, v, do, lse, delta, t):
    s = q.shape[1]
    n = s // t
    scale = float((DN + DR) ** -0.5)
    qi, ki = _causal_pairs(n, by_key=True)

    def body(qi_ref, ki_ref, q_ref, k_ref, v_ref, do_ref, lse_ref, dl_ref, dq_ref, dk_ref, dv_ref,
             dk_sc, dv_sc, s_sc, dp_sc, p_sc, ds_sc):
        p = pl.program_id(1)
        i = qi_ref[p]
        j = ki_ref[p]

        @pl.when(p == 0)
        def _():
            dq_ref[...] = jnp.zeros_like(dq_ref)

        @pl.when(i == j)
        def _():
            dk_sc[...] = jnp.zeros_like(dk_sc)
            dv_sc[...] = jnp.zeros_like(dv_sc)

        def step(diag):
            for h in range(ATT_HEADS):
                s_sc[h] = _dot_nt(q_ref[h], k_ref[h])
                dp_sc[h] = _dot_nt(do_ref[:, DN * h:DN * (h + 1)], v_ref[h, :, 0:DN])
            for h in range(ATT_HEADS):
                for r0 in range(0, t, ATT_ROWS):
                    rows = slice(r0, r0 + ATT_ROWS)
                    width = _diag_width(r0, t) if diag else t
                    sc = s_sc[h, rows, 0:width] * scale
                    if diag:
                        sc = jnp.where(_diag_mask_rows(r0, width), sc, -jnp.inf)
                    pr = jnp.exp(sc - lse_ref[h, rows, :])
                    ds = pr * (dp_sc[h, rows, 0:width] - dl_ref[h, rows, :]) * scale
                    p_sc[h, rows, 0:width] = pr.astype(BF16)
                    ds_sc[h, rows, 0:width] = ds.astype(BF16)
                    if width < t:
                        p_sc[h, rows, width:t] = jnp.zeros((ATT_ROWS, t - width), BF16)
                        ds_sc[h, rows, width:t] = jnp.zeros((ATT_ROWS, t - width), BF16)
            q_rows = pl.ds(pl.multiple_of(i * t, t), t)
            for h in range(ATT_HEADS):
                dv_sc[h] += _dot_tn(p_sc[h], do_ref[:, DN * h:DN * (h + 1)])
                dk_sc[h] += _dot_tn(ds_sc[h], q_ref[h])
                dq_ref[h, q_rows, :] += _dot(ds_sc[h], k_ref[h])

        @pl.when(i > j)
        def _():
            step(False)

        @pl.when(i == j)
        def _():
            step(True)

        @pl.when(i == n - 1)
        def _():
            dk_ref[...] = dk_sc[...]
            dv_ref[...] = dv_sc[...]

    hb = ATT_HEADS
    grid_spec = pltpu.PrefetchScalarGridSpec(
        num_scalar_prefetch=2, grid=(H // hb, int(qi.shape[0])),
        in_specs=[pl.BlockSpec((hb, t, 2 * DN), lambda h, p, qi, ki: (h, qi[p], 0)),
                  pl.BlockSpec((hb, t, 2 * DN), lambda h, p, qi, ki: (h, ki[p], 0)),
                  pl.BlockSpec((hb, t, 2 * DN), lambda h, p, qi, ki: (h, ki[p], 0)),
                  pl.BlockSpec((t, hb * DN), lambda h, p, qi, ki: (qi[p], h)),
                  pl.BlockSpec((hb, t, 1), lambda h, p, qi, ki: (h, qi[p], 0)),
                  pl.BlockSpec((hb, t, 1), lambda h, p, qi, ki: (h, qi[p], 0))],
        out_specs=[pl.BlockSpec((hb, s, 2 * DN), lambda h, p, qi, ki: (h, 0, 0)),
                   pl.BlockSpec((hb, t, 2 * DN), lambda h, p, qi, ki: (h, ki[p], 0)),
                   pl.BlockSpec((hb, t, DN), lambda h, p, qi, ki: (h, ki[p], 0))],
        scratch_shapes=[pltpu.VMEM((hb, t, 2 * DN), F32), pltpu.VMEM((hb, t, DN), F32),
                        pltpu.VMEM((hb, t, t), F32), pltpu.VMEM((hb, t, t), F32),
                        pltpu.VMEM((hb, t, t), BF16), pltpu.VMEM((hb, t, t), BF16)])
    return pl.pallas_call(
        body, grid_spec=grid_spec,
        out_shape=[jax.ShapeDtypeStruct((H, s, 2 * DN), F32), jax.ShapeDtypeStruct((H, s, 2 * DN), F32),
                   jax.ShapeDtypeStruct((H, s, DN), F32)],
        name="attn_bwd", compiler_params=_params(("parallel", "arbitrary"), VMEM_BIG))(
            qi, ki, q, k, v, do, lse, delta)


def _middle(za, o, proj_g, x, tgt, gate, fnw, wco, wao, wo, ts):
    s = x.shape[0]
    inv_d = 1.0 / D

    def body(za_ref, o_ref, bg_ref, ga_ref, gb_ref, x_ref, t_ref, gate_ref, fnw_ref, wco_ref, wao_ref, wo_ref,
             dx2_ref, dza_ref, do_ref, dl_ref, dpg_ref, zb_ref, mg_ref, dmo_ref, dya_ref, dyb_ref, vec_ref):
        @pl.when(pl.program_id(0) == 0)
        def _():
            vec_ref[...] = jnp.zeros_like(vec_ref)

        ov = o_ref[...]
        bg = bg_ref[...]
        sb = _sigmoid(bg)
        silu_b = bg * sb
        zb = (ov * silu_b).astype(BF16)
        zb_ref[...] = zb
        ya = _dot(za_ref[...], wco_ref[...])
        yb = _dot(zb, wao_ref[...])
        sa = _sigmoid(ga_ref[...])
        sg = _sigmoid(gb_ref[...])
        mg = (sa * ya + sg * yb).astype(BF16)
        mg_ref[...] = mg
        mo = _dot(mg, wo_ref[...])
        gate_v = gate_ref[...]
        x2 = x_ref[...] + gate_v * mo
        r = lax.rsqrt(_rowmean(x2 * x2) + EPS)
        xh = x2 * r
        fw = fnw_ref[...]
        e = xh * fw - t_ref[...]
        vec_ref[2:3, :] += _colsum(e * e)
        dy = e * inv_d
        vec_ref[0:1, :] += _colsum(dy * xh)
        dxh = dy * fw
        dx2 = r * (dxh - xh * _rowmean(dxh * xh))
        dx2_ref[...] = dx2
        vec_ref[1:2, :] += _colsum(dx2 * mo)
        dmo = (gate_v * dx2).astype(BF16)
        dmo_ref[...] = dmo
        dmg = _dot_nt(dmo, wo_ref[...])
        dya = (sa * dmg).astype(BF16)
        dyb = (sg * dmg).astype(BF16)
        dya_ref[...] = dya
        dyb_ref[...] = dyb
        dpg_ref[:, D:2 * D] = (dmg * ya * (sa * (1.0 - sa))).astype(BF16)
        dpg_ref[:, 2 * D:3 * D] = (dmg * yb * (sg * (1.0 - sg))).astype(BF16)
        dza_ref[...] = _dot_nt(dya, wco_ref[...])
        dzb = _dot_nt(dyb, wao_ref[...])
        dov = dzb * silu_b
        do_ref[...] = dov.astype(BF16)
        dpg_ref[:, 0:D] = (dzb * ov * _dsilu(bg, sb)).astype(BF16)
        dprod = dov * ov
        for h in range(H):
            dl_ref[h] = jnp.sum(dprod[:, DN * h:DN * (h + 1)], axis=-1, keepdims=True)

    col = lambda c: pl.BlockSpec((ts, D), lambda i, c=c: (i, c))
    row = pl.BlockSpec((ts, D), lambda i: (i, 0))
    vec = pl.BlockSpec((1, D), lambda i: (0, 0))
    wsp = pl.BlockSpec((D, D), lambda i: (0, 0))
    bf = jax.ShapeDtypeStruct((s, D), BF16)
    ff = jax.ShapeDtypeStruct((s, D), F32)
    return pl.pallas_call(
        body, grid=(s // ts,),
        in_specs=[row, row, col(0), col(1), col(2), row, row, vec, vec, wsp, wsp, wsp],
        out_specs=[row, row, row, pl.BlockSpec((H, ts, 1), lambda i: (0, i, 0)),
                   pl.BlockSpec((ts, G_COLS), lambda i: (i, 0)), row, row, row, row, row,
                   pl.BlockSpec((8, D), lambda i: (0, 0))],
        out_shape=[ff, ff, bf, jax.ShapeDtypeStruct((H, s, 1), F32), jax.ShapeDtypeStruct((s, G_COLS), BF16),
                   bf, bf, bf, bf, bf, jax.ShapeDtypeStruct((8, D), F32)],
        name="middle", compiler_params=_params(("arbitrary",), VMEM_BIG))(
            za, o, proj_g, proj_g, proj_g, x, tgt, gate, fnw, wco, wao, wo)


def _input_bwd(dpa, dpl, dpg, wa, wl, wg, x, dx2, norm_w, scale, ts, parts):
    s = x.shape[0]

    def body(dpa_ref, dpl_ref, dpg_ref, wa_ref, wl_ref, wg_ref, x_ref, dx2_ref, nw_ref, sc_ref, gx_ref, gv_ref):
        @pl.when(pl.program_id(0) == 0)
        def _():
            gv_ref[...] = jnp.zeros_like(gv_ref)

        dh = (_dot_nt(dpa_ref[...], wa_ref[...]) + _dot_nt(dpl_ref[...], wl_ref[...])
              + _dot_nt(dpg_ref[...], wg_ref[...]))
        xv = x_ref[...]
        r = lax.rsqrt(_rowmean(xv * xv) + EPS)
        xh = xv * r
        nw = nw_ref[...]
        gv_ref[0:1, :] += _colsum(dh)
        gv_ref[1:2, :] += _colsum(dh * (xh * nw))
        dy = dh * (1.0 + sc_ref[...])
        gv_ref[2:3, :] += _colsum(dy * xh)
        dxh = dy * nw
        gx_ref[...] = dx2_ref[...] + r * (dxh - xh * _rowmean(dxh * xh))

    const = lambda shape: pl.BlockSpec(shape, lambda i: (0, 0))
    rowb = lambda w: pl.BlockSpec((ts, w), lambda i: (i, 0))
    side_in, side_out, side_shapes, side_sems = _scatter_operands(parts)
    outs = pl.pallas_call(
        _scatter_alongside(body, 10, 2, len(parts), s // ts - 1), grid=(s // ts,),
        in_specs=[rowb(A_COLS), rowb(L_COLS), rowb(G_COLS), const((D, A_COLS)), const((D, L_COLS)),
                  const((D, G_COLS)), rowb(D), rowb(D), const((1, D)), const((1, D))] + side_in,
        out_specs=[rowb(D), const((8, D))] + side_out,
        out_shape=[jax.ShapeDtypeStruct((s, D), F32), jax.ShapeDtypeStruct((8, D), F32)] + side_shapes,
        scratch_shapes=side_sems,
        name="input_bwd", compiler_params=_params(("arbitrary",), VMEM_BIG))(
            dpa, dpl, dpg, wa, wl, wg, x, dx2, norm_w, scale, *parts)
    return outs[0], outs[1], list(outs[2:])


def _adamw(w, g, m, v, tr, name):
    rows, cols = w.shape
    c1 = 1.0 - ADAM_B1 ** ADAM_STEP
    c2 = 1.0 - ADAM_B2 ** ADAM_STEP

    def body(w_ref, g_ref, m_ref, v_ref, d_ref, nm_ref, nv_ref):
        gv = g_ref[...]
        nm = ADAM_B1 * m_ref[...] + (1.0 - ADAM_B1) * gv
        nv = ADAM_B2 * v_ref[...] + (1.0 - ADAM_B2) * (gv * gv)
        nm_ref[...] = nm
        nv_ref[...] = nv
        d_ref[...] = -ADAM_LR * ((nm / c1) / (jnp.sqrt(nv / c2) + ADAM_EPS) + ADAM_WD * w_ref[...])

    blk = pl.BlockSpec((tr, cols), lambda i: (i, 0))
    shp = jax.ShapeDtypeStruct((rows, cols), F32)
    return pl.pallas_call(
        body, grid=(rows // tr,), in_specs=[blk] * 4, out_specs=[blk] * 3, out_shape=[shp] * 3, name=name,
        compiler_params=_params(("parallel",)))(w, g, m, v)


def _ada_fwd(c_all, w_ada_shard, b_ada_shard):
    def body(c_ref, w_ref, b_ref, o_ref):
        cv = c_ref[...]
        o_ref[...] = jnp.dot(cv * _sigmoid(cv), w_ref[...], preferred_element_type=F32,
                             precision=lax.Precision.HIGHEST) + b_ref[...]

    return pl.pallas_call(
        body, out_shape=jax.ShapeDtypeStruct((N_DEV, w_ada_shard.shape[1]), F32), name="ada_fwd")(
            c_all, w_ada_shard, b_ada_shard)


def _ada_bwd(c_all_t, dmod_shard):
    def body(c_ref, d_ref, o_ref):
        cv = c_ref[...]
        o_ref[...] = jnp.dot(cv * _sigmoid(cv), d_ref[...], preferred_element_type=F32,
                             precision=lax.Precision.HIGHEST)

    return pl.pallas_call(
        body, out_shape=jax.ShapeDtypeStruct((D, dmod_shard.shape[1]), F32), name="ada_bwd")(c_all_t, dmod_shard)


def _sum_slabs(stack, tr, name):
    n, rows, cols = stack.shape

    def body(s_ref, o_ref):
        acc = s_ref[0]
        for k in range(1, n):
            acc = acc + s_ref[k]
        o_ref[...] = acc

    return pl.pallas_call(
        body, grid=(rows // tr,), in_specs=[pl.BlockSpec((n, tr, cols), lambda i: (0, i, 0))],
        out_specs=pl.BlockSpec((tr, cols), lambda i: (i, 0)), out_shape=jax.ShapeDtypeStruct((rows, cols), F32),
        name=name, compiler_params=_params(("parallel",)))(stack)


def _sum_chip_slabs(arrived, part, place, tr, name):
    n, rows, cols = arrived.shape
    per = rows // tr

    def body(place_ref, a_ref, p_ref, o_ref):
        acc = p_ref[0].astype(F32)
        for k in range(n):
            acc = acc + a_ref[k].astype(F32)
        o_ref[...] = acc

    grid_spec = pltpu.PrefetchScalarGridSpec(
        num_scalar_prefetch=1, grid=(per,),
        in_specs=[pl.BlockSpec((n, tr, cols), lambda i, pc: (0, i, 0)),
                  pl.BlockSpec((1, tr, cols), lambda i, pc: (pc[0], i, 0))],
        out_specs=pl.BlockSpec((tr, cols), lambda i, pc: (pc[1] * per + i, 0)))
    return pl.pallas_call(
        body, grid_spec=grid_spec, out_shape=jax.ShapeDtypeStruct((2 * rows, cols), F32), name=name,
        compiler_params=_params(("parallel",)))(place, arrived, part)


def _add_own_half(full, other, core, tr, name):
    n, rows, cols = other.shape
    per = rows // tr

    def body(c_ref, f_ref, o_ref, out_ref):
        out_ref[...] = (f_ref[...] + o_ref[...]).astype(BF16)

    grid_spec = pltpu.PrefetchScalarGridSpec(
        num_scalar_prefetch=1, grid=(n, per),
        in_specs=[pl.BlockSpec((1, tr, cols), lambda k, i, c: (k, c[0] * per + i, 0)),
                  pl.BlockSpec((1, tr, cols), lambda k, i, c: (k, i, 0))],
        out_specs=pl.BlockSpec((1, tr, cols), lambda k, i, c: (k, i, 0)))
    return pl.pallas_call(
        body, grid_spec=grid_spec, out_shape=jax.ShapeDtypeStruct((n, rows, cols), BF16), name=name,
        compiler_params=_params(("parallel", "parallel")))(core, full, other)


def _allgather8(block, src_rows, vmem, name):
    n = block.shape[1]
    m = src_rows
    sliced = block.shape[0] != m

    def body(x_ref, out_ref, send_sems, recv_sems, local_sem):
        x, y, c = _coords()
        me, sibling = (x, y, c), (x, y, 1 - c)
        chips = [(1 - x, y), (x, 1 - y), (1 - x, 1 - y)]
        src = x_ref.at[pl.ds(pl.multiple_of(c * m, 16), m), :] if sliced else x_ref

        def rows(px, py, pc):
            return out_ref.at[pl.ds(pl.multiple_of((4 * px + 2 * py + pc) * m, 8), m), :]

        def copy(k, blk, to, source=None):
            return pltpu.make_async_remote_copy(
                src_ref=rows(*blk) if source is None else source, dst_ref=rows(*blk),
                send_sem=send_sems.at[k], recv_sem=recv_sems.at[k], device_id=to, device_id_type=MESH)

        mine = pltpu.make_async_copy(src, rows(*me), local_sem)
        mine.start()
        first = [copy(0, me, sibling, source=src)]
        first += [copy(1 + j, me, (*chip, c), source=src) for j, chip in enumerate(chips)]
        for cp in first:
            cp.start()
        passed = [copy(4 + j, (*chip, c), sibling) for j, chip in enumerate(chips)]
        for j, chip in enumerate(chips):
            copy(1 + j, (*chip, c), me).wait_recv()
            passed[j].start()
        copy(0, sibling, me).wait_recv()
        for j, chip in enumerate(chips):
            copy(4 + j, (*chip, 1 - c), me).wait_recv()
        for cp in first + passed:
            cp.wait_send()
        mine.wait()

    space = pltpu.VMEM if vmem else pl.ANY
    return pl.pallas_call(
        body, out_shape=jax.ShapeDtypeStruct((N_DEV * m, n), block.dtype),
        in_specs=[pl.BlockSpec(memory_space=space)], out_specs=pl.BlockSpec(memory_space=space),
        scratch_shapes=[pltpu.SemaphoreType.DMA((7,)), pltpu.SemaphoreType.DMA((7,)), pltpu.SemaphoreType.DMA],
        name=name)(block)


def _gather_weights(shards):
    n = len(shards)
    halves = [a.shape[0] // 2 for a in shards]

    def body(*refs):
        x_refs, out_refs = refs[:n], refs[n:2 * n]
        send_sems, recv_sems, local_sems = refs[2 * n:]
        x, y, c = _coords()
        me, sibling = (x, y, c), (x, y, 1 - c)
        chips = [(1 - x, y), (x, 1 - y), (1 - x, 1 - y)]

        def src(a):
            return x_refs[a].at[pl.ds(pl.multiple_of(c * halves[a], 16), halves[a]), :]

        def blk(a, px, py, pc):
            return out_refs[a].at[4 * px + 2 * py + pc]

        def copy(a, k, who, to, source=None):
            return pltpu.make_async_remote_copy(
                src_ref=blk(a, *who) if source is None else source, dst_ref=blk(a, *who),
                send_sem=send_sems.at[7 * a + k], recv_sem=recv_sems.at[7 * a + k], device_id=to,
                device_id_type=MESH)

        mine = [pltpu.make_async_copy(src(a), blk(a, *me), local_sems.at[a]) for a in range(n)]
        for cp in mine:
            cp.start()
        started = []
        for a in range(n):
            started.append(copy(a, 0, me, sibling, source=src(a)))
            started += [copy(a, 1 + j, me, (*chip, c), source=src(a)) for j, chip in enumerate(chips)]
        for cp in started:
            cp.start()
        for j, chip in enumerate(chips):
            for a in range(n):
                copy(a, 1 + j, (*chip, c), me).wait_recv()
                onward = copy(a, 4 + j, (*chip, c), sibling)
                onward.start()
                started.append(onward)
        for a in range(n):
            copy(a, 0, sibling, me).wait_recv()
        for j, chip in enumerate(chips):
            for a in range(n):
                copy(a, 4 + j, (*chip, 1 - c), me).wait_recv()
        for cp in started:
            cp.wait_send()
        for cp in mine:
            cp.wait()

    outs = pl.pallas_call(
        body, out_shape=[jax.ShapeDtypeStruct((N_DEV, h, a.shape[1]), a.dtype) for a, h in zip(shards, halves)],
        in_specs=[HBM_REF] * n, out_specs=[HBM_REF] * n,
        scratch_shapes=[pltpu.SemaphoreType.DMA((7 * n,)), pltpu.SemaphoreType.DMA((7 * n,)),
                        pltpu.SemaphoreType.DMA((n,))],
        name="gather_weights")(*shards)
    return [o.reshape(N_CHIP, a.shape[0], a.shape[1]) for o, a in zip(outs, shards)]


def _swap_halves_with_sibling(fulls, name):
    n = len(fulls)
    halves = [a.shape[1] // 2 for a in fulls]

    def body(*refs):
        f_refs, got_refs = refs[:n], refs[n:2 * n]
        send_sems, recv_sems = refs[2 * n:]
        x, y, c = _coords()
        copies = []
        for a in range(n):
            src = f_refs[a].at[:, pl.ds(pl.multiple_of((1 - c) * halves[a], 8), halves[a]), :]
            copies.append(pltpu.make_async_remote_copy(
                src_ref=src, dst_ref=got_refs[a], send_sem=send_sems.at[a], recv_sem=recv_sems.at[a],
                device_id=(x, y, 1 - c), device_id_type=MESH))
        for cp in copies:
            cp.start()
        for cp in copies:
            cp.wait()

    return pl.pallas_call(
        body, out_shape=[jax.ShapeDtypeStruct((a.shape[0], h, a.shape[2]), a.dtype) for a, h in zip(fulls, halves)],
        in_specs=[HBM_REF] * n, out_specs=[HBM_REF] * n,
        scratch_shapes=[pltpu.SemaphoreType.DMA((n,)), pltpu.SemaphoreType.DMA((n,))],
        name=name)(*fulls)


def _join_halves_with_sibling(wholes):
    n = len(wholes)

    def body(*refs):
        out_refs = refs[n:2 * n]
        send_sems, recv_sems = refs[2 * n:]
        x, y, c = _coords()

        def push(a, core):
            rows = wholes[a].shape[0] // 2
            half = out_refs[a].at[pl.ds(pl.multiple_of(core * rows, 8), rows), :]
            return pltpu.make_async_remote_copy(
                src_ref=half, dst_ref=half, send_sem=send_sems.at[a], recv_sem=recv_sems.at[a],
                device_id=(x, y, 1 - c), device_id_type=MESH)

        for a in range(n):
            push(a, c).start()
        for a in range(n):
            push(a, 1 - c).wait_recv()
        for a in range(n):
            push(a, c).wait_send()

    return pl.pallas_call(
        body, out_shape=[jax.ShapeDtypeStruct(a.shape, a.dtype) for a in wholes],
        in_specs=[HBM_REF] * n, out_specs=[HBM_REF] * n, input_output_aliases={a: a for a in range(n)},
        scratch_shapes=[pltpu.SemaphoreType.DMA((n,)), pltpu.SemaphoreType.DMA((n,))],
        name="rs_pair_join")(*wholes)


def _cols_to_slabs(g):
    rows, cols = g.shape
    return g.reshape(rows, N_CHIP, cols // N_CHIP).transpose(1, 0, 2)


def _slabs_to_cols(w):
    n, rows, cols = w.shape
    return w.transpose(1, 0, 2).reshape(rows, n * cols)


def _uq_to_padded(w_uq):
    per = w_uq.reshape(RQ, H, DN + DR)
    nope = per[:, :, :DN].reshape(RQ, H * DN)
    rope = jnp.pad(per[:, :, DN:], ((0, 0), (0, 0), (0, LANE - DR))).reshape(RQ, H * LANE)
    return jnp.concatenate([nope, rope], axis=1)


def _uq_from_padded(g):
    nope = g[:, :H * DN].reshape(RQ, H, DN)
    rope = g[:, H * DN:].reshape(RQ, H, LANE)[:, :, :DR]
    return jnp.concatenate([nope, rope], axis=2).reshape(RQ, H * (DN + DR))


def _rope_tables(positions):
    inv_freq = ROPE_THETA ** (-jnp.arange(0, DR, 2, dtype=F32) / DR)
    ang = positions.astype(F32)[:, None] * inv_freq
    cos, sin = jnp.cos(ang), jnp.sin(ang)
    return jnp.tile(cos, (1, 4)), jnp.tile(jnp.concatenate([-sin, sin], axis=1), (1, 2))


def _pair_sums(fulls, core, tag):
    from_sibling = _swap_halves_with_sibling(fulls, f"rs_pair_swap_{tag}")
    return [_add_own_half(f, o, core, min(256, o.shape[1]), f"add_own_half_{tag}{n}")
            for n, (f, o) in enumerate(zip(fulls, from_sibling))]


def _local_step(x, tgt, cos_t, sin_t, mod, weights, small, tiles, place):
    ts, ts_in, tm_nn, tm_tn, t_attn, chunk = tiles
    wa, wl, wg, w_uq2, w_ukv, wco, wao, wo, conv_w = weights
    norm_w, conv_b, ln_w, ln_b, q_norm_w, kv_norm_w, fnw = small
    shift, scale, gate = mod[:, 0:D], mod[:, D:2 * D], mod[:, 2 * D:3 * D]

    h = _adaln_norm(x, norm_w, shift, scale, ts)
    proj_a = _mm_nn(h, wa, tm_nn, D, "proj_a")
    proj_l = _mm_nn(h, wl, tm_nn, L_COLS, "proj_l")
    proj_g = _mm_nn(h, wg, tm_nn, D, "proj_g")
    u0, u1, za = _conv_fwd(proj_a, conv_w, conv_b, ln_w, ln_b, ts, chunk)
    qn, kvn, q, k, v = _mla_prep(proj_l, q_norm_w, kv_norm_w, w_uq2, w_ukv, cos_t, sin_t, ts)
    o, lse = _attn_fwd(q, k, v, t_attn)
    (dx2, dza, do, delta, dpg, zb, mg, dmo, dya, dyb, vec_mid) = _middle(
        za, o, proj_g, x, tgt, gate, fnw, wco, wao, wo, ts)
    g_wo = _mm_tn(mg, dmo, tm_tn, D, "grad_w_out")
    g_wco = _mm_tn(za, dya, tm_tn, D, "grad_w_conv_out")
    g_wao = _mm_tn(zb, dyb, tm_tn, D, "grad_w_attn_out")
    dq, dk, dv = _attn_bwd(q, k, v, do, lse, delta, t_attn)
    dpl, g_wuq2, g_wukv, vec_mla = _mla_prep_bwd(
        dq, dk, dv, proj_l, qn, kvn, q_norm_w, kv_norm_w, w_uq2, w_ukv, cos_t, sin_t, ts)

    core = place[1:2]
    nr = D // N_CHIP
    early = [_cols_to_slabs(_uq_from_padded(g_wuq2)), _cols_to_slabs(g_wukv), g_wco.reshape(N_CHIP, nr, D),
             g_wao.reshape(N_CHIP, nr, D), g_wo.reshape(N_CHIP, nr, D)]
    early_sums = _pair_sums(early, core, "a")
    dpa, g_conv_w, vec_conv, early_got = _conv_bwd(dza, proj_a, u0, u1, conv_w, ln_w, ln_b, ts, chunk, early_sums)

    g_wa = _mm_tn(h, dpa, tm_tn, D, "grad_w_in_a")
    g_wl = _mm_tn(h, dpl, tm_tn, L_COLS, "grad_w_in_l")
    g_wg = _mm_tn(h, dpg, tm_tn, D, "grad_w_in_g")
    g_w_in = jnp.concatenate([g_wa, g_wl[:, 0:L_COLS_RAW], g_wg], axis=1)
    late_sums = _pair_sums([_cols_to_slabs(g_w_in)], core, "b")
    grad_x, vec_in, late_got = _input_bwd(dpa, dpl, dpg, wa, wl, wg, x, dx2, norm_w, scale, ts_in, late_sums)

    wholes = [_sum_chip_slabs(a, p, place, min(128, a.shape[1]), f"sum_chip_slabs_{n}")
              for n, (a, p) in enumerate(zip(late_got + early_got, late_sums + early_sums))]
    shards = _join_halves_with_sibling(wholes)

    dmod = jnp.concatenate([vec_in[0:1], vec_in[1:2], vec_mid[1:2]], axis=1)
    sums = dict(dmod=dmod, norm_w=vec_in[2:3], conv_b=vec_conv[2:3], ln_w=vec_conv[0:1], ln_b=vec_conv[1:2],
                q_norm_w=vec_mla[0:1], kv_norm_w=vec_mla[1:2], final_norm_w=vec_mid[0:1], loss=vec_mid[2:3],
                conv_w=g_conv_w)
    return grad_x, shards, sums


SMALL_ORDER = (("dmod", 3 * D), ("norm_w", D), ("conv_b", D), ("ln_w", D), ("ln_b", D), ("q_norm_w", RQ),
               ("kv_norm_w", RQ), ("final_norm_w", D), ("loss", D), ("conv_w", HALO * D))
SMALL_ROWS = 336


def kernel(x, c, positions, w_ada, b_ada, norm_w, w_in, conv_w, conv_b, conv_ln_w, conv_ln_b, w_conv_out, q_norm_w, w_uq, kv_norm_w, w_ukv, w_attn_out, w_out, final_norm_w, loss_target, m_w_ada, m_b_ada, m_norm_w, m_w_in, m_conv_w, m_conv_b, m_conv_ln_w, m_conv_ln_b, m_w_conv_out, m_q_norm_w, m_w_uq, m_kv_norm_w, m_w_ukv, m_w_attn_out, m_w_out, m_final_norm_w, v_w_ada, v_b_ada, v_norm_w, v_w_in, v_conv_w, v_conv_b, v_conv_ln_w, v_conv_ln_b, v_w_conv_out, v_q_norm_w, v_w_uq, v_kv_norm_w, v_w_ukv, v_w_attn_out, v_w_out, v_final_norm_w):
    ix, iy, ic = _coords()
    chip = 2 * ix + iy
    dev = 4 * ix + 2 * iy + ic
    s = x.shape[1]
    tiles = (256, 512, 1024, 2048, 512, 32)

    conv_w_pad = jnp.pad(conv_w[0], ((0, HALO - KC), (0, 0)))
    small_in = jnp.concatenate([c.reshape(8, LANE), conv_w_pad.reshape(64, LANE)], axis=0)
    small_all = _allgather8(small_in, 72, True, "gather_c_conv").reshape(N_DEV, 72, LANE)
    c_all = small_all[:, 0:8].reshape(N_DEV, D)
    conv_full = jnp.concatenate(
        [small_all[2 * k, 8:72].reshape(HALO, D // N_CHIP) for k in range(N_CHIP)], axis=1)

    shards = [w[0].astype(BF16) for w in (w_in, w_uq, w_ukv, w_conv_out, w_attn_out, w_out)]
    g_in, g_uq, g_ukv, g_co, g_ao, g_o = _gather_weights(shards)
    w_in_f, w_uq_f, w_ukv_f = _slabs_to_cols(g_in), _slabs_to_cols(g_uq), _slabs_to_cols(g_ukv)
    wco, wao, wo = g_co.reshape(D, D), g_ao.reshape(D, D), g_o.reshape(D, D)
    wa = w_in_f[:, 0:A_COLS]
    wl = jnp.pad(w_in_f[:, A_COLS:A_COLS + L_COLS_RAW], ((0, 0), (0, L_COLS - L_COLS_RAW)))
    wg = w_in_f[:, A_COLS + L_COLS_RAW:]
    weights = (wa, wl, wg, _uq_to_padded(w_uq_f), w_ukv_f, wco, wao, wo, conv_full)

    ada_cols = w_ada.shape[2]
    b_shard = lax.dynamic_slice(b_ada, (0, chip * ada_cols), (1, ada_cols))
    mod_part = _ada_fwd(c_all, w_ada[0], b_shard)
    mod_all = _allgather8(mod_part, N_DEV, True, "gather_mod").reshape(N_DEV, N_DEV, ada_cols)
    mod = jnp.concatenate(
        [lax.dynamic_slice(mod_all[2 * k], (dev, 0), (1, ada_cols)) for k in range(N_CHIP)], axis=1)

    cos_t, sin_t = _rope_tables(positions[0])
    small = (norm_w, conv_b, conv_ln_w, conv_ln_b, q_norm_w, kv_norm_w, final_norm_w.reshape(1, D))
    place = jnp.stack([chip, ic]).astype(jnp.int32)
    grad_x, shards, sums = _local_step(x[0], loss_target[0], cos_t, sin_t, mod, weights, small, tiles, place)
    g_w_in_s, g_w_uq_s, g_w_ukv_s, g_wco_s, g_wao_s, g_wo_s = shards

    small_flat = jnp.concatenate([sums[name].reshape(-1) for name, _ in SMALL_ORDER])
    small_flat = jnp.pad(small_flat, (0, SMALL_ROWS * LANE - small_flat.shape[0]))
    small_g = _allgather8(small_flat.reshape(SMALL_ROWS, LANE), SMALL_ROWS, True, "gather_small_grads")
    small_g = small_g.reshape(N_DEV, SMALL_ROWS, LANE)
    small_sum = _sum_slabs(small_g, SMALL_ROWS, "sum_small_grads").reshape(-1)
    tot, pos = {}, 0
    for name, size in SMALL_ORDER:
        tot[name] = small_sum[pos:pos + size]
        pos += size
    loss = (0.5 / D) * jnp.sum(tot["loss"])
    dmod_all = small_g.reshape(N_DEV, -1)[:, 0:3 * D]
    g_b_ada = tot["dmod"].reshape(1, 3 * D)
    dmod_shard = lax.dynamic_slice(dmod_all, (0, chip * ada_cols), (N_DEV, ada_cols))
    g_w_ada = _ada_bwd(c_all.T, dmod_shard).reshape(1, D, ada_cols)
    g_conv_w = lax.dynamic_slice(tot["conv_w"].reshape(HALO, D), (0, chip * (D // N_CHIP)), (KC, D // N_CHIP))
    g_conv_w = g_conv_w.reshape(1, KC, D // N_CHIP)

    def big(w, g, m, v, tr, name):
        d, nm, nv = _adamw(w[0], g, m[0], v[0], tr, name)
        return g[None], d[None], nm[None], nv[None]

    vec_names = ("b_ada", "norm_w", "conv_b", "conv_ln_w", "conv_ln_b", "q_norm_w", "kv_norm_w", "final_norm_w")
    vec_w = (b_ada, norm_w, conv_b, conv_ln_w, conv_ln_b, q_norm_w, kv_norm_w, final_norm_w)
    vec_m = (m_b_ada, m_norm_w, m_conv_b, m_conv_ln_w, m_conv_ln_b, m_q_norm_w, m_kv_norm_w, m_final_norm_w)
    vec_v = (v_b_ada, v_norm_w, v_conv_b, v_conv_ln_w, v_conv_ln_b, v_q_norm_w, v_kv_norm_w, v_final_norm_w)
    vec_g = (g_b_ada, tot["norm_w"], tot["conv_b"], tot["ln_w"], tot["ln_b"], tot["q_norm_w"], tot["kv_norm_w"],
             tot["final_norm_w"])
    vec_g = tuple(g.reshape(w.shape) for g, w in zip(vec_g, vec_w))
    cat = lambda arrs: jnp.concatenate([a.reshape(-1) for a in arrs]).reshape(-1, LANE)
    vd, vnm, vnv = _adamw(cat(vec_w), cat(vec_g), cat(vec_m), cat(vec_v), cat(vec_w).shape[0], "adamw_vectors")

    def split(packed):
        flat, out, pos = packed.reshape(-1), [], 0
        for w in vec_w:
            out.append(flat[pos:pos + w.size].reshape(w.shape))
            pos += w.size
        return out

    res = {}
    for name, g, d, nm, nv in zip(vec_names, vec_g, split(vd), split(vnm), split(vnv)):
        res[name] = (g, d, nm, nv)
    res["w_ada"] = big(w_ada, g_w_ada[0], m_w_ada, v_w_ada, 256, "adamw_w_ada")
    res["w_in"] = big(w_in, g_w_in_s, m_w_in, v_w_in, 256, "adamw_w_in")
    res["conv_w"] = big(conv_w, g_conv_w[0], m_conv_w, v_conv_w, KC, "adamw_conv_w")
    res["w_conv_out"] = big(w_conv_out, g_wco_s, m_w_conv_out, v_w_conv_out, 256, "adamw_w_conv_out")
    res["w_uq"] = big(w_uq, g_w_uq_s, m_w_uq, v_w_uq, 256, "adamw_w_uq")
    res["w_ukv"] = big(w_ukv, g_w_ukv_s, m_w_ukv, v_w_ukv, 256, "adamw_w_ukv")
    res["w_attn_out"] = big(w_attn_out, g_wao_s, m_w_attn_out, v_w_attn_out, 256, "adamw_w_attn_out")
    res["w_out"] = big(w_out, g_wo_s, m_w_out, v_w_out, 256, "adamw_w_out")

    order = ("w_ada", "b_ada", "norm_w", "w_in", "conv_w", "conv_b", "conv_ln_w", "conv_ln_b", "w_conv_out",
             "q_norm_w", "w_uq", "kv_norm_w", "w_ukv", "w_attn_out", "w_out", "final_norm_w")
    outs = [loss, grad_x[None]]
    for slot in range(4):
        outs += [res[name][slot] for name in order]
    return tuple(outs)
```

```python
import functools

import numpy as np
import jax
import jax.numpy as jnp
from jax import lax
from jax.experimental import pallas as pl
from jax.experimental.pallas import tpu as pltpu

F32 = jnp.float32
BF16 = jnp.bfloat16
MESH = pl.DeviceIdType.MESH

D = 1024
H = 8
DN = 128
DR = 64
RQ = 256
KC = 31
HALO = 32
EPS = 1e-6
ROPE_THETA = 10000.0
N_CHIP = 4
N_DEV = 8
LANE = 128
VMEM_BIG = 56 * 1024 * 1024

ADAM_LR = 0.001
ADAM_B1 = 0.9
ADAM_B2 = 0.999
ADAM_EPS = 1e-08
ADAM_WD = 0.01
ADAM_STEP = 10

A_COLS = 3 * D
L_COLS_RAW = RQ + RQ + DR
L_COLS = 640
G_COLS = 3 * D
IN_COLS = A_COLS + L_COLS_RAW + G_COLS


def _params(sem=None, vmem=None):
    kw = {}
    if sem is not None:
        kw["dimension_semantics"] = sem
    if vmem is not None:
        kw["vmem_limit_bytes"] = vmem
    return pltpu.CompilerParams(**kw)


def _dot(a, b):
    return jnp.dot(a, b, preferred_element_type=F32)


def _dot_nt(a, b):
    return lax.dot_general(a, b, (((1,), (1,)), ((), ())), preferred_element_type=F32)


def _dot_tn(a, b):
    return lax.dot_general(a, b, (((0,), (0,)), ((), ())), preferred_element_type=F32)


def _colsum(v):
    return jnp.sum(v, axis=0, keepdims=True)


def _rowmean(v):
    return jnp.mean(v, axis=-1, keepdims=True)


def _sigmoid(v):
    return jax.nn.sigmoid(v)


def _dsilu(v, s):
    return s * (1.0 + v * (1.0 - s))


def _swap_halves(v, first_half):
    return jnp.where(first_half, pltpu.roll(v, 96, 1), pltpu.roll(v, 32, 1))


def _first_half_mask(rows):
    lane = lax.broadcasted_iota(jnp.int32, (rows, LANE), 1)
    return (lane % 64) < 32


def _adaln_norm(x, norm_w, shift, scale, ts):
    s = x.shape[0]

    def body(x_ref, nw_ref, sh_ref, sc_ref, h_ref):
        xv = x_ref[...]
        r = lax.rsqrt(_rowmean(xv * xv) + EPS)
        y = xv * r * nw_ref[...]
        h_ref[...] = (y * (1.0 + sc_ref[...]) + sh_ref[...]).astype(BF16)

    row = pl.BlockSpec((ts, D), lambda i: (i, 0))
    vec = pl.BlockSpec((1, D), lambda i: (0, 0))
    return pl.pallas_call(
        body, grid=(s // ts,), in_specs=[row, vec, vec, vec], out_specs=row,
        out_shape=jax.ShapeDtypeStruct((s, D), BF16), name="adaln_norm",
        compiler_params=_params(("parallel",)))(x, norm_w, shift, scale)


def _mm_nn(a, b, tm, tn, name):
    m, k = a.shape
    n = b.shape[1]

    def body(a_ref, b_ref, o_ref):
        o_ref[...] = _dot(a_ref[...], b_ref[...])

    return pl.pallas_call(
        body, grid=(n // tn, m // tm),
        in_specs=[pl.BlockSpec((tm, k), lambda j, i: (i, 0)), pl.BlockSpec((k, tn), lambda j, i: (0, j))],
        out_specs=pl.BlockSpec((tm, tn), lambda j, i: (i, j)),
        out_shape=jax.ShapeDtypeStruct((m, n), F32), name=name,
        compiler_params=_params(("parallel", "parallel"), VMEM_BIG))(a, b)


def _mm_tn(a, b, tm, tn, name):
    m, k = a.shape
    n = b.shape[1]

    def body(a_ref, b_ref, o_ref):
        @pl.when(pl.program_id(1) == 0)
        def _():
            o_ref[...] = jnp.zeros_like(o_ref)
        o_ref[...] += _dot_tn(a_ref[...], b_ref[...])

    return pl.pallas_call(
        body, grid=(n // tn, m // tm),
        in_specs=[pl.BlockSpec((tm, k), lambda j, i: (i, 0)), pl.BlockSpec((tm, tn), lambda j, i: (i, j))],
        out_specs=pl.BlockSpec((k, tn), lambda j, i: (0, j)),
        out_shape=jax.ShapeDtypeStruct((k, n), F32), name=name,
        compiler_params=_params(("parallel", "arbitrary"), VMEM_BIG))(a, b)


def _coords():
    return lax.axis_index("x"), lax.axis_index("y"), lax.axis_index("c")


HBM_REF = pl.BlockSpec(memory_space=pl.ANY)


def _chip_scatter_copies(p_refs, got_refs, send_sems, recv_sems):
    x, y, c = _coords()
    copies = []
    for a in range(len(p_refs)):
        for j, (px, py) in enumerate([(1 - x, y), (x, 1 - y), (1 - x, 1 - y)]):
            copies.append(pltpu.make_async_remote_copy(
                src_ref=p_refs[a].at[2 * px + py], dst_ref=got_refs[a].at[j], send_sem=send_sems.at[3 * a + j],
                recv_sem=recv_sems.at[3 * a + j], device_id=(px, py, c), device_id_type=MESH))
    return copies


def _scatter_alongside(body, n_in, n_out, n_parts, last_step):
    def wrapped(*refs):
        ins, parts = refs[:n_in], refs[n_in:n_in + n_parts]
        rest = refs[n_in + n_parts:]
        outs, got = rest[:n_out], rest[n_out:n_out + n_parts]
        scratch, (send_sems, recv_sems) = rest[n_out + n_parts:-2], rest[-2:]

        @pl.when(pl.program_id(0) == 0)
        def _():
            for cp in _chip_scatter_copies(parts, got, send_sems, recv_sems):
                cp.start()

        body(*ins, *outs, *scratch)

        @pl.when(pl.program_id(0) == last_step)
        def _():
            for cp in _chip_scatter_copies(parts, got, send_sems, recv_sems):
                cp.wait()

    return wrapped


def _scatter_operands(parts):
    n = len(parts)
    shapes = [jax.ShapeDtypeStruct((3,) + a.shape[1:], a.dtype) for a in parts]
    sems = [pltpu.SemaphoreType.DMA((3 * n,)), pltpu.SemaphoreType.DMA((3 * n,))]
    return [HBM_REF] * n, [HBM_REF] * n, shapes, sems


def _shifted_copies(win_ref, sh_ref, rows):
    for p in range(1, 8):
        sh_ref[p - 1, 0:rows, :] = win_ref[pl.ds(p, rows), :]


def _tap_rows(win_ref, sh_ref, start, rows):
    p = start % 8
    if p == 0:
        return win_ref[pl.ds(start, rows), :]
    return sh_ref[p - 1, pl.ds(start - p, rows), :]


def _conv_taps(win_ref, sh_ref, w_ref, rows, chunk, offset_of_tap):
    pieces = []
    for c0 in range(0, rows, chunk):
        acc = None
        for j in range(KC):
            term = w_ref[j:j + 1, :] * _tap_rows(win_ref, sh_ref, c0 + offset_of_tap(j), chunk)
            acc = term if acc is None else acc + term
        pieces.append(acc)
    return pieces


def _conv_fwd(proj_a, conv_w, conv_b, ln_w, ln_b, ts, chunk, shards):
    s = proj_a.shape[0]

    def body(av_ref, al_ref, ag_ref, w_ref, b_ref, lw_ref, lb_ref, u0_ref, u1_ref, za_ref, win_ref, sh_ref):
        @pl.when(pl.program_id(0) == 0)
        def _():
            win_ref[0:HALO, :] = jnp.zeros((HALO, D), F32)

        u0 = av_ref[...] * _sigmoid(al_ref[...])
        u0_ref[...] = u0
        win_ref[HALO:HALO + ts, :] = u0
        _shifted_copies(win_ref, sh_ref, ts + HALO - 8)
        pieces = _conv_taps(win_ref, sh_ref, w_ref, ts, chunk, lambda j: HALO - (KC - 1) + j)
        for n, acc in enumerate(pieces):
            u1_ref[n * chunk:(n + 1) * chunk, :] = acc + b_ref[...]
        win_ref[0:HALO, :] = win_ref[ts:ts + HALO, :]

        u1 = u1_ref[...]
        xc = u1 - _rowmean(u1)
        rstd = lax.rsqrt(_rowmean(xc * xc) + EPS)
        u2 = xc * rstd * lw_ref[...] + lb_ref[...]
        u3 = u2 * _sigmoid(u2)
        ag = ag_ref[...]
        za_ref[...] = (u3 * (ag * _sigmoid(ag))).astype(BF16)

    col = lambda c: pl.BlockSpec((ts, D), lambda i, c=c: (i, c))
    row = pl.BlockSpec((ts, D), lambda i: (i, 0))
    vec = pl.BlockSpec((1, D), lambda i: (0, 0))
    n = len(shards)
    gathered_shapes, sems = _gather_operands(shards)
    outs = pl.pallas_call(
        _gather_alongside(body, 7, 3, n, s // ts - 1), grid=(s // ts,),
        in_specs=[col(0), col(1), col(2), pl.BlockSpec((HALO, D), lambda i: (0, 0)), vec, vec, vec] + [HBM_REF] * n,
        out_specs=[row, row, row] + [HBM_REF] * n,
        out_shape=[jax.ShapeDtypeStruct((s, D), F32), jax.ShapeDtypeStruct((s, D), F32),
                   jax.ShapeDtypeStruct((s, D), BF16)] + gathered_shapes,
        scratch_shapes=[pltpu.VMEM((ts + HALO, D), F32), pltpu.VMEM((7, ts + HALO, D), F32)] + sems,
        name="conv_fwd", compiler_params=_params(("arbitrary",), VMEM_BIG))(
            proj_a, proj_a, proj_a, conv_w, conv_b, ln_w, ln_b, *shards)
    return outs[0], outs[1], outs[2], _as_chip_slabs(outs[3:], shards)


def _conv_bwd(dza, proj_a, u0, u1, conv_w, ln_w, ln_b, ts, chunk, parts):
    s = dza.shape[0]
    nt = s // ts
    per = ts // HALO

    def body(dza_ref, av_ref, al_ref, ag_ref, u0_ref, u0p_ref, u1_ref, w_ref, lw_ref, lb_ref,
             dpa_ref, gw_ref, gv_ref, dwin_ref, uwin_ref, du0_ref, gwp_ref, dsh_ref, ush_ref):
        step = pl.program_id(0)
        tile = nt - 1 - step

        @pl.when(step == 0)
        def _():
            dwin_ref[ts:ts + HALO, :] = jnp.zeros((HALO, D), F32)
            gwp_ref[...] = jnp.zeros_like(gwp_ref)
            gv_ref[...] = jnp.zeros_like(gv_ref)

        ag = ag_ref[...]
        sg = _sigmoid(ag)
        u1 = u1_ref[...]
        xc = u1 - _rowmean(u1)
        rstd = lax.rsqrt(_rowmean(xc * xc) + EPS)
        xh = xc * rstd
        u2 = xh * lw_ref[...] + lb_ref[...]
        s2 = _sigmoid(u2)
        dz = dza_ref[...]
        du3 = dz * (ag * sg)
        dpa_ref[:, 2 * D:3 * D] = (dz * (u2 * s2) * _dsilu(ag, sg)).astype(BF16)
        du2 = du3 * _dsilu(u2, s2)
        gv_ref[0:1, :] += _colsum(du2 * xh)
        gv_ref[1:2, :] += _colsum(du2)
        dxh = du2 * lw_ref[...]
        du1 = rstd * (dxh - _rowmean(dxh) - xh * _rowmean(dxh * xh))
        gv_ref[2:3, :] += _colsum(du1)
        dwin_ref[0:ts, :] = du1

        uwin_ref[0:HALO, :] = jnp.where(tile == 0, 0.0, u0p_ref[...])
        uwin_ref[HALO:HALO + ts, :] = u0_ref[...]

        _shifted_copies(dwin_ref, dsh_ref, ts + HALO - 8)
        _shifted_copies(uwin_ref, ush_ref, ts + HALO - 8)
        pieces = _conv_taps(dwin_ref, dsh_ref, w_ref, ts, chunk, lambda j: (KC - 1) - j)
        for n, acc in enumerate(pieces):
            du0_ref[n * chunk:(n + 1) * chunk, :] = acc
        for c0 in range(0, ts, chunk):
            dchunk = dwin_ref[c0:c0 + chunk, :]
            for j in range(KC):
                prod = dchunk * _tap_rows(uwin_ref, ush_ref, c0 + HALO - (KC - 1) + j, chunk)
                gwp_ref[8 * j:8 * j + 8, :] += jnp.sum(prod.reshape(chunk // 8, 8, D), axis=0)
        dwin_ref[ts:ts + HALO, :] = dwin_ref[0:HALO, :]

        du0 = du0_ref[...]
        al = al_ref[...]
        sl = _sigmoid(al)
        dpa_ref[:, 0:D] = (du0 * sl).astype(BF16)
        dpa_ref[:, D:2 * D] = (du0 * av_ref[...] * sl * (1.0 - sl)).astype(BF16)

        @pl.when(step == nt - 1)
        def _():
            for j in range(KC):
                gw_ref[j:j + 1, :] = _colsum(gwp_ref[8 * j:8 * j + 8, :])
            gw_ref[KC:HALO, :] = jnp.zeros((HALO - KC, D), F32)

    rev = lambda i: nt - 1 - i
    col = lambda c: pl.BlockSpec((ts, D), lambda i, c=c: (rev(i), c))
    row = pl.BlockSpec((ts, D), lambda i: (rev(i), 0))
    vec = pl.BlockSpec((1, D), lambda i: (0, 0))
    halo = pl.BlockSpec((HALO, D), lambda i: (jnp.maximum(rev(i) * per - 1, 0), 0))
    side_in, side_out, side_shapes, side_sems = _scatter_operands(parts)
    outs = pl.pallas_call(
        _scatter_alongside(body, 10, 3, len(parts), nt - 1), grid=(nt,),
        in_specs=[row, col(0), col(1), col(2), row, halo, row, pl.BlockSpec((HALO, D), lambda i: (0, 0)), vec, vec]
        + side_in,
        out_specs=[pl.BlockSpec((ts, A_COLS), lambda i: (rev(i), 0)),
                   pl.BlockSpec((HALO, D), lambda i: (0, 0)), pl.BlockSpec((8, D), lambda i: (0, 0))] + side_out,
        out_shape=[jax.ShapeDtypeStruct((s, A_COLS), BF16), jax.ShapeDtypeStruct((HALO, D), F32),
                   jax.ShapeDtypeStruct((8, D), F32)] + side_shapes,
        scratch_shapes=[pltpu.VMEM((ts + HALO, D), F32), pltpu.VMEM((ts + HALO, D), F32),
                        pltpu.VMEM((ts, D), F32), pltpu.VMEM((8 * HALO, D), F32),
                        pltpu.VMEM((7, ts + HALO, D), F32), pltpu.VMEM((7, ts + HALO, D), F32)] + side_sems,
        name="conv_bwd", compiler_params=_params(("arbitrary",), VMEM_BIG))(
            dza, proj_a, proj_a, proj_a, u0, u0, u1, conv_w, ln_w, ln_b, *parts)
    return outs[0], outs[1], outs[2], list(outs[3:])


def _mla_prep(proj_l, q_norm_w, kv_norm_w, w_uq2, w_ukv, cos_t, sin_t, ts):
    s = proj_l.shape[0]

    def body(pl_ref, qw_ref, kw_ref, wq_ref, wkv_ref, c_ref, s_ref, qn_ref, kvn_ref, q_ref, k_ref, v_ref):
        first = _first_half_mask(ts)
        cs = c_ref[...]
        sn = s_ref[...]

        def rms(v, w):
            return v * lax.rsqrt(_rowmean(v * v) + EPS) * w

        def rope(v):
            return v * cs + _swap_halves(v, first) * sn

        qn = rms(pl_ref[:, 0:RQ], qw_ref[...]).astype(BF16)
        kvn = rms(pl_ref[:, RQ:2 * RQ], kw_ref[...]).astype(BF16)
        qn_ref[...] = qn
        kvn_ref[...] = kvn
        q = _dot(qn, wq_ref[...])
        kv = _dot(kvn, wkv_ref[...])
        kr = rope(pl_ref[:, 2 * RQ:2 * RQ + LANE]).astype(BF16)
        for h in range(H):
            q_ref[h, :, 0:DN] = q[:, DN * h:DN * (h + 1)].astype(BF16)
            q_ref[h, :, DN:2 * DN] = rope(q[:, H * DN + LANE * h:H * DN + LANE * (h + 1)]).astype(BF16)
            k_ref[h, :, 0:DN] = kv[:, 2 * DN * h:2 * DN * h + DN].astype(BF16)
            k_ref[h, :, DN:2 * DN] = kr
            v_ref[h, :, 0:DN] = kv[:, 2 * DN * h + DN:2 * DN * (h + 1)].astype(BF16)
            v_ref[h, :, DN:2 * DN] = jnp.ones((ts, DN), BF16)

    const = lambda shape: pl.BlockSpec(shape, lambda i: (0,) * len(shape))
    rowb = lambda w: pl.BlockSpec((ts, w), lambda i: (i, 0))
    head = lambda w: pl.BlockSpec((H, ts, w), lambda i: (0, i, 0))
    return pl.pallas_call(
        body, grid=(s // ts,),
        in_specs=[rowb(L_COLS), const((1, RQ)), const((1, RQ)), const((RQ, 2 * H * DN)), const((RQ, 2 * H * DN)),
                  rowb(LANE), rowb(LANE)],
        out_specs=[rowb(RQ), rowb(RQ), head(2 * DN), head(2 * DN), head(2 * DN)],
        out_shape=[jax.ShapeDtypeStruct((s, RQ), BF16), jax.ShapeDtypeStruct((s, RQ), BF16),
                   jax.ShapeDtypeStruct((H, s, 2 * DN), BF16), jax.ShapeDtypeStruct((H, s, 2 * DN), BF16),
                   jax.ShapeDtypeStruct((H, s, 2 * DN), BF16)],
        name="mla_prep", compiler_params=_params(("parallel",)))(
            proj_l, q_norm_w, kv_norm_w, w_uq2, w_ukv, cos_t, sin_t)


def _mla_prep_bwd(dq, dk, dv, proj_l, qn, kvn, q_norm_w, kv_norm_w, w_uq2, w_ukv, cos_t, sin_t, ts):
    s = proj_l.shape[0]

    def body(dq_ref, dk_ref, dv_ref, pl_ref, qn_ref, kvn_ref, qw_ref, kw_ref, wq_ref, wkv_ref, c_ref, s_ref,
             dpl_ref, gwq_ref, gwkv_ref, gv_ref, dq2_ref, dkv2_ref):
        @pl.when(pl.program_id(0) == 0)
        def _():
            gwq_ref[...] = jnp.zeros_like(gwq_ref)
            gwkv_ref[...] = jnp.zeros_like(gwkv_ref)
            gv_ref[...] = jnp.zeros_like(gv_ref)

        first = _first_half_mask(ts)
        cs = c_ref[...]
        sn = s_ref[...]

        def rope_bwd(g):
            return g * cs + _swap_halves(g * sn, first)

        def rms_bwd(v, w, dy):
            r = lax.rsqrt(_rowmean(v * v) + EPS)
            vh = v * r
            dvh = dy * w
            return r * (dvh - vh * _rowmean(dvh * vh)), _colsum(dy * vh)

        dkr = None
        for h in range(H):
            dq2_ref[:, DN * h:DN * (h + 1)] = dq_ref[h, :, 0:DN].astype(BF16)
            dq2_ref[:, H * DN + LANE * h:H * DN + LANE * (h + 1)] = rope_bwd(dq_ref[h, :, DN:2 * DN]).astype(BF16)
            dkv2_ref[:, 2 * DN * h:2 * DN * h + DN] = dk_ref[h, :, 0:DN].astype(BF16)
            dkv2_ref[:, 2 * DN * h + DN:2 * DN * (h + 1)] = dv_ref[h].astype(BF16)
            part = dk_ref[h, :, DN:2 * DN]
            dkr = part if dkr is None else dkr + part

        dq2 = dq2_ref[...]
        dkv2 = dkv2_ref[...]
        gwq_ref[...] += _dot_tn(qn_ref[...], dq2)
        gwkv_ref[...] += _dot_tn(kvn_ref[...], dkv2)
        dcq, gq = rms_bwd(pl_ref[:, 0:RQ], qw_ref[...], _dot_nt(dq2, wq_ref[...]))
        dckv, gkv = rms_bwd(pl_ref[:, RQ:2 * RQ], kw_ref[...], _dot_nt(dkv2, wkv_ref[...]))
        gv_ref[0:1, :] += gq
        gv_ref[1:2, :] += gkv
        dpl_ref[:, 0:RQ] = dcq.astype(BF16)
        dpl_ref[:, RQ:2 * RQ] = dckv.astype(BF16)
        dpl_ref[:, 2 * RQ:2 * RQ + LANE] = rope_bwd(dkr).astype(BF16)

    const = lambda shape: pl.BlockSpec(shape, lambda i: (0,) * len(shape))
    rowb = lambda w: pl.BlockSpec((ts, w), lambda i: (i, 0))
    head = lambda w: pl.BlockSpec((H, ts, w), lambda i: (0, i, 0))
    return pl.pallas_call(
        body, grid=(s // ts,),
        in_specs=[head(2 * DN), head(2 * DN), head(DN), rowb(L_COLS), rowb(RQ), rowb(RQ), const((1, RQ)),
                  const((1, RQ)), const((RQ, 2 * H * DN)), const((RQ, 2 * H * DN)), rowb(LANE), rowb(LANE)],
        out_specs=[rowb(L_COLS), const((RQ, 2 * H * DN)), const((RQ, 2 * H * DN)), const((8, RQ))],
        out_shape=[jax.ShapeDtypeStruct((s, L_COLS), BF16), jax.ShapeDtypeStruct((RQ, 2 * H * DN), F32),
                   jax.ShapeDtypeStruct((RQ, 2 * H * DN), F32), jax.ShapeDtypeStruct((8, RQ), F32)],
        scratch_shapes=[pltpu.VMEM((ts, 2 * H * DN), BF16), pltpu.VMEM((ts, 2 * H * DN), BF16)],
        name="mla_prep_bwd", compiler_params=_params(("arbitrary",), VMEM_BIG))(
            dq, dk, dv, proj_l, qn, kvn, q_norm_w, kv_norm_w, w_uq2, w_ukv, cos_t, sin_t)


def _causal_pairs(n, by_key):
    if by_key:
        pairs = [(i, j) for j in range(n) for i in range(j, n)]
    else:
        pairs = [(i, j) for i in range(n) for j in range(i + 1)]
    return (jnp.asarray(np.array([p[0] for p in pairs], np.int32)),
            jnp.asarray(np.array([p[1] for p in pairs], np.int32)))


ATT_HEADS = 2
ATT_ROWS = 64


def _diag_width(r0, t):
    return min(t, -(-(r0 + ATT_ROWS) // LANE) * LANE)


def _diag_mask_rows(r0, width):
    rows = r0 + lax.broadcasted_iota(jnp.int32, (ATT_ROWS, width), 0)
    cols = lax.broadcasted_iota(jnp.int32, (ATT_ROWS, width), 1)
    return cols <= rows


def _diag_mask(t):
    return lax.broadcasted_iota(jnp.int32, (t, t), 1) <= lax.broadcasted_iota(jnp.int32, (t, t), 0)


def _attn_fwd(q, k, v, t):
    s = q.shape[1]
    n = s // t
    scale = float((DN + DR) ** -0.5)
    qi, ki = _causal_pairs(n, by_key=False)

    def body(qi_ref, ki_ref, q_ref, k_ref, v_ref, o_ref, lse_ref, *scratch):
        per_head = [scratch[4 * h:4 * h + 4] for h in range(ATT_HEADS)]
        p = pl.program_id(1)
        i = qi_ref[p]
        j = ki_ref[p]

        @pl.when(j == 0)
        def _():
            for m_sc, acc_sc, _, _ in per_head:
                m_sc[...] = jnp.full_like(m_sc, -jnp.inf)
                acc_sc[...] = jnp.zeros_like(acc_sc)

        def scores(h, diag):
            sc = _dot_nt(q_ref[h], k_ref[h])
            if diag:
                sc = jnp.where(_diag_mask(t), sc, -jnp.inf)
            per_head[h][2][...] = sc

        def softmax(h):
            m_sc, acc_sc, s_sc, p_sc = per_head[h]
            m_prev = m_sc[...]
            m_new = jnp.maximum(m_prev, jnp.max(s_sc[...], axis=-1, keepdims=True) * scale)
            m_sc[...] = m_new
            acc_sc[...] = jnp.exp(m_prev - m_new) * acc_sc[...]
            for r0 in range(0, t, ATT_ROWS):
                rows = slice(r0, r0 + ATT_ROWS)
                p_sc[rows, :] = jnp.exp(s_sc[rows, :] * scale - m_sc[rows, :]).astype(BF16)

        def values(h):
            _, acc_sc, _, p_sc = per_head[h]
            acc_sc[...] += _dot(p_sc[...], v_ref[h])

        def step(diag):
            scores(0, diag)
            for h in range(ATT_HEADS):
                if h + 1 < ATT_HEADS:
                    scores(h + 1, diag)
                softmax(h)
                values(h)

        @pl.when(j < i)
        def _():
            step(False)

        @pl.when(j == i)
        def _():
            step(True)
            for h, (m_sc, acc_sc, _, _) in enumerate(per_head):
                l = acc_sc[:, DN:2 * DN]
                o_ref[:, DN * h:DN * (h + 1)] = acc_sc[:, 0:DN] / l
                lse_ref[h] = m_sc[...] + jnp.log(l[:, 0:1])

    hb = ATT_HEADS
    grid_spec = pltpu.PrefetchScalarGridSpec(
        num_scalar_prefetch=2, grid=(H // hb, int(qi.shape[0])),
        in_specs=[pl.BlockSpec((hb, t, 2 * DN), lambda h, p, qi, ki: (h, qi[p], 0)),
                  pl.BlockSpec((hb, t, 2 * DN), lambda h, p, qi, ki: (h, ki[p], 0)),
                  pl.BlockSpec((hb, t, 2 * DN), lambda h, p, qi, ki: (h, ki[p], 0))],
        out_specs=[pl.BlockSpec((t, hb * DN), lambda h, p, qi, ki: (qi[p], h)),
                   pl.BlockSpec((hb, t, 1), lambda h, p, qi, ki: (h, qi[p], 0))],
        scratch_shapes=[pltpu.VMEM((t, 1), F32), pltpu.VMEM((t, 2 * DN), F32),
                        pltpu.VMEM((t, t), F32), pltpu.VMEM((t, t), BF16)] * hb)
    return pl.pallas_call(
        body, grid_spec=grid_spec,
        out_shape=[jax.ShapeDtypeStruct((s, H * DN), F32), jax.ShapeDtypeStruct((H, s, 1), F32)],
        name="attn_fwd", compiler_params=_params(("parallel", "arbitrary"), VMEM_BIG))(qi, ki, q, k, v)


def _attn_bwd(q, k, v, do, lse, delta, t):
    s = q.shape[1]
    n = s // t
    scale = float((DN + DR) ** -0.5)
    qi, ki = _causal_pairs(n, by_key=True)

    def body(qi_ref, ki_ref, q_ref, k_ref, v_ref, do_ref, lse_ref, dl_ref, dq_ref, dk_ref, dv_ref,
             dk_sc, dv_sc, s_sc, dp_sc, p_sc, ds_sc):
        p = pl.program_id(1)
        i = qi_ref[p]
        j = ki_ref[p]

        @pl.when(p == 0)
        def _():
            dq_ref[...] = jnp.zeros_like(dq_ref)

        @pl.when(i == j)
        def _():
            dk_sc[...] = jnp.zeros_like(dk_sc)
            dv_sc[...] = jnp.zeros_like(dv_sc)

        def step(diag):
            for h in range(ATT_HEADS):
                s_sc[h] = _dot_nt(q_ref[h], k_ref[h])
                dp_sc[h] = _dot_nt(do_ref[:, DN * h:DN * (h + 1)], v_ref[h, :, 0:DN])
            for h in range(ATT_HEADS):
                for r0 in range(0, t, ATT_ROWS):
                    rows = slice(r0, r0 + ATT_ROWS)
                    width = _diag_width(r0, t) if diag else t
                    sc = s_sc[h, rows, 0:width] * scale
                    if diag:
                        sc = jnp.where(_diag_mask_rows(r0, width), sc, -jnp.inf)
                    pr = jnp.exp(sc - lse_ref[h, rows, :])
                    ds = pr * (dp_sc[h, rows, 0:width] - dl_ref[h, rows, :]) * scale
                    p_sc[h, rows, 0:width] = pr.astype(BF16)
                    ds_sc[h, rows, 0:width] = ds.astype(BF16)
                    if width < t:
                        p_sc[h, rows, width:t] = jnp.zeros((ATT_ROWS, t - width), BF16)
                        ds_sc[h, rows, width:t] = jnp.zeros((ATT_ROWS, t - width), BF16)
            q_rows = pl.ds(pl.multiple_of(i * t, t), t)
            for h in range(ATT_HEADS):
                dv_sc[h] += _dot_tn(p_sc[h], do_ref[:, DN * h:DN * (h + 1)])
                dk_sc[h] += _dot_tn(ds_sc[h], q_ref[h])
                dq_ref[h, q_rows, :] += _dot(ds_sc[h], k_ref[h])

        @pl.when(i > j)
        def _():
            step(False)

        @pl.when(i == j)
        def _():
            step(True)

        @pl.when(i == n - 1)
        def _():
            dk_ref[...] = dk_sc[...]
            dv_ref[...] = dv_sc[...]

    hb = ATT_HEADS
    grid_spec = pltpu.PrefetchScalarGridSpec(
        num_scalar_prefetch=2, grid=(H // hb, int(qi.shape[0])),
        in_specs=[pl.BlockSpec((hb, t, 2 * DN), lambda h, p, qi, ki: (h, qi[p], 0)),
                  pl.BlockSpec((hb, t, 2 * DN), lambda h, p, qi, ki: (h, ki[p], 0)),
                  pl.BlockSpec((hb, t, 2 * DN), lambda h, p, qi, ki: (h, ki[p], 0)),
                  pl.BlockSpec((t, hb * DN), lambda h, p, qi, ki: (qi[p], h)),
                  pl.BlockSpec((hb, t, 1), lambda h, p, qi, ki: (h, qi[p], 0)),
                  pl.BlockSpec((hb, t, 1), lambda h, p, qi, ki: (h, qi[p], 0))],
        out_specs=[pl.BlockSpec((hb, s, 2 * DN), lambda h, p, qi, ki: (h, 0, 0)),
                   pl.BlockSpec((hb, t, 2 * DN), lambda h, p, qi, ki: (h, ki[p], 0)),
                   pl.BlockSpec((hb, t, DN), lambda h, p, qi, ki: (h, ki[p], 0))],
        scratch_shapes=[pltpu.VMEM((hb, t, 2 * DN), F32), pltpu.VMEM((hb, t, DN), F32),
                        pltpu.VMEM((hb, t, t), F32), pltpu.VMEM((hb, t, t), F32),
                        pltpu.VMEM((hb, t, t), BF16), pltpu.VMEM((hb, t, t), BF16)])
    return pl.pallas_call(
        body, grid_spec=grid_spec,
        out_shape=[jax.ShapeDtypeStruct((H, s, 2 * DN), F32), jax.ShapeDtypeStruct((H, s, 2 * DN), F32),
                   jax.ShapeDtypeStruct((H, s, DN), F32)],
        name="attn_bwd", compiler_params=_params(("parallel", "arbitrary"), VMEM_BIG))(
            qi, ki, q, k, v, do, lse, delta)


def _middle(za, o, proj_g, x, tgt, gate, fnw, wco, wao, wo, ts):
    s = x.shape[0]
    inv_d = 1.0 / D

    def body(za_ref, o_ref, bg_ref, ga_ref, gb_ref, x_ref, t_ref, gate_ref, fnw_ref, wco_ref, wao_ref, wo_ref,
             dx2_ref, dza_ref, do_ref, dl_ref, dpg_ref, zb_ref, mg_ref, dmo_ref, dya_ref, dyb_ref, vec_ref):
        @pl.when(pl.program_id(0) == 0)
        def _():
            vec_ref[...] = jnp.zeros_like(vec_ref)

        ov = o_ref[...]
        bg = bg_ref[...]
        sb = _sigmoid(bg)
        silu_b = bg * sb
        zb = (ov * silu_b).astype(BF16)
        zb_ref[...] = zb
        ya = _dot(za_ref[...], wco_ref[...])
        yb = _dot(zb, wao_ref[...])
        sa = _sigmoid(ga_ref[...])
        sg = _sigmoid(gb_ref[...])
        mg = (sa * ya + sg * yb).astype(BF16)
        mg_ref[...] = mg
        mo = _dot(mg, wo_ref[...])
        gate_v = gate_ref[...]
        x2 = x_ref[...] + gate_v * mo
        r = lax.rsqrt(_rowmean(x2 * x2) + EPS)
        xh = x2 * r
        fw = fnw_ref[...]
        e = xh * fw - t_ref[...]
        vec_ref[2:3, :] += _colsum(e * e)
        dy = e * inv_d
        vec_ref[0:1, :] += _colsum(dy * xh)
        dxh = dy * fw
        dx2 = r * (dxh - xh * _rowmean(dxh * xh))
        dx2_ref[...] = dx2
        vec_ref[1:2, :] += _colsum(dx2 * mo)
        dmo = (gate_v * dx2).astype(BF16)
        dmo_ref[...] = dmo
        dmg = _dot_nt(dmo, wo_ref[...])
        dya = (sa * dmg).astype(BF16)
        dyb = (sg * dmg).astype(BF16)
        dya_ref[...] = dya
        dyb_ref[...] = dyb
        dpg_ref[:, D:2 * D] = (dmg * ya * (sa * (1.0 - sa))).astype(BF16)
        dpg_ref[:, 2 * D:3 * D] = (dmg * yb * (sg * (1.0 - sg))).astype(BF16)
        dza_ref[...] = _dot_nt(dya, wco_ref[...])
        dzb = _dot_nt(dyb, wao_ref[...])
        dov = dzb * silu_b
        do_ref[...] = dov.astype(BF16)
        dpg_ref[:, 0:D] = (dzb * ov * _dsilu(bg, sb)).astype(BF16)
        dprod = dov * ov
        for h in range(H):
            dl_ref[h] = jnp.sum(dprod[:, DN * h:DN * (h + 1)], axis=-1, keepdims=True)

    col = lambda c: pl.BlockSpec((ts, D), lambda i, c=c: (i, c))
    row = pl.BlockSpec((ts, D), lambda i: (i, 0))
    vec = pl.BlockSpec((1, D), lambda i: (0, 0))
    wsp = pl.BlockSpec((D, D), lambda i: (0, 0))
    bf = jax.ShapeDtypeStruct((s, D), BF16)
    ff = jax.ShapeDtypeStruct((s, D), F32)
    return pl.pallas_call(
        body, grid=(s // ts,),
        in_specs=[row, row, col(0), col(1), col(2), row, row, vec, vec, wsp, wsp, wsp],
        out_specs=[row, row, row, pl.BlockSpec((H, ts, 1), lambda i: (0, i, 0)),
                   pl.BlockSpec((ts, G_COLS), lambda i: (i, 0)), row, row, row, row, row,
                   pl.BlockSpec((8, D), lambda i: (0, 0))],
        out_shape=[ff, ff, bf, jax.ShapeDtypeStruct((H, s, 1), F32), jax.ShapeDtypeStruct((s, G_COLS), BF16),
                   bf, bf, bf, bf, bf, jax.ShapeDtypeStruct((8, D), F32)],
        name="middle", compiler_params=_params(("arbitrary",), VMEM_BIG))(
            za, o, proj_g, proj_g, proj_g, x, tgt, gate, fnw, wco, wao, wo)


def _input_bwd(dpa, dpl, dpg, wa, wl, wg, x, dx2, norm_w, scale, ts, parts):
    s = x.shape[0]

    def body(dpa_ref, dpl_ref, dpg_ref, wa_ref, wl_ref, wg_ref, x_ref, dx2_ref, nw_ref, sc_ref, gx_ref, gv_ref):
        @pl.when(pl.program_id(0) == 0)
        def _():
            gv_ref[...] = jnp.zeros_like(gv_ref)

        dh = (_dot_nt(dpa_ref[...], wa_ref[...]) + _dot_nt(dpl_ref[...], wl_ref[...])
              + _dot_nt(dpg_ref[...], wg_ref[...]))
        xv = x_ref[...]
        r = lax.rsqrt(_rowmean(xv * xv) + EPS)
        xh = xv * r
        nw = nw_ref[...]
        gv_ref[0:1, :] += _colsum(dh)
        gv_ref[1:2, :] += _colsum(dh * (xh * nw))
        dy = dh * (1.0 + sc_ref[...])
        gv_ref[2:3, :] += _colsum(dy * xh)
        dxh = dy * nw
        gx_ref[...] = dx2_ref[...] + r * (dxh - xh * _rowmean(dxh * xh))

    const = lambda shape: pl.BlockSpec(shape, lambda i: (0, 0))
    rowb = lambda w: pl.BlockSpec((ts, w), lambda i: (i, 0))
    side_in, side_out, side_shapes, side_sems = _scatter_operands(parts)
    outs = pl.pallas_call(
        _scatter_alongside(body, 10, 2, len(parts), s // ts - 1), grid=(s // ts,),
        in_specs=[rowb(A_COLS), rowb(L_COLS), rowb(G_COLS), const((D, A_COLS)), const((D, L_COLS)),
                  const((D, G_COLS)), rowb(D), rowb(D), const((1, D)), const((1, D))] + side_in,
        out_specs=[rowb(D), const((8, D))] + side_out,
        out_shape=[jax.ShapeDtypeStruct((s, D), F32), jax.ShapeDtypeStruct((8, D), F32)] + side_shapes,
        scratch_shapes=side_sems,
        name="input_bwd", compiler_params=_params(("arbitrary",), VMEM_BIG))(
            dpa, dpl, dpg, wa, wl, wg, x, dx2, norm_w, scale, *parts)
    return outs[0], outs[1], list(outs[2:])


def _adamw(w, g, m, v, tr, name):
    rows, cols = w.shape
    c1 = 1.0 - ADAM_B1 ** ADAM_STEP
    c2 = 1.0 - ADAM_B2 ** ADAM_STEP

    def body(w_ref, g_ref, m_ref, v_ref, d_ref, nm_ref, nv_ref):
        gv = g_ref[...]
        nm = ADAM_B1 * m_ref[...] + (1.0 - ADAM_B1) * gv
        nv = ADAM_B2 * v_ref[...] + (1.0 - ADAM_B2) * (gv * gv)
        nm_ref[...] = nm
        nv_ref[...] = nv
        d_ref[...] = -ADAM_LR * ((nm / c1) / (jnp.sqrt(nv / c2) + ADAM_EPS) + ADAM_WD * w_ref[...])

    blk = pl.BlockSpec((tr, cols), lambda i: (i, 0))
    shp = jax.ShapeDtypeStruct((rows, cols), F32)
    return pl.pallas_call(
        body, grid=(rows // tr,), in_specs=[blk] * 4, out_specs=[blk] * 3, out_shape=[shp] * 3, name=name,
        compiler_params=_params(("parallel",)))(w, g, m, v)


def _ada_fwd(c_all, w_ada_shard, b_ada_shard):
    def body(c_ref, w_ref, b_ref, o_ref):
        cv = c_ref[...]
        o_ref[...] = jnp.dot(cv * _sigmoid(cv), w_ref[...], preferred_element_type=F32,
                             precision=lax.Precision.HIGHEST) + b_ref[...]

    return pl.pallas_call(
        body, out_shape=jax.ShapeDtypeStruct((N_DEV, w_ada_shard.shape[1]), F32), name="ada_fwd")(
            c_all, w_ada_shard, b_ada_shard)


def _ada_bwd(c_all_t, dmod_shard):
    def body(c_ref, d_ref, o_ref):
        cv = c_ref[...]
        o_ref[...] = jnp.dot(cv * _sigmoid(cv), d_ref[...], preferred_element_type=F32,
                             precision=lax.Precision.HIGHEST)

    return pl.pallas_call(
        body, out_shape=jax.ShapeDtypeStruct((D, dmod_shard.shape[1]), F32), name="ada_bwd")(c_all_t, dmod_shard)


def _sum_slabs(stack, tr, name):
    n, rows, cols = stack.shape

    def body(s_ref, o_ref):
        acc = s_ref[0]
        for k in range(1, n):
            acc = acc + s_ref[k]
        o_ref[...] = acc

    return pl.pallas_call(
        body, grid=(rows // tr,), in_specs=[pl.BlockSpec((n, tr, cols), lambda i: (0, i, 0))],
        out_specs=pl.BlockSpec((tr, cols), lambda i: (i, 0)), out_shape=jax.ShapeDtypeStruct((rows, cols), F32),
        name=name, compiler_params=_params(("parallel",)))(stack)


def _sum_chip_slabs(arrived, part, place, tr, name):
    n, rows, cols = arrived.shape
    per = rows // tr

    def body(place_ref, a_ref, p_ref, o_ref):
        acc = p_ref[0].astype(F32)
        for k in range(n):
            acc = acc + a_ref[k].astype(F32)
        o_ref[...] = acc

    grid_spec = pltpu.PrefetchScalarGridSpec(
        num_scalar_prefetch=1, grid=(per,),
        in_specs=[pl.BlockSpec((n, tr, cols), lambda i, pc: (0, i, 0)),
                  pl.BlockSpec((1, tr, cols), lambda i, pc: (pc[0], i, 0))],
        out_specs=pl.BlockSpec((tr, cols), lambda i, pc: (pc[1] * per + i, 0)))
    return pl.pallas_call(
        body, grid_spec=grid_spec, out_shape=jax.ShapeDtypeStruct((2 * rows, cols), F32), name=name,
        compiler_params=_params(("parallel",)))(place, arrived, part)


def _add_own_half(full, other, core, tr, name):
    n, rows, cols = other.shape
    per = rows // tr

    def body(c_ref, f_ref, o_ref, out_ref):
        out_ref[...] = (f_ref[...] + o_ref[...]).astype(BF16)

    grid_spec = pltpu.PrefetchScalarGridSpec(
        num_scalar_prefetch=1, grid=(n, per),
        in_specs=[pl.BlockSpec((1, tr, cols), lambda k, i, c: (k, c[0] * per + i, 0)),
                  pl.BlockSpec((1, tr, cols), lambda k, i, c: (k, i, 0))],
        out_specs=pl.BlockSpec((1, tr, cols), lambda k, i, c: (k, i, 0)))
    return pl.pallas_call(
        body, grid_spec=grid_spec, out_shape=jax.ShapeDtypeStruct((n, rows, cols), BF16), name=name,
        compiler_params=_params(("parallel", "parallel")))(core, full, other)


def _allgather8(block, src_rows, vmem, name):
    n = block.shape[1]
    m = src_rows
    sliced = block.shape[0] != m

    def body(x_ref, out_ref, send_sems, recv_sems, local_sem):
        x, y, c = _coords()
        me, sibling = (x, y, c), (x, y, 1 - c)
        chips = [(1 - x, y), (x, 1 - y), (1 - x, 1 - y)]
        src = x_ref.at[pl.ds(pl.multiple_of(c * m, 16), m), :] if sliced else x_ref

        def rows(px, py, pc):
            return out_ref.at[pl.ds(pl.multiple_of((4 * px + 2 * py + pc) * m, 8), m), :]

        def copy(k, blk, to, source=None):
            return pltpu.make_async_remote_copy(
                src_ref=rows(*blk) if source is None else source, dst_ref=rows(*blk),
                send_sem=send_sems.at[k], recv_sem=recv_sems.at[k], device_id=to, device_id_type=MESH)

        mine = pltpu.make_async_copy(src, rows(*me), local_sem)
        mine.start()
        first = [copy(0, me, sibling, source=src)]
        first += [copy(1 + j, me, (*chip, c), source=src) for j, chip in enumerate(chips)]
        for cp in first:
            cp.start()
        passed = [copy(4 + j, (*chip, c), sibling) for j, chip in enumerate(chips)]
        for j, chip in enumerate(chips):
            copy(1 + j, (*chip, c), me).wait_recv()
            passed[j].start()
        copy(0, sibling, me).wait_recv()
        for j, chip in enumerate(chips):
            copy(4 + j, (*chip, 1 - c), me).wait_recv()
        for cp in first + passed:
            cp.wait_send()
        mine.wait()

    space = pltpu.VMEM if vmem else pl.ANY
    return pl.pallas_call(
        body, out_shape=jax.ShapeDtypeStruct((N_DEV * m, n), block.dtype),
        in_specs=[pl.BlockSpec(memory_space=space)], out_specs=pl.BlockSpec(memory_space=space),
        scratch_shapes=[pltpu.SemaphoreType.DMA((7,)), pltpu.SemaphoreType.DMA((7,)), pltpu.SemaphoreType.DMA],
        name=name)(block)


def _gather_plan(x_refs, out_refs, send_sems, recv_sems, local_sems):
    n = len(x_refs)
    halves = [r.shape[0] // 2 for r in x_refs]
    x, y, c = _coords()
    me, sibling = (x, y, c), (x, y, 1 - c)
    chips = [(1 - x, y), (x, 1 - y), (1 - x, 1 - y)]

    def src(a):
        return x_refs[a].at[pl.ds(pl.multiple_of(c * halves[a], 16), halves[a]), :]

    def blk(a, px, py, pc):
        return out_refs[a].at[4 * px + 2 * py + pc]

    def copy(a, k, who, to, source=None):
        return pltpu.make_async_remote_copy(
            src_ref=blk(a, *who) if source is None else source, dst_ref=blk(a, *who),
            send_sem=send_sems.at[7 * a + k], recv_sem=recv_sems.at[7 * a + k], device_id=to, device_id_type=MESH)

    def mine(a):
        return pltpu.make_async_copy(src(a), blk(a, *me), local_sems.at[a])

    def first(a):
        return ([copy(a, 0, me, sibling, source=src(a))]
                + [copy(a, 1 + j, me, (*chip, c), source=src(a)) for j, chip in enumerate(chips)])

    def begin():
        for a in range(n):
            mine(a).start()
        for a in range(n):
            for cp in first(a):
                cp.start()

    def finish():
        onward = []
        for j, chip in enumerate(chips):
            for a in range(n):
                copy(a, 1 + j, (*chip, c), me).wait_recv()
                onward.append(copy(a, 4 + j, (*chip, c), sibling))
                onward[-1].start()
        for a in range(n):
            copy(a, 0, sibling, me).wait_recv()
        for j, chip in enumerate(chips):
            for a in range(n):
                copy(a, 4 + j, (*chip, 1 - c), me).wait_recv()
        for a in range(n):
            for cp in first(a):
                cp.wait_send()
        for cp in onward:
            cp.wait_send()
        for a in range(n):
            mine(a).wait()

    return begin, finish


def _gather_operands(shards):
    n = len(shards)
    shapes = [jax.ShapeDtypeStruct((N_DEV, a.shape[0] // 2, a.shape[1]), a.dtype) for a in shards]
    sems = [pltpu.SemaphoreType.DMA((7 * n,)), pltpu.SemaphoreType.DMA((7 * n,)), pltpu.SemaphoreType.DMA((n,))]
    return shapes, sems


def _as_chip_slabs(gathered, shards):
    return [o.reshape(N_CHIP, a.shape[0], a.shape[1]) for o, a in zip(gathered, shards)]


def _gather_weights(shards):
    n = len(shards)

    def body(*refs):
        begin, finish = _gather_plan(refs[:n], refs[n:2 * n], *refs[2 * n:])
        begin()
        finish()

    shapes, sems = _gather_operands(shards)
    outs = pl.pallas_call(
        body, out_shape=shapes, in_specs=[HBM_REF] * n, out_specs=[HBM_REF] * n, scratch_shapes=sems,
        name="gather_weights")(*shards)
    return _as_chip_slabs(outs, shards)


def _gather_alongside(body, n_in, n_out, n_shards, last_step):
    def wrapped(*refs):
        ins, shards = refs[:n_in], refs[n_in:n_in + n_shards]
        rest = refs[n_in + n_shards:]
        outs, gathered = rest[:n_out], rest[n_out:n_out + n_shards]
        scratch, sems = rest[n_out + n_shards:-3], rest[-3:]

        @pl.when(pl.program_id(0) == 0)
        def _():
            _gather_plan(shards, gathered, *sems)[0]()

        body(*ins, *outs, *scratch)

        @pl.when(pl.program_id(0) == last_step)
        def _():
            _gather_plan(shards, gathered, *sems)[1]()

    return wrapped


def _swap_halves_with_sibling(fulls, name):
    n = len(fulls)
    halves = [a.shape[1] // 2 for a in fulls]

    def body(*refs):
        f_refs, got_refs = refs[:n], refs[n:2 * n]
        send_sems, recv_sems = refs[2 * n:]
        x, y, c = _coords()
        copies = []
        for a in range(n):
            src = f_refs[a].at[:, pl.ds(pl.multiple_of((1 - c) * halves[a], 8), halves[a]), :]
            copies.append(pltpu.make_async_remote_copy(
                src_ref=src, dst_ref=got_refs[a], send_sem=send_sems.at[a], recv_sem=recv_sems.at[a],
                device_id=(x, y, 1 - c), device_id_type=MESH))
        for cp in copies:
            cp.start()
        for cp in copies:
            cp.wait()

    return pl.pallas_call(
        body, out_shape=[jax.ShapeDtypeStruct((a.shape[0], h, a.shape[2]), a.dtype) for a, h in zip(fulls, halves)],
        in_specs=[HBM_REF] * n, out_specs=[HBM_REF] * n,
        scratch_shapes=[pltpu.SemaphoreType.DMA((n,)), pltpu.SemaphoreType.DMA((n,))],
        name=name)(*fulls)


def _join_halves_with_sibling(wholes):
    n = len(wholes)

    def body(*refs):
        out_refs = refs[n:2 * n]
        send_sems, recv_sems = refs[2 * n:]
        x, y, c = _coords()

        def push(a, core):
            rows = wholes[a].shape[0] // 2
            half = out_refs[a].at[pl.ds(pl.multiple_of(core * rows, 8), rows), :]
            return pltpu.make_async_remote_copy(
                src_ref=half, dst_ref=half, send_sem=send_sems.at[a], recv_sem=recv_sems.at[a],
                device_id=(x, y, 1 - c), device_id_type=MESH)

        for a in range(n):
            push(a, c).start()
        for a in range(n):
            push(a, 1 - c).wait_recv()
        for a in range(n):
            push(a, c).wait_send()

    return pl.pallas_call(
        body, out_shape=[jax.ShapeDtypeStruct(a.shape, a.dtype) for a in wholes],
        in_specs=[HBM_REF] * n, out_specs=[HBM_REF] * n, input_output_aliases={a: a for a in range(n)},
        scratch_shapes=[pltpu.SemaphoreType.DMA((n,)), pltpu.SemaphoreType.DMA((n,))],
        name="rs_pair_join")(*wholes)


def _cols_to_slabs(g):
    rows, cols = g.shape
    return g.reshape(rows, N_CHIP, cols // N_CHIP).transpose(1, 0, 2)


def _slabs_to_cols(w):
    n, rows, cols = w.shape
    return w.transpose(1, 0, 2).reshape(rows, n * cols)


def _uq_to_padded(w_uq):
    per = w_uq.reshape(RQ, H, DN + DR)
    nope = per[:, :, :DN].reshape(RQ, H * DN)
    rope = jnp.pad(per[:, :, DN:], ((0, 0), (0, 0), (0, LANE - DR))).reshape(RQ, H * LANE)
    return jnp.concatenate([nope, rope], axis=1)


def _uq_from_padded(g):
    nope = g[:, :H * DN].reshape(RQ, H, DN)
    rope = g[:, H * DN:].reshape(RQ, H, LANE)[:, :, :DR]
    return jnp.concatenate([nope, rope], axis=2).reshape(RQ, H * (DN + DR))


def _rope_tables(positions):
    inv_freq = ROPE_THETA ** (-jnp.arange(0, DR, 2, dtype=F32) / DR)
    ang = positions.astype(F32)[:, None] * inv_freq
    cos, sin = jnp.cos(ang), jnp.sin(ang)
    return jnp.tile(cos, (1, 4)), jnp.tile(jnp.concatenate([-sin, sin], axis=1), (1, 2))


def _pair_sums(fulls, core, tag):
    from_sibling = _swap_halves_with_sibling(fulls, f"rs_pair_swap_{tag}")
    return [_add_own_half(f, o, core, min(256, o.shape[1]), f"add_own_half_{tag}{n}")
            for n, (f, o) in enumerate(zip(fulls, from_sibling))]


def _local_step(x, tgt, cos_t, sin_t, mod, weights, small, tiles, place):
    ts, ts_in, tm_nn, tm_tn, t_attn, chunk = tiles
    wa, wl, wg, later_shards, conv_w = weights
    norm_w, conv_b, ln_w, ln_b, q_norm_w, kv_norm_w, fnw = small
    shift, scale, gate = mod[:, 0:D], mod[:, D:2 * D], mod[:, 2 * D:3 * D]

    h = _adaln_norm(x, norm_w, shift, scale, ts)
    proj_a = _mm_nn(h, wa, tm_nn, D, "proj_a")
    u0, u1, za, (g_uq, g_ukv, g_co, g_ao, g_o) = _conv_fwd(proj_a, conv_w, conv_b, ln_w, ln_b, ts, chunk, later_shards)
    w_uq2, w_ukv = _uq_to_padded(_slabs_to_cols(g_uq)), _slabs_to_cols(g_ukv)
    wco, wao, wo = g_co.reshape(D, D), g_ao.reshape(D, D), g_o.reshape(D, D)
    proj_l = _mm_nn(h, wl, tm_nn, L_COLS, "proj_l")
    proj_g = _mm_nn(h, wg, tm_nn, D, "proj_g")
    qn, kvn, q, k, v = _mla_prep(proj_l, q_norm_w, kv_norm_w, w_uq2, w_ukv, cos_t, sin_t, ts)
    o, lse = _attn_fwd(q, k, v, t_attn)
    (dx2, dza, do, delta, dpg, zb, mg, dmo, dya, dyb, vec_mid) = _middle(
        za, o, proj_g, x, tgt, gate, fnw, wco, wao, wo, ts)
    g_wo = _mm_tn(mg, dmo, tm_tn, D, "grad_w_out")
    g_wco = _mm_tn(za, dya, tm_tn, D, "grad_w_conv_out")
    g_wao = _mm_tn(zb, dyb, tm_tn, D, "grad_w_attn_out")
    dq, dk, dv = _attn_bwd(q, k, v, do, lse, delta, t_attn)
    dpl, g_wuq2, g_wukv, vec_mla = _mla_prep_bwd(
        dq, dk, dv, proj_l, qn, kvn, q_norm_w, kv_norm_w, w_uq2, w_ukv, cos_t, sin_t, ts)

    core = place[1:2]
    nr = D // N_CHIP
    early = [_cols_to_slabs(_uq_from_padded(g_wuq2)), _cols_to_slabs(g_wukv), g_wco.reshape(N_CHIP, nr, D),
             g_wao.reshape(N_CHIP, nr, D), g_wo.reshape(N_CHIP, nr, D)]
    early_sums = _pair_sums(early, core, "a")
    dpa, g_conv_w, vec_conv, early_got = _conv_bwd(dza, proj_a, u0, u1, conv_w, ln_w, ln_b, ts, chunk, early_sums)

    g_wa = _mm_tn(h, dpa, tm_tn, D, "grad_w_in_a")
    g_wl = _mm_tn(h, dpl, tm_tn, L_COLS, "grad_w_in_l")
    g_wg = _mm_tn(h, dpg, tm_tn, D, "grad_w_in_g")
    g_w_in = jnp.concatenate([g_wa, g_wl[:, 0:L_COLS_RAW], g_wg], axis=1)
    late_sums = _pair_sums([_cols_to_slabs(g_w_in)], core, "b")
    grad_x, vec_in, late_got = _input_bwd(dpa, dpl, dpg, wa, wl, wg, x, dx2, norm_w, scale, ts_in, late_sums)

    wholes = [_sum_chip_slabs(a, p, place, min(128, a.shape[1]), f"sum_chip_slabs_{n}")
              for n, (a, p) in enumerate(zip(late_got + early_got, late_sums + early_sums))]
    shards = _join_halves_with_sibling(wholes)

    dmod = jnp.concatenate([vec_in[0:1], vec_in[1:2], vec_mid[1:2]], axis=1)
    sums = dict(dmod=dmod, norm_w=vec_in[2:3], conv_b=vec_conv[2:3], ln_w=vec_conv[0:1], ln_b=vec_conv[1:2],
                q_norm_w=vec_mla[0:1], kv_norm_w=vec_mla[1:2], final_norm_w=vec_mid[0:1], loss=vec_mid[2:3],
                conv_w=g_conv_w)
    return grad_x, shards, sums


SMALL_ORDER = (("dmod", 3 * D), ("norm_w", D), ("conv_b", D), ("ln_w", D), ("ln_b", D), ("q_norm_w", RQ),
               ("kv_norm_w", RQ), ("final_norm_w", D), ("loss", D), ("conv_w", HALO * D))
SMALL_ROWS = 336


def kernel(x, c, positions, w_ada, b_ada, norm_w, w_in, conv_w, conv_b, conv_ln_w, conv_ln_b, w_conv_out, q_norm_w, w_uq, kv_norm_w, w_ukv, w_attn_out, w_out, final_norm_w, loss_target, m_w_ada, m_b_ada, m_norm_w, m_w_in, m_conv_w, m_conv_b, m_conv_ln_w, m_conv_ln_b, m_w_conv_out, m_q_norm_w, m_w_uq, m_kv_norm_w, m_w_ukv, m_w_attn_out, m_w_out, m_final_norm_w, v_w_ada, v_b_ada, v_norm_w, v_w_in, v_conv_w, v_conv_b, v_conv_ln_w, v_conv_ln_b, v_w_conv_out, v_q_norm_w, v_w_uq, v_kv_norm_w, v_w_ukv, v_w_attn_out, v_w_out, v_final_norm_w):
    ix, iy, ic = _coords()
    chip = 2 * ix + iy
    dev = 4 * ix + 2 * iy + ic
    s = x.shape[1]
    tiles = (256, 512, 1024, 2048, 512, 32)

    conv_w_pad = jnp.pad(conv_w[0], ((0, HALO - KC), (0, 0)))
    small_in = jnp.concatenate([c.reshape(8, LANE), conv_w_pad.reshape(64, LANE)], axis=0)
    small_all = _allgather8(small_in, 72, True, "gather_c_conv").reshape(N_DEV, 72, LANE)
    c_all = small_all[:, 0:8].reshape(N_DEV, D)
    conv_full = jnp.concatenate(
        [small_all[2 * k, 8:72].reshape(HALO, D // N_CHIP) for k in range(N_CHIP)], axis=1)

    (g_in,) = _gather_weights([w_in[0].astype(BF16)])
    w_in_f = _slabs_to_cols(g_in)
    wa = w_in_f[:, 0:A_COLS]
    wl = jnp.pad(w_in_f[:, A_COLS:A_COLS + L_COLS_RAW], ((0, 0), (0, L_COLS - L_COLS_RAW)))
    wg = w_in_f[:, A_COLS + L_COLS_RAW:]
    later_shards = [w[0].astype(BF16) for w in (w_uq, w_ukv, w_conv_out, w_attn_out, w_out)]
    weights = (wa, wl, wg, later_shards, conv_full)

    ada_cols = w_ada.shape[2]
    b_shard = lax.dynamic_slice(b_ada, (0, chip * ada_cols), (1, ada_cols))
    mod_part = _ada_fwd(c_all, w_ada[0], b_shard)
    mod_all = _allgather8(mod_part, N_DEV, True, "gather_mod").reshape(N_DEV, N_DEV, ada_cols)
    mod = jnp.concatenate(
        [lax.dynamic_slice(mod_all[2 * k], (dev, 0), (1, ada_cols)) for k in range(N_CHIP)], axis=1)

    cos_t, sin_t = _rope_tables(positions[0])
    small = (norm_w, conv_b, conv_ln_w, conv_ln_b, q_norm_w, kv_norm_w, final_norm_w.reshape(1, D))
    place = jnp.stack([chip, ic]).astype(jnp.int32)
    grad_x, shards, sums = _local_step(x[0], loss_target[0], cos_t, sin_t, mod, weights, small, tiles, place)
    g_w_in_s, g_w_uq_s, g_w_ukv_s, g_wco_s, g_wao_s, g_wo_s = shards

    small_flat = jnp.concatenate([sums[name].reshape(-1) for name, _ in SMALL_ORDER])
    small_flat = jnp.pad(small_flat, (0, SMALL_ROWS * LANE - small_flat.shape[0]))
    small_g = _allgather8(small_flat.reshape(SMALL_ROWS, LANE), SMALL_ROWS, True, "gather_small_grads")
    small_g = small_g.reshape(N_DEV, SMALL_ROWS, LANE)
    small_sum = _sum_slabs(small_g, SMALL_ROWS, "sum_small_grads").reshape(-1)
    tot, pos = {}, 0
    for name, size in SMALL_ORDER:
        tot[name] = small_sum[pos:pos + size]
        pos += size
    loss = (0.5 / D) * jnp.sum(tot["loss"])
    dmod_all = small_g.reshape(N_DEV, -1)[:, 0:3 * D]
    g_b_ada = tot["dmod"].reshape(1, 3 * D)
    dmod_shard = lax.dynamic_slice(dmod_all, (0, chip * ada_cols), (N_DEV, ada_cols))
    g_w_ada = _ada_bwd(c_all.T, dmod_shard).reshape(1, D, ada_cols)
    g_conv_w = lax.dynamic_slice(tot["conv_w"].reshape(HALO, D), (0, chip * (D // N_CHIP)), (KC, D // N_CHIP))
    g_conv_w = g_conv_w.reshape(1, KC, D // N_CHIP)

    def big(w, g, m, v, tr, name):
        d, nm, nv = _adamw(w[0], g, m[0], v[0], tr, name)
        return g[None], d[None], nm[None], nv[None]

    vec_names = ("b_ada", "norm_w", "conv_b", "conv_ln_w", "conv_ln_b", "q_norm_w", "kv_norm_w", "final_norm_w")
    vec_w = (b_ada, norm_w, conv_b, conv_ln_w, conv_ln_b, q_norm_w, kv_norm_w, final_norm_w)
    vec_m = (m_b_ada, m_norm_w, m_conv_b, m_conv_ln_w, m_conv_ln_b, m_q_norm_w, m_kv_norm_w, m_final_norm_w)
    vec_v = (v_b_ada, v_norm_w, v_conv_b, v_conv_ln_w, v_conv_ln_b, v_q_norm_w, v_kv_norm_w, v_final_norm_w)
    vec_g = (g_b_ada, tot["norm_w"], tot["conv_b"], tot["ln_w"], tot["ln_b"], tot["q_norm_w"], tot["kv_norm_w"],
             tot["final_norm_w"])
    vec_g = tuple(g.reshape(w.shape) for g, w in zip(vec_g, vec_w))
    cat = lambda arrs: jnp.concatenate([a.reshape(-1) for a in arrs]).reshape(-1, LANE)
    vd, vnm, vnv = _adamw(cat(vec_w), cat(vec_g), cat(vec_m), cat(vec_v), cat(vec_w).shape[0], "adamw_vectors")

    def split(packed):
        flat, out, pos = packed.reshape(-1), [], 0
        for w in vec_w:
            out.append(flat[pos:pos + w.size].reshape(w.shape))
            pos += w.size
        return out

    res = {}
    for name, g, d, nm, nv in zip(vec_names, vec_g, split(vd), split(vnm), split(vnv)):
        res[name] = (g, d, nm, nv)
    res["w_ada"] = big(w_ada, g_w_ada[0], m_w_ada, v_w_ada, 256, "adamw_w_ada")
    res["w_in"] = big(w_in, g_w_in_s, m_w_in, v_w_in, 256, "adamw_w_in")
    res["conv_w"] = big(conv_w, g_conv_w[0], m_conv_w, v_conv_w, KC, "adamw_conv_w")
    res["w_conv_out"] = big(w_conv_out, g_wco_s, m_w_conv_out, v_w_conv_out, 256, "adamw_w_conv_out")
    res["w_uq"] = big(w_uq, g_w_uq_s, m_w_uq, v_w_uq, 256, "adamw_w_uq")
    res["w_ukv"] = big(w_ukv, g_w_ukv_s, m_w_ukv, v_w_ukv, 256, "adamw_w_ukv")
    res["w_attn_out"] = big(w_attn_out, g_wao_s, m_w_attn_out, v_w_attn_out, 256, "adamw_w_attn_out")
    res["w_out"] = big(w_out, g_wo_s, m_w_out, v_w_out, 256, "adamw_w_out")

    order = ("w_ada", "b_ada", "norm_w", "w_in", "conv_w", "conv_b", "conv_ln_w", "conv_ln_b", "w_conv_out",
             "q_norm_w", "w_uq", "kv_norm_w", "w_ukv", "w_attn_out", "w_out", "final_norm_w")
    outs = [loss, grad_x[None]]
    for slot in range(4):
        outs += [res[name][slot] for name in order]
    return tuple(outs)
```

```python
import functools

import numpy as np
import jax
import jax.numpy as jnp
from jax import lax
from jax.experimental import pallas as pl
from jax.experimental.pallas import tpu as pltpu

F32 = jnp.float32
BF16 = jnp.bfloat16
MESH = pl.DeviceIdType.MESH

D = 1024
H = 8
DN = 128
DR = 64
RQ = 256
KC = 31
HALO = 32
EPS = 1e-6
ROPE_THETA = 10000.0
N_CHIP = 4
N_DEV = 8
LANE = 128
VMEM_BIG = 56 * 1024 * 1024

ADAM_LR = 0.001
ADAM_B1 = 0.9
ADAM_B2 = 0.999
ADAM_EPS = 1e-08
ADAM_WD = 0.01
ADAM_STEP = 10

A_COLS = 3 * D
L_COLS_RAW = RQ + RQ + DR
L_COLS = 640
G_COLS = 3 * D
IN_COLS = A_COLS + L_COLS_RAW + G_COLS


def _params(sem=None, vmem=None):
    kw = {}
    if sem is not None:
        kw["dimension_semantics"] = sem
    if vmem is not None:
        kw["vmem_limit_bytes"] = vmem
    return pltpu.CompilerParams(**kw)


def _dot(a, b):
    return jnp.dot(a, b, preferred_element_type=F32)


def _dot_nt(a, b):
    return lax.dot_general(a, b, (((1,), (1,)), ((), ())), preferred_element_type=F32)


def _dot_tn(a, b):
    return lax.dot_general(a, b, (((0,), (0,)), ((), ())), preferred_element_type=F32)


def _colsum(v):
    return jnp.sum(v, axis=0, keepdims=True)


def _rowmean(v):
    return jnp.mean(v, axis=-1, keepdims=True)


def _sigmoid(v):
    return jax.nn.sigmoid(v)


def _dsilu(v, s):
    return s * (1.0 + v * (1.0 - s))


def _swap_halves(v, first_half):
    return jnp.where(first_half, pltpu.roll(v, 96, 1), pltpu.roll(v, 32, 1))


def _first_half_mask(rows):
    lane = lax.broadcasted_iota(jnp.int32, (rows, LANE), 1)
    return (lane % 64) < 32


def _adaln_norm(x, norm_w, shift, scale, ts):
    s = x.shape[0]

    def body(x_ref, nw_ref, sh_ref, sc_ref, h_ref):
        xv = x_ref[...]
        r = lax.rsqrt(_rowmean(xv * xv) + EPS)
        y = xv * r * nw_ref[...]
        h_ref[...] = (y * (1.0 + sc_ref[...]) + sh_ref[...]).astype(BF16)

    row = pl.BlockSpec((ts, D), lambda i: (i, 0))
    vec = pl.BlockSpec((1, D), lambda i: (0, 0))
    return pl.pallas_call(
        body, grid=(s // ts,), in_specs=[row, vec, vec, vec], out_specs=row,
        out_shape=jax.ShapeDtypeStruct((s, D), BF16), name="adaln_norm",
        compiler_params=_params(("parallel",)))(x, norm_w, shift, scale)


def _mm_nn(a, b, tm, tn, name):
    m, k = a.shape
    n = b.shape[1]

    def body(a_ref, b_ref, o_ref):
        o_ref[...] = _dot(a_ref[...], b_ref[...])

    return pl.pallas_call(
        body, grid=(n // tn, m // tm),
        in_specs=[pl.BlockSpec((tm, k), lambda j, i: (i, 0)), pl.BlockSpec((k, tn), lambda j, i: (0, j))],
        out_specs=pl.BlockSpec((tm, tn), lambda j, i: (i, j)),
        out_shape=jax.ShapeDtypeStruct((m, n), F32), name=name,
        compiler_params=_params(("parallel", "parallel"), VMEM_BIG))(a, b)


def _mm_tn(a, b, tm, tn, name):
    m, k = a.shape
    n = b.shape[1]

    def body(a_ref, b_ref, o_ref):
        @pl.when(pl.program_id(1) == 0)
        def _():
            o_ref[...] = jnp.zeros_like(o_ref)
        o_ref[...] += _dot_tn(a_ref[...], b_ref[...])

    return pl.pallas_call(
        body, grid=(n // tn, m // tm),
        in_specs=[pl.BlockSpec((tm, k), lambda j, i: (i, 0)), pl.BlockSpec((tm, tn), lambda j, i: (i, j))],
        out_specs=pl.BlockSpec((k, tn), lambda j, i: (0, j)),
        out_shape=jax.ShapeDtypeStruct((k, n), F32), name=name,
        compiler_params=_params(("parallel", "arbitrary"), VMEM_BIG))(a, b)


def _coords():
    return lax.axis_index("x"), lax.axis_index("y"), lax.axis_index("c")


HBM_REF = pl.BlockSpec(memory_space=pl.ANY)


def _chip_scatter_copies(p_refs, got_refs, send_sems, recv_sems):
    x, y, c = _coords()
    copies = []
    for a in range(len(p_refs)):
        for j, (px, py) in enumerate([(1 - x, y), (x, 1 - y), (1 - x, 1 - y)]):
            copies.append(pltpu.make_async_remote_copy(
                src_ref=p_refs[a].at[2 * px + py], dst_ref=got_refs[a].at[j], send_sem=send_sems.at[3 * a + j],
                recv_sem=recv_sems.at[3 * a + j], device_id=(px, py, c), device_id_type=MESH))
    return copies


def _scatter_alongside(body, n_in, n_out, n_parts, last_step):
    def wrapped(*refs):
        ins, parts = refs[:n_in], refs[n_in:n_in + n_parts]
        rest = refs[n_in + n_parts:]
        outs, got = rest[:n_out], rest[n_out:n_out + n_parts]
        scratch, (send_sems, recv_sems) = rest[n_out + n_parts:-2], rest[-2:]

        @pl.when(pl.program_id(0) == 0)
        def _():
            for cp in _chip_scatter_copies(parts, got, send_sems, recv_sems):
                cp.start()

        body(*ins, *outs, *scratch)

        @pl.when(pl.program_id(0) == last_step)
        def _():
            for cp in _chip_scatter_copies(parts, got, send_sems, recv_sems):
                cp.wait()

    return wrapped


def _scatter_operands(parts):
    n = len(parts)
    shapes = [jax.ShapeDtypeStruct((3,) + a.shape[1:], a.dtype) for a in parts]
    sems = [pltpu.SemaphoreType.DMA((3 * n,)), pltpu.SemaphoreType.DMA((3 * n,))]
    return [HBM_REF] * n, [HBM_REF] * n, shapes, sems


def _shifted_copies(win_ref, sh_ref, rows):
    for p in range(1, 8):
        sh_ref[p - 1, 0:rows, :] = win_ref[pl.ds(p, rows), :]


def _tap_rows(win_ref, sh_ref, start, rows):
    p = start % 8
    if p == 0:
        return win_ref[pl.ds(start, rows), :]
    return sh_ref[p - 1, pl.ds(start - p, rows), :]


def _conv_taps(win_ref, sh_ref, w_ref, rows, chunk, offset_of_tap):
    pieces = []
    for c0 in range(0, rows, chunk):
        acc = None
        for j in range(KC):
            term = w_ref[j:j + 1, :] * _tap_rows(win_ref, sh_ref, c0 + offset_of_tap(j), chunk)
            acc = term if acc is None else acc + term
        pieces.append(acc)
    return pieces


def _conv_fwd(proj_a, conv_w, conv_b, ln_w, ln_b, ts, chunk, shards):
    s = proj_a.shape[0]

    def body(av_ref, al_ref, ag_ref, w_ref, b_ref, lw_ref, lb_ref, u0_ref, u1_ref, za_ref, win_ref, sh_ref):
        @pl.when(pl.program_id(0) == 0)
        def _():
            win_ref[0:HALO, :] = jnp.zeros((HALO, D), F32)

        u0 = av_ref[...] * _sigmoid(al_ref[...])
        u0_ref[...] = u0
        win_ref[HALO:HALO + ts, :] = u0
        _shifted_copies(win_ref, sh_ref, ts + HALO - 8)
        pieces = _conv_taps(win_ref, sh_ref, w_ref, ts, chunk, lambda j: HALO - (KC - 1) + j)
        for n, acc in enumerate(pieces):
            u1_ref[n * chunk:(n + 1) * chunk, :] = acc + b_ref[...]
        win_ref[0:HALO, :] = win_ref[ts:ts + HALO, :]

        u1 = u1_ref[...]
        xc = u1 - _rowmean(u1)
        rstd = lax.rsqrt(_rowmean(xc * xc) + EPS)
        u2 = xc * rstd * lw_ref[...] + lb_ref[...]
        u3 = u2 * _sigmoid(u2)
        ag = ag_ref[...]
        za_ref[...] = (u3 * (ag * _sigmoid(ag))).astype(BF16)

    col = lambda c: pl.BlockSpec((ts, D), lambda i, c=c: (i, c))
    row = pl.BlockSpec((ts, D), lambda i: (i, 0))
    vec = pl.BlockSpec((1, D), lambda i: (0, 0))
    n = len(shards)
    gathered_shapes, sems = _gather_operands(shards)
    outs = pl.pallas_call(
        _gather_alongside(body, 7, 3, n, s // ts - 1), grid=(s // ts,),
        in_specs=[col(0), col(1), col(2), pl.BlockSpec((HALO, D), lambda i: (0, 0)), vec, vec, vec] + [HBM_REF] * n,
        out_specs=[row, row, row] + [HBM_REF] * n,
        out_shape=[jax.ShapeDtypeStruct((s, D), F32), jax.ShapeDtypeStruct((s, D), F32),
                   jax.ShapeDtypeStruct((s, D), BF16)] + gathered_shapes,
        scratch_shapes=[pltpu.VMEM((ts + HALO, D), F32), pltpu.VMEM((7, ts + HALO, D), F32)] + sems,
        name="conv_fwd", compiler_params=_params(("arbitrary",), VMEM_BIG))(
            proj_a, proj_a, proj_a, conv_w, conv_b, ln_w, ln_b, *shards)
    return outs[0], outs[1], outs[2], _as_chip_slabs(outs[3:], shards)


def _conv_bwd(dza, proj_a, u0, u1, conv_w, ln_w, ln_b, ts, chunk, parts):
    s = dza.shape[0]
    nt = s // ts
    per = ts // HALO

    def body(dza_ref, av_ref, al_ref, ag_ref, u0_ref, u0p_ref, u1_ref, w_ref, lw_ref, lb_ref,
             dpa_ref, gw_ref, gv_ref, dwin_ref, uwin_ref, du0_ref, gwp_ref, dsh_ref, ush_ref):
        step = pl.program_id(0)
        tile = nt - 1 - step

        @pl.when(step == 0)
        def _():
            dwin_ref[ts:ts + HALO, :] = jnp.zeros((HALO, D), F32)
            gwp_ref[...] = jnp.zeros_like(gwp_ref)
            gv_ref[...] = jnp.zeros_like(gv_ref)

        ag = ag_ref[...]
        sg = _sigmoid(ag)
        u1 = u1_ref[...]
        xc = u1 - _rowmean(u1)
        rstd = lax.rsqrt(_rowmean(xc * xc) + EPS)
        xh = xc * rstd
        u2 = xh * lw_ref[...] + lb_ref[...]
        s2 = _sigmoid(u2)
        dz = dza_ref[...]
        du3 = dz * (ag * sg)
        dpa_ref[:, 2 * D:3 * D] = (dz * (u2 * s2) * _dsilu(ag, sg)).astype(BF16)
        du2 = du3 * _dsilu(u2, s2)
        gv_ref[0:1, :] += _colsum(du2 * xh)
        gv_ref[1:2, :] += _colsum(du2)
        dxh = du2 * lw_ref[...]
        du1 = rstd * (dxh - _rowmean(dxh) - xh * _rowmean(dxh * xh))
        gv_ref[2:3, :] += _colsum(du1)
        dwin_ref[0:ts, :] = du1

        uwin_ref[0:HALO, :] = jnp.where(tile == 0, 0.0, u0p_ref[...])
        uwin_ref[HALO:HALO + ts, :] = u0_ref[...]

        _shifted_copies(dwin_ref, dsh_ref, ts + HALO - 8)
        _shifted_copies(uwin_ref, ush_ref, ts + HALO - 8)
        pieces = _conv_taps(dwin_ref, dsh_ref, w_ref, ts, chunk, lambda j: (KC - 1) - j)
        for n, acc in enumerate(pieces):
            du0_ref[n * chunk:(n + 1) * chunk, :] = acc
        for c0 in range(0, ts, chunk):
            dchunk = dwin_ref[c0:c0 + chunk, :]
            for j in range(KC):
                prod = dchunk * _tap_rows(uwin_ref, ush_ref, c0 + HALO - (KC - 1) + j, chunk)
                gwp_ref[8 * j:8 * j + 8, :] += jnp.sum(prod.reshape(chunk // 8, 8, D), axis=0)
        dwin_ref[ts:ts + HALO, :] = dwin_ref[0:HALO, :]

        du0 = du0_ref[...]
        al = al_ref[...]
        sl = _sigmoid(al)
        dpa_ref[:, 0:D] = (du0 * sl).astype(BF16)
        dpa_ref[:, D:2 * D] = (du0 * av_ref[...] * sl * (1.0 - sl)).astype(BF16)

        @pl.when(step == nt - 1)
        def _():
            for j in range(KC):
                gw_ref[j:j + 1, :] = _colsum(gwp_ref[8 * j:8 * j + 8, :])
            gw_ref[KC:HALO, :] = jnp.zeros((HALO - KC, D), F32)

    rev = lambda i: nt - 1 - i
    col = lambda c: pl.BlockSpec((ts, D), lambda i, c=c: (rev(i), c))
    row = pl.BlockSpec((ts, D), lambda i: (rev(i), 0))
    vec = pl.BlockSpec((1, D), lambda i: (0, 0))
    halo = pl.BlockSpec((HALO, D), lambda i: (jnp.maximum(rev(i) * per - 1, 0), 0))
    side_in, side_out, side_shapes, side_sems = _scatter_operands(parts)
    outs = pl.pallas_call(
        _scatter_alongside(body, 10, 3, len(parts), nt - 1), grid=(nt,),
        in_specs=[row, col(0), col(1), col(2), row, halo, row, pl.BlockSpec((HALO, D), lambda i: (0, 0)), vec, vec]
        + side_in,
        out_specs=[pl.BlockSpec((ts, A_COLS), lambda i: (rev(i), 0)),
                   pl.BlockSpec((HALO, D), lambda i: (0, 0)), pl.BlockSpec((8, D), lambda i: (0, 0))] + side_out,
        out_shape=[jax.ShapeDtypeStruct((s, A_COLS), BF16), jax.ShapeDtypeStruct((HALO, D), F32),
                   jax.ShapeDtypeStruct((8, D), F32)] + side_shapes,
        scratch_shapes=[pltpu.VMEM((ts + HALO, D), F32), pltpu.VMEM((ts + HALO, D), F32),
                        pltpu.VMEM((ts, D), F32), pltpu.VMEM((8 * HALO, D), F32),
                        pltpu.VMEM((7, ts + HALO, D), F32), pltpu.VMEM((7, ts + HALO, D), F32)] + side_sems,
        name="conv_bwd", compiler_params=_params(("arbitrary",), VMEM_BIG))(
            dza, proj_a, proj_a, proj_a, u0, u0, u1, conv_w, ln_w, ln_b, *parts)
    return outs[0], outs[1], outs[2], list(outs[3:])


def _mla_prep(proj_l, q_norm_w, kv_norm_w, w_uq2, w_ukv, cos_t, sin_t, ts):
    s = proj_l.shape[0]

    def body(pl_ref, qw_ref, kw_ref, wq_ref, wkv_ref, c_ref, s_ref, qn_ref, kvn_ref, q_ref, k_ref, v_ref):
        first = _first_half_mask(ts)
        cs = c_ref[...]
        sn = s_ref[...]

        def rms(v, w):
            return v * lax.rsqrt(_rowmean(v * v) + EPS) * w

        def rope(v):
            return v * cs + _swap_halves(v, first) * sn

        qn = rms(pl_ref[:, 0:RQ], qw_ref[...]).astype(BF16)
        kvn = rms(pl_ref[:, RQ:2 * RQ], kw_ref[...]).astype(BF16)
        qn_ref[...] = qn
        kvn_ref[...] = kvn
        q = _dot(qn, wq_ref[...])
        kv = _dot(kvn, wkv_ref[...])
        kr = rope(pl_ref[:, 2 * RQ:2 * RQ + LANE]).astype(BF16)
        for h in range(H):
            q_ref[h, :, 0:DN] = q[:, DN * h:DN * (h + 1)].astype(BF16)
            q_ref[h, :, DN:2 * DN] = rope(q[:, H * DN + LANE * h:H * DN + LANE * (h + 1)]).astype(BF16)
            k_ref[h, :, 0:DN] = kv[:, 2 * DN * h:2 * DN * h + DN].astype(BF16)
            k_ref[h, :, DN:2 * DN] = kr
            v_ref[h, :, 0:DN] = kv[:, 2 * DN * h + DN:2 * DN * (h + 1)].astype(BF16)
            v_ref[h, :, DN:2 * DN] = jnp.ones((ts, DN), BF16)

    const = lambda shape: pl.BlockSpec(shape, lambda i: (0,) * len(shape))
    rowb = lambda w: pl.BlockSpec((ts, w), lambda i: (i, 0))
    head = lambda w: pl.BlockSpec((H, ts, w), lambda i: (0, i, 0))
    return pl.pallas_call(
        body, grid=(s // ts,),
        in_specs=[rowb(L_COLS), const((1, RQ)), const((1, RQ)), const((RQ, 2 * H * DN)), const((RQ, 2 * H * DN)),
                  rowb(LANE), rowb(LANE)],
        out_specs=[rowb(RQ), rowb(RQ), head(2 * DN), head(2 * DN), head(2 * DN)],
        out_shape=[jax.ShapeDtypeStruct((s, RQ), BF16), jax.ShapeDtypeStruct((s, RQ), BF16),
                   jax.ShapeDtypeStruct((H, s, 2 * DN), BF16), jax.ShapeDtypeStruct((H, s, 2 * DN), BF16),
                   jax.ShapeDtypeStruct((H, s, 2 * DN), BF16)],
        name="mla_prep", compiler_params=_params(("parallel",)))(
            proj_l, q_norm_w, kv_norm_w, w_uq2, w_ukv, cos_t, sin_t)


def _mla_prep_bwd(dq, dk, dv, proj_l, qn, kvn, q_norm_w, kv_norm_w, w_uq2, w_ukv, cos_t, sin_t, ts):
    s = proj_l.shape[0]

    def body(dq_ref, dk_ref, dv_ref, pl_ref, qn_ref, kvn_ref, qw_ref, kw_ref, wq_ref, wkv_ref, c_ref, s_ref,
             dpl_ref, gwq_ref, gwkv_ref, gv_ref, dq2_ref, dkv2_ref):
        @pl.when(pl.program_id(0) == 0)
        def _():
            gwq_ref[...] = jnp.zeros_like(gwq_ref)
            gwkv_ref[...] = jnp.zeros_like(gwkv_ref)
            gv_ref[...] = jnp.zeros_like(gv_ref)

        first = _first_half_mask(ts)
        cs = c_ref[...]
        sn = s_ref[...]

        def rope_bwd(g):
            return g * cs + _swap_halves(g * sn, first)

        def rms_bwd(v, w, dy):
            r = lax.rsqrt(_rowmean(v * v) + EPS)
            vh = v * r
            dvh = dy * w
            return r * (dvh - vh * _rowmean(dvh * vh)), _colsum(dy * vh)

        dkr = None
        for h in range(H):
            dq2_ref[:, DN * h:DN * (h + 1)] = dq_ref[h, :, 0:DN].astype(BF16)
            dq2_ref[:, H * DN + LANE * h:H * DN + LANE * (h + 1)] = rope_bwd(dq_ref[h, :, DN:2 * DN]).astype(BF16)
            dkv2_ref[:, 2 * DN * h:2 * DN * h + DN] = dk_ref[h, :, 0:DN].astype(BF16)
            dkv2_ref[:, 2 * DN * h + DN:2 * DN * (h + 1)] = dv_ref[h].astype(BF16)
            part = dk_ref[h, :, DN:2 * DN]
            dkr = part if dkr is None else dkr + part

        dq2 = dq2_ref[...]
        dkv2 = dkv2_ref[...]
        gwq_ref[...] += _dot_tn(qn_ref[...], dq2)
        gwkv_ref[...] += _dot_tn(kvn_ref[...], dkv2)
        dcq, gq = rms_bwd(pl_ref[:, 0:RQ], qw_ref[...], _dot_nt(dq2, wq_ref[...]))
        dckv, gkv = rms_bwd(pl_ref[:, RQ:2 * RQ], kw_ref[...], _dot_nt(dkv2, wkv_ref[...]))
        gv_ref[0:1, :] += gq
        gv_ref[1:2, :] += gkv
        dpl_ref[:, 0:RQ] = dcq.astype(BF16)
        dpl_ref[:, RQ:2 * RQ] = dckv.astype(BF16)
        dpl_ref[:, 2 * RQ:2 * RQ + LANE] = rope_bwd(dkr).astype(BF16)

    const = lambda shape: pl.BlockSpec(shape, lambda i: (0,) * len(shape))
    rowb = lambda w: pl.BlockSpec((ts, w), lambda i: (i, 0))
    head = lambda w: pl.BlockSpec((H, ts, w), lambda i: (0, i, 0))
    return pl.pallas_call(
        body, grid=(s // ts,),
        in_specs=[head(2 * DN), head(2 * DN), head(DN), rowb(L_COLS), rowb(RQ), rowb(RQ), const((1, RQ)),
                  const((1, RQ)), const((RQ, 2 * H * DN)), const((RQ, 2 * H * DN)), rowb(LANE), rowb(LANE)],
        out_specs=[rowb(L_COLS), const((RQ, 2 * H * DN)), const((RQ, 2 * H * DN)), const((8, RQ))],
        out_shape=[jax.ShapeDtypeStruct((s, L_COLS), BF16), jax.ShapeDtypeStruct((RQ, 2 * H * DN), F32),
                   jax.ShapeDtypeStruct((RQ, 2 * H * DN), F32), jax.ShapeDtypeStruct((8, RQ), F32)],
        scratch_shapes=[pltpu.VMEM((ts, 2 * H * DN), BF16), pltpu.VMEM((ts, 2 * H * DN), BF16)],
        name="mla_prep_bwd", compiler_params=_params(("arbitrary",), VMEM_BIG))(
            dq, dk, dv, proj_l, qn, kvn, q_norm_w, kv_norm_w, w_uq2, w_ukv, cos_t, sin_t)


def _causal_pairs(n, by_key):
    if by_key:
        pairs = [(i, j) for j in range(n) for i in range(j, n)]
    else:
        pairs = [(i, j) for i in range(n) for j in range(i + 1)]
    return (jnp.asarray(np.array([p[0] for p in pairs], np.int32)),
            jnp.asarray(np.array([p[1] for p in pairs], np.int32)))


LOG2E = 1.4426950408889634
LN2 = 0.6931471805599453
ATT_HEADS_FWD = 4
ATT_HEADS = 2
ATT_ROWS = 64


def _diag_width(r0, t):
    return min(t, -(-(r0 + ATT_ROWS) // LANE) * LANE)


def _diag_mask_rows(r0, width):
    rows = r0 + lax.broadcasted_iota(jnp.int32, (ATT_ROWS, width), 0)
    cols = lax.broadcasted_iota(jnp.int32, (ATT_ROWS, width), 1)
    return cols <= rows


def _diag_mask(t):
    return lax.broadcasted_iota(jnp.int32, (t, t), 1) <= lax.broadcasted_iota(jnp.int32, (t, t), 0)


def _attn_fwd(q, k, v, t):
    s = q.shape[1]
    n = s // t
    scale2 = float((DN + DR) ** -0.5) * LOG2E
    qi, ki = _causal_pairs(n, by_key=False)

    def body(qi_ref, ki_ref, q_ref, k_ref, v_ref, o_ref, lse_ref, *scratch):
        per_head = [scratch[5 * h:5 * h + 5] for h in range(ATT_HEADS_FWD)]
        p = pl.program_id(1)
        i = qi_ref[p]
        j = ki_ref[p]

        @pl.when(j == 0)
        def _():
            for m_sc, acc_sc, _, _, _ in per_head:
                m_sc[...] = jnp.full_like(m_sc, -jnp.inf)
                acc_sc[...] = jnp.zeros_like(acc_sc)

        def scores(h, diag):
            sc = _dot_nt(q_ref[h], k_ref[h])
            if diag:
                sc = jnp.where(_diag_mask(t), sc, -jnp.inf)
            per_head[h][2][...] = sc

        def rowmax(h, rows):
            per_head[h][4][rows, :] = jnp.max(per_head[h][2][rows, :], axis=-1, keepdims=True)

        def stats(h):
            m_sc, acc_sc, _, _, mx_sc = per_head[h]
            m_prev = m_sc[...]
            m_new = jnp.maximum(m_prev, mx_sc[...] * scale2)
            m_sc[...] = m_new
            acc_sc[...] = jnp.exp2(m_prev - m_new) * acc_sc[...]

        def probs(h, rows):
            m_sc, _, s_sc, p_sc, _ = per_head[h]
            p_sc[rows, :] = jnp.exp2(s_sc[rows, :] * scale2 - m_sc[rows, :]).astype(BF16)

        def values(h):
            _, acc_sc, _, p_sc, _ = per_head[h]
            acc_sc[...] += _dot(p_sc[...], v_ref[h])

        def step(diag):
            blocks = [slice(r0, r0 + ATT_ROWS) for r0 in range(0, t, ATT_ROWS)]
            for h in range(ATT_HEADS_FWD):
                scores(h, diag)
            for rows in blocks:
                rowmax(0, rows)
            stats(0)
            for h in range(ATT_HEADS_FWD):
                for rows in blocks:
                    probs(h, rows)
                    if h + 1 < ATT_HEADS_FWD:
                        rowmax(h + 1, rows)
                if h + 1 < ATT_HEADS_FWD:
                    stats(h + 1)
                values(h)

        @pl.when(j < i)
        def _():
            step(False)

        @pl.when(j == i)
        def _():
            step(True)
            for h, (m_sc, acc_sc, _, _, _) in enumerate(per_head):
                l = acc_sc[:, DN:2 * DN]
                o_ref[:, DN * h:DN * (h + 1)] = acc_sc[:, 0:DN] / l
                lse_ref[h] = (m_sc[...] + jnp.log2(l[:, 0:1])) * LN2

    hb = ATT_HEADS_FWD
    grid_spec = pltpu.PrefetchScalarGridSpec(
        num_scalar_prefetch=2, grid=(H // hb, int(qi.shape[0])),
        in_specs=[pl.BlockSpec((hb, t, 2 * DN), lambda h, p, qi, ki: (h, qi[p], 0)),
                  pl.BlockSpec((hb, t, 2 * DN), lambda h, p, qi, ki: (h, ki[p], 0)),
                  pl.BlockSpec((hb, t, 2 * DN), lambda h, p, qi, ki: (h, ki[p], 0))],
        out_specs=[pl.BlockSpec((t, hb * DN), lambda h, p, qi, ki: (qi[p], h)),
                   pl.BlockSpec((hb, t, 1), lambda h, p, qi, ki: (h, qi[p], 0))],
        scratch_shapes=[pltpu.VMEM((t, 1), F32), pltpu.VMEM((t, 2 * DN), F32), pltpu.VMEM((t, t), F32),
                        pltpu.VMEM((t, t), BF16), pltpu.VMEM((t, 1), F32)] * hb)
    return pl.pallas_call(
        body, grid_spec=grid_spec,
        out_shape=[jax.ShapeDtypeStruct((s, H * DN), F32), jax.ShapeDtypeStruct((H, s, 1), F32)],
        name="attn_fwd", compiler_params=_params(("parallel", "arbitrary"), VMEM_BIG))(qi, ki, q, k, v)


def _attn_bwd(q, k, v, do, lse, delta, t):
    s = q.shape[1]
    n = s // t
    scale = float((DN + DR) ** -0.5)
    qi, ki = _causal_pairs(n, by_key=True)

    def body(qi_ref, ki_ref, q_ref, k_ref, v_ref, do_ref, lse_ref, dl_ref, dq_ref, dk_ref, dv_ref,
             dk_sc, dv_sc, s_sc, dp_sc, p_sc, ds_sc):
        p = pl.program_id(1)
        i = qi_ref[p]
        j = ki_ref[p]

        @pl.when(p == 0)
        def _():
            dq_ref[...] = jnp.zeros_like(dq_ref)

        @pl.when(i == j)
        def _():
            dk_sc[...] = jnp.zeros_like(dk_sc)
            dv_sc[...] = jnp.zeros_like(dv_sc)

        def step(diag):
            for h in range(ATT_HEADS):
                s_sc[h] = _dot_nt(q_ref[h], k_ref[h])
                dp_sc[h] = _dot_nt(do_ref[:, DN * h:DN * (h + 1)], v_ref[h, :, 0:DN])
            for h in range(ATT_HEADS):
                for r0 in range(0, t, ATT_ROWS):
                    rows = slice(r0, r0 + ATT_ROWS)
                    width = _diag_width(r0, t) if diag else t
                    sc = s_sc[h, rows, 0:width] * (scale * LOG2E)
                    if diag:
                        sc = jnp.where(_diag_mask_rows(r0, width), sc, -jnp.inf)
                    pr = jnp.exp2(sc - lse_ref[h, rows, :] * LOG2E)
                    ds = pr * (dp_sc[h, rows, 0:width] - dl_ref[h, rows, :]) * scale
                    p_sc[h, rows, 0:width] = pr.astype(BF16)
                    ds_sc[h, rows, 0:width] = ds.astype(BF16)
                    if width < t:
                        p_sc[h, rows, width:t] = jnp.zeros((ATT_ROWS, t - width), BF16)
                        ds_sc[h, rows, width:t] = jnp.zeros((ATT_ROWS, t - width), BF16)
            q_rows = pl.ds(pl.multiple_of(i * t, t), t)
            for h in range(ATT_HEADS):
                dv_sc[h] += _dot_tn(p_sc[h], do_ref[:, DN * h:DN * (h + 1)])
                dk_sc[h] += _dot_tn(ds_sc[h], q_ref[h])
                dq_ref[h, q_rows, :] += _dot(ds_sc[h], k_ref[h])

        @pl.when(i > j)
        def _():
            step(False)

        @pl.when(i == j)
        def _():
            step(True)

        @pl.when(i == n - 1)
        def _():
            dk_ref[...] = dk_sc[...]
            dv_ref[...] = dv_sc[...]

    hb = ATT_HEADS
    grid_spec = pltpu.PrefetchScalarGridSpec(
        num_scalar_prefetch=2, grid=(H // hb, int(qi.shape[0])),
        in_specs=[pl.BlockSpec((hb, t, 2 * DN), lambda h, p, qi, ki: (h, qi[p], 0)),
                  pl.BlockSpec((hb, t, 2 * DN), lambda h, p, qi, ki: (h, ki[p], 0)),
                  pl.BlockSpec((hb, t, 2 * DN), lambda h, p, qi, ki: (h, ki[p], 0)),
                  pl.BlockSpec((t, hb * DN), lambda h, p, qi, ki: (qi[p], h)),
                  pl.BlockSpec((hb, t, 1), lambda h, p, qi, ki: (h, qi[p], 0)),
                  pl.BlockSpec((hb, t, 1), lambda h, p, qi, ki: (h, qi[p], 0))],
        out_specs=[pl.BlockSpec((hb, s, 2 * DN), lambda h, p, qi, ki: (h, 0, 0)),
                   pl.BlockSpec((hb, t, 2 * DN), lambda h, p, qi, ki: (h, ki[p], 0)),
                   pl.BlockSpec((hb, t, DN), lambda h, p, qi, ki: (h, ki[p], 0))],
        scratch_shapes=[pltpu.VMEM((hb, t, 2 * DN), F32), pltpu.VMEM((hb, t, DN), F32),
                        pltpu.VMEM((hb, t, t), F32), pltpu.VMEM((hb, t, t), F32),
                        pltpu.VMEM((hb, t, t), BF16), pltpu.VMEM((hb, t, t), BF16)])
    return pl.pallas_call(
        body, grid_spec=grid_spec,
        out_shape=[jax.ShapeDtypeStruct((H, s, 2 * DN), F32), jax.ShapeDtypeStruct((H, s, 2 * DN), F32),
                   jax.ShapeDtypeStruct((H, s, DN), F32)],
        name="attn_bwd", compiler_params=_params(("parallel", "arbitrary"), VMEM_BIG))(
            qi, ki, q, k, v, do, lse, delta)


def _middle(za, o, proj_g, x, tgt, gate, fnw, wco, wao, wo, ts):
    s = x.shape[0]
    inv_d = 1.0 / D

    def body(za_ref, o_ref, bg_ref, ga_ref, gb_ref, x_ref, t_ref, gate_ref, fnw_ref, wco_ref, wao_ref, wo_ref,
             dx2_ref, dza_ref, do_ref, dl_ref, dpg_ref, zb_ref, mg_ref, dmo_ref, dya_ref, dyb_ref, vec_ref):
        @pl.when(pl.program_id(0) == 0)
        def _():
            vec_ref[...] = jnp.zeros_like(vec_ref)

        ov = o_ref[...]
        bg = bg_ref[...]
        sb = _sigmoid(bg)
        silu_b = bg * sb
        zb = (ov * silu_b).astype(BF16)
        zb_ref[...] = zb
        ya = _dot(za_ref[...], wco_ref[...])
        yb = _dot(zb, wao_ref[...])
        sa = _sigmoid(ga_ref[...])
        sg = _sigmoid(gb_ref[...])
        mg = (sa * ya + sg * yb).astype(BF16)
        mg_ref[...] = mg
        mo = _dot(mg, wo_ref[...])
        gate_v = gate_ref[...]
        x2 = x_ref[...] + gate_v * mo
        r = lax.rsqrt(_rowmean(x2 * x2) + EPS)
        xh = x2 * r
        fw = fnw_ref[...]
        e = xh * fw - t_ref[...]
        vec_ref[2:3, :] += _colsum(e * e)
        dy = e * inv_d
        vec_ref[0:1, :] += _colsum(dy * xh)
        dxh = dy * fw
        dx2 = r * (dxh - xh * _rowmean(dxh * xh))
        dx2_ref[...] = dx2
        vec_ref[1:2, :] += _colsum(dx2 * mo)
        dmo = (gate_v * dx2).astype(BF16)
        dmo_ref[...] = dmo
        dmg = _dot_nt(dmo, wo_ref[...])
        dya = (sa * dmg).astype(BF16)
        dyb = (sg * dmg).astype(BF16)
        dya_ref[...] = dya
        dyb_ref[...] = dyb
        dpg_ref[:, D:2 * D] = (dmg * ya * (sa * (1.0 - sa))).astype(BF16)
        dpg_ref[:, 2 * D:3 * D] = (dmg * yb * (sg * (1.0 - sg))).astype(BF16)
        dza_ref[...] = _dot_nt(dya, wco_ref[...])
        dzb = _dot_nt(dyb, wao_ref[...])
        dov = dzb * silu_b
        do_ref[...] = dov.astype(BF16)
        dpg_ref[:, 0:D] = (dzb * ov * _dsilu(bg, sb)).astype(BF16)
        dprod = dov * ov
        for h in range(H):
            dl_ref[h] = jnp.sum(dprod[:, DN * h:DN * (h + 1)], axis=-1, keepdims=True)

    col = lambda c: pl.BlockSpec((ts, D), lambda i, c=c: (i, c))
    row = pl.BlockSpec((ts, D), lambda i: (i, 0))
    vec = pl.BlockSpec((1, D), lambda i: (0, 0))
    wsp = pl.BlockSpec((D, D), lambda i: (0, 0))
    bf = jax.ShapeDtypeStruct((s, D), BF16)
    ff = jax.ShapeDtypeStruct((s, D), F32)
    return pl.pallas_call(
        body, grid=(s // ts,),
        in_specs=[row, row, col(0), col(1), col(2), row, row, vec, vec, wsp, wsp, wsp],
        out_specs=[row, row, row, pl.BlockSpec((H, ts, 1), lambda i: (0, i, 0)),
                   pl.BlockSpec((ts, G_COLS), lambda i: (i, 0)), row, row, row, row, row,
                   pl.BlockSpec((8, D), lambda i: (0, 0))],
        out_shape=[ff, ff, bf, jax.ShapeDtypeStruct((H, s, 1), F32), jax.ShapeDtypeStruct((s, G_COLS), BF16),
                   bf, bf, bf, bf, bf, jax.ShapeDtypeStruct((8, D), F32)],
        name="middle", compiler_params=_params(("arbitrary",), VMEM_BIG))(
            za, o, proj_g, proj_g, proj_g, x, tgt, gate, fnw, wco, wao, wo)


def _input_bwd(dpa, dpl, dpg, wa, wl, wg, x, dx2, norm_w, scale, ts, parts):
    s = x.shape[0]

    def body(dpa_ref, dpl_ref, dpg_ref, wa_ref, wl_ref, wg_ref, x_ref, dx2_ref, nw_ref, sc_ref, gx_ref, gv_ref):
        @pl.when(pl.program_id(0) == 0)
        def _():
            gv_ref[...] = jnp.zeros_like(gv_ref)

        dh = (_dot_nt(dpa_ref[...], wa_ref[...]) + _dot_nt(dpl_ref[...], wl_ref[...])
              + _dot_nt(dpg_ref[...], wg_ref[...]))
        xv = x_ref[...]
        r = lax.rsqrt(_rowmean(xv * xv) + EPS)
        xh = xv * r
        nw = nw_ref[...]
        gv_ref[0:1, :] += _colsum(dh)
        gv_ref[1:2, :] += _colsum(dh * (xh * nw))
        dy = dh * (1.0 + sc_ref[...])
        gv_ref[2:3, :] += _colsum(dy * xh)
        dxh = dy * nw
        gx_ref[...] = dx2_ref[...] + r * (dxh - xh * _rowmean(dxh * xh))

    const = lambda shape: pl.BlockSpec(shape, lambda i: (0, 0))
    rowb = lambda w: pl.BlockSpec((ts, w), lambda i: (i, 0))
    side_in, side_out, side_shapes, side_sems = _scatter_operands(parts)
    outs = pl.pallas_call(
        _scatter_alongside(body, 10, 2, len(parts), s // ts - 1), grid=(s // ts,),
        in_specs=[rowb(A_COLS), rowb(L_COLS), rowb(G_COLS), const((D, A_COLS)), const((D, L_COLS)),
                  const((D, G_COLS)), rowb(D), rowb(D), const((1, D)), const((1, D))] + side_in,
        out_specs=[rowb(D), const((8, D))] + side_out,
        out_shape=[jax.ShapeDtypeStruct((s, D), F32), jax.ShapeDtypeStruct((8, D), F32)] + side_shapes,
        scratch_shapes=side_sems,
        name="input_bwd", compiler_params=_params(("arbitrary",), VMEM_BIG))(
            dpa, dpl, dpg, wa, wl, wg, x, dx2, norm_w, scale, *parts)
    return outs[0], outs[1], list(outs[2:])


def _adamw(w, g, m, v, tr, name):
    rows, cols = w.shape
    c1 = 1.0 - ADAM_B1 ** ADAM_STEP
    c2 = 1.0 - ADAM_B2 ** ADAM_STEP

    def body(w_ref, g_ref, m_ref, v_ref, d_ref, nm_ref, nv_ref):
        gv = g_ref[...]
        nm = ADAM_B1 * m_ref[...] + (1.0 - ADAM_B1) * gv
        nv = ADAM_B2 * v_ref[...] + (1.0 - ADAM_B2) * (gv * gv)
        nm_ref[...] = nm
        nv_ref[...] = nv
        d_ref[...] = -ADAM_LR * ((nm / c1) / (jnp.sqrt(nv / c2) + ADAM_EPS) + ADAM_WD * w_ref[...])

    blk = pl.BlockSpec((tr, cols), lambda i: (i, 0))
    shp = jax.ShapeDtypeStruct((rows, cols), F32)
    return pl.pallas_call(
        body, grid=(rows // tr,), in_specs=[blk] * 4, out_specs=[blk] * 3, out_shape=[shp] * 3, name=name,
        compiler_params=_params(("parallel",)))(w, g, m, v)


def _ada_fwd(c_all, w_ada_shard, b_ada_shard):
    def body(c_ref, w_ref, b_ref, o_ref):
        cv = c_ref[...]
        o_ref[...] = jnp.dot(cv * _sigmoid(cv), w_ref[...], preferred_element_type=F32,
                             precision=lax.Precision.HIGHEST) + b_ref[...]

    return pl.pallas_call(
        body, out_shape=jax.ShapeDtypeStruct((N_DEV, w_ada_shard.shape[1]), F32), name="ada_fwd")(
            c_all, w_ada_shard, b_ada_shard)


def _ada_bwd(c_all_t, dmod_shard):
    def body(c_ref, d_ref, o_ref):
        cv = c_ref[...]
        o_ref[...] = jnp.dot(cv * _sigmoid(cv), d_ref[...], preferred_element_type=F32,
                             precision=lax.Precision.HIGHEST)

    return pl.pallas_call(
        body, out_shape=jax.ShapeDtypeStruct((D, dmod_shard.shape[1]), F32), name="ada_bwd")(c_all_t, dmod_shard)


def _sum_slabs(stack, tr, name):
    n, rows, cols = stack.shape

    def body(s_ref, o_ref):
        acc = s_ref[0]
        for k in range(1, n):
            acc = acc + s_ref[k]
        o_ref[...] = acc

    return pl.pallas_call(
        body, grid=(rows // tr,), in_specs=[pl.BlockSpec((n, tr, cols), lambda i: (0, i, 0))],
        out_specs=pl.BlockSpec((tr, cols), lambda i: (i, 0)), out_shape=jax.ShapeDtypeStruct((rows, cols), F32),
        name=name, compiler_params=_params(("parallel",)))(stack)


def _sum_chip_slabs(arrived, part, place, tr, name):
    n, rows, cols = arrived.shape
    per = rows // tr

    def body(place_ref, a_ref, p_ref, o_ref):
        acc = p_ref[0].astype(F32)
        for k in range(n):
            acc = acc + a_ref[k].astype(F32)
        o_ref[...] = acc

    grid_spec = pltpu.PrefetchScalarGridSpec(
        num_scalar_prefetch=1, grid=(per,),
        in_specs=[pl.BlockSpec((n, tr, cols), lambda i, pc: (0, i, 0)),
                  pl.BlockSpec((1, tr, cols), lambda i, pc: (pc[0], i, 0))],
        out_specs=pl.BlockSpec((tr, cols), lambda i, pc: (pc[1] * per + i, 0)))
    return pl.pallas_call(
        body, grid_spec=grid_spec, out_shape=jax.ShapeDtypeStruct((2 * rows, cols), F32), name=name,
        compiler_params=_params(("parallel",)))(place, arrived, part)


def _add_own_half(full, other, core, tr, name):
    n, rows, cols = other.shape
    per = rows // tr

    def body(c_ref, f_ref, o_ref, out_ref):
        out_ref[...] = (f_ref[...] + o_ref[...]).astype(BF16)

    grid_spec = pltpu.PrefetchScalarGridSpec(
        num_scalar_prefetch=1, grid=(n, per),
        in_specs=[pl.BlockSpec((1, tr, cols), lambda k, i, c: (k, c[0] * per + i, 0)),
                  pl.BlockSpec((1, tr, cols), lambda k, i, c: (k, i, 0))],
        out_specs=pl.BlockSpec((1, tr, cols), lambda k, i, c: (k, i, 0)))
    return pl.pallas_call(
        body, grid_spec=grid_spec, out_shape=jax.ShapeDtypeStruct((n, rows, cols), BF16), name=name,
        compiler_params=_params(("parallel", "parallel")))(core, full, other)


def _allgather8(block, src_rows, vmem, name):
    n = block.shape[1]
    m = src_rows
    sliced = block.shape[0] != m

    def body(x_ref, out_ref, send_sems, recv_sems, local_sem):
        x, y, c = _coords()
        me, sibling = (x, y, c), (x, y, 1 - c)
        chips = [(1 - x, y), (x, 1 - y), (1 - x, 1 - y)]
        src = x_ref.at[pl.ds(pl.multiple_of(c * m, 16), m), :] if sliced else x_ref

        def rows(px, py, pc):
            return out_ref.at[pl.ds(pl.multiple_of((4 * px + 2 * py + pc) * m, 8), m), :]

        def copy(k, blk, to, source=None):
            return pltpu.make_async_remote_copy(
                src_ref=rows(*blk) if source is None else source, dst_ref=rows(*blk),
                send_sem=send_sems.at[k], recv_sem=recv_sems.at[k], device_id=to, device_id_type=MESH)

        mine = pltpu.make_async_copy(src, rows(*me), local_sem)
        mine.start()
        first = [copy(0, me, sibling, source=src)]
        first += [copy(1 + j, me, (*chip, c), source=src) for j, chip in enumerate(chips)]
        for cp in first:
            cp.start()
        passed = [copy(4 + j, (*chip, c), sibling) for j, chip in enumerate(chips)]
        for j, chip in enumerate(chips):
            copy(1 + j, (*chip, c), me).wait_recv()
            passed[j].start()
        copy(0, sibling, me).wait_recv()
        for j, chip in enumerate(chips):
            copy(4 + j, (*chip, 1 - c), me).wait_recv()
        for cp in first + passed:
            cp.wait_send()
        mine.wait()

    space = pltpu.VMEM if vmem else pl.ANY
    return pl.pallas_call(
        body, out_shape=jax.ShapeDtypeStruct((N_DEV * m, n), block.dtype),
        in_specs=[pl.BlockSpec(memory_space=space)], out_specs=pl.BlockSpec(memory_space=space),
        scratch_shapes=[pltpu.SemaphoreType.DMA((7,)), pltpu.SemaphoreType.DMA((7,)), pltpu.SemaphoreType.DMA],
        name=name)(block)


def _gather_plan(x_refs, out_refs, send_sems, recv_sems, local_sems):
    n = len(x_refs)
    halves = [r.shape[0] // 2 for r in x_refs]
    x, y, c = _coords()
    me, sibling = (x, y, c), (x, y, 1 - c)
    chips = [(1 - x, y), (x, 1 - y), (1 - x, 1 - y)]

    def src(a):
        return x_refs[a].at[pl.ds(pl.multiple_of(c * halves[a], 16), halves[a]), :]

    def blk(a, px, py, pc):
        return out_refs[a].at[4 * px + 2 * py + pc]

    def copy(a, k, who, to, source=None):
        return pltpu.make_async_remote_copy(
            src_ref=blk(a, *who) if source is None else source, dst_ref=blk(a, *who),
            send_sem=send_sems.at[7 * a + k], recv_sem=recv_sems.at[7 * a + k], device_id=to, device_id_type=MESH)

    def mine(a):
        return pltpu.make_async_copy(src(a), blk(a, *me), local_sems.at[a])

    def first(a):
        return ([copy(a, 0, me, sibling, source=src(a))]
                + [copy(a, 1 + j, me, (*chip, c), source=src(a)) for j, chip in enumerate(chips)])

    def begin():
        for a in range(n):
            mine(a).start()
        for a in range(n):
            for cp in first(a):
                cp.start()

    def finish():
        onward = []
        for j, chip in enumerate(chips):
            for a in range(n):
                copy(a, 1 + j, (*chip, c), me).wait_recv()
                onward.append(copy(a, 4 + j, (*chip, c), sibling))
                onward[-1].start()
        for a in range(n):
            copy(a, 0, sibling, me).wait_recv()
        for j, chip in enumerate(chips):
            for a in range(n):
                copy(a, 4 + j, (*chip, 1 - c), me).wait_recv()
        for a in range(n):
            for cp in first(a):
                cp.wait_send()
        for cp in onward:
            cp.wait_send()
        for a in range(n):
            mine(a).wait()

    return begin, finish


def _gather_operands(shards):
    n = len(shards)
    shapes = [jax.ShapeDtypeStruct((N_DEV, a.shape[0] // 2, a.shape[1]), a.dtype) for a in shards]
    sems = [pltpu.SemaphoreType.DMA((7 * n,)), pltpu.SemaphoreType.DMA((7 * n,)), pltpu.SemaphoreType.DMA((n,))]
    return shapes, sems


def _as_chip_slabs(gathered, shards):
    return [o.reshape(N_CHIP, a.shape[0], a.shape[1]) for o, a in zip(gathered, shards)]


def _gather_weights(shards):
    n = len(shards)

    def body(*refs):
        begin, finish = _gather_plan(refs[:n], refs[n:2 * n], *refs[2 * n:])
        begin()
        finish()

    shapes, sems = _gather_operands(shards)
    outs = pl.pallas_call(
        body, out_shape=shapes, in_specs=[HBM_REF] * n, out_specs=[HBM_REF] * n, scratch_shapes=sems,
        name="gather_weights")(*shards)
    return _as_chip_slabs(outs, shards)


def _gather_alongside(body, n_in, n_out, n_shards, last_step):
    def wrapped(*refs):
        ins, shards = refs[:n_in], refs[n_in:n_in + n_shards]
        rest = refs[n_in + n_shards:]
        outs, gathered = rest[:n_out], rest[n_out:n_out + n_shards]
        scratch, sems = rest[n_out + n_shards:-3], rest[-3:]

        @pl.when(pl.program_id(0) == 0)
        def _():
            _gather_plan(shards, gathered, *sems)[0]()

        body(*ins, *outs, *scratch)

        @pl.when(pl.program_id(0) == last_step)
        def _():
            _gather_plan(shards, gathered, *sems)[1]()

    return wrapped


def _swap_halves_with_sibling(fulls, name):
    n = len(fulls)
    halves = [a.shape[1] // 2 for a in fulls]

    def body(*refs):
        f_refs, got_refs = refs[:n], refs[n:2 * n]
        send_sems, recv_sems = refs[2 * n:]
        x, y, c = _coords()
        copies = []
        for a in range(n):
            src = f_refs[a].at[:, pl.ds(pl.multiple_of((1 - c) * halves[a], 8), halves[a]), :]
            copies.append(pltpu.make_async_remote_copy(
                src_ref=src, dst_ref=got_refs[a], send_sem=send_sems.at[a], recv_sem=recv_sems.at[a],
                device_id=(x, y, 1 - c), device_id_type=MESH))
        for cp in copies:
            cp.start()
        for cp in copies:
            cp.wait()

    return pl.pallas_call(
        body, out_shape=[jax.ShapeDtypeStruct((a.shape[0], h, a.shape[2]), a.dtype) for a, h in zip(fulls, halves)],
        in_specs=[HBM_REF] * n, out_specs=[HBM_REF] * n,
        scratch_shapes=[pltpu.SemaphoreType.DMA((n,)), pltpu.SemaphoreType.DMA((n,))],
        name=name)(*fulls)


def _join_halves_with_sibling(wholes):
    n = len(wholes)

    def body(*refs):
        out_refs = refs[n:2 * n]
        send_sems, recv_sems = refs[2 * n:]
        x, y, c = _coords()

        def push(a, core):
            rows = wholes[a].shape[0] // 2
            half = out_refs[a].at[pl.ds(pl.multiple_of(core * rows, 8), rows), :]
            return pltpu.make_async_remote_copy(
                src_ref=half, dst_ref=half, send_sem=send_sems.at[a], recv_sem=recv_sems.at[a],
                device_id=(x, y, 1 - c), device_id_type=MESH)

        for a in range(n):
            push(a, c).start()
        for a in range(n):
            push(a, 1 - c).wait_recv()
        for a in range(n):
            push(a, c).wait_send()

    return pl.pallas_call(
        body, out_shape=[jax.ShapeDtypeStruct(a.shape, a.dtype) for a in wholes],
        in_specs=[HBM_REF] * n, out_specs=[HBM_REF] * n, input_output_aliases={a: a for a in range(n)},
        scratch_shapes=[pltpu.SemaphoreType.DMA((n,)), pltpu.SemaphoreType.DMA((n,))],
        name="rs_pair_join")(*wholes)


def _cols_to_slabs(g):
    rows, cols = g.shape
    return g.reshape(rows, N_CHIP, cols // N_CHIP).transpose(1, 0, 2)


def _slabs_to_cols(w):
    n, rows, cols = w.shape
    return w.transpose(1, 0, 2).reshape(rows, n * cols)


def _uq_to_padded(w_uq):
    per = w_uq.reshape(RQ, H, DN + DR)
    nope = per[:, :, :DN].reshape(RQ, H * DN)
    rope = jnp.pad(per[:, :, DN:], ((0, 0), (0, 0), (0, LANE - DR))).reshape(RQ, H * LANE)
    return jnp.concatenate([nope, rope], axis=1)


def _uq_from_padded(g):
    nope = g[:, :H * DN].reshape(RQ, H, DN)
    rope = g[:, H * DN:].reshape(RQ, H, LANE)[:, :, :DR]
    return jnp.concatenate([nope, rope], axis=2).reshape(RQ, H * (DN + DR))


def _rope_tables(positions):
    inv_freq = ROPE_THETA ** (-jnp.arange(0, DR, 2, dtype=F32) / DR)
    ang = positions.astype(F32)[:, None] * inv_freq
    cos, sin = jnp.cos(ang), jnp.sin(ang)
    return jnp.tile(cos, (1, 4)), jnp.tile(jnp.concatenate([-sin, sin], axis=1), (1, 2))


def _pair_sums(fulls, core, tag):
    from_sibling = _swap_halves_with_sibling(fulls, f"rs_pair_swap_{tag}")
    return [_add_own_half(f, o, core, min(256, o.shape[1]), f"add_own_half_{tag}{n}")
            for n, (f, o) in enumerate(zip(fulls, from_sibling))]


def _local_step(x, tgt, cos_t, sin_t, mod, weights, small, tiles, place):
    ts, ts_in, tm_nn, tm_tn, t_attn, chunk = tiles
    wa, wl, wg, later_shards, conv_w = weights
    norm_w, conv_b, ln_w, ln_b, q_norm_w, kv_norm_w, fnw = small
    shift, scale, gate = mod[:, 0:D], mod[:, D:2 * D], mod[:, 2 * D:3 * D]

    h = _adaln_norm(x, norm_w, shift, scale, ts)
    proj_a = _mm_nn(h, wa, tm_nn, D, "proj_a")
    u0, u1, za, (g_uq, g_ukv, g_co, g_ao, g_o) = _conv_fwd(proj_a, conv_w, conv_b, ln_w, ln_b, ts, chunk, later_shards)
    w_uq2, w_ukv = _uq_to_padded(_slabs_to_cols(g_uq)), _slabs_to_cols(g_ukv)
    wco, wao, wo = g_co.reshape(D, D), g_ao.reshape(D, D), g_o.reshape(D, D)
    proj_l = _mm_nn(h, wl, tm_nn, L_COLS, "proj_l")
    proj_g = _mm_nn(h, wg, tm_nn, D, "proj_g")
    qn, kvn, q, k, v = _mla_prep(proj_l, q_norm_w, kv_norm_w, w_uq2, w_ukv, cos_t, sin_t, ts)
    o, lse = _attn_fwd(q, k, v, t_attn)
    (dx2, dza, do, delta, dpg, zb, mg, dmo, dya, dyb, vec_mid) = _middle(
        za, o, proj_g, x, tgt, gate, fnw, wco, wao, wo, ts)
    g_wo = _mm_tn(mg, dmo, tm_tn, D, "grad_w_out")
    g_wco = _mm_tn(za, dya, tm_tn, D, "grad_w_conv_out")
    g_wao = _mm_tn(zb, dyb, tm_tn, D, "grad_w_attn_out")
    dq, dk, dv = _attn_bwd(q, k, v, do, lse, delta, t_attn)
    dpl, g_wuq2, g_wukv, vec_mla = _mla_prep_bwd(
        dq, dk, dv, proj_l, qn, kvn, q_norm_w, kv_norm_w, w_uq2, w_ukv, cos_t, sin_t, ts)

    core = place[1:2]
    nr = D // N_CHIP
    early = [_cols_to_slabs(_uq_from_padded(g_wuq2)), _cols_to_slabs(g_wukv), g_wco.reshape(N_CHIP, nr, D),
             g_wao.reshape(N_CHIP, nr, D), g_wo.reshape(N_CHIP, nr, D)]
    early_sums = _pair_sums(early, core, "a")
    dpa, g_conv_w, vec_conv, early_got = _conv_bwd(dza, proj_a, u0, u1, conv_w, ln_w, ln_b, ts, chunk, early_sums)

    g_wa = _mm_tn(h, dpa, tm_tn, D, "grad_w_in_a")
    g_wl = _mm_tn(h, dpl, tm_tn, L_COLS, "grad_w_in_l")
    g_wg = _mm_tn(h, dpg, tm_tn, D, "grad_w_in_g")
    g_w_in = jnp.concatenate([g_wa, g_wl[:, 0:L_COLS_RAW], g_wg], axis=1)
    late_sums = _pair_sums([_cols_to_slabs(g_w_in)], core, "b")
    grad_x, vec_in, late_got = _input_bwd(dpa, dpl, dpg, wa, wl, wg, x, dx2, norm_w, scale, ts_in, late_sums)

    wholes = [_sum_chip_slabs(a, p, place, min(128, a.shape[1]), f"sum_chip_slabs_{n}")
              for n, (a, p) in enumerate(zip(late_got + early_got, late_sums + early_sums))]
    shards = _join_halves_with_sibling(wholes)

    dmod = jnp.concatenate([vec_in[0:1], vec_in[1:2], vec_mid[1:2]], axis=1)
    sums = dict(dmod=dmod, norm_w=vec_in[2:3], conv_b=vec_conv[2:3], ln_w=vec_conv[0:1], ln_b=vec_conv[1:2],
                q_norm_w=vec_mla[0:1], kv_norm_w=vec_mla[1:2], final_norm_w=vec_mid[0:1], loss=vec_mid[2:3],
                conv_w=g_conv_w)
    return grad_x, shards, sums


SMALL_ORDER = (("dmod", 3 * D), ("norm_w", D), ("conv_b", D), ("ln_w", D), ("ln_b", D), ("q_norm_w", RQ),
               ("kv_norm_w", RQ), ("final_norm_w", D), ("loss", D), ("conv_w", HALO * D))
SMALL_ROWS = 336


def kernel(x, c, positions, w_ada, b_ada, norm_w, w_in, conv_w, conv_b, conv_ln_w, conv_ln_b, w_conv_out, q_norm_w, w_uq, kv_norm_w, w_ukv, w_attn_out, w_out, final_norm_w, loss_target, m_w_ada, m_b_ada, m_norm_w, m_w_in, m_conv_w, m_conv_b, m_conv_ln_w, m_conv_ln_b, m_w_conv_out, m_q_norm_w, m_w_uq, m_kv_norm_w, m_w_ukv, m_w_attn_out, m_w_out, m_final_norm_w, v_w_ada, v_b_ada, v_norm_w, v_w_in, v_conv_w, v_conv_b, v_conv_ln_w, v_conv_ln_b, v_w_conv_out, v_q_norm_w, v_w_uq, v_kv_norm_w, v_w_ukv, v_w_attn_out, v_w_out, v_final_norm_w):
    ix, iy, ic = _coords()
    chip = 2 * ix + iy
    dev = 4 * ix + 2 * iy + ic
    s = x.shape[1]
    tiles = (256, 512, 1024, 2048, 512, 32)

    conv_w_pad = jnp.pad(conv_w[0], ((0, HALO - KC), (0, 0)))
    small_in = jnp.concatenate([c.reshape(8, LANE), conv_w_pad.reshape(64, LANE)], axis=0)
    small_all = _allgather8(small_in, 72, True, "gather_c_conv").reshape(N_DEV, 72, LANE)
    c_all = small_all[:, 0:8].reshape(N_DEV, D)
    conv_full = jnp.concatenate(
        [small_all[2 * k, 8:72].reshape(HALO, D // N_CHIP) for k in range(N_CHIP)], axis=1)

    (g_in,) = _gather_weights([w_in[0].astype(BF16)])
    w_in_f = _slabs_to_cols(g_in)
    wa = w_in_f[:, 0:A_COLS]
    wl = jnp.pad(w_in_f[:, A_COLS:A_COLS + L_COLS_RAW], ((0, 0), (0, L_COLS - L_COLS_RAW)))
    wg = w_in_f[:, A_COLS + L_COLS_RAW:]
    later_shards = [w[0].astype(BF16) for w in (w_uq, w_ukv, w_conv_out, w_attn_out, w_out)]
    weights = (wa, wl, wg, later_shards, conv_full)

    ada_cols = w_ada.shape[2]
    b_shard = lax.dynamic_slice(b_ada, (0, chip * ada_cols), (1, ada_cols))
    mod_part = _ada_fwd(c_all, w_ada[0], b_shard)
    mod_all = _allgather8(mod_part, N_DEV, True, "gather_mod").reshape(N_DEV, N_DEV, ada_cols)
    mod = jnp.concatenate(
        [lax.dynamic_slice(mod_all[2 * k], (dev, 0), (1, ada_cols)) for k in range(N_CHIP)], axis=1)

    cos_t, sin_t = _rope_tables(positions[0])
    small = (norm_w, conv_b, conv_ln_w, conv_ln_b, q_norm_w, kv_norm_w, final_norm_w.reshape(1, D))
    place = jnp.stack([chip, ic]).astype(jnp.int32)
    grad_x, shards, sums = _local_step(x[0], loss_target[0], cos_t, sin_t, mod, weights, small, tiles, place)
    g_w_in_s, g_w_uq_s, g_w_ukv_s, g_wco_s, g_wao_s, g_wo_s = shards

    small_flat = jnp.concatenate([sums[name].reshape(-1) for name, _ in SMALL_ORDER])
    small_flat = jnp.pad(small_flat, (0, SMALL_ROWS * LANE - small_flat.shape[0]))
    small_g = _allgather8(small_flat.reshape(SMALL_ROWS, LANE), SMALL_ROWS, True, "gather_small_grads")
    small_g = small_g.reshape(N_DEV, SMALL_ROWS, LANE)
    small_sum = _sum_slabs(small_g, SMALL_ROWS, "sum_small_grads").reshape(-1)
    tot, pos = {}, 0
    for name, size in SMALL_ORDER:
        tot[name] = small_sum[pos:pos + size]
        pos += size
    loss = (0.5 / D) * jnp.sum(tot["loss"])
    dmod_all = small_g.reshape(N_DEV, -1)[:, 0:3 * D]
    g_b_ada = tot["dmod"].reshape(1, 3 * D)
    dmod_shard = lax.dynamic_slice(dmod_all, (0, chip * ada_cols), (N_DEV, ada_cols))
    g_w_ada = _ada_bwd(c_all.T, dmod_shard).reshape(1, D, ada_cols)
    g_conv_w = lax.dynamic_slice(tot["conv_w"].reshape(HALO, D), (0, chip * (D // N_CHIP)), (KC, D // N_CHIP))
    g_conv_w = g_conv_w.reshape(1, KC, D // N_CHIP)

    def big(w, g, m, v, tr, name):
        d, nm, nv = _adamw(w[0], g, m[0], v[0], tr, name)
        return g[None], d[None], nm[None], nv[None]

    vec_names = ("b_ada", "norm_w", "conv_b", "conv_ln_w", "conv_ln_b", "q_norm_w", "kv_norm_w", "final_norm_w")
    vec_w = (b_ada, norm_w, conv_b, conv_ln_w, conv_ln_b, q_norm_w, kv_norm_w, final_norm_w)
    vec_m = (m_b_ada, m_norm_w, m_conv_b, m_conv_ln_w, m_conv_ln_b, m_q_norm_w, m_kv_norm_w, m_final_norm_w)
    vec_v = (v_b_ada, v_norm_w, v_conv_b, v_conv_ln_w, v_conv_ln_b, v_q_norm_w, v_kv_norm_w, v_final_norm_w)
    vec_g = (g_b_ada, tot["norm_w"], tot["conv_b"], tot["ln_w"], tot["ln_b"], tot["q_norm_w"], tot["kv_norm_w"],
             tot["final_norm_w"])
    vec_g = tuple(g.reshape(w.shape) for g, w in zip(vec_g, vec_w))
    cat = lambda arrs: jnp.concatenate([a.reshape(-1) for a in arrs]).reshape(-1, LANE)
    vd, vnm, vnv = _adamw(cat(vec_w), cat(vec_g), cat(vec_m), cat(vec_v), cat(vec_w).shape[0], "adamw_vectors")

    def split(packed):
        flat, out, pos = packed.reshape(-1), [], 0
        for w in vec_w:
            out.append(flat[pos:pos + w.size].reshape(w.shape))
            pos += w.size
        return out

    res = {}
    for name, g, d, nm, nv in zip(vec_names, vec_g, split(vd), split(vnm), split(vnv)):
        res[name] = (g, d, nm, nv)
    res["w_ada"] = big(w_ada, g_w_ada[0], m_w_ada, v_w_ada, 256, "adamw_w_ada")
    res["w_in"] = big(w_in, g_w_in_s, m_w_in, v_w_in, 256, "adamw_w_in")
    res["conv_w"] = big(conv_w, g_conv_w[0], m_conv_w, v_conv_w, KC, "adamw_conv_w")
    res["w_conv_out"] = big(w_conv_out, g_wco_s, m_w_conv_out, v_w_conv_out, 256, "adamw_w_conv_out")
    res["w_uq"] = big(w_uq, g_w_uq_s, m_w_uq, v_w_uq, 256, "adamw_w_uq")
    res["w_ukv"] = big(w_ukv, g_w_ukv_s, m_w_ukv, v_w_ukv, 256, "adamw_w_ukv")
    res["w_attn_out"] = big(w_attn_out, g_wao_s, m_w_attn_out, v_w_attn_out, 256, "adamw_w_attn_out")
    res["w_out"] = big(w_out, g_wo_s, m_w_out, v_w_out, 256, "adamw_w_out")

    order = ("w_ada", "b_ada", "norm_w", "w_in", "conv_w", "conv_b", "conv_ln_w", "conv_ln_b", "w_conv_out",
             "q_norm_w", "w_uq", "kv_norm_w", "w_ukv", "w_attn_out", "w_out", "final_norm_w")
    outs = [loss, grad_x[None]]
    for slot in range(4):
        outs += [res[name][slot] for name in order]
    return tuple(outs)
```

```python
import functools

import numpy as np
import jax
import jax.numpy as jnp
from jax import lax
from jax.experimental import pallas as pl
from jax.experimental.pallas import tpu as pltpu

F32 = jnp.float32
BF16 = jnp.bfloat16
MESH = pl.DeviceIdType.MESH

D = 1024
H = 8
DN = 128
DR = 64
RQ = 256
KC = 31
HALO = 32
EPS = 1e-6
ROPE_THETA = 10000.0
N_CHIP = 4
N_DEV = 8
LANE = 128
VMEM_BIG = 56 * 1024 * 1024

ADAM_LR = 0.001
ADAM_B1 = 0.9
ADAM_B2 = 0.999
ADAM_EPS = 1e-08
ADAM_WD = 0.01
ADAM_STEP = 10

A_COLS = 3 * D
L_COLS_RAW = RQ + RQ + DR
L_COLS = 640
G_COLS = 3 * D
IN_COLS = A_COLS + L_COLS_RAW + G_COLS


def _params(sem=None, vmem=None):
    kw = {}
    if sem is not None:
        kw["dimension_semantics"] = sem
    if vmem is not None:
        kw["vmem_limit_bytes"] = vmem
    return pltpu.CompilerParams(**kw)


def _dot(a, b):
    return jnp.dot(a, b, preferred_element_type=F32)


def _dot_nt(a, b):
    return lax.dot_general(a, b, (((1,), (1,)), ((), ())), preferred_element_type=F32)


def _dot_tn(a, b):
    return lax.dot_general(a, b, (((0,), (0,)), ((), ())), preferred_element_type=F32)


def _colsum(v):
    return jnp.sum(v, axis=0, keepdims=True)


def _rowmean(v):
    return jnp.mean(v, axis=-1, keepdims=True)


def _sigmoid(v):
    return jax.nn.sigmoid(v)


def _dsilu(v, s):
    return s * (1.0 + v * (1.0 - s))


def _swap_halves(v, first_half):
    return jnp.where(first_half, pltpu.roll(v, 96, 1), pltpu.roll(v, 32, 1))


def _first_half_mask(rows):
    lane = lax.broadcasted_iota(jnp.int32, (rows, LANE), 1)
    return (lane % 64) < 32


def _adaln_norm(x, norm_w, shift, scale, ts):
    s = x.shape[0]

    def body(x_ref, nw_ref, sh_ref, sc_ref, h_ref):
        xv = x_ref[...]
        r = lax.rsqrt(_rowmean(xv * xv) + EPS)
        y = xv * r * nw_ref[...]
        h_ref[...] = (y * (1.0 + sc_ref[...]) + sh_ref[...]).astype(BF16)

    row = pl.BlockSpec((ts, D), lambda i: (i, 0))
    vec = pl.BlockSpec((1, D), lambda i: (0, 0))
    return pl.pallas_call(
        body, grid=(s // ts,), in_specs=[row, vec, vec, vec], out_specs=row,
        out_shape=jax.ShapeDtypeStruct((s, D), BF16), name="adaln_norm",
        compiler_params=_params(("parallel",)))(x, norm_w, shift, scale)


def _mm_nn(a, b, tm, tn, name):
    m, k = a.shape
    n = b.shape[1]

    def body(a_ref, b_ref, o_ref):
        o_ref[...] = _dot(a_ref[...], b_ref[...])

    return pl.pallas_call(
        body, grid=(n // tn, m // tm),
        in_specs=[pl.BlockSpec((tm, k), lambda j, i: (i, 0)), pl.BlockSpec((k, tn), lambda j, i: (0, j))],
        out_specs=pl.BlockSpec((tm, tn), lambda j, i: (i, j)),
        out_shape=jax.ShapeDtypeStruct((m, n), F32), name=name,
        compiler_params=_params(("parallel", "parallel"), VMEM_BIG))(a, b)


def _mm_tn(a, b, tm, tk, tn, name):
    m, k = a.shape
    n = b.shape[1]

    def body(a_ref, b_ref, o_ref):
        @pl.when(pl.program_id(2) == 0)
        def _():
            o_ref[...] = jnp.zeros_like(o_ref)
        o_ref[...] += _dot_tn(a_ref[...], b_ref[...])

    return pl.pallas_call(
        body, grid=(k // tk, n // tn, m // tm),
        in_specs=[pl.BlockSpec((tm, tk), lambda r, j, i: (i, r)), pl.BlockSpec((tm, tn), lambda r, j, i: (i, j))],
        out_specs=pl.BlockSpec((tk, tn), lambda r, j, i: (r, j)),
        out_shape=jax.ShapeDtypeStruct((k, n), F32), name=name,
        compiler_params=_params(("parallel", "parallel", "arbitrary"), VMEM_BIG))(a, b)


def _coords():
    return lax.axis_index("x"), lax.axis_index("y"), lax.axis_index("c")


HBM_REF = pl.BlockSpec(memory_space=pl.ANY)


def _chip_scatter_copies(p_refs, got_refs, send_sems, recv_sems):
    x, y, c = _coords()
    copies = []
    for a in range(len(p_refs)):
        for j, (px, py) in enumerate([(1 - x, y), (x, 1 - y), (1 - x, 1 - y)]):
            copies.append(pltpu.make_async_remote_copy(
                src_ref=p_refs[a].at[2 * px + py], dst_ref=got_refs[a].at[j], send_sem=send_sems.at[3 * a + j],
                recv_sem=recv_sems.at[3 * a + j], device_id=(px, py, c), device_id_type=MESH))
    return copies


def _scatter_alongside(body, n_in, n_out, n_parts, last_step):
    def wrapped(*refs):
        ins, parts = refs[:n_in], refs[n_in:n_in + n_parts]
        rest = refs[n_in + n_parts:]
        outs, got = rest[:n_out], rest[n_out:n_out + n_parts]
        scratch, (send_sems, recv_sems) = rest[n_out + n_parts:-2], rest[-2:]

        @pl.when(pl.program_id(0) == 0)
        def _():
            for cp in _chip_scatter_copies(parts, got, send_sems, recv_sems):
                cp.start()

        body(*ins, *outs, *scratch)

        @pl.when(pl.program_id(0) == last_step)
        def _():
            for cp in _chip_scatter_copies(parts, got, send_sems, recv_sems):
                cp.wait()

    return wrapped


def _scatter_operands(parts):
    n = len(parts)
    shapes = [jax.ShapeDtypeStruct((3,) + a.shape[1:], a.dtype) for a in parts]
    sems = [pltpu.SemaphoreType.DMA((3 * n,)), pltpu.SemaphoreType.DMA((3 * n,))]
    return [HBM_REF] * n, [HBM_REF] * n, shapes, sems


def _shifted_copies(win_ref, sh_ref, rows):
    for p in range(1, 8):
        sh_ref[p - 1, 0:rows, :] = win_ref[pl.ds(p, rows), :]


def _tap_rows(win_ref, sh_ref, start, rows):
    p = start % 8
    if p == 0:
        return win_ref[pl.ds(start, rows), :]
    return sh_ref[p - 1, pl.ds(start - p, rows), :]


def _conv_taps(win_ref, sh_ref, w_ref, rows, chunk, offset_of_tap):
    pieces = []
    for c0 in range(0, rows, chunk):
        acc = None
        for j in range(KC):
            term = w_ref[j:j + 1, :] * _tap_rows(win_ref, sh_ref, c0 + offset_of_tap(j), chunk)
            acc = term if acc is None else acc + term
        pieces.append(acc)
    return pieces


def _conv_fwd(proj_a, conv_w, conv_b, ln_w, ln_b, ts, chunk, shards):
    s = proj_a.shape[0]

    def body(av_ref, al_ref, ag_ref, w_ref, b_ref, lw_ref, lb_ref, u0_ref, u1_ref, za_ref, win_ref, sh_ref):
        @pl.when(pl.program_id(0) == 0)
        def _():
            win_ref[0:HALO, :] = jnp.zeros((HALO, D), F32)

        u0 = av_ref[...] * _sigmoid(al_ref[...])
        u0_ref[...] = u0
        win_ref[HALO:HALO + ts, :] = u0
        _shifted_copies(win_ref, sh_ref, ts + HALO - 8)
        pieces = _conv_taps(win_ref, sh_ref, w_ref, ts, chunk, lambda j: HALO - (KC - 1) + j)
        for n, acc in enumerate(pieces):
            u1_ref[n * chunk:(n + 1) * chunk, :] = acc + b_ref[...]
        win_ref[0:HALO, :] = win_ref[ts:ts + HALO, :]

        u1 = u1_ref[...]
        xc = u1 - _rowmean(u1)
        rstd = lax.rsqrt(_rowmean(xc * xc) + EPS)
        u2 = xc * rstd * lw_ref[...] + lb_ref[...]
        u3 = u2 * _sigmoid(u2)
        ag = ag_ref[...]
        za_ref[...] = (u3 * (ag * _sigmoid(ag))).astype(BF16)

    col = lambda c: pl.BlockSpec((ts, D), lambda i, c=c: (i, c))
    row = pl.BlockSpec((ts, D), lambda i: (i, 0))
    vec = pl.BlockSpec((1, D), lambda i: (0, 0))
    n = len(shards)
    gathered_shapes, sems = _gather_operands(shards)
    outs = pl.pallas_call(
        _gather_alongside(body, 7, 3, n, s // ts - 1), grid=(s // ts,),
        in_specs=[col(0), col(1), col(2), pl.BlockSpec((HALO, D), lambda i: (0, 0)), vec, vec, vec] + [HBM_REF] * n,
        out_specs=[row, row, row] + [HBM_REF] * n,
        out_shape=[jax.ShapeDtypeStruct((s, D), F32), jax.ShapeDtypeStruct((s, D), F32),
                   jax.ShapeDtypeStruct((s, D), BF16)] + gathered_shapes,
        scratch_shapes=[pltpu.VMEM((ts + HALO, D), F32), pltpu.VMEM((7, ts + HALO, D), F32)] + sems,
        name="conv_fwd", compiler_params=_params(("arbitrary",), VMEM_BIG))(
            proj_a, proj_a, proj_a, conv_w, conv_b, ln_w, ln_b, *shards)
    return outs[0], outs[1], outs[2], _as_chip_slabs(outs[3:], shards)


def _conv_bwd(dza, proj_a, u0, u1, conv_w, ln_w, ln_b, ts, chunk, parts):
    s = dza.shape[0]
    nt = s // ts
    per = ts // HALO

    def body(dza_ref, av_ref, al_ref, ag_ref, u0_ref, u0p_ref, u1_ref, w_ref, lw_ref, lb_ref,
             dpa_ref, gw_ref, gv_ref, dwin_ref, uwin_ref, du0_ref, gwp_ref, dsh_ref, ush_ref):
        step = pl.program_id(0)
        tile = nt - 1 - step

        @pl.when(step == 0)
        def _():
            dwin_ref[ts:ts + HALO, :] = jnp.zeros((HALO, D), F32)
            gwp_ref[...] = jnp.zeros_like(gwp_ref)
            gv_ref[...] = jnp.zeros_like(gv_ref)

        ag = ag_ref[...]
        sg = _sigmoid(ag)
        u1 = u1_ref[...]
        xc = u1 - _rowmean(u1)
        rstd = lax.rsqrt(_rowmean(xc * xc) + EPS)
        xh = xc * rstd
        u2 = xh * lw_ref[...] + lb_ref[...]
        s2 = _sigmoid(u2)
        dz = dza_ref[...]
        du3 = dz * (ag * sg)
        dpa_ref[:, 2 * D:3 * D] = (dz * (u2 * s2) * _dsilu(ag, sg)).astype(BF16)
        du2 = du3 * _dsilu(u2, s2)
        gv_ref[0:1, :] += _colsum(du2 * xh)
        gv_ref[1:2, :] += _colsum(du2)
        dxh = du2 * lw_ref[...]
        du1 = rstd * (dxh - _rowmean(dxh) - xh * _rowmean(dxh * xh))
        gv_ref[2:3, :] += _colsum(du1)
        dwin_ref[0:ts, :] = du1

        uwin_ref[0:HALO, :] = jnp.where(tile == 0, 0.0, u0p_ref[...])
        uwin_ref[HALO:HALO + ts, :] = u0_ref[...]

        _shifted_copies(dwin_ref, dsh_ref, ts + HALO - 8)
        _shifted_copies(uwin_ref, ush_ref, ts + HALO - 8)
        pieces = _conv_taps(dwin_ref, dsh_ref, w_ref, ts, chunk, lambda j: (KC - 1) - j)
        for n, acc in enumerate(pieces):
            du0_ref[n * chunk:(n + 1) * chunk, :] = acc
        for c0 in range(0, ts, chunk):
            dchunk = dwin_ref[c0:c0 + chunk, :]
            for j in range(KC):
                prod = dchunk * _tap_rows(uwin_ref, ush_ref, c0 + HALO - (KC - 1) + j, chunk)
                gwp_ref[8 * j:8 * j + 8, :] += jnp.sum(prod.reshape(chunk // 8, 8, D), axis=0)
        dwin_ref[ts:ts + HALO, :] = dwin_ref[0:HALO, :]

        du0 = du0_ref[...]
        al = al_ref[...]
        sl = _sigmoid(al)
        dpa_ref[:, 0:D] = (du0 * sl).astype(BF16)
        dpa_ref[:, D:2 * D] = (du0 * av_ref[...] * sl * (1.0 - sl)).astype(BF16)

        @pl.when(step == nt - 1)
        def _():
            for j in range(KC):
                gw_ref[j:j + 1, :] = _colsum(gwp_ref[8 * j:8 * j + 8, :])
            gw_ref[KC:HALO, :] = jnp.zeros((HALO - KC, D), F32)

    rev = lambda i: nt - 1 - i
    col = lambda c: pl.BlockSpec((ts, D), lambda i, c=c: (rev(i), c))
    row = pl.BlockSpec((ts, D), lambda i: (rev(i), 0))
    vec = pl.BlockSpec((1, D), lambda i: (0, 0))
    halo = pl.BlockSpec((HALO, D), lambda i: (jnp.maximum(rev(i) * per - 1, 0), 0))
    side_in, side_out, side_shapes, side_sems = _scatter_operands(parts)
    outs = pl.pallas_call(
        _scatter_alongside(body, 10, 3, len(parts), nt - 1), grid=(nt,),
        in_specs=[row, col(0), col(1), col(2), row, halo, row, pl.BlockSpec((HALO, D), lambda i: (0, 0)), vec, vec]
        + side_in,
        out_specs=[pl.BlockSpec((ts, A_COLS), lambda i: (rev(i), 0)),
                   pl.BlockSpec((HALO, D), lambda i: (0, 0)), pl.BlockSpec((8, D), lambda i: (0, 0))] + side_out,
        out_shape=[jax.ShapeDtypeStruct((s, A_COLS), BF16), jax.ShapeDtypeStruct((HALO, D), F32),
                   jax.ShapeDtypeStruct((8, D), F32)] + side_shapes,
        scratch_shapes=[pltpu.VMEM((ts + HALO, D), F32), pltpu.VMEM((ts + HALO, D), F32),
                        pltpu.VMEM((ts, D), F32), pltpu.VMEM((8 * HALO, D), F32),
                        pltpu.VMEM((7, ts + HALO, D), F32), pltpu.VMEM((7, ts + HALO, D), F32)] + side_sems,
        name="conv_bwd", compiler_params=_params(("arbitrary",), VMEM_BIG))(
            dza, proj_a, proj_a, proj_a, u0, u0, u1, conv_w, ln_w, ln_b, *parts)
    return outs[0], outs[1], outs[2], list(outs[3:])


def _mla_prep(proj_l, q_norm_w, kv_norm_w, w_uq2, w_ukv, cos_t, sin_t, ts):
    s = proj_l.shape[0]

    def body(pl_ref, qw_ref, kw_ref, wq_ref, wkv_ref, c_ref, s_ref, qn_ref, kvn_ref, q_ref, k_ref, v_ref):
        first = _first_half_mask(ts)
        cs = c_ref[...]
        sn = s_ref[...]

        def rms(v, w):
            return v * lax.rsqrt(_rowmean(v * v) + EPS) * w

        def rope(v):
            return v * cs + _swap_halves(v, first) * sn

        qn = rms(pl_ref[:, 0:RQ], qw_ref[...]).astype(BF16)
        kvn = rms(pl_ref[:, RQ:2 * RQ], kw_ref[...]).astype(BF16)
        qn_ref[...] = qn
        kvn_ref[...] = kvn
        q = _dot(qn, wq_ref[...])
        kv = _dot(kvn, wkv_ref[...])
        kr = rope(pl_ref[:, 2 * RQ:2 * RQ + LANE]).astype(BF16)
        for h in range(H):
            q_ref[h, :, 0:DN] = q[:, DN * h:DN * (h + 1)].astype(BF16)
            q_ref[h, :, DN:2 * DN] = rope(q[:, H * DN + LANE * h:H * DN + LANE * (h + 1)]).astype(BF16)
            k_ref[h, :, 0:DN] = kv[:, 2 * DN * h:2 * DN * h + DN].astype(BF16)
            k_ref[h, :, DN:2 * DN] = kr
            v_ref[h, :, 0:DN] = kv[:, 2 * DN * h + DN:2 * DN * (h + 1)].astype(BF16)
            v_ref[h, :, DN:2 * DN] = jnp.ones((ts, DN), BF16)

    const = lambda shape: pl.BlockSpec(shape, lambda i: (0,) * len(shape))
    rowb = lambda w: pl.BlockSpec((ts, w), lambda i: (i, 0))
    head = lambda w: pl.BlockSpec((H, ts, w), lambda i: (0, i, 0))
    return pl.pallas_call(
        body, grid=(s // ts,),
        in_specs=[rowb(L_COLS), const((1, RQ)), const((1, RQ)), const((RQ, 2 * H * DN)), const((RQ, 2 * H * DN)),
                  rowb(LANE), rowb(LANE)],
        out_specs=[rowb(RQ), rowb(RQ), head(2 * DN), head(2 * DN), head(2 * DN)],
        out_shape=[jax.ShapeDtypeStruct((s, RQ), BF16), jax.ShapeDtypeStruct((s, RQ), BF16),
                   jax.ShapeDtypeStruct((H, s, 2 * DN), BF16), jax.ShapeDtypeStruct((H, s, 2 * DN), BF16),
                   jax.ShapeDtypeStruct((H, s, 2 * DN), BF16)],
        name="mla_prep", compiler_params=_params(("parallel",)))(
            proj_l, q_norm_w, kv_norm_w, w_uq2, w_ukv, cos_t, sin_t)


def _mla_prep_bwd(dq, dk, dv, proj_l, qn, kvn, q_norm_w, kv_norm_w, w_uq2, w_ukv, cos_t, sin_t, ts):
    s = proj_l.shape[0]

    def body(dq_ref, dk_ref, dv_ref, pl_ref, qn_ref, kvn_ref, qw_ref, kw_ref, wq_ref, wkv_ref, c_ref, s_ref,
             dpl_ref, gwq_ref, gwkv_ref, gv_ref, dq2_ref, dkv2_ref):
        @pl.when(pl.program_id(0) == 0)
        def _():
            gwq_ref[...] = jnp.zeros_like(gwq_ref)
            gwkv_ref[...] = jnp.zeros_like(gwkv_ref)
            gv_ref[...] = jnp.zeros_like(gv_ref)

        first = _first_half_mask(ts)
        cs = c_ref[...]
        sn = s_ref[...]

        def rope_bwd(g):
            return g * cs + _swap_halves(g * sn, first)

        def rms_bwd(v, w, dy):
            r = lax.rsqrt(_rowmean(v * v) + EPS)
            vh = v * r
            dvh = dy * w
            return r * (dvh - vh * _rowmean(dvh * vh)), _colsum(dy * vh)

        dkr = None
        for h in range(H):
            dq2_ref[:, DN * h:DN * (h + 1)] = dq_ref[h, :, 0:DN].astype(BF16)
            dq2_ref[:, H * DN + LANE * h:H * DN + LANE * (h + 1)] = rope_bwd(dq_ref[h, :, DN:2 * DN]).astype(BF16)
            dkv2_ref[:, 2 * DN * h:2 * DN * h + DN] = dk_ref[h, :, 0:DN].astype(BF16)
            dkv2_ref[:, 2 * DN * h + DN:2 * DN * (h + 1)] = dv_ref[h].astype(BF16)
            part = dk_ref[h, :, DN:2 * DN]
            dkr = part if dkr is None else dkr + part

        dq2 = dq2_ref[...]
        dkv2 = dkv2_ref[...]
        gwq_ref[...] += _dot_tn(qn_ref[...], dq2)
        gwkv_ref[...] += _dot_tn(kvn_ref[...], dkv2)
        dcq, gq = rms_bwd(pl_ref[:, 0:RQ], qw_ref[...], _dot_nt(dq2, wq_ref[...]))
        dckv, gkv = rms_bwd(pl_ref[:, RQ:2 * RQ], kw_ref[...], _dot_nt(dkv2, wkv_ref[...]))
        gv_ref[0:1, :] += gq
        gv_ref[1:2, :] += gkv
        dpl_ref[:, 0:RQ] = dcq.astype(BF16)
        dpl_ref[:, RQ:2 * RQ] = dckv.astype(BF16)
        dpl_ref[:, 2 * RQ:2 * RQ + LANE] = rope_bwd(dkr).astype(BF16)

    const = lambda shape: pl.BlockSpec(shape, lambda i: (0,) * len(shape))
    rowb = lambda w: pl.BlockSpec((ts, w), lambda i: (i, 0))
    head = lambda w: pl.BlockSpec((H, ts, w), lambda i: (0, i, 0))
    return pl.pallas_call(
        body, grid=(s // ts,),
        in_specs=[head(2 * DN), head(2 * DN), head(DN), rowb(L_COLS), rowb(RQ), rowb(RQ), const((1, RQ)),
                  const((1, RQ)), const((RQ, 2 * H * DN)), const((RQ, 2 * H * DN)), rowb(LANE), rowb(LANE)],
        out_specs=[rowb(L_COLS), const((RQ, 2 * H * DN)), const((RQ, 2 * H * DN)), const((8, RQ))],
        out_shape=[jax.ShapeDtypeStruct((s, L_COLS), BF16), jax.ShapeDtypeStruct((RQ, 2 * H * DN), F32),
                   jax.ShapeDtypeStruct((RQ, 2 * H * DN), F32), jax.ShapeDtypeStruct((8, RQ), F32)],
        scratch_shapes=[pltpu.VMEM((ts, 2 * H * DN), BF16), pltpu.VMEM((ts, 2 * H * DN), BF16)],
        name="mla_prep_bwd", compiler_params=_params(("arbitrary",), VMEM_BIG))(
            dq, dk, dv, proj_l, qn, kvn, q_norm_w, kv_norm_w, w_uq2, w_ukv, cos_t, sin_t)


def _causal_pairs(n, by_key):
    if by_key:
        pairs = [(i, j) for j in range(n) for i in range(j, n)]
    else:
        pairs = [(i, j) for i in range(n) for j in range(i + 1)]
    return (jnp.asarray(np.array([p[0] for p in pairs], np.int32)),
            jnp.asarray(np.array([p[1] for p in pairs], np.int32)))


LOG2E = 1.4426950408889634
LN2 = 0.6931471805599453
ATT_HEADS_FWD = 4
ATT_HEADS = 2
W_IN_ROWS = 336
ATT_ROWS = 64


def _diag_width(r0, t):
    return min(t, -(-(r0 + ATT_ROWS) // LANE) * LANE)


def _diag_mask_rows(r0, width):
    rows = r0 + lax.broadcasted_iota(jnp.int32, (ATT_ROWS, width), 0)
    cols = lax.broadcasted_iota(jnp.int32, (ATT_ROWS, width), 1)
    return cols <= rows


def _diag_mask(t):
    return lax.broadcasted_iota(jnp.int32, (t, t), 1) <= lax.broadcasted_iota(jnp.int32, (t, t), 0)


def _attn_fwd(q, k, v, t):
    s = q.shape[1]
    n = s // t
    scale2 = float((DN + DR) ** -0.5) * LOG2E
    qi, ki = _causal_pairs(n, by_key=False)

    def body(qi_ref, ki_ref, q_ref, k_ref, v_ref, o_ref, lse_ref, *scratch):
        per_head = [scratch[5 * h:5 * h + 5] for h in range(ATT_HEADS_FWD)]
        p = pl.program_id(1)
        i = qi_ref[p]
        j = ki_ref[p]

        @pl.when(j == 0)
        def _():
            for m_sc, acc_sc, _, _, _ in per_head:
                m_sc[...] = jnp.full_like(m_sc, -jnp.inf)
                acc_sc[...] = jnp.zeros_like(acc_sc)

        def scores(h, diag):
            sc = _dot_nt(q_ref[h], k_ref[h])
            if diag:
                sc = jnp.where(_diag_mask(t), sc, -jnp.inf)
            per_head[h][2][...] = sc

        def rowmax(h, rows):
            per_head[h][4][rows, :] = jnp.max(per_head[h][2][rows, :], axis=-1, keepdims=True)

        def stats(h):
            m_sc, acc_sc, _, _, mx_sc = per_head[h]
            m_prev = m_sc[...]
            m_new = jnp.maximum(m_prev, mx_sc[...] * scale2)
            m_sc[...] = m_new
            acc_sc[...] = jnp.exp2(m_prev - m_new) * acc_sc[...]

        def probs(h, rows):
            m_sc, _, s_sc, p_sc, _ = per_head[h]
            p_sc[rows, :] = jnp.exp2(s_sc[rows, :] * scale2 - m_sc[rows, :]).astype(BF16)

        def values(h):
            _, acc_sc, _, p_sc, _ = per_head[h]
            acc_sc[...] += _dot(p_sc[...], v_ref[h])

        def step(diag):
            blocks = [slice(r0, r0 + ATT_ROWS) for r0 in range(0, t, ATT_ROWS)]
            for h in range(ATT_HEADS_FWD):
                scores(h, diag)
            for rows in blocks:
                rowmax(0, rows)
            stats(0)
            for h in range(ATT_HEADS_FWD):
                for rows in blocks:
                    probs(h, rows)
                    if h + 1 < ATT_HEADS_FWD:
                        rowmax(h + 1, rows)
                if h + 1 < ATT_HEADS_FWD:
                    stats(h + 1)
                values(h)

        @pl.when(j < i)
        def _():
            step(False)

        @pl.when(j == i)
        def _():
            step(True)
            for h, (m_sc, acc_sc, _, _, _) in enumerate(per_head):
                l = acc_sc[:, DN:2 * DN]
                o_ref[:, DN * h:DN * (h + 1)] = acc_sc[:, 0:DN] / l
                lse_ref[h] = (m_sc[...] + jnp.log2(l[:, 0:1])) * LN2

    hb = ATT_HEADS_FWD
    grid_spec = pltpu.PrefetchScalarGridSpec(
        num_scalar_prefetch=2, grid=(H // hb, int(qi.shape[0])),
        in_specs=[pl.BlockSpec((hb, t, 2 * DN), lambda h, p, qi, ki: (h, qi[p], 0)),
                  pl.BlockSpec((hb, t, 2 * DN), lambda h, p, qi, ki: (h, ki[p], 0)),
                  pl.BlockSpec((hb, t, 2 * DN), lambda h, p, qi, ki: (h, ki[p], 0))],
        out_specs=[pl.BlockSpec((t, hb * DN), lambda h, p, qi, ki: (qi[p], h)),
                   pl.BlockSpec((hb, t, 1), lambda h, p, qi, ki: (h, qi[p], 0))],
        scratch_shapes=[pltpu.VMEM((t, 1), F32), pltpu.VMEM((t, 2 * DN), F32), pltpu.VMEM((t, t), F32),
                        pltpu.VMEM((t, t), BF16), pltpu.VMEM((t, 1), F32)] * hb)
    return pl.pallas_call(
        body, grid_spec=grid_spec,
        out_shape=[jax.ShapeDtypeStruct((s, H * DN), F32), jax.ShapeDtypeStruct((H, s, 1), F32)],
        name="attn_fwd", compiler_params=_params(("parallel", "arbitrary"), VMEM_BIG))(qi, ki, q, k, v)


def _attn_bwd(q, k, v, do, lse, delta, t):
    s = q.shape[1]
    n = s // t
    scale = float((DN + DR) ** -0.5)
    qi, ki = _causal_pairs(n, by_key=True)

    def body(qi_ref, ki_ref, q_ref, k_ref, v_ref, do_ref, lse_ref, dl_ref, dq_ref, dk_ref, dv_ref,
             dk_sc, dv_sc, s_sc, dp_sc, p_sc, ds_sc):
        p = pl.program_id(1)
        i = qi_ref[p]
        j = ki_ref[p]

        @pl.when(p == 0)
        def _():
            dq_ref[...] = jnp.zeros_like(dq_ref)

        @pl.when(i == j)
        def _():
            dk_sc[...] = jnp.zeros_like(dk_sc)
            dv_sc[...] = jnp.zeros_like(dv_sc)

        def step(diag):
            for h in range(ATT_HEADS):
                s_sc[h] = _dot_nt(q_ref[h], k_ref[h])
                dp_sc[h] = _dot_nt(do_ref[:, DN * h:DN * (h + 1)], v_ref[h, :, 0:DN])
            for h in range(ATT_HEADS):
                for r0 in range(0, t, ATT_ROWS):
                    rows = slice(r0, r0 + ATT_ROWS)
                    width = _diag_width(r0, t) if diag else t
                    sc = s_sc[h, rows, 0:width] * (scale * LOG2E)
                    if diag:
                        sc = jnp.where(_diag_mask_rows(r0, width), sc, -jnp.inf)
                    pr = jnp.exp2(sc - lse_ref[h, rows, :] * LOG2E)
                    ds = pr * (dp_sc[h, rows, 0:width] - dl_ref[h, rows, :]) * scale
                    p_sc[h, rows, 0:width] = pr.astype(BF16)
                    ds_sc[h, rows, 0:width] = ds.astype(BF16)
                    if width < t:
                        p_sc[h, rows, width:t] = jnp.zeros((ATT_ROWS, t - width), BF16)
                        ds_sc[h, rows, width:t] = jnp.zeros((ATT_ROWS, t - width), BF16)
            q_rows = pl.ds(pl.multiple_of(i * t, t), t)
            for h in range(ATT_HEADS):
                dv_sc[h] += _dot_tn(p_sc[h], do_ref[:, DN * h:DN * (h + 1)])
                dk_sc[h] += _dot_tn(ds_sc[h], q_ref[h])
                dq_ref[h, q_rows, :] += _dot(ds_sc[h], k_ref[h])

        @pl.when(i > j)
        def _():
            step(False)

        @pl.when(i == j)
        def _():
            step(True)

        @pl.when(i == n - 1)
        def _():
            dk_ref[...] = dk_sc[...]
            dv_ref[...] = dv_sc[...]

    hb = ATT_HEADS
    grid_spec = pltpu.PrefetchScalarGridSpec(
        num_scalar_prefetch=2, grid=(H // hb, int(qi.shape[0])),
        in_specs=[pl.BlockSpec((hb, t, 2 * DN), lambda h, p, qi, ki: (h, qi[p], 0)),
                  pl.BlockSpec((hb, t, 2 * DN), lambda h, p, qi, ki: (h, ki[p], 0)),
                  pl.BlockSpec((hb, t, 2 * DN), lambda h, p, qi, ki: (h, ki[p], 0)),
                  pl.BlockSpec((t, hb * DN), lambda h, p, qi, ki: (qi[p], h)),
                  pl.BlockSpec((hb, t, 1), lambda h, p, qi, ki: (h, qi[p], 0)),
                  pl.BlockSpec((hb, t, 1), lambda h, p, qi, ki: (h, qi[p], 0))],
        out_specs=[pl.BlockSpec((hb, s, 2 * DN), lambda h, p, qi, ki: (h, 0, 0)),
                   pl.BlockSpec((hb, t, 2 * DN), lambda h, p, qi, ki: (h, ki[p], 0)),
                   pl.BlockSpec((hb, t, DN), lambda h, p, qi, ki: (h, ki[p], 0))],
        scratch_shapes=[pltpu.VMEM((hb, t, 2 * DN), F32), pltpu.VMEM((hb, t, DN), F32),
                        pltpu.VMEM((hb, t, t), F32), pltpu.VMEM((hb, t, t), F32),
                        pltpu.VMEM((hb, t, t), BF16), pltpu.VMEM((hb, t, t), BF16)])
    return pl.pallas_call(
        body, grid_spec=grid_spec,
        out_shape=[jax.ShapeDtypeStruct((H, s, 2 * DN), F32), jax.ShapeDtypeStruct((H, s, 2 * DN), F32),
                   jax.ShapeDtypeStruct((H, s, DN), F32)],
        name="attn_bwd", compiler_params=_params(("parallel", "arbitrary"), VMEM_BIG))(
            qi, ki, q, k, v, do, lse, delta)


def _middle(za, o, proj_g, x, tgt, gate, fnw, wco, wao, wo, ts):
    s = x.shape[0]
    inv_d = 1.0 / D

    def body(za_ref, o_ref, bg_ref, ga_ref, gb_ref, x_ref, t_ref, gate_ref, fnw_ref, wco_ref, wao_ref, wo_ref,
             dx2_ref, dza_ref, do_ref, dl_ref, dpg_ref, zb_ref, mg_ref, dmo_ref, dya_ref, dyb_ref, vec_ref):
        @pl.when(pl.program_id(0) == 0)
        def _():
            vec_ref[...] = jnp.zeros_like(vec_ref)

        ov = o_ref[...]
        bg = bg_ref[...]
        sb = _sigmoid(bg)
        silu_b = bg * sb
        zb = (ov * silu_b).astype(BF16)
        zb_ref[...] = zb
        ya = _dot(za_ref[...], wco_ref[...])
        yb = _dot(zb, wao_ref[...])
        sa = _sigmoid(ga_ref[...])
        sg = _sigmoid(gb_ref[...])
        mg = (sa * ya + sg * yb).astype(BF16)
        mg_ref[...] = mg
        mo = _dot(mg, wo_ref[...])
        gate_v = gate_ref[...]
        x2 = x_ref[...] + gate_v * mo
        r = lax.rsqrt(_rowmean(x2 * x2) + EPS)
        xh = x2 * r
        fw = fnw_ref[...]
        e = xh * fw - t_ref[...]
        vec_ref[2:3, :] += _colsum(e * e)
        dy = e * inv_d
        vec_ref[0:1, :] += _colsum(dy * xh)
        dxh = dy * fw
        dx2 = r * (dxh - xh * _rowmean(dxh * xh))
        dx2_ref[...] = dx2
        vec_ref[1:2, :] += _colsum(dx2 * mo)
        dmo = (gate_v * dx2).astype(BF16)
        dmo_ref[...] = dmo
        dmg = _dot_nt(dmo, wo_ref[...])
        dya = (sa * dmg).astype(BF16)
        dyb = (sg * dmg).astype(BF16)
        dya_ref[...] = dya
        dyb_ref[...] = dyb
        dpg_ref[:, D:2 * D] = (dmg * ya * (sa * (1.0 - sa))).astype(BF16)
        dpg_ref[:, 2 * D:3 * D] = (dmg * yb * (sg * (1.0 - sg))).astype(BF16)
        dza_ref[...] = _dot_nt(dya, wco_ref[...])
        dzb = _dot_nt(dyb, wao_ref[...])
        dov = dzb * silu_b
        do_ref[...] = dov.astype(BF16)
        dpg_ref[:, 0:D] = (dzb * ov * _dsilu(bg, sb)).astype(BF16)
        dprod = dov * ov
        for h in range(H):
            dl_ref[h] = jnp.sum(dprod[:, DN * h:DN * (h + 1)], axis=-1, keepdims=True)

    col = lambda c: pl.BlockSpec((ts, D), lambda i, c=c: (i, c))
    row = pl.BlockSpec((ts, D), lambda i: (i, 0))
    vec = pl.BlockSpec((1, D), lambda i: (0, 0))
    wsp = pl.BlockSpec((D, D), lambda i: (0, 0))
    bf = jax.ShapeDtypeStruct((s, D), BF16)
    ff = jax.ShapeDtypeStruct((s, D), F32)
    return pl.pallas_call(
        body, grid=(s // ts,),
        in_specs=[row, row, col(0), col(1), col(2), row, row, vec, vec, wsp, wsp, wsp],
        out_specs=[row, row, row, pl.BlockSpec((H, ts, 1), lambda i: (0, i, 0)),
                   pl.BlockSpec((ts, G_COLS), lambda i: (i, 0)), row, row, row, row, row,
                   pl.BlockSpec((8, D), lambda i: (0, 0))],
        out_shape=[ff, ff, bf, jax.ShapeDtypeStruct((H, s, 1), F32), jax.ShapeDtypeStruct((s, G_COLS), BF16),
                   bf, bf, bf, bf, bf, jax.ShapeDtypeStruct((8, D), F32)],
        name="middle", compiler_params=_params(("arbitrary",), VMEM_BIG))(
            za, o, proj_g, proj_g, proj_g, x, tgt, gate, fnw, wco, wao, wo)


def _input_bwd(dpa, dpl, dpg, wa, wl, wg, x, dx2, norm_w, scale, ts, parts):
    s = x.shape[0]

    def body(dpa_ref, dpl_ref, dpg_ref, wa_ref, wl_ref, wg_ref, x_ref, dx2_ref, nw_ref, sc_ref, gx_ref, gv_ref):
        @pl.when(pl.program_id(0) == 0)
        def _():
            gv_ref[...] = jnp.zeros_like(gv_ref)

        dh = (_dot_nt(dpa_ref[...], wa_ref[...]) + _dot_nt(dpl_ref[...], wl_ref[...])
              + _dot_nt(dpg_ref[...], wg_ref[...]))
        xv = x_ref[...]
        r = lax.rsqrt(_rowmean(xv * xv) + EPS)
        xh = xv * r
        nw = nw_ref[...]
        gv_ref[0:1, :] += _colsum(dh)
        gv_ref[1:2, :] += _colsum(dh * (xh * nw))
        dy = dh * (1.0 + sc_ref[...])
        gv_ref[2:3, :] += _colsum(dy * xh)
        dxh = dy * nw
        gx_ref[...] = dx2_ref[...] + r * (dxh - xh * _rowmean(dxh * xh))

    const = lambda shape: pl.BlockSpec(shape, lambda i: (0, 0))
    rowb = lambda w: pl.BlockSpec((ts, w), lambda i: (i, 0))
    side_in, side_out, side_shapes, side_sems = _scatter_operands(parts)
    outs = pl.pallas_call(
        _scatter_alongside(body, 10, 2, len(parts), s // ts - 1), grid=(s // ts,),
        in_specs=[rowb(A_COLS), rowb(L_COLS), rowb(G_COLS), const((D, A_COLS)), const((D, L_COLS)),
                  const((D, G_COLS)), rowb(D), rowb(D), const((1, D)), const((1, D))] + side_in,
        out_specs=[rowb(D), const((8, D))] + side_out,
        out_shape=[jax.ShapeDtypeStruct((s, D), F32), jax.ShapeDtypeStruct((8, D), F32)] + side_shapes,
        scratch_shapes=side_sems,
        name="input_bwd", compiler_params=_params(("arbitrary",), VMEM_BIG))(
            dpa, dpl, dpg, wa, wl, wg, x, dx2, norm_w, scale, *parts)
    return outs[0], outs[1], list(outs[2:])


def _adamw(w, g, m, v, tr, name):
    lead, (rows, cols) = w.shape[:-2], w.shape[-2:]
    c1 = 1.0 - ADAM_B1 ** ADAM_STEP
    c2 = 1.0 - ADAM_B2 ** ADAM_STEP

    def body(w_ref, g_ref, m_ref, v_ref, d_ref, nm_ref, nv_ref):
        gv = g_ref[...]
        nm = ADAM_B1 * m_ref[...] + (1.0 - ADAM_B1) * gv
        nv = ADAM_B2 * v_ref[...] + (1.0 - ADAM_B2) * (gv * gv)
        nm_ref[...] = nm
        nv_ref[...] = nv
        d_ref[...] = -ADAM_LR * ((nm / c1) / (jnp.sqrt(nv / c2) + ADAM_EPS) + ADAM_WD * w_ref[...])

    blk = pl.BlockSpec((1,) * len(lead) + (tr, cols), lambda i: (0,) * len(lead) + (i, 0))
    shp = jax.ShapeDtypeStruct(w.shape, F32)
    return pl.pallas_call(
        body, grid=(rows // tr,), in_specs=[blk] * 4, out_specs=[blk] * 3, out_shape=[shp] * 3, name=name,
        compiler_params=_params(("parallel",), VMEM_BIG))(w, g.reshape(w.shape), m, v)


def _ada_fwd(c_all, w_ada_shard, b_ada_shard):
    def body(c_ref, w_ref, b_ref, o_ref):
        cv = c_ref[...]
        o_ref[...] = jnp.dot(cv * _sigmoid(cv), w_ref[...], preferred_element_type=F32,
                             precision=lax.Precision.HIGHEST) + b_ref[...]

    return pl.pallas_call(
        body, out_shape=jax.ShapeDtypeStruct((N_DEV, w_ada_shard.shape[1]), F32), name="ada_fwd")(
            c_all, w_ada_shard, b_ada_shard)


def _ada_bwd(c_all_t, dmod_shard):
    def body(c_ref, d_ref, o_ref):
        cv = c_ref[...]
        o_ref[...] = jnp.dot(cv * _sigmoid(cv), d_ref[...], preferred_element_type=F32,
                             precision=lax.Precision.HIGHEST)

    return pl.pallas_call(
        body, out_shape=jax.ShapeDtypeStruct((D, dmod_shard.shape[1]), F32), name="ada_bwd")(c_all_t, dmod_shard)


def _sum_slabs(stack, tr, name):
    n, rows, cols = stack.shape

    def body(s_ref, o_ref):
        acc = s_ref[0]
        for k in range(1, n):
            acc = acc + s_ref[k]
        o_ref[...] = acc

    return pl.pallas_call(
        body, grid=(rows // tr,), in_specs=[pl.BlockSpec((n, tr, cols), lambda i: (0, i, 0))],
        out_specs=pl.BlockSpec((tr, cols), lambda i: (i, 0)), out_shape=jax.ShapeDtypeStruct((rows, cols), F32),
        name=name, compiler_params=_params(("parallel",)))(stack)


def _sum_chip_slabs(arrived, part, place, tr, name, axis):
    n, rows, cols = arrived.shape
    per = rows // tr

    def body(place_ref, a_ref, p_ref, o_ref):
        acc = p_ref[0].astype(F32)
        for k in range(n):
            acc = acc + a_ref[k].astype(F32)
        o_ref[...] = acc

    if axis == 1:
        whole, out_map = (2 * rows, cols), lambda i, pc: (pc[1] * per + i, 0)
    else:
        whole, out_map = (rows, 2 * cols), lambda i, pc: (i, pc[1])
    grid_spec = pltpu.PrefetchScalarGridSpec(
        num_scalar_prefetch=1, grid=(per,),
        in_specs=[pl.BlockSpec((n, tr, cols), lambda i, pc: (0, i, 0)),
                  pl.BlockSpec((1, tr, cols), lambda i, pc: (pc[0], i, 0))],
        out_specs=pl.BlockSpec((tr, cols), out_map))
    return pl.pallas_call(
        body, grid_spec=grid_spec, out_shape=jax.ShapeDtypeStruct(whole, F32), name=name,
        compiler_params=_params(("parallel",)))(place, arrived, part)


def _add_own_half(full, other, core, tr, name, axis):
    n, rows, cols = other.shape
    per = rows // tr

    def body(c_ref, f_ref, o_ref, out_ref):
        out_ref[...] = (f_ref[...] + o_ref[...]).astype(BF16)

    full_map = (lambda k, i, c: (k, c[0] * per + i, 0)) if axis == 1 else (lambda k, i, c: (k, i, c[0]))
    grid_spec = pltpu.PrefetchScalarGridSpec(
        num_scalar_prefetch=1, grid=(n, per),
        in_specs=[pl.BlockSpec((1, tr, cols), full_map),
                  pl.BlockSpec((1, tr, cols), lambda k, i, c: (k, i, 0))],
        out_specs=pl.BlockSpec((1, tr, cols), lambda k, i, c: (k, i, 0)))
    return pl.pallas_call(
        body, grid_spec=grid_spec, out_shape=jax.ShapeDtypeStruct((n, rows, cols), BF16), name=name,
        compiler_params=_params(("parallel", "parallel")))(core, full, other)


def _allgather8(block, src_rows, vmem, name):
    n = block.shape[1]
    m = src_rows
    sliced = block.shape[0] != m

    def body(x_ref, out_ref, send_sems, recv_sems, local_sem):
        x, y, c = _coords()
        me, sibling = (x, y, c), (x, y, 1 - c)
        chips = [(1 - x, y), (x, 1 - y), (1 - x, 1 - y)]
        src = x_ref.at[pl.ds(pl.multiple_of(c * m, 16), m), :] if sliced else x_ref

        def rows(px, py, pc):
            return out_ref.at[pl.ds(pl.multiple_of((4 * px + 2 * py + pc) * m, 8), m), :]

        def copy(k, blk, to, source=None):
            return pltpu.make_async_remote_copy(
                src_ref=rows(*blk) if source is None else source, dst_ref=rows(*blk),
                send_sem=send_sems.at[k], recv_sem=recv_sems.at[k], device_id=to, device_id_type=MESH)

        mine = pltpu.make_async_copy(src, rows(*me), local_sem)
        mine.start()
        first = [copy(0, me, sibling, source=src)]
        first += [copy(1 + j, me, (*chip, c), source=src) for j, chip in enumerate(chips)]
        for cp in first:
            cp.start()
        passed = [copy(4 + j, (*chip, c), sibling) for j, chip in enumerate(chips)]
        for j, chip in enumerate(chips):
            copy(1 + j, (*chip, c), me).wait_recv()
            passed[j].start()
        copy(0, sibling, me).wait_recv()
        for j, chip in enumerate(chips):
            copy(4 + j, (*chip, 1 - c), me).wait_recv()
        for cp in first + passed:
            cp.wait_send()
        mine.wait()

    space = pltpu.VMEM if vmem else pl.ANY
    return pl.pallas_call(
        body, out_shape=jax.ShapeDtypeStruct((N_DEV * m, n), block.dtype),
        in_specs=[pl.BlockSpec(memory_space=space)], out_specs=pl.BlockSpec(memory_space=space),
        scratch_shapes=[pltpu.SemaphoreType.DMA((7,)), pltpu.SemaphoreType.DMA((7,)), pltpu.SemaphoreType.DMA],
        name=name)(block)


def _gather_plan(x_refs, out_refs, send_sems, recv_sems, local_sems):
    n = len(x_refs)
    halves = [r.shape[0] // 2 for r in x_refs]
    x, y, c = _coords()
    me, sibling = (x, y, c), (x, y, 1 - c)
    chips = [(1 - x, y), (x, 1 - y), (1 - x, 1 - y)]

    def src(a):
        return x_refs[a].at[pl.ds(pl.multiple_of(c * halves[a], 16), halves[a]), :]

    def blk(a, px, py, pc):
        return out_refs[a].at[4 * px + 2 * py + pc]

    def copy(a, k, who, to, source=None):
        return pltpu.make_async_remote_copy(
            src_ref=blk(a, *who) if source is None else source, dst_ref=blk(a, *who),
            send_sem=send_sems.at[7 * a + k], recv_sem=recv_sems.at[7 * a + k], device_id=to, device_id_type=MESH)

    def mine(a):
        return pltpu.make_async_copy(src(a), blk(a, *me), local_sems.at[a])

    def first(a):
        return ([copy(a, 0, me, sibling, source=src(a))]
                + [copy(a, 1 + j, me, (*chip, c), source=src(a)) for j, chip in enumerate(chips)])

    def begin():
        for a in range(n):
            mine(a).start()
        for a in range(n):
            for cp in first(a):
                cp.start()

    def finish():
        onward = []
        for j, chip in enumerate(chips):
            for a in range(n):
                copy(a, 1 + j, (*chip, c), me).wait_recv()
                onward.append(copy(a, 4 + j, (*chip, c), sibling))
                onward[-1].start()
        for a in range(n):
            copy(a, 0, sibling, me).wait_recv()
        for j, chip in enumerate(chips):
            for a in range(n):
                copy(a, 4 + j, (*chip, 1 - c), me).wait_recv()
        for a in range(n):
            for cp in first(a):
                cp.wait_send()
        for cp in onward:
            cp.wait_send()
        for a in range(n):
            mine(a).wait()

    return begin, finish


def _gather_operands(shards):
    n = len(shards)
    shapes = [jax.ShapeDtypeStruct((N_DEV, a.shape[0] // 2, a.shape[1]), a.dtype) for a in shards]
    sems = [pltpu.SemaphoreType.DMA((7 * n,)), pltpu.SemaphoreType.DMA((7 * n,)), pltpu.SemaphoreType.DMA((n,))]
    return shapes, sems


def _as_chip_slabs(gathered, shards):
    return [o.reshape(N_CHIP, a.shape[0], a.shape[1]) for o, a in zip(gathered, shards)]


def _gather_weights(shards):
    n = len(shards)

    def body(*refs):
        begin, finish = _gather_plan(refs[:n], refs[n:2 * n], *refs[2 * n:])
        begin()
        finish()

    shapes, sems = _gather_operands(shards)
    outs = pl.pallas_call(
        body, out_shape=shapes, in_specs=[HBM_REF] * n, out_specs=[HBM_REF] * n, scratch_shapes=sems,
        name="gather_weights")(*shards)
    return _as_chip_slabs(outs, shards)


def _gather_alongside(body, n_in, n_out, n_shards, last_step):
    def wrapped(*refs):
        ins, shards = refs[:n_in], refs[n_in:n_in + n_shards]
        rest = refs[n_in + n_shards:]
        outs, gathered = rest[:n_out], rest[n_out:n_out + n_shards]
        scratch, sems = rest[n_out + n_shards:-3], rest[-3:]

        @pl.when(pl.program_id(0) == 0)
        def _():
            _gather_plan(shards, gathered, *sems)[0]()

        body(*ins, *outs, *scratch)

        @pl.when(pl.program_id(0) == last_step)
        def _():
            _gather_plan(shards, gathered, *sems)[1]()

    return wrapped


def _half(ref, axis, which, ndim):
    size = ref.shape[axis] // 2
    idx = [slice(None)] * ndim
    idx[axis] = pl.ds(pl.multiple_of(which * size, 8 if axis == ndim - 2 else LANE), size)
    return ref.at[tuple(idx)]


def _swap_halves_with_sibling(fulls, name, axes):
    n = len(fulls)

    def body(*refs):
        f_refs, got_refs = refs[:n], refs[n:2 * n]
        send_sems, recv_sems = refs[2 * n:]
        x, y, c = _coords()
        copies = []
        for a in range(n):
            copies.append(pltpu.make_async_remote_copy(
                src_ref=_half(f_refs[a], axes[a], 1 - c, 3), dst_ref=got_refs[a], send_sem=send_sems.at[a],
                recv_sem=recv_sems.at[a], device_id=(x, y, 1 - c), device_id_type=MESH))
        for cp in copies:
            cp.start()
        for cp in copies:
            cp.wait()

    def halved(a, axis):
        shape = list(a.shape)
        shape[axis] //= 2
        return jax.ShapeDtypeStruct(tuple(shape), a.dtype)

    return pl.pallas_call(
        body, out_shape=[halved(a, ax) for a, ax in zip(fulls, axes)],
        in_specs=[HBM_REF] * n, out_specs=[HBM_REF] * n,
        scratch_shapes=[pltpu.SemaphoreType.DMA((n,)), pltpu.SemaphoreType.DMA((n,))],
        name=name)(*fulls)


def _join_halves_with_sibling(wholes, axes):
    n = len(wholes)

    def body(*refs):
        out_refs = refs[n:2 * n]
        send_sems, recv_sems = refs[2 * n:]
        x, y, c = _coords()

        def push(a, core):
            half = _half(out_refs[a], axes[a] - 1, core, 2)
            return pltpu.make_async_remote_copy(
                src_ref=half, dst_ref=half, send_sem=send_sems.at[a], recv_sem=recv_sems.at[a],
                device_id=(x, y, 1 - c), device_id_type=MESH)

        for a in range(n):
            push(a, c).start()
        for a in range(n):
            push(a, 1 - c).wait_recv()
        for a in range(n):
            push(a, c).wait_send()

    return pl.pallas_call(
        body, out_shape=[jax.ShapeDtypeStruct(a.shape, a.dtype) for a in wholes],
        in_specs=[HBM_REF] * n, out_specs=[HBM_REF] * n, input_output_aliases={a: a for a in range(n)},
        scratch_shapes=[pltpu.SemaphoreType.DMA((n,)), pltpu.SemaphoreType.DMA((n,))],
        name="rs_pair_join")(*wholes)


def _cols_to_slabs(g):
    rows, cols = g.shape
    return g.reshape(rows, N_CHIP, cols // N_CHIP).transpose(1, 0, 2)


def _slabs_to_cols(w):
    n, rows, cols = w.shape
    return w.transpose(1, 0, 2).reshape(rows, n * cols)


def _col_window(slabs, start, stop):
    n = slabs.shape[2]
    pieces = []
    for k in range(N_CHIP):
        lo, hi = max(start, k * n), min(stop, (k + 1) * n)
        if lo < hi:
            pieces.append(slabs[k][:, lo - k * n:hi - k * n])
    return pieces[0] if len(pieces) == 1 else jnp.concatenate(pieces, axis=1)


def _slabs_from_groups(groups, n):
    slabs = []
    for k in range(N_CHIP):
        pieces, off = [], 0
        for g in groups:
            lo, hi = max(k * n, off), min((k + 1) * n, off + g.shape[0])
            if lo < hi:
                pieces.append(g[lo - off:hi - off])
            off += g.shape[0]
        slabs.append(pieces[0] if len(pieces) == 1 else jnp.concatenate(pieces, axis=0))
    return jnp.stack(slabs)


def _uq_to_padded(w_uq):
    per = w_uq.reshape(RQ, H, DN + DR)
    nope = per[:, :, :DN].reshape(RQ, H * DN)
    rope = jnp.pad(per[:, :, DN:], ((0, 0), (0, 0), (0, LANE - DR))).reshape(RQ, H * LANE)
    return jnp.concatenate([nope, rope], axis=1)


def _uq_from_padded(g):
    nope = g[:, :H * DN].reshape(RQ, H, DN)
    rope = g[:, H * DN:].reshape(RQ, H, LANE)[:, :, :DR]
    return jnp.concatenate([nope, rope], axis=2).reshape(RQ, H * (DN + DR))


def _rope_tables(positions):
    inv_freq = ROPE_THETA ** (-jnp.arange(0, DR, 2, dtype=F32) / DR)
    ang = positions.astype(F32)[:, None] * inv_freq
    cos, sin = jnp.cos(ang), jnp.sin(ang)
    return jnp.tile(cos, (1, 4)), jnp.tile(jnp.concatenate([-sin, sin], axis=1), (1, 2))


def _pair_sums(fulls, core, tag, axes, tr):
    from_sibling = _swap_halves_with_sibling(fulls, f"rs_pair_swap_{tag}", axes)
    return [_add_own_half(f, o, core, min(tr, o.shape[1]), f"add_own_half_{tag}{n}", ax)
            for n, (f, o, ax) in enumerate(zip(fulls, from_sibling, axes))]


def _local_step(x, tgt, cos_t, sin_t, mod, weights, small, tiles, place):
    ts, ts_in, tm_nn, tm_tn, t_attn, chunk = tiles
    wa, wl, wg, later_shards, conv_w = weights
    norm_w, conv_b, ln_w, ln_b, q_norm_w, kv_norm_w, fnw = small
    shift, scale, gate = mod[:, 0:D], mod[:, D:2 * D], mod[:, 2 * D:3 * D]

    h = _adaln_norm(x, norm_w, shift, scale, ts)
    proj_a = _mm_nn(h, wa, tm_nn, D, "proj_a")
    u0, u1, za, (g_uq, g_ukv, g_co, g_ao, g_o) = _conv_fwd(proj_a, conv_w, conv_b, ln_w, ln_b, ts, chunk, later_shards)
    w_uq2, w_ukv = _uq_to_padded(_slabs_to_cols(g_uq)), _slabs_to_cols(g_ukv)
    wco, wao, wo = g_co.reshape(D, D), g_ao.reshape(D, D), g_o.reshape(D, D)
    proj_l = _mm_nn(h, wl, tm_nn, L_COLS, "proj_l")
    proj_g = _mm_nn(h, wg, tm_nn, D, "proj_g")
    qn, kvn, q, k, v = _mla_prep(proj_l, q_norm_w, kv_norm_w, w_uq2, w_ukv, cos_t, sin_t, ts)
    o, lse = _attn_fwd(q, k, v, t_attn)
    (dx2, dza, do, delta, dpg, zb, mg, dmo, dya, dyb, vec_mid) = _middle(
        za, o, proj_g, x, tgt, gate, fnw, wco, wao, wo, ts)
    g_wo = _mm_tn(mg, dmo, tm_tn, D, D, "grad_w_out")
    g_wco = _mm_tn(za, dya, tm_tn, D, D, "grad_w_conv_out")
    g_wao = _mm_tn(zb, dyb, tm_tn, D, D, "grad_w_attn_out")
    dq, dk, dv = _attn_bwd(q, k, v, do, lse, delta, t_attn)
    dpl, g_wuq2, g_wukv, vec_mla = _mla_prep_bwd(
        dq, dk, dv, proj_l, qn, kvn, q_norm_w, kv_norm_w, w_uq2, w_ukv, cos_t, sin_t, ts)

    core = place[1:2]
    nr = D // N_CHIP
    early = [_cols_to_slabs(_uq_from_padded(g_wuq2)), _cols_to_slabs(g_wukv), g_wco.reshape(N_CHIP, nr, D),
             g_wao.reshape(N_CHIP, nr, D), g_wo.reshape(N_CHIP, nr, D)]
    early_sums = _pair_sums(early, core, "a", [1] * len(early), 256)
    dpa, g_conv_w, vec_conv, early_got = _conv_bwd(dza, proj_a, u0, u1, conv_w, ln_w, ln_b, ts, chunk, early_sums)

    g_wa_t = _mm_tn(dpa, h, tm_tn, D, D, "grad_w_in_a")
    g_wl_t = _mm_tn(dpl, h, tm_tn, L_COLS, D, "grad_w_in_l")
    g_wg_t = _mm_tn(dpg, h, tm_tn, D, D, "grad_w_in_g")
    g_w_in_slabs = _slabs_from_groups([g_wa_t, g_wl_t[0:L_COLS_RAW], g_wg_t], IN_COLS // N_CHIP)
    late_sums = _pair_sums([g_w_in_slabs], core, "b", [2], W_IN_ROWS)
    grad_x, vec_in, late_got = _input_bwd(dpa, dpl, dpg, wa, wl, wg, x, dx2, norm_w, scale, ts_in, late_sums)

    axes = [2] + [1] * len(early)
    wholes = [_sum_chip_slabs(a, p, place, min(W_IN_ROWS if ax == 2 else 128, a.shape[1]), f"sum_chip_slabs_{n}", ax)
              for n, (a, p, ax) in enumerate(zip(late_got + early_got, late_sums + early_sums, axes))]
    shards = _join_halves_with_sibling(wholes, axes)

    dmod = jnp.concatenate([vec_in[0:1], vec_in[1:2], vec_mid[1:2]], axis=1)
    sums = dict(dmod=dmod, norm_w=vec_in[2:3], conv_b=vec_conv[2:3], ln_w=vec_conv[0:1], ln_b=vec_conv[1:2],
                q_norm_w=vec_mla[0:1], kv_norm_w=vec_mla[1:2], final_norm_w=vec_mid[0:1], loss=vec_mid[2:3],
                conv_w=g_conv_w)
    return grad_x, shards, sums


SMALL_ORDER = (("dmod", 3 * D), ("norm_w", D), ("conv_b", D), ("ln_w", D), ("ln_b", D), ("q_norm_w", RQ),
               ("kv_norm_w", RQ), ("final_norm_w", D), ("loss", D), ("conv_w", HALO * D))
SMALL_ROWS = 336


def kernel(x, c, positions, w_ada, b_ada, norm_w, w_in, conv_w, conv_b, conv_ln_w, conv_ln_b, w_conv_out, q_norm_w, w_uq, kv_norm_w, w_ukv, w_attn_out, w_out, final_norm_w, loss_target, m_w_ada, m_b_ada, m_norm_w, m_w_in, m_conv_w, m_conv_b, m_conv_ln_w, m_conv_ln_b, m_w_conv_out, m_q_norm_w, m_w_uq, m_kv_norm_w, m_w_ukv, m_w_attn_out, m_w_out, m_final_norm_w, v_w_ada, v_b_ada, v_norm_w, v_w_in, v_conv_w, v_conv_b, v_conv_ln_w, v_conv_ln_b, v_w_conv_out, v_q_norm_w, v_w_uq, v_kv_norm_w, v_w_ukv, v_w_attn_out, v_w_out, v_final_norm_w):
    ix, iy, ic = _coords()
    chip = 2 * ix + iy
    dev = 4 * ix + 2 * iy + ic
    s = x.shape[1]
    tiles = (256, 512, 1024, 2048, 512, 32)

    conv_w_pad = jnp.pad(conv_w[0], ((0, HALO - KC), (0, 0)))
    small_in = jnp.concatenate([c.reshape(8, LANE), conv_w_pad.reshape(64, LANE)], axis=0)
    small_all = _allgather8(small_in, 72, True, "gather_c_conv").reshape(N_DEV, 72, LANE)
    c_all = small_all[:, 0:8].reshape(N_DEV, D)
    conv_full = jnp.concatenate(
        [small_all[2 * k, 8:72].reshape(HALO, D // N_CHIP) for k in range(N_CHIP)], axis=1)

    (g_in,) = _gather_weights([w_in[0].astype(BF16)])
    wa = _col_window(g_in, 0, A_COLS)
    wl = jnp.pad(_col_window(g_in, A_COLS, A_COLS + L_COLS_RAW), ((0, 0), (0, L_COLS - L_COLS_RAW)))
    wg = _col_window(g_in, A_COLS + L_COLS_RAW, IN_COLS)
    later_shards = [w[0].astype(BF16) for w in (w_uq, w_ukv, w_conv_out, w_attn_out, w_out)]
    weights = (wa, wl, wg, later_shards, conv_full)

    ada_cols = w_ada.shape[2]
    b_shard = lax.dynamic_slice(b_ada, (0, chip * ada_cols), (1, ada_cols))
    mod_part = _ada_fwd(c_all, w_ada[0], b_shard)
    mod_all = _allgather8(mod_part, N_DEV, True, "gather_mod").reshape(N_DEV, N_DEV, ada_cols)
    mod = jnp.concatenate(
        [lax.dynamic_slice(mod_all[2 * k], (dev, 0), (1, ada_cols)) for k in range(N_CHIP)], axis=1)

    cos_t, sin_t = _rope_tables(positions[0])
    small = (norm_w, conv_b, conv_ln_w, conv_ln_b, q_norm_w, kv_norm_w, final_norm_w.reshape(1, D))
    place = jnp.stack([chip, ic]).astype(jnp.int32)
    grad_x, shards, sums = _local_step(x[0], loss_target[0], cos_t, sin_t, mod, weights, small, tiles, place)
    g_w_in_s, g_w_uq_s, g_w_ukv_s, g_wco_s, g_wao_s, g_wo_s = shards

    small_flat = jnp.concatenate([sums[name].reshape(-1) for name, _ in SMALL_ORDER])
    small_flat = jnp.pad(small_flat, (0, SMALL_ROWS * LANE - small_flat.shape[0]))
    small_g = _allgather8(small_flat.reshape(SMALL_ROWS, LANE), SMALL_ROWS, True, "gather_small_grads")
    small_g = small_g.reshape(N_DEV, SMALL_ROWS, LANE)
    small_sum = _sum_slabs(small_g, SMALL_ROWS, "sum_small_grads").reshape(-1)
    tot, pos = {}, 0
    for name, size in SMALL_ORDER:
        tot[name] = small_sum[pos:pos + size]
        pos += size
    loss = (0.5 / D) * jnp.sum(tot["loss"])
    dmod_all = small_g.reshape(N_DEV, -1)[:, 0:3 * D]
    g_b_ada = tot["dmod"].reshape(1, 3 * D)
    dmod_shard = lax.dynamic_slice(dmod_all, (0, chip * ada_cols), (N_DEV, ada_cols))
    g_w_ada = _ada_bwd(c_all.T, dmod_shard).reshape(1, D, ada_cols)
    g_conv_w = lax.dynamic_slice(tot["conv_w"].reshape(HALO, D), (0, chip * (D // N_CHIP)), (KC, D // N_CHIP))
    g_conv_w = g_conv_w.reshape(1, KC, D // N_CHIP)

    def big(w, g, m, v, tr, name):
        d, nm, nv = _adamw(w, g, m, v, tr, name)
        return g.reshape(w.shape), d, nm, nv

    vec_names = ("b_ada", "norm_w", "conv_b", "conv_ln_w", "conv_ln_b", "q_norm_w", "kv_norm_w", "final_norm_w")
    vec_w = (b_ada, norm_w, conv_b, conv_ln_w, conv_ln_b, q_norm_w, kv_norm_w, final_norm_w)
    vec_m = (m_b_ada, m_norm_w, m_conv_b, m_conv_ln_w, m_conv_ln_b, m_q_norm_w, m_kv_norm_w, m_final_norm_w)
    vec_v = (v_b_ada, v_norm_w, v_conv_b, v_conv_ln_w, v_conv_ln_b, v_q_norm_w, v_kv_norm_w, v_final_norm_w)
    vec_g = (g_b_ada, tot["norm_w"], tot["conv_b"], tot["ln_w"], tot["ln_b"], tot["q_norm_w"], tot["kv_norm_w"],
             tot["final_norm_w"])
    vec_g = tuple(g.reshape(w.shape) for g, w in zip(vec_g, vec_w))
    cat = lambda arrs: jnp.concatenate([a.reshape(-1) for a in arrs]).reshape(-1, LANE)
    vd, vnm, vnv = _adamw(cat(vec_w), cat(vec_g), cat(vec_m), cat(vec_v), cat(vec_w).shape[0], "adamw_vectors")

    def split(packed):
        flat, out, pos = packed.reshape(-1), [], 0
        for w in vec_w:
            out.append(flat[pos:pos + w.size].reshape(w.shape))
            pos += w.size
        return out

    res = {}
    for name, g, d, nm, nv in zip(vec_names, vec_g, split(vd), split(vnm), split(vnv)):
        res[name] = (g, d, nm, nv)
    res["w_ada"] = big(w_ada, g_w_ada[0], m_w_ada, v_w_ada, 256, "adamw_w_ada")
    t_in = [a[0].T for a in (w_in, m_w_in, v_w_in)]
    d_t, nm_t, nv_t = _adamw(t_in[0], g_w_in_s, t_in[1], t_in[2], W_IN_ROWS, "adamw_w_in")
    res["w_in"] = tuple(a.T[None] for a in (g_w_in_s, d_t, nm_t, nv_t))
    res["conv_w"] = big(conv_w, g_conv_w[0], m_conv_w, v_conv_w, KC, "adamw_conv_w")
    res["w_conv_out"] = big(w_conv_out, g_wco_s, m_w_conv_out, v_w_conv_out, 256, "adamw_w_conv_out")
    res["w_uq"] = big(w_uq, g_w_uq_s, m_w_uq, v_w_uq, 256, "adamw_w_uq")
    res["w_ukv"] = big(w_ukv, g_w_ukv_s, m_w_ukv, v_w_ukv, 256, "adamw_w_ukv")
    res["w_attn_out"] = big(w_attn_out, g_wao_s, m_w_attn_out, v_w_attn_out, 256, "adamw_w_attn_out")
    res["w_out"] = big(w_out, g_wo_s, m_w_out, v_w_out, 256, "adamw_w_out")

    order = ("w_ada", "b_ada", "norm_w", "w_in", "conv_w", "conv_b", "conv_ln_w", "conv_ln_b", "w_conv_out",
             "q_norm_w", "w_uq", "kv_norm_w", "w_ukv", "w_attn_out", "w_out", "final_norm_w")
    outs = [loss, grad_x[None]]
    for slot in range(4):
        outs += [res[name][slot] for name in order]
    return tuple(outs)
```

```python
import functools

import numpy as np
import jax
import jax.numpy as jnp
from jax import lax
from jax.experimental import pallas as pl
from jax.experimental.pallas import tpu as pltpu

F32 = jnp.float32
BF16 = jnp.bfloat16
MESH = pl.DeviceIdType.MESH

D = 1024
H = 8
DN = 128
DR = 64
RQ = 256
KC = 31
HALO = 32
EPS = 1e-6
ROPE_THETA = 10000.0
N_CHIP = 4
N_DEV = 8
LANE = 128
VMEM_BIG = 56 * 1024 * 1024

ADAM_LR = 0.001
ADAM_B1 = 0.9
ADAM_B2 = 0.999
ADAM_EPS = 1e-08
ADAM_WD = 0.01
ADAM_STEP = 10

A_COLS = 3 * D
L_COLS_RAW = RQ + RQ + DR
L_COLS = 640
G_COLS = 3 * D
IN_COLS = A_COLS + L_COLS_RAW + G_COLS


def _params(sem=None, vmem=None):
    kw = {}
    if sem is not None:
        kw["dimension_semantics"] = sem
    if vmem is not None:
        kw["vmem_limit_bytes"] = vmem
    return pltpu.CompilerParams(**kw)


def _dot(a, b):
    return jnp.dot(a, b, preferred_element_type=F32)


def _dot_nt(a, b):
    return lax.dot_general(a, b, (((1,), (1,)), ((), ())), preferred_element_type=F32)


def _dot_tn(a, b):
    return lax.dot_general(a, b, (((0,), (0,)), ((), ())), preferred_element_type=F32)


def _colsum(v):
    return jnp.sum(v, axis=0, keepdims=True)


def _rowmean(v):
    return jnp.mean(v, axis=-1, keepdims=True)


def _sigmoid(v):
    return jax.nn.sigmoid(v)


def _dsilu(v, s):
    return s * (1.0 + v * (1.0 - s))


def _swap_halves(v, first_half):
    return jnp.where(first_half, pltpu.roll(v, 96, 1), pltpu.roll(v, 32, 1))


def _first_half_mask(rows):
    lane = lax.broadcasted_iota(jnp.int32, (rows, LANE), 1)
    return (lane % 64) < 32


def _adaln_norm(x, norm_w, shift, scale, ts):
    s = x.shape[0]

    def body(x_ref, nw_ref, sh_ref, sc_ref, h_ref):
        xv = x_ref[...]
        r = lax.rsqrt(_rowmean(xv * xv) + EPS)
        y = xv * r * nw_ref[...]
        h_ref[...] = (y * (1.0 + sc_ref[...]) + sh_ref[...]).astype(BF16)

    row = pl.BlockSpec((ts, D), lambda i: (i, 0))
    vec = pl.BlockSpec((1, D), lambda i: (0, 0))
    return pl.pallas_call(
        body, grid=(s // ts,), in_specs=[row, vec, vec, vec], out_specs=row,
        out_shape=jax.ShapeDtypeStruct((s, D), BF16), name="adaln_norm",
        compiler_params=_params(("parallel",)))(x, norm_w, shift, scale)


def _mm_nn(a, b, tm, tn, name):
    m, k = a.shape
    n = b.shape[1]

    def body(a_ref, b_ref, o_ref):
        o_ref[...] = _dot(a_ref[...], b_ref[...])

    return pl.pallas_call(
        body, grid=(n // tn, m // tm),
        in_specs=[pl.BlockSpec((tm, k), lambda j, i: (i, 0)), pl.BlockSpec((k, tn), lambda j, i: (0, j))],
        out_specs=pl.BlockSpec((tm, tn), lambda j, i: (i, j)),
        out_shape=jax.ShapeDtypeStruct((m, n), F32), name=name,
        compiler_params=_params(("parallel", "parallel"), VMEM_BIG))(a, b)


def _mm_tn(a, b, tm, tk, tn, name):
    m, k = a.shape
    n = b.shape[1]

    def body(a_ref, b_ref, o_ref):
        @pl.when(pl.program_id(2) == 0)
        def _():
            o_ref[...] = jnp.zeros_like(o_ref)
        o_ref[...] += _dot_tn(a_ref[...], b_ref[...])

    return pl.pallas_call(
        body, grid=(k // tk, n // tn, m // tm),
        in_specs=[pl.BlockSpec((tm, tk), lambda r, j, i: (i, r)), pl.BlockSpec((tm, tn), lambda r, j, i: (i, j))],
        out_specs=pl.BlockSpec((tk, tn), lambda r, j, i: (r, j)),
        out_shape=jax.ShapeDtypeStruct((k, n), F32), name=name,
        compiler_params=_params(("parallel", "parallel", "arbitrary"), VMEM_BIG))(a, b)


def _coords():
    return lax.axis_index("x"), lax.axis_index("y"), lax.axis_index("c")


HBM_REF = pl.BlockSpec(memory_space=pl.ANY)


def _chip_scatter_copies(p_refs, got_refs, send_sems, recv_sems):
    x, y, c = _coords()
    copies = []
    for a in range(len(p_refs)):
        for j, (px, py) in enumerate([(1 - x, y), (x, 1 - y), (1 - x, 1 - y)]):
            copies.append(pltpu.make_async_remote_copy(
                src_ref=p_refs[a].at[2 * px + py], dst_ref=got_refs[a].at[j], send_sem=send_sems.at[3 * a + j],
                recv_sem=recv_sems.at[3 * a + j], device_id=(px, py, c), device_id_type=MESH))
    return copies


def _scatter_alongside(body, n_in, n_out, n_parts, last_step):
    def wrapped(*refs):
        ins, parts = refs[:n_in], refs[n_in:n_in + n_parts]
        rest = refs[n_in + n_parts:]
        outs, got = rest[:n_out], rest[n_out:n_out + n_parts]
        scratch, (send_sems, recv_sems) = rest[n_out + n_parts:-2], rest[-2:]

        @pl.when(pl.program_id(0) == 0)
        def _():
            for cp in _chip_scatter_copies(parts, got, send_sems, recv_sems):
                cp.start()

        body(*ins, *outs, *scratch)

        @pl.when(pl.program_id(0) == last_step)
        def _():
            for cp in _chip_scatter_copies(parts, got, send_sems, recv_sems):
                cp.wait()

    return wrapped


def _scatter_operands(parts):
    n = len(parts)
    shapes = [jax.ShapeDtypeStruct((3,) + a.shape[1:], a.dtype) for a in parts]
    sems = [pltpu.SemaphoreType.DMA((3 * n,)), pltpu.SemaphoreType.DMA((3 * n,))]
    return [HBM_REF] * n, [HBM_REF] * n, shapes, sems


def _shifted_copies(win_ref, sh_ref, rows):
    for p in range(1, 8):
        sh_ref[p - 1, 0:rows, :] = win_ref[pl.ds(p, rows), :]


def _tap_rows(win_ref, sh_ref, start, rows):
    p = start % 8
    if p == 0:
        return win_ref[pl.ds(start, rows), :]
    return sh_ref[p - 1, pl.ds(start - p, rows), :]


def _conv_taps(win_ref, sh_ref, w_ref, rows, chunk, offset_of_tap):
    pieces = []
    for c0 in range(0, rows, chunk):
        acc = None
        for j in range(KC):
            term = w_ref[j:j + 1, :] * _tap_rows(win_ref, sh_ref, c0 + offset_of_tap(j), chunk)
            acc = term if acc is None else acc + term
        pieces.append(acc)
    return pieces


def _conv_fwd(proj_a, conv_w, conv_b, ln_w, ln_b, ts, chunk, shards):
    s = proj_a.shape[0]

    def body(av_ref, al_ref, ag_ref, w_ref, b_ref, lw_ref, lb_ref, u0_ref, u1_ref, za_ref, win_ref, sh_ref):
        @pl.when(pl.program_id(0) == 0)
        def _():
            win_ref[0:HALO, :] = jnp.zeros((HALO, D), F32)

        u0 = av_ref[...] * _sigmoid(al_ref[...])
        u0_ref[...] = u0
        win_ref[HALO:HALO + ts, :] = u0
        _shifted_copies(win_ref, sh_ref, ts + HALO - 8)
        pieces = _conv_taps(win_ref, sh_ref, w_ref, ts, chunk, lambda j: HALO - (KC - 1) + j)
        for n, acc in enumerate(pieces):
            u1_ref[n * chunk:(n + 1) * chunk, :] = acc + b_ref[...]
        win_ref[0:HALO, :] = win_ref[ts:ts + HALO, :]

        u1 = u1_ref[...]
        xc = u1 - _rowmean(u1)
        rstd = lax.rsqrt(_rowmean(xc * xc) + EPS)
        u2 = xc * rstd * lw_ref[...] + lb_ref[...]
        u3 = u2 * _sigmoid(u2)
        ag = ag_ref[...]
        za_ref[...] = (u3 * (ag * _sigmoid(ag))).astype(BF16)

    col = lambda c: pl.BlockSpec((ts, D), lambda i, c=c: (i, c))
    row = pl.BlockSpec((ts, D), lambda i: (i, 0))
    vec = pl.BlockSpec((1, D), lambda i: (0, 0))
    n = len(shards)
    gathered_shapes, sems = _gather_operands(shards)
    outs = pl.pallas_call(
        _gather_alongside(body, 7, 3, n, s // ts - 1), grid=(s // ts,),
        in_specs=[col(0), col(1), col(2), pl.BlockSpec((HALO, D), lambda i: (0, 0)), vec, vec, vec] + [HBM_REF] * n,
        out_specs=[row, row, row] + [HBM_REF] * n,
        out_shape=[jax.ShapeDtypeStruct((s, D), F32), jax.ShapeDtypeStruct((s, D), F32),
                   jax.ShapeDtypeStruct((s, D), BF16)] + gathered_shapes,
        scratch_shapes=[pltpu.VMEM((ts + HALO, D), F32), pltpu.VMEM((7, ts + HALO, D), F32)] + sems,
        name="conv_fwd", compiler_params=_params(("arbitrary",), VMEM_BIG))(
            proj_a, proj_a, proj_a, conv_w, conv_b, ln_w, ln_b, *shards)
    return outs[0], outs[1], outs[2], _as_chip_slabs(outs[3:], shards)


def _conv_bwd(dza, proj_a, u0, u1, conv_w, ln_w, ln_b, ts, chunk, parts):
    s = dza.shape[0]
    nt = s // ts
    per = ts // HALO

    def body(dza_ref, av_ref, al_ref, ag_ref, u0_ref, u0p_ref, u1_ref, w_ref, lw_ref, lb_ref,
             dpa_ref, gw_ref, gv_ref, dwin_ref, uwin_ref, du0_ref, gwp_ref, dsh_ref, ush_ref):
        step = pl.program_id(0)
        tile = nt - 1 - step

        @pl.when(step == 0)
        def _():
            dwin_ref[ts:ts + HALO, :] = jnp.zeros((HALO, D), F32)
            gwp_ref[...] = jnp.zeros_like(gwp_ref)
            gv_ref[...] = jnp.zeros_like(gv_ref)

        ag = ag_ref[...]
        sg = _sigmoid(ag)
        u1 = u1_ref[...]
        xc = u1 - _rowmean(u1)
        rstd = lax.rsqrt(_rowmean(xc * xc) + EPS)
        xh = xc * rstd
        u2 = xh * lw_ref[...] + lb_ref[...]
        s2 = _sigmoid(u2)
        dz = dza_ref[...]
        du3 = dz * (ag * sg)
        dpa_ref[:, 2 * D:3 * D] = (dz * (u2 * s2) * _dsilu(ag, sg)).astype(BF16)
        du2 = du3 * _dsilu(u2, s2)
        gv_ref[0:1, :] += _colsum(du2 * xh)
        gv_ref[1:2, :] += _colsum(du2)
        dxh = du2 * lw_ref[...]
        du1 = rstd * (dxh - _rowmean(dxh) - xh * _rowmean(dxh * xh))
        gv_ref[2:3, :] += _colsum(du1)
        dwin_ref[0:ts, :] = du1

        uwin_ref[0:HALO, :] = jnp.where(tile == 0, 0.0, u0p_ref[...])
        uwin_ref[HALO:HALO + ts, :] = u0_ref[...]

        _shifted_copies(dwin_ref, dsh_ref, ts + HALO - 8)
        _shifted_copies(uwin_ref, ush_ref, ts + HALO - 8)
        pieces = _conv_taps(dwin_ref, dsh_ref, w_ref, ts, chunk, lambda j: (KC - 1) - j)
        for n, acc in enumerate(pieces):
            du0_ref[n * chunk:(n + 1) * chunk, :] = acc
        for c0 in range(0, ts, chunk):
            dchunk = dwin_ref[c0:c0 + chunk, :]
            for j in range(KC):
                prod = dchunk * _tap_rows(uwin_ref, ush_ref, c0 + HALO - (KC - 1) + j, chunk)
                gwp_ref[8 * j:8 * j + 8, :] += jnp.sum(prod.reshape(chunk // 8, 8, D), axis=0)
        dwin_ref[ts:ts + HALO, :] = dwin_ref[0:HALO, :]

        du0 = du0_ref[...]
        al = al_ref[...]
        sl = _sigmoid(al)
        dpa_ref[:, 0:D] = (du0 * sl).astype(BF16)
        dpa_ref[:, D:2 * D] = (du0 * av_ref[...] * sl * (1.0 - sl)).astype(BF16)

        @pl.when(step == nt - 1)
        def _():
            for j in range(KC):
                gw_ref[j:j + 1, :] = _colsum(gwp_ref[8 * j:8 * j + 8, :])
            gw_ref[KC:HALO, :] = jnp.zeros((HALO - KC, D), F32)

    rev = lambda i: nt - 1 - i
    col = lambda c: pl.BlockSpec((ts, D), lambda i, c=c: (rev(i), c))
    row = pl.BlockSpec((ts, D), lambda i: (rev(i), 0))
    vec = pl.BlockSpec((1, D), lambda i: (0, 0))
    halo = pl.BlockSpec((HALO, D), lambda i: (jnp.maximum(rev(i) * per - 1, 0), 0))
    side_in, side_out, side_shapes, side_sems = _scatter_operands(parts)
    outs = pl.pallas_call(
        _scatter_alongside(body, 10, 3, len(parts), nt - 1), grid=(nt,),
        in_specs=[row, col(0), col(1), col(2), row, halo, row, pl.BlockSpec((HALO, D), lambda i: (0, 0)), vec, vec]
        + side_in,
        out_specs=[pl.BlockSpec((ts, A_COLS), lambda i: (rev(i), 0)),
                   pl.BlockSpec((HALO, D), lambda i: (0, 0)), pl.BlockSpec((8, D), lambda i: (0, 0))] + side_out,
        out_shape=[jax.ShapeDtypeStruct((s, A_COLS), BF16), jax.ShapeDtypeStruct((HALO, D), F32),
                   jax.ShapeDtypeStruct((8, D), F32)] + side_shapes,
        scratch_shapes=[pltpu.VMEM((ts + HALO, D), F32), pltpu.VMEM((ts + HALO, D), F32),
                        pltpu.VMEM((ts, D), F32), pltpu.VMEM((8 * HALO, D), F32),
                        pltpu.VMEM((7, ts + HALO, D), F32), pltpu.VMEM((7, ts + HALO, D), F32)] + side_sems,
        name="conv_bwd", compiler_params=_params(("arbitrary",), VMEM_BIG))(
            dza, proj_a, proj_a, proj_a, u0, u0, u1, conv_w, ln_w, ln_b, *parts)
    return outs[0], outs[1], outs[2], list(outs[3:])


def _mla_prep(proj_l, q_norm_w, kv_norm_w, w_uq2, w_ukv, cos_t, sin_t, ts):
    s = proj_l.shape[0]

    def body(pl_ref, qw_ref, kw_ref, wq_ref, wkv_ref, c_ref, s_ref, qn_ref, kvn_ref, q_ref, k_ref, v_ref):
        first = _first_half_mask(ts)
        cs = c_ref[...]
        sn = s_ref[...]

        def rms(v, w):
            return v * lax.rsqrt(_rowmean(v * v) + EPS) * w

        def rope(v):
            return v * cs + _swap_halves(v, first) * sn

        qn = rms(pl_ref[:, 0:RQ], qw_ref[...]).astype(BF16)
        kvn = rms(pl_ref[:, RQ:2 * RQ], kw_ref[...]).astype(BF16)
        qn_ref[...] = qn
        kvn_ref[...] = kvn
        q = _dot(qn, wq_ref[...])
        kv = _dot(kvn, wkv_ref[...])
        kr = rope(pl_ref[:, 2 * RQ:2 * RQ + LANE]).astype(BF16)
        for h in range(H):
            q_ref[h, :, 0:DN] = q[:, DN * h:DN * (h + 1)].astype(BF16)
            q_ref[h, :, DN:2 * DN] = rope(q[:, H * DN + LANE * h:H * DN + LANE * (h + 1)]).astype(BF16)
            k_ref[h, :, 0:DN] = kv[:, 2 * DN * h:2 * DN * h + DN].astype(BF16)
            k_ref[h, :, DN:2 * DN] = kr
            v_ref[h, :, 0:DN] = kv[:, 2 * DN * h + DN:2 * DN * (h + 1)].astype(BF16)
            v_ref[h, :, DN:2 * DN] = jnp.ones((ts, DN), BF16)

    const = lambda shape: pl.BlockSpec(shape, lambda i: (0,) * len(shape))
    rowb = lambda w: pl.BlockSpec((ts, w), lambda i: (i, 0))
    head = lambda w: pl.BlockSpec((H, ts, w), lambda i: (0, i, 0))
    return pl.pallas_call(
        body, grid=(s // ts,),
        in_specs=[rowb(L_COLS), const((1, RQ)), const((1, RQ)), const((RQ, 2 * H * DN)), const((RQ, 2 * H * DN)),
                  rowb(LANE), rowb(LANE)],
        out_specs=[rowb(RQ), rowb(RQ), head(2 * DN), head(2 * DN), head(2 * DN)],
        out_shape=[jax.ShapeDtypeStruct((s, RQ), BF16), jax.ShapeDtypeStruct((s, RQ), BF16),
                   jax.ShapeDtypeStruct((H, s, 2 * DN), BF16), jax.ShapeDtypeStruct((H, s, 2 * DN), BF16),
                   jax.ShapeDtypeStruct((H, s, 2 * DN), BF16)],
        name="mla_prep", compiler_params=_params(("parallel",)))(
            proj_l, q_norm_w, kv_norm_w, w_uq2, w_ukv, cos_t, sin_t)


def _mla_prep_bwd(dq, dk, dv, proj_l, qn, kvn, q_norm_w, kv_norm_w, w_uq2, w_ukv, cos_t, sin_t, ts):
    s = proj_l.shape[0]

    def body(dq_ref, dk_ref, dv_ref, pl_ref, qn_ref, kvn_ref, qw_ref, kw_ref, wq_ref, wkv_ref, c_ref, s_ref,
             dpl_ref, gwq_ref, gwkv_ref, gv_ref, dq2_ref, dkv2_ref):
        @pl.when(pl.program_id(0) == 0)
        def _():
            gwq_ref[...] = jnp.zeros_like(gwq_ref)
            gwkv_ref[...] = jnp.zeros_like(gwkv_ref)
            gv_ref[...] = jnp.zeros_like(gv_ref)

        first = _first_half_mask(ts)
        cs = c_ref[...]
        sn = s_ref[...]

        def rope_bwd(g):
            return g * cs + _swap_halves(g * sn, first)

        def rms_bwd(v, w, dy):
            r = lax.rsqrt(_rowmean(v * v) + EPS)
            vh = v * r
            dvh = dy * w
            return r * (dvh - vh * _rowmean(dvh * vh)), _colsum(dy * vh)

        dkr = None
        for h in range(H):
            dq2_ref[:, DN * h:DN * (h + 1)] = dq_ref[h, :, 0:DN].astype(BF16)
            dq2_ref[:, H * DN + LANE * h:H * DN + LANE * (h + 1)] = rope_bwd(dq_ref[h, :, DN:2 * DN]).astype(BF16)
            dkv2_ref[:, 2 * DN * h:2 * DN * h + DN] = dk_ref[h, :, 0:DN].astype(BF16)
            dkv2_ref[:, 2 * DN * h + DN:2 * DN * (h + 1)] = dv_ref[h].astype(BF16)
            part = dk_ref[h, :, DN:2 * DN]
            dkr = part if dkr is None else dkr + part

        dq2 = dq2_ref[...]
        dkv2 = dkv2_ref[...]
        gwq_ref[...] += _dot_tn(qn_ref[...], dq2)
        gwkv_ref[...] += _dot_tn(kvn_ref[...], dkv2)
        dcq, gq = rms_bwd(pl_ref[:, 0:RQ], qw_ref[...], _dot_nt(dq2, wq_ref[...]))
        dckv, gkv = rms_bwd(pl_ref[:, RQ:2 * RQ], kw_ref[...], _dot_nt(dkv2, wkv_ref[...]))
        gv_ref[0:1, :] += gq
        gv_ref[1:2, :] += gkv
        dpl_ref[:, 0:RQ] = dcq.astype(BF16)
        dpl_ref[:, RQ:2 * RQ] = dckv.astype(BF16)
        dpl_ref[:, 2 * RQ:2 * RQ + LANE] = rope_bwd(dkr).astype(BF16)

    const = lambda shape: pl.BlockSpec(shape, lambda i: (0,) * len(shape))
    rowb = lambda w: pl.BlockSpec((ts, w), lambda i: (i, 0))
    head = lambda w: pl.BlockSpec((H, ts, w), lambda i: (0, i, 0))
    return pl.pallas_call(
        body, grid=(s // ts,),
        in_specs=[head(2 * DN), head(2 * DN), head(DN), rowb(L_COLS), rowb(RQ), rowb(RQ), const((1, RQ)),
                  const((1, RQ)), const((RQ, 2 * H * DN)), const((RQ, 2 * H * DN)), rowb(LANE), rowb(LANE)],
        out_specs=[rowb(L_COLS), const((RQ, 2 * H * DN)), const((RQ, 2 * H * DN)), const((8, RQ))],
        out_shape=[jax.ShapeDtypeStruct((s, L_COLS), BF16), jax.ShapeDtypeStruct((RQ, 2 * H * DN), F32),
                   jax.ShapeDtypeStruct((RQ, 2 * H * DN), F32), jax.ShapeDtypeStruct((8, RQ), F32)],
        scratch_shapes=[pltpu.VMEM((ts, 2 * H * DN), BF16), pltpu.VMEM((ts, 2 * H * DN), BF16)],
        name="mla_prep_bwd", compiler_params=_params(("arbitrary",), VMEM_BIG))(
            dq, dk, dv, proj_l, qn, kvn, q_norm_w, kv_norm_w, w_uq2, w_ukv, cos_t, sin_t)


def _causal_pairs(n, by_key):
    if by_key:
        pairs = [(i, j) for j in range(n) for i in range(j, n)]
    else:
        pairs = [(i, j) for i in range(n) for j in range(i + 1)]
    return (jnp.asarray(np.array([p[0] for p in pairs], np.int32)),
            jnp.asarray(np.array([p[1] for p in pairs], np.int32)))


LOG2E = 1.4426950408889634
LN2 = 0.6931471805599453
ATT_HEADS_FWD = 4
ATT_HEADS = 2
W_IN_ROWS = 336
ATT_ROWS = 64


def _diag_width(r0, t):
    return min(t, -(-(r0 + ATT_ROWS) // LANE) * LANE)


def _diag_mask_rows(r0, width):
    rows = r0 + lax.broadcasted_iota(jnp.int32, (ATT_ROWS, width), 0)
    cols = lax.broadcasted_iota(jnp.int32, (ATT_ROWS, width), 1)
    return cols <= rows


def _diag_mask(t):
    return lax.broadcasted_iota(jnp.int32, (t, t), 1) <= lax.broadcasted_iota(jnp.int32, (t, t), 0)


def _attn_fwd(q, k, v, t):
    s = q.shape[1]
    n = s // t
    scale2 = float((DN + DR) ** -0.5) * LOG2E
    qi, ki = _causal_pairs(n, by_key=False)

    def body(qi_ref, ki_ref, q_ref, k_ref, v_ref, o_ref, lse_ref, *scratch):
        per_head = [scratch[5 * h:5 * h + 5] for h in range(ATT_HEADS_FWD)]
        p = pl.program_id(1)
        i = qi_ref[p]
        j = ki_ref[p]

        @pl.when(j == 0)
        def _():
            for m_sc, acc_sc, _, _, _ in per_head:
                m_sc[...] = jnp.full_like(m_sc, -jnp.inf)
                acc_sc[...] = jnp.zeros_like(acc_sc)

        def scores(h, diag):
            sc = _dot_nt(q_ref[h], k_ref[h])
            if diag:
                sc = jnp.where(_diag_mask(t), sc, -jnp.inf)
            per_head[h][2][...] = sc

        def rowmax(h, rows):
            per_head[h][4][rows, :] = jnp.max(per_head[h][2][rows, :], axis=-1, keepdims=True)

        def stats(h):
            m_sc, acc_sc, _, _, mx_sc = per_head[h]
            m_prev = m_sc[...]
            m_new = jnp.maximum(m_prev, mx_sc[...] * scale2)
            m_sc[...] = m_new
            acc_sc[...] = jnp.exp2(m_prev - m_new) * acc_sc[...]

        def probs(h, rows):
            m_sc, _, s_sc, p_sc, _ = per_head[h]
            p_sc[rows, :] = jnp.exp2(s_sc[rows, :] * scale2 - m_sc[rows, :]).astype(BF16)

        def values(h):
            _, acc_sc, _, p_sc, _ = per_head[h]
            acc_sc[...] += _dot(p_sc[...], v_ref[h])

        def step(diag):
            blocks = [slice(r0, r0 + ATT_ROWS) for r0 in range(0, t, ATT_ROWS)]
            for h in range(ATT_HEADS_FWD):
                scores(h, diag)
            for rows in blocks:
                rowmax(0, rows)
            stats(0)
            for h in range(ATT_HEADS_FWD):
                for rows in blocks:
                    probs(h, rows)
                    if h + 1 < ATT_HEADS_FWD:
                        rowmax(h + 1, rows)
                if h + 1 < ATT_HEADS_FWD:
                    stats(h + 1)
                values(h)

        @pl.when(j < i)
        def _():
            step(False)

        @pl.when(j == i)
        def _():
            step(True)
            for h, (m_sc, acc_sc, _, _, _) in enumerate(per_head):
                l = acc_sc[:, DN:2 * DN]
                o_ref[:, DN * h:DN * (h + 1)] = acc_sc[:, 0:DN] / l
                lse_ref[h] = (m_sc[...] + jnp.log2(l[:, 0:1])) * LN2

    hb = ATT_HEADS_FWD
    grid_spec = pltpu.PrefetchScalarGridSpec(
        num_scalar_prefetch=2, grid=(H // hb, int(qi.shape[0])),
        in_specs=[pl.BlockSpec((hb, t, 2 * DN), lambda h, p, qi, ki: (h, qi[p], 0)),
                  pl.BlockSpec((hb, t, 2 * DN), lambda h, p, qi, ki: (h, ki[p], 0)),
                  pl.BlockSpec((hb, t, 2 * DN), lambda h, p, qi, ki: (h, ki[p], 0))],
        out_specs=[pl.BlockSpec((t, hb * DN), lambda h, p, qi, ki: (qi[p], h)),
                   pl.BlockSpec((hb, t, 1), lambda h, p, qi, ki: (h, qi[p], 0))],
        scratch_shapes=[pltpu.VMEM((t, 1), F32), pltpu.VMEM((t, 2 * DN), F32), pltpu.VMEM((t, t), F32),
                        pltpu.VMEM((t, t), BF16), pltpu.VMEM((t, 1), F32)] * hb)
    return pl.pallas_call(
        body, grid_spec=grid_spec,
        out_shape=[jax.ShapeDtypeStruct((s, H * DN), F32), jax.ShapeDtypeStruct((H, s, 1), F32)],
        name="attn_fwd", compiler_params=_params(("parallel", "arbitrary"), VMEM_BIG))(qi, ki, q, k, v)


def _attn_bwd(q, k, v, do, lse, delta, t):
    s = q.shape[1]
    n = s // t
    scale = float((DN + DR) ** -0.5)
    qi, ki = _causal_pairs(n, by_key=True)

    def body(qi_ref, ki_ref, q_ref, k_ref, v_ref, do_ref, lse_ref, dl_ref, dq_ref, dk_ref, dv_ref,
             dk_sc, dv_sc, s_sc, dp_sc, p_sc, ds_sc):
        p = pl.program_id(1)
        i = qi_ref[p]
        j = ki_ref[p]

        @pl.when(p == 0)
        def _():
            dq_ref[...] = jnp.zeros_like(dq_ref)

        @pl.when(i == j)
        def _():
            dk_sc[...] = jnp.zeros_like(dk_sc)
            dv_sc[...] = jnp.zeros_like(dv_sc)

        def step(diag):
            for h in range(ATT_HEADS):
                s_sc[h] = _dot_nt(q_ref[h], k_ref[h])
                dp_sc[h] = _dot_nt(do_ref[:, DN * h:DN * (h + 1)], v_ref[h, :, 0:DN])
            for h in range(ATT_HEADS):
                for r0 in range(0, t, ATT_ROWS):
                    rows = slice(r0, r0 + ATT_ROWS)
                    width = _diag_width(r0, t) if diag else t
                    sc = s_sc[h, rows, 0:width] * (scale * LOG2E)
                    if diag:
                        sc = jnp.where(_diag_mask_rows(r0, width), sc, -jnp.inf)
                    pr = jnp.exp2(sc - lse_ref[h, rows, :] * LOG2E)
                    ds = pr * (dp_sc[h, rows, 0:width] - dl_ref[h, rows, :]) * scale
                    p_sc[h, rows, 0:width] = pr.astype(BF16)
                    ds_sc[h, rows, 0:width] = ds.astype(BF16)
                    if width < t:
                        p_sc[h, rows, width:t] = jnp.zeros((ATT_ROWS, t - width), BF16)
                        ds_sc[h, rows, width:t] = jnp.zeros((ATT_ROWS, t - width), BF16)
            q_rows = pl.ds(pl.multiple_of(i * t, t), t)
            for h in range(ATT_HEADS):
                dv_sc[h] += _dot_tn(p_sc[h], do_ref[:, DN * h:DN * (h + 1)])
                dk_sc[h] += _dot_tn(ds_sc[h], q_ref[h])
                dq_ref[h, q_rows, :] += _dot(ds_sc[h], k_ref[h])

        @pl.when(i > j)
        def _():
            step(False)

        @pl.when(i == j)
        def _():
            step(True)

        @pl.when(i == n - 1)
        def _():
            dk_ref[...] = dk_sc[...]
            dv_ref[...] = dv_sc[...]

    hb = ATT_HEADS
    grid_spec = pltpu.PrefetchScalarGridSpec(
        num_scalar_prefetch=2, grid=(H // hb, int(qi.shape[0])),
        in_specs=[pl.BlockSpec((hb, t, 2 * DN), lambda h, p, qi, ki: (h, qi[p], 0)),
                  pl.BlockSpec((hb, t, 2 * DN), lambda h, p, qi, ki: (h, ki[p], 0)),
                  pl.BlockSpec((hb, t, 2 * DN), lambda h, p, qi, ki: (h, ki[p], 0)),
                  pl.BlockSpec((t, hb * DN), lambda h, p, qi, ki: (qi[p], h)),
                  pl.BlockSpec((hb, t, 1), lambda h, p, qi, ki: (h, qi[p], 0)),
                  pl.BlockSpec((hb, t, 1), lambda h, p, qi, ki: (h, qi[p], 0))],
        out_specs=[pl.BlockSpec((hb, s, 2 * DN), lambda h, p, qi, ki: (h, 0, 0)),
                   pl.BlockSpec((hb, t, 2 * DN), lambda h, p, qi, ki: (h, ki[p], 0)),
                   pl.BlockSpec((hb, t, DN), lambda h, p, qi, ki: (h, ki[p], 0))],
        scratch_shapes=[pltpu.VMEM((hb, t, 2 * DN), F32), pltpu.VMEM((hb, t, DN), F32),
                        pltpu.VMEM((hb, t, t), F32), pltpu.VMEM((hb, t, t), F32),
                        pltpu.VMEM((hb, t, t), BF16), pltpu.VMEM((hb, t, t), BF16)])
    return pl.pallas_call(
        body, grid_spec=grid_spec,
        out_shape=[jax.ShapeDtypeStruct((H, s, 2 * DN), F32), jax.ShapeDtypeStruct((H, s, 2 * DN), F32),
                   jax.ShapeDtypeStruct((H, s, DN), F32)],
        name="attn_bwd", compiler_params=_params(("parallel", "arbitrary"), VMEM_BIG))(
            qi, ki, q, k, v, do, lse, delta)


def _middle(za, o, proj_g, x, tgt, gate, fnw, wco, wao, wo, ts):
    s = x.shape[0]
    inv_d = 1.0 / D

    def body(za_ref, o_ref, bg_ref, ga_ref, gb_ref, x_ref, t_ref, gate_ref, fnw_ref, wco_ref, wao_ref, wo_ref,
             dx2_ref, dza_ref, do_ref, dl_ref, dpg_ref, zb_ref, mg_ref, dmo_ref, dya_ref, dyb_ref, vec_ref):
        @pl.when(pl.program_id(0) == 0)
        def _():
            vec_ref[...] = jnp.zeros_like(vec_ref)

        ov = o_ref[...]
        bg = bg_ref[...]
        sb = _sigmoid(bg)
        silu_b = bg * sb
        zb = (ov * silu_b).astype(BF16)
        zb_ref[...] = zb
        ya = _dot(za_ref[...], wco_ref[...])
        yb = _dot(zb, wao_ref[...])
        sa = _sigmoid(ga_ref[...])
        sg = _sigmoid(gb_ref[...])
        mg = (sa * ya + sg * yb).astype(BF16)
        mg_ref[...] = mg
        mo = _dot(mg, wo_ref[...])
        gate_v = gate_ref[...]
        x2 = x_ref[...] + gate_v * mo
        r = lax.rsqrt(_rowmean(x2 * x2) + EPS)
        xh = x2 * r
        fw = fnw_ref[...]
        e = xh * fw - t_ref[...]
        vec_ref[2:3, :] += _colsum(e * e)
        dy = e * inv_d
        vec_ref[0:1, :] += _colsum(dy * xh)
        dxh = dy * fw
        dx2 = r * (dxh - xh * _rowmean(dxh * xh))
        dx2_ref[...] = dx2
        vec_ref[1:2, :] += _colsum(dx2 * mo)
        dmo = (gate_v * dx2).astype(BF16)
        dmo_ref[...] = dmo
        dmg = _dot_nt(dmo, wo_ref[...])
        dya = (sa * dmg).astype(BF16)
        dyb = (sg * dmg).astype(BF16)
        dya_ref[...] = dya
        dyb_ref[...] = dyb
        dpg_ref[:, D:2 * D] = (dmg * ya * (sa * (1.0 - sa))).astype(BF16)
        dpg_ref[:, 2 * D:3 * D] = (dmg * yb * (sg * (1.0 - sg))).astype(BF16)
        dza_ref[...] = _dot_nt(dya, wco_ref[...])
        dzb = _dot_nt(dyb, wao_ref[...])
        dov = dzb * silu_b
        do_ref[...] = dov.astype(BF16)
        dpg_ref[:, 0:D] = (dzb * ov * _dsilu(bg, sb)).astype(BF16)
        dprod = dov * ov
        for h in range(H):
            dl_ref[h] = jnp.sum(dprod[:, DN * h:DN * (h + 1)], axis=-1, keepdims=True)

    col = lambda c: pl.BlockSpec((ts, D), lambda i, c=c: (i, c))
    row = pl.BlockSpec((ts, D), lambda i: (i, 0))
    vec = pl.BlockSpec((1, D), lambda i: (0, 0))
    wsp = pl.BlockSpec((D, D), lambda i: (0, 0))
    bf = jax.ShapeDtypeStruct((s, D), BF16)
    ff = jax.ShapeDtypeStruct((s, D), F32)
    return pl.pallas_call(
        body, grid=(s // ts,),
        in_specs=[row, row, col(0), col(1), col(2), row, row, vec, vec, wsp, wsp, wsp],
        out_specs=[row, row, row, pl.BlockSpec((H, ts, 1), lambda i: (0, i, 0)),
                   pl.BlockSpec((ts, G_COLS), lambda i: (i, 0)), row, row, row, row, row,
                   pl.BlockSpec((8, D), lambda i: (0, 0))],
        out_shape=[ff, ff, bf, jax.ShapeDtypeStruct((H, s, 1), F32), jax.ShapeDtypeStruct((s, G_COLS), BF16),
                   bf, bf, bf, bf, bf, jax.ShapeDtypeStruct((8, D), F32)],
        name="middle", compiler_params=_params(("arbitrary",), VMEM_BIG))(
            za, o, proj_g, proj_g, proj_g, x, tgt, gate, fnw, wco, wao, wo)


def _input_bwd(dpa, dpl, dpg, wa, wl, wg, x, dx2, norm_w, scale, ts, parts):
    s = x.shape[0]

    def body(dpa_ref, dpl_ref, dpg_ref, wa_ref, wl_ref, wg_ref, x_ref, dx2_ref, nw_ref, sc_ref, gx_ref, gv_ref):
        @pl.when(pl.program_id(0) == 0)
        def _():
            gv_ref[...] = jnp.zeros_like(gv_ref)

        dh = (_dot_nt(dpa_ref[...], wa_ref[...]) + _dot_nt(dpl_ref[...], wl_ref[...])
              + _dot_nt(dpg_ref[...], wg_ref[...]))
        xv = x_ref[...]
        r = lax.rsqrt(_rowmean(xv * xv) + EPS)
        xh = xv * r
        nw = nw_ref[...]
        gv_ref[0:1, :] += _colsum(dh)
        gv_ref[1:2, :] += _colsum(dh * (xh * nw))
        dy = dh * (1.0 + sc_ref[...])
        gv_ref[2:3, :] += _colsum(dy * xh)
        dxh = dy * nw
        gx_ref[...] = dx2_ref[...] + r * (dxh - xh * _rowmean(dxh * xh))

    const = lambda shape: pl.BlockSpec(shape, lambda i: (0, 0))
    rowb = lambda w: pl.BlockSpec((ts, w), lambda i: (i, 0))
    side_in, side_out, side_shapes, side_sems = _scatter_operands(parts)
    outs = pl.pallas_call(
        _scatter_alongside(body, 10, 2, len(parts), s // ts - 1), grid=(s // ts,),
        in_specs=[rowb(A_COLS), rowb(L_COLS), rowb(G_COLS), const((D, A_COLS)), const((D, L_COLS)),
                  const((D, G_COLS)), rowb(D), rowb(D), const((1, D)), const((1, D))] + side_in,
        out_specs=[rowb(D), const((8, D))] + side_out,
        out_shape=[jax.ShapeDtypeStruct((s, D), F32), jax.ShapeDtypeStruct((8, D), F32)] + side_shapes,
        scratch_shapes=side_sems,
        name="input_bwd", compiler_params=_params(("arbitrary",), VMEM_BIG))(
            dpa, dpl, dpg, wa, wl, wg, x, dx2, norm_w, scale, *parts)
    return outs[0], outs[1], list(outs[2:])


def _adamw_math(w, g, m, v):
    nm = ADAM_B1 * m + (1.0 - ADAM_B1) * g
    nv = ADAM_B2 * v + (1.0 - ADAM_B2) * (g * g)
    m_hat = nm / (1.0 - ADAM_B1 ** ADAM_STEP)
    v_hat = nv / (1.0 - ADAM_B2 ** ADAM_STEP)
    return -ADAM_LR * (m_hat / (jnp.sqrt(v_hat) + ADAM_EPS) + ADAM_WD * w), nm, nv


def _adamw(w, g, m, v, tr, name):
    lead, (rows, cols) = w.shape[:-2], w.shape[-2:]

    def body(w_ref, g_ref, m_ref, v_ref, d_ref, nm_ref, nv_ref):
        d_ref[...], nm_ref[...], nv_ref[...] = _adamw_math(w_ref[...], g_ref[...], m_ref[...], v_ref[...])

    blk = pl.BlockSpec((1,) * len(lead) + (tr, cols), lambda i: (0,) * len(lead) + (i, 0))
    shp = jax.ShapeDtypeStruct(w.shape, F32)
    return pl.pallas_call(
        body, grid=(rows // tr,), in_specs=[blk] * 4, out_specs=[blk] * 3, out_shape=[shp] * 3, name=name,
        compiler_params=_params(("parallel",), VMEM_BIG))(w, g.reshape(w.shape), m, v)


ROW_SHIFT, ROW_SCALE, ROW_NORM_W = 0, 1, 2
ROW_FINAL_NORM_W, ROW_GATE, ROW_LOSS = 8, 9, 10
ROW_LN_W, ROW_LN_B, ROW_CONV_B = 16, 17, 18
ROW_Q_NORM_W, ROW_KV_NORM_W = 24, 25
ROW_CONV_W = 32
SUM_ROWS = 64
VECTOR_ROWS = ((ROW_SHIFT, ROW_SCALE, ROW_GATE), (ROW_NORM_W,), (ROW_CONV_B,), (ROW_LN_W,), (ROW_LN_B,),
               (ROW_Q_NORM_W,), (ROW_KV_NORM_W,), (ROW_FINAL_NORM_W,))


def _small_finalize(gathered, vectors, conv, chip):
    n = len(vectors)
    cw = conv[0].shape[2]

    def body(chip_ref, g_ref, *refs):
        ins, outs = refs[:3 * n + 3], refs[3 * n + 3:]
        tot = g_ref[0]
        for k in range(1, N_DEV):
            tot = tot + g_ref[k]
        for p, rows in enumerate(VECTOR_ROWS):
            w_ref, m_ref, v_ref = ins[3 * p:3 * p + 3]
            g_out, d_out, nm_out, nv_out = outs[4 * p:4 * p + 4]
            width = w_ref.shape[1] // len(rows)
            for q, r in enumerate(rows):
                lanes = slice(q * width, (q + 1) * width)
                g = tot[r:r + 1, 0:width]
                g_out[:, lanes] = g
                d_out[:, lanes], nm_out[:, lanes], nv_out[:, lanes] = _adamw_math(
                    w_ref[:, lanes], g, m_ref[:, lanes], v_ref[:, lanes])
        cols = pl.ds(pl.multiple_of(chip_ref[0] * cw, LANE), cw)
        gc = g_ref[0, pl.ds(ROW_CONV_W, KC), cols]
        for k in range(1, N_DEV):
            gc = gc + g_ref[k, pl.ds(ROW_CONV_W, KC), cols]
        cw_ref, cm_ref, cv_ref = ins[3 * n:3 * n + 3]
        g_out, d_out, nm_out, nv_out, dmod_ref, loss_ref = outs[4 * n:]
        g_out[0] = gc
        d_out[0], nm_out[0], nv_out[0] = _adamw_math(cw_ref[0], gc, cm_ref[0], cv_ref[0])
        for k in range(N_DEV):
            for q, r in enumerate((ROW_SHIFT, ROW_SCALE, ROW_GATE)):
                dmod_ref[k:k + 1, q * D:(q + 1) * D] = g_ref[k, r:r + 1, :]
        loss_ref[...] = (0.5 / D) * jnp.sum(tot[ROW_LOSS:ROW_LOSS + 1, :], axis=-1, keepdims=True)

    flat_in = [a for triple in vectors for a in triple] + list(conv)
    shapes = [jax.ShapeDtypeStruct(w.shape, F32) for w, _, _ in vectors for _ in range(4)]
    shapes += [jax.ShapeDtypeStruct(conv[0].shape, F32)] * 4
    shapes += [jax.ShapeDtypeStruct((N_DEV, 3 * D), F32), jax.ShapeDtypeStruct((1, 1), F32)]
    whole = pl.BlockSpec(memory_space=pltpu.VMEM)
    return pl.pallas_call(
        body, out_shape=shapes,
        in_specs=[pl.BlockSpec(memory_space=pltpu.SMEM)] + [whole] * (1 + len(flat_in)),
        out_specs=[whole] * len(shapes), name="small_finalize")(chip, gathered, *flat_in)


def _ada_fwd(c_all, w_ada_shard, b_ada_shard):
    def body(c_ref, w_ref, b_ref, o_ref):
        cv = c_ref[...]
        o_ref[...] = jnp.dot(cv * _sigmoid(cv), w_ref[...], preferred_element_type=F32,
                             precision=lax.Precision.HIGHEST) + b_ref[...]

    return pl.pallas_call(
        body, out_shape=jax.ShapeDtypeStruct((N_DEV, w_ada_shard.shape[1]), F32), name="ada_fwd")(
            c_all, w_ada_shard, b_ada_shard)


def _ada_bwd(c_all_t, dmod_shard):
    def body(c_ref, d_ref, o_ref):
        cv = c_ref[...]
        o_ref[...] = jnp.dot(cv * _sigmoid(cv), d_ref[...], preferred_element_type=F32,
                             precision=lax.Precision.HIGHEST)

    return pl.pallas_call(
        body, out_shape=jax.ShapeDtypeStruct((D, dmod_shard.shape[1]), F32), name="ada_bwd")(c_all_t, dmod_shard)


def _sum_chip_slabs(arrived, part, place, tr, name, axis):
    n, rows, cols = arrived.shape
    per = rows // tr

    def body(place_ref, a_ref, p_ref, o_ref):
        acc = p_ref[0].astype(F32)
        for k in range(n):
            acc = acc + a_ref[k].astype(F32)
        o_ref[...] = acc

    if axis == 1:
        whole, out_map = (2 * rows, cols), lambda i, pc: (pc[1] * per + i, 0)
    else:
        whole, out_map = (rows, 2 * cols), lambda i, pc: (i, pc[1])
    grid_spec = pltpu.PrefetchScalarGridSpec(
        num_scalar_prefetch=1, grid=(per,),
        in_specs=[pl.BlockSpec((n, tr, cols), lambda i, pc: (0, i, 0)),
                  pl.BlockSpec((1, tr, cols), lambda i, pc: (pc[0], i, 0))],
        out_specs=pl.BlockSpec((tr, cols), out_map))
    return pl.pallas_call(
        body, grid_spec=grid_spec, out_shape=jax.ShapeDtypeStruct(whole, F32), name=name,
        compiler_params=_params(("parallel",)))(place, arrived, part)


def _add_own_half(full, other, core, tr, name, axis):
    n, rows, cols = other.shape
    per = rows // tr

    def body(c_ref, f_ref, o_ref, out_ref):
        out_ref[...] = (f_ref[...] + o_ref[...]).astype(BF16)

    full_map = (lambda k, i, c: (k, c[0] * per + i, 0)) if axis == 1 else (lambda k, i, c: (k, i, c[0]))
    grid_spec = pltpu.PrefetchScalarGridSpec(
        num_scalar_prefetch=1, grid=(n, per),
        in_specs=[pl.BlockSpec((1, tr, cols), full_map),
                  pl.BlockSpec((1, tr, cols), lambda k, i, c: (k, i, 0))],
        out_specs=pl.BlockSpec((1, tr, cols), lambda k, i, c: (k, i, 0)))
    return pl.pallas_call(
        body, grid_spec=grid_spec, out_shape=jax.ShapeDtypeStruct((n, rows, cols), BF16), name=name,
        compiler_params=_params(("parallel", "parallel")))(core, full, other)


def _allgather8(block, src_rows, vmem, name):
    n = block.shape[1]
    m = src_rows
    sliced = block.shape[0] != m

    def body(x_ref, out_ref, send_sems, recv_sems, local_sem):
        x, y, c = _coords()
        me, sibling = (x, y, c), (x, y, 1 - c)
        chips = [(1 - x, y), (x, 1 - y), (1 - x, 1 - y)]
        src = x_ref.at[pl.ds(pl.multiple_of(c * m, 16), m), :] if sliced else x_ref

        def rows(px, py, pc):
            return out_ref.at[pl.ds(pl.multiple_of((4 * px + 2 * py + pc) * m, 8), m), :]

        def copy(k, blk, to, source=None):
            return pltpu.make_async_remote_copy(
                src_ref=rows(*blk) if source is None else source, dst_ref=rows(*blk),
                send_sem=send_sems.at[k], recv_sem=recv_sems.at[k], device_id=to, device_id_type=MESH)

        mine = pltpu.make_async_copy(src, rows(*me), local_sem)
        mine.start()
        first = [copy(0, me, sibling, source=src)]
        first += [copy(1 + j, me, (*chip, c), source=src) for j, chip in enumerate(chips)]
        for cp in first:
            cp.start()
        passed = [copy(4 + j, (*chip, c), sibling) for j, chip in enumerate(chips)]
        for j, chip in enumerate(chips):
            copy(1 + j, (*chip, c), me).wait_recv()
            passed[j].start()
        copy(0, sibling, me).wait_recv()
        for j, chip in enumerate(chips):
            copy(4 + j, (*chip, 1 - c), me).wait_recv()
        for cp in first + passed:
            cp.wait_send()
        mine.wait()

    space = pltpu.VMEM if vmem else pl.ANY
    return pl.pallas_call(
        body, out_shape=jax.ShapeDtypeStruct((N_DEV * m, n), block.dtype),
        in_specs=[pl.BlockSpec(memory_space=space)], out_specs=pl.BlockSpec(memory_space=space),
        scratch_shapes=[pltpu.SemaphoreType.DMA((7,)), pltpu.SemaphoreType.DMA((7,)), pltpu.SemaphoreType.DMA],
        name=name)(block)


def _gather_plan(x_refs, out_refs, send_sems, recv_sems, local_sems):
    n = len(x_refs)
    halves = [r.shape[0] // 2 for r in x_refs]
    x, y, c = _coords()
    me, sibling = (x, y, c), (x, y, 1 - c)
    chips = [(1 - x, y), (x, 1 - y), (1 - x, 1 - y)]

    def src(a):
        return x_refs[a].at[pl.ds(pl.multiple_of(c * halves[a], 16), halves[a]), :]

    def blk(a, px, py, pc):
        return out_refs[a].at[4 * px + 2 * py + pc]

    def copy(a, k, who, to, source=None):
        return pltpu.make_async_remote_copy(
            src_ref=blk(a, *who) if source is None else source, dst_ref=blk(a, *who),
            send_sem=send_sems.at[7 * a + k], recv_sem=recv_sems.at[7 * a + k], device_id=to, device_id_type=MESH)

    def mine(a):
        return pltpu.make_async_copy(src(a), blk(a, *me), local_sems.at[a])

    def first(a):
        return ([copy(a, 0, me, sibling, source=src(a))]
                + [copy(a, 1 + j, me, (*chip, c), source=src(a)) for j, chip in enumerate(chips)])

    def begin():
        for a in range(n):
            mine(a).start()
        for a in range(n):
            for cp in first(a):
                cp.start()

    def finish():
        onward = []
        for j, chip in enumerate(chips):
            for a in range(n):
                copy(a, 1 + j, (*chip, c), me).wait_recv()
                onward.append(copy(a, 4 + j, (*chip, c), sibling))
                onward[-1].start()
        for a in range(n):
            copy(a, 0, sibling, me).wait_recv()
        for j, chip in enumerate(chips):
            for a in range(n):
                copy(a, 4 + j, (*chip, 1 - c), me).wait_recv()
        for a in range(n):
            for cp in first(a):
                cp.wait_send()
        for cp in onward:
            cp.wait_send()
        for a in range(n):
            mine(a).wait()

    return begin, finish


def _gather_operands(shards):
    n = len(shards)
    shapes = [jax.ShapeDtypeStruct((N_DEV, a.shape[0] // 2, a.shape[1]), a.dtype) for a in shards]
    sems = [pltpu.SemaphoreType.DMA((7 * n,)), pltpu.SemaphoreType.DMA((7 * n,)), pltpu.SemaphoreType.DMA((n,))]
    return shapes, sems


def _as_chip_slabs(gathered, shards):
    return [o.reshape(N_CHIP, a.shape[0], a.shape[1]) for o, a in zip(gathered, shards)]


def _gather_weights(shards):
    n = len(shards)

    def body(*refs):
        begin, finish = _gather_plan(refs[:n], refs[n:2 * n], *refs[2 * n:])
        begin()
        finish()

    shapes, sems = _gather_operands(shards)
    outs = pl.pallas_call(
        body, out_shape=shapes, in_specs=[HBM_REF] * n, out_specs=[HBM_REF] * n, scratch_shapes=sems,
        name="gather_weights")(*shards)
    return _as_chip_slabs(outs, shards)


def _gather_alongside(body, n_in, n_out, n_shards, last_step):
    def wrapped(*refs):
        ins, shards = refs[:n_in], refs[n_in:n_in + n_shards]
        rest = refs[n_in + n_shards:]
        outs, gathered = rest[:n_out], rest[n_out:n_out + n_shards]
        scratch, sems = rest[n_out + n_shards:-3], rest[-3:]

        @pl.when(pl.program_id(0) == 0)
        def _():
            _gather_plan(shards, gathered, *sems)[0]()

        body(*ins, *outs, *scratch)

        @pl.when(pl.program_id(0) == last_step)
        def _():
            _gather_plan(shards, gathered, *sems)[1]()

    return wrapped


def _half(ref, axis, which, ndim):
    size = ref.shape[axis] // 2
    idx = [slice(None)] * ndim
    idx[axis] = pl.ds(pl.multiple_of(which * size, 8 if axis == ndim - 2 else LANE), size)
    return ref.at[tuple(idx)]


def _swap_halves_with_sibling(fulls, name, axes):
    n = len(fulls)

    def body(*refs):
        f_refs, got_refs = refs[:n], refs[n:2 * n]
        send_sems, recv_sems = refs[2 * n:]
        x, y, c = _coords()
        copies = []
        for a in range(n):
            copies.append(pltpu.make_async_remote_copy(
                src_ref=_half(f_refs[a], axes[a], 1 - c, 3), dst_ref=got_refs[a], send_sem=send_sems.at[a],
                recv_sem=recv_sems.at[a], device_id=(x, y, 1 - c), device_id_type=MESH))
        for cp in copies:
            cp.start()
        for cp in copies:
            cp.wait()

    def halved(a, axis):
        shape = list(a.shape)
        shape[axis] //= 2
        return jax.ShapeDtypeStruct(tuple(shape), a.dtype)

    return pl.pallas_call(
        body, out_shape=[halved(a, ax) for a, ax in zip(fulls, axes)],
        in_specs=[HBM_REF] * n, out_specs=[HBM_REF] * n,
        scratch_shapes=[pltpu.SemaphoreType.DMA((n,)), pltpu.SemaphoreType.DMA((n,))],
        name=name)(*fulls)


def _join_halves_with_sibling(wholes, axes):
    n = len(wholes)

    def body(*refs):
        out_refs = refs[n:2 * n]
        send_sems, recv_sems = refs[2 * n:]
        x, y, c = _coords()

        def push(a, core):
            half = _half(out_refs[a], axes[a] - 1, core, 2)
            return pltpu.make_async_remote_copy(
                src_ref=half, dst_ref=half, send_sem=send_sems.at[a], recv_sem=recv_sems.at[a],
                device_id=(x, y, 1 - c), device_id_type=MESH)

        for a in range(n):
            push(a, c).start()
        for a in range(n):
            push(a, 1 - c).wait_recv()
        for a in range(n):
            push(a, c).wait_send()

    return pl.pallas_call(
        body, out_shape=[jax.ShapeDtypeStruct(a.shape, a.dtype) for a in wholes],
        in_specs=[HBM_REF] * n, out_specs=[HBM_REF] * n, input_output_aliases={a: a for a in range(n)},
        scratch_shapes=[pltpu.SemaphoreType.DMA((n,)), pltpu.SemaphoreType.DMA((n,))],
        name="rs_pair_join")(*wholes)


def _cols_to_slabs(g):
    rows, cols = g.shape
    return g.reshape(rows, N_CHIP, cols // N_CHIP).transpose(1, 0, 2)


def _slabs_to_cols(w):
    n, rows, cols = w.shape
    return w.transpose(1, 0, 2).reshape(rows, n * cols)


def _col_window(slabs, start, stop):
    n = slabs.shape[2]
    pieces = []
    for k in range(N_CHIP):
        lo, hi = max(start, k * n), min(stop, (k + 1) * n)
        if lo < hi:
            pieces.append(slabs[k][:, lo - k * n:hi - k * n])
    return pieces[0] if len(pieces) == 1 else jnp.concatenate(pieces, axis=1)


def _slabs_from_groups(groups, n):
    slabs = []
    for k in range(N_CHIP):
        pieces, off = [], 0
        for g in groups:
            lo, hi = max(k * n, off), min((k + 1) * n, off + g.shape[0])
            if lo < hi:
                pieces.append(g[lo - off:hi - off])
            off += g.shape[0]
        slabs.append(pieces[0] if len(pieces) == 1 else jnp.concatenate(pieces, axis=0))
    return jnp.stack(slabs)


def _uq_to_padded(w_uq):
    per = w_uq.reshape(RQ, H, DN + DR)
    nope = per[:, :, :DN].reshape(RQ, H * DN)
    rope = jnp.pad(per[:, :, DN:], ((0, 0), (0, 0), (0, LANE - DR))).reshape(RQ, H * LANE)
    return jnp.concatenate([nope, rope], axis=1)


def _uq_from_padded(g):
    nope = g[:, :H * DN].reshape(RQ, H, DN)
    rope = g[:, H * DN:].reshape(RQ, H, LANE)[:, :, :DR]
    return jnp.concatenate([nope, rope], axis=2).reshape(RQ, H * (DN + DR))


def _rope_tables(positions):
    inv_freq = ROPE_THETA ** (-jnp.arange(0, DR, 2, dtype=F32) / DR)
    ang = positions.astype(F32)[:, None] * inv_freq
    cos, sin = jnp.cos(ang), jnp.sin(ang)
    return jnp.tile(cos, (1, 4)), jnp.tile(jnp.concatenate([-sin, sin], axis=1), (1, 2))


def _pair_sums(fulls, core, tag, axes, tr):
    from_sibling = _swap_halves_with_sibling(fulls, f"rs_pair_swap_{tag}", axes)
    return [_add_own_half(f, o, core, min(tr, o.shape[1]), f"add_own_half_{tag}{n}", ax)
            for n, (f, o, ax) in enumerate(zip(fulls, from_sibling, axes))]


def _local_step(x, tgt, cos_t, sin_t, mod, weights, small, tiles, place):
    ts, ts_in, tm_nn, tm_tn, t_attn, chunk = tiles
    wa, wl, wg, later_shards, conv_w = weights
    norm_w, conv_b, ln_w, ln_b, q_norm_w, kv_norm_w, fnw = small
    shift, scale, gate = mod[:, 0:D], mod[:, D:2 * D], mod[:, 2 * D:3 * D]

    h = _adaln_norm(x, norm_w, shift, scale, ts)
    proj_a = _mm_nn(h, wa, tm_nn, D, "proj_a")
    u0, u1, za, (g_uq, g_ukv, g_co, g_ao, g_o) = _conv_fwd(proj_a, conv_w, conv_b, ln_w, ln_b, ts, chunk, later_shards)
    w_uq2, w_ukv = _uq_to_padded(_slabs_to_cols(g_uq)), _slabs_to_cols(g_ukv)
    wco, wao, wo = g_co.reshape(D, D), g_ao.reshape(D, D), g_o.reshape(D, D)
    proj_l = _mm_nn(h, wl, tm_nn, L_COLS, "proj_l")
    proj_g = _mm_nn(h, wg, tm_nn, D, "proj_g")
    qn, kvn, q, k, v = _mla_prep(proj_l, q_norm_w, kv_norm_w, w_uq2, w_ukv, cos_t, sin_t, ts)
    o, lse = _attn_fwd(q, k, v, t_attn)
    (dx2, dza, do, delta, dpg, zb, mg, dmo, dya, dyb, vec_mid) = _middle(
        za, o, proj_g, x, tgt, gate, fnw, wco, wao, wo, ts)
    g_wo = _mm_tn(mg, dmo, tm_tn, D, D, "grad_w_out")
    g_wco = _mm_tn(za, dya, tm_tn, D, D, "grad_w_conv_out")
    g_wao = _mm_tn(zb, dyb, tm_tn, D, D, "grad_w_attn_out")
    dq, dk, dv = _attn_bwd(q, k, v, do, lse, delta, t_attn)
    dpl, g_wuq2, g_wukv, vec_mla = _mla_prep_bwd(
        dq, dk, dv, proj_l, qn, kvn, q_norm_w, kv_norm_w, w_uq2, w_ukv, cos_t, sin_t, ts)

    core = place[1:2]
    nr = D // N_CHIP
    early = [_cols_to_slabs(_uq_from_padded(g_wuq2)), _cols_to_slabs(g_wukv), g_wco.reshape(N_CHIP, nr, D),
             g_wao.reshape(N_CHIP, nr, D), g_wo.reshape(N_CHIP, nr, D)]
    early_sums = _pair_sums(early, core, "a", [1] * len(early), 256)
    dpa, g_conv_w, vec_conv, early_got = _conv_bwd(dza, proj_a, u0, u1, conv_w, ln_w, ln_b, ts, chunk, early_sums)

    g_wa_t = _mm_tn(dpa, h, tm_tn, D, D, "grad_w_in_a")
    g_wl_t = _mm_tn(dpl, h, tm_tn, L_COLS, D, "grad_w_in_l")
    g_wg_t = _mm_tn(dpg, h, tm_tn, D, D, "grad_w_in_g")
    g_w_in_slabs = _slabs_from_groups([g_wa_t, g_wl_t[0:L_COLS_RAW], g_wg_t], IN_COLS // N_CHIP)
    late_sums = _pair_sums([g_w_in_slabs], core, "b", [2], W_IN_ROWS)
    grad_x, vec_in, late_got = _input_bwd(dpa, dpl, dpg, wa, wl, wg, x, dx2, norm_w, scale, ts_in, late_sums)

    axes = [2] + [1] * len(early)
    wholes = [_sum_chip_slabs(a, p, place, min(W_IN_ROWS if ax == 2 else 128, a.shape[1]), f"sum_chip_slabs_{n}", ax)
              for n, (a, p, ax) in enumerate(zip(late_got + early_got, late_sums + early_sums, axes))]
    shards = _join_halves_with_sibling(wholes, axes)

    col_sums = jnp.concatenate(
        [vec_in, vec_mid, vec_conv, jnp.pad(vec_mla, ((0, 0), (0, D - RQ))), g_conv_w], axis=0)
    return grad_x, shards, col_sums


def kernel(x, c, positions, w_ada, b_ada, norm_w, w_in, conv_w, conv_b, conv_ln_w, conv_ln_b, w_conv_out, q_norm_w, w_uq, kv_norm_w, w_ukv, w_attn_out, w_out, final_norm_w, loss_target, m_w_ada, m_b_ada, m_norm_w, m_w_in, m_conv_w, m_conv_b, m_conv_ln_w, m_conv_ln_b, m_w_conv_out, m_q_norm_w, m_w_uq, m_kv_norm_w, m_w_ukv, m_w_attn_out, m_w_out, m_final_norm_w, v_w_ada, v_b_ada, v_norm_w, v_w_in, v_conv_w, v_conv_b, v_conv_ln_w, v_conv_ln_b, v_w_conv_out, v_q_norm_w, v_w_uq, v_kv_norm_w, v_w_ukv, v_w_attn_out, v_w_out, v_final_norm_w):
    ix, iy, ic = _coords()
    chip = 2 * ix + iy
    dev = 4 * ix + 2 * iy + ic
    s = x.shape[1]
    tiles = (256, 512, 1024, 2048, 512, 32)

    conv_w_pad = jnp.pad(conv_w[0], ((0, HALO - KC), (0, 0)))
    small_in = jnp.concatenate([c.reshape(8, LANE), conv_w_pad.reshape(64, LANE)], axis=0)
    small_all = _allgather8(small_in, 72, True, "gather_c_conv").reshape(N_DEV, 72, LANE)
    c_all = small_all[:, 0:8].reshape(N_DEV, D)
    conv_full = jnp.concatenate(
        [small_all[2 * k, 8:72].reshape(HALO, D // N_CHIP) for k in range(N_CHIP)], axis=1)

    (g_in,) = _gather_weights([w_in[0].astype(BF16)])
    wa = _col_window(g_in, 0, A_COLS)
    wl = jnp.pad(_col_window(g_in, A_COLS, A_COLS + L_COLS_RAW), ((0, 0), (0, L_COLS - L_COLS_RAW)))
    wg = _col_window(g_in, A_COLS + L_COLS_RAW, IN_COLS)
    later_shards = [w[0].astype(BF16) for w in (w_uq, w_ukv, w_conv_out, w_attn_out, w_out)]
    weights = (wa, wl, wg, later_shards, conv_full)

    ada_cols = w_ada.shape[2]
    b_shard = lax.dynamic_slice(b_ada, (0, chip * ada_cols), (1, ada_cols))
    mod_part = _ada_fwd(c_all, w_ada[0], b_shard)
    mod_all = _allgather8(mod_part, N_DEV, True, "gather_mod").reshape(N_DEV, N_DEV, ada_cols)
    mod = jnp.concatenate(
        [lax.dynamic_slice(mod_all[2 * k], (dev, 0), (1, ada_cols)) for k in range(N_CHIP)], axis=1)

    cos_t, sin_t = _rope_tables(positions[0])
    small = (norm_w, conv_b, conv_ln_w, conv_ln_b, q_norm_w, kv_norm_w, final_norm_w.reshape(1, D))
    place = jnp.stack([chip, ic]).astype(jnp.int32)
    grad_x, shards, col_sums = _local_step(x[0], loss_target[0], cos_t, sin_t, mod, weights, small, tiles, place)
    g_w_in_s, g_w_uq_s, g_w_ukv_s, g_wco_s, g_wao_s, g_wo_s = shards

    gathered = _allgather8(col_sums, SUM_ROWS, True, "gather_small_grads").reshape(N_DEV, SUM_ROWS, D)
    vec_names = ("b_ada", "norm_w", "conv_b", "conv_ln_w", "conv_ln_b", "q_norm_w", "kv_norm_w", "final_norm_w")
    row = lambda a: a.reshape(1, -1)
    vectors = [(row(b_ada), row(m_b_ada), row(v_b_ada)), (norm_w, m_norm_w, v_norm_w), (conv_b, m_conv_b, v_conv_b),
               (conv_ln_w, m_conv_ln_w, v_conv_ln_w), (conv_ln_b, m_conv_ln_b, v_conv_ln_b),
               (q_norm_w, m_q_norm_w, v_q_norm_w), (kv_norm_w, m_kv_norm_w, v_kv_norm_w),
               (row(final_norm_w), row(m_final_norm_w), row(v_final_norm_w))]
    fin = _small_finalize(gathered, vectors, (conv_w, m_conv_w, v_conv_w), place[0:1])
    res = {}
    for p, (name, (w, _, _)) in enumerate(zip(vec_names, vectors)):
        shape = final_norm_w.shape if name == "final_norm_w" else w.shape
        res[name] = tuple(a.reshape(shape) for a in fin[4 * p:4 * p + 4])
    res["conv_w"] = tuple(fin[4 * len(vectors):4 * len(vectors) + 4])
    dmod_all, loss = fin[-2], fin[-1].reshape(())
    dmod_shard = lax.dynamic_slice(dmod_all, (0, chip * ada_cols), (N_DEV, ada_cols))
    g_w_ada = _ada_bwd(c_all.T, dmod_shard).reshape(1, D, ada_cols)

    def big(w, g, m, v, tr, name):
        d, nm, nv = _adamw(w, g, m, v, tr, name)
        return g.reshape(w.shape), d, nm, nv

    res["w_ada"] = big(w_ada, g_w_ada[0], m_w_ada, v_w_ada, 256, "adamw_w_ada")
    t_in = [a[0].T for a in (w_in, m_w_in, v_w_in)]
    d_t, nm_t, nv_t = _adamw(t_in[0], g_w_in_s, t_in[1], t_in[2], W_IN_ROWS, "adamw_w_in")
    res["w_in"] = tuple(a.T[None] for a in (g_w_in_s, d_t, nm_t, nv_t))
    res["w_conv_out"] = big(w_conv_out, g_wco_s, m_w_conv_out, v_w_conv_out, 256, "adamw_w_conv_out")
    res["w_uq"] = big(w_uq, g_w_uq_s, m_w_uq, v_w_uq, 256, "adamw_w_uq")
    res["w_ukv"] = big(w_ukv, g_w_ukv_s, m_w_ukv, v_w_ukv, 256, "adamw_w_ukv")
    res["w_attn_out"] = big(w_attn_out, g_wao_s, m_w_attn_out, v_w_attn_out, 256, "adamw_w_attn_out")
    res["w_out"] = big(w_out, g_wo_s, m_w_out, v_w_out, 256, "adamw_w_out")

    order = ("w_ada", "b_ada", "norm_w", "w_in", "conv_w", "conv_b", "conv_ln_w", "conv_ln_b", "w_conv_out",
             "q_norm_w", "w_uq", "kv_norm_w", "w_ukv", "w_attn_out", "w_out", "final_norm_w")
    outs = [loss, grad_x[None]]
    for slot in range(4):
        outs += [res[name][slot] for name in order]
    return tuple(outs)
```

```python
import functools

import numpy as np
import jax
import jax.numpy as jnp
from jax import lax
from jax.experimental import pallas as pl
from jax.experimental.pallas import tpu as pltpu

F32 = jnp.float32
BF16 = jnp.bfloat16
MESH = pl.DeviceIdType.MESH

D = 1024
H = 8
DN = 128
DR = 64
RQ = 256
KC = 31
HALO = 32
EPS = 1e-6
ROPE_THETA = 10000.0
N_CHIP = 4
N_DEV = 8
LANE = 128
VMEM_BIG = 56 * 1024 * 1024

ADAM_LR = 0.001
ADAM_B1 = 0.9
ADAM_B2 = 0.999
ADAM_EPS = 1e-08
ADAM_WD = 0.01
ADAM_STEP = 10

A_COLS = 3 * D
L_COLS_RAW = RQ + RQ + DR
L_COLS = 640
G_COLS = 3 * D
IN_COLS = A_COLS + L_COLS_RAW + G_COLS


def _params(sem=None, vmem=None):
    kw = {}
    if sem is not None:
        kw["dimension_semantics"] = sem
    if vmem is not None:
        kw["vmem_limit_bytes"] = vmem
    return pltpu.CompilerParams(**kw)


def _dot(a, b):
    return jnp.dot(a, b, preferred_element_type=F32)


def _dot_nt(a, b):
    return lax.dot_general(a, b, (((1,), (1,)), ((), ())), preferred_element_type=F32)


def _dot_tn(a, b):
    return lax.dot_general(a, b, (((0,), (0,)), ((), ())), preferred_element_type=F32)


def _colsum(v):
    return jnp.sum(v, axis=0, keepdims=True)


def _rowmean(v):
    return jnp.mean(v, axis=-1, keepdims=True)


def _sigmoid(v):
    return jax.nn.sigmoid(v)


def _dsilu(v, s):
    return s * (1.0 + v * (1.0 - s))


def _swap_halves(v, first_half):
    return jnp.where(first_half, pltpu.roll(v, 96, 1), pltpu.roll(v, 32, 1))


def _first_half_mask(rows):
    lane = lax.broadcasted_iota(jnp.int32, (rows, LANE), 1)
    return (lane % 64) < 32


def _adaln_norm(x, norm_w, shift, scale, ts):
    s = x.shape[0]

    def body(x_ref, nw_ref, sh_ref, sc_ref, h_ref):
        xv = x_ref[...]
        r = lax.rsqrt(_rowmean(xv * xv) + EPS)
        y = xv * r * nw_ref[...]
        h_ref[...] = (y * (1.0 + sc_ref[...]) + sh_ref[...]).astype(BF16)

    row = pl.BlockSpec((ts, D), lambda i: (i, 0))
    vec = pl.BlockSpec((1, D), lambda i: (0, 0))
    return pl.pallas_call(
        body, grid=(s // ts,), in_specs=[row, vec, vec, vec], out_specs=row,
        out_shape=jax.ShapeDtypeStruct((s, D), BF16), name="adaln_norm",
        compiler_params=_params(("parallel",)))(x, norm_w, shift, scale)


def _mm_nn(a, b, tm, tn, name):
    m, k = a.shape
    n = b.shape[1]

    def body(a_ref, b_ref, o_ref):
        o_ref[...] = _dot(a_ref[...], b_ref[...])

    return pl.pallas_call(
        body, grid=(n // tn, m // tm),
        in_specs=[pl.BlockSpec((tm, k), lambda j, i: (i, 0)), pl.BlockSpec((k, tn), lambda j, i: (0, j))],
        out_specs=pl.BlockSpec((tm, tn), lambda j, i: (i, j)),
        out_shape=jax.ShapeDtypeStruct((m, n), F32), name=name,
        compiler_params=_params(("parallel", "parallel"), VMEM_BIG))(a, b)


def _mm_tn(a, b, tm, tk, tn, name, out_dtype=F32):
    m, k = a.shape
    n = b.shape[1]
    steps = m // tm

    def body(a_ref, b_ref, o_ref, acc_ref):
        @pl.when(pl.program_id(2) == 0)
        def _():
            acc_ref[...] = jnp.zeros_like(acc_ref)
        acc_ref[...] += _dot_tn(a_ref[...], b_ref[...])

        @pl.when(pl.program_id(2) == steps - 1)
        def _():
            o_ref[...] = acc_ref[...].astype(out_dtype)

    return pl.pallas_call(
        body, grid=(k // tk, n // tn, steps),
        in_specs=[pl.BlockSpec((tm, tk), lambda r, j, i: (i, r)), pl.BlockSpec((tm, tn), lambda r, j, i: (i, j))],
        out_specs=pl.BlockSpec((tk, tn), lambda r, j, i: (r, j)),
        out_shape=jax.ShapeDtypeStruct((k, n), out_dtype), scratch_shapes=[pltpu.VMEM((tk, tn), F32)], name=name,
        compiler_params=_params(("parallel", "parallel", "arbitrary"), VMEM_BIG))(a, b)


def _coords():
    return lax.axis_index("x"), lax.axis_index("y"), lax.axis_index("c")


HBM_REF = pl.BlockSpec(memory_space=pl.ANY)


def _chip_scatter_copies(p_refs, got_refs, send_sems, recv_sems):
    x, y, c = _coords()
    copies = []
    for a in range(len(p_refs)):
        for j, (px, py) in enumerate([(1 - x, y), (x, 1 - y), (1 - x, 1 - y)]):
            copies.append(pltpu.make_async_remote_copy(
                src_ref=p_refs[a].at[2 * px + py], dst_ref=got_refs[a].at[j], send_sem=send_sems.at[3 * a + j],
                recv_sem=recv_sems.at[3 * a + j], device_id=(px, py, c), device_id_type=MESH))
    return copies


RELATIONS = [(dx, dy, dc) for dx in (0, 1) for dy in (0, 1) for dc in (0, 1)][1:]


def _device_scatter_copies(p_refs, got_refs, send_sems, recv_sems):
    x, y, c = _coords()
    copies = []
    for a in range(len(p_refs)):
        half = p_refs[a].shape[1] // 2
        for j, (dx, dy, dc) in enumerate(RELATIONS):
            px, py, pc = (1 - x if dx else x), (1 - y if dy else y), (1 - c if dc else c)
            src = p_refs[a].at[2 * px + py, pl.ds(pl.multiple_of(pc * half, 16), half), :]
            copies.append(pltpu.make_async_remote_copy(
                src_ref=src, dst_ref=got_refs[a].at[j], send_sem=send_sems.at[7 * a + j],
                recv_sem=recv_sems.at[7 * a + j], device_id=(px, py, pc), device_id_type=MESH))
    return copies


def _scatter_alongside(body, n_in, n_out, n_parts, last_step, make_copies):
    def wrapped(*refs):
        ins, parts = refs[:n_in], refs[n_in:n_in + n_parts]
        rest = refs[n_in + n_parts:]
        outs, got = rest[:n_out], rest[n_out:n_out + n_parts]
        scratch, (send_sems, recv_sems) = rest[n_out + n_parts:-2], rest[-2:]

        @pl.when(pl.program_id(0) == 0)
        def _():
            for cp in make_copies(parts, got, send_sems, recv_sems):
                cp.start()

        body(*ins, *outs, *scratch)

        @pl.when(pl.program_id(0) == last_step)
        def _():
            for cp in make_copies(parts, got, send_sems, recv_sems):
                cp.wait()

    return wrapped


def _scatter_operands(parts, per_device):
    n = len(parts)
    if per_device:
        slots, shapes = 7, [jax.ShapeDtypeStruct((7, a.shape[1] // 2, a.shape[2]), a.dtype) for a in parts]
    else:
        slots, shapes = 3, [jax.ShapeDtypeStruct((3,) + a.shape[1:], a.dtype) for a in parts]
    sems = [pltpu.SemaphoreType.DMA((slots * n,)), pltpu.SemaphoreType.DMA((slots * n,))]
    return [HBM_REF] * n, [HBM_REF] * n, shapes, sems


def _shifted_copies(win_ref, sh_ref, rows):
    for p in range(1, 8):
        sh_ref[p - 1, 0:rows, :] = win_ref[pl.ds(p, rows), :]


def _tap_rows(win_ref, sh_ref, start, rows):
    p = start % 8
    if p == 0:
        return win_ref[pl.ds(start, rows), :]
    return sh_ref[p - 1, pl.ds(start - p, rows), :]


def _conv_taps(win_ref, sh_ref, w_ref, rows, chunk, offset_of_tap):
    pieces = []
    for c0 in range(0, rows, chunk):
        acc = None
        for j in range(KC):
            term = w_ref[j:j + 1, :] * _tap_rows(win_ref, sh_ref, c0 + offset_of_tap(j), chunk)
            acc = term if acc is None else acc + term
        pieces.append(acc)
    return pieces


def _conv_fwd(proj_a, conv_w, conv_b, ln_w, ln_b, ts, chunk, shards):
    s = proj_a.shape[0]

    def body(av_ref, al_ref, ag_ref, w_ref, b_ref, lw_ref, lb_ref, u0_ref, u1_ref, za_ref, win_ref, sh_ref):
        @pl.when(pl.program_id(0) == 0)
        def _():
            win_ref[0:HALO, :] = jnp.zeros((HALO, D), F32)

        u0 = av_ref[...] * _sigmoid(al_ref[...])
        u0_ref[...] = u0
        win_ref[HALO:HALO + ts, :] = u0
        _shifted_copies(win_ref, sh_ref, ts + HALO - 8)
        pieces = _conv_taps(win_ref, sh_ref, w_ref, ts, chunk, lambda j: HALO - (KC - 1) + j)
        for n, acc in enumerate(pieces):
            u1_ref[n * chunk:(n + 1) * chunk, :] = acc + b_ref[...]
        win_ref[0:HALO, :] = win_ref[ts:ts + HALO, :]

        u1 = u1_ref[...]
        xc = u1 - _rowmean(u1)
        rstd = lax.rsqrt(_rowmean(xc * xc) + EPS)
        u2 = xc * rstd * lw_ref[...] + lb_ref[...]
        u3 = u2 * _sigmoid(u2)
        ag = ag_ref[...]
        za_ref[...] = (u3 * (ag * _sigmoid(ag))).astype(BF16)

    col = lambda c: pl.BlockSpec((ts, D), lambda i, c=c: (i, c))
    row = pl.BlockSpec((ts, D), lambda i: (i, 0))
    vec = pl.BlockSpec((1, D), lambda i: (0, 0))
    n = len(shards)
    gathered_shapes, sems = _gather_operands(shards)
    outs = pl.pallas_call(
        _gather_alongside(body, 7, 3, n, s // ts - 1), grid=(s // ts,),
        in_specs=[col(0), col(1), col(2), pl.BlockSpec((HALO, D), lambda i: (0, 0)), vec, vec, vec] + [HBM_REF] * n,
        out_specs=[row, row, row] + [HBM_REF] * n,
        out_shape=[jax.ShapeDtypeStruct((s, D), F32), jax.ShapeDtypeStruct((s, D), F32),
                   jax.ShapeDtypeStruct((s, D), BF16)] + gathered_shapes,
        scratch_shapes=[pltpu.VMEM((ts + HALO, D), F32), pltpu.VMEM((7, ts + HALO, D), F32)] + sems,
        name="conv_fwd", compiler_params=_params(("arbitrary",), VMEM_BIG))(
            proj_a, proj_a, proj_a, conv_w, conv_b, ln_w, ln_b, *shards)
    return outs[0], outs[1], outs[2], _as_chip_slabs(outs[3:], shards)


def _conv_bwd(dza, proj_a, u0, u1, conv_w, ln_w, ln_b, ts, chunk, parts):
    s = dza.shape[0]
    nt = s // ts
    per = ts // HALO

    def body(dza_ref, av_ref, al_ref, ag_ref, u0_ref, u0p_ref, u1_ref, w_ref, lw_ref, lb_ref,
             dpa_ref, gw_ref, gv_ref, dwin_ref, uwin_ref, du0_ref, gwp_ref, dsh_ref, ush_ref):
        step = pl.program_id(0)
        tile = nt - 1 - step

        @pl.when(step == 0)
        def _():
            dwin_ref[ts:ts + HALO, :] = jnp.zeros((HALO, D), F32)
            gwp_ref[...] = jnp.zeros_like(gwp_ref)
            gv_ref[...] = jnp.zeros_like(gv_ref)

        ag = ag_ref[...]
        sg = _sigmoid(ag)
        u1 = u1_ref[...]
        xc = u1 - _rowmean(u1)
        rstd = lax.rsqrt(_rowmean(xc * xc) + EPS)
        xh = xc * rstd
        u2 = xh * lw_ref[...] + lb_ref[...]
        s2 = _sigmoid(u2)
        dz = dza_ref[...]
        du3 = dz * (ag * sg)
        dpa_ref[:, 2 * D:3 * D] = (dz * (u2 * s2) * _dsilu(ag, sg)).astype(BF16)
        du2 = du3 * _dsilu(u2, s2)
        gv_ref[0:1, :] += _colsum(du2 * xh)
        gv_ref[1:2, :] += _colsum(du2)
        dxh = du2 * lw_ref[...]
        du1 = rstd * (dxh - _rowmean(dxh) - xh * _rowmean(dxh * xh))
        gv_ref[2:3, :] += _colsum(du1)
        dwin_ref[0:ts, :] = du1

        uwin_ref[0:HALO, :] = jnp.where(tile == 0, 0.0, u0p_ref[...])
        uwin_ref[HALO:HALO + ts, :] = u0_ref[...]

        _shifted_copies(dwin_ref, dsh_ref, ts + HALO - 8)
        _shifted_copies(uwin_ref, ush_ref, ts + HALO - 8)
        pieces = _conv_taps(dwin_ref, dsh_ref, w_ref, ts, chunk, lambda j: (KC - 1) - j)
        for n, acc in enumerate(pieces):
            du0_ref[n * chunk:(n + 1) * chunk, :] = acc
        for c0 in range(0, ts, chunk):
            dchunk = dwin_ref[c0:c0 + chunk, :]
            for j in range(KC):
                prod = dchunk * _tap_rows(uwin_ref, ush_ref, c0 + HALO - (KC - 1) + j, chunk)
                gwp_ref[8 * j:8 * j + 8, :] += jnp.sum(prod.reshape(chunk // 8, 8, D), axis=0)
        dwin_ref[ts:ts + HALO, :] = dwin_ref[0:HALO, :]

        du0 = du0_ref[...]
        al = al_ref[...]
        sl = _sigmoid(al)
        dpa_ref[:, 0:D] = (du0 * sl).astype(BF16)
        dpa_ref[:, D:2 * D] = (du0 * av_ref[...] * sl * (1.0 - sl)).astype(BF16)

        @pl.when(step == nt - 1)
        def _():
            for j in range(KC):
                gw_ref[j:j + 1, :] = _colsum(gwp_ref[8 * j:8 * j + 8, :])
            gw_ref[KC:HALO, :] = jnp.zeros((HALO - KC, D), F32)

    rev = lambda i: nt - 1 - i
    col = lambda c: pl.BlockSpec((ts, D), lambda i, c=c: (rev(i), c))
    row = pl.BlockSpec((ts, D), lambda i: (rev(i), 0))
    vec = pl.BlockSpec((1, D), lambda i: (0, 0))
    halo = pl.BlockSpec((HALO, D), lambda i: (jnp.maximum(rev(i) * per - 1, 0), 0))
    side_in, side_out, side_shapes, side_sems = _scatter_operands(parts, True)
    outs = pl.pallas_call(
        _scatter_alongside(body, 10, 3, len(parts), nt - 1, _device_scatter_copies), grid=(nt,),
        in_specs=[row, col(0), col(1), col(2), row, halo, row, pl.BlockSpec((HALO, D), lambda i: (0, 0)), vec, vec]
        + side_in,
        out_specs=[pl.BlockSpec((ts, A_COLS), lambda i: (rev(i), 0)),
                   pl.BlockSpec((HALO, D), lambda i: (0, 0)), pl.BlockSpec((8, D), lambda i: (0, 0))] + side_out,
        out_shape=[jax.ShapeDtypeStruct((s, A_COLS), BF16), jax.ShapeDtypeStruct((HALO, D), F32),
                   jax.ShapeDtypeStruct((8, D), F32)] + side_shapes,
        scratch_shapes=[pltpu.VMEM((ts + HALO, D), F32), pltpu.VMEM((ts + HALO, D), F32),
                        pltpu.VMEM((ts, D), F32), pltpu.VMEM((8 * HALO, D), F32),
                        pltpu.VMEM((7, ts + HALO, D), F32), pltpu.VMEM((7, ts + HALO, D), F32)] + side_sems,
        name="conv_bwd", compiler_params=_params(("arbitrary",), VMEM_BIG))(
            dza, proj_a, proj_a, proj_a, u0, u0, u1, conv_w, ln_w, ln_b, *parts)
    return outs[0], outs[1], outs[2], list(outs[3:])


def _mla_prep(proj_l, q_norm_w, kv_norm_w, w_uq2, w_ukv, cos_t, sin_t, ts):
    s = proj_l.shape[0]

    def body(pl_ref, qw_ref, kw_ref, wq_ref, wkv_ref, c_ref, s_ref, qn_ref, kvn_ref, q_ref, k_ref, v_ref):
        first = _first_half_mask(ts)
        cs = c_ref[...]
        sn = s_ref[...]

        def rms(v, w):
            return v * lax.rsqrt(_rowmean(v * v) + EPS) * w

        def rope(v):
            return v * cs + _swap_halves(v, first) * sn

        qn = rms(pl_ref[:, 0:RQ], qw_ref[...]).astype(BF16)
        kvn = rms(pl_ref[:, RQ:2 * RQ], kw_ref[...]).astype(BF16)
        qn_ref[...] = qn
        kvn_ref[...] = kvn
        q = _dot(qn, wq_ref[...])
        kv = _dot(kvn, wkv_ref[...])
        kr = rope(pl_ref[:, 2 * RQ:2 * RQ + LANE]).astype(BF16)
        for h in range(H):
            q_ref[h, :, 0:DN] = q[:, DN * h:DN * (h + 1)].astype(BF16)
            q_ref[h, :, DN:2 * DN] = rope(q[:, H * DN + LANE * h:H * DN + LANE * (h + 1)]).astype(BF16)
            k_ref[h, :, 0:DN] = kv[:, 2 * DN * h:2 * DN * h + DN].astype(BF16)
            k_ref[h, :, DN:2 * DN] = kr
            v_ref[h, :, 0:DN] = kv[:, 2 * DN * h + DN:2 * DN * (h + 1)].astype(BF16)
            v_ref[h, :, DN:2 * DN] = jnp.ones((ts, DN), BF16)

    const = lambda shape: pl.BlockSpec(shape, lambda i: (0,) * len(shape))
    rowb = lambda w: pl.BlockSpec((ts, w), lambda i: (i, 0))
    head = lambda w: pl.BlockSpec((H, ts, w), lambda i: (0, i, 0))
    return pl.pallas_call(
        body, grid=(s // ts,),
        in_specs=[rowb(L_COLS), const((1, RQ)), const((1, RQ)), const((RQ, 2 * H * DN)), const((RQ, 2 * H * DN)),
                  rowb(LANE), rowb(LANE)],
        out_specs=[rowb(RQ), rowb(RQ), head(2 * DN), head(2 * DN), head(2 * DN)],
        out_shape=[jax.ShapeDtypeStruct((s, RQ), BF16), jax.ShapeDtypeStruct((s, RQ), BF16),
                   jax.ShapeDtypeStruct((H, s, 2 * DN), BF16), jax.ShapeDtypeStruct((H, s, 2 * DN), BF16),
                   jax.ShapeDtypeStruct((H, s, 2 * DN), BF16)],
        name="mla_prep", compiler_params=_params(("parallel",)))(
            proj_l, q_norm_w, kv_norm_w, w_uq2, w_ukv, cos_t, sin_t)


def _mla_prep_bwd(dq, dk, dv, proj_l, qn, kvn, q_norm_w, kv_norm_w, w_uq2, w_ukv, cos_t, sin_t, ts):
    s = proj_l.shape[0]

    def body(dq_ref, dk_ref, dv_ref, pl_ref, qn_ref, kvn_ref, qw_ref, kw_ref, wq_ref, wkv_ref, c_ref, s_ref,
             dpl_ref, gwq_ref, gwkv_ref, gv_ref, dq2_ref, dkv2_ref):
        @pl.when(pl.program_id(0) == 0)
        def _():
            gwq_ref[...] = jnp.zeros_like(gwq_ref)
            gwkv_ref[...] = jnp.zeros_like(gwkv_ref)
            gv_ref[...] = jnp.zeros_like(gv_ref)

        first = _first_half_mask(ts)
        cs = c_ref[...]
        sn = s_ref[...]

        def rope_bwd(g):
            return g * cs + _swap_halves(g * sn, first)

        def rms_bwd(v, w, dy):
            r = lax.rsqrt(_rowmean(v * v) + EPS)
            vh = v * r
            dvh = dy * w
            return r * (dvh - vh * _rowmean(dvh * vh)), _colsum(dy * vh)

        dkr = None
        for h in range(H):
            dq2_ref[:, DN * h:DN * (h + 1)] = dq_ref[h, :, 0:DN].astype(BF16)
            dq2_ref[:, H * DN + LANE * h:H * DN + LANE * (h + 1)] = rope_bwd(dq_ref[h, :, DN:2 * DN]).astype(BF16)
            dkv2_ref[:, 2 * DN * h:2 * DN * h + DN] = dk_ref[h, :, 0:DN].astype(BF16)
            dkv2_ref[:, 2 * DN * h + DN:2 * DN * (h + 1)] = dv_ref[h].astype(BF16)
            part = dk_ref[h, :, DN:2 * DN]
            dkr = part if dkr is None else dkr + part

        dq2 = dq2_ref[...]
        dkv2 = dkv2_ref[...]
        gwq_ref[...] += _dot_tn(qn_ref[...], dq2)
        gwkv_ref[...] += _dot_tn(kvn_ref[...], dkv2)
        dcq, gq = rms_bwd(pl_ref[:, 0:RQ], qw_ref[...], _dot_nt(dq2, wq_ref[...]))
        dckv, gkv = rms_bwd(pl_ref[:, RQ:2 * RQ], kw_ref[...], _dot_nt(dkv2, wkv_ref[...]))
        gv_ref[0:1, :] += gq
        gv_ref[1:2, :] += gkv
        dpl_ref[:, 0:RQ] = dcq.astype(BF16)
        dpl_ref[:, RQ:2 * RQ] = dckv.astype(BF16)
        dpl_ref[:, 2 * RQ:2 * RQ + LANE] = rope_bwd(dkr).astype(BF16)

    const = lambda shape: pl.BlockSpec(shape, lambda i: (0,) * len(shape))
    rowb = lambda w: pl.BlockSpec((ts, w), lambda i: (i, 0))
    head = lambda w: pl.BlockSpec((H, ts, w), lambda i: (0, i, 0))
    return pl.pallas_call(
        body, grid=(s // ts,),
        in_specs=[head(2 * DN), head(2 * DN), head(DN), rowb(L_COLS), rowb(RQ), rowb(RQ), const((1, RQ)),
                  const((1, RQ)), const((RQ, 2 * H * DN)), const((RQ, 2 * H * DN)), rowb(LANE), rowb(LANE)],
        out_specs=[rowb(L_COLS), const((RQ, 2 * H * DN)), const((RQ, 2 * H * DN)), const((8, RQ))],
        out_shape=[jax.ShapeDtypeStruct((s, L_COLS), BF16), jax.ShapeDtypeStruct((RQ, 2 * H * DN), F32),
                   jax.ShapeDtypeStruct((RQ, 2 * H * DN), F32), jax.ShapeDtypeStruct((8, RQ), F32)],
        scratch_shapes=[pltpu.VMEM((ts, 2 * H * DN), BF16), pltpu.VMEM((ts, 2 * H * DN), BF16)],
        name="mla_prep_bwd", compiler_params=_params(("arbitrary",), VMEM_BIG))(
            dq, dk, dv, proj_l, qn, kvn, q_norm_w, kv_norm_w, w_uq2, w_ukv, cos_t, sin_t)


def _causal_pairs(n, by_key):
    if by_key:
        pairs = [(i, j) for j in range(n) for i in range(j, n)]
    else:
        pairs = [(i, j) for i in range(n) for j in range(i + 1)]
    return (jnp.asarray(np.array([p[0] for p in pairs], np.int32)),
            jnp.asarray(np.array([p[1] for p in pairs], np.int32)))


LOG2E = 1.4426950408889634
LN2 = 0.6931471805599453
ATT_HEADS_FWD = 4
ATT_HEADS = 2
W_IN_ROWS = 336
ATT_ROWS = 64


def _diag_width(r0, t):
    return min(t, -(-(r0 + ATT_ROWS) // LANE) * LANE)


def _diag_mask_rows(r0, width):
    rows = r0 + lax.broadcasted_iota(jnp.int32, (ATT_ROWS, width), 0)
    cols = lax.broadcasted_iota(jnp.int32, (ATT_ROWS, width), 1)
    return cols <= rows


def _diag_mask(t):
    return lax.broadcasted_iota(jnp.int32, (t, t), 1) <= lax.broadcasted_iota(jnp.int32, (t, t), 0)


def _attn_fwd(q, k, v, t):
    s = q.shape[1]
    n = s // t
    scale2 = float((DN + DR) ** -0.5) * LOG2E
    qi, ki = _causal_pairs(n, by_key=False)

    def body(qi_ref, ki_ref, q_ref, k_ref, v_ref, o_ref, lse_ref, *scratch):
        per_head = [scratch[5 * h:5 * h + 5] for h in range(ATT_HEADS_FWD)]
        p = pl.program_id(1)
        i = qi_ref[p]
        j = ki_ref[p]

        @pl.when(j == 0)
        def _():
            for m_sc, acc_sc, _, _, _ in per_head:
                m_sc[...] = jnp.full_like(m_sc, -jnp.inf)
                acc_sc[...] = jnp.zeros_like(acc_sc)

        def scores(h, diag):
            sc = _dot_nt(q_ref[h], k_ref[h])
            if diag:
                sc = jnp.where(_diag_mask(t), sc, -jnp.inf)
            per_head[h][2][...] = sc

        def rowmax(h, rows):
            per_head[h][4][rows, :] = jnp.max(per_head[h][2][rows, :], axis=-1, keepdims=True)

        def stats(h):
            m_sc, acc_sc, _, _, mx_sc = per_head[h]
            m_prev = m_sc[...]
            m_new = jnp.maximum(m_prev, mx_sc[...] * scale2)
            m_sc[...] = m_new
            acc_sc[...] = jnp.exp2(m_prev - m_new) * acc_sc[...]

        def probs(h, rows):
            m_sc, _, s_sc, p_sc, _ = per_head[h]
            p_sc[rows, :] = jnp.exp2(s_sc[rows, :] * scale2 - m_sc[rows, :]).astype(BF16)

        def values(h):
            _, acc_sc, _, p_sc, _ = per_head[h]
            acc_sc[...] += _dot(p_sc[...], v_ref[h])

        def step(diag):
            blocks = [slice(r0, r0 + ATT_ROWS) for r0 in range(0, t, ATT_ROWS)]
            for h in range(ATT_HEADS_FWD):
                scores(h, diag)
            for rows in blocks:
                rowmax(0, rows)
            stats(0)
            for h in range(ATT_HEADS_FWD):
                for rows in blocks:
                    probs(h, rows)
                    if h + 1 < ATT_HEADS_FWD:
                        rowmax(h + 1, rows)
                if h + 1 < ATT_HEADS_FWD:
                    stats(h + 1)
                values(h)

        @pl.when(j < i)
        def _():
            step(False)

        @pl.when(j == i)
        def _():
            step(True)
            for h, (m_sc, acc_sc, _, _, _) in enumerate(per_head):
                l = acc_sc[:, DN:2 * DN]
                o_ref[:, DN * h:DN * (h + 1)] = acc_sc[:, 0:DN] / l
                lse_ref[h] = (m_sc[...] + jnp.log2(l[:, 0:1])) * LN2

    hb = ATT_HEADS_FWD
    grid_spec = pltpu.PrefetchScalarGridSpec(
        num_scalar_prefetch=2, grid=(H // hb, int(qi.shape[0])),
        in_specs=[pl.BlockSpec((hb, t, 2 * DN), lambda h, p, qi, ki: (h, qi[p], 0)),
                  pl.BlockSpec((hb, t, 2 * DN), lambda h, p, qi, ki: (h, ki[p], 0)),
                  pl.BlockSpec((hb, t, 2 * DN), lambda h, p, qi, ki: (h, ki[p], 0))],
        out_specs=[pl.BlockSpec((t, hb * DN), lambda h, p, qi, ki: (qi[p], h)),
                   pl.BlockSpec((hb, t, 1), lambda h, p, qi, ki: (h, qi[p], 0))],
        scratch_shapes=[pltpu.VMEM((t, 1), F32), pltpu.VMEM((t, 2 * DN), F32), pltpu.VMEM((t, t), F32),
                        pltpu.VMEM((t, t), BF16), pltpu.VMEM((t, 1), F32)] * hb)
    return pl.pallas_call(
        body, grid_spec=grid_spec,
        out_shape=[jax.ShapeDtypeStruct((s, H * DN), F32), jax.ShapeDtypeStruct((H, s, 1), F32)],
        name="attn_fwd", compiler_params=_params(("parallel", "arbitrary"), VMEM_BIG))(qi, ki, q, k, v)


def _attn_bwd(q, k, v, do, lse, delta, t):
    s = q.shape[1]
    n = s // t
    scale = float((DN + DR) ** -0.5)
    qi, ki = _causal_pairs(n, by_key=True)

    def body(qi_ref, ki_ref, q_ref, k_ref, v_ref, do_ref, lse_ref, dl_ref, dq_ref, dk_ref, dv_ref,
             dk_sc, dv_sc, s_sc, dp_sc, p_sc, ds_sc):
        p = pl.program_id(1)
        i = qi_ref[p]
        j = ki_ref[p]

        @pl.when(p == 0)
        def _():
            dq_ref[...] = jnp.zeros_like(dq_ref)

        @pl.when(i == j)
        def _():
            dk_sc[...] = jnp.zeros_like(dk_sc)
            dv_sc[...] = jnp.zeros_like(dv_sc)

        def step(diag):
            for h in range(ATT_HEADS):
                s_sc[h] = _dot_nt(q_ref[h], k_ref[h])
                dp_sc[h] = _dot_nt(do_ref[:, DN * h:DN * (h + 1)], v_ref[h, :, 0:DN])
            for h in range(ATT_HEADS):
                for r0 in range(0, t, ATT_ROWS):
                    rows = slice(r0, r0 + ATT_ROWS)
                    width = _diag_width(r0, t) if diag else t
                    sc = s_sc[h, rows, 0:width] * (scale * LOG2E)
                    if diag:
                        sc = jnp.where(_diag_mask_rows(r0, width), sc, -jnp.inf)
                    pr = jnp.exp2(sc - lse_ref[h, rows, :] * LOG2E)
                    ds = pr * (dp_sc[h, rows, 0:width] - dl_ref[h, rows, :]) * scale
                    p_sc[h, rows, 0:width] = pr.astype(BF16)
                    ds_sc[h, rows, 0:width] = ds.astype(BF16)
                    if width < t:
                        p_sc[h, rows, width:t] = jnp.zeros((ATT_ROWS, t - width), BF16)
                        ds_sc[h, rows, width:t] = jnp.zeros((ATT_ROWS, t - width), BF16)
            q_rows = pl.ds(pl.multiple_of(i * t, t), t)
            for h in range(ATT_HEADS):
                dv_sc[h] += _dot_tn(p_sc[h], do_ref[:, DN * h:DN * (h + 1)])
                dk_sc[h] += _dot_tn(ds_sc[h], q_ref[h])
                dq_ref[h, q_rows, :] += _dot(ds_sc[h], k_ref[h])

        @pl.when(i > j)
        def _():
            step(False)

        @pl.when(i == j)
        def _():
            step(True)

        @pl.when(i == n - 1)
        def _():
            dk_ref[...] = dk_sc[...]
            dv_ref[...] = dv_sc[...]

    hb = ATT_HEADS
    grid_spec = pltpu.PrefetchScalarGridSpec(
        num_scalar_prefetch=2, grid=(H // hb, int(qi.shape[0])),
        in_specs=[pl.BlockSpec((hb, t, 2 * DN), lambda h, p, qi, ki: (h, qi[p], 0)),
                  pl.BlockSpec((hb, t, 2 * DN), lambda h, p, qi, ki: (h, ki[p], 0)),
                  pl.BlockSpec((hb, t, 2 * DN), lambda h, p, qi, ki: (h, ki[p], 0)),
                  pl.BlockSpec((t, hb * DN), lambda h, p, qi, ki: (qi[p], h)),
                  pl.BlockSpec((hb, t, 1), lambda h, p, qi, ki: (h, qi[p], 0)),
                  pl.BlockSpec((hb, t, 1), lambda h, p, qi, ki: (h, qi[p], 0))],
        out_specs=[pl.BlockSpec((hb, s, 2 * DN), lambda h, p, qi, ki: (h, 0, 0)),
                   pl.BlockSpec((hb, t, 2 * DN), lambda h, p, qi, ki: (h, ki[p], 0)),
                   pl.BlockSpec((hb, t, DN), lambda h, p, qi, ki: (h, ki[p], 0))],
        scratch_shapes=[pltpu.VMEM((hb, t, 2 * DN), F32), pltpu.VMEM((hb, t, DN), F32),
                        pltpu.VMEM((hb, t, t), F32), pltpu.VMEM((hb, t, t), F32),
                        pltpu.VMEM((hb, t, t), BF16), pltpu.VMEM((hb, t, t), BF16)])
    return pl.pallas_call(
        body, grid_spec=grid_spec,
        out_shape=[jax.ShapeDtypeStruct((H, s, 2 * DN), F32), jax.ShapeDtypeStruct((H, s, 2 * DN), F32),
                   jax.ShapeDtypeStruct((H, s, DN), F32)],
        name="attn_bwd", compiler_params=_params(("parallel", "arbitrary"), VMEM_BIG))(
            qi, ki, q, k, v, do, lse, delta)


def _middle(za, o, proj_g, x, tgt, gate, fnw, wco, wao, wo, ts):
    s = x.shape[0]
    inv_d = 1.0 / D

    def body(za_ref, o_ref, bg_ref, ga_ref, gb_ref, x_ref, t_ref, gate_ref, fnw_ref, wco_ref, wao_ref, wo_ref,
             dx2_ref, dza_ref, do_ref, dl_ref, dpg_ref, zb_ref, mg_ref, dmo_ref, dya_ref, dyb_ref, vec_ref):
        @pl.when(pl.program_id(0) == 0)
        def _():
            vec_ref[...] = jnp.zeros_like(vec_ref)

        ov = o_ref[...]
        bg = bg_ref[...]
        sb = _sigmoid(bg)
        silu_b = bg * sb
        zb = (ov * silu_b).astype(BF16)
        zb_ref[...] = zb
        ya = _dot(za_ref[...], wco_ref[...])
        yb = _dot(zb, wao_ref[...])
        sa = _sigmoid(ga_ref[...])
        sg = _sigmoid(gb_ref[...])
        mg = (sa * ya + sg * yb).astype(BF16)
        mg_ref[...] = mg
        mo = _dot(mg, wo_ref[...])
        gate_v = gate_ref[...]
        x2 = x_ref[...] + gate_v * mo
        r = lax.rsqrt(_rowmean(x2 * x2) + EPS)
        xh = x2 * r
        fw = fnw_ref[...]
        e = xh * fw - t_ref[...]
        vec_ref[2:3, :] += _colsum(e * e)
        dy = e * inv_d
        vec_ref[0:1, :] += _colsum(dy * xh)
        dxh = dy * fw
        dx2 = r * (dxh - xh * _rowmean(dxh * xh))
        dx2_ref[...] = dx2
        vec_ref[1:2, :] += _colsum(dx2 * mo)
        dmo = (gate_v * dx2).astype(BF16)
        dmo_ref[...] = dmo
        dmg = _dot_nt(dmo, wo_ref[...])
        dya = (sa * dmg).astype(BF16)
        dyb = (sg * dmg).astype(BF16)
        dya_ref[...] = dya
        dyb_ref[...] = dyb
        dpg_ref[:, D:2 * D] = (dmg * ya * (sa * (1.0 - sa))).astype(BF16)
        dpg_ref[:, 2 * D:3 * D] = (dmg * yb * (sg * (1.0 - sg))).astype(BF16)
        dza_ref[...] = _dot_nt(dya, wco_ref[...])
        dzb = _dot_nt(dyb, wao_ref[...])
        dov = dzb * silu_b
        do_ref[...] = dov.astype(BF16)
        dpg_ref[:, 0:D] = (dzb * ov * _dsilu(bg, sb)).astype(BF16)
        dprod = dov * ov
        for h in range(H):
            dl_ref[h] = jnp.sum(dprod[:, DN * h:DN * (h + 1)], axis=-1, keepdims=True)

    col = lambda c: pl.BlockSpec((ts, D), lambda i, c=c: (i, c))
    row = pl.BlockSpec((ts, D), lambda i: (i, 0))
    vec = pl.BlockSpec((1, D), lambda i: (0, 0))
    wsp = pl.BlockSpec((D, D), lambda i: (0, 0))
    bf = jax.ShapeDtypeStruct((s, D), BF16)
    ff = jax.ShapeDtypeStruct((s, D), F32)
    return pl.pallas_call(
        body, grid=(s // ts,),
        in_specs=[row, row, col(0), col(1), col(2), row, row, vec, vec, wsp, wsp, wsp],
        out_specs=[row, row, row, pl.BlockSpec((H, ts, 1), lambda i: (0, i, 0)),
                   pl.BlockSpec((ts, G_COLS), lambda i: (i, 0)), row, row, row, row, row,
                   pl.BlockSpec((8, D), lambda i: (0, 0))],
        out_shape=[ff, ff, bf, jax.ShapeDtypeStruct((H, s, 1), F32), jax.ShapeDtypeStruct((s, G_COLS), BF16),
                   bf, bf, bf, bf, bf, jax.ShapeDtypeStruct((8, D), F32)],
        name="middle", compiler_params=_params(("arbitrary",), VMEM_BIG))(
            za, o, proj_g, proj_g, proj_g, x, tgt, gate, fnw, wco, wao, wo)


def _input_bwd(dpa, dpl, dpg, wa, wl, wg, x, dx2, norm_w, scale, ts, parts):
    s = x.shape[0]

    def body(dpa_ref, dpl_ref, dpg_ref, wa_ref, wl_ref, wg_ref, x_ref, dx2_ref, nw_ref, sc_ref, gx_ref, gv_ref):
        @pl.when(pl.program_id(0) == 0)
        def _():
            gv_ref[...] = jnp.zeros_like(gv_ref)

        dh = (_dot_nt(dpa_ref[...], wa_ref[...]) + _dot_nt(dpl_ref[...], wl_ref[...])
              + _dot_nt(dpg_ref[...], wg_ref[...]))
        xv = x_ref[...]
        r = lax.rsqrt(_rowmean(xv * xv) + EPS)
        xh = xv * r
        nw = nw_ref[...]
        gv_ref[0:1, :] += _colsum(dh)
        gv_ref[1:2, :] += _colsum(dh * (xh * nw))
        dy = dh * (1.0 + sc_ref[...])
        gv_ref[2:3, :] += _colsum(dy * xh)
        dxh = dy * nw
        gx_ref[...] = dx2_ref[...] + r * (dxh - xh * _rowmean(dxh * xh))

    const = lambda shape: pl.BlockSpec(shape, lambda i: (0, 0))
    rowb = lambda w: pl.BlockSpec((ts, w), lambda i: (i, 0))
    side_in, side_out, side_shapes, side_sems = _scatter_operands(parts, False)
    outs = pl.pallas_call(
        _scatter_alongside(body, 10, 2, len(parts), s // ts - 1, _chip_scatter_copies), grid=(s // ts,),
        in_specs=[rowb(A_COLS), rowb(L_COLS), rowb(G_COLS), const((D, A_COLS)), const((D, L_COLS)),
                  const((D, G_COLS)), rowb(D), rowb(D), const((1, D)), const((1, D))] + side_in,
        out_specs=[rowb(D), const((8, D))] + side_out,
        out_shape=[jax.ShapeDtypeStruct((s, D), F32), jax.ShapeDtypeStruct((8, D), F32)] + side_shapes,
        scratch_shapes=side_sems,
        name="input_bwd", compiler_params=_params(("arbitrary",), VMEM_BIG))(
            dpa, dpl, dpg, wa, wl, wg, x, dx2, norm_w, scale, *parts)
    return outs[0], outs[1], list(outs[2:])


def _adamw_math(w, g, m, v):
    nm = ADAM_B1 * m + (1.0 - ADAM_B1) * g
    nv = ADAM_B2 * v + (1.0 - ADAM_B2) * (g * g)
    m_hat = nm / (1.0 - ADAM_B1 ** ADAM_STEP)
    v_hat = nv / (1.0 - ADAM_B2 ** ADAM_STEP)
    return -ADAM_LR * (m_hat / (jnp.sqrt(v_hat) + ADAM_EPS) + ADAM_WD * w), nm, nv


def _adamw(w, g, m, v, tr, name):
    lead, (rows, cols) = w.shape[:-2], w.shape[-2:]

    def body(w_ref, g_ref, m_ref, v_ref, d_ref, nm_ref, nv_ref):
        d_ref[...], nm_ref[...], nv_ref[...] = _adamw_math(w_ref[...], g_ref[...], m_ref[...], v_ref[...])

    blk = pl.BlockSpec((1,) * len(lead) + (tr, cols), lambda i: (0,) * len(lead) + (i, 0))
    shp = jax.ShapeDtypeStruct(w.shape, F32)
    return pl.pallas_call(
        body, grid=(rows // tr,), in_specs=[blk] * 4, out_specs=[blk] * 3, out_shape=[shp] * 3, name=name,
        compiler_params=_params(("parallel",), VMEM_BIG))(w, g.reshape(w.shape), m, v)


ROW_SHIFT, ROW_SCALE, ROW_NORM_W = 0, 1, 2
ROW_FINAL_NORM_W, ROW_GATE, ROW_LOSS = 8, 9, 10
ROW_LN_W, ROW_LN_B, ROW_CONV_B = 16, 17, 18
ROW_Q_NORM_W, ROW_KV_NORM_W = 24, 25
ROW_CONV_W = 32
SUM_ROWS = 64
VECTOR_ROWS = ((ROW_SHIFT, ROW_SCALE, ROW_GATE), (ROW_NORM_W,), (ROW_CONV_B,), (ROW_LN_W,), (ROW_LN_B,),
               (ROW_Q_NORM_W,), (ROW_KV_NORM_W,), (ROW_FINAL_NORM_W,))


def _small_finalize(gathered, vectors, conv, chip):
    n = len(vectors)
    cw = conv[0].shape[2]

    def body(chip_ref, g_ref, *refs):
        ins, outs = refs[:3 * n + 3], refs[3 * n + 3:]
        tot = g_ref[0]
        for k in range(1, N_DEV):
            tot = tot + g_ref[k]
        for p, rows in enumerate(VECTOR_ROWS):
            w_ref, m_ref, v_ref = ins[3 * p:3 * p + 3]
            g_out, d_out, nm_out, nv_out = outs[4 * p:4 * p + 4]
            width = w_ref.shape[1] // len(rows)
            for q, r in enumerate(rows):
                lanes = slice(q * width, (q + 1) * width)
                g = tot[r:r + 1, 0:width]
                g_out[:, lanes] = g
                d_out[:, lanes], nm_out[:, lanes], nv_out[:, lanes] = _adamw_math(
                    w_ref[:, lanes], g, m_ref[:, lanes], v_ref[:, lanes])
        cols = pl.ds(pl.multiple_of(chip_ref[0] * cw, LANE), cw)
        gc = g_ref[0, pl.ds(ROW_CONV_W, KC), cols]
        for k in range(1, N_DEV):
            gc = gc + g_ref[k, pl.ds(ROW_CONV_W, KC), cols]
        cw_ref, cm_ref, cv_ref = ins[3 * n:3 * n + 3]
        g_out, d_out, nm_out, nv_out, dmod_ref, loss_ref = outs[4 * n:]
        g_out[0] = gc
        d_out[0], nm_out[0], nv_out[0] = _adamw_math(cw_ref[0], gc, cm_ref[0], cv_ref[0])
        for k in range(N_DEV):
            for q, r in enumerate((ROW_SHIFT, ROW_SCALE, ROW_GATE)):
                dmod_ref[k:k + 1, q * D:(q + 1) * D] = g_ref[k, r:r + 1, :]
        loss_ref[...] = (0.5 / D) * jnp.sum(tot[ROW_LOSS:ROW_LOSS + 1, :], axis=-1, keepdims=True)

    flat_in = [a for triple in vectors for a in triple] + list(conv)
    shapes = [jax.ShapeDtypeStruct(w.shape, F32) for w, _, _ in vectors for _ in range(4)]
    shapes += [jax.ShapeDtypeStruct(conv[0].shape, F32)] * 4
    shapes += [jax.ShapeDtypeStruct((N_DEV, 3 * D), F32), jax.ShapeDtypeStruct((1, 1), F32)]
    whole = pl.BlockSpec(memory_space=pltpu.VMEM)
    return pl.pallas_call(
        body, out_shape=shapes,
        in_specs=[pl.BlockSpec(memory_space=pltpu.SMEM)] + [whole] * (1 + len(flat_in)),
        out_specs=[whole] * len(shapes), name="small_finalize")(chip, gathered, *flat_in)


def _ada_fwd(c_all, w_ada_shard, b_ada_shard):
    def body(c_ref, w_ref, b_ref, o_ref):
        cv = c_ref[...]
        o_ref[...] = jnp.dot(cv * _sigmoid(cv), w_ref[...], preferred_element_type=F32,
                             precision=lax.Precision.HIGHEST) + b_ref[...]

    return pl.pallas_call(
        body, out_shape=jax.ShapeDtypeStruct((N_DEV, w_ada_shard.shape[1]), F32), name="ada_fwd")(
            c_all, w_ada_shard, b_ada_shard)


def _ada_bwd(c_all_t, dmod_shard):
    def body(c_ref, d_ref, o_ref):
        cv = c_ref[...]
        o_ref[...] = jnp.dot(cv * _sigmoid(cv), d_ref[...], preferred_element_type=F32,
                             precision=lax.Precision.HIGHEST)

    return pl.pallas_call(
        body, out_shape=jax.ShapeDtypeStruct((D, dmod_shard.shape[1]), F32), name="ada_bwd")(c_all_t, dmod_shard)


def _sum_chip_slabs(arrived, part, place, tr, name, axis):
    n, rows, cols = arrived.shape
    per = rows // tr
    own_map = ((lambda i, pc: (pc[0], i, 0)) if part.shape[1] == rows
               else (lambda i, pc: (pc[0], pc[1] * per + i, 0)))

    def body(place_ref, a_ref, p_ref, o_ref):
        acc = p_ref[0].astype(F32)
        for k in range(n):
            acc = acc + a_ref[k].astype(F32)
        o_ref[...] = acc

    if axis == 1:
        whole, out_map = (2 * rows, cols), lambda i, pc: (pc[1] * per + i, 0)
    else:
        whole, out_map = (rows, 2 * cols), lambda i, pc: (i, pc[1])
    grid_spec = pltpu.PrefetchScalarGridSpec(
        num_scalar_prefetch=1, grid=(per,),
        in_specs=[pl.BlockSpec((n, tr, cols), lambda i, pc: (0, i, 0)),
                  pl.BlockSpec((1, tr, cols), own_map)],
        out_specs=pl.BlockSpec((tr, cols), out_map))
    return pl.pallas_call(
        body, grid_spec=grid_spec, out_shape=jax.ShapeDtypeStruct(whole, F32), name=name,
        compiler_params=_params(("parallel",)))(place, arrived, part)


def _add_own_half(full, other, core, tr, name, axis):
    n, rows, cols = other.shape
    per = rows // tr

    def body(c_ref, f_ref, o_ref, out_ref):
        out_ref[...] = (f_ref[...] + o_ref[...]).astype(BF16)

    full_map = (lambda k, i, c: (k, c[0] * per + i, 0)) if axis == 1 else (lambda k, i, c: (k, i, c[0]))
    grid_spec = pltpu.PrefetchScalarGridSpec(
        num_scalar_prefetch=1, grid=(n, per),
        in_specs=[pl.BlockSpec((1, tr, cols), full_map),
                  pl.BlockSpec((1, tr, cols), lambda k, i, c: (k, i, 0))],
        out_specs=pl.BlockSpec((1, tr, cols), lambda k, i, c: (k, i, 0)))
    return pl.pallas_call(
        body, grid_spec=grid_spec, out_shape=jax.ShapeDtypeStruct((n, rows, cols), BF16), name=name,
        compiler_params=_params(("parallel", "parallel")))(core, full, other)


def _allgather8(block, src_rows, vmem, name):
    n = block.shape[1]
    m = src_rows
    sliced = block.shape[0] != m

    def body(x_ref, out_ref, send_sems, recv_sems, local_sem):
        x, y, c = _coords()
        me, sibling = (x, y, c), (x, y, 1 - c)
        chips = [(1 - x, y), (x, 1 - y), (1 - x, 1 - y)]
        src = x_ref.at[pl.ds(pl.multiple_of(c * m, 16), m), :] if sliced else x_ref

        def rows(px, py, pc):
            return out_ref.at[pl.ds(pl.multiple_of((4 * px + 2 * py + pc) * m, 8), m), :]

        def copy(k, blk, to, source=None):
            return pltpu.make_async_remote_copy(
                src_ref=rows(*blk) if source is None else source, dst_ref=rows(*blk),
                send_sem=send_sems.at[k], recv_sem=recv_sems.at[k], device_id=to, device_id_type=MESH)

        mine = pltpu.make_async_copy(src, rows(*me), local_sem)
        mine.start()
        first = [copy(0, me, sibling, source=src)]
        first += [copy(1 + j, me, (*chip, c), source=src) for j, chip in enumerate(chips)]
        for cp in first:
            cp.start()
        passed = [copy(4 + j, (*chip, c), sibling) for j, chip in enumerate(chips)]
        for j, chip in enumerate(chips):
            copy(1 + j, (*chip, c), me).wait_recv()
            passed[j].start()
        copy(0, sibling, me).wait_recv()
        for j, chip in enumerate(chips):
            copy(4 + j, (*chip, 1 - c), me).wait_recv()
        for cp in first + passed:
            cp.wait_send()
        mine.wait()

    space = pltpu.VMEM if vmem else pl.ANY
    return pl.pallas_call(
        body, out_shape=jax.ShapeDtypeStruct((N_DEV * m, n), block.dtype),
        in_specs=[pl.BlockSpec(memory_space=space)], out_specs=pl.BlockSpec(memory_space=space),
        scratch_shapes=[pltpu.SemaphoreType.DMA((7,)), pltpu.SemaphoreType.DMA((7,)), pltpu.SemaphoreType.DMA],
        name=name)(block)


def _gather_plan(x_refs, out_refs, send_sems, recv_sems, local_sems):
    n = len(x_refs)
    halves = [r.shape[0] // 2 for r in x_refs]
    x, y, c = _coords()
    me, sibling = (x, y, c), (x, y, 1 - c)
    chips = [(1 - x, y), (x, 1 - y), (1 - x, 1 - y)]

    def src(a):
        return x_refs[a].at[pl.ds(pl.multiple_of(c * halves[a], 16), halves[a]), :]

    def blk(a, px, py, pc):
        return out_refs[a].at[4 * px + 2 * py + pc]

    def copy(a, k, who, to, source=None):
        return pltpu.make_async_remote_copy(
            src_ref=blk(a, *who) if source is None else source, dst_ref=blk(a, *who),
            send_sem=send_sems.at[7 * a + k], recv_sem=recv_sems.at[7 * a + k], device_id=to, device_id_type=MESH)

    def mine(a):
        return pltpu.make_async_copy(src(a), blk(a, *me), local_sems.at[a])

    def first(a):
        return ([copy(a, 0, me, sibling, source=src(a))]
                + [copy(a, 1 + j, me, (*chip, c), source=src(a)) for j, chip in enumerate(chips)])

    def begin():
        for a in range(n):
            mine(a).start()
        for a in range(n):
            for cp in first(a):
                cp.start()

    def finish():
        onward = []
        for j, chip in enumerate(chips):
            for a in range(n):
                copy(a, 1 + j, (*chip, c), me).wait_recv()
                onward.append(copy(a, 4 + j, (*chip, c), sibling))
                onward[-1].start()
        for a in range(n):
            copy(a, 0, sibling, me).wait_recv()
        for j, chip in enumerate(chips):
            for a in range(n):
                copy(a, 4 + j, (*chip, 1 - c), me).wait_recv()
        for a in range(n):
            for cp in first(a):
                cp.wait_send()
        for cp in onward:
            cp.wait_send()
        for a in range(n):
            mine(a).wait()

    return begin, finish


def _gather_operands(shards):
    n = len(shards)
    shapes = [jax.ShapeDtypeStruct((N_DEV, a.shape[0] // 2, a.shape[1]), a.dtype) for a in shards]
    sems = [pltpu.SemaphoreType.DMA((7 * n,)), pltpu.SemaphoreType.DMA((7 * n,)), pltpu.SemaphoreType.DMA((n,))]
    return shapes, sems


def _as_chip_slabs(gathered, shards):
    return [o.reshape(N_CHIP, a.shape[0], a.shape[1]) for o, a in zip(gathered, shards)]


def _gather_weights(shards):
    n = len(shards)

    def body(*refs):
        begin, finish = _gather_plan(refs[:n], refs[n:2 * n], *refs[2 * n:])
        begin()
        finish()

    shapes, sems = _gather_operands(shards)
    outs = pl.pallas_call(
        body, out_shape=shapes, in_specs=[HBM_REF] * n, out_specs=[HBM_REF] * n, scratch_shapes=sems,
        name="gather_weights")(*shards)
    return _as_chip_slabs(outs, shards)


def _gather_alongside(body, n_in, n_out, n_shards, last_step):
    def wrapped(*refs):
        ins, shards = refs[:n_in], refs[n_in:n_in + n_shards]
        rest = refs[n_in + n_shards:]
        outs, gathered = rest[:n_out], rest[n_out:n_out + n_shards]
        scratch, sems = rest[n_out + n_shards:-3], rest[-3:]

        @pl.when(pl.program_id(0) == 0)
        def _():
            _gather_plan(shards, gathered, *sems)[0]()

        body(*ins, *outs, *scratch)

        @pl.when(pl.program_id(0) == last_step)
        def _():
            _gather_plan(shards, gathered, *sems)[1]()

    return wrapped


def _half(ref, axis, which, ndim):
    size = ref.shape[axis] // 2
    idx = [slice(None)] * ndim
    idx[axis] = pl.ds(pl.multiple_of(which * size, 8 if axis == ndim - 2 else LANE), size)
    return ref.at[tuple(idx)]


def _swap_halves_with_sibling(fulls, name, axes):
    n = len(fulls)

    def body(*refs):
        f_refs, got_refs = refs[:n], refs[n:2 * n]
        send_sems, recv_sems = refs[2 * n:]
        x, y, c = _coords()
        copies = []
        for a in range(n):
            copies.append(pltpu.make_async_remote_copy(
                src_ref=_half(f_refs[a], axes[a], 1 - c, 3), dst_ref=got_refs[a], send_sem=send_sems.at[a],
                recv_sem=recv_sems.at[a], device_id=(x, y, 1 - c), device_id_type=MESH))
        for cp in copies:
            cp.start()
        for cp in copies:
            cp.wait()

    def halved(a, axis):
        shape = list(a.shape)
        shape[axis] //= 2
        return jax.ShapeDtypeStruct(tuple(shape), a.dtype)

    return pl.pallas_call(
        body, out_shape=[halved(a, ax) for a, ax in zip(fulls, axes)],
        in_specs=[HBM_REF] * n, out_specs=[HBM_REF] * n,
        scratch_shapes=[pltpu.SemaphoreType.DMA((n,)), pltpu.SemaphoreType.DMA((n,))],
        name=name)(*fulls)


def _join_halves_with_sibling(wholes, axes):
    n = len(wholes)

    def body(*refs):
        out_refs = refs[n:2 * n]
        send_sems, recv_sems = refs[2 * n:]
        x, y, c = _coords()

        def push(a, core):
            half = _half(out_refs[a], axes[a] - 1, core, 2)
            return pltpu.make_async_remote_copy(
                src_ref=half, dst_ref=half, send_sem=send_sems.at[a], recv_sem=recv_sems.at[a],
                device_id=(x, y, 1 - c), device_id_type=MESH)

        for a in range(n):
            push(a, c).start()
        for a in range(n):
            push(a, 1 - c).wait_recv()
        for a in range(n):
            push(a, c).wait_send()

    return pl.pallas_call(
        body, out_shape=[jax.ShapeDtypeStruct(a.shape, a.dtype) for a in wholes],
        in_specs=[HBM_REF] * n, out_specs=[HBM_REF] * n, input_output_aliases={a: a for a in range(n)},
        scratch_shapes=[pltpu.SemaphoreType.DMA((n,)), pltpu.SemaphoreType.DMA((n,))],
        name="rs_pair_join")(*wholes)


def _cols_to_slabs(g):
    rows, cols = g.shape
    return g.reshape(rows, N_CHIP, cols // N_CHIP).transpose(1, 0, 2)


def _slabs_to_cols(w):
    n, rows, cols = w.shape
    return w.transpose(1, 0, 2).reshape(rows, n * cols)


def _col_window(slabs, start, stop):
    n = slabs.shape[2]
    pieces = []
    for k in range(N_CHIP):
        lo, hi = max(start, k * n), min(stop, (k + 1) * n)
        if lo < hi:
            pieces.append(slabs[k][:, lo - k * n:hi - k * n])
    return pieces[0] if len(pieces) == 1 else jnp.concatenate(pieces, axis=1)


def _slabs_from_groups(groups, n):
    slabs = []
    for k in range(N_CHIP):
        pieces, off = [], 0
        for g in groups:
            lo, hi = max(k * n, off), min((k + 1) * n, off + g.shape[0])
            if lo < hi:
                pieces.append(g[lo - off:hi - off])
            off += g.shape[0]
        slabs.append(pieces[0] if len(pieces) == 1 else jnp.concatenate(pieces, axis=0))
    return jnp.stack(slabs)


def _uq_to_padded(w_uq):
    per = w_uq.reshape(RQ, H, DN + DR)
    nope = per[:, :, :DN].reshape(RQ, H * DN)
    rope = jnp.pad(per[:, :, DN:], ((0, 0), (0, 0), (0, LANE - DR))).reshape(RQ, H * LANE)
    return jnp.concatenate([nope, rope], axis=1)


def _uq_from_padded(g):
    nope = g[:, :H * DN].reshape(RQ, H, DN)
    rope = g[:, H * DN:].reshape(RQ, H, LANE)[:, :, :DR]
    return jnp.concatenate([nope, rope], axis=2).reshape(RQ, H * (DN + DR))


def _rope_tables(positions):
    inv_freq = ROPE_THETA ** (-jnp.arange(0, DR, 2, dtype=F32) / DR)
    ang = positions.astype(F32)[:, None] * inv_freq
    cos, sin = jnp.cos(ang), jnp.sin(ang)
    return jnp.tile(cos, (1, 4)), jnp.tile(jnp.concatenate([-sin, sin], axis=1), (1, 2))


def _pair_sums(fulls, core, tag, axes, tr):
    from_sibling = _swap_halves_with_sibling(fulls, f"rs_pair_swap_{tag}", axes)
    return [_add_own_half(f, o, core, min(tr, o.shape[1]), f"add_own_half_{tag}{n}", ax)
            for n, (f, o, ax) in enumerate(zip(fulls, from_sibling, axes))]


def _local_step(x, tgt, cos_t, sin_t, mod, weights, small, tiles, place):
    ts, ts_in, tm_nn, tm_tn, t_attn, chunk = tiles
    wa, wl, wg, later_shards, conv_w = weights
    norm_w, conv_b, ln_w, ln_b, q_norm_w, kv_norm_w, fnw = small
    shift, scale, gate = mod[:, 0:D], mod[:, D:2 * D], mod[:, 2 * D:3 * D]

    h = _adaln_norm(x, norm_w, shift, scale, ts)
    proj_a = _mm_nn(h, wa, tm_nn, D, "proj_a")
    u0, u1, za, (g_uq, g_ukv, g_co, g_ao, g_o) = _conv_fwd(proj_a, conv_w, conv_b, ln_w, ln_b, ts, chunk, later_shards)
    w_uq2, w_ukv = _uq_to_padded(_slabs_to_cols(g_uq)), _slabs_to_cols(g_ukv)
    wco, wao, wo = g_co.reshape(D, D), g_ao.reshape(D, D), g_o.reshape(D, D)
    proj_l = _mm_nn(h, wl, tm_nn, L_COLS, "proj_l")
    proj_g = _mm_nn(h, wg, tm_nn, D, "proj_g")
    qn, kvn, q, k, v = _mla_prep(proj_l, q_norm_w, kv_norm_w, w_uq2, w_ukv, cos_t, sin_t, ts)
    o, lse = _attn_fwd(q, k, v, t_attn)
    (dx2, dza, do, delta, dpg, zb, mg, dmo, dya, dyb, vec_mid) = _middle(
        za, o, proj_g, x, tgt, gate, fnw, wco, wao, wo, ts)
    g_wo = _mm_tn(mg, dmo, tm_tn, D, D, "grad_w_out", BF16)
    g_wco = _mm_tn(za, dya, tm_tn, D, D, "grad_w_conv_out", BF16)
    g_wao = _mm_tn(zb, dyb, tm_tn, D, D, "grad_w_attn_out", BF16)
    dq, dk, dv = _attn_bwd(q, k, v, do, lse, delta, t_attn)
    dpl, g_wuq2, g_wukv, vec_mla = _mla_prep_bwd(
        dq, dk, dv, proj_l, qn, kvn, q_norm_w, kv_norm_w, w_uq2, w_ukv, cos_t, sin_t, ts)

    core = place[1:2]
    nr = D // N_CHIP
    early = [_cols_to_slabs(_uq_from_padded(g_wuq2)).astype(BF16), _cols_to_slabs(g_wukv).astype(BF16),
             g_wco.reshape(N_CHIP, nr, D), g_wao.reshape(N_CHIP, nr, D), g_wo.reshape(N_CHIP, nr, D)]
    dpa, g_conv_w, vec_conv, early_got = _conv_bwd(dza, proj_a, u0, u1, conv_w, ln_w, ln_b, ts, chunk, early)

    g_wa_t = _mm_tn(dpa, h, tm_tn, D, D, "grad_w_in_a")
    g_wl_t = _mm_tn(dpl, h, tm_tn, L_COLS, D, "grad_w_in_l")
    g_wg_t = _mm_tn(dpg, h, tm_tn, D, D, "grad_w_in_g")
    g_w_in_slabs = _slabs_from_groups([g_wa_t, g_wl_t[0:L_COLS_RAW], g_wg_t], IN_COLS // N_CHIP)
    late_sums = _pair_sums([g_w_in_slabs], core, "b", [2], W_IN_ROWS)
    grad_x, vec_in, late_got = _input_bwd(dpa, dpl, dpg, wa, wl, wg, x, dx2, norm_w, scale, ts_in, late_sums)

    axes = [2] + [1] * len(early)
    wholes = [_sum_chip_slabs(a, p, place, min(W_IN_ROWS if ax == 2 else 128, a.shape[1]), f"sum_chip_slabs_{n}", ax)
              for n, (a, p, ax) in enumerate(zip(late_got + early_got, late_sums + early, axes))]
    shards = _join_halves_with_sibling(wholes, axes)

    col_sums = jnp.concatenate(
        [vec_in, vec_mid, vec_conv, jnp.pad(vec_mla, ((0, 0), (0, D - RQ))), g_conv_w], axis=0)
    return grad_x, shards, col_sums


def kernel(x, c, positions, w_ada, b_ada, norm_w, w_in, conv_w, conv_b, conv_ln_w, conv_ln_b, w_conv_out, q_norm_w, w_uq, kv_norm_w, w_ukv, w_attn_out, w_out, final_norm_w, loss_target, m_w_ada, m_b_ada, m_norm_w, m_w_in, m_conv_w, m_conv_b, m_conv_ln_w, m_conv_ln_b, m_w_conv_out, m_q_norm_w, m_w_uq, m_kv_norm_w, m_w_ukv, m_w_attn_out, m_w_out, m_final_norm_w, v_w_ada, v_b_ada, v_norm_w, v_w_in, v_conv_w, v_conv_b, v_conv_ln_w, v_conv_ln_b, v_w_conv_out, v_q_norm_w, v_w_uq, v_kv_norm_w, v_w_ukv, v_w_attn_out, v_w_out, v_final_norm_w):
    ix, iy, ic = _coords()
    chip = 2 * ix + iy
    dev = 4 * ix + 2 * iy + ic
    s = x.shape[1]
    tiles = (256, 512, 1024, 2048, 512, 32)

    conv_w_pad = jnp.pad(conv_w[0], ((0, HALO - KC), (0, 0)))
    small_in = jnp.concatenate([c.reshape(8, LANE), conv_w_pad.reshape(64, LANE)], axis=0)
    small_all = _allgather8(small_in, 72, True, "gather_c_conv").reshape(N_DEV, 72, LANE)
    c_all = small_all[:, 0:8].reshape(N_DEV, D)
    conv_full = jnp.concatenate(
        [small_all[2 * k, 8:72].reshape(HALO, D // N_CHIP) for k in range(N_CHIP)], axis=1)

    (g_in,) = _gather_weights([w_in[0].astype(BF16)])
    wa = _col_window(g_in, 0, A_COLS)
    wl = jnp.pad(_col_window(g_in, A_COLS, A_COLS + L_COLS_RAW), ((0, 0), (0, L_COLS - L_COLS_RAW)))
    wg = _col_window(g_in, A_COLS + L_COLS_RAW, IN_COLS)
    later_shards = [w[0].astype(BF16) for w in (w_uq, w_ukv, w_conv_out, w_attn_out, w_out)]
    weights = (wa, wl, wg, later_shards, conv_full)

    ada_cols = w_ada.shape[2]
    b_shard = lax.dynamic_slice(b_ada, (0, chip * ada_cols), (1, ada_cols))
    mod_part = _ada_fwd(c_all, w_ada[0], b_shard)
    mod_all = _allgather8(mod_part, N_DEV, True, "gather_mod").reshape(N_DEV, N_DEV, ada_cols)
    mod = jnp.concatenate(
        [lax.dynamic_slice(mod_all[2 * k], (dev, 0), (1, ada_cols)) for k in range(N_CHIP)], axis=1)

    cos_t, sin_t = _rope_tables(positions[0])
    small = (norm_w, conv_b, conv_ln_w, conv_ln_b, q_norm_w, kv_norm_w, final_norm_w.reshape(1, D))
    place = jnp.stack([chip, ic]).astype(jnp.int32)
    grad_x, shards, col_sums = _local_step(x[0], loss_target[0], cos_t, sin_t, mod, weights, small, tiles, place)
    g_w_in_s, g_w_uq_s, g_w_ukv_s, g_wco_s, g_wao_s, g_wo_s = shards

    gathered = _allgather8(col_sums, SUM_ROWS, True, "gather_small_grads").reshape(N_DEV, SUM_ROWS, D)
    vec_names = ("b_ada", "norm_w", "conv_b", "conv_ln_w", "conv_ln_b", "q_norm_w", "kv_norm_w", "final_norm_w")
    row = lambda a: a.reshape(1, -1)
    vectors = [(row(b_ada), row(m_b_ada), row(v_b_ada)), (norm_w, m_norm_w, v_norm_w), (conv_b, m_conv_b, v_conv_b),
               (conv_ln_w, m_conv_ln_w, v_conv_ln_w), (conv_ln_b, m_conv_ln_b, v_conv_ln_b),
               (q_norm_w, m_q_norm_w, v_q_norm_w), (kv_norm_w, m_kv_norm_w, v_kv_norm_w),
               (row(final_norm_w), row(m_final_norm_w), row(v_final_norm_w))]
    fin = _small_finalize(gathered, vectors, (conv_w, m_conv_w, v_conv_w), place[0:1])
    res = {}
    for p, (name, (w, _, _)) in enumerate(zip(vec_names, vectors)):
        shape = final_norm_w.shape if name == "final_norm_w" else w.shape
        res[name] = tuple(a.reshape(shape) for a in fin[4 * p:4 * p + 4])
    res["conv_w"] = tuple(fin[4 * len(vectors):4 * len(vectors) + 4])
    dmod_all, loss = fin[-2], fin[-1].reshape(())
    dmod_shard = lax.dynamic_slice(dmod_all, (0, chip * ada_cols), (N_DEV, ada_cols))
    g_w_ada = _ada_bwd(c_all.T, dmod_shard).reshape(1, D, ada_cols)

    def big(w, g, m, v, tr, name):
        d, nm, nv = _adamw(w, g, m, v, tr, name)
        return g.reshape(w.shape), d, nm, nv

    res["w_ada"] = big(w_ada, g_w_ada[0], m_w_ada, v_w_ada, 256, "adamw_w_ada")
    t_in = [a[0].T for a in (w_in, m_w_in, v_w_in)]
    d_t, nm_t, nv_t = _adamw(t_in[0], g_w_in_s, t_in[1], t_in[2], W_IN_ROWS, "adamw_w_in")
    res["w_in"] = tuple(a.T[None] for a in (g_w_in_s, d_t, nm_t, nv_t))
    res["w_conv_out"] = big(w_conv_out, g_wco_s, m_w_conv_out, v_w_conv_out, 256, "adamw_w_conv_out")
    res["w_uq"] = big(w_uq, g_w_uq_s, m_w_uq, v_w_uq, 256, "adamw_w_uq")
    res["w_ukv"] = big(w_ukv, g_w_ukv_s, m_w_ukv, v_w_ukv, 256, "adamw_w_ukv")
    res["w_attn_out"] = big(w_attn_out, g_wao_s, m_w_attn_out, v_w_attn_out, 256, "adamw_w_attn_out")
    res["w_out"] = big(w_out, g_wo_s, m_w_out, v_w_out, 256, "adamw_w_out")

    order = ("w_ada", "b_ada", "norm_w", "w_in", "conv_w", "conv_b", "conv_ln_w", "conv_ln_b", "w_conv_out",
             "q_norm_w", "w_uq", "kv_norm_w", "w_ukv", "w_attn_out", "w_out", "final_norm_w")
    outs = [loss, grad_x[None]]
    for slot in range(4):
        outs += [res[name][slot] for name in order]
    return tuple(outs)
```

```python
import functools

import numpy as np
import jax
import jax.numpy as jnp
from jax import lax
from jax.experimental import pallas as pl
from jax.experimental.pallas import tpu as pltpu

F32 = jnp.float32
BF16 = jnp.bfloat16
MESH = pl.DeviceIdType.MESH

D = 1024
H = 8
DN = 128
DR = 64
RQ = 256
KC = 31
HALO = 32
EPS = 1e-6
ROPE_THETA = 10000.0
N_CHIP = 4
N_DEV = 8
LANE = 128
VMEM_BIG = 56 * 1024 * 1024

ADAM_LR = 0.001
ADAM_B1 = 0.9
ADAM_B2 = 0.999
ADAM_EPS = 1e-08
ADAM_WD = 0.01
ADAM_STEP = 10

A_COLS = 3 * D
L_COLS_RAW = RQ + RQ + DR
L_COLS = 640
G_COLS = 3 * D
IN_COLS = A_COLS + L_COLS_RAW + G_COLS


def _params(sem=None, vmem=None):
    kw = {}
    if sem is not None:
        kw["dimension_semantics"] = sem
    if vmem is not None:
        kw["vmem_limit_bytes"] = vmem
    return pltpu.CompilerParams(**kw)


def _dot(a, b):
    return jnp.dot(a, b, preferred_element_type=F32)


def _dot_nt(a, b):
    return lax.dot_general(a, b, (((1,), (1,)), ((), ())), preferred_element_type=F32)


def _dot_tn(a, b):
    return lax.dot_general(a, b, (((0,), (0,)), ((), ())), preferred_element_type=F32)


def _colsum(v):
    return jnp.sum(v, axis=0, keepdims=True)


def _rowmean(v):
    return jnp.mean(v, axis=-1, keepdims=True)


def _sigmoid(v):
    return jax.nn.sigmoid(v)


def _dsilu(v, s):
    return s * (1.0 + v * (1.0 - s))


def _swap_halves(v, first_half):
    return jnp.where(first_half, pltpu.roll(v, 96, 1), pltpu.roll(v, 32, 1))


def _first_half_mask(rows):
    lane = lax.broadcasted_iota(jnp.int32, (rows, LANE), 1)
    return (lane % 64) < 32


def _adaln_norm(x, norm_w, shift, scale, ts):
    s = x.shape[0]

    def body(x_ref, nw_ref, sh_ref, sc_ref, h_ref):
        xv = x_ref[...]
        r = lax.rsqrt(_rowmean(xv * xv) + EPS)
        y = xv * r * nw_ref[...]
        h_ref[...] = (y * (1.0 + sc_ref[...]) + sh_ref[...]).astype(BF16)

    row = pl.BlockSpec((ts, D), lambda i: (i, 0))
    vec = pl.BlockSpec((1, D), lambda i: (0, 0))
    return pl.pallas_call(
        body, grid=(s // ts,), in_specs=[row, vec, vec, vec], out_specs=row,
        out_shape=jax.ShapeDtypeStruct((s, D), BF16), name="adaln_norm",
        compiler_params=_params(("parallel",)))(x, norm_w, shift, scale)


def _mm_nn(a, b, tm, tn, name):
    m, k = a.shape
    n = b.shape[1]

    def body(a_ref, b_ref, o_ref):
        o_ref[...] = _dot(a_ref[...], b_ref[...])

    return pl.pallas_call(
        body, grid=(n // tn, m // tm),
        in_specs=[pl.BlockSpec((tm, k), lambda j, i: (i, 0)), pl.BlockSpec((k, tn), lambda j, i: (0, j))],
        out_specs=pl.BlockSpec((tm, tn), lambda j, i: (i, j)),
        out_shape=jax.ShapeDtypeStruct((m, n), F32), name=name,
        compiler_params=_params(("parallel", "parallel"), VMEM_BIG))(a, b)


def _mm_tn(a, b, tm, tk, tn, name, out_dtype=F32):
    m, k = a.shape
    n = b.shape[1]
    steps = m // tm

    def body(a_ref, b_ref, o_ref, acc_ref):
        @pl.when(pl.program_id(2) == 0)
        def _():
            acc_ref[...] = jnp.zeros_like(acc_ref)
        acc_ref[...] += _dot_tn(a_ref[...], b_ref[...])

        @pl.when(pl.program_id(2) == steps - 1)
        def _():
            o_ref[...] = acc_ref[...].astype(out_dtype)

    return pl.pallas_call(
        body, grid=(k // tk, n // tn, steps),
        in_specs=[pl.BlockSpec((tm, tk), lambda r, j, i: (i, r)), pl.BlockSpec((tm, tn), lambda r, j, i: (i, j))],
        out_specs=pl.BlockSpec((tk, tn), lambda r, j, i: (r, j)),
        out_shape=jax.ShapeDtypeStruct((k, n), out_dtype), scratch_shapes=[pltpu.VMEM((tk, tn), F32)], name=name,
        compiler_params=_params(("parallel", "parallel", "arbitrary"), VMEM_BIG))(a, b)


def _coords():
    return lax.axis_index("x"), lax.axis_index("y"), lax.axis_index("c")


HBM_REF = pl.BlockSpec(memory_space=pl.ANY)


def _chip_scatter_copies(p_refs, got_refs, send_sems, recv_sems):
    x, y, c = _coords()
    copies = []
    for a in range(len(p_refs)):
        for j, (px, py) in enumerate([(1 - x, y), (x, 1 - y), (1 - x, 1 - y)]):
            copies.append(pltpu.make_async_remote_copy(
                src_ref=p_refs[a].at[2 * px + py], dst_ref=got_refs[a].at[j], send_sem=send_sems.at[3 * a + j],
                recv_sem=recv_sems.at[3 * a + j], device_id=(px, py, c), device_id_type=MESH))
    return copies


RELATIONS = [(dx, dy, dc) for dx in (0, 1) for dy in (0, 1) for dc in (0, 1)][1:]


def _device_scatter_copies(p_refs, got_refs, send_sems, recv_sems):
    x, y, c = _coords()
    copies = []
    for a in range(len(p_refs)):
        half = p_refs[a].shape[1] // 2
        for j, (dx, dy, dc) in enumerate(RELATIONS):
            px, py, pc = (1 - x if dx else x), (1 - y if dy else y), (1 - c if dc else c)
            src = p_refs[a].at[2 * px + py, pl.ds(pl.multiple_of(pc * half, 16), half), :]
            copies.append(pltpu.make_async_remote_copy(
                src_ref=src, dst_ref=got_refs[a].at[j], send_sem=send_sems.at[7 * a + j],
                recv_sem=recv_sems.at[7 * a + j], device_id=(px, py, pc), device_id_type=MESH))
    return copies


def _scatter_alongside(body, n_in, n_out, n_parts, last_step, make_copies):
    def wrapped(*refs):
        ins, parts = refs[:n_in], refs[n_in:n_in + n_parts]
        rest = refs[n_in + n_parts:]
        outs, got = rest[:n_out], rest[n_out:n_out + n_parts]
        scratch, (send_sems, recv_sems) = rest[n_out + n_parts:-2], rest[-2:]

        @pl.when(pl.program_id(0) == 0)
        def _():
            for cp in make_copies(parts, got, send_sems, recv_sems):
                cp.start()

        body(*ins, *outs, *scratch)

        @pl.when(pl.program_id(0) == last_step)
        def _():
            for cp in make_copies(parts, got, send_sems, recv_sems):
                cp.wait()

    return wrapped


def _scatter_operands(parts, per_device):
    n = len(parts)
    if per_device:
        slots, shapes = 7, [jax.ShapeDtypeStruct((7, a.shape[1] // 2, a.shape[2]), a.dtype) for a in parts]
    else:
        slots, shapes = 3, [jax.ShapeDtypeStruct((3,) + a.shape[1:], a.dtype) for a in parts]
    sems = [pltpu.SemaphoreType.DMA((slots * n,)), pltpu.SemaphoreType.DMA((slots * n,))]
    return [HBM_REF] * n, [HBM_REF] * n, shapes, sems


def _shifted_copies(win_ref, sh_ref, rows):
    for p in range(1, 8):
        sh_ref[p - 1, 0:rows, :] = win_ref[pl.ds(p, rows), :]


def _tap_rows(win_ref, sh_ref, start, rows):
    p = start % 8
    if p == 0:
        return win_ref[pl.ds(start, rows), :]
    return sh_ref[p - 1, pl.ds(start - p, rows), :]


def _conv_taps(win_ref, sh_ref, w_ref, rows, chunk, offset_of_tap):
    pieces = []
    for c0 in range(0, rows, chunk):
        acc = None
        for j in range(KC):
            term = w_ref[j:j + 1, :] * _tap_rows(win_ref, sh_ref, c0 + offset_of_tap(j), chunk)
            acc = term if acc is None else acc + term
        pieces.append(acc)
    return pieces


def _conv_fwd(proj_a, conv_w, conv_b, ln_w, ln_b, ts, chunk, shards, senders, begun):
    s = proj_a.shape[0]

    def body(av_ref, al_ref, ag_ref, w_ref, b_ref, lw_ref, lb_ref, u0_ref, u1_ref, za_ref, win_ref, sh_ref):
        @pl.when(pl.program_id(0) == 0)
        def _():
            win_ref[0:HALO, :] = jnp.zeros((HALO, D), F32)

        u0 = av_ref[...] * _sigmoid(al_ref[...])
        u0_ref[...] = u0
        win_ref[HALO:HALO + ts, :] = u0
        _shifted_copies(win_ref, sh_ref, ts + HALO - 8)
        pieces = _conv_taps(win_ref, sh_ref, w_ref, ts, chunk, lambda j: HALO - (KC - 1) + j)
        for n, acc in enumerate(pieces):
            u1_ref[n * chunk:(n + 1) * chunk, :] = acc + b_ref[...]
        win_ref[0:HALO, :] = win_ref[ts:ts + HALO, :]

        u1 = u1_ref[...]
        xc = u1 - _rowmean(u1)
        rstd = lax.rsqrt(_rowmean(xc * xc) + EPS)
        u2 = xc * rstd * lw_ref[...] + lb_ref[...]
        u3 = u2 * _sigmoid(u2)
        ag = ag_ref[...]
        za_ref[...] = (u3 * (ag * _sigmoid(ag))).astype(BF16)

    col = lambda c: pl.BlockSpec((ts, D), lambda i, c=c: (i, c))
    row = pl.BlockSpec((ts, D), lambda i: (i, 0))
    vec = pl.BlockSpec((1, D), lambda i: (0, 0))
    n = len(shards)
    gathered_shapes, sems = _gather_operands(shards)
    outs = pl.pallas_call(
        _gather_alongside(body, 7, 3, n, s // ts - 1, senders), grid=(s // ts,),
        in_specs=[col(0), col(1), col(2), pl.BlockSpec((HALO, D), lambda i: (0, 0)), vec, vec, vec]
        + [HBM_REF] * (n + 1),
        out_specs=[row, row, row] + [HBM_REF] * n, input_output_aliases={7 + n: 3},
        out_shape=[jax.ShapeDtypeStruct((s, D), F32), jax.ShapeDtypeStruct((s, D), F32),
                   jax.ShapeDtypeStruct((s, D), BF16)] + gathered_shapes,
        scratch_shapes=[pltpu.VMEM((ts + HALO, D), F32), pltpu.VMEM((7, ts + HALO, D), F32)] + sems,
        name="conv_fwd", compiler_params=_params(("arbitrary",), VMEM_BIG))(
            proj_a, proj_a, proj_a, conv_w, conv_b, ln_w, ln_b, *shards, begun)
    return outs[0], outs[1], outs[2], _as_chip_slabs(outs[3:], shards)


def _conv_bwd(dza, proj_a, u0, u1, conv_w, ln_w, ln_b, ts, chunk, parts):
    s = dza.shape[0]
    nt = s // ts
    per = ts // HALO

    def body(dza_ref, av_ref, al_ref, ag_ref, u0_ref, u0p_ref, u1_ref, w_ref, lw_ref, lb_ref,
             dpa_ref, gw_ref, gv_ref, dwin_ref, uwin_ref, du0_ref, gwp_ref, dsh_ref, ush_ref):
        step = pl.program_id(0)
        tile = nt - 1 - step

        @pl.when(step == 0)
        def _():
            dwin_ref[ts:ts + HALO, :] = jnp.zeros((HALO, D), F32)
            gwp_ref[...] = jnp.zeros_like(gwp_ref)
            gv_ref[...] = jnp.zeros_like(gv_ref)

        ag = ag_ref[...]
        sg = _sigmoid(ag)
        u1 = u1_ref[...]
        xc = u1 - _rowmean(u1)
        rstd = lax.rsqrt(_rowmean(xc * xc) + EPS)
        xh = xc * rstd
        u2 = xh * lw_ref[...] + lb_ref[...]
        s2 = _sigmoid(u2)
        dz = dza_ref[...]
        du3 = dz * (ag * sg)
        dpa_ref[:, 2 * D:3 * D] = (dz * (u2 * s2) * _dsilu(ag, sg)).astype(BF16)
        du2 = du3 * _dsilu(u2, s2)
        gv_ref[0:1, :] += _colsum(du2 * xh)
        gv_ref[1:2, :] += _colsum(du2)
        dxh = du2 * lw_ref[...]
        du1 = rstd * (dxh - _rowmean(dxh) - xh * _rowmean(dxh * xh))
        gv_ref[2:3, :] += _colsum(du1)
        dwin_ref[0:ts, :] = du1

        uwin_ref[0:HALO, :] = jnp.where(tile == 0, 0.0, u0p_ref[...])
        uwin_ref[HALO:HALO + ts, :] = u0_ref[...]

        _shifted_copies(dwin_ref, dsh_ref, ts + HALO - 8)
        _shifted_copies(uwin_ref, ush_ref, ts + HALO - 8)
        pieces = _conv_taps(dwin_ref, dsh_ref, w_ref, ts, chunk, lambda j: (KC - 1) - j)
        for n, acc in enumerate(pieces):
            du0_ref[n * chunk:(n + 1) * chunk, :] = acc
        for c0 in range(0, ts, chunk):
            dchunk = dwin_ref[c0:c0 + chunk, :]
            for j in range(KC):
                prod = dchunk * _tap_rows(uwin_ref, ush_ref, c0 + HALO - (KC - 1) + j, chunk)
                gwp_ref[8 * j:8 * j + 8, :] += jnp.sum(prod.reshape(chunk // 8, 8, D), axis=0)
        dwin_ref[ts:ts + HALO, :] = dwin_ref[0:HALO, :]

        du0 = du0_ref[...]
        al = al_ref[...]
        sl = _sigmoid(al)
        dpa_ref[:, 0:D] = (du0 * sl).astype(BF16)
        dpa_ref[:, D:2 * D] = (du0 * av_ref[...] * sl * (1.0 - sl)).astype(BF16)

        @pl.when(step == nt - 1)
        def _():
            for j in range(KC):
                gw_ref[j:j + 1, :] = _colsum(gwp_ref[8 * j:8 * j + 8, :])
            gw_ref[KC:HALO, :] = jnp.zeros((HALO - KC, D), F32)

    rev = lambda i: nt - 1 - i
    col = lambda c: pl.BlockSpec((ts, D), lambda i, c=c: (rev(i), c))
    row = pl.BlockSpec((ts, D), lambda i: (rev(i), 0))
    vec = pl.BlockSpec((1, D), lambda i: (0, 0))
    halo = pl.BlockSpec((HALO, D), lambda i: (jnp.maximum(rev(i) * per - 1, 0), 0))
    side_in, side_out, side_shapes, side_sems = _scatter_operands(parts, True)
    outs = pl.pallas_call(
        _scatter_alongside(body, 10, 3, len(parts), nt - 1, _device_scatter_copies), grid=(nt,),
        in_specs=[row, col(0), col(1), col(2), row, halo, row, pl.BlockSpec((HALO, D), lambda i: (0, 0)), vec, vec]
        + side_in,
        out_specs=[pl.BlockSpec((ts, A_COLS), lambda i: (rev(i), 0)),
                   pl.BlockSpec((HALO, D), lambda i: (0, 0)), pl.BlockSpec((8, D), lambda i: (0, 0))] + side_out,
        out_shape=[jax.ShapeDtypeStruct((s, A_COLS), BF16), jax.ShapeDtypeStruct((HALO, D), F32),
                   jax.ShapeDtypeStruct((8, D), F32)] + side_shapes,
        scratch_shapes=[pltpu.VMEM((ts + HALO, D), F32), pltpu.VMEM((ts + HALO, D), F32),
                        pltpu.VMEM((ts, D), F32), pltpu.VMEM((8 * HALO, D), F32),
                        pltpu.VMEM((7, ts + HALO, D), F32), pltpu.VMEM((7, ts + HALO, D), F32)] + side_sems,
        name="conv_bwd", compiler_params=_params(("arbitrary",), VMEM_BIG))(
            dza, proj_a, proj_a, proj_a, u0, u0, u1, conv_w, ln_w, ln_b, *parts)
    return outs[0], outs[1], outs[2], list(outs[3:])


def _mla_prep(proj_l, q_norm_w, kv_norm_w, w_uq2, w_ukv, cos_t, sin_t, ts):
    s = proj_l.shape[0]

    def body(pl_ref, qw_ref, kw_ref, wq_ref, wkv_ref, c_ref, s_ref, qn_ref, kvn_ref, q_ref, k_ref, v_ref):
        first = _first_half_mask(ts)
        cs = c_ref[...]
        sn = s_ref[...]

        def rms(v, w):
            return v * lax.rsqrt(_rowmean(v * v) + EPS) * w

        def rope(v):
            return v * cs + _swap_halves(v, first) * sn

        qn = rms(pl_ref[:, 0:RQ], qw_ref[...]).astype(BF16)
        kvn = rms(pl_ref[:, RQ:2 * RQ], kw_ref[...]).astype(BF16)
        qn_ref[...] = qn
        kvn_ref[...] = kvn
        q = _dot(qn, wq_ref[...])
        kv = _dot(kvn, wkv_ref[...])
        kr = rope(pl_ref[:, 2 * RQ:2 * RQ + LANE]).astype(BF16)
        for h in range(H):
            q_ref[h, :, 0:DN] = q[:, DN * h:DN * (h + 1)].astype(BF16)
            q_ref[h, :, DN:2 * DN] = rope(q[:, H * DN + LANE * h:H * DN + LANE * (h + 1)]).astype(BF16)
            k_ref[h, :, 0:DN] = kv[:, 2 * DN * h:2 * DN * h + DN].astype(BF16)
            k_ref[h, :, DN:2 * DN] = kr
            v_ref[h, :, 0:DN] = kv[:, 2 * DN * h + DN:2 * DN * (h + 1)].astype(BF16)
            v_ref[h, :, DN:2 * DN] = jnp.ones((ts, DN), BF16)

    const = lambda shape: pl.BlockSpec(shape, lambda i: (0,) * len(shape))
    rowb = lambda w: pl.BlockSpec((ts, w), lambda i: (i, 0))
    head = lambda w: pl.BlockSpec((H, ts, w), lambda i: (0, i, 0))
    return pl.pallas_call(
        body, grid=(s // ts,),
        in_specs=[rowb(L_COLS), const((1, RQ)), const((1, RQ)), const((RQ, 2 * H * DN)), const((RQ, 2 * H * DN)),
                  rowb(LANE), rowb(LANE)],
        out_specs=[rowb(RQ), rowb(RQ), head(2 * DN), head(2 * DN), head(2 * DN)],
        out_shape=[jax.ShapeDtypeStruct((s, RQ), BF16), jax.ShapeDtypeStruct((s, RQ), BF16),
                   jax.ShapeDtypeStruct((H, s, 2 * DN), BF16), jax.ShapeDtypeStruct((H, s, 2 * DN), BF16),
                   jax.ShapeDtypeStruct((H, s, 2 * DN), BF16)],
        name="mla_prep", compiler_params=_params(("parallel",)))(
            proj_l, q_norm_w, kv_norm_w, w_uq2, w_ukv, cos_t, sin_t)


def _mla_prep_bwd(dq, dk, dv, proj_l, qn, kvn, q_norm_w, kv_norm_w, w_uq2, w_ukv, cos_t, sin_t, ts):
    s = proj_l.shape[0]

    def body(dq_ref, dk_ref, dv_ref, pl_ref, qn_ref, kvn_ref, qw_ref, kw_ref, wq_ref, wkv_ref, c_ref, s_ref,
             dpl_ref, gwq_ref, gwkv_ref, gv_ref, dq2_ref, dkv2_ref):
        @pl.when(pl.program_id(0) == 0)
        def _():
            gwq_ref[...] = jnp.zeros_like(gwq_ref)
            gwkv_ref[...] = jnp.zeros_like(gwkv_ref)
            gv_ref[...] = jnp.zeros_like(gv_ref)

        first = _first_half_mask(ts)
        cs = c_ref[...]
        sn = s_ref[...]

        def rope_bwd(g):
            return g * cs + _swap_halves(g * sn, first)

        def rms_bwd(v, w, dy):
            r = lax.rsqrt(_rowmean(v * v) + EPS)
            vh = v * r
            dvh = dy * w
            return r * (dvh - vh * _rowmean(dvh * vh)), _colsum(dy * vh)

        dkr = None
        for h in range(H):
            dq2_ref[:, DN * h:DN * (h + 1)] = dq_ref[h, :, 0:DN].astype(BF16)
            dq2_ref[:, H * DN + LANE * h:H * DN + LANE * (h + 1)] = rope_bwd(dq_ref[h, :, DN:2 * DN]).astype(BF16)
            dkv2_ref[:, 2 * DN * h:2 * DN * h + DN] = dk_ref[h, :, 0:DN].astype(BF16)
            dkv2_ref[:, 2 * DN * h + DN:2 * DN * (h + 1)] = dv_ref[h].astype(BF16)
            part = dk_ref[h, :, DN:2 * DN]
            dkr = part if dkr is None else dkr + part

        dq2 = dq2_ref[...]
        dkv2 = dkv2_ref[...]
        gwq_ref[...] += _dot_tn(qn_ref[...], dq2)
        gwkv_ref[...] += _dot_tn(kvn_ref[...], dkv2)
        dcq, gq = rms_bwd(pl_ref[:, 0:RQ], qw_ref[...], _dot_nt(dq2, wq_ref[...]))
        dckv, gkv = rms_bwd(pl_ref[:, RQ:2 * RQ], kw_ref[...], _dot_nt(dkv2, wkv_ref[...]))
        gv_ref[0:1, :] += gq
        gv_ref[1:2, :] += gkv
        dpl_ref[:, 0:RQ] = dcq.astype(BF16)
        dpl_ref[:, RQ:2 * RQ] = dckv.astype(BF16)
        dpl_ref[:, 2 * RQ:2 * RQ + LANE] = rope_bwd(dkr).astype(BF16)

    const = lambda shape: pl.BlockSpec(shape, lambda i: (0,) * len(shape))
    rowb = lambda w: pl.BlockSpec((ts, w), lambda i: (i, 0))
    head = lambda w: pl.BlockSpec((H, ts, w), lambda i: (0, i, 0))
    return pl.pallas_call(
        body, grid=(s // ts,),
        in_specs=[head(2 * DN), head(2 * DN), head(DN), rowb(L_COLS), rowb(RQ), rowb(RQ), const((1, RQ)),
                  const((1, RQ)), const((RQ, 2 * H * DN)), const((RQ, 2 * H * DN)), rowb(LANE), rowb(LANE)],
        out_specs=[rowb(L_COLS), const((RQ, 2 * H * DN)), const((RQ, 2 * H * DN)), const((8, RQ))],
        out_shape=[jax.ShapeDtypeStruct((s, L_COLS), BF16), jax.ShapeDtypeStruct((RQ, 2 * H * DN), F32),
                   jax.ShapeDtypeStruct((RQ, 2 * H * DN), F32), jax.ShapeDtypeStruct((8, RQ), F32)],
        scratch_shapes=[pltpu.VMEM((ts, 2 * H * DN), BF16), pltpu.VMEM((ts, 2 * H * DN), BF16)],
        name="mla_prep_bwd", compiler_params=_params(("arbitrary",), VMEM_BIG))(
            dq, dk, dv, proj_l, qn, kvn, q_norm_w, kv_norm_w, w_uq2, w_ukv, cos_t, sin_t)


def _causal_pairs(n, by_key):
    if by_key:
        pairs = [(i, j) for j in range(n) for i in range(j, n)]
    else:
        pairs = [(i, j) for i in range(n) for j in range(i + 1)]
    return (jnp.asarray(np.array([p[0] for p in pairs], np.int32)),
            jnp.asarray(np.array([p[1] for p in pairs], np.int32)))


LOG2E = 1.4426950408889634
LN2 = 0.6931471805599453
ATT_HEADS_FWD = 4
ATT_HEADS = 2
W_IN_ROWS = 336
ATT_ROWS = 64


def _diag_width(r0, t):
    return min(t, -(-(r0 + ATT_ROWS) // LANE) * LANE)


def _diag_mask_rows(r0, width):
    rows = r0 + lax.broadcasted_iota(jnp.int32, (ATT_ROWS, width), 0)
    cols = lax.broadcasted_iota(jnp.int32, (ATT_ROWS, width), 1)
    return cols <= rows


def _diag_mask(t):
    return lax.broadcasted_iota(jnp.int32, (t, t), 1) <= lax.broadcasted_iota(jnp.int32, (t, t), 0)


def _attn_fwd(q, k, v, t):
    s = q.shape[1]
    n = s // t
    scale2 = float((DN + DR) ** -0.5) * LOG2E
    qi, ki = _causal_pairs(n, by_key=False)

    def body(qi_ref, ki_ref, q_ref, k_ref, v_ref, o_ref, lse_ref, *scratch):
        per_head = [scratch[5 * h:5 * h + 5] for h in range(ATT_HEADS_FWD)]
        p = pl.program_id(1)
        i = qi_ref[p]
        j = ki_ref[p]

        @pl.when(j == 0)
        def _():
            for m_sc, acc_sc, _, _, _ in per_head:
                m_sc[...] = jnp.full_like(m_sc, -jnp.inf)
                acc_sc[...] = jnp.zeros_like(acc_sc)

        def scores(h, diag):
            sc = _dot_nt(q_ref[h], k_ref[h])
            if diag:
                sc = jnp.where(_diag_mask(t), sc, -jnp.inf)
            per_head[h][2][...] = sc

        def rowmax(h, rows):
            per_head[h][4][rows, :] = jnp.max(per_head[h][2][rows, :], axis=-1, keepdims=True)

        def stats(h):
            m_sc, acc_sc, _, _, mx_sc = per_head[h]
            m_prev = m_sc[...]
            m_new = jnp.maximum(m_prev, mx_sc[...] * scale2)
            m_sc[...] = m_new
            acc_sc[...] = jnp.exp2(m_prev - m_new) * acc_sc[...]

        def probs(h, rows):
            m_sc, _, s_sc, p_sc, _ = per_head[h]
            p_sc[rows, :] = jnp.exp2(s_sc[rows, :] * scale2 - m_sc[rows, :]).astype(BF16)

        def values(h):
            _, acc_sc, _, p_sc, _ = per_head[h]
            acc_sc[...] += _dot(p_sc[...], v_ref[h])

        def step(diag):
            blocks = [slice(r0, r0 + ATT_ROWS) for r0 in range(0, t, ATT_ROWS)]
            for h in range(ATT_HEADS_FWD):
                scores(h, diag)
            for rows in blocks:
                rowmax(0, rows)
            stats(0)
            for h in range(ATT_HEADS_FWD):
                for rows in blocks:
                    probs(h, rows)
                    if h + 1 < ATT_HEADS_FWD:
                        rowmax(h + 1, rows)
                if h + 1 < ATT_HEADS_FWD:
                    stats(h + 1)
                values(h)

        @pl.when(j < i)
        def _():
            step(False)

        @pl.when(j == i)
        def _():
            step(True)
            for h, (m_sc, acc_sc, _, _, _) in enumerate(per_head):
                l = acc_sc[:, DN:2 * DN]
                o_ref[:, DN * h:DN * (h + 1)] = acc_sc[:, 0:DN] / l
                lse_ref[h] = (m_sc[...] + jnp.log2(l[:, 0:1])) * LN2

    hb = ATT_HEADS_FWD
    grid_spec = pltpu.PrefetchScalarGridSpec(
        num_scalar_prefetch=2, grid=(H // hb, int(qi.shape[0])),
        in_specs=[pl.BlockSpec((hb, t, 2 * DN), lambda h, p, qi, ki: (h, qi[p], 0)),
                  pl.BlockSpec((hb, t, 2 * DN), lambda h, p, qi, ki: (h, ki[p], 0)),
                  pl.BlockSpec((hb, t, 2 * DN), lambda h, p, qi, ki: (h, ki[p], 0))],
        out_specs=[pl.BlockSpec((t, hb * DN), lambda h, p, qi, ki: (qi[p], h)),
                   pl.BlockSpec((hb, t, 1), lambda h, p, qi, ki: (h, qi[p], 0))],
        scratch_shapes=[pltpu.VMEM((t, 1), F32), pltpu.VMEM((t, 2 * DN), F32), pltpu.VMEM((t, t), F32),
                        pltpu.VMEM((t, t), BF16), pltpu.VMEM((t, 1), F32)] * hb)
    return pl.pallas_call(
        body, grid_spec=grid_spec,
        out_shape=[jax.ShapeDtypeStruct((s, H * DN), F32), jax.ShapeDtypeStruct((H, s, 1), F32)],
        name="attn_fwd", compiler_params=_params(("parallel", "arbitrary"), VMEM_BIG))(qi, ki, q, k, v)


def _attn_bwd(q, k, v, do, lse, delta, t):
    s = q.shape[1]
    n = s // t
    scale = float((DN + DR) ** -0.5)
    qi, ki = _causal_pairs(n, by_key=True)

    def body(qi_ref, ki_ref, q_ref, k_ref, v_ref, do_ref, lse_ref, dl_ref, dq_ref, dk_ref, dv_ref,
             dk_sc, dv_sc, s_sc, dp_sc, p_sc, ds_sc):
        p = pl.program_id(1)
        i = qi_ref[p]
        j = ki_ref[p]

        @pl.when(p == 0)
        def _():
            dq_ref[...] = jnp.zeros_like(dq_ref)

        @pl.when(i == j)
        def _():
            dk_sc[...] = jnp.zeros_like(dk_sc)
            dv_sc[...] = jnp.zeros_like(dv_sc)

        def step(diag):
            for h in range(ATT_HEADS):
                s_sc[h] = _dot_nt(q_ref[h], k_ref[h])
                dp_sc[h] = _dot_nt(do_ref[:, DN * h:DN * (h + 1)], v_ref[h, :, 0:DN])
            for h in range(ATT_HEADS):
                for r0 in range(0, t, ATT_ROWS):
                    rows = slice(r0, r0 + ATT_ROWS)
                    width = _diag_width(r0, t) if diag else t
                    sc = s_sc[h, rows, 0:width] * (scale * LOG2E)
                    if diag:
                        sc = jnp.where(_diag_mask_rows(r0, width), sc, -jnp.inf)
                    pr = jnp.exp2(sc - lse_ref[h, rows, :] * LOG2E)
                    ds = pr * (dp_sc[h, rows, 0:width] - dl_ref[h, rows, :]) * scale
                    p_sc[h, rows, 0:width] = pr.astype(BF16)
                    ds_sc[h, rows, 0:width] = ds.astype(BF16)
                    if width < t:
                        p_sc[h, rows, width:t] = jnp.zeros((ATT_ROWS, t - width), BF16)
                        ds_sc[h, rows, width:t] = jnp.zeros((ATT_ROWS, t - width), BF16)
            q_rows = pl.ds(pl.multiple_of(i * t, t), t)
            for h in range(ATT_HEADS):
                dv_sc[h] += _dot_tn(p_sc[h], do_ref[:, DN * h:DN * (h + 1)])
                dk_sc[h] += _dot_tn(ds_sc[h], q_ref[h])
                dq_ref[h, q_rows, :] += _dot(ds_sc[h], k_ref[h])

        @pl.when(i > j)
        def _():
            step(False)

        @pl.when(i == j)
        def _():
            step(True)

        @pl.when(i == n - 1)
        def _():
            dk_ref[...] = dk_sc[...]
            dv_ref[...] = dv_sc[...]

    hb = ATT_HEADS
    grid_spec = pltpu.PrefetchScalarGridSpec(
        num_scalar_prefetch=2, grid=(H // hb, int(qi.shape[0])),
        in_specs=[pl.BlockSpec((hb, t, 2 * DN), lambda h, p, qi, ki: (h, qi[p], 0)),
                  pl.BlockSpec((hb, t, 2 * DN), lambda h, p, qi, ki: (h, ki[p], 0)),
                  pl.BlockSpec((hb, t, 2 * DN), lambda h, p, qi, ki: (h, ki[p], 0)),
                  pl.BlockSpec((t, hb * DN), lambda h, p, qi, ki: (qi[p], h)),
                  pl.BlockSpec((hb, t, 1), lambda h, p, qi, ki: (h, qi[p], 0)),
                  pl.BlockSpec((hb, t, 1), lambda h, p, qi, ki: (h, qi[p], 0))],
        out_specs=[pl.BlockSpec((hb, s, 2 * DN), lambda h, p, qi, ki: (h, 0, 0)),
                   pl.BlockSpec((hb, t, 2 * DN), lambda h, p, qi, ki: (h, ki[p], 0)),
                   pl.BlockSpec((hb, t, DN), lambda h, p, qi, ki: (h, ki[p], 0))],
        scratch_shapes=[pltpu.VMEM((hb, t, 2 * DN), F32), pltpu.VMEM((hb, t, DN), F32),
                        pltpu.VMEM((hb, t, t), F32), pltpu.VMEM((hb, t, t), F32),
                        pltpu.VMEM((hb, t, t), BF16), pltpu.VMEM((hb, t, t), BF16)])
    return pl.pallas_call(
        body, grid_spec=grid_spec,
        out_shape=[jax.ShapeDtypeStruct((H, s, 2 * DN), F32), jax.ShapeDtypeStruct((H, s, 2 * DN), F32),
                   jax.ShapeDtypeStruct((H, s, DN), F32)],
        name="attn_bwd", compiler_params=_params(("parallel", "arbitrary"), VMEM_BIG))(
            qi, ki, q, k, v, do, lse, delta)


def _middle(za, o, proj_g, x, tgt, gate, fnw, wco, wao, wo, ts):
    s = x.shape[0]
    inv_d = 1.0 / D

    def body(za_ref, o_ref, bg_ref, ga_ref, gb_ref, x_ref, t_ref, gate_ref, fnw_ref, wco_ref, wao_ref, wo_ref,
             dx2_ref, dza_ref, do_ref, dl_ref, dpg_ref, zb_ref, mg_ref, dmo_ref, dya_ref, dyb_ref, vec_ref):
        @pl.when(pl.program_id(0) == 0)
        def _():
            vec_ref[...] = jnp.zeros_like(vec_ref)

        ov = o_ref[...]
        bg = bg_ref[...]
        sb = _sigmoid(bg)
        silu_b = bg * sb
        zb = (ov * silu_b).astype(BF16)
        zb_ref[...] = zb
        ya = _dot(za_ref[...], wco_ref[...])
        yb = _dot(zb, wao_ref[...])
        sa = _sigmoid(ga_ref[...])
        sg = _sigmoid(gb_ref[...])
        mg = (sa * ya + sg * yb).astype(BF16)
        mg_ref[...] = mg
        mo = _dot(mg, wo_ref[...])
        gate_v = gate_ref[...]
        x2 = x_ref[...] + gate_v * mo
        r = lax.rsqrt(_rowmean(x2 * x2) + EPS)
        xh = x2 * r
        fw = fnw_ref[...]
        e = xh * fw - t_ref[...]
        vec_ref[2:3, :] += _colsum(e * e)
        dy = e * inv_d
        vec_ref[0:1, :] += _colsum(dy * xh)
        dxh = dy * fw
        dx2 = r * (dxh - xh * _rowmean(dxh * xh))
        dx2_ref[...] = dx2
        vec_ref[1:2, :] += _colsum(dx2 * mo)
        dmo = (gate_v * dx2).astype(BF16)
        dmo_ref[...] = dmo
        dmg = _dot_nt(dmo, wo_ref[...])
        dya = (sa * dmg).astype(BF16)
        dyb = (sg * dmg).astype(BF16)
        dya_ref[...] = dya
        dyb_ref[...] = dyb
        dpg_ref[:, D:2 * D] = (dmg * ya * (sa * (1.0 - sa))).astype(BF16)
        dpg_ref[:, 2 * D:3 * D] = (dmg * yb * (sg * (1.0 - sg))).astype(BF16)
        dza_ref[...] = _dot_nt(dya, wco_ref[...])
        dzb = _dot_nt(dyb, wao_ref[...])
        dov = dzb * silu_b
        do_ref[...] = dov.astype(BF16)
        dpg_ref[:, 0:D] = (dzb * ov * _dsilu(bg, sb)).astype(BF16)
        dprod = dov * ov
        for h in range(H):
            dl_ref[h] = jnp.sum(dprod[:, DN * h:DN * (h + 1)], axis=-1, keepdims=True)

    col = lambda c: pl.BlockSpec((ts, D), lambda i, c=c: (i, c))
    row = pl.BlockSpec((ts, D), lambda i: (i, 0))
    vec = pl.BlockSpec((1, D), lambda i: (0, 0))
    wsp = pl.BlockSpec((D, D), lambda i: (0, 0))
    bf = jax.ShapeDtypeStruct((s, D), BF16)
    ff = jax.ShapeDtypeStruct((s, D), F32)
    return pl.pallas_call(
        body, grid=(s // ts,),
        in_specs=[row, row, col(0), col(1), col(2), row, row, vec, vec, wsp, wsp, wsp],
        out_specs=[row, row, row, pl.BlockSpec((H, ts, 1), lambda i: (0, i, 0)),
                   pl.BlockSpec((ts, G_COLS), lambda i: (i, 0)), row, row, row, row, row,
                   pl.BlockSpec((8, D), lambda i: (0, 0))],
        out_shape=[ff, ff, bf, jax.ShapeDtypeStruct((H, s, 1), F32), jax.ShapeDtypeStruct((s, G_COLS), BF16),
                   bf, bf, bf, bf, bf, jax.ShapeDtypeStruct((8, D), F32)],
        name="middle", compiler_params=_params(("arbitrary",), VMEM_BIG))(
            za, o, proj_g, proj_g, proj_g, x, tgt, gate, fnw, wco, wao, wo)


def _input_bwd(dpa, dpl, dpg, wa, wl, wg, x, dx2, norm_w, scale, ts, parts):
    s = x.shape[0]

    def body(dpa_ref, dpl_ref, dpg_ref, wa_ref, wl_ref, wg_ref, x_ref, dx2_ref, nw_ref, sc_ref, gx_ref, gv_ref):
        @pl.when(pl.program_id(0) == 0)
        def _():
            gv_ref[...] = jnp.zeros_like(gv_ref)

        dh = (_dot_nt(dpa_ref[...], wa_ref[...]) + _dot_nt(dpl_ref[...], wl_ref[...])
              + _dot_nt(dpg_ref[...], wg_ref[...]))
        xv = x_ref[...]
        r = lax.rsqrt(_rowmean(xv * xv) + EPS)
        xh = xv * r
        nw = nw_ref[...]
        gv_ref[0:1, :] += _colsum(dh)
        gv_ref[1:2, :] += _colsum(dh * (xh * nw))
        dy = dh * (1.0 + sc_ref[...])
        gv_ref[2:3, :] += _colsum(dy * xh)
        dxh = dy * nw
        gx_ref[...] = dx2_ref[...] + r * (dxh - xh * _rowmean(dxh * xh))

    const = lambda shape: pl.BlockSpec(shape, lambda i: (0, 0))
    rowb = lambda w: pl.BlockSpec((ts, w), lambda i: (i, 0))
    side_in, side_out, side_shapes, side_sems = _scatter_operands(parts, False)
    outs = pl.pallas_call(
        _scatter_alongside(body, 10, 2, len(parts), s // ts - 1, _chip_scatter_copies), grid=(s // ts,),
        in_specs=[rowb(A_COLS), rowb(L_COLS), rowb(G_COLS), const((D, A_COLS)), const((D, L_COLS)),
                  const((D, G_COLS)), rowb(D), rowb(D), const((1, D)), const((1, D))] + side_in,
        out_specs=[rowb(D), const((8, D))] + side_out,
        out_shape=[jax.ShapeDtypeStruct((s, D), F32), jax.ShapeDtypeStruct((8, D), F32)] + side_shapes,
        scratch_shapes=side_sems,
        name="input_bwd", compiler_params=_params(("arbitrary",), VMEM_BIG))(
            dpa, dpl, dpg, wa, wl, wg, x, dx2, norm_w, scale, *parts)
    return outs[0], outs[1], list(outs[2:])


def _adamw_math(w, g, m, v):
    nm = ADAM_B1 * m + (1.0 - ADAM_B1) * g
    nv = ADAM_B2 * v + (1.0 - ADAM_B2) * (g * g)
    m_hat = nm / (1.0 - ADAM_B1 ** ADAM_STEP)
    v_hat = nv / (1.0 - ADAM_B2 ** ADAM_STEP)
    return -ADAM_LR * (m_hat / (jnp.sqrt(v_hat) + ADAM_EPS) + ADAM_WD * w), nm, nv


def _adamw(w, g, m, v, tr, name):
    lead, (rows, cols) = w.shape[:-2], w.shape[-2:]

    def body(w_ref, g_ref, m_ref, v_ref, d_ref, nm_ref, nv_ref):
        d_ref[...], nm_ref[...], nv_ref[...] = _adamw_math(w_ref[...], g_ref[...], m_ref[...], v_ref[...])

    blk = pl.BlockSpec((1,) * len(lead) + (tr, cols), lambda i: (0,) * len(lead) + (i, 0))
    shp = jax.ShapeDtypeStruct(w.shape, F32)
    return pl.pallas_call(
        body, grid=(rows // tr,), in_specs=[blk] * 4, out_specs=[blk] * 3, out_shape=[shp] * 3, name=name,
        compiler_params=_params(("parallel",), VMEM_BIG))(w, g.reshape(w.shape), m, v)


ROW_SHIFT, ROW_SCALE, ROW_NORM_W = 0, 1, 2
ROW_FINAL_NORM_W, ROW_GATE, ROW_LOSS = 8, 9, 10
ROW_LN_W, ROW_LN_B, ROW_CONV_B = 16, 17, 18
ROW_Q_NORM_W, ROW_KV_NORM_W = 24, 25
ROW_CONV_W = 32
SUM_ROWS = 64
VECTOR_ROWS = ((ROW_SHIFT, ROW_SCALE, ROW_GATE), (ROW_NORM_W,), (ROW_CONV_B,), (ROW_LN_W,), (ROW_LN_B,),
               (ROW_Q_NORM_W,), (ROW_KV_NORM_W,), (ROW_FINAL_NORM_W,))


def _small_finalize(gathered, vectors, conv, chip):
    n = len(vectors)
    cw = conv[0].shape[2]

    def body(chip_ref, g_ref, *refs):
        ins, outs = refs[:3 * n + 3], refs[3 * n + 3:]
        tot = g_ref[0]
        for k in range(1, N_DEV):
            tot = tot + g_ref[k]
        for p, rows in enumerate(VECTOR_ROWS):
            w_ref, m_ref, v_ref = ins[3 * p:3 * p + 3]
            g_out, d_out, nm_out, nv_out = outs[4 * p:4 * p + 4]
            width = w_ref.shape[1] // len(rows)
            for q, r in enumerate(rows):
                lanes = slice(q * width, (q + 1) * width)
                g = tot[r:r + 1, 0:width]
                g_out[:, lanes] = g
                d_out[:, lanes], nm_out[:, lanes], nv_out[:, lanes] = _adamw_math(
                    w_ref[:, lanes], g, m_ref[:, lanes], v_ref[:, lanes])
        cols = pl.ds(pl.multiple_of(chip_ref[0] * cw, LANE), cw)
        gc = g_ref[0, pl.ds(ROW_CONV_W, KC), cols]
        for k in range(1, N_DEV):
            gc = gc + g_ref[k, pl.ds(ROW_CONV_W, KC), cols]
        cw_ref, cm_ref, cv_ref = ins[3 * n:3 * n + 3]
        g_out, d_out, nm_out, nv_out, dmod_ref, loss_ref = outs[4 * n:]
        g_out[0] = gc
        d_out[0], nm_out[0], nv_out[0] = _adamw_math(cw_ref[0], gc, cm_ref[0], cv_ref[0])
        for k in range(N_DEV):
            for q, r in enumerate((ROW_SHIFT, ROW_SCALE, ROW_GATE)):
                dmod_ref[k:k + 1, q * D:(q + 1) * D] = g_ref[k, r:r + 1, :]
        loss_ref[...] = (0.5 / D) * jnp.sum(tot[ROW_LOSS:ROW_LOSS + 1, :], axis=-1, keepdims=True)

    flat_in = [a for triple in vectors for a in triple] + list(conv)
    shapes = [jax.ShapeDtypeStruct(w.shape, F32) for w, _, _ in vectors for _ in range(4)]
    shapes += [jax.ShapeDtypeStruct(conv[0].shape, F32)] * 4
    shapes += [jax.ShapeDtypeStruct((N_DEV, 3 * D), F32), jax.ShapeDtypeStruct((1, 1), F32)]
    whole = pl.BlockSpec(memory_space=pltpu.VMEM)
    return pl.pallas_call(
        body, out_shape=shapes,
        in_specs=[pl.BlockSpec(memory_space=pltpu.SMEM)] + [whole] * (1 + len(flat_in)),
        out_specs=[whole] * len(shapes), name="small_finalize")(chip, gathered, *flat_in)


def _ada_fwd(c_all, w_ada_shard, b_ada_shard):
    def body(c_ref, w_ref, b_ref, o_ref):
        cv = c_ref[...]
        o_ref[...] = jnp.dot(cv * _sigmoid(cv), w_ref[...], preferred_element_type=F32,
                             precision=lax.Precision.HIGHEST) + b_ref[...]

    return pl.pallas_call(
        body, out_shape=jax.ShapeDtypeStruct((N_DEV, w_ada_shard.shape[1]), F32), name="ada_fwd")(
            c_all, w_ada_shard, b_ada_shard)


def _ada_bwd(c_all_t, dmod_shard):
    def body(c_ref, d_ref, o_ref):
        cv = c_ref[...]
        o_ref[...] = jnp.dot(cv * _sigmoid(cv), d_ref[...], preferred_element_type=F32,
                             precision=lax.Precision.HIGHEST)

    return pl.pallas_call(
        body, out_shape=jax.ShapeDtypeStruct((D, dmod_shard.shape[1]), F32), name="ada_bwd")(c_all_t, dmod_shard)


def _sum_chip_slabs(arrived, part, place, tr, name, axis):
    n, rows, cols = arrived.shape
    per = rows // tr
    own_map = ((lambda i, pc: (pc[0], i, 0)) if part.shape[1] == rows
               else (lambda i, pc: (pc[0], pc[1] * per + i, 0)))

    def body(place_ref, a_ref, p_ref, o_ref):
        acc = p_ref[0].astype(F32)
        for k in range(n):
            acc = acc + a_ref[k].astype(F32)
        o_ref[...] = acc

    if axis == 1:
        whole, out_map = (2 * rows, cols), lambda i, pc: (pc[1] * per + i, 0)
    else:
        whole, out_map = (rows, 2 * cols), lambda i, pc: (i, pc[1])
    grid_spec = pltpu.PrefetchScalarGridSpec(
        num_scalar_prefetch=1, grid=(per,),
        in_specs=[pl.BlockSpec((n, tr, cols), lambda i, pc: (0, i, 0)),
                  pl.BlockSpec((1, tr, cols), own_map)],
        out_specs=pl.BlockSpec((tr, cols), out_map))
    return pl.pallas_call(
        body, grid_spec=grid_spec, out_shape=jax.ShapeDtypeStruct(whole, F32), name=name,
        compiler_params=_params(("parallel",)))(place, arrived, part)


def _add_own_half(full, other, core, tr, name, axis):
    n, rows, cols = other.shape
    per = rows // tr

    def body(c_ref, f_ref, o_ref, out_ref):
        out_ref[...] = (f_ref[...] + o_ref[...]).astype(BF16)

    full_map = (lambda k, i, c: (k, c[0] * per + i, 0)) if axis == 1 else (lambda k, i, c: (k, i, c[0]))
    grid_spec = pltpu.PrefetchScalarGridSpec(
        num_scalar_prefetch=1, grid=(n, per),
        in_specs=[pl.BlockSpec((1, tr, cols), full_map),
                  pl.BlockSpec((1, tr, cols), lambda k, i, c: (k, i, 0))],
        out_specs=pl.BlockSpec((1, tr, cols), lambda k, i, c: (k, i, 0)))
    return pl.pallas_call(
        body, grid_spec=grid_spec, out_shape=jax.ShapeDtypeStruct((n, rows, cols), BF16), name=name,
        compiler_params=_params(("parallel", "parallel")))(core, full, other)


def _allgather8(block, src_rows, vmem, name):
    n = block.shape[1]
    m = src_rows
    sliced = block.shape[0] != m

    def body(x_ref, out_ref, send_sems, recv_sems, local_sem):
        x, y, c = _coords()
        me, sibling = (x, y, c), (x, y, 1 - c)
        chips = [(1 - x, y), (x, 1 - y), (1 - x, 1 - y)]
        src = x_ref.at[pl.ds(pl.multiple_of(c * m, 16), m), :] if sliced else x_ref

        def rows(px, py, pc):
            return out_ref.at[pl.ds(pl.multiple_of((4 * px + 2 * py + pc) * m, 8), m), :]

        def copy(k, blk, to, source=None):
            return pltpu.make_async_remote_copy(
                src_ref=rows(*blk) if source is None else source, dst_ref=rows(*blk),
                send_sem=send_sems.at[k], recv_sem=recv_sems.at[k], device_id=to, device_id_type=MESH)

        mine = pltpu.make_async_copy(src, rows(*me), local_sem)
        mine.start()
        first = [copy(0, me, sibling, source=src)]
        first += [copy(1 + j, me, (*chip, c), source=src) for j, chip in enumerate(chips)]
        for cp in first:
            cp.start()
        passed = [copy(4 + j, (*chip, c), sibling) for j, chip in enumerate(chips)]
        for j, chip in enumerate(chips):
            copy(1 + j, (*chip, c), me).wait_recv()
            passed[j].start()
        copy(0, sibling, me).wait_recv()
        for j, chip in enumerate(chips):
            copy(4 + j, (*chip, 1 - c), me).wait_recv()
        for cp in first + passed:
            cp.wait_send()
        mine.wait()

    space = pltpu.VMEM if vmem else pl.ANY
    return pl.pallas_call(
        body, out_shape=jax.ShapeDtypeStruct((N_DEV * m, n), block.dtype),
        in_specs=[pl.BlockSpec(memory_space=space)], out_specs=pl.BlockSpec(memory_space=space),
        scratch_shapes=[pltpu.SemaphoreType.DMA((7,)), pltpu.SemaphoreType.DMA((7,)), pltpu.SemaphoreType.DMA],
        name=name)(block)


def _when(pred, fn):
    if pred is None:
        fn()
    else:
        pl.when(pred)(fn)


def _gather_plan(x_refs, out_refs, send_sems, recv_sems, local_sems, senders):
    n = len(x_refs)
    halves = [r.shape[0] // 2 for r in x_refs]
    x, y, c = _coords()
    me, sibling = (x, y, c), (x, y, 1 - c)
    chips = [(1 - x, y), (x, 1 - y), (1 - x, 1 - y)]

    def sends(a):
        return None if senders[a] is None else x == senders[a]

    def arrives(a, j):
        if senders[a] is None:
            return None
        return x == senders[a] if j == 1 else x != senders[a]

    def src(a):
        return x_refs[a].at[pl.ds(pl.multiple_of(c * halves[a], 16), halves[a]), :]

    def blk(a, px, py, pc):
        return out_refs[a].at[4 * px + 2 * py + pc]

    def copy(a, k, who, to, source=None):
        return pltpu.make_async_remote_copy(
            src_ref=blk(a, *who) if source is None else source, dst_ref=blk(a, *who),
            send_sem=send_sems.at[7 * a + k], recv_sem=recv_sems.at[7 * a + k], device_id=to, device_id_type=MESH)

    def mine(a):
        return pltpu.make_async_copy(src(a), blk(a, *me), local_sems.at[a])

    def first(a):
        return ([copy(a, 0, me, sibling, source=src(a))]
                + [copy(a, 1 + j, me, (*chip, c), source=src(a)) for j, chip in enumerate(chips)])

    def onward(a, j):
        return copy(a, 4 + j, (*chips[j], c), sibling)

    def begin():
        def start_own(a):
            mine(a).start()
            for cp in first(a):
                cp.start()

        for a in range(n):
            _when(sends(a), functools.partial(start_own, a))

    def finish():
        def pass_on(a, j):
            copy(a, 1 + j, (*chips[j], c), me).wait_recv()
            onward(a, j).start()

        def own_sent(a):
            for cp in first(a):
                cp.wait_send()
            mine(a).wait()

        for j in range(3):
            for a in range(n):
                _when(arrives(a, j), functools.partial(pass_on, a, j))
        for a in range(n):
            _when(sends(a), lambda a=a: copy(a, 0, sibling, me).wait_recv())
        for j in range(3):
            for a in range(n):
                _when(arrives(a, j), lambda a=a, j=j: copy(a, 4 + j, (*chips[j], 1 - c), me).wait_recv())
        for j in range(3):
            for a in range(n):
                _when(arrives(a, j), lambda a=a, j=j: onward(a, j).wait_send())
        for a in range(n):
            _when(sends(a), functools.partial(own_sent, a))

    return begin, finish


def _gather_operands(shards):
    n = len(shards)
    shapes = [jax.ShapeDtypeStruct((N_DEV, a.shape[0] // 2, a.shape[1]), a.dtype) for a in shards]
    sems = [pltpu.SemaphoreType.DMA((7 * n,)), pltpu.SemaphoreType.DMA((7 * n,)), pltpu.SemaphoreType.DMA((n,))]
    return shapes, sems


def _as_chip_slabs(gathered, shards):
    return [o.reshape(N_CHIP, a.shape[0], a.shape[1]) for o, a in zip(gathered, shards)]


def _gather_weights(shards, senders):
    n = len(shards)

    def body(*refs):
        begin, finish = _gather_plan(refs[:n], refs[n:2 * n], *refs[2 * n:], senders)
        begin()
        finish()

    shapes, sems = _gather_operands(shards)
    return pl.pallas_call(
        body, out_shape=shapes, in_specs=[HBM_REF] * n, out_specs=[HBM_REF] * n, scratch_shapes=sems,
        name="gather_weights")(*shards)


def _gather_alongside(body, n_in, n_out, n_shards, last_step, senders):
    def wrapped(*refs):
        ins, shards = refs[:n_in], refs[n_in:n_in + n_shards]
        rest = refs[n_in + n_shards + 1:]
        outs, gathered = rest[:n_out], rest[n_out:n_out + n_shards]
        scratch, sems = rest[n_out + n_shards:-3], rest[-3:]

        @pl.when(pl.program_id(0) == 0)
        def _():
            _gather_plan(shards, gathered, *sems, senders)[0]()

        body(*ins, *outs, *scratch)

        @pl.when(pl.program_id(0) == last_step)
        def _():
            _gather_plan(shards, gathered, *sems, senders)[1]()

    return wrapped


def _half(ref, axis, which, ndim):
    size = ref.shape[axis] // 2
    idx = [slice(None)] * ndim
    idx[axis] = pl.ds(pl.multiple_of(which * size, 8 if axis == ndim - 2 else LANE), size)
    return ref.at[tuple(idx)]


def _swap_halves_with_sibling(fulls, name, axes):
    n = len(fulls)

    def body(*refs):
        f_refs, got_refs = refs[:n], refs[n:2 * n]
        send_sems, recv_sems = refs[2 * n:]
        x, y, c = _coords()
        copies = []
        for a in range(n):
            copies.append(pltpu.make_async_remote_copy(
                src_ref=_half(f_refs[a], axes[a], 1 - c, 3), dst_ref=got_refs[a], send_sem=send_sems.at[a],
                recv_sem=recv_sems.at[a], device_id=(x, y, 1 - c), device_id_type=MESH))
        for cp in copies:
            cp.start()
        for cp in copies:
            cp.wait()

    def halved(a, axis):
        shape = list(a.shape)
        shape[axis] //= 2
        return jax.ShapeDtypeStruct(tuple(shape), a.dtype)

    return pl.pallas_call(
        body, out_shape=[halved(a, ax) for a, ax in zip(fulls, axes)],
        in_specs=[HBM_REF] * n, out_specs=[HBM_REF] * n,
        scratch_shapes=[pltpu.SemaphoreType.DMA((n,)), pltpu.SemaphoreType.DMA((n,))],
        name=name)(*fulls)


def _join_halves_with_sibling(wholes, axes):
    n = len(wholes)

    def body(*refs):
        out_refs = refs[n:2 * n]
        send_sems, recv_sems = refs[2 * n:]
        x, y, c = _coords()

        def push(a, core):
            half = _half(out_refs[a], axes[a] - 1, core, 2)
            return pltpu.make_async_remote_copy(
                src_ref=half, dst_ref=half, send_sem=send_sems.at[a], recv_sem=recv_sems.at[a],
                device_id=(x, y, 1 - c), device_id_type=MESH)

        for a in range(n):
            push(a, c).start()
        for a in range(n):
            push(a, 1 - c).wait_recv()
        for a in range(n):
            push(a, c).wait_send()

    return pl.pallas_call(
        body, out_shape=[jax.ShapeDtypeStruct(a.shape, a.dtype) for a in wholes],
        in_specs=[HBM_REF] * n, out_specs=[HBM_REF] * n, input_output_aliases={a: a for a in range(n)},
        scratch_shapes=[pltpu.SemaphoreType.DMA((n,)), pltpu.SemaphoreType.DMA((n,))],
        name="rs_pair_join")(*wholes)


def _cols_to_slabs(g):
    rows, cols = g.shape
    return g.reshape(rows, N_CHIP, cols // N_CHIP).transpose(1, 0, 2)


def _slabs_to_cols(w):
    n, rows, cols = w.shape
    return w.transpose(1, 0, 2).reshape(rows, n * cols)


def _col_window(slabs, start, stop):
    n = slabs.shape[2]
    pieces = []
    for k in range(N_CHIP):
        lo, hi = max(start, k * n), min(stop, (k + 1) * n)
        if lo < hi:
            pieces.append(slabs[k][:, lo - k * n:hi - k * n])
    return pieces[0] if len(pieces) == 1 else jnp.concatenate(pieces, axis=1)


def _slabs_from_groups(groups, n):
    slabs = []
    for k in range(N_CHIP):
        pieces, off = [], 0
        for g in groups:
            lo, hi = max(k * n, off), min((k + 1) * n, off + g.shape[0])
            if lo < hi:
                pieces.append(g[lo - off:hi - off])
            off += g.shape[0]
        slabs.append(pieces[0] if len(pieces) == 1 else jnp.concatenate(pieces, axis=0))
    return jnp.stack(slabs)


def _uq_to_padded(w_uq):
    per = w_uq.reshape(RQ, H, DN + DR)
    nope = per[:, :, :DN].reshape(RQ, H * DN)
    rope = jnp.pad(per[:, :, DN:], ((0, 0), (0, 0), (0, LANE - DR))).reshape(RQ, H * LANE)
    return jnp.concatenate([nope, rope], axis=1)


def _uq_from_padded(g):
    nope = g[:, :H * DN].reshape(RQ, H, DN)
    rope = g[:, H * DN:].reshape(RQ, H, LANE)[:, :, :DR]
    return jnp.concatenate([nope, rope], axis=2).reshape(RQ, H * (DN + DR))


def _rope_tables(positions):
    inv_freq = ROPE_THETA ** (-jnp.arange(0, DR, 2, dtype=F32) / DR)
    ang = positions.astype(F32)[:, None] * inv_freq
    cos, sin = jnp.cos(ang), jnp.sin(ang)
    return jnp.tile(cos, (1, 4)), jnp.tile(jnp.concatenate([-sin, sin], axis=1), (1, 2))


def _pair_sums(fulls, core, tag, axes, tr):
    from_sibling = _swap_halves_with_sibling(fulls, f"rs_pair_swap_{tag}", axes)
    return [_add_own_half(f, o, core, min(tr, o.shape[1]), f"add_own_half_{tag}{n}", ax)
            for n, (f, o, ax) in enumerate(zip(fulls, from_sibling, axes))]


def _local_step(x, tgt, cos_t, sin_t, mod, weights, small, tiles, place):
    ts, ts_in, tm_nn, tm_tn, t_attn, chunk = tiles
    wa, w_in_shard, w_in_begun, later_shards, conv_w = weights
    norm_w, conv_b, ln_w, ln_b, q_norm_w, kv_norm_w, fnw = small
    shift, scale, gate = mod[:, 0:D], mod[:, D:2 * D], mod[:, 2 * D:3 * D]

    h = _adaln_norm(x, norm_w, shift, scale, ts)
    proj_a = _mm_nn(h, wa, tm_nn, D, "proj_a")
    u0, u1, za, (g_in, g_uq, g_ukv, g_co, g_ao, g_o) = _conv_fwd(
        proj_a, conv_w, conv_b, ln_w, ln_b, ts, chunk, [w_in_shard] + later_shards, [1] + [None] * len(later_shards),
        w_in_begun)
    wl = jnp.pad(_col_window(g_in, A_COLS, A_COLS + L_COLS_RAW), ((0, 0), (0, L_COLS - L_COLS_RAW)))
    wg = _col_window(g_in, A_COLS + L_COLS_RAW, IN_COLS)
    w_uq2, w_ukv = _uq_to_padded(_slabs_to_cols(g_uq)), _slabs_to_cols(g_ukv)
    wco, wao, wo = g_co.reshape(D, D), g_ao.reshape(D, D), g_o.reshape(D, D)
    proj_l = _mm_nn(h, wl, tm_nn, L_COLS, "proj_l")
    proj_g = _mm_nn(h, wg, tm_nn, D, "proj_g")
    qn, kvn, q, k, v = _mla_prep(proj_l, q_norm_w, kv_norm_w, w_uq2, w_ukv, cos_t, sin_t, ts)
    o, lse = _attn_fwd(q, k, v, t_attn)
    (dx2, dza, do, delta, dpg, zb, mg, dmo, dya, dyb, vec_mid) = _middle(
        za, o, proj_g, x, tgt, gate, fnw, wco, wao, wo, ts)
    g_wo = _mm_tn(mg, dmo, tm_tn, D, D, "grad_w_out", BF16)
    g_wco = _mm_tn(za, dya, tm_tn, D, D, "grad_w_conv_out", BF16)
    g_wao = _mm_tn(zb, dyb, tm_tn, D, D, "grad_w_attn_out", BF16)
    dq, dk, dv = _attn_bwd(q, k, v, do, lse, delta, t_attn)
    dpl, g_wuq2, g_wukv, vec_mla = _mla_prep_bwd(
        dq, dk, dv, proj_l, qn, kvn, q_norm_w, kv_norm_w, w_uq2, w_ukv, cos_t, sin_t, ts)

    core = place[1:2]
    nr = D // N_CHIP
    early = [_cols_to_slabs(_uq_from_padded(g_wuq2)).astype(BF16), _cols_to_slabs(g_wukv).astype(BF16),
             g_wco.reshape(N_CHIP, nr, D), g_wao.reshape(N_CHIP, nr, D), g_wo.reshape(N_CHIP, nr, D)]
    dpa, g_conv_w, vec_conv, early_got = _conv_bwd(dza, proj_a, u0, u1, conv_w, ln_w, ln_b, ts, chunk, early)

    g_wa_t = _mm_tn(dpa, h, tm_tn, D, D, "grad_w_in_a")
    g_wl_t = _mm_tn(dpl, h, tm_tn, L_COLS, D, "grad_w_in_l")
    g_wg_t = _mm_tn(dpg, h, tm_tn, D, D, "grad_w_in_g")
    g_w_in_slabs = _slabs_from_groups([g_wa_t, g_wl_t[0:L_COLS_RAW], g_wg_t], IN_COLS // N_CHIP)
    late_sums = _pair_sums([g_w_in_slabs], core, "b", [2], W_IN_ROWS)
    grad_x, vec_in, late_got = _input_bwd(dpa, dpl, dpg, wa, wl, wg, x, dx2, norm_w, scale, ts_in, late_sums)

    axes = [2] + [1] * len(early)
    wholes = [_sum_chip_slabs(a, p, place, min(W_IN_ROWS if ax == 2 else 128, a.shape[1]), f"sum_chip_slabs_{n}", ax)
              for n, (a, p, ax) in enumerate(zip(late_got + early_got, late_sums + early, axes))]
    shards = _join_halves_with_sibling(wholes, axes)

    col_sums = jnp.concatenate(
        [vec_in, vec_mid, vec_conv, jnp.pad(vec_mla, ((0, 0), (0, D - RQ))), g_conv_w], axis=0)
    return grad_x, shards, col_sums


def kernel(x, c, positions, w_ada, b_ada, norm_w, w_in, conv_w, conv_b, conv_ln_w, conv_ln_b, w_conv_out, q_norm_w, w_uq, kv_norm_w, w_ukv, w_attn_out, w_out, final_norm_w, loss_target, m_w_ada, m_b_ada, m_norm_w, m_w_in, m_conv_w, m_conv_b, m_conv_ln_w, m_conv_ln_b, m_w_conv_out, m_q_norm_w, m_w_uq, m_kv_norm_w, m_w_ukv, m_w_attn_out, m_w_out, m_final_norm_w, v_w_ada, v_b_ada, v_norm_w, v_w_in, v_conv_w, v_conv_b, v_conv_ln_w, v_conv_ln_b, v_w_conv_out, v_q_norm_w, v_w_uq, v_kv_norm_w, v_w_ukv, v_w_attn_out, v_w_out, v_final_norm_w):
    ix, iy, ic = _coords()
    chip = 2 * ix + iy
    dev = 4 * ix + 2 * iy + ic
    s = x.shape[1]
    tiles = (256, 512, 1024, 2048, 512, 32)

    conv_w_pad = jnp.pad(conv_w[0], ((0, HALO - KC), (0, 0)))
    small_in = jnp.concatenate([c.reshape(8, LANE), conv_w_pad.reshape(64, LANE)], axis=0)
    small_all = _allgather8(small_in, 72, True, "gather_c_conv").reshape(N_DEV, 72, LANE)
    c_all = small_all[:, 0:8].reshape(N_DEV, D)
    conv_full = jnp.concatenate(
        [small_all[2 * k, 8:72].reshape(HALO, D // N_CHIP) for k in range(N_CHIP)], axis=1)

    w_in_shard = w_in[0].astype(BF16)
    w_in_begun = _gather_weights([w_in_shard], [0])
    wa = _col_window(_as_chip_slabs(w_in_begun, [w_in_shard])[0], 0, A_COLS)
    later_shards = [w[0].astype(BF16) for w in (w_uq, w_ukv, w_conv_out, w_attn_out, w_out)]
    weights = (wa, w_in_shard, w_in_begun[0], later_shards, conv_full)

    ada_cols = w_ada.shape[2]
    b_shard = lax.dynamic_slice(b_ada, (0, chip * ada_cols), (1, ada_cols))
    mod_part = _ada_fwd(c_all, w_ada[0], b_shard)
    mod_all = _allgather8(mod_part, N_DEV, True, "gather_mod").reshape(N_DEV, N_DEV, ada_cols)
    mod = jnp.concatenate(
        [lax.dynamic_slice(mod_all[2 * k], (dev, 0), (1, ada_cols)) for k in range(N_CHIP)], axis=1)

    cos_t, sin_t = _rope_tables(positions[0])
    small = (norm_w, conv_b, conv_ln_w, conv_ln_b, q_norm_w, kv_norm_w, final_norm_w.reshape(1, D))
    place = jnp.stack([chip, ic]).astype(jnp.int32)
    grad_x, shards, col_sums = _local_step(x[0], loss_target[0], cos_t, sin_t, mod, weights, small, tiles, place)
    g_w_in_s, g_w_uq_s, g_w_ukv_s, g_wco_s, g_wao_s, g_wo_s = shards

    gathered = _allgather8(col_sums, SUM_ROWS, True, "gather_small_grads").reshape(N_DEV, SUM_ROWS, D)
    vec_names = ("b_ada", "norm_w", "conv_b", "conv_ln_w", "conv_ln_b", "q_norm_w", "kv_norm_w", "final_norm_w")
    row = lambda a: a.reshape(1, -1)
    vectors = [(row(b_ada), row(m_b_ada), row(v_b_ada)), (norm_w, m_norm_w, v_norm_w), (conv_b, m_conv_b, v_conv_b),
               (conv_ln_w, m_conv_ln_w, v_conv_ln_w), (conv_ln_b, m_conv_ln_b, v_conv_ln_b),
               (q_norm_w, m_q_norm_w, v_q_norm_w), (kv_norm_w, m_kv_norm_w, v_kv_norm_w),
               (row(final_norm_w), row(m_final_norm_w), row(v_final_norm_w))]
    fin = _small_finalize(gathered, vectors, (conv_w, m_conv_w, v_conv_w), place[0:1])
    res = {}
    for p, (name, (w, _, _)) in enumerate(zip(vec_names, vectors)):
        shape = final_norm_w.shape if name == "final_norm_w" else w.shape
        res[name] = tuple(a.reshape(shape) for a in fin[4 * p:4 * p + 4])
    res["conv_w"] = tuple(fin[4 * len(vectors):4 * len(vectors) + 4])
    dmod_all, loss = fin[-2], fin[-1].reshape(())
    dmod_shard = lax.dynamic_slice(dmod_all, (0, chip * ada_cols), (N_DEV, ada_cols))
    g_w_ada = _ada_bwd(c_all.T, dmod_shard).reshape(1, D, ada_cols)

    def big(w, g, m, v, tr, name):
        d, nm, nv = _adamw(w, g, m, v, tr, name)
        return g.reshape(w.shape), d, nm, nv

    res["w_ada"] = big(w_ada, g_w_ada[0], m_w_ada, v_w_ada, 256, "adamw_w_ada")
    t_in = [a[0].T for a in (w_in, m_w_in, v_w_in)]
    d_t, nm_t, nv_t = _adamw(t_in[0], g_w_in_s, t_in[1], t_in[2], W_IN_ROWS, "adamw_w_in")
    res["w_in"] = tuple(a.T[None] for a in (g_w_in_s, d_t, nm_t, nv_t))
    res["w_conv_out"] = big(w_conv_out, g_wco_s, m_w_conv_out, v_w_conv_out, 256, "adamw_w_conv_out")
    res["w_uq"] = big(w_uq, g_w_uq_s, m_w_uq, v_w_uq, 256, "adamw_w_uq")
    res["w_ukv"] = big(w_ukv, g_w_ukv_s, m_w_ukv, v_w_ukv, 256, "adamw_w_ukv")
    res["w_attn_out"] = big(w_attn_out, g_wao_s, m_w_attn_out, v_w_attn_out, 256, "adamw_w_attn_out")
    res["w_out"] = big(w_out, g_wo_s, m_w_out, v_w_out, 256, "adamw_w_out")

    order = ("w_ada", "b_ada", "norm_w", "w_in", "conv_w", "conv_b", "conv_ln_w", "conv_ln_b", "w_conv_out",
             "q_norm_w", "w_uq", "kv_norm_w", "w_ukv", "w_attn_out", "w_out", "final_norm_w")
    outs = [loss, grad_x[None]]
    for slot in range(4):
        outs += [res[name][slot] for name in order]
    return tuple(outs)
```

```python
import functools

import numpy as np
import jax
import jax.numpy as jnp
from jax import lax
from jax.experimental import pallas as pl
from jax.experimental.pallas import tpu as pltpu

F32 = jnp.float32
BF16 = jnp.bfloat16
MESH = pl.DeviceIdType.MESH

D = 1024
H = 8
DN = 128
DR = 64
RQ = 256
KC = 31
HALO = 32
EPS = 1e-6
ROPE_THETA = 10000.0
N_CHIP = 4
N_DEV = 8
LANE = 128
VMEM_BIG = 56 * 1024 * 1024

ADAM_LR = 0.001
ADAM_B1 = 0.9
ADAM_B2 = 0.999
ADAM_EPS = 1e-08
ADAM_WD = 0.01
ADAM_STEP = 10

A_COLS = 3 * D
L_COLS_RAW = RQ + RQ + DR
L_COLS = 640
G_COLS = 3 * D
IN_COLS = A_COLS + L_COLS_RAW + G_COLS


def _params(sem=None, vmem=None):
    kw = {}
    if sem is not None:
        kw["dimension_semantics"] = sem
    if vmem is not None:
        kw["vmem_limit_bytes"] = vmem
    return pltpu.CompilerParams(**kw)


def _dot(a, b):
    return jnp.dot(a, b, preferred_element_type=F32)


def _dot_nt(a, b):
    return lax.dot_general(a, b, (((1,), (1,)), ((), ())), preferred_element_type=F32)


def _dot_tn(a, b):
    return lax.dot_general(a, b, (((0,), (0,)), ((), ())), preferred_element_type=F32)


def _colsum(v):
    return jnp.sum(v, axis=0, keepdims=True)


def _rowmean(v):
    return jnp.mean(v, axis=-1, keepdims=True)


def _sigmoid(v):
    return jax.nn.sigmoid(v)


def _dsilu(v, s):
    return s * (1.0 + v * (1.0 - s))


def _swap_halves(v, first_half):
    return jnp.where(first_half, pltpu.roll(v, 96, 1), pltpu.roll(v, 32, 1))


def _first_half_mask(rows):
    lane = lax.broadcasted_iota(jnp.int32, (rows, LANE), 1)
    return (lane % 64) < 32


def _adaln_norm(x, norm_w, shift, scale, ts, shards):
    s = x.shape[0]

    def body(x_ref, nw_ref, sh_ref, sc_ref, h_ref):
        xv = x_ref[...]
        r = lax.rsqrt(_rowmean(xv * xv) + EPS)
        y = xv * r * nw_ref[...]
        h_ref[...] = (y * (1.0 + sc_ref[...]) + sh_ref[...]).astype(BF16)

    row = pl.BlockSpec((ts, D), lambda i: (i, 0))
    vec = pl.BlockSpec((1, D), lambda i: (0, 0))
    n = len(shards)
    gathered_shapes, sems = _gather_operands(shards)
    outs = pl.pallas_call(
        _gather_alongside(body, 4, 1, n, s // ts - 1), grid=(s // ts,),
        in_specs=[row, vec, vec, vec] + [HBM_REF] * n, out_specs=[row] + [HBM_REF] * n,
        out_shape=[jax.ShapeDtypeStruct((s, D), BF16)] + gathered_shapes, scratch_shapes=sems, name="adaln_norm",
        compiler_params=_params(("arbitrary",)))(x, norm_w, shift, scale, *shards)
    return outs[0], _as_chip_slabs(outs[1:], shards)


def _mm_nn(a, b, tm, tn, name):
    m, k = a.shape
    n = b.shape[1]

    def body(a_ref, b_ref, o_ref):
        o_ref[...] = _dot(a_ref[...], b_ref[...])

    return pl.pallas_call(
        body, grid=(n // tn, m // tm),
        in_specs=[pl.BlockSpec((tm, k), lambda j, i: (i, 0)), pl.BlockSpec((k, tn), lambda j, i: (0, j))],
        out_specs=pl.BlockSpec((tm, tn), lambda j, i: (i, j)),
        out_shape=jax.ShapeDtypeStruct((m, n), F32), name=name,
        compiler_params=_params(("parallel", "parallel"), VMEM_BIG))(a, b)


def _mm_tn(a, b, tm, tk, tn, name, out_dtype=F32):
    m, k = a.shape
    n = b.shape[1]
    steps = m // tm

    def body(a_ref, b_ref, o_ref, acc_ref):
        @pl.when(pl.program_id(2) == 0)
        def _():
            acc_ref[...] = jnp.zeros_like(acc_ref)
        acc_ref[...] += _dot_tn(a_ref[...], b_ref[...])

        @pl.when(pl.program_id(2) == steps - 1)
        def _():
            o_ref[...] = acc_ref[...].astype(out_dtype)

    return pl.pallas_call(
        body, grid=(k // tk, n // tn, steps),
        in_specs=[pl.BlockSpec((tm, tk), lambda r, j, i: (i, r)), pl.BlockSpec((tm, tn), lambda r, j, i: (i, j))],
        out_specs=pl.BlockSpec((tk, tn), lambda r, j, i: (r, j)),
        out_shape=jax.ShapeDtypeStruct((k, n), out_dtype), scratch_shapes=[pltpu.VMEM((tk, tn), F32)], name=name,
        compiler_params=_params(("parallel", "parallel", "arbitrary"), VMEM_BIG))(a, b)


def _coords():
    return lax.axis_index("x"), lax.axis_index("y"), lax.axis_index("c")


HBM_REF = pl.BlockSpec(memory_space=pl.ANY)


def _chip_scatter_copies(p_refs, got_refs, send_sems, recv_sems):
    x, y, c = _coords()
    copies = []
    for a in range(len(p_refs)):
        for j, (px, py) in enumerate([(1 - x, y), (x, 1 - y), (1 - x, 1 - y)]):
            copies.append(pltpu.make_async_remote_copy(
                src_ref=p_refs[a].at[2 * px + py], dst_ref=got_refs[a].at[j], send_sem=send_sems.at[3 * a + j],
                recv_sem=recv_sems.at[3 * a + j], device_id=(px, py, c), device_id_type=MESH))
    return copies


RELATIONS = [(dx, dy, dc) for dx in (0, 1) for dy in (0, 1) for dc in (0, 1)][1:]


def _device_scatter_copies(p_refs, got_refs, send_sems, recv_sems):
    x, y, c = _coords()
    copies = []
    for a in range(len(p_refs)):
        half = p_refs[a].shape[1] // 2
        for j, (dx, dy, dc) in enumerate(RELATIONS):
            px, py, pc = (1 - x if dx else x), (1 - y if dy else y), (1 - c if dc else c)
            src = p_refs[a].at[2 * px + py, pl.ds(pl.multiple_of(pc * half, 16), half), :]
            copies.append(pltpu.make_async_remote_copy(
                src_ref=src, dst_ref=got_refs[a].at[j], send_sem=send_sems.at[7 * a + j],
                recv_sem=recv_sems.at[7 * a + j], device_id=(px, py, pc), device_id_type=MESH))
    return copies


def _scatter_alongside(body, n_in, n_out, n_parts, last_step, make_copies):
    def wrapped(*refs):
        ins, parts = refs[:n_in], refs[n_in:n_in + n_parts]
        rest = refs[n_in + n_parts:]
        outs, got = rest[:n_out], rest[n_out:n_out + n_parts]
        scratch, (send_sems, recv_sems) = rest[n_out + n_parts:-2], rest[-2:]

        @pl.when(pl.program_id(0) == 0)
        def _():
            for cp in make_copies(parts, got, send_sems, recv_sems):
                cp.start()

        body(*ins, *outs, *scratch)

        @pl.when(pl.program_id(0) == last_step)
        def _():
            for cp in make_copies(parts, got, send_sems, recv_sems):
                cp.wait()

    return wrapped


def _scatter_operands(parts, per_device):
    n = len(parts)
    if per_device:
        slots, shapes = 7, [jax.ShapeDtypeStruct((7, a.shape[1] // 2, a.shape[2]), a.dtype) for a in parts]
    else:
        slots, shapes = 3, [jax.ShapeDtypeStruct((3,) + a.shape[1:], a.dtype) for a in parts]
    sems = [pltpu.SemaphoreType.DMA((slots * n,)), pltpu.SemaphoreType.DMA((slots * n,))]
    return [HBM_REF] * n, [HBM_REF] * n, shapes, sems


def _shifted_copies(win_ref, sh_ref, rows):
    for p in range(1, 8):
        sh_ref[p - 1, 0:rows, :] = win_ref[pl.ds(p, rows), :]


def _tap_rows(win_ref, sh_ref, start, rows):
    p = start % 8
    if p == 0:
        return win_ref[pl.ds(start, rows), :]
    return sh_ref[p - 1, pl.ds(start - p, rows), :]


def _conv_taps(win_ref, sh_ref, w_ref, rows, chunk, offset_of_tap):
    pieces = []
    for c0 in range(0, rows, chunk):
        acc = None
        for j in range(KC):
            term = w_ref[j:j + 1, :] * _tap_rows(win_ref, sh_ref, c0 + offset_of_tap(j), chunk)
            acc = term if acc is None else acc + term
        pieces.append(acc)
    return pieces


def _conv_fwd(proj_a, conv_w, conv_b, ln_w, ln_b, ts, chunk, shards):
    s = proj_a.shape[0]

    def body(av_ref, al_ref, ag_ref, w_ref, b_ref, lw_ref, lb_ref, u0_ref, u1_ref, za_ref, win_ref, sh_ref):
        @pl.when(pl.program_id(0) == 0)
        def _():
            win_ref[0:HALO, :] = jnp.zeros((HALO, D), F32)

        u0 = av_ref[...] * _sigmoid(al_ref[...])
        u0_ref[...] = u0
        win_ref[HALO:HALO + ts, :] = u0
        _shifted_copies(win_ref, sh_ref, ts + HALO - 8)
        pieces = _conv_taps(win_ref, sh_ref, w_ref, ts, chunk, lambda j: HALO - (KC - 1) + j)
        for n, acc in enumerate(pieces):
            u1_ref[n * chunk:(n + 1) * chunk, :] = acc + b_ref[...]
        win_ref[0:HALO, :] = win_ref[ts:ts + HALO, :]

        u1 = u1_ref[...]
        xc = u1 - _rowmean(u1)
        rstd = lax.rsqrt(_rowmean(xc * xc) + EPS)
        u2 = xc * rstd * lw_ref[...] + lb_ref[...]
        u3 = u2 * _sigmoid(u2)
        ag = ag_ref[...]
        za_ref[...] = (u3 * (ag * _sigmoid(ag))).astype(BF16)

    col = lambda c: pl.BlockSpec((ts, D), lambda i, c=c: (i, c))
    row = pl.BlockSpec((ts, D), lambda i: (i, 0))
    vec = pl.BlockSpec((1, D), lambda i: (0, 0))
    n = len(shards)
    gathered_shapes, sems = _gather_operands(shards)
    outs = pl.pallas_call(
        _gather_alongside(body, 7, 3, n, s // ts - 1), grid=(s // ts,),
        in_specs=[col(0), col(1), col(2), pl.BlockSpec((HALO, D), lambda i: (0, 0)), vec, vec, vec] + [HBM_REF] * n,
        out_specs=[row, row, row] + [HBM_REF] * n,
        out_shape=[jax.ShapeDtypeStruct((s, D), F32), jax.ShapeDtypeStruct((s, D), F32),
                   jax.ShapeDtypeStruct((s, D), BF16)] + gathered_shapes,
        scratch_shapes=[pltpu.VMEM((ts + HALO, D), F32), pltpu.VMEM((7, ts + HALO, D), F32)] + sems,
        name="conv_fwd", compiler_params=_params(("arbitrary",), VMEM_BIG))(
            proj_a, proj_a, proj_a, conv_w, conv_b, ln_w, ln_b, *shards)
    return outs[0], outs[1], outs[2], _as_chip_slabs(outs[3:], shards)


def _conv_bwd(dza, proj_a, u0, u1, conv_w, ln_w, ln_b, ts, chunk, parts):
    s = dza.shape[0]
    nt = s // ts
    per = ts // HALO

    def body(dza_ref, av_ref, al_ref, ag_ref, u0_ref, u0p_ref, u1_ref, w_ref, lw_ref, lb_ref,
             dpa_ref, gw_ref, gv_ref, dwin_ref, uwin_ref, du0_ref, gwp_ref, dsh_ref, ush_ref):
        step = pl.program_id(0)
        tile = nt - 1 - step

        @pl.when(step == 0)
        def _():
            dwin_ref[ts:ts + HALO, :] = jnp.zeros((HALO, D), F32)
            gwp_ref[...] = jnp.zeros_like(gwp_ref)
            gv_ref[...] = jnp.zeros_like(gv_ref)

        ag = ag_ref[...]
        sg = _sigmoid(ag)
        u1 = u1_ref[...]
        xc = u1 - _rowmean(u1)
        rstd = lax.rsqrt(_rowmean(xc * xc) + EPS)
        xh = xc * rstd
        u2 = xh * lw_ref[...] + lb_ref[...]
        s2 = _sigmoid(u2)
        dz = dza_ref[...]
        du3 = dz * (ag * sg)
        dpa_ref[:, 2 * D:3 * D] = (dz * (u2 * s2) * _dsilu(ag, sg)).astype(BF16)
        du2 = du3 * _dsilu(u2, s2)
        gv_ref[0:1, :] += _colsum(du2 * xh)
        gv_ref[1:2, :] += _colsum(du2)
        dxh = du2 * lw_ref[...]
        du1 = rstd * (dxh - _rowmean(dxh) - xh * _rowmean(dxh * xh))
        gv_ref[2:3, :] += _colsum(du1)
        dwin_ref[0:ts, :] = du1

        uwin_ref[0:HALO, :] = jnp.where(tile == 0, 0.0, u0p_ref[...])
        uwin_ref[HALO:HALO + ts, :] = u0_ref[...]

        _shifted_copies(dwin_ref, dsh_ref, ts + HALO - 8)
        _shifted_copies(uwin_ref, ush_ref, ts + HALO - 8)
        pieces = _conv_taps(dwin_ref, dsh_ref, w_ref, ts, chunk, lambda j: (KC - 1) - j)
        for n, acc in enumerate(pieces):
            du0_ref[n * chunk:(n + 1) * chunk, :] = acc
        for c0 in range(0, ts, chunk):
            dchunk = dwin_ref[c0:c0 + chunk, :]
            for j in range(KC):
                prod = dchunk * _tap_rows(uwin_ref, ush_ref, c0 + HALO - (KC - 1) + j, chunk)
                gwp_ref[8 * j:8 * j + 8, :] += jnp.sum(prod.reshape(chunk // 8, 8, D), axis=0)
        dwin_ref[ts:ts + HALO, :] = dwin_ref[0:HALO, :]

        du0 = du0_ref[...]
        al = al_ref[...]
        sl = _sigmoid(al)
        dpa_ref[:, 0:D] = (du0 * sl).astype(BF16)
        dpa_ref[:, D:2 * D] = (du0 * av_ref[...] * sl * (1.0 - sl)).astype(BF16)

        @pl.when(step == nt - 1)
        def _():
            for j in range(KC):
                gw_ref[j:j + 1, :] = _colsum(gwp_ref[8 * j:8 * j + 8, :])
            gw_ref[KC:HALO, :] = jnp.zeros((HALO - KC, D), F32)

    rev = lambda i: nt - 1 - i
    col = lambda c: pl.BlockSpec((ts, D), lambda i, c=c: (rev(i), c))
    row = pl.BlockSpec((ts, D), lambda i: (rev(i), 0))
    vec = pl.BlockSpec((1, D), lambda i: (0, 0))
    halo = pl.BlockSpec((HALO, D), lambda i: (jnp.maximum(rev(i) * per - 1, 0), 0))
    side_in, side_out, side_shapes, side_sems = _scatter_operands(parts, True)
    outs = pl.pallas_call(
        _scatter_alongside(body, 10, 3, len(parts), nt - 1, _device_scatter_copies), grid=(nt,),
        in_specs=[row, col(0), col(1), col(2), row, halo, row, pl.BlockSpec((HALO, D), lambda i: (0, 0)), vec, vec]
        + side_in,
        out_specs=[pl.BlockSpec((ts, A_COLS), lambda i: (rev(i), 0)),
                   pl.BlockSpec((HALO, D), lambda i: (0, 0)), pl.BlockSpec((8, D), lambda i: (0, 0))] + side_out,
        out_shape=[jax.ShapeDtypeStruct((s, A_COLS), BF16), jax.ShapeDtypeStruct((HALO, D), F32),
                   jax.ShapeDtypeStruct((8, D), F32)] + side_shapes,
        scratch_shapes=[pltpu.VMEM((ts + HALO, D), F32), pltpu.VMEM((ts + HALO, D), F32),
                        pltpu.VMEM((ts, D), F32), pltpu.VMEM((8 * HALO, D), F32),
                        pltpu.VMEM((7, ts + HALO, D), F32), pltpu.VMEM((7, ts + HALO, D), F32)] + side_sems,
        name="conv_bwd", compiler_params=_params(("arbitrary",), VMEM_BIG))(
            dza, proj_a, proj_a, proj_a, u0, u0, u1, conv_w, ln_w, ln_b, *parts)
    return outs[0], outs[1], outs[2], list(outs[3:])


def _mla_prep(proj_l, q_norm_w, kv_norm_w, w_uq2, w_ukv, cos_t, sin_t, ts):
    s = proj_l.shape[0]

    def body(pl_ref, qw_ref, kw_ref, wq_ref, wkv_ref, c_ref, s_ref, qn_ref, kvn_ref, q_ref, k_ref, v_ref):
        first = _first_half_mask(ts)
        cs = c_ref[...]
        sn = s_ref[...]

        def rms(v, w):
            return v * lax.rsqrt(_rowmean(v * v) + EPS) * w

        def rope(v):
            return v * cs + _swap_halves(v, first) * sn

        qn = rms(pl_ref[:, 0:RQ], qw_ref[...]).astype(BF16)
        kvn = rms(pl_ref[:, RQ:2 * RQ], kw_ref[...]).astype(BF16)
        qn_ref[...] = qn
        kvn_ref[...] = kvn
        q = _dot(qn, wq_ref[...])
        kv = _dot(kvn, wkv_ref[...])
        kr = rope(pl_ref[:, 2 * RQ:2 * RQ + LANE]).astype(BF16)
        for h in range(H):
            q_ref[h, :, 0:DN] = q[:, DN * h:DN * (h + 1)].astype(BF16)
            q_ref[h, :, DN:2 * DN] = rope(q[:, H * DN + LANE * h:H * DN + LANE * (h + 1)]).astype(BF16)
            k_ref[h, :, 0:DN] = kv[:, 2 * DN * h:2 * DN * h + DN].astype(BF16)
            k_ref[h, :, DN:2 * DN] = kr
            v_ref[h, :, 0:DN] = kv[:, 2 * DN * h + DN:2 * DN * (h + 1)].astype(BF16)
            v_ref[h, :, DN:2 * DN] = jnp.ones((ts, DN), BF16)

    const = lambda shape: pl.BlockSpec(shape, lambda i: (0,) * len(shape))
    rowb = lambda w: pl.BlockSpec((ts, w), lambda i: (i, 0))
    head = lambda w: pl.BlockSpec((H, ts, w), lambda i: (0, i, 0))
    return pl.pallas_call(
        body, grid=(s // ts,),
        in_specs=[rowb(L_COLS), const((1, RQ)), const((1, RQ)), const((RQ, 2 * H * DN)), const((RQ, 2 * H * DN)),
                  rowb(LANE), rowb(LANE)],
        out_specs=[rowb(RQ), rowb(RQ), head(2 * DN), head(2 * DN), head(2 * DN)],
        out_shape=[jax.ShapeDtypeStruct((s, RQ), BF16), jax.ShapeDtypeStruct((s, RQ), BF16),
                   jax.ShapeDtypeStruct((H, s, 2 * DN), BF16), jax.ShapeDtypeStruct((H, s, 2 * DN), BF16),
                   jax.ShapeDtypeStruct((H, s, 2 * DN), BF16)],
        name="mla_prep", compiler_params=_params(("parallel",)))(
            proj_l, q_norm_w, kv_norm_w, w_uq2, w_ukv, cos_t, sin_t)


def _mla_prep_bwd(dq, dk, dv, proj_l, qn, kvn, q_norm_w, kv_norm_w, w_uq2, w_ukv, cos_t, sin_t, ts):
    s = proj_l.shape[0]

    def body(dq_ref, dk_ref, dv_ref, pl_ref, qn_ref, kvn_ref, qw_ref, kw_ref, wq_ref, wkv_ref, c_ref, s_ref,
             dpl_ref, gwq_ref, gwkv_ref, gv_ref, dq2_ref, dkv2_ref):
        @pl.when(pl.program_id(0) == 0)
        def _():
            gwq_ref[...] = jnp.zeros_like(gwq_ref)
            gwkv_ref[...] = jnp.zeros_like(gwkv_ref)
            gv_ref[...] = jnp.zeros_like(gv_ref)

        first = _first_half_mask(ts)
        cs = c_ref[...] * ATT_SCALE
        sn = s_ref[...] * ATT_SCALE

        def rope_bwd(g):
            return g * cs + _swap_halves(g * sn, first)

        def rms_bwd(v, w, dy):
            r = lax.rsqrt(_rowmean(v * v) + EPS)
            vh = v * r
            dvh = dy * w
            return r * (dvh - vh * _rowmean(dvh * vh)), _colsum(dy * vh)

        dkr = None
        for h in range(H):
            dq2_ref[:, DN * h:DN * (h + 1)] = (dq_ref[h, :, 0:DN] * ATT_SCALE).astype(BF16)
            dq2_ref[:, H * DN + LANE * h:H * DN + LANE * (h + 1)] = rope_bwd(dq_ref[h, :, DN:2 * DN]).astype(BF16)
            dkv2_ref[:, 2 * DN * h:2 * DN * h + DN] = (dk_ref[h, :, 0:DN] * ATT_SCALE).astype(BF16)
            dkv2_ref[:, 2 * DN * h + DN:2 * DN * (h + 1)] = dv_ref[h].astype(BF16)
            part = dk_ref[h, :, DN:2 * DN]
            dkr = part if dkr is None else dkr + part

        dq2 = dq2_ref[...]
        dkv2 = dkv2_ref[...]
        gwq_ref[...] += _dot_tn(qn_ref[...], dq2)
        gwkv_ref[...] += _dot_tn(kvn_ref[...], dkv2)
        dcq, gq = rms_bwd(pl_ref[:, 0:RQ], qw_ref[...], _dot_nt(dq2, wq_ref[...]))
        dckv, gkv = rms_bwd(pl_ref[:, RQ:2 * RQ], kw_ref[...], _dot_nt(dkv2, wkv_ref[...]))
        gv_ref[0:1, :] += gq
        gv_ref[1:2, :] += gkv
        dpl_ref[:, 0:RQ] = dcq.astype(BF16)
        dpl_ref[:, RQ:2 * RQ] = dckv.astype(BF16)
        dpl_ref[:, 2 * RQ:2 * RQ + LANE] = rope_bwd(dkr).astype(BF16)

    const = lambda shape: pl.BlockSpec(shape, lambda i: (0,) * len(shape))
    rowb = lambda w: pl.BlockSpec((ts, w), lambda i: (i, 0))
    head = lambda w: pl.BlockSpec((H, ts, w), lambda i: (0, i, 0))
    return pl.pallas_call(
        body, grid=(s // ts,),
        in_specs=[head(2 * DN), head(2 * DN), head(DN), rowb(L_COLS), rowb(RQ), rowb(RQ), const((1, RQ)),
                  const((1, RQ)), const((RQ, 2 * H * DN)), const((RQ, 2 * H * DN)), rowb(LANE), rowb(LANE)],
        out_specs=[rowb(L_COLS), const((RQ, 2 * H * DN)), const((RQ, 2 * H * DN)), const((8, RQ))],
        out_shape=[jax.ShapeDtypeStruct((s, L_COLS), BF16), jax.ShapeDtypeStruct((RQ, 2 * H * DN), F32),
                   jax.ShapeDtypeStruct((RQ, 2 * H * DN), F32), jax.ShapeDtypeStruct((8, RQ), F32)],
        scratch_shapes=[pltpu.VMEM((ts, 2 * H * DN), BF16), pltpu.VMEM((ts, 2 * H * DN), BF16)],
        name="mla_prep_bwd", compiler_params=_params(("arbitrary",), VMEM_BIG))(
            dq, dk, dv, proj_l, qn, kvn, q_norm_w, kv_norm_w, w_uq2, w_ukv, cos_t, sin_t)


def _causal_pairs(n, by_key):
    if by_key:
        pairs = [(i, j) for j in range(n) for i in range(j, n)]
    else:
        pairs = [(i, j) for i in range(n) for j in range(i + 1)]
    return (jnp.asarray(np.array([p[0] for p in pairs], np.int32)),
            jnp.asarray(np.array([p[1] for p in pairs], np.int32)))


ATT_SCALE = float((DN + DR) ** -0.5)
LOG2E = 1.4426950408889634
LN2 = 0.6931471805599453
ATT_HEADS_FWD = 4
ATT_HEADS = 2
W_IN_ROWS = 336
ATT_ROWS = 64


def _diag_width(r0, t):
    return min(t, -(-(r0 + ATT_ROWS) // LANE) * LANE)


def _diag_mask_rows(r0, width):
    rows = r0 + lax.broadcasted_iota(jnp.int32, (ATT_ROWS, width), 0)
    cols = lax.broadcasted_iota(jnp.int32, (ATT_ROWS, width), 1)
    return cols <= rows


def _diag_mask(t):
    return lax.broadcasted_iota(jnp.int32, (t, t), 1) <= lax.broadcasted_iota(jnp.int32, (t, t), 0)


def _attn_fwd(q, k, v, t):
    s = q.shape[1]
    n = s // t
    scale2 = float((DN + DR) ** -0.5) * LOG2E
    qi, ki = _causal_pairs(n, by_key=False)

    def body(qi_ref, ki_ref, q_ref, k_ref, v_ref, o_ref, lse_ref, *scratch):
        per_head = [scratch[5 * h:5 * h + 5] for h in range(ATT_HEADS_FWD)]
        p = pl.program_id(1)
        i = qi_ref[p]
        j = ki_ref[p]

        @pl.when(j == 0)
        def _():
            for m_sc, acc_sc, _, _, _ in per_head:
                m_sc[...] = jnp.full_like(m_sc, -jnp.inf)
                acc_sc[...] = jnp.zeros_like(acc_sc)

        def scores(h, diag):
            sc = _dot_nt(q_ref[h], k_ref[h])
            if diag:
                sc = jnp.where(_diag_mask(t), sc, -jnp.inf)
            per_head[h][2][...] = sc

        def rowmax(h, rows):
            per_head[h][4][rows, :] = jnp.max(per_head[h][2][rows, :], axis=-1, keepdims=True)

        def stats(h):
            m_sc, acc_sc, _, _, mx_sc = per_head[h]
            m_prev = m_sc[...]
            m_new = jnp.maximum(m_prev, mx_sc[...] * scale2)
            m_sc[...] = m_new
            acc_sc[...] = jnp.exp2(m_prev - m_new) * acc_sc[...]

        def probs(h, rows):
            m_sc, _, s_sc, p_sc, _ = per_head[h]
            p_sc[rows, :] = jnp.exp2(s_sc[rows, :] * scale2 - m_sc[rows, :]).astype(BF16)

        def values(h):
            _, acc_sc, _, p_sc, _ = per_head[h]
            acc_sc[...] += _dot(p_sc[...], v_ref[h])

        def step(diag):
            blocks = [slice(r0, r0 + ATT_ROWS) for r0 in range(0, t, ATT_ROWS)]
            for h in range(ATT_HEADS_FWD):
                scores(h, diag)
            for rows in blocks:
                rowmax(0, rows)
            stats(0)
            for h in range(ATT_HEADS_FWD):
                for rows in blocks:
                    probs(h, rows)
                    if h + 1 < ATT_HEADS_FWD:
                        rowmax(h + 1, rows)
                if h + 1 < ATT_HEADS_FWD:
                    stats(h + 1)
                values(h)

        @pl.when(j < i)
        def _():
            step(False)

        @pl.when(j == i)
        def _():
            step(True)
            for h, (m_sc, acc_sc, _, _, _) in enumerate(per_head):
                l = acc_sc[:, DN:2 * DN]
                o_ref[:, DN * h:DN * (h + 1)] = acc_sc[:, 0:DN] / l
                lse_ref[h] = (m_sc[...] + jnp.log2(l[:, 0:1])) * LN2

    hb = ATT_HEADS_FWD
    grid_spec = pltpu.PrefetchScalarGridSpec(
        num_scalar_prefetch=2, grid=(H // hb, int(qi.shape[0])),
        in_specs=[pl.BlockSpec((hb, t, 2 * DN), lambda h, p, qi, ki: (h, qi[p], 0)),
                  pl.BlockSpec((hb, t, 2 * DN), lambda h, p, qi, ki: (h, ki[p], 0)),
                  pl.BlockSpec((hb, t, 2 * DN), lambda h, p, qi, ki: (h, ki[p], 0))],
        out_specs=[pl.BlockSpec((t, hb * DN), lambda h, p, qi, ki: (qi[p], h)),
                   pl.BlockSpec((hb, t, 1), lambda h, p, qi, ki: (h, qi[p], 0))],
        scratch_shapes=[pltpu.VMEM((t, 1), F32), pltpu.VMEM((t, 2 * DN), F32), pltpu.VMEM((t, t), F32),
                        pltpu.VMEM((t, t), BF16), pltpu.VMEM((t, 1), F32)] * hb)
    return pl.pallas_call(
        body, grid_spec=grid_spec,
        out_shape=[jax.ShapeDtypeStruct((s, H * DN), F32), jax.ShapeDtypeStruct((H, s, 1), F32)],
        name="attn_fwd", compiler_params=_params(("parallel", "arbitrary"), VMEM_BIG))(qi, ki, q, k, v)


def _attn_bwd(q, k, v, do, lse, delta, t):
    s = q.shape[1]
    n = s // t
    scale = ATT_SCALE
    qi, ki = _causal_pairs(n, by_key=True)

    def body(qi_ref, ki_ref, q_ref, k_ref, v_ref, do_ref, lse_ref, dl_ref, dq_ref, dk_ref, dv_ref,
             dk_sc, dv_sc, s_sc, dp_sc, p_sc, ds_sc):
        p = pl.program_id(1)
        i = qi_ref[p]
        j = ki_ref[p]

        @pl.when(p == 0)
        def _():
            dq_ref[...] = jnp.zeros_like(dq_ref)

        @pl.when(i == j)
        def _():
            dk_sc[...] = jnp.zeros_like(dk_sc)
            dv_sc[...] = jnp.zeros_like(dv_sc)

        def step(diag):
            for h in range(ATT_HEADS):
                s_sc[h] = _dot_nt(q_ref[h], k_ref[h])
                dp_sc[h] = _dot_nt(do_ref[:, DN * h:DN * (h + 1)], v_ref[h, :, 0:DN])
            for h in range(ATT_HEADS):
                for r0 in range(0, t, ATT_ROWS):
                    rows = slice(r0, r0 + ATT_ROWS)
                    width = _diag_width(r0, t) if diag else t
                    sc = s_sc[h, rows, 0:width] * (scale * LOG2E)
                    if diag:
                        sc = jnp.where(_diag_mask_rows(r0, width), sc, -jnp.inf)
                    pr = jnp.exp2(sc - lse_ref[h, rows, :] * LOG2E)
                    ds = pr * (dp_sc[h, rows, 0:width] - dl_ref[h, rows, :])
                    p_sc[h, rows, 0:width] = pr.astype(BF16)
                    ds_sc[h, rows, 0:width] = ds.astype(BF16)
                    if width < t:
                        p_sc[h, rows, width:t] = jnp.zeros((ATT_ROWS, t - width), BF16)
                        ds_sc[h, rows, width:t] = jnp.zeros((ATT_ROWS, t - width), BF16)
            q_rows = pl.ds(pl.multiple_of(i * t, t), t)
            for h in range(ATT_HEADS):
                dv_sc[h] += _dot_tn(p_sc[h], do_ref[:, DN * h:DN * (h + 1)])
                dk_sc[h] += _dot_tn(ds_sc[h], q_ref[h])
                dq_ref[h, q_rows, :] += _dot(ds_sc[h], k_ref[h])

        @pl.when(i > j)
        def _():
            step(False)

        @pl.when(i == j)
        def _():
            step(True)

        @pl.when(i == n - 1)
        def _():
            dk_ref[...] = dk_sc[...]
            dv_ref[...] = dv_sc[...]

    hb = ATT_HEADS
    grid_spec = pltpu.PrefetchScalarGridSpec(
        num_scalar_prefetch=2, grid=(H // hb, int(qi.shape[0])),
        in_specs=[pl.BlockSpec((hb, t, 2 * DN), lambda h, p, qi, ki: (h, qi[p], 0)),
                  pl.BlockSpec((hb, t, 2 * DN), lambda h, p, qi, ki: (h, ki[p], 0)),
                  pl.BlockSpec((hb, t, 2 * DN), lambda h, p, qi, ki: (h, ki[p], 0)),
                  pl.BlockSpec((t, hb * DN), lambda h, p, qi, ki: (qi[p], h)),
                  pl.BlockSpec((hb, t, 1), lambda h, p, qi, ki: (h, qi[p], 0)),
                  pl.BlockSpec((hb, t, 1), lambda h, p, qi, ki: (h, qi[p], 0))],
        out_specs=[pl.BlockSpec((hb, s, 2 * DN), lambda h, p, qi, ki: (h, 0, 0)),
                   pl.BlockSpec((hb, t, 2 * DN), lambda h, p, qi, ki: (h, ki[p], 0)),
                   pl.BlockSpec((hb, t, DN), lambda h, p, qi, ki: (h, ki[p], 0))],
        scratch_shapes=[pltpu.VMEM((hb, t, 2 * DN), F32), pltpu.VMEM((hb, t, DN), F32),
                        pltpu.VMEM((hb, t, t), F32), pltpu.VMEM((hb, t, t), F32),
                        pltpu.VMEM((hb, t, t), BF16), pltpu.VMEM((hb, t, t), BF16)])
    return pl.pallas_call(
        body, grid_spec=grid_spec,
        out_shape=[jax.ShapeDtypeStruct((H, s, 2 * DN), F32), jax.ShapeDtypeStruct((H, s, 2 * DN), F32),
                   jax.ShapeDtypeStruct((H, s, DN), F32)],
        name="attn_bwd", compiler_params=_params(("parallel", "arbitrary"), VMEM_BIG))(
            qi, ki, q, k, v, do, lse, delta)


def _middle(za, o, proj_g, x, tgt, gate, fnw, wco, wao, wo, ts):
    s = x.shape[0]
    inv_d = 1.0 / D

    def body(za_ref, o_ref, bg_ref, ga_ref, gb_ref, x_ref, t_ref, gate_ref, fnw_ref, wco_ref, wao_ref, wo_ref,
             dx2_ref, dza_ref, do_ref, dl_ref, dpg_ref, zb_ref, mg_ref, dmo_ref, dya_ref, dyb_ref, vec_ref):
        @pl.when(pl.program_id(0) == 0)
        def _():
            vec_ref[...] = jnp.zeros_like(vec_ref)

        ov = o_ref[...]
        bg = bg_ref[...]
        sb = _sigmoid(bg)
        silu_b = bg * sb
        zb = (ov * silu_b).astype(BF16)
        zb_ref[...] = zb
        ya = _dot(za_ref[...], wco_ref[...])
        yb = _dot(zb, wao_ref[...])
        sa = _sigmoid(ga_ref[...])
        sg = _sigmoid(gb_ref[...])
        mg = (sa * ya + sg * yb).astype(BF16)
        mg_ref[...] = mg
        mo = _dot(mg, wo_ref[...])
        gate_v = gate_ref[...]
        x2 = x_ref[...] + gate_v * mo
        r = lax.rsqrt(_rowmean(x2 * x2) + EPS)
        xh = x2 * r
        fw = fnw_ref[...]
        e = xh * fw - t_ref[...]
        vec_ref[2:3, :] += _colsum(e * e)
        dy = e * inv_d
        vec_ref[0:1, :] += _colsum(dy * xh)
        dxh = dy * fw
        dx2 = r * (dxh - xh * _rowmean(dxh * xh))
        dx2_ref[...] = dx2
        vec_ref[1:2, :] += _colsum(dx2 * mo)
        dmo = (gate_v * dx2).astype(BF16)
        dmo_ref[...] = dmo
        dmg = _dot_nt(dmo, wo_ref[...])
        dya = (sa * dmg).astype(BF16)
        dyb = (sg * dmg).astype(BF16)
        dya_ref[...] = dya
        dyb_ref[...] = dyb
        dpg_ref[:, D:2 * D] = (dmg * ya * (sa * (1.0 - sa))).astype(BF16)
        dpg_ref[:, 2 * D:3 * D] = (dmg * yb * (sg * (1.0 - sg))).astype(BF16)
        dza_ref[...] = _dot_nt(dya, wco_ref[...])
        dzb = _dot_nt(dyb, wao_ref[...])
        dov = dzb * silu_b
        do_ref[...] = dov.astype(BF16)
        dpg_ref[:, 0:D] = (dzb * ov * _dsilu(bg, sb)).astype(BF16)
        dprod = dov * ov
        for h in range(H):
            dl_ref[h] = jnp.sum(dprod[:, DN * h:DN * (h + 1)], axis=-1, keepdims=True)

    col = lambda c: pl.BlockSpec((ts, D), lambda i, c=c: (i, c))
    row = pl.BlockSpec((ts, D), lambda i: (i, 0))
    vec = pl.BlockSpec((1, D), lambda i: (0, 0))
    wsp = pl.BlockSpec((D, D), lambda i: (0, 0))
    bf = jax.ShapeDtypeStruct((s, D), BF16)
    ff = jax.ShapeDtypeStruct((s, D), F32)
    return pl.pallas_call(
        body, grid=(s // ts,),
        in_specs=[row, row, col(0), col(1), col(2), row, row, vec, vec, wsp, wsp, wsp],
        out_specs=[row, row, row, pl.BlockSpec((H, ts, 1), lambda i: (0, i, 0)),
                   pl.BlockSpec((ts, G_COLS), lambda i: (i, 0)), row, row, row, row, row,
                   pl.BlockSpec((8, D), lambda i: (0, 0))],
        out_shape=[ff, ff, bf, jax.ShapeDtypeStruct((H, s, 1), F32), jax.ShapeDtypeStruct((s, G_COLS), BF16),
                   bf, bf, bf, bf, bf, jax.ShapeDtypeStruct((8, D), F32)],
        name="middle", compiler_params=_params(("arbitrary",), VMEM_BIG))(
            za, o, proj_g, proj_g, proj_g, x, tgt, gate, fnw, wco, wao, wo)


def _input_bwd(dpa, dpl, dpg, wa, wl, wg, x, dx2, norm_w, scale, ts, parts):
    s = x.shape[0]

    def body(dpa_ref, dpl_ref, dpg_ref, wa_ref, wl_ref, wg_ref, x_ref, dx2_ref, nw_ref, sc_ref, gx_ref, gv_ref):
        @pl.when(pl.program_id(0) == 0)
        def _():
            gv_ref[...] = jnp.zeros_like(gv_ref)

        dh = (_dot_nt(dpa_ref[...], wa_ref[...]) + _dot_nt(dpl_ref[...], wl_ref[...])
              + _dot_nt(dpg_ref[...], wg_ref[...]))
        xv = x_ref[...]
        r = lax.rsqrt(_rowmean(xv * xv) + EPS)
        xh = xv * r
        nw = nw_ref[...]
        gv_ref[0:1, :] += _colsum(dh)
        gv_ref[1:2, :] += _colsum(dh * (xh * nw))
        dy = dh * (1.0 + sc_ref[...])
        gv_ref[2:3, :] += _colsum(dy * xh)
        dxh = dy * nw
        gx_ref[...] = dx2_ref[...] + r * (dxh - xh * _rowmean(dxh * xh))

    const = lambda shape: pl.BlockSpec(shape, lambda i: (0, 0))
    rowb = lambda w: pl.BlockSpec((ts, w), lambda i: (i, 0))
    side_in, side_out, side_shapes, side_sems = _scatter_operands(parts, False)
    outs = pl.pallas_call(
        _scatter_alongside(body, 10, 2, len(parts), s // ts - 1, _chip_scatter_copies), grid=(s // ts,),
        in_specs=[rowb(A_COLS), rowb(L_COLS), rowb(G_COLS), const((D, A_COLS)), const((D, L_COLS)),
                  const((D, G_COLS)), rowb(D), rowb(D), const((1, D)), const((1, D))] + side_in,
        out_specs=[rowb(D), const((8, D))] + side_out,
        out_shape=[jax.ShapeDtypeStruct((s, D), F32), jax.ShapeDtypeStruct((8, D), F32)] + side_shapes,
        scratch_shapes=side_sems,
        name="input_bwd", compiler_params=_params(("arbitrary",), VMEM_BIG))(
            dpa, dpl, dpg, wa, wl, wg, x, dx2, norm_w, scale, *parts)
    return outs[0], outs[1], list(outs[2:])


def _adamw_math(w, g, m, v):
    nm = ADAM_B1 * m + (1.0 - ADAM_B1) * g
    nv = ADAM_B2 * v + (1.0 - ADAM_B2) * (g * g)
    m_hat = nm / (1.0 - ADAM_B1 ** ADAM_STEP)
    v_hat = nv / (1.0 - ADAM_B2 ** ADAM_STEP)
    return -ADAM_LR * (m_hat / (jnp.sqrt(v_hat) + ADAM_EPS) + ADAM_WD * w), nm, nv


def _adamw(w, g, m, v, tr, name):
    lead, (rows, cols) = w.shape[:-2], w.shape[-2:]

    def body(w_ref, g_ref, m_ref, v_ref, d_ref, nm_ref, nv_ref):
        d_ref[...], nm_ref[...], nv_ref[...] = _adamw_math(w_ref[...], g_ref[...], m_ref[...], v_ref[...])

    blk = pl.BlockSpec((1,) * len(lead) + (tr, cols), lambda i: (0,) * len(lead) + (i, 0))
    shp = jax.ShapeDtypeStruct(w.shape, F32)
    return pl.pallas_call(
        body, grid=(rows // tr,), in_specs=[blk] * 4, out_specs=[blk] * 3, out_shape=[shp] * 3, name=name,
        compiler_params=_params(("parallel",), VMEM_BIG))(w, g.reshape(w.shape), m, v)


ROW_SHIFT, ROW_SCALE, ROW_NORM_W = 0, 1, 2
ROW_FINAL_NORM_W, ROW_GATE, ROW_LOSS = 8, 9, 10
ROW_LN_W, ROW_LN_B, ROW_CONV_B = 16, 17, 18
ROW_Q_NORM_W, ROW_KV_NORM_W = 24, 25
ROW_CONV_W = 32
SUM_ROWS = 64
VECTOR_ROWS = ((ROW_SHIFT, ROW_SCALE, ROW_GATE), (ROW_NORM_W,), (ROW_CONV_B,), (ROW_LN_W,), (ROW_LN_B,),
               (ROW_Q_NORM_W,), (ROW_KV_NORM_W,), (ROW_FINAL_NORM_W,))


def _small_finalize(gathered, vectors, conv, chip):
    n = len(vectors)
    cw = conv[0].shape[2]

    def body(chip_ref, g_ref, *refs):
        ins, outs = refs[:3 * n + 3], refs[3 * n + 3:]
        tot = g_ref[0]
        for k in range(1, N_DEV):
            tot = tot + g_ref[k]
        for p, rows in enumerate(VECTOR_ROWS):
            w_ref, m_ref, v_ref = ins[3 * p:3 * p + 3]
            g_out, d_out, nm_out, nv_out = outs[4 * p:4 * p + 4]
            width = w_ref.shape[1] // len(rows)
            for q, r in enumerate(rows):
                lanes = slice(q * width, (q + 1) * width)
                g = tot[r:r + 1, 0:width]
                g_out[:, lanes] = g
                d_out[:, lanes], nm_out[:, lanes], nv_out[:, lanes] = _adamw_math(
                    w_ref[:, lanes], g, m_ref[:, lanes], v_ref[:, lanes])
        cols = pl.ds(pl.multiple_of(chip_ref[0] * cw, LANE), cw)
        gc = g_ref[0, pl.ds(ROW_CONV_W, KC), cols]
        for k in range(1, N_DEV):
            gc = gc + g_ref[k, pl.ds(ROW_CONV_W, KC), cols]
        cw_ref, cm_ref, cv_ref = ins[3 * n:3 * n + 3]
        g_out, d_out, nm_out, nv_out, dmod_ref, loss_ref = outs[4 * n:]
        g_out[0] = gc
        d_out[0], nm_out[0], nv_out[0] = _adamw_math(cw_ref[0], gc, cm_ref[0], cv_ref[0])
        for k in range(N_DEV):
            for q, r in enumerate((ROW_SHIFT, ROW_SCALE, ROW_GATE)):
                dmod_ref[k:k + 1, q * D:(q + 1) * D] = g_ref[k, r:r + 1, :]
        loss_ref[...] = (0.5 / D) * jnp.sum(tot[ROW_LOSS:ROW_LOSS + 1, :], axis=-1, keepdims=True)

    flat_in = [a for triple in vectors for a in triple] + list(conv)
    shapes = [jax.ShapeDtypeStruct(w.shape, F32) for w, _, _ in vectors for _ in range(4)]
    shapes += [jax.ShapeDtypeStruct(conv[0].shape, F32)] * 4
    shapes += [jax.ShapeDtypeStruct((N_DEV, 3 * D), F32), jax.ShapeDtypeStruct((1, 1), F32)]
    whole = pl.BlockSpec(memory_space=pltpu.VMEM)
    return pl.pallas_call(
        body, out_shape=shapes,
        in_specs=[pl.BlockSpec(memory_space=pltpu.SMEM)] + [whole] * (1 + len(flat_in)),
        out_specs=[whole] * len(shapes), name="small_finalize")(chip, gathered, *flat_in)


def _ada_fwd(c_all, w_ada_shard, b_ada_shard):
    def body(c_ref, w_ref, b_ref, o_ref):
        cv = c_ref[...]
        o_ref[...] = jnp.dot(cv * _sigmoid(cv), w_ref[...], preferred_element_type=F32,
                             precision=lax.Precision.HIGHEST) + b_ref[...]

    return pl.pallas_call(
        body, out_shape=jax.ShapeDtypeStruct((N_DEV, w_ada_shard.shape[1]), F32), name="ada_fwd")(
            c_all, w_ada_shard, b_ada_shard)


def _ada_bwd(c_all_t, dmod_shard):
    def body(c_ref, d_ref, o_ref):
        cv = c_ref[...]
        o_ref[...] = jnp.dot(cv * _sigmoid(cv), d_ref[...], preferred_element_type=F32,
                             precision=lax.Precision.HIGHEST)

    return pl.pallas_call(
        body, out_shape=jax.ShapeDtypeStruct((D, dmod_shard.shape[1]), F32), name="ada_bwd")(c_all_t, dmod_shard)


def _sum_chip_slabs(arrived, part, place, tr, name, axis):
    n, rows, cols = arrived.shape
    per = rows // tr
    own_map = ((lambda i, pc: (pc[0], i, 0)) if part.shape[1] == rows
               else (lambda i, pc: (pc[0], pc[1] * per + i, 0)))

    def body(place_ref, a_ref, p_ref, o_ref):
        acc = p_ref[0].astype(F32)
        for k in range(n):
            acc = acc + a_ref[k].astype(F32)
        o_ref[...] = acc

    if axis == 1:
        whole, out_map = (2 * rows, cols), lambda i, pc: (pc[1] * per + i, 0)
    else:
        whole, out_map = (rows, 2 * cols), lambda i, pc: (i, pc[1])
    grid_spec = pltpu.PrefetchScalarGridSpec(
        num_scalar_prefetch=1, grid=(per,),
        in_specs=[pl.BlockSpec((n, tr, cols), lambda i, pc: (0, i, 0)),
                  pl.BlockSpec((1, tr, cols), own_map)],
        out_specs=pl.BlockSpec((tr, cols), out_map))
    return pl.pallas_call(
        body, grid_spec=grid_spec, out_shape=jax.ShapeDtypeStruct(whole, F32), name=name,
        compiler_params=_params(("parallel",)))(place, arrived, part)


def _add_own_half(full, other, core, tr, name, axis):
    n, rows, cols = other.shape
    per = rows // tr

    def body(c_ref, f_ref, o_ref, out_ref):
        out_ref[...] = (f_ref[...] + o_ref[...]).astype(BF16)

    full_map = (lambda k, i, c: (k, c[0] * per + i, 0)) if axis == 1 else (lambda k, i, c: (k, i, c[0]))
    grid_spec = pltpu.PrefetchScalarGridSpec(
        num_scalar_prefetch=1, grid=(n, per),
        in_specs=[pl.BlockSpec((1, tr, cols), full_map),
                  pl.BlockSpec((1, tr, cols), lambda k, i, c: (k, i, 0))],
        out_specs=pl.BlockSpec((1, tr, cols), lambda k, i, c: (k, i, 0)))
    return pl.pallas_call(
        body, grid_spec=grid_spec, out_shape=jax.ShapeDtypeStruct((n, rows, cols), BF16), name=name,
        compiler_params=_params(("parallel", "parallel")))(core, full, other)


def _allgather8(block, src_rows, vmem, name):
    n = block.shape[1]
    m = src_rows
    sliced = block.shape[0] != m

    def body(x_ref, out_ref, send_sems, recv_sems, local_sem):
        x, y, c = _coords()
        me, sibling = (x, y, c), (x, y, 1 - c)
        chips = [(1 - x, y), (x, 1 - y), (1 - x, 1 - y)]
        src = x_ref.at[pl.ds(pl.multiple_of(c * m, 16), m), :] if sliced else x_ref

        def rows(px, py, pc):
            return out_ref.at[pl.ds(pl.multiple_of((4 * px + 2 * py + pc) * m, 8), m), :]

        def copy(k, blk, to, source=None):
            return pltpu.make_async_remote_copy(
                src_ref=rows(*blk) if source is None else source, dst_ref=rows(*blk),
                send_sem=send_sems.at[k], recv_sem=recv_sems.at[k], device_id=to, device_id_type=MESH)

        mine = pltpu.make_async_copy(src, rows(*me), local_sem)
        mine.start()
        first = [copy(0, me, sibling, source=src)]
        first += [copy(1 + j, me, (*chip, c), source=src) for j, chip in enumerate(chips)]
        for cp in first:
            cp.start()
        passed = [copy(4 + j, (*chip, c), sibling) for j, chip in enumerate(chips)]
        for j, chip in enumerate(chips):
            copy(1 + j, (*chip, c), me).wait_recv()
            passed[j].start()
        copy(0, sibling, me).wait_recv()
        for j, chip in enumerate(chips):
            copy(4 + j, (*chip, 1 - c), me).wait_recv()
        for cp in first + passed:
            cp.wait_send()
        mine.wait()

    space = pltpu.VMEM if vmem else pl.ANY
    return pl.pallas_call(
        body, out_shape=jax.ShapeDtypeStruct((N_DEV * m, n), block.dtype),
        in_specs=[pl.BlockSpec(memory_space=space)], out_specs=pl.BlockSpec(memory_space=space),
        scratch_shapes=[pltpu.SemaphoreType.DMA((7,)), pltpu.SemaphoreType.DMA((7,)), pltpu.SemaphoreType.DMA],
        name=name)(block)


def _gather_plan(x_refs, out_refs, send_sems, recv_sems, local_sems):
    n = len(x_refs)
    halves = [r.shape[0] // 2 for r in x_refs]
    x, y, c = _coords()
    me, sibling = (x, y, c), (x, y, 1 - c)
    chips = [(1 - x, y), (x, 1 - y), (1 - x, 1 - y)]

    def src(a):
        return x_refs[a].at[pl.ds(pl.multiple_of(c * halves[a], 16), halves[a]), :]

    def blk(a, px, py, pc):
        return out_refs[a].at[4 * px + 2 * py + pc]

    def copy(a, k, who, to, source=None):
        return pltpu.make_async_remote_copy(
            src_ref=blk(a, *who) if source is None else source, dst_ref=blk(a, *who),
            send_sem=send_sems.at[7 * a + k], recv_sem=recv_sems.at[7 * a + k], device_id=to, device_id_type=MESH)

    def mine(a):
        return pltpu.make_async_copy(src(a), blk(a, *me), local_sems.at[a])

    def first(a):
        return ([copy(a, 0, me, sibling, source=src(a))]
                + [copy(a, 1 + j, me, (*chip, c), source=src(a)) for j, chip in enumerate(chips)])

    def begin():
        for a in range(n):
            mine(a).start()
        for a in range(n):
            for cp in first(a):
                cp.start()

    def finish():
        onward = []
        for j, chip in enumerate(chips):
            for a in range(n):
                copy(a, 1 + j, (*chip, c), me).wait_recv()
                onward.append(copy(a, 4 + j, (*chip, c), sibling))
                onward[-1].start()
        for a in range(n):
            copy(a, 0, sibling, me).wait_recv()
        for j, chip in enumerate(chips):
            for a in range(n):
                copy(a, 4 + j, (*chip, 1 - c), me).wait_recv()
        for a in range(n):
            for cp in first(a):
                cp.wait_send()
        for cp in onward:
            cp.wait_send()
        for a in range(n):
            mine(a).wait()

    return begin, finish


def _gather_operands(shards):
    n = len(shards)
    shapes = [jax.ShapeDtypeStruct((N_DEV, a.shape[0] // 2, a.shape[1]), a.dtype) for a in shards]
    sems = [pltpu.SemaphoreType.DMA((7 * n,)), pltpu.SemaphoreType.DMA((7 * n,)), pltpu.SemaphoreType.DMA((n,))]
    return shapes, sems


def _as_chip_slabs(gathered, shards):
    return [o.reshape(N_CHIP, a.shape[0], a.shape[1]) for o, a in zip(gathered, shards)]


def _gather_alongside(body, n_in, n_out, n_shards, last_step):
    def wrapped(*refs):
        ins, shards = refs[:n_in], refs[n_in:n_in + n_shards]
        rest = refs[n_in + n_shards:]
        outs, gathered = rest[:n_out], rest[n_out:n_out + n_shards]
        scratch, sems = rest[n_out + n_shards:-3], rest[-3:]

        @pl.when(pl.program_id(0) == 0)
        def _():
            _gather_plan(shards, gathered, *sems)[0]()

        body(*ins, *outs, *scratch)

        @pl.when(pl.program_id(0) == last_step)
        def _():
            _gather_plan(shards, gathered, *sems)[1]()

    return wrapped


def _half(ref, axis, which, ndim):
    size = ref.shape[axis] // 2
    idx = [slice(None)] * ndim
    idx[axis] = pl.ds(pl.multiple_of(which * size, 8 if axis == ndim - 2 else LANE), size)
    return ref.at[tuple(idx)]


def _swap_halves_with_sibling(fulls, name, axes):
    n = len(fulls)

    def body(*refs):
        f_refs, got_refs = refs[:n], refs[n:2 * n]
        send_sems, recv_sems = refs[2 * n:]
        x, y, c = _coords()
        copies = []
        for a in range(n):
            copies.append(pltpu.make_async_remote_copy(
                src_ref=_half(f_refs[a], axes[a], 1 - c, 3), dst_ref=got_refs[a], send_sem=send_sems.at[a],
                recv_sem=recv_sems.at[a], device_id=(x, y, 1 - c), device_id_type=MESH))
        for cp in copies:
            cp.start()
        for cp in copies:
            cp.wait()

    def halved(a, axis):
        shape = list(a.shape)
        shape[axis] //= 2
        return jax.ShapeDtypeStruct(tuple(shape), a.dtype)

    return pl.pallas_call(
        body, out_shape=[halved(a, ax) for a, ax in zip(fulls, axes)],
        in_specs=[HBM_REF] * n, out_specs=[HBM_REF] * n,
        scratch_shapes=[pltpu.SemaphoreType.DMA((n,)), pltpu.SemaphoreType.DMA((n,))],
        name=name)(*fulls)


def _join_halves_with_sibling(wholes, axes):
    n = len(wholes)

    def body(*refs):
        out_refs = refs[n:2 * n]
        send_sems, recv_sems = refs[2 * n:]
        x, y, c = _coords()

        def push(a, core):
            half = _half(out_refs[a], axes[a] - 1, core, 2)
            return pltpu.make_async_remote_copy(
                src_ref=half, dst_ref=half, send_sem=send_sems.at[a], recv_sem=recv_sems.at[a],
                device_id=(x, y, 1 - c), device_id_type=MESH)

        for a in range(n):
            push(a, c).start()
        for a in range(n):
            push(a, 1 - c).wait_recv()
        for a in range(n):
            push(a, c).wait_send()

    return pl.pallas_call(
        body, out_shape=[jax.ShapeDtypeStruct(a.shape, a.dtype) for a in wholes],
        in_specs=[HBM_REF] * n, out_specs=[HBM_REF] * n, input_output_aliases={a: a for a in range(n)},
        scratch_shapes=[pltpu.SemaphoreType.DMA((n,)), pltpu.SemaphoreType.DMA((n,))],
        name="rs_pair_join")(*wholes)


def _cols_to_slabs(g):
    rows, cols = g.shape
    return g.reshape(rows, N_CHIP, cols // N_CHIP).transpose(1, 0, 2)


def _slabs_to_cols(w):
    n, rows, cols = w.shape
    return w.transpose(1, 0, 2).reshape(rows, n * cols)


def _col_window(slabs, start, stop):
    n = slabs.shape[2]
    pieces = []
    for k in range(N_CHIP):
        lo, hi = max(start, k * n), min(stop, (k + 1) * n)
        if lo < hi:
            pieces.append(slabs[k][:, lo - k * n:hi - k * n])
    return pieces[0] if len(pieces) == 1 else jnp.concatenate(pieces, axis=1)


def _slabs_from_groups(groups, n):
    slabs = []
    for k in range(N_CHIP):
        pieces, off = [], 0
        for g in groups:
            lo, hi = max(k * n, off), min((k + 1) * n, off + g.shape[0])
            if lo < hi:
                pieces.append(g[lo - off:hi - off])
            off += g.shape[0]
        slabs.append(pieces[0] if len(pieces) == 1 else jnp.concatenate(pieces, axis=0))
    return jnp.stack(slabs)


def _uq_to_padded(w_uq):
    per = w_uq.reshape(RQ, H, DN + DR)
    nope = per[:, :, :DN].reshape(RQ, H * DN)
    rope = jnp.pad(per[:, :, DN:], ((0, 0), (0, 0), (0, LANE - DR))).reshape(RQ, H * LANE)
    return jnp.concatenate([nope, rope], axis=1)


def _uq_from_padded(g):
    nope = g[:, :H * DN].reshape(RQ, H, DN)
    rope = g[:, H * DN:].reshape(RQ, H, LANE)[:, :, :DR]
    return jnp.concatenate([nope, rope], axis=2).reshape(RQ, H * (DN + DR))


def _rope_tables(positions):
    inv_freq = ROPE_THETA ** (-jnp.arange(0, DR, 2, dtype=F32) / DR)
    ang = positions.astype(F32)[:, None] * inv_freq
    cos, sin = jnp.cos(ang), jnp.sin(ang)
    return jnp.tile(cos, (1, 4)), jnp.tile(jnp.concatenate([-sin, sin], axis=1), (1, 2))


def _pair_sums(fulls, core, tag, axes, tr):
    from_sibling = _swap_halves_with_sibling(fulls, f"rs_pair_swap_{tag}", axes)
    return [_add_own_half(f, o, core, min(tr, o.shape[1]), f"add_own_half_{tag}{n}", ax)
            for n, (f, o, ax) in enumerate(zip(fulls, from_sibling, axes))]


def _local_step(x, tgt, cos_t, sin_t, mod, weights, small, tiles, place):
    ts, ts_in, tm_nn, tm_tn, t_attn, chunk = tiles
    w_in_shard, later_shards, conv_w = weights
    norm_w, conv_b, ln_w, ln_b, q_norm_w, kv_norm_w, fnw = small
    shift, scale, gate = mod[:, 0:D], mod[:, D:2 * D], mod[:, 2 * D:3 * D]

    h, (g_in,) = _adaln_norm(x, norm_w, shift, scale, ts, [w_in_shard])
    wa = _col_window(g_in, 0, A_COLS)
    wl = jnp.pad(_col_window(g_in, A_COLS, A_COLS + L_COLS_RAW), ((0, 0), (0, L_COLS - L_COLS_RAW)))
    wg = _col_window(g_in, A_COLS + L_COLS_RAW, IN_COLS)
    proj_a = _mm_nn(h, wa, tm_nn, D, "proj_a")
    u0, u1, za, (g_uq, g_ukv, g_co, g_ao, g_o) = _conv_fwd(proj_a, conv_w, conv_b, ln_w, ln_b, ts, chunk, later_shards)
    w_uq2, w_ukv = _uq_to_padded(_slabs_to_cols(g_uq)), _slabs_to_cols(g_ukv)
    wco, wao, wo = g_co.reshape(D, D), g_ao.reshape(D, D), g_o.reshape(D, D)
    proj_l = _mm_nn(h, wl, tm_nn, L_COLS, "proj_l")
    proj_g = _mm_nn(h, wg, tm_nn, D, "proj_g")
    qn, kvn, q, k, v = _mla_prep(proj_l, q_norm_w, kv_norm_w, w_uq2, w_ukv, cos_t, sin_t, ts)
    o, lse = _attn_fwd(q, k, v, t_attn)
    (dx2, dza, do, delta, dpg, zb, mg, dmo, dya, dyb, vec_mid) = _middle(
        za, o, proj_g, x, tgt, gate, fnw, wco, wao, wo, ts)
    g_wo = _mm_tn(mg, dmo, tm_tn, D, D, "grad_w_out", BF16)
    g_wco = _mm_tn(za, dya, tm_tn, D, D, "grad_w_conv_out", BF16)
    g_wao = _mm_tn(zb, dyb, tm_tn, D, D, "grad_w_attn_out", BF16)
    dq, dk, dv = _attn_bwd(q, k, v, do, lse, delta, t_attn)
    dpl, g_wuq2, g_wukv, vec_mla = _mla_prep_bwd(
        dq, dk, dv, proj_l, qn, kvn, q_norm_w, kv_norm_w, w_uq2, w_ukv, cos_t, sin_t, ts)

    core = place[1:2]
    nr = D // N_CHIP
    early = [_cols_to_slabs(_uq_from_padded(g_wuq2)).astype(BF16), _cols_to_slabs(g_wukv).astype(BF16),
             g_wco.reshape(N_CHIP, nr, D), g_wao.reshape(N_CHIP, nr, D), g_wo.reshape(N_CHIP, nr, D)]
    dpa, g_conv_w, vec_conv, early_got = _conv_bwd(dza, proj_a, u0, u1, conv_w, ln_w, ln_b, ts, chunk, early)

    g_wa_t = _mm_tn(dpa, h, tm_tn, D, D, "grad_w_in_a")
    g_wl_t = _mm_tn(dpl, h, tm_tn, L_COLS, D, "grad_w_in_l")
    g_wg_t = _mm_tn(dpg, h, tm_tn, D, D, "grad_w_in_g")
    g_w_in_slabs = _slabs_from_groups([g_wa_t, g_wl_t[0:L_COLS_RAW], g_wg_t], IN_COLS // N_CHIP)
    late_sums = _pair_sums([g_w_in_slabs], core, "b", [2], W_IN_ROWS)
    grad_x, vec_in, late_got = _input_bwd(dpa, dpl, dpg, wa, wl, wg, x, dx2, norm_w, scale, ts_in, late_sums)

    axes = [2] + [1] * len(early)
    wholes = [_sum_chip_slabs(a, p, place, min(W_IN_ROWS if ax == 2 else 128, a.shape[1]), f"sum_chip_slabs_{n}", ax)
              for n, (a, p, ax) in enumerate(zip(late_got + early_got, late_sums + early, axes))]
    shards = _join_halves_with_sibling(wholes, axes)

    col_sums = jnp.concatenate(
        [vec_in, vec_mid, vec_conv, jnp.pad(vec_mla, ((0, 0), (0, D - RQ))), g_conv_w], axis=0)
    return grad_x, shards, col_sums


def kernel(x, c, positions, w_ada, b_ada, norm_w, w_in, conv_w, conv_b, conv_ln_w, conv_ln_b, w_conv_out, q_norm_w, w_uq, kv_norm_w, w_ukv, w_attn_out, w_out, final_norm_w, loss_target, m_w_ada, m_b_ada, m_norm_w, m_w_in, m_conv_w, m_conv_b, m_conv_ln_w, m_conv_ln_b, m_w_conv_out, m_q_norm_w, m_w_uq, m_kv_norm_w, m_w_ukv, m_w_attn_out, m_w_out, m_final_norm_w, v_w_ada, v_b_ada, v_norm_w, v_w_in, v_conv_w, v_conv_b, v_conv_ln_w, v_conv_ln_b, v_w_conv_out, v_q_norm_w, v_w_uq, v_kv_norm_w, v_w_ukv, v_w_attn_out, v_w_out, v_final_norm_w):
    ix, iy, ic = _coords()
    chip = 2 * ix + iy
    dev = 4 * ix + 2 * iy + ic
    s = x.shape[1]
    tiles = (256, 512, 1024, 2048, 512, 32)

    conv_w_pad = jnp.pad(conv_w[0], ((0, HALO - KC), (0, 0)))
    small_in = jnp.concatenate([c.reshape(8, LANE), conv_w_pad.reshape(64, LANE)], axis=0)
    small_all = _allgather8(small_in, 72, True, "gather_c_conv").reshape(N_DEV, 72, LANE)
    c_all = small_all[:, 0:8].reshape(N_DEV, D)
    conv_full = jnp.concatenate(
        [small_all[2 * k, 8:72].reshape(HALO, D // N_CHIP) for k in range(N_CHIP)], axis=1)

    later_shards = [w[0].astype(BF16) for w in (w_uq, w_ukv, w_conv_out, w_attn_out, w_out)]
    weights = (w_in[0].astype(BF16), later_shards, conv_full)

    ada_cols = w_ada.shape[2]
    b_shard = lax.dynamic_slice(b_ada, (0, chip * ada_cols), (1, ada_cols))
    mod_part = _ada_fwd(c_all, w_ada[0], b_shard)
    mod_all = _allgather8(mod_part, N_DEV, True, "gather_mod").reshape(N_DEV, N_DEV, ada_cols)
    mod = jnp.concatenate(
        [lax.dynamic_slice(mod_all[2 * k], (dev, 0), (1, ada_cols)) for k in range(N_CHIP)], axis=1)

    cos_t, sin_t = _rope_tables(positions[0])
    small = (norm_w, conv_b, conv_ln_w, conv_ln_b, q_norm_w, kv_norm_w, final_norm_w.reshape(1, D))
    place = jnp.stack([chip, ic]).astype(jnp.int32)
    grad_x, shards, col_sums = _local_step(x[0], loss_target[0], cos_t, sin_t, mod, weights, small, tiles, place)
    g_w_in_s, g_w_uq_s, g_w_ukv_s, g_wco_s, g_wao_s, g_wo_s = shards

    gathered = _allgather8(col_sums, SUM_ROWS, True, "gather_small_grads").reshape(N_DEV, SUM_ROWS, D)
    vec_names = ("b_ada", "norm_w", "conv_b", "conv_ln_w", "conv_ln_b", "q_norm_w", "kv_norm_w", "final_norm_w")
    row = lambda a: a.reshape(1, -1)
    vectors = [(row(b_ada), row(m_b_ada), row(v_b_ada)), (norm_w, m_norm_w, v_norm_w), (conv_b, m_conv_b, v_conv_b),
               (conv_ln_w, m_conv_ln_w, v_conv_ln_w), (conv_ln_b, m_conv_ln_b, v_conv_ln_b),
               (q_norm_w, m_q_norm_w, v_q_norm_w), (kv_norm_w, m_kv_norm_w, v_kv_norm_w),
               (row(final_norm_w), row(m_final_norm_w), row(v_final_norm_w))]
    fin = _small_finalize(gathered, vectors, (conv_w, m_conv_w, v_conv_w), place[0:1])
    res = {}
    for p, (name, (w, _, _)) in enumerate(zip(vec_names, vectors)):
        shape = final_norm_w.shape if name == "final_norm_w" else w.shape
        res[name] = tuple(a.reshape(shape) for a in fin[4 * p:4 * p + 4])
    res["conv_w"] = tuple(fin[4 * len(vectors):4 * len(vectors) + 4])
    dmod_all, loss = fin[-2], fin[-1].reshape(())
    dmod_shard = lax.dynamic_slice(dmod_all, (0, chip * ada_cols), (N_DEV, ada_cols))
    g_w_ada = _ada_bwd(c_all.T, dmod_shard).reshape(1, D, ada_cols)

    def big(w, g, m, v, tr, name):
        d, nm, nv = _adamw(w, g, m, v, tr, name)
        return g.reshape(w.shape), d, nm, nv

    res["w_ada"] = big(w_ada, g_w_ada[0], m_w_ada, v_w_ada, 256, "adamw_w_ada")
    t_in = [a[0].T for a in (w_in, m_w_in, v_w_in)]
    d_t, nm_t, nv_t = _adamw(t_in[0], g_w_in_s, t_in[1], t_in[2], W_IN_ROWS, "adamw_w_in")
    res["w_in"] = tuple(a.T[None] for a in (g_w_in_s, d_t, nm_t, nv_t))
    res["w_conv_out"] = big(w_conv_out, g_wco_s, m_w_conv_out, v_w_conv_out, 256, "adamw_w_conv_out")
    res["w_uq"] = big(w_uq, g_w_uq_s, m_w_uq, v_w_uq, 256, "adamw_w_uq")
    res["w_ukv"] = big(w_ukv, g_w_ukv_s, m_w_ukv, v_w_ukv, 256, "adamw_w_ukv")
    res["w_attn_out"] = big(w_attn_out, g_wao_s, m_w_attn_out, v_w_attn_out, 256, "adamw_w_attn_out")
    res["w_out"] = big(w_out, g_wo_s, m_w_out, v_w_out, 256, "adamw_w_out")

    order = ("w_ada", "b_ada", "norm_w", "w_in", "conv_w", "conv_b", "conv_ln_w", "conv_ln_b", "w_conv_out",
             "q_norm_w", "w_uq", "kv_norm_w", "w_ukv", "w_attn_out", "w_out", "final_norm_w")
    outs = [loss, grad_x[None]]
    for slot in range(4):
        outs += [res[name][slot] for name in order]
    return tuple(outs)
```

```python
import functools

import numpy as np
import jax
import jax.numpy as jnp
from jax import lax
from jax.experimental import pallas as pl
from jax.experimental.pallas import tpu as pltpu

F32 = jnp.float32
BF16 = jnp.bfloat16
MESH = pl.DeviceIdType.MESH

D = 1024
H = 8
DN = 128
DR = 64
RQ = 256
KC = 31
HALO = 32
EPS = 1e-6
ROPE_THETA = 10000.0
N_CHIP = 4
N_DEV = 8
LANE = 128
VMEM_BIG = 56 * 1024 * 1024

ADAM_LR = 0.001
ADAM_B1 = 0.9
ADAM_B2 = 0.999
ADAM_EPS = 1e-08
ADAM_WD = 0.01
ADAM_STEP = 10

A_COLS = 3 * D
L_COLS_RAW = RQ + RQ + DR
L_COLS = 640
G_COLS = 3 * D
IN_COLS = A_COLS + L_COLS_RAW + G_COLS


def _params(sem=None, vmem=None):
    kw = {}
    if sem is not None:
        kw["dimension_semantics"] = sem
    if vmem is not None:
        kw["vmem_limit_bytes"] = vmem
    return pltpu.CompilerParams(**kw)


def _dot(a, b):
    return jnp.dot(a, b, preferred_element_type=F32)


def _dot_nt(a, b):
    return lax.dot_general(a, b, (((1,), (1,)), ((), ())), preferred_element_type=F32)


def _dot_tn(a, b):
    return lax.dot_general(a, b, (((0,), (0,)), ((), ())), preferred_element_type=F32)


def _colsum(v):
    return jnp.sum(v, axis=0, keepdims=True)


def _rowmean(v):
    return jnp.mean(v, axis=-1, keepdims=True)


def _sigmoid(v):
    return jax.nn.sigmoid(v)


def _dsilu(v, s):
    return s * (1.0 + v * (1.0 - s))


def _swap_halves(v, first_half):
    return jnp.where(first_half, pltpu.roll(v, 96, 1), pltpu.roll(v, 32, 1))


def _first_half_mask(rows):
    lane = lax.broadcasted_iota(jnp.int32, (rows, LANE), 1)
    return (lane % 64) < 32


def _adaln_norm(x, norm_w, shift, scale, ts, shards):
    s = x.shape[0]

    def body(x_ref, nw_ref, sh_ref, sc_ref, h_ref):
        xv = x_ref[...]
        r = lax.rsqrt(_rowmean(xv * xv) + EPS)
        y = xv * r * nw_ref[...]
        h_ref[...] = (y * (1.0 + sc_ref[...]) + sh_ref[...]).astype(BF16)

    row = pl.BlockSpec((ts, D), lambda i: (i, 0))
    vec = pl.BlockSpec((1, D), lambda i: (0, 0))
    n = len(shards)
    gathered_shapes, sems = _gather_operands(shards)
    outs = pl.pallas_call(
        _gather_alongside(body, 4, 1, n, s // ts - 1), grid=(s // ts,),
        in_specs=[row, vec, vec, vec] + [HBM_REF] * n, out_specs=[row] + [HBM_REF] * n,
        out_shape=[jax.ShapeDtypeStruct((s, D), BF16)] + gathered_shapes, scratch_shapes=sems, name="adaln_norm",
        compiler_params=_params(("arbitrary",)))(x, norm_w, shift, scale, *shards)
    return outs[0], _as_chip_slabs(outs[1:], shards)


def _mm_nn(a, b, tm, tn, name):
    m, k = a.shape
    n = b.shape[1]

    def body(a_ref, b_ref, o_ref):
        o_ref[...] = _dot(a_ref[...], b_ref[...])

    return pl.pallas_call(
        body, grid=(n // tn, m // tm),
        in_specs=[pl.BlockSpec((tm, k), lambda j, i: (i, 0)), pl.BlockSpec((k, tn), lambda j, i: (0, j))],
        out_specs=pl.BlockSpec((tm, tn), lambda j, i: (i, j)),
        out_shape=jax.ShapeDtypeStruct((m, n), F32), name=name,
        compiler_params=_params(("parallel", "parallel"), VMEM_BIG))(a, b)


def _mm_tn(a, b, tm, tk, tn, name, out_dtype=F32):
    m, k = a.shape
    n = b.shape[1]
    steps = m // tm

    def body(a_ref, b_ref, o_ref, acc_ref):
        @pl.when(pl.program_id(2) == 0)
        def _():
            acc_ref[...] = jnp.zeros_like(acc_ref)
        acc_ref[...] += _dot_tn(a_ref[...], b_ref[...])

        @pl.when(pl.program_id(2) == steps - 1)
        def _():
            o_ref[...] = acc_ref[...].astype(out_dtype)

    return pl.pallas_call(
        body, grid=(k // tk, n // tn, steps),
        in_specs=[pl.BlockSpec((tm, tk), lambda r, j, i: (i, r)), pl.BlockSpec((tm, tn), lambda r, j, i: (i, j))],
        out_specs=pl.BlockSpec((tk, tn), lambda r, j, i: (r, j)),
        out_shape=jax.ShapeDtypeStruct((k, n), out_dtype), scratch_shapes=[pltpu.VMEM((tk, tn), F32)], name=name,
        compiler_params=_params(("parallel", "parallel", "arbitrary"), VMEM_BIG))(a, b)


def _coords():
    return lax.axis_index("x"), lax.axis_index("y"), lax.axis_index("c")


HBM_REF = pl.BlockSpec(memory_space=pl.ANY)


def _chip_scatter_copies(p_refs, got_refs, send_sems, recv_sems):
    x, y, c = _coords()
    copies = []
    for a in range(len(p_refs)):
        for j, (px, py) in enumerate([(1 - x, y), (x, 1 - y), (1 - x, 1 - y)]):
            copies.append(pltpu.make_async_remote_copy(
                src_ref=p_refs[a].at[2 * px + py], dst_ref=got_refs[a].at[j], send_sem=send_sems.at[3 * a + j],
                recv_sem=recv_sems.at[3 * a + j], device_id=(px, py, c), device_id_type=MESH))
    return copies


RELATIONS = [(dx, dy, dc) for dx in (0, 1) for dy in (0, 1) for dc in (0, 1)][1:]


def _device_scatter_copies(p_refs, got_refs, send_sems, recv_sems):
    x, y, c = _coords()
    copies = []
    for a in range(len(p_refs)):
        half = p_refs[a].shape[1] // 2
        for j, (dx, dy, dc) in enumerate(RELATIONS):
            px, py, pc = (1 - x if dx else x), (1 - y if dy else y), (1 - c if dc else c)
            src = p_refs[a].at[2 * px + py, pl.ds(pl.multiple_of(pc * half, 16), half), :]
            copies.append(pltpu.make_async_remote_copy(
                src_ref=src, dst_ref=got_refs[a].at[j], send_sem=send_sems.at[7 * a + j],
                recv_sem=recv_sems.at[7 * a + j], device_id=(px, py, pc), device_id_type=MESH))
    return copies


def _scatter_alongside(body, n_in, n_out, n_parts, last_step, make_copies):
    def wrapped(*refs):
        ins, parts = refs[:n_in], refs[n_in:n_in + n_parts]
        rest = refs[n_in + n_parts:]
        outs, got = rest[:n_out], rest[n_out:n_out + n_parts]
        scratch, (send_sems, recv_sems) = rest[n_out + n_parts:-2], rest[-2:]

        @pl.when(pl.program_id(0) == 0)
        def _():
            for cp in make_copies(parts, got, send_sems, recv_sems):
                cp.start()

        body(*ins, *outs, *scratch)

        @pl.when(pl.program_id(0) == last_step)
        def _():
            for cp in make_copies(parts, got, send_sems, recv_sems):
                cp.wait()

    return wrapped


def _scatter_operands(parts, per_device):
    n = len(parts)
    if per_device:
        slots, shapes = 7, [jax.ShapeDtypeStruct((7, a.shape[1] // 2, a.shape[2]), a.dtype) for a in parts]
    else:
        slots, shapes = 3, [jax.ShapeDtypeStruct((3,) + a.shape[1:], a.dtype) for a in parts]
    sems = [pltpu.SemaphoreType.DMA((slots * n,)), pltpu.SemaphoreType.DMA((slots * n,))]
    return [HBM_REF] * n, [HBM_REF] * n, shapes, sems


def _shifted_copies(win_ref, sh_ref, rows):
    for p in range(1, 8):
        sh_ref[p - 1, 0:rows, :] = win_ref[pl.ds(p, rows), :]


def _tap_rows(win_ref, sh_ref, start, rows):
    p = start % 8
    if p == 0:
        return win_ref[pl.ds(start, rows), :]
    return sh_ref[p - 1, pl.ds(start - p, rows), :]


def _conv_taps(win_ref, sh_ref, w_ref, rows, chunk, offset_of_tap):
    pieces = []
    for c0 in range(0, rows, chunk):
        acc = None
        for j in range(KC):
            term = w_ref[j:j + 1, :] * _tap_rows(win_ref, sh_ref, c0 + offset_of_tap(j), chunk)
            acc = term if acc is None else acc + term
        pieces.append(acc)
    return pieces


def _conv_fwd(proj_a, conv_w, conv_b, ln_w, ln_b, ts, chunk, shards):
    s = proj_a.shape[0]

    def body(av_ref, al_ref, ag_ref, w_ref, b_ref, lw_ref, lb_ref, u0_ref, u1_ref, za_ref, win_ref, sh_ref):
        @pl.when(pl.program_id(0) == 0)
        def _():
            win_ref[0:HALO, :] = jnp.zeros((HALO, D), F32)

        u0 = av_ref[...] * _sigmoid(al_ref[...])
        u0_ref[...] = u0
        win_ref[HALO:HALO + ts, :] = u0
        _shifted_copies(win_ref, sh_ref, ts + HALO - 8)
        pieces = _conv_taps(win_ref, sh_ref, w_ref, ts, chunk, lambda j: HALO - (KC - 1) + j)
        for n, acc in enumerate(pieces):
            u1_ref[n * chunk:(n + 1) * chunk, :] = acc + b_ref[...]
        win_ref[0:HALO, :] = win_ref[ts:ts + HALO, :]

        u1 = u1_ref[...]
        xc = u1 - _rowmean(u1)
        rstd = lax.rsqrt(_rowmean(xc * xc) + EPS)
        u2 = xc * rstd * lw_ref[...] + lb_ref[...]
        u3 = u2 * _sigmoid(u2)
        ag = ag_ref[...]
        za_ref[...] = (u3 * (ag * _sigmoid(ag))).astype(BF16)

    col = lambda c: pl.BlockSpec((ts, D), lambda i, c=c: (i, c))
    row = pl.BlockSpec((ts, D), lambda i: (i, 0))
    vec = pl.BlockSpec((1, D), lambda i: (0, 0))
    n = len(shards)
    gathered_shapes, sems = _gather_operands(shards)
    outs = pl.pallas_call(
        _gather_alongside(body, 7, 3, n, s // ts - 1), grid=(s // ts,),
        in_specs=[col(0), col(1), col(2), pl.BlockSpec((HALO, D), lambda i: (0, 0)), vec, vec, vec] + [HBM_REF] * n,
        out_specs=[row, row, row] + [HBM_REF] * n,
        out_shape=[jax.ShapeDtypeStruct((s, D), F32), jax.ShapeDtypeStruct((s, D), F32),
                   jax.ShapeDtypeStruct((s, D), BF16)] + gathered_shapes,
        scratch_shapes=[pltpu.VMEM((ts + HALO, D), F32), pltpu.VMEM((7, ts + HALO, D), F32)] + sems,
        name="conv_fwd", compiler_params=_params(("arbitrary",), VMEM_BIG))(
            proj_a, proj_a, proj_a, conv_w, conv_b, ln_w, ln_b, *shards)
    return outs[0], outs[1], outs[2], _as_chip_slabs(outs[3:], shards)


def _conv_bwd(dza, proj_a, u0, u1, conv_w, ln_w, ln_b, ts, chunk, parts):
    s = dza.shape[0]
    nt = s // ts
    per = ts // HALO

    def body(dza_ref, av_ref, al_ref, ag_ref, u0_ref, u0p_ref, u1_ref, w_ref, lw_ref, lb_ref,
             dpa_ref, gw_ref, gv_ref, dwin_ref, uwin_ref, du0_ref, gwp_ref, dsh_ref, ush_ref):
        step = pl.program_id(0)
        tile = nt - 1 - step

        @pl.when(step == 0)
        def _():
            dwin_ref[ts:ts + HALO, :] = jnp.zeros((HALO, D), F32)
            gwp_ref[...] = jnp.zeros_like(gwp_ref)
            gv_ref[...] = jnp.zeros_like(gv_ref)

        ag = ag_ref[...]
        sg = _sigmoid(ag)
        u1 = u1_ref[...]
        xc = u1 - _rowmean(u1)
        rstd = lax.rsqrt(_rowmean(xc * xc) + EPS)
        xh = xc * rstd
        u2 = xh * lw_ref[...] + lb_ref[...]
        s2 = _sigmoid(u2)
        dz = dza_ref[...]
        du3 = dz * (ag * sg)
        dpa_ref[:, 2 * D:3 * D] = (dz * (u2 * s2) * _dsilu(ag, sg)).astype(BF16)
        du2 = du3 * _dsilu(u2, s2)
        gv_ref[0:1, :] += _colsum(du2 * xh)
        gv_ref[1:2, :] += _colsum(du2)
        dxh = du2 * lw_ref[...]
        du1 = rstd * (dxh - _rowmean(dxh) - xh * _rowmean(dxh * xh))
        gv_ref[2:3, :] += _colsum(du1)
        dwin_ref[0:ts, :] = du1

        uwin_ref[0:HALO, :] = jnp.where(tile == 0, 0.0, u0p_ref[...])
        uwin_ref[HALO:HALO + ts, :] = u0_ref[...]

        _shifted_copies(dwin_ref, dsh_ref, ts + HALO - 8)
        _shifted_copies(uwin_ref, ush_ref, ts + HALO - 8)
        pieces = _conv_taps(dwin_ref, dsh_ref, w_ref, ts, chunk, lambda j: (KC - 1) - j)
        for n, acc in enumerate(pieces):
            du0_ref[n * chunk:(n + 1) * chunk, :] = acc
        for c0 in range(0, ts, chunk):
            dchunk = dwin_ref[c0:c0 + chunk, :]
            for j in range(KC):
                prod = dchunk * _tap_rows(uwin_ref, ush_ref, c0 + HALO - (KC - 1) + j, chunk)
                gwp_ref[8 * j:8 * j + 8, :] += jnp.sum(prod.reshape(chunk // 8, 8, D), axis=0)
        dwin_ref[ts:ts + HALO, :] = dwin_ref[0:HALO, :]

        du0 = du0_ref[...]
        al = al_ref[...]
        sl = _sigmoid(al)
        dpa_ref[:, 0:D] = (du0 * sl).astype(BF16)
        dpa_ref[:, D:2 * D] = (du0 * av_ref[...] * sl * (1.0 - sl)).astype(BF16)

        @pl.when(step == nt - 1)
        def _():
            for j in range(KC):
                gw_ref[j:j + 1, :] = _colsum(gwp_ref[8 * j:8 * j + 8, :])
            gw_ref[KC:HALO, :] = jnp.zeros((HALO - KC, D), F32)

    rev = lambda i: nt - 1 - i
    col = lambda c: pl.BlockSpec((ts, D), lambda i, c=c: (rev(i), c))
    row = pl.BlockSpec((ts, D), lambda i: (rev(i), 0))
    vec = pl.BlockSpec((1, D), lambda i: (0, 0))
    halo = pl.BlockSpec((HALO, D), lambda i: (jnp.maximum(rev(i) * per - 1, 0), 0))
    side_in, side_out, side_shapes, side_sems = _scatter_operands(parts, True)
    outs = pl.pallas_call(
        _scatter_alongside(body, 10, 3, len(parts), nt - 1, _device_scatter_copies), grid=(nt,),
        in_specs=[row, col(0), col(1), col(2), row, halo, row, pl.BlockSpec((HALO, D), lambda i: (0, 0)), vec, vec]
        + side_in,
        out_specs=[pl.BlockSpec((ts, A_COLS), lambda i: (rev(i), 0)),
                   pl.BlockSpec((HALO, D), lambda i: (0, 0)), pl.BlockSpec((8, D), lambda i: (0, 0))] + side_out,
        out_shape=[jax.ShapeDtypeStruct((s, A_COLS), BF16), jax.ShapeDtypeStruct((HALO, D), F32),
                   jax.ShapeDtypeStruct((8, D), F32)] + side_shapes,
        scratch_shapes=[pltpu.VMEM((ts + HALO, D), F32), pltpu.VMEM((ts + HALO, D), F32),
                        pltpu.VMEM((ts, D), F32), pltpu.VMEM((8 * HALO, D), F32),
                        pltpu.VMEM((7, ts + HALO, D), F32), pltpu.VMEM((7, ts + HALO, D), F32)] + side_sems,
        name="conv_bwd", compiler_params=_params(("arbitrary",), VMEM_BIG))(
            dza, proj_a, proj_a, proj_a, u0, u0, u1, conv_w, ln_w, ln_b, *parts)
    return outs[0], outs[1], outs[2], list(outs[3:])


def _mla_prep(proj_l, q_norm_w, kv_norm_w, w_uq2, w_ukv, cos_t, sin_t, ts):
    s = proj_l.shape[0]

    def body(pl_ref, qw_ref, kw_ref, wq_ref, wkv_ref, c_ref, s_ref, qn_ref, kvn_ref, q_ref, k_ref, v_ref):
        first = _first_half_mask(ts)
        cs = c_ref[...]
        sn = s_ref[...]

        def rms(v, w):
            return v * lax.rsqrt(_rowmean(v * v) + EPS) * w

        def rope(v):
            return v * cs + _swap_halves(v, first) * sn

        qn = rms(pl_ref[:, 0:RQ], qw_ref[...]).astype(BF16)
        kvn = rms(pl_ref[:, RQ:2 * RQ], kw_ref[...]).astype(BF16)
        qn_ref[...] = qn
        kvn_ref[...] = kvn
        q = _dot(qn, wq_ref[...])
        kv = _dot(kvn, wkv_ref[...])
        kr = rope(pl_ref[:, 2 * RQ:2 * RQ + LANE]).astype(BF16)
        for h in range(H):
            q_ref[h, :, 0:DN] = q[:, DN * h:DN * (h + 1)].astype(BF16)
            q_ref[h, :, DN:2 * DN] = rope(q[:, H * DN + LANE * h:H * DN + LANE * (h + 1)]).astype(BF16)
            k_ref[h, :, 0:DN] = kv[:, 2 * DN * h:2 * DN * h + DN].astype(BF16)
            k_ref[h, :, DN:2 * DN] = kr
            v_ref[h, :, 0:DN] = kv[:, 2 * DN * h + DN:2 * DN * (h + 1)].astype(BF16)
            v_ref[h, :, DN:2 * DN] = jnp.ones((ts, DN), BF16)

    const = lambda shape: pl.BlockSpec(shape, lambda i: (0,) * len(shape))
    rowb = lambda w: pl.BlockSpec((ts, w), lambda i: (i, 0))
    head = lambda w: pl.BlockSpec((H, ts, w), lambda i: (0, i, 0))
    return pl.pallas_call(
        body, grid=(s // ts,),
        in_specs=[rowb(L_COLS), const((1, RQ)), const((1, RQ)), const((RQ, 2 * H * DN)), const((RQ, 2 * H * DN)),
                  rowb(LANE), rowb(LANE)],
        out_specs=[rowb(RQ), rowb(RQ), head(2 * DN), head(2 * DN), head(2 * DN)],
        out_shape=[jax.ShapeDtypeStruct((s, RQ), BF16), jax.ShapeDtypeStruct((s, RQ), BF16),
                   jax.ShapeDtypeStruct((H, s, 2 * DN), BF16), jax.ShapeDtypeStruct((H, s, 2 * DN), BF16),
                   jax.ShapeDtypeStruct((H, s, 2 * DN), BF16)],
        name="mla_prep", compiler_params=_params(("parallel",)))(
            proj_l, q_norm_w, kv_norm_w, w_uq2, w_ukv, cos_t, sin_t)


def _mla_prep_bwd(dq, dk, dv, proj_l, qn, kvn, q_norm_w, kv_norm_w, w_uq2, w_ukv, cos_t, sin_t, ts):
    s = proj_l.shape[0]

    def body(dq_ref, dk_ref, dv_ref, pl_ref, qn_ref, kvn_ref, qw_ref, kw_ref, wq_ref, wkv_ref, c_ref, s_ref,
             dpl_ref, gwq_ref, gwkv_ref, gv_ref, dq2_ref, dkv2_ref):
        @pl.when(pl.program_id(0) == 0)
        def _():
            gwq_ref[...] = jnp.zeros_like(gwq_ref)
            gwkv_ref[...] = jnp.zeros_like(gwkv_ref)
            gv_ref[...] = jnp.zeros_like(gv_ref)

        first = _first_half_mask(ts)
        cs = c_ref[...] * ATT_SCALE
        sn = s_ref[...] * ATT_SCALE

        def rope_bwd(g):
            return g * cs + _swap_halves(g * sn, first)

        def rms_bwd(v, w, dy):
            r = lax.rsqrt(_rowmean(v * v) + EPS)
            vh = v * r
            dvh = dy * w
            return r * (dvh - vh * _rowmean(dvh * vh)), _colsum(dy * vh)

        dkr = None
        for h in range(H):
            dq2_ref[:, DN * h:DN * (h + 1)] = (dq_ref[h, :, 0:DN] * ATT_SCALE).astype(BF16)
            dq2_ref[:, H * DN + LANE * h:H * DN + LANE * (h + 1)] = rope_bwd(dq_ref[h, :, DN:2 * DN]).astype(BF16)
            dkv2_ref[:, 2 * DN * h:2 * DN * h + DN] = (dk_ref[h, :, 0:DN] * ATT_SCALE).astype(BF16)
            dkv2_ref[:, 2 * DN * h + DN:2 * DN * (h + 1)] = dv_ref[h].astype(BF16)
            part = dk_ref[h, :, DN:2 * DN]
            dkr = part if dkr is None else dkr + part

        dq2 = dq2_ref[...]
        dkv2 = dkv2_ref[...]
        gwq_ref[...] += _dot_tn(qn_ref[...], dq2)
        gwkv_ref[...] += _dot_tn(kvn_ref[...], dkv2)
        dcq, gq = rms_bwd(pl_ref[:, 0:RQ], qw_ref[...], _dot_nt(dq2, wq_ref[...]))
        dckv, gkv = rms_bwd(pl_ref[:, RQ:2 * RQ], kw_ref[...], _dot_nt(dkv2, wkv_ref[...]))
        gv_ref[0:1, :] += gq
        gv_ref[1:2, :] += gkv
        dpl_ref[:, 0:RQ] = dcq.astype(BF16)
        dpl_ref[:, RQ:2 * RQ] = dckv.astype(BF16)
        dpl_ref[:, 2 * RQ:2 * RQ + LANE] = rope_bwd(dkr).astype(BF16)

    const = lambda shape: pl.BlockSpec(shape, lambda i: (0,) * len(shape))
    rowb = lambda w: pl.BlockSpec((ts, w), lambda i: (i, 0))
    head = lambda w: pl.BlockSpec((H, ts, w), lambda i: (0, i, 0))
    return pl.pallas_call(
        body, grid=(s // ts,),
        in_specs=[head(2 * DN), head(2 * DN), head(DN), rowb(L_COLS), rowb(RQ), rowb(RQ), const((1, RQ)),
                  const((1, RQ)), const((RQ, 2 * H * DN)), const((RQ, 2 * H * DN)), rowb(LANE), rowb(LANE)],
        out_specs=[rowb(L_COLS), const((RQ, 2 * H * DN)), const((RQ, 2 * H * DN)), const((8, RQ))],
        out_shape=[jax.ShapeDtypeStruct((s, L_COLS), BF16), jax.ShapeDtypeStruct((RQ, 2 * H * DN), F32),
                   jax.ShapeDtypeStruct((RQ, 2 * H * DN), F32), jax.ShapeDtypeStruct((8, RQ), F32)],
        scratch_shapes=[pltpu.VMEM((ts, 2 * H * DN), BF16), pltpu.VMEM((ts, 2 * H * DN), BF16)],
        name="mla_prep_bwd", compiler_params=_params(("arbitrary",), VMEM_BIG))(
            dq, dk, dv, proj_l, qn, kvn, q_norm_w, kv_norm_w, w_uq2, w_ukv, cos_t, sin_t)


def _causal_pairs(n, by_key):
    if by_key:
        pairs = [(i, j) for j in range(n) for i in range(j, n)]
    else:
        pairs = [(i, j) for i in range(n) for j in range(i + 1)]
    return (jnp.asarray(np.array([p[0] for p in pairs], np.int32)),
            jnp.asarray(np.array([p[1] for p in pairs], np.int32)))


ATT_SCALE = float((DN + DR) ** -0.5)
LOG2E = 1.4426950408889634
LN2 = 0.6931471805599453
ATT_HEADS_FWD = 4
ATT_HEADS = 2
W_IN_ROWS = 336
ATT_ROWS = 64


def _diag_width(r0, t):
    return min(t, -(-(r0 + ATT_ROWS) // LANE) * LANE)


def _diag_mask_rows(r0, width):
    rows = r0 + lax.broadcasted_iota(jnp.int32, (ATT_ROWS, width), 0)
    cols = lax.broadcasted_iota(jnp.int32, (ATT_ROWS, width), 1)
    return cols <= rows


def _diag_mask(t):
    return lax.broadcasted_iota(jnp.int32, (t, t), 1) <= lax.broadcasted_iota(jnp.int32, (t, t), 0)


def _attn_fwd(q, k, v, t):
    s = q.shape[1]
    n = s // t
    scale2 = float((DN + DR) ** -0.5) * LOG2E
    qi, ki = _causal_pairs(n, by_key=False)

    def body(qi_ref, ki_ref, q_ref, k_ref, v_ref, o_ref, lse_ref, *scratch):
        per_head = [scratch[5 * h:5 * h + 5] for h in range(ATT_HEADS_FWD)]
        p = pl.program_id(1)
        i = qi_ref[p]
        j = ki_ref[p]

        @pl.when(j == 0)
        def _():
            for m_sc, acc_sc, _, _, _ in per_head:
                m_sc[...] = jnp.full_like(m_sc, -jnp.inf)
                acc_sc[...] = jnp.zeros_like(acc_sc)

        def scores(h, diag):
            sc = _dot_nt(q_ref[h], k_ref[h])
            if diag:
                sc = jnp.where(_diag_mask(t), sc, -jnp.inf)
            per_head[h][2][...] = sc

        def rowmax(h, rows):
            per_head[h][4][rows, :] = jnp.max(per_head[h][2][rows, :], axis=-1, keepdims=True)

        def stats(h):
            m_sc, acc_sc, _, _, mx_sc = per_head[h]
            m_prev = m_sc[...]
            m_new = jnp.maximum(m_prev, mx_sc[...] * scale2)
            m_sc[...] = m_new
            acc_sc[...] = jnp.exp2(m_prev - m_new) * acc_sc[...]

        def probs(h, rows):
            m_sc, _, s_sc, p_sc, _ = per_head[h]
            p_sc[rows, :] = jnp.exp2(s_sc[rows, :] * scale2 - m_sc[rows, :]).astype(BF16)

        def values(h):
            _, acc_sc, _, p_sc, _ = per_head[h]
            acc_sc[...] += _dot(p_sc[...], v_ref[h])

        def step(diag):
            blocks = [slice(r0, r0 + ATT_ROWS) for r0 in range(0, t, ATT_ROWS)]
            for h in range(ATT_HEADS_FWD):
                scores(h, diag)
            for rows in blocks:
                rowmax(0, rows)
            stats(0)
            for h in range(ATT_HEADS_FWD):
                for rows in blocks:
                    probs(h, rows)
                    if h + 1 < ATT_HEADS_FWD:
                        rowmax(h + 1, rows)
                if h + 1 < ATT_HEADS_FWD:
                    stats(h + 1)
                values(h)

        @pl.when(j < i)
        def _():
            step(False)

        @pl.when(j == i)
        def _():
            step(True)
            for h, (m_sc, acc_sc, _, _, _) in enumerate(per_head):
                l = acc_sc[:, DN:2 * DN]
                o_ref[:, DN * h:DN * (h + 1)] = acc_sc[:, 0:DN] / l
                lse_ref[h] = (m_sc[...] + jnp.log2(l[:, 0:1])) * LN2

    hb = ATT_HEADS_FWD
    grid_spec = pltpu.PrefetchScalarGridSpec(
        num_scalar_prefetch=2, grid=(H // hb, int(qi.shape[0])),
        in_specs=[pl.BlockSpec((hb, t, 2 * DN), lambda h, p, qi, ki: (h, qi[p], 0)),
                  pl.BlockSpec((hb, t, 2 * DN), lambda h, p, qi, ki: (h, ki[p], 0)),
                  pl.BlockSpec((hb, t, 2 * DN), lambda h, p, qi, ki: (h, ki[p], 0))],
        out_specs=[pl.BlockSpec((t, hb * DN), lambda h, p, qi, ki: (qi[p], h)),
                   pl.BlockSpec((hb, t, 1), lambda h, p, qi, ki: (h, qi[p], 0))],
        scratch_shapes=[pltpu.VMEM((t, 1), F32), pltpu.VMEM((t, 2 * DN), F32), pltpu.VMEM((t, t), F32),
                        pltpu.VMEM((t, t), BF16), pltpu.VMEM((t, 1), F32)] * hb)
    return pl.pallas_call(
        body, grid_spec=grid_spec,
        out_shape=[jax.ShapeDtypeStruct((s, H * DN), F32), jax.ShapeDtypeStruct((H, s, 1), F32)],
        name="attn_fwd", compiler_params=_params(("parallel", "arbitrary"), VMEM_BIG))(qi, ki, q, k, v)


def _attn_bwd(q, k, v, do, lse, delta, t):
    s = q.shape[1]
    n = s // t
    scale = ATT_SCALE
    qi, ki = _causal_pairs(n, by_key=True)

    def body(qi_ref, ki_ref, q_ref, k_ref, v_ref, do_ref, lse_ref, dl_ref, dq_ref, dk_ref, dv_ref,
             dk_sc, dv_sc, s_sc, dp_sc, p_sc, ds_sc):
        p = pl.program_id(1)
        i = qi_ref[p]
        j = ki_ref[p]

        @pl.when(p == 0)
        def _():
            dq_ref[...] = jnp.zeros_like(dq_ref)

        @pl.when(i == j)
        def _():
            dk_sc[...] = jnp.zeros_like(dk_sc)
            dv_sc[...] = jnp.zeros_like(dv_sc)

        def step(diag):
            for h in range(ATT_HEADS):
                s_sc[h] = _dot_nt(q_ref[h], k_ref[h])
                dp_sc[h] = _dot_nt(do_ref[:, DN * h:DN * (h + 1)], v_ref[h, :, 0:DN])
            for h in range(ATT_HEADS):
                for r0 in range(0, t, ATT_ROWS):
                    rows = slice(r0, r0 + ATT_ROWS)
                    width = _diag_width(r0, t) if diag else t
                    sc = s_sc[h, rows, 0:width] * (scale * LOG2E)
                    if diag:
                        sc = jnp.where(_diag_mask_rows(r0, width), sc, -jnp.inf)
                    pr = jnp.exp2(sc - lse_ref[h, rows, :] * LOG2E)
                    ds = pr * (dp_sc[h, rows, 0:width] - dl_ref[h, rows, :])
                    p_sc[h, rows, 0:width] = pr.astype(BF16)
                    ds_sc[h, rows, 0:width] = ds.astype(BF16)
                    if width < t:
                        p_sc[h, rows, width:t] = jnp.zeros((ATT_ROWS, t - width), BF16)
                        ds_sc[h, rows, width:t] = jnp.zeros((ATT_ROWS, t - width), BF16)
            q_rows = pl.ds(pl.multiple_of(i * t, t), t)
            for h in range(ATT_HEADS):
                dv_sc[h] += _dot_tn(p_sc[h], do_ref[:, DN * h:DN * (h + 1)])
                dk_sc[h] += _dot_tn(ds_sc[h], q_ref[h])
                dq_ref[h, q_rows, :] += _dot(ds_sc[h], k_ref[h])

        @pl.when(i > j)
        def _():
            step(False)

        @pl.when(i == j)
        def _():
            step(True)

        @pl.when(i == n - 1)
        def _():
            dk_ref[...] = dk_sc[...]
            dv_ref[...] = dv_sc[...]

    hb = ATT_HEADS
    grid_spec = pltpu.PrefetchScalarGridSpec(
        num_scalar_prefetch=2, grid=(H // hb, int(qi.shape[0])),
        in_specs=[pl.BlockSpec((hb, t, 2 * DN), lambda h, p, qi, ki: (h, qi[p], 0)),
                  pl.BlockSpec((hb, t, 2 * DN), lambda h, p, qi, ki: (h, ki[p], 0)),
                  pl.BlockSpec((hb, t, 2 * DN), lambda h, p, qi, ki: (h, ki[p], 0)),
                  pl.BlockSpec((t, hb * DN), lambda h, p, qi, ki: (qi[p], h)),
                  pl.BlockSpec((hb, t, 1), lambda h, p, qi, ki: (h, qi[p], 0)),
                  pl.BlockSpec((hb, t, 1), lambda h, p, qi, ki: (h, qi[p], 0))],
        out_specs=[pl.BlockSpec((hb, s, 2 * DN), lambda h, p, qi, ki: (h, 0, 0)),
                   pl.BlockSpec((hb, t, 2 * DN), lambda h, p, qi, ki: (h, ki[p], 0)),
                   pl.BlockSpec((hb, t, DN), lambda h, p, qi, ki: (h, ki[p], 0))],
        scratch_shapes=[pltpu.VMEM((hb, t, 2 * DN), F32), pltpu.VMEM((hb, t, DN), F32),
                        pltpu.VMEM((hb, t, t), F32), pltpu.VMEM((hb, t, t), F32),
                        pltpu.VMEM((hb, t, t), BF16), pltpu.VMEM((hb, t, t), BF16)])
    return pl.pallas_call(
        body, grid_spec=grid_spec,
        out_shape=[jax.ShapeDtypeStruct((H, s, 2 * DN), F32), jax.ShapeDtypeStruct((H, s, 2 * DN), F32),
                   jax.ShapeDtypeStruct((H, s, DN), F32)],
        name="attn_bwd", compiler_params=_params(("parallel", "arbitrary"), VMEM_BIG))(
            qi, ki, q, k, v, do, lse, delta)


def _middle(za, o, proj_g, x, tgt, gate, fnw, wco, wao, wo, ts):
    s = x.shape[0]
    inv_d = 1.0 / D

    def body(za_ref, o_ref, bg_ref, ga_ref, gb_ref, x_ref, t_ref, gate_ref, fnw_ref, wco_ref, wao_ref, wo_ref,
             dx2_ref, dza_ref, do_ref, dl_ref, dpg_ref, zb_ref, mg_ref, dmo_ref, dya_ref, dyb_ref, vec_ref):
        @pl.when(pl.program_id(0) == 0)
        def _():
            vec_ref[...] = jnp.zeros_like(vec_ref)

        ov = o_ref[...]
        bg = bg_ref[...]
        sb = _sigmoid(bg)
        silu_b = bg * sb
        zb = (ov * silu_b).astype(BF16)
        zb_ref[...] = zb
        ya = _dot(za_ref[...], wco_ref[...])
        yb = _dot(zb, wao_ref[...])
        sa = _sigmoid(ga_ref[...])
        sg = _sigmoid(gb_ref[...])
        mg = (sa * ya + sg * yb).astype(BF16)
        mg_ref[...] = mg
        mo = _dot(mg, wo_ref[...])
        gate_v = gate_ref[...]
        x2 = x_ref[...] + gate_v * mo
        r = lax.rsqrt(_rowmean(x2 * x2) + EPS)
        xh = x2 * r
        fw = fnw_ref[...]
        e = xh * fw - t_ref[...]
        vec_ref[2:3, :] += _colsum(e * e)
        dy = e * inv_d
        vec_ref[0:1, :] += _colsum(dy * xh)
        dxh = dy * fw
        dx2 = r * (dxh - xh * _rowmean(dxh * xh))
        dx2_ref[...] = dx2
        vec_ref[1:2, :] += _colsum(dx2 * mo)
        dmo = (gate_v * dx2).astype(BF16)
        dmo_ref[...] = dmo
        dmg = _dot_nt(dmo, wo_ref[...])
        dya = (sa * dmg).astype(BF16)
        dyb = (sg * dmg).astype(BF16)
        dya_ref[...] = dya
        dyb_ref[...] = dyb
        dpg_ref[:, D:2 * D] = (dmg * ya * (sa * (1.0 - sa))).astype(BF16)
        dpg_ref[:, 2 * D:3 * D] = (dmg * yb * (sg * (1.0 - sg))).astype(BF16)
        dza_ref[...] = _dot_nt(dya, wco_ref[...])
        dzb = _dot_nt(dyb, wao_ref[...])
        dov = dzb * silu_b
        do_ref[...] = dov.astype(BF16)
        dpg_ref[:, 0:D] = (dzb * ov * _dsilu(bg, sb)).astype(BF16)
        dprod = dov * ov
        for h in range(H):
            dl_ref[h] = jnp.sum(dprod[:, DN * h:DN * (h + 1)], axis=-1, keepdims=True)

    col = lambda c: pl.BlockSpec((ts, D), lambda i, c=c: (i, c))
    row = pl.BlockSpec((ts, D), lambda i: (i, 0))
    vec = pl.BlockSpec((1, D), lambda i: (0, 0))
    wsp = pl.BlockSpec((D, D), lambda i: (0, 0))
    bf = jax.ShapeDtypeStruct((s, D), BF16)
    ff = jax.ShapeDtypeStruct((s, D), F32)
    return pl.pallas_call(
        body, grid=(s // ts,),
        in_specs=[row, row, col(0), col(1), col(2), row, row, vec, vec, wsp, wsp, wsp],
        out_specs=[row, row, row, pl.BlockSpec((H, ts, 1), lambda i: (0, i, 0)),
                   pl.BlockSpec((ts, G_COLS), lambda i: (i, 0)), row, row, row, row, row,
                   pl.BlockSpec((8, D), lambda i: (0, 0))],
        out_shape=[ff, ff, bf, jax.ShapeDtypeStruct((H, s, 1), F32), jax.ShapeDtypeStruct((s, G_COLS), BF16),
                   bf, bf, bf, bf, bf, jax.ShapeDtypeStruct((8, D), F32)],
        name="middle", compiler_params=_params(("arbitrary",), VMEM_BIG))(
            za, o, proj_g, proj_g, proj_g, x, tgt, gate, fnw, wco, wao, wo)


def _input_bwd(dpa, dpl, dpg, wa, wl, wg, x, dx2, norm_w, scale, ts, parts):
    s = x.shape[0]

    def body(dpa_ref, dpl_ref, dpg_ref, wa_ref, wl_ref, wg_ref, x_ref, dx2_ref, nw_ref, sc_ref, gx_ref, gv_ref):
        @pl.when(pl.program_id(0) == 0)
        def _():
            gv_ref[...] = jnp.zeros_like(gv_ref)

        dh = (_dot_nt(dpa_ref[...], wa_ref[...]) + _dot_nt(dpl_ref[...], wl_ref[...])
              + _dot_nt(dpg_ref[...], wg_ref[...]))
        xv = x_ref[...]
        r = lax.rsqrt(_rowmean(xv * xv) + EPS)
        xh = xv * r
        nw = nw_ref[...]
        gv_ref[0:1, :] += _colsum(dh)
        gv_ref[1:2, :] += _colsum(dh * (xh * nw))
        dy = dh * (1.0 + sc_ref[...])
        gv_ref[2:3, :] += _colsum(dy * xh)
        dxh = dy * nw
        gx_ref[...] = dx2_ref[...] + r * (dxh - xh * _rowmean(dxh * xh))

    const = lambda shape: pl.BlockSpec(shape, lambda i: (0, 0))
    rowb = lambda w: pl.BlockSpec((ts, w), lambda i: (i, 0))
    side_in, side_out, side_shapes, side_sems = _scatter_operands(parts, False)
    outs = pl.pallas_call(
        _scatter_alongside(body, 10, 2, len(parts), s // ts - 1, _chip_scatter_copies), grid=(s // ts,),
        in_specs=[rowb(A_COLS), rowb(L_COLS), rowb(G_COLS), const((D, A_COLS)), const((D, L_COLS)),
                  const((D, G_COLS)), rowb(D), rowb(D), const((1, D)), const((1, D))] + side_in,
        out_specs=[rowb(D), const((8, D))] + side_out,
        out_shape=[jax.ShapeDtypeStruct((s, D), F32), jax.ShapeDtypeStruct((8, D), F32)] + side_shapes,
        scratch_shapes=side_sems,
        name="input_bwd", compiler_params=_params(("arbitrary",), VMEM_BIG))(
            dpa, dpl, dpg, wa, wl, wg, x, dx2, norm_w, scale, *parts)
    return outs[0], outs[1], list(outs[2:])


def _adamw_math(w, g, m, v):
    nm = ADAM_B1 * m + (1.0 - ADAM_B1) * g
    nv = ADAM_B2 * v + (1.0 - ADAM_B2) * (g * g)
    m_hat = nm / (1.0 - ADAM_B1 ** ADAM_STEP)
    v_hat = nv / (1.0 - ADAM_B2 ** ADAM_STEP)
    return -ADAM_LR * (m_hat / (jnp.sqrt(v_hat) + ADAM_EPS) + ADAM_WD * w), nm, nv


def _adamw(w, g, m, v, tr, name):
    lead, (rows, cols) = w.shape[:-2], w.shape[-2:]

    def body(w_ref, g_ref, m_ref, v_ref, d_ref, nm_ref, nv_ref):
        d_ref[...], nm_ref[...], nv_ref[...] = _adamw_math(w_ref[...], g_ref[...], m_ref[...], v_ref[...])

    blk = pl.BlockSpec((1,) * len(lead) + (tr, cols), lambda i: (0,) * len(lead) + (i, 0))
    shp = jax.ShapeDtypeStruct(w.shape, F32)
    return pl.pallas_call(
        body, grid=(rows // tr,), in_specs=[blk] * 4, out_specs=[blk] * 3, out_shape=[shp] * 3, name=name,
        compiler_params=_params(("parallel",), VMEM_BIG))(w, g.reshape(w.shape), m, v)


ROW_SHIFT, ROW_SCALE, ROW_NORM_W = 0, 1, 2
ROW_FINAL_NORM_W, ROW_GATE, ROW_LOSS = 8, 9, 10
ROW_LN_W, ROW_LN_B, ROW_CONV_B = 16, 17, 18
ROW_Q_NORM_W, ROW_KV_NORM_W = 24, 25
ROW_CONV_W = 32
SUM_ROWS = 64
VECTOR_ROWS = ((ROW_SHIFT, ROW_SCALE, ROW_GATE), (ROW_NORM_W,), (ROW_CONV_B,), (ROW_LN_W,), (ROW_LN_B,),
               (ROW_Q_NORM_W,), (ROW_KV_NORM_W,), (ROW_FINAL_NORM_W,))


def _small_finalize(gathered, vectors, conv, chip):
    n = len(vectors)
    cw = conv[0].shape[2]

    def body(chip_ref, g_ref, *refs):
        ins, outs = refs[:3 * n + 3], refs[3 * n + 3:]
        tot = g_ref[0]
        for k in range(1, N_DEV):
            tot = tot + g_ref[k]
        for p, rows in enumerate(VECTOR_ROWS):
            w_ref, m_ref, v_ref = ins[3 * p:3 * p + 3]
            g_out, d_out, nm_out, nv_out = outs[4 * p:4 * p + 4]
            width = w_ref.shape[1] // len(rows)
            for q, r in enumerate(rows):
                lanes = slice(q * width, (q + 1) * width)
                g = tot[r:r + 1, 0:width]
                g_out[:, lanes] = g
                d_out[:, lanes], nm_out[:, lanes], nv_out[:, lanes] = _adamw_math(
                    w_ref[:, lanes], g, m_ref[:, lanes], v_ref[:, lanes])
        cols = pl.ds(pl.multiple_of(chip_ref[0] * cw, LANE), cw)
        gc = g_ref[0, pl.ds(ROW_CONV_W, KC), cols]
        for k in range(1, N_DEV):
            gc = gc + g_ref[k, pl.ds(ROW_CONV_W, KC), cols]
        cw_ref, cm_ref, cv_ref = ins[3 * n:3 * n + 3]
        g_out, d_out, nm_out, nv_out, dmod_ref, loss_ref = outs[4 * n:]
        g_out[0] = gc
        d_out[0], nm_out[0], nv_out[0] = _adamw_math(cw_ref[0], gc, cm_ref[0], cv_ref[0])
        for k in range(N_DEV):
            for q, r in enumerate((ROW_SHIFT, ROW_SCALE, ROW_GATE)):
                dmod_ref[k:k + 1, q * D:(q + 1) * D] = g_ref[k, r:r + 1, :]
        loss_ref[...] = (0.5 / D) * jnp.sum(tot[ROW_LOSS:ROW_LOSS + 1, :], axis=-1, keepdims=True)

    flat_in = [a for triple in vectors for a in triple] + list(conv)
    shapes = [jax.ShapeDtypeStruct(w.shape, F32) for w, _, _ in vectors for _ in range(4)]
    shapes += [jax.ShapeDtypeStruct(conv[0].shape, F32)] * 4
    shapes += [jax.ShapeDtypeStruct((N_DEV, 3 * D), F32), jax.ShapeDtypeStruct((1, 1), F32)]
    whole = pl.BlockSpec(memory_space=pltpu.VMEM)
    return pl.pallas_call(
        body, out_shape=shapes,
        in_specs=[pl.BlockSpec(memory_space=pltpu.SMEM)] + [whole] * (1 + len(flat_in)),
        out_specs=[whole] * len(shapes), name="small_finalize")(chip, gathered, *flat_in)


def _ada_fwd(c_all, w_ada_shard, b_ada_shard):
    def body(c_ref, w_ref, b_ref, o_ref):
        cv = c_ref[...]
        o_ref[...] = jnp.dot(cv * _sigmoid(cv), w_ref[...], preferred_element_type=F32,
                             precision=lax.Precision.HIGHEST) + b_ref[...]

    return pl.pallas_call(
        body, out_shape=jax.ShapeDtypeStruct((N_DEV, w_ada_shard.shape[1]), F32), name="ada_fwd")(
            c_all, w_ada_shard, b_ada_shard)


def _ada_bwd(c_all_t, dmod_shard):
    def body(c_ref, d_ref, o_ref):
        cv = c_ref[...]
        o_ref[...] = jnp.dot(cv * _sigmoid(cv), d_ref[...], preferred_element_type=F32,
                             precision=lax.Precision.HIGHEST)

    return pl.pallas_call(
        body, out_shape=jax.ShapeDtypeStruct((D, dmod_shard.shape[1]), F32), name="ada_bwd")(c_all_t, dmod_shard)


def _sum_chip_slabs(arrived, part, place, tr, name, axis):
    n, rows, cols = arrived.shape
    per = rows // tr
    own_map = ((lambda i, pc: (pc[0], i, 0)) if part.shape[1] == rows
               else (lambda i, pc: (pc[0], pc[1] * per + i, 0)))

    def body(place_ref, a_ref, p_ref, o_ref):
        acc = p_ref[0].astype(F32)
        for k in range(n):
            acc = acc + a_ref[k].astype(F32)
        o_ref[...] = acc

    if axis == 1:
        whole, out_map = (2 * rows, cols), lambda i, pc: (pc[1] * per + i, 0)
    else:
        whole, out_map = (rows, 2 * cols), lambda i, pc: (i, pc[1])
    grid_spec = pltpu.PrefetchScalarGridSpec(
        num_scalar_prefetch=1, grid=(per,),
        in_specs=[pl.BlockSpec((n, tr, cols), lambda i, pc: (0, i, 0)),
                  pl.BlockSpec((1, tr, cols), own_map)],
        out_specs=pl.BlockSpec((tr, cols), out_map))
    return pl.pallas_call(
        body, grid_spec=grid_spec, out_shape=jax.ShapeDtypeStruct(whole, F32), name=name,
        compiler_params=_params(("parallel",)))(place, arrived, part)


def _sum_device_partials(arrived, parts, place):
    n = len(arrived)

    def body(place_ref, *refs):
        a_refs, p_refs, o_refs = refs[:n], refs[n:2 * n], refs[2 * n:]
        for a in range(n):
            acc = p_refs[a][0].astype(F32)
            for k in range(arrived[a].shape[0]):
                acc = acc + a_refs[a][k].astype(F32)
            o_refs[a][...] = acc

    grid_spec = pltpu.PrefetchScalarGridSpec(
        num_scalar_prefetch=1, grid=(1,),
        in_specs=[pl.BlockSpec(a.shape, lambda i, pc: (0, 0, 0)) for a in arrived]
        + [pl.BlockSpec((1,) + a.shape[1:], lambda i, pc: (pc[0], pc[1], 0)) for a in arrived],
        out_specs=[pl.BlockSpec(a.shape[1:], lambda i, pc: (pc[1], 0)) for a in arrived])
    return pl.pallas_call(
        body, grid_spec=grid_spec,
        out_shape=[jax.ShapeDtypeStruct((2 * a.shape[1], a.shape[2]), F32) for a in arrived],
        name="sum_device_partials", compiler_params=_params(("arbitrary",), VMEM_BIG))(place, *arrived, *parts)


def _adamw_many(ws, gs, ms, vs, tr):
    n = len(ws)
    rows = ws[0].shape[1]

    def body(*refs):
        ins, outs = refs[:4 * n], refs[4 * n:]
        for a in range(n):
            w_ref, g_ref, m_ref, v_ref = ins[4 * a:4 * a + 4]
            outs[3 * a][...], outs[3 * a + 1][...], outs[3 * a + 2][...] = _adamw_math(
                w_ref[...], g_ref[...], m_ref[...], v_ref[...])

    blk = lambda w: pl.BlockSpec((1, tr, w.shape[2]), lambda i: (0, i, 0))
    gs = [g.reshape(w.shape) for g, w in zip(gs, ws)]
    flat = [a for quad in zip(ws, gs, ms, vs) for a in quad]
    outs = pl.pallas_call(
        body, grid=(rows // tr,), in_specs=[blk(w) for w in ws for _ in range(4)],
        out_specs=[blk(w) for w in ws for _ in range(3)],
        out_shape=[jax.ShapeDtypeStruct(w.shape, F32) for w in ws for _ in range(3)], name="adamw_small_matrices",
        compiler_params=_params(("parallel",), VMEM_BIG))(*flat)
    return [(gs[a], outs[3 * a], outs[3 * a + 1], outs[3 * a + 2]) for a in range(n)]


def _add_own_half(full, other, core, tr, name, axis):
    n, rows, cols = other.shape
    per = rows // tr

    def body(c_ref, f_ref, o_ref, out_ref):
        out_ref[...] = (f_ref[...] + o_ref[...]).astype(BF16)

    full_map = (lambda k, i, c: (k, c[0] * per + i, 0)) if axis == 1 else (lambda k, i, c: (k, i, c[0]))
    grid_spec = pltpu.PrefetchScalarGridSpec(
        num_scalar_prefetch=1, grid=(n, per),
        in_specs=[pl.BlockSpec((1, tr, cols), full_map),
                  pl.BlockSpec((1, tr, cols), lambda k, i, c: (k, i, 0))],
        out_specs=pl.BlockSpec((1, tr, cols), lambda k, i, c: (k, i, 0)))
    return pl.pallas_call(
        body, grid_spec=grid_spec, out_shape=jax.ShapeDtypeStruct((n, rows, cols), BF16), name=name,
        compiler_params=_params(("parallel", "parallel")))(core, full, other)


def _allgather8(block, src_rows, vmem, name):
    n = block.shape[1]
    m = src_rows
    sliced = block.shape[0] != m

    def body(x_ref, out_ref, send_sems, recv_sems, local_sem):
        x, y, c = _coords()
        me, sibling = (x, y, c), (x, y, 1 - c)
        chips = [(1 - x, y), (x, 1 - y), (1 - x, 1 - y)]
        src = x_ref.at[pl.ds(pl.multiple_of(c * m, 16), m), :] if sliced else x_ref

        def rows(px, py, pc):
            return out_ref.at[pl.ds(pl.multiple_of((4 * px + 2 * py + pc) * m, 8), m), :]

        def copy(k, blk, to, source=None):
            return pltpu.make_async_remote_copy(
                src_ref=rows(*blk) if source is None else source, dst_ref=rows(*blk),
                send_sem=send_sems.at[k], recv_sem=recv_sems.at[k], device_id=to, device_id_type=MESH)

        mine = pltpu.make_async_copy(src, rows(*me), local_sem)
        mine.start()
        first = [copy(0, me, sibling, source=src)]
        first += [copy(1 + j, me, (*chip, c), source=src) for j, chip in enumerate(chips)]
        for cp in first:
            cp.start()
        passed = [copy(4 + j, (*chip, c), sibling) for j, chip in enumerate(chips)]
        for j, chip in enumerate(chips):
            copy(1 + j, (*chip, c), me).wait_recv()
            passed[j].start()
        copy(0, sibling, me).wait_recv()
        for j, chip in enumerate(chips):
            copy(4 + j, (*chip, 1 - c), me).wait_recv()
        for cp in first + passed:
            cp.wait_send()
        mine.wait()

    space = pltpu.VMEM if vmem else pl.ANY
    return pl.pallas_call(
        body, out_shape=jax.ShapeDtypeStruct((N_DEV * m, n), block.dtype),
        in_specs=[pl.BlockSpec(memory_space=space)], out_specs=pl.BlockSpec(memory_space=space),
        scratch_shapes=[pltpu.SemaphoreType.DMA((7,)), pltpu.SemaphoreType.DMA((7,)), pltpu.SemaphoreType.DMA],
        name=name)(block)


def _gather_plan(x_refs, out_refs, send_sems, recv_sems, local_sems):
    n = len(x_refs)
    halves = [r.shape[0] // 2 for r in x_refs]
    x, y, c = _coords()
    me, sibling = (x, y, c), (x, y, 1 - c)
    chips = [(1 - x, y), (x, 1 - y), (1 - x, 1 - y)]

    def src(a):
        return x_refs[a].at[pl.ds(pl.multiple_of(c * halves[a], 16), halves[a]), :]

    def blk(a, px, py, pc):
        return out_refs[a].at[4 * px + 2 * py + pc]

    def copy(a, k, who, to, source=None):
        return pltpu.make_async_remote_copy(
            src_ref=blk(a, *who) if source is None else source, dst_ref=blk(a, *who),
            send_sem=send_sems.at[7 * a + k], recv_sem=recv_sems.at[7 * a + k], device_id=to, device_id_type=MESH)

    def mine(a):
        return pltpu.make_async_copy(src(a), blk(a, *me), local_sems.at[a])

    def first(a):
        return ([copy(a, 0, me, sibling, source=src(a))]
                + [copy(a, 1 + j, me, (*chip, c), source=src(a)) for j, chip in enumerate(chips)])

    def begin():
        for a in range(n):
            mine(a).start()
        for a in range(n):
            for cp in first(a):
                cp.start()

    def finish():
        onward = []
        for j, chip in enumerate(chips):
            for a in range(n):
                copy(a, 1 + j, (*chip, c), me).wait_recv()
                onward.append(copy(a, 4 + j, (*chip, c), sibling))
                onward[-1].start()
        for a in range(n):
            copy(a, 0, sibling, me).wait_recv()
        for j, chip in enumerate(chips):
            for a in range(n):
                copy(a, 4 + j, (*chip, 1 - c), me).wait_recv()
        for a in range(n):
            for cp in first(a):
                cp.wait_send()
        for cp in onward:
            cp.wait_send()
        for a in range(n):
            mine(a).wait()

    return begin, finish


def _gather_operands(shards):
    n = len(shards)
    shapes = [jax.ShapeDtypeStruct((N_DEV, a.shape[0] // 2, a.shape[1]), a.dtype) for a in shards]
    sems = [pltpu.SemaphoreType.DMA((7 * n,)), pltpu.SemaphoreType.DMA((7 * n,)), pltpu.SemaphoreType.DMA((n,))]
    return shapes, sems


def _as_chip_slabs(gathered, shards):
    return [o.reshape(N_CHIP, a.shape[0], a.shape[1]) for o, a in zip(gathered, shards)]


def _gather_alongside(body, n_in, n_out, n_shards, last_step):
    def wrapped(*refs):
        ins, shards = refs[:n_in], refs[n_in:n_in + n_shards]
        rest = refs[n_in + n_shards:]
        outs, gathered = rest[:n_out], rest[n_out:n_out + n_shards]
        scratch, sems = rest[n_out + n_shards:-3], rest[-3:]

        @pl.when(pl.program_id(0) == 0)
        def _():
            _gather_plan(shards, gathered, *sems)[0]()

        body(*ins, *outs, *scratch)

        @pl.when(pl.program_id(0) == last_step)
        def _():
            _gather_plan(shards, gathered, *sems)[1]()

    return wrapped


def _half(ref, axis, which, ndim):
    size = ref.shape[axis] // 2
    idx = [slice(None)] * ndim
    idx[axis] = pl.ds(pl.multiple_of(which * size, 8 if axis == ndim - 2 else LANE), size)
    return ref.at[tuple(idx)]


def _swap_halves_with_sibling(fulls, name, axes):
    n = len(fulls)

    def body(*refs):
        f_refs, got_refs = refs[:n], refs[n:2 * n]
        send_sems, recv_sems = refs[2 * n:]
        x, y, c = _coords()
        copies = []
        for a in range(n):
            copies.append(pltpu.make_async_remote_copy(
                src_ref=_half(f_refs[a], axes[a], 1 - c, 3), dst_ref=got_refs[a], send_sem=send_sems.at[a],
                recv_sem=recv_sems.at[a], device_id=(x, y, 1 - c), device_id_type=MESH))
        for cp in copies:
            cp.start()
        for cp in copies:
            cp.wait()

    def halved(a, axis):
        shape = list(a.shape)
        shape[axis] //= 2
        return jax.ShapeDtypeStruct(tuple(shape), a.dtype)

    return pl.pallas_call(
        body, out_shape=[halved(a, ax) for a, ax in zip(fulls, axes)],
        in_specs=[HBM_REF] * n, out_specs=[HBM_REF] * n,
        scratch_shapes=[pltpu.SemaphoreType.DMA((n,)), pltpu.SemaphoreType.DMA((n,))],
        name=name)(*fulls)


def _join_halves_with_sibling(wholes, axes, block):
    n = len(wholes)
    twice = jnp.concatenate([block, block], axis=0)
    gathered_shapes, gather_sems = _gather_operands([twice])

    def body(*refs):
        out_refs = refs[n + 1:2 * n + 1]
        send_sems, recv_sems = refs[2 * n + 2:2 * n + 4]
        begin, finish = _gather_plan([refs[n]], [refs[2 * n + 1]], *refs[2 * n + 4:])
        x, y, c = _coords()

        def push(a, core):
            half = _half(out_refs[a], axes[a] - 1, core, 2)
            return pltpu.make_async_remote_copy(
                src_ref=half, dst_ref=half, send_sem=send_sems.at[a], recv_sem=recv_sems.at[a],
                device_id=(x, y, 1 - c), device_id_type=MESH)

        begin()
        for a in range(n):
            push(a, c).start()
        finish()
        for a in range(n):
            push(a, 1 - c).wait_recv()
        for a in range(n):
            push(a, c).wait_send()

    outs = pl.pallas_call(
        body, out_shape=[jax.ShapeDtypeStruct(a.shape, a.dtype) for a in wholes] + gathered_shapes,
        in_specs=[HBM_REF] * (n + 1), out_specs=[HBM_REF] * (n + 1), input_output_aliases={a: a for a in range(n)},
        scratch_shapes=[pltpu.SemaphoreType.DMA((n,)), pltpu.SemaphoreType.DMA((n,))] + gather_sems,
        name="rs_pair_join")(*wholes, twice)
    return outs[:n], outs[n]


def _cols_to_slabs(g):
    rows, cols = g.shape
    return g.reshape(rows, N_CHIP, cols // N_CHIP).transpose(1, 0, 2)


def _slabs_to_cols(w):
    n, rows, cols = w.shape
    return w.transpose(1, 0, 2).reshape(rows, n * cols)


def _col_window(slabs, start, stop):
    n = slabs.shape[2]
    pieces = []
    for k in range(N_CHIP):
        lo, hi = max(start, k * n), min(stop, (k + 1) * n)
        if lo < hi:
            pieces.append(slabs[k][:, lo - k * n:hi - k * n])
    return pieces[0] if len(pieces) == 1 else jnp.concatenate(pieces, axis=1)


def _slabs_from_groups(groups, n):
    slabs = []
    for k in range(N_CHIP):
        pieces, off = [], 0
        for g in groups:
            lo, hi = max(k * n, off), min((k + 1) * n, off + g.shape[0])
            if lo < hi:
                pieces.append(g[lo - off:hi - off])
            off += g.shape[0]
        slabs.append(pieces[0] if len(pieces) == 1 else jnp.concatenate(pieces, axis=0))
    return jnp.stack(slabs)


def _uq_to_padded(w_uq):
    per = w_uq.reshape(RQ, H, DN + DR)
    nope = per[:, :, :DN].reshape(RQ, H * DN)
    rope = jnp.pad(per[:, :, DN:], ((0, 0), (0, 0), (0, LANE - DR))).reshape(RQ, H * LANE)
    return jnp.concatenate([nope, rope], axis=1)


def _uq_from_padded(g):
    nope = g[:, :H * DN].reshape(RQ, H, DN)
    rope = g[:, H * DN:].reshape(RQ, H, LANE)[:, :, :DR]
    return jnp.concatenate([nope, rope], axis=2).reshape(RQ, H * (DN + DR))


def _rope_tables(positions):
    inv_freq = ROPE_THETA ** (-jnp.arange(0, DR, 2, dtype=F32) / DR)
    ang = positions.astype(F32)[:, None] * inv_freq
    cos, sin = jnp.cos(ang), jnp.sin(ang)
    return jnp.tile(cos, (1, 4)), jnp.tile(jnp.concatenate([-sin, sin], axis=1), (1, 2))


def _pair_sums(fulls, core, tag, axes, tr):
    from_sibling = _swap_halves_with_sibling(fulls, f"rs_pair_swap_{tag}", axes)
    return [_add_own_half(f, o, core, min(tr, o.shape[1]), f"add_own_half_{tag}{n}", ax)
            for n, (f, o, ax) in enumerate(zip(fulls, from_sibling, axes))]


def _local_step(x, tgt, cos_t, sin_t, mod, weights, small, tiles, place):
    ts, ts_in, tm_nn, tm_tn, t_attn, chunk = tiles
    w_in_shard, later_shards, conv_w = weights
    norm_w, conv_b, ln_w, ln_b, q_norm_w, kv_norm_w, fnw = small
    shift, scale, gate = mod[:, 0:D], mod[:, D:2 * D], mod[:, 2 * D:3 * D]

    h, (g_in,) = _adaln_norm(x, norm_w, shift, scale, ts, [w_in_shard])
    wa = _col_window(g_in, 0, A_COLS)
    wl = jnp.pad(_col_window(g_in, A_COLS, A_COLS + L_COLS_RAW), ((0, 0), (0, L_COLS - L_COLS_RAW)))
    wg = _col_window(g_in, A_COLS + L_COLS_RAW, IN_COLS)
    proj_a = _mm_nn(h, wa, tm_nn, D, "proj_a")
    u0, u1, za, (g_uq, g_ukv, g_co, g_ao, g_o) = _conv_fwd(proj_a, conv_w, conv_b, ln_w, ln_b, ts, chunk, later_shards)
    w_uq2, w_ukv = _uq_to_padded(_slabs_to_cols(g_uq)), _slabs_to_cols(g_ukv)
    wco, wao, wo = g_co.reshape(D, D), g_ao.reshape(D, D), g_o.reshape(D, D)
    proj_l = _mm_nn(h, wl, tm_nn, L_COLS, "proj_l")
    proj_g = _mm_nn(h, wg, tm_nn, D, "proj_g")
    qn, kvn, q, k, v = _mla_prep(proj_l, q_norm_w, kv_norm_w, w_uq2, w_ukv, cos_t, sin_t, ts)
    o, lse = _attn_fwd(q, k, v, t_attn)
    (dx2, dza, do, delta, dpg, zb, mg, dmo, dya, dyb, vec_mid) = _middle(
        za, o, proj_g, x, tgt, gate, fnw, wco, wao, wo, ts)
    g_wo = _mm_tn(mg, dmo, tm_tn, D, D, "grad_w_out", BF16)
    g_wco = _mm_tn(za, dya, tm_tn, D, D, "grad_w_conv_out", BF16)
    g_wao = _mm_tn(zb, dyb, tm_tn, D, D, "grad_w_attn_out", BF16)
    dq, dk, dv = _attn_bwd(q, k, v, do, lse, delta, t_attn)
    dpl, g_wuq2, g_wukv, vec_mla = _mla_prep_bwd(
        dq, dk, dv, proj_l, qn, kvn, q_norm_w, kv_norm_w, w_uq2, w_ukv, cos_t, sin_t, ts)

    core = place[1:2]
    nr = D // N_CHIP
    early = [_cols_to_slabs(_uq_from_padded(g_wuq2)).astype(BF16), _cols_to_slabs(g_wukv).astype(BF16),
             g_wco.reshape(N_CHIP, nr, D), g_wao.reshape(N_CHIP, nr, D), g_wo.reshape(N_CHIP, nr, D)]
    dpa, g_conv_w, vec_conv, early_got = _conv_bwd(dza, proj_a, u0, u1, conv_w, ln_w, ln_b, ts, chunk, early)

    g_wa_t = _mm_tn(dpa, h, tm_tn, D, D, "grad_w_in_a")
    g_wl_t = _mm_tn(dpl, h, tm_tn, L_COLS, D, "grad_w_in_l")
    g_wg_t = _mm_tn(dpg, h, tm_tn, D, D, "grad_w_in_g")
    g_w_in_slabs = _slabs_from_groups([g_wa_t, g_wl_t[0:L_COLS_RAW], g_wg_t], IN_COLS // N_CHIP)
    late_sums = _pair_sums([g_w_in_slabs], core, "b", [2], W_IN_ROWS)
    grad_x, vec_in, late_got = _input_bwd(dpa, dpl, dpg, wa, wl, wg, x, dx2, norm_w, scale, ts_in, late_sums)

    col_sums = jnp.concatenate(
        [vec_in, vec_mid, vec_conv, jnp.pad(vec_mla, ((0, 0), (0, D - RQ))), g_conv_w], axis=0)
    wholes = ([_sum_chip_slabs(late_got[0], late_sums[0], place, W_IN_ROWS, "sum_chip_slabs_w_in", 2)]
              + list(_sum_device_partials(early_got, early, place)))
    shards, all_col_sums = _join_halves_with_sibling(wholes, [2] + [1] * len(early), col_sums)

    return grad_x, shards, all_col_sums


def kernel(x, c, positions, w_ada, b_ada, norm_w, w_in, conv_w, conv_b, conv_ln_w, conv_ln_b, w_conv_out, q_norm_w, w_uq, kv_norm_w, w_ukv, w_attn_out, w_out, final_norm_w, loss_target, m_w_ada, m_b_ada, m_norm_w, m_w_in, m_conv_w, m_conv_b, m_conv_ln_w, m_conv_ln_b, m_w_conv_out, m_q_norm_w, m_w_uq, m_kv_norm_w, m_w_ukv, m_w_attn_out, m_w_out, m_final_norm_w, v_w_ada, v_b_ada, v_norm_w, v_w_in, v_conv_w, v_conv_b, v_conv_ln_w, v_conv_ln_b, v_w_conv_out, v_q_norm_w, v_w_uq, v_kv_norm_w, v_w_ukv, v_w_attn_out, v_w_out, v_final_norm_w):
    ix, iy, ic = _coords()
    chip = 2 * ix + iy
    dev = 4 * ix + 2 * iy + ic
    s = x.shape[1]
    tiles = (256, 512, 1024, 2048, 512, 32)

    conv_w_pad = jnp.pad(conv_w[0], ((0, HALO - KC), (0, 0)))
    small_in = jnp.concatenate([c.reshape(8, LANE), conv_w_pad.reshape(64, LANE)], axis=0)
    small_all = _allgather8(small_in, 72, True, "gather_c_conv").reshape(N_DEV, 72, LANE)
    c_all = small_all[:, 0:8].reshape(N_DEV, D)
    conv_full = jnp.concatenate(
        [small_all[2 * k, 8:72].reshape(HALO, D // N_CHIP) for k in range(N_CHIP)], axis=1)

    later_shards = [w[0].astype(BF16) for w in (w_uq, w_ukv, w_conv_out, w_attn_out, w_out)]
    weights = (w_in[0].astype(BF16), later_shards, conv_full)

    ada_cols = w_ada.shape[2]
    b_shard = lax.dynamic_slice(b_ada, (0, chip * ada_cols), (1, ada_cols))
    mod_part = _ada_fwd(c_all, w_ada[0], b_shard)
    mod_all = _allgather8(mod_part, N_DEV, True, "gather_mod").reshape(N_DEV, N_DEV, ada_cols)
    mod = jnp.concatenate(
        [lax.dynamic_slice(mod_all[2 * k], (dev, 0), (1, ada_cols)) for k in range(N_CHIP)], axis=1)

    cos_t, sin_t = _rope_tables(positions[0])
    small = (norm_w, conv_b, conv_ln_w, conv_ln_b, q_norm_w, kv_norm_w, final_norm_w.reshape(1, D))
    place = jnp.stack([chip, ic]).astype(jnp.int32)
    grad_x, shards, gathered = _local_step(x[0], loss_target[0], cos_t, sin_t, mod, weights, small, tiles, place)
    g_w_in_s, g_w_uq_s, g_w_ukv_s, g_wco_s, g_wao_s, g_wo_s = shards

    vec_names = ("b_ada", "norm_w", "conv_b", "conv_ln_w", "conv_ln_b", "q_norm_w", "kv_norm_w", "final_norm_w")
    row = lambda a: a.reshape(1, -1)
    vectors = [(row(b_ada), row(m_b_ada), row(v_b_ada)), (norm_w, m_norm_w, v_norm_w), (conv_b, m_conv_b, v_conv_b),
               (conv_ln_w, m_conv_ln_w, v_conv_ln_w), (conv_ln_b, m_conv_ln_b, v_conv_ln_b),
               (q_norm_w, m_q_norm_w, v_q_norm_w), (kv_norm_w, m_kv_norm_w, v_kv_norm_w),
               (row(final_norm_w), row(m_final_norm_w), row(v_final_norm_w))]
    fin = _small_finalize(gathered, vectors, (conv_w, m_conv_w, v_conv_w), place[0:1])
    res = {}
    for p, (name, (w, _, _)) in enumerate(zip(vec_names, vectors)):
        shape = final_norm_w.shape if name == "final_norm_w" else w.shape
        res[name] = tuple(a.reshape(shape) for a in fin[4 * p:4 * p + 4])
    res["conv_w"] = tuple(fin[4 * len(vectors):4 * len(vectors) + 4])
    dmod_all, loss = fin[-2], fin[-1].reshape(())
    dmod_shard = lax.dynamic_slice(dmod_all, (0, chip * ada_cols), (N_DEV, ada_cols))
    g_w_ada = _ada_bwd(c_all.T, dmod_shard).reshape(1, D, ada_cols)

    def big(w, g, m, v, tr, name):
        d, nm, nv = _adamw(w, g, m, v, tr, name)
        return g.reshape(w.shape), d, nm, nv

    res["w_ada"] = big(w_ada, g_w_ada[0], m_w_ada, v_w_ada, 256, "adamw_w_ada")
    t_in = [a[0].T for a in (w_in, m_w_in, v_w_in)]
    d_t, nm_t, nv_t = _adamw(t_in[0], g_w_in_s, t_in[1], t_in[2], W_IN_ROWS, "adamw_w_in")
    res["w_in"] = tuple(a.T[None] for a in (g_w_in_s, d_t, nm_t, nv_t))
    small = _adamw_many(
        [w_uq, w_ukv, w_conv_out, w_attn_out, w_out], [g_w_uq_s, g_w_ukv_s, g_wco_s, g_wao_s, g_wo_s],
        [m_w_uq, m_w_ukv, m_w_conv_out, m_w_attn_out, m_w_out], [v_w_uq, v_w_ukv, v_w_conv_out, v_w_attn_out, v_w_out],
        128)
    res["w_uq"], res["w_ukv"], res["w_conv_out"], res["w_attn_out"], res["w_out"] = small

    order = ("w_ada", "b_ada", "norm_w", "w_in", "conv_w", "conv_b", "conv_ln_w", "conv_ln_b", "w_conv_out",
             "q_norm_w", "w_uq", "kv_norm_w", "w_ukv", "w_attn_out", "w_out", "final_norm_w")
    outs = [loss, grad_x[None]]
    for slot in range(4):
        outs += [res[name][slot] for name in order]
    return tuple(outs)
```

```python
import functools

import numpy as np
import jax
import jax.numpy as jnp
from jax import lax
from jax.experimental import pallas as pl
from jax.experimental.pallas import tpu as pltpu

F32 = jnp.float32
BF16 = jnp.bfloat16
MESH = pl.DeviceIdType.MESH

D = 1024
H = 8
DN = 128
DR = 64
RQ = 256
KC = 31
HALO = 32
EPS = 1e-6
ROPE_THETA = 10000.0
N_CHIP = 4
N_DEV = 8
LANE = 128
VMEM_BIG = 56 * 1024 * 1024

ADAM_LR = 0.001
ADAM_B1 = 0.9
ADAM_B2 = 0.999
ADAM_EPS = 1e-08
ADAM_WD = 0.01
ADAM_STEP = 10

A_COLS = 3 * D
L_COLS_RAW = RQ + RQ + DR
L_COLS = 640
G_COLS = 3 * D
IN_COLS = A_COLS + L_COLS_RAW + G_COLS


def _params(sem=None, vmem=None):
    kw = {}
    if sem is not None:
        kw["dimension_semantics"] = sem
    if vmem is not None:
        kw["vmem_limit_bytes"] = vmem
    return pltpu.CompilerParams(**kw)


def _dot(a, b):
    return jnp.dot(a, b, preferred_element_type=F32)


def _dot_nt(a, b):
    return lax.dot_general(a, b, (((1,), (1,)), ((), ())), preferred_element_type=F32)


def _dot_tn(a, b):
    return lax.dot_general(a, b, (((0,), (0,)), ((), ())), preferred_element_type=F32)


def _colsum(v):
    return jnp.sum(v, axis=0, keepdims=True)


def _rowmean(v):
    return jnp.mean(v, axis=-1, keepdims=True)


def _sigmoid(v):
    return jax.nn.sigmoid(v)


def _dsilu(v, s):
    return s * (1.0 + v * (1.0 - s))


def _swap_halves(v, first_half):
    return jnp.where(first_half, pltpu.roll(v, 96, 1), pltpu.roll(v, 32, 1))


def _first_half_mask(rows):
    lane = lax.broadcasted_iota(jnp.int32, (rows, LANE), 1)
    return (lane % 64) < 32


def _adaln_norm(x, norm_w, shift, scale, ts, shards):
    s = x.shape[0]

    def body(x_ref, nw_ref, sh_ref, sc_ref, h_ref):
        xv = x_ref[...]
        r = lax.rsqrt(_rowmean(xv * xv) + EPS)
        y = xv * r * nw_ref[...]
        h_ref[...] = (y * (1.0 + sc_ref[...]) + sh_ref[...]).astype(BF16)

    row = pl.BlockSpec((ts, D), lambda i: (i, 0))
    vec = pl.BlockSpec((1, D), lambda i: (0, 0))
    n = len(shards)
    gathered_shapes, sems = _gather_operands(shards)
    outs = pl.pallas_call(
        _gather_alongside(body, 4, 1, n, s // ts - 1), grid=(s // ts,),
        in_specs=[row, vec, vec, vec] + [HBM_REF] * n, out_specs=[row] + [HBM_REF] * n,
        out_shape=[jax.ShapeDtypeStruct((s, D), BF16)] + gathered_shapes, scratch_shapes=sems, name="adaln_norm",
        compiler_params=_params(("arbitrary",)))(x, norm_w, shift, scale, *shards)
    return outs[0], _as_chip_slabs(outs[1:], shards)


def _mm_nn(a, b, tm, tn, name):
    m, k = a.shape
    n = b.shape[1]

    def body(a_ref, b_ref, o_ref):
        o_ref[...] = _dot(a_ref[...], b_ref[...])

    return pl.pallas_call(
        body, grid=(n // tn, m // tm),
        in_specs=[pl.BlockSpec((tm, k), lambda j, i: (i, 0)), pl.BlockSpec((k, tn), lambda j, i: (0, j))],
        out_specs=pl.BlockSpec((tm, tn), lambda j, i: (i, j)),
        out_shape=jax.ShapeDtypeStruct((m, n), F32), name=name,
        compiler_params=_params(("parallel", "parallel"), VMEM_BIG))(a, b)


def _mm_tn(a, b, tm, tk, tn, name, out_dtype=F32):
    m, k = a.shape
    n = b.shape[1]
    steps = m // tm

    def body(a_ref, b_ref, o_ref, acc_ref):
        @pl.when(pl.program_id(2) == 0)
        def _():
            acc_ref[...] = jnp.zeros_like(acc_ref)
        acc_ref[...] += _dot_tn(a_ref[...], b_ref[...])

        @pl.when(pl.program_id(2) == steps - 1)
        def _():
            o_ref[...] = acc_ref[...].astype(out_dtype)

    return pl.pallas_call(
        body, grid=(k // tk, n // tn, steps),
        in_specs=[pl.BlockSpec((tm, tk), lambda r, j, i: (i, r)), pl.BlockSpec((tm, tn), lambda r, j, i: (i, j))],
        out_specs=pl.BlockSpec((tk, tn), lambda r, j, i: (r, j)),
        out_shape=jax.ShapeDtypeStruct((k, n), out_dtype), scratch_shapes=[pltpu.VMEM((tk, tn), F32)], name=name,
        compiler_params=_params(("parallel", "parallel", "arbitrary"), VMEM_BIG))(a, b)


def _coords():
    return lax.axis_index("x"), lax.axis_index("y"), lax.axis_index("c")


HBM_REF = pl.BlockSpec(memory_space=pl.ANY)


def _chip_scatter_copies(p_refs, got_refs, send_sems, recv_sems):
    x, y, c = _coords()
    copies = []
    for a in range(len(p_refs)):
        for j, (px, py) in enumerate([(1 - x, y), (x, 1 - y), (1 - x, 1 - y)]):
            copies.append(pltpu.make_async_remote_copy(
                src_ref=p_refs[a].at[2 * px + py], dst_ref=got_refs[a].at[j], send_sem=send_sems.at[3 * a + j],
                recv_sem=recv_sems.at[3 * a + j], device_id=(px, py, c), device_id_type=MESH))
    return copies


RELATIONS = [(dx, dy, dc) for dx in (0, 1) for dy in (0, 1) for dc in (0, 1)][1:]


def _device_scatter_copies(p_refs, got_refs, send_sems, recv_sems):
    x, y, c = _coords()
    copies = []
    for a in range(len(p_refs)):
        half = p_refs[a].shape[1] // 2
        for j, (dx, dy, dc) in enumerate(RELATIONS):
            px, py, pc = (1 - x if dx else x), (1 - y if dy else y), (1 - c if dc else c)
            src = p_refs[a].at[2 * px + py, pl.ds(pl.multiple_of(pc * half, 16), half), :]
            copies.append(pltpu.make_async_remote_copy(
                src_ref=src, dst_ref=got_refs[a].at[j], send_sem=send_sems.at[7 * a + j],
                recv_sem=recv_sems.at[7 * a + j], device_id=(px, py, pc), device_id_type=MESH))
    return copies


def _scatter_alongside(body, n_in, n_out, n_parts, last_step, make_copies):
    def wrapped(*refs):
        ins, parts = refs[:n_in], refs[n_in:n_in + n_parts]
        rest = refs[n_in + n_parts:]
        outs, got = rest[:n_out], rest[n_out:n_out + n_parts]
        scratch, (send_sems, recv_sems) = rest[n_out + n_parts:-2], rest[-2:]

        @pl.when(pl.program_id(0) == 0)
        def _():
            for cp in make_copies(parts, got, send_sems, recv_sems):
                cp.start()

        body(*ins, *outs, *scratch)

        @pl.when(pl.program_id(0) == last_step)
        def _():
            for cp in make_copies(parts, got, send_sems, recv_sems):
                cp.wait()

    return wrapped


def _scatter_operands(parts, per_device):
    n = len(parts)
    if per_device:
        slots, shapes = 7, [jax.ShapeDtypeStruct((7, a.shape[1] // 2, a.shape[2]), a.dtype) for a in parts]
    else:
        slots, shapes = 3, [jax.ShapeDtypeStruct((3,) + a.shape[1:], a.dtype) for a in parts]
    sems = [pltpu.SemaphoreType.DMA((slots * n,)), pltpu.SemaphoreType.DMA((slots * n,))]
    return [HBM_REF] * n, [HBM_REF] * n, shapes, sems


def _shifted_copies(win_ref, sh_ref, rows):
    for p in range(1, 8):
        sh_ref[p - 1, 0:rows, :] = win_ref[pl.ds(p, rows), :]


def _tap_rows(win_ref, sh_ref, start, rows):
    p = start % 8
    if p == 0:
        return win_ref[pl.ds(start, rows), :]
    return sh_ref[p - 1, pl.ds(start - p, rows), :]


def _conv_taps(win_ref, sh_ref, w_ref, rows, chunk, offset_of_tap):
    pieces = []
    for c0 in range(0, rows, chunk):
        acc = None
        for j in range(KC):
            term = w_ref[j:j + 1, :] * _tap_rows(win_ref, sh_ref, c0 + offset_of_tap(j), chunk)
            acc = term if acc is None else acc + term
        pieces.append(acc)
    return pieces


def _conv_fwd(proj_a, conv_w, conv_b, ln_w, ln_b, ts, chunk, shards):
    s = proj_a.shape[0]

    def body(av_ref, al_ref, ag_ref, w_ref, b_ref, lw_ref, lb_ref, u0_ref, u1_ref, za_ref, win_ref, sh_ref):
        @pl.when(pl.program_id(0) == 0)
        def _():
            win_ref[0:HALO, :] = jnp.zeros((HALO, D), F32)

        u0 = av_ref[...] * _sigmoid(al_ref[...])
        u0_ref[...] = u0
        win_ref[HALO:HALO + ts, :] = u0
        _shifted_copies(win_ref, sh_ref, ts + HALO - 8)
        pieces = _conv_taps(win_ref, sh_ref, w_ref, ts, chunk, lambda j: HALO - (KC - 1) + j)
        for n, acc in enumerate(pieces):
            u1_ref[n * chunk:(n + 1) * chunk, :] = acc + b_ref[...]
        win_ref[0:HALO, :] = win_ref[ts:ts + HALO, :]

        u1 = u1_ref[...]
        xc = u1 - _rowmean(u1)
        rstd = lax.rsqrt(_rowmean(xc * xc) + EPS)
        u2 = xc * rstd * lw_ref[...] + lb_ref[...]
        u3 = u2 * _sigmoid(u2)
        ag = ag_ref[...]
        za_ref[...] = (u3 * (ag * _sigmoid(ag))).astype(BF16)

    col = lambda c: pl.BlockSpec((ts, D), lambda i, c=c: (i, c))
    row = pl.BlockSpec((ts, D), lambda i: (i, 0))
    vec = pl.BlockSpec((1, D), lambda i: (0, 0))
    n = len(shards)
    gathered_shapes, sems = _gather_operands(shards)
    outs = pl.pallas_call(
        _gather_alongside(body, 7, 3, n, s // ts - 1), grid=(s // ts,),
        in_specs=[col(0), col(1), col(2), pl.BlockSpec((HALO, D), lambda i: (0, 0)), vec, vec, vec] + [HBM_REF] * n,
        out_specs=[row, row, row] + [HBM_REF] * n,
        out_shape=[jax.ShapeDtypeStruct((s, D), F32), jax.ShapeDtypeStruct((s, D), F32),
                   jax.ShapeDtypeStruct((s, D), BF16)] + gathered_shapes,
        scratch_shapes=[pltpu.VMEM((ts + HALO, D), F32), pltpu.VMEM((7, ts + HALO, D), F32)] + sems,
        name="conv_fwd", compiler_params=_params(("arbitrary",), VMEM_BIG))(
            proj_a, proj_a, proj_a, conv_w, conv_b, ln_w, ln_b, *shards)
    return outs[0], outs[1], outs[2], _as_chip_slabs(outs[3:], shards)


def _conv_bwd(dza, proj_a, u0, u1, conv_w, ln_w, ln_b, ts, chunk, parts):
    s = dza.shape[0]
    nt = s // ts
    per = ts // HALO

    def body(dza_ref, av_ref, al_ref, ag_ref, u0_ref, u0p_ref, u1_ref, w_ref, lw_ref, lb_ref,
             dpa_ref, gw_ref, gv_ref, dwin_ref, uwin_ref, du0_ref, gwp_ref, dsh_ref, ush_ref):
        step = pl.program_id(0)
        tile = nt - 1 - step

        @pl.when(step == 0)
        def _():
            dwin_ref[ts:ts + HALO, :] = jnp.zeros((HALO, D), F32)
            gwp_ref[...] = jnp.zeros_like(gwp_ref)
            gv_ref[...] = jnp.zeros_like(gv_ref)

        ag = ag_ref[...]
        sg = _sigmoid(ag)
        u1 = u1_ref[...]
        xc = u1 - _rowmean(u1)
        rstd = lax.rsqrt(_rowmean(xc * xc) + EPS)
        xh = xc * rstd
        u2 = xh * lw_ref[...] + lb_ref[...]
        s2 = _sigmoid(u2)
        dz = dza_ref[...]
        du3 = dz * (ag * sg)
        dpa_ref[:, 2 * D:3 * D] = (dz * (u2 * s2) * _dsilu(ag, sg)).astype(BF16)
        du2 = du3 * _dsilu(u2, s2)
        gv_ref[0:1, :] += _colsum(du2 * xh)
        gv_ref[1:2, :] += _colsum(du2)
        dxh = du2 * lw_ref[...]
        du1 = rstd * (dxh - _rowmean(dxh) - xh * _rowmean(dxh * xh))
        gv_ref[2:3, :] += _colsum(du1)
        dwin_ref[0:ts, :] = du1

        uwin_ref[0:HALO, :] = jnp.where(tile == 0, 0.0, u0p_ref[...])
        uwin_ref[HALO:HALO + ts, :] = u0_ref[...]

        _shifted_copies(dwin_ref, dsh_ref, ts + HALO - 8)
        _shifted_copies(uwin_ref, ush_ref, ts + HALO - 8)
        pieces = _conv_taps(dwin_ref, dsh_ref, w_ref, ts, chunk, lambda j: (KC - 1) - j)
        for n, acc in enumerate(pieces):
            du0_ref[n * chunk:(n + 1) * chunk, :] = acc
        for c0 in range(0, ts, chunk):
            dchunk = dwin_ref[c0:c0 + chunk, :]
            for j in range(KC):
                prod = dchunk * _tap_rows(uwin_ref, ush_ref, c0 + HALO - (KC - 1) + j, chunk)
                gwp_ref[8 * j:8 * j + 8, :] += jnp.sum(prod.reshape(chunk // 8, 8, D), axis=0)
        dwin_ref[ts:ts + HALO, :] = dwin_ref[0:HALO, :]

        du0 = du0_ref[...]
        al = al_ref[...]
        sl = _sigmoid(al)
        dpa_ref[:, 0:D] = (du0 * sl).astype(BF16)
        dpa_ref[:, D:2 * D] = (du0 * av_ref[...] * sl * (1.0 - sl)).astype(BF16)

        @pl.when(step == nt - 1)
        def _():
            for j in range(KC):
                gw_ref[j:j + 1, :] = _colsum(gwp_ref[8 * j:8 * j + 8, :])
            gw_ref[KC:HALO, :] = jnp.zeros((HALO - KC, D), F32)

    rev = lambda i: nt - 1 - i
    col = lambda c: pl.BlockSpec((ts, D), lambda i, c=c: (rev(i), c))
    row = pl.BlockSpec((ts, D), lambda i: (rev(i), 0))
    vec = pl.BlockSpec((1, D), lambda i: (0, 0))
    halo = pl.BlockSpec((HALO, D), lambda i: (jnp.maximum(rev(i) * per - 1, 0), 0))
    side_in, side_out, side_shapes, side_sems = _scatter_operands(parts, True)
    outs = pl.pallas_call(
        _scatter_alongside(body, 10, 3, len(parts), nt - 1, _device_scatter_copies), grid=(nt,),
        in_specs=[row, col(0), col(1), col(2), row, halo, row, pl.BlockSpec((HALO, D), lambda i: (0, 0)), vec, vec]
        + side_in,
        out_specs=[pl.BlockSpec((ts, A_COLS), lambda i: (rev(i), 0)),
                   pl.BlockSpec((HALO, D), lambda i: (0, 0)), pl.BlockSpec((8, D), lambda i: (0, 0))] + side_out,
        out_shape=[jax.ShapeDtypeStruct((s, A_COLS), BF16), jax.ShapeDtypeStruct((HALO, D), F32),
                   jax.ShapeDtypeStruct((8, D), F32)] + side_shapes,
        scratch_shapes=[pltpu.VMEM((ts + HALO, D), F32), pltpu.VMEM((ts + HALO, D), F32),
                        pltpu.VMEM((ts, D), F32), pltpu.VMEM((8 * HALO, D), F32),
                        pltpu.VMEM((7, ts + HALO, D), F32), pltpu.VMEM((7, ts + HALO, D), F32)] + side_sems,
        name="conv_bwd", compiler_params=_params(("arbitrary",), VMEM_BIG))(
            dza, proj_a, proj_a, proj_a, u0, u0, u1, conv_w, ln_w, ln_b, *parts)
    return outs[0], outs[1], outs[2], list(outs[3:])


def _mla_prep(proj_l, q_norm_w, kv_norm_w, w_uq2, w_ukv, cos_t, sin_t, ts):
    s = proj_l.shape[0]

    def body(pl_ref, qw_ref, kw_ref, wq_ref, wkv_ref, c_ref, s_ref, qn_ref, kvn_ref, q_ref, k_ref, v_ref):
        first = _first_half_mask(ts)
        cs = c_ref[...]
        sn = s_ref[...]

        def rms(v, w):
            return v * lax.rsqrt(_rowmean(v * v) + EPS) * w

        def rope(v):
            return v * cs + _swap_halves(v, first) * sn

        qn = rms(pl_ref[:, 0:RQ], qw_ref[...]).astype(BF16)
        kvn = rms(pl_ref[:, RQ:2 * RQ], kw_ref[...]).astype(BF16)
        qn_ref[...] = qn
        kvn_ref[...] = kvn
        q = _dot(qn, wq_ref[...])
        kv = _dot(kvn, wkv_ref[...])
        kr = rope(pl_ref[:, 2 * RQ:2 * RQ + LANE]).astype(BF16)
        for h in range(H):
            q_ref[h, :, 0:DN] = q[:, DN * h:DN * (h + 1)].astype(BF16)
            q_ref[h, :, DN:2 * DN] = rope(q[:, H * DN + LANE * h:H * DN + LANE * (h + 1)]).astype(BF16)
            k_ref[h, :, 0:DN] = kv[:, 2 * DN * h:2 * DN * h + DN].astype(BF16)
            k_ref[h, :, DN:2 * DN] = kr
            v_ref[h, :, 0:DN] = kv[:, 2 * DN * h + DN:2 * DN * (h + 1)].astype(BF16)
            v_ref[h, :, DN:2 * DN] = jnp.ones((ts, DN), BF16)

    const = lambda shape: pl.BlockSpec(shape, lambda i: (0,) * len(shape))
    rowb = lambda w: pl.BlockSpec((ts, w), lambda i: (i, 0))
    head = lambda w: pl.BlockSpec((H, ts, w), lambda i: (0, i, 0))
    return pl.pallas_call(
        body, grid=(s // ts,),
        in_specs=[rowb(L_COLS), const((1, RQ)), const((1, RQ)), const((RQ, 2 * H * DN)), const((RQ, 2 * H * DN)),
                  rowb(LANE), rowb(LANE)],
        out_specs=[rowb(RQ), rowb(RQ), head(2 * DN), head(2 * DN), head(2 * DN)],
        out_shape=[jax.ShapeDtypeStruct((s, RQ), BF16), jax.ShapeDtypeStruct((s, RQ), BF16),
                   jax.ShapeDtypeStruct((H, s, 2 * DN), BF16), jax.ShapeDtypeStruct((H, s, 2 * DN), BF16),
                   jax.ShapeDtypeStruct((H, s, 2 * DN), BF16)],
        name="mla_prep", compiler_params=_params(("parallel",), VMEM_BIG))(
            proj_l, q_norm_w, kv_norm_w, w_uq2, w_ukv, cos_t, sin_t)


def _mla_prep_bwd(dq, dk, dv, proj_l, qn, kvn, q_norm_w, kv_norm_w, w_uq2, w_ukv, cos_t, sin_t, ts):
    s = proj_l.shape[0]

    def body(dq_ref, dk_ref, dv_ref, pl_ref, qn_ref, kvn_ref, qw_ref, kw_ref, wq_ref, wkv_ref, c_ref, s_ref,
             dpl_ref, gwq_ref, gwkv_ref, gv_ref, dq2_ref, dkv2_ref):
        @pl.when(pl.program_id(0) == 0)
        def _():
            gwq_ref[...] = jnp.zeros_like(gwq_ref)
            gwkv_ref[...] = jnp.zeros_like(gwkv_ref)
            gv_ref[...] = jnp.zeros_like(gv_ref)

        first = _first_half_mask(ts)
        cs = c_ref[...] * ATT_SCALE
        sn = s_ref[...] * ATT_SCALE

        def rope_bwd(g):
            return g * cs + _swap_halves(g * sn, first)

        def rms_bwd(v, w, dy):
            r = lax.rsqrt(_rowmean(v * v) + EPS)
            vh = v * r
            dvh = dy * w
            return r * (dvh - vh * _rowmean(dvh * vh)), _colsum(dy * vh)

        dkr = None
        for h in range(H):
            dq2_ref[:, DN * h:DN * (h + 1)] = (dq_ref[h, :, 0:DN] * ATT_SCALE).astype(BF16)
            dq2_ref[:, H * DN + LANE * h:H * DN + LANE * (h + 1)] = rope_bwd(dq_ref[h, :, DN:2 * DN]).astype(BF16)
            dkv2_ref[:, 2 * DN * h:2 * DN * h + DN] = (dk_ref[h, :, 0:DN] * ATT_SCALE).astype(BF16)
            dkv2_ref[:, 2 * DN * h + DN:2 * DN * (h + 1)] = dv_ref[h].astype(BF16)
            part = dk_ref[h, :, DN:2 * DN]
            dkr = part if dkr is None else dkr + part

        dq2 = dq2_ref[...]
        dkv2 = dkv2_ref[...]
        gwq_ref[...] += _dot_tn(qn_ref[...], dq2)
        gwkv_ref[...] += _dot_tn(kvn_ref[...], dkv2)
        dcq, gq = rms_bwd(pl_ref[:, 0:RQ], qw_ref[...], _dot_nt(dq2, wq_ref[...]))
        dckv, gkv = rms_bwd(pl_ref[:, RQ:2 * RQ], kw_ref[...], _dot_nt(dkv2, wkv_ref[...]))
        gv_ref[0:1, :] += gq
        gv_ref[1:2, :] += gkv
        dpl_ref[:, 0:RQ] = dcq.astype(BF16)
        dpl_ref[:, RQ:2 * RQ] = dckv.astype(BF16)
        dpl_ref[:, 2 * RQ:2 * RQ + LANE] = rope_bwd(dkr).astype(BF16)

    const = lambda shape: pl.BlockSpec(shape, lambda i: (0,) * len(shape))
    rowb = lambda w: pl.BlockSpec((ts, w), lambda i: (i, 0))
    head = lambda w: pl.BlockSpec((H, ts, w), lambda i: (0, i, 0))
    return pl.pallas_call(
        body, grid=(s // ts,),
        in_specs=[head(2 * DN), head(2 * DN), head(DN), rowb(L_COLS), rowb(RQ), rowb(RQ), const((1, RQ)),
                  const((1, RQ)), const((RQ, 2 * H * DN)), const((RQ, 2 * H * DN)), rowb(LANE), rowb(LANE)],
        out_specs=[rowb(L_COLS), const((RQ, 2 * H * DN)), const((RQ, 2 * H * DN)), const((8, RQ))],
        out_shape=[jax.ShapeDtypeStruct((s, L_COLS), BF16), jax.ShapeDtypeStruct((RQ, 2 * H * DN), F32),
                   jax.ShapeDtypeStruct((RQ, 2 * H * DN), F32), jax.ShapeDtypeStruct((8, RQ), F32)],
        scratch_shapes=[pltpu.VMEM((ts, 2 * H * DN), BF16), pltpu.VMEM((ts, 2 * H * DN), BF16)],
        name="mla_prep_bwd", compiler_params=_params(("arbitrary",), VMEM_BIG))(
            dq, dk, dv, proj_l, qn, kvn, q_norm_w, kv_norm_w, w_uq2, w_ukv, cos_t, sin_t)


def _causal_pairs(n, by_key):
    if by_key:
        pairs = [(i, j) for j in range(n) for i in range(j, n)]
    else:
        pairs = [(i, j) for i in range(n) for j in range(i + 1)]
    return (jnp.asarray(np.array([p[0] for p in pairs], np.int32)),
            jnp.asarray(np.array([p[1] for p in pairs], np.int32)))


ATT_SCALE = float((DN + DR) ** -0.5)
LOG2E = 1.4426950408889634
LN2 = 0.6931471805599453
ATT_HEADS_FWD = 4
ATT_HEADS = 2
W_IN_ROWS = 336
ATT_ROWS = 64


def _diag_width(r0, t):
    return min(t, -(-(r0 + ATT_ROWS) // LANE) * LANE)


def _diag_mask_rows(r0, width):
    rows = r0 + lax.broadcasted_iota(jnp.int32, (ATT_ROWS, width), 0)
    cols = lax.broadcasted_iota(jnp.int32, (ATT_ROWS, width), 1)
    return cols <= rows


def _diag_mask(t):
    return lax.broadcasted_iota(jnp.int32, (t, t), 1) <= lax.broadcasted_iota(jnp.int32, (t, t), 0)


def _attn_fwd(q, k, v, t):
    s = q.shape[1]
    n = s // t
    scale2 = float((DN + DR) ** -0.5) * LOG2E
    qi, ki = _causal_pairs(n, by_key=False)

    def body(qi_ref, ki_ref, q_ref, k_ref, v_ref, o_ref, lse_ref, *scratch):
        per_head = [scratch[5 * h:5 * h + 5] for h in range(ATT_HEADS_FWD)]
        p = pl.program_id(1)
        i = qi_ref[p]
        j = ki_ref[p]

        @pl.when(j == 0)
        def _():
            for m_sc, acc_sc, _, _, _ in per_head:
                m_sc[...] = jnp.full_like(m_sc, -jnp.inf)
                acc_sc[...] = jnp.zeros_like(acc_sc)

        def scores(h, diag):
            sc = _dot_nt(q_ref[h], k_ref[h])
            if diag:
                sc = jnp.where(_diag_mask(t), sc, -jnp.inf)
            per_head[h][2][...] = sc

        def rowmax(h, rows):
            per_head[h][4][rows, :] = jnp.max(per_head[h][2][rows, :], axis=-1, keepdims=True)

        def stats(h):
            m_sc, acc_sc, _, _, mx_sc = per_head[h]
            m_prev = m_sc[...]
            m_new = jnp.maximum(m_prev, mx_sc[...] * scale2)
            m_sc[...] = m_new
            acc_sc[...] = jnp.exp2(m_prev - m_new) * acc_sc[...]

        def probs(h, rows):
            m_sc, _, s_sc, p_sc, _ = per_head[h]
            p_sc[rows, :] = jnp.exp2(s_sc[rows, :] * scale2 - m_sc[rows, :]).astype(BF16)

        def values(h):
            _, acc_sc, _, p_sc, _ = per_head[h]
            acc_sc[...] += _dot(p_sc[...], v_ref[h])

        def step(diag):
            blocks = [slice(r0, r0 + ATT_ROWS) for r0 in range(0, t, ATT_ROWS)]
            for h in range(ATT_HEADS_FWD):
                scores(h, diag)
            for rows in blocks:
                rowmax(0, rows)
            stats(0)
            for h in range(ATT_HEADS_FWD):
                for rows in blocks:
                    probs(h, rows)
                    if h + 1 < ATT_HEADS_FWD:
                        rowmax(h + 1, rows)
                if h + 1 < ATT_HEADS_FWD:
                    stats(h + 1)
                values(h)

        @pl.when(j < i)
        def _():
            step(False)

        @pl.when(j == i)
        def _():
            step(True)
            for h, (m_sc, acc_sc, _, _, _) in enumerate(per_head):
                l = acc_sc[:, DN:2 * DN]
                o_ref[:, DN * h:DN * (h + 1)] = acc_sc[:, 0:DN] / l
                lse_ref[h] = (m_sc[...] + jnp.log2(l[:, 0:1])) * LN2

    hb = ATT_HEADS_FWD
    grid_spec = pltpu.PrefetchScalarGridSpec(
        num_scalar_prefetch=2, grid=(H // hb, int(qi.shape[0])),
        in_specs=[pl.BlockSpec((hb, t, 2 * DN), lambda h, p, qi, ki: (h, qi[p], 0)),
                  pl.BlockSpec((hb, t, 2 * DN), lambda h, p, qi, ki: (h, ki[p], 0)),
                  pl.BlockSpec((hb, t, 2 * DN), lambda h, p, qi, ki: (h, ki[p], 0))],
        out_specs=[pl.BlockSpec((t, hb * DN), lambda h, p, qi, ki: (qi[p], h)),
                   pl.BlockSpec((hb, t, 1), lambda h, p, qi, ki: (h, qi[p], 0))],
        scratch_shapes=[pltpu.VMEM((t, 1), F32), pltpu.VMEM((t, 2 * DN), F32), pltpu.VMEM((t, t), F32),
                        pltpu.VMEM((t, t), BF16), pltpu.VMEM((t, 1), F32)] * hb)
    return pl.pallas_call(
        body, grid_spec=grid_spec,
        out_shape=[jax.ShapeDtypeStruct((s, H * DN), F32), jax.ShapeDtypeStruct((H, s, 1), F32)],
        name="attn_fwd", compiler_params=_params(("parallel", "arbitrary"), VMEM_BIG))(qi, ki, q, k, v)


def _attn_bwd(q, k, v, do, lse, delta, t):
    s = q.shape[1]
    n = s // t
    scale = ATT_SCALE
    qi, ki = _causal_pairs(n, by_key=True)

    def body(qi_ref, ki_ref, q_ref, k_ref, v_ref, do_ref, lse_ref, dl_ref, dq_ref, dk_ref, dv_ref,
             dk_sc, dv_sc, s_sc, dp_sc, p_sc, ds_sc):
        p = pl.program_id(1)
        i = qi_ref[p]
        j = ki_ref[p]

        @pl.when(p == 0)
        def _():
            dq_ref[...] = jnp.zeros_like(dq_ref)

        @pl.when(i == j)
        def _():
            dk_sc[...] = jnp.zeros_like(dk_sc)
            dv_sc[...] = jnp.zeros_like(dv_sc)

        def step(diag):
            for h in range(ATT_HEADS):
                s_sc[h] = _dot_nt(q_ref[h], k_ref[h])
                dp_sc[h] = _dot_nt(do_ref[:, DN * h:DN * (h + 1)], v_ref[h, :, 0:DN])
            for h in range(ATT_HEADS):
                for r0 in range(0, t, ATT_ROWS):
                    rows = slice(r0, r0 + ATT_ROWS)
                    width = _diag_width(r0, t) if diag else t
                    sc = s_sc[h, rows, 0:width] * (scale * LOG2E)
                    if diag:
                        sc = jnp.where(_diag_mask_rows(r0, width), sc, -jnp.inf)
                    pr = jnp.exp2(sc - lse_ref[h, rows, :] * LOG2E)
                    ds = pr * (dp_sc[h, rows, 0:width] - dl_ref[h, rows, :])
                    p_sc[h, rows, 0:width] = pr.astype(BF16)
                    ds_sc[h, rows, 0:width] = ds.astype(BF16)
                    if width < t:
                        p_sc[h, rows, width:t] = jnp.zeros((ATT_ROWS, t - width), BF16)
                        ds_sc[h, rows, width:t] = jnp.zeros((ATT_ROWS, t - width), BF16)
            q_rows = pl.ds(pl.multiple_of(i * t, t), t)
            for h in range(ATT_HEADS):
                dv_sc[h] += _dot_tn(p_sc[h], do_ref[:, DN * h:DN * (h + 1)])
                dk_sc[h] += _dot_tn(ds_sc[h], q_ref[h])
                dq_ref[h, q_rows, :] += _dot(ds_sc[h], k_ref[h])

        @pl.when(i > j)
        def _():
            step(False)

        @pl.when(i == j)
        def _():
            step(True)

        @pl.when(i == n - 1)
        def _():
            dk_ref[...] = dk_sc[...]
            dv_ref[...] = dv_sc[...]

    hb = ATT_HEADS
    grid_spec = pltpu.PrefetchScalarGridSpec(
        num_scalar_prefetch=2, grid=(H // hb, int(qi.shape[0])),
        in_specs=[pl.BlockSpec((hb, t, 2 * DN), lambda h, p, qi, ki: (h, qi[p], 0)),
                  pl.BlockSpec((hb, t, 2 * DN), lambda h, p, qi, ki: (h, ki[p], 0)),
                  pl.BlockSpec((hb, t, 2 * DN), lambda h, p, qi, ki: (h, ki[p], 0)),
                  pl.BlockSpec((t, hb * DN), lambda h, p, qi, ki: (qi[p], h)),
                  pl.BlockSpec((hb, t, 1), lambda h, p, qi, ki: (h, qi[p], 0)),
                  pl.BlockSpec((hb, t, 1), lambda h, p, qi, ki: (h, qi[p], 0))],
        out_specs=[pl.BlockSpec((hb, s, 2 * DN), lambda h, p, qi, ki: (h, 0, 0)),
                   pl.BlockSpec((hb, t, 2 * DN), lambda h, p, qi, ki: (h, ki[p], 0)),
                   pl.BlockSpec((hb, t, DN), lambda h, p, qi, ki: (h, ki[p], 0))],
        scratch_shapes=[pltpu.VMEM((hb, t, 2 * DN), F32), pltpu.VMEM((hb, t, DN), F32),
                        pltpu.VMEM((hb, t, t), F32), pltpu.VMEM((hb, t, t), F32),
                        pltpu.VMEM((hb, t, t), BF16), pltpu.VMEM((hb, t, t), BF16)])
    return pl.pallas_call(
        body, grid_spec=grid_spec,
        out_shape=[jax.ShapeDtypeStruct((H, s, 2 * DN), F32), jax.ShapeDtypeStruct((H, s, 2 * DN), F32),
                   jax.ShapeDtypeStruct((H, s, DN), F32)],
        name="attn_bwd", compiler_params=_params(("parallel", "arbitrary"), VMEM_BIG))(
            qi, ki, q, k, v, do, lse, delta)


def _middle(za, o, proj_g, x, tgt, gate, fnw, wco, wao, wo, ts):
    s = x.shape[0]
    inv_d = 1.0 / D

    def body(za_ref, o_ref, bg_ref, ga_ref, gb_ref, x_ref, t_ref, gate_ref, fnw_ref, wco_ref, wao_ref, wo_ref,
             dx2_ref, dza_ref, do_ref, dl_ref, dpg_ref, zb_ref, mg_ref, dmo_ref, dya_ref, dyb_ref, vec_ref):
        @pl.when(pl.program_id(0) == 0)
        def _():
            vec_ref[...] = jnp.zeros_like(vec_ref)

        ov = o_ref[...]
        bg = bg_ref[...]
        sb = _sigmoid(bg)
        silu_b = bg * sb
        zb = (ov * silu_b).astype(BF16)
        zb_ref[...] = zb
        ya = _dot(za_ref[...], wco_ref[...])
        yb = _dot(zb, wao_ref[...])
        sa = _sigmoid(ga_ref[...])
        sg = _sigmoid(gb_ref[...])
        mg = (sa * ya + sg * yb).astype(BF16)
        mg_ref[...] = mg
        mo = _dot(mg, wo_ref[...])
        gate_v = gate_ref[...]
        x2 = x_ref[...] + gate_v * mo
        r = lax.rsqrt(_rowmean(x2 * x2) + EPS)
        xh = x2 * r
        fw = fnw_ref[...]
        e = xh * fw - t_ref[...]
        vec_ref[2:3, :] += _colsum(e * e)
        dy = e * inv_d
        vec_ref[0:1, :] += _colsum(dy * xh)
        dxh = dy * fw
        dx2 = r * (dxh - xh * _rowmean(dxh * xh))
        dx2_ref[...] = dx2
        vec_ref[1:2, :] += _colsum(dx2 * mo)
        dmo = (gate_v * dx2).astype(BF16)
        dmo_ref[...] = dmo
        dmg = _dot_nt(dmo, wo_ref[...])
        dya = (sa * dmg).astype(BF16)
        dyb = (sg * dmg).astype(BF16)
        dya_ref[...] = dya
        dyb_ref[...] = dyb
        dpg_ref[:, D:2 * D] = (dmg * ya * (sa * (1.0 - sa))).astype(BF16)
        dpg_ref[:, 2 * D:3 * D] = (dmg * yb * (sg * (1.0 - sg))).astype(BF16)
        dza_ref[...] = _dot_nt(dya, wco_ref[...])
        dzb = _dot_nt(dyb, wao_ref[...])
        dov = dzb * silu_b
        do_ref[...] = dov.astype(BF16)
        dpg_ref[:, 0:D] = (dzb * ov * _dsilu(bg, sb)).astype(BF16)
        dprod = dov * ov
        for h in range(H):
            dl_ref[h] = jnp.sum(dprod[:, DN * h:DN * (h + 1)], axis=-1, keepdims=True)

    col = lambda c: pl.BlockSpec((ts, D), lambda i, c=c: (i, c))
    row = pl.BlockSpec((ts, D), lambda i: (i, 0))
    vec = pl.BlockSpec((1, D), lambda i: (0, 0))
    wsp = pl.BlockSpec((D, D), lambda i: (0, 0))
    bf = jax.ShapeDtypeStruct((s, D), BF16)
    ff = jax.ShapeDtypeStruct((s, D), F32)
    return pl.pallas_call(
        body, grid=(s // ts,),
        in_specs=[row, row, col(0), col(1), col(2), row, row, vec, vec, wsp, wsp, wsp],
        out_specs=[row, row, row, pl.BlockSpec((H, ts, 1), lambda i: (0, i, 0)),
                   pl.BlockSpec((ts, G_COLS), lambda i: (i, 0)), row, row, row, row, row,
                   pl.BlockSpec((8, D), lambda i: (0, 0))],
        out_shape=[ff, ff, bf, jax.ShapeDtypeStruct((H, s, 1), F32), jax.ShapeDtypeStruct((s, G_COLS), BF16),
                   bf, bf, bf, bf, bf, jax.ShapeDtypeStruct((8, D), F32)],
        name="middle", compiler_params=_params(("arbitrary",), VMEM_BIG))(
            za, o, proj_g, proj_g, proj_g, x, tgt, gate, fnw, wco, wao, wo)


def _input_bwd(dpa, dpl, dpg, wa, wl, wg, x, dx2, norm_w, scale, ts, parts):
    s = x.shape[0]

    def body(dpa_ref, dpl_ref, dpg_ref, wa_ref, wl_ref, wg_ref, x_ref, dx2_ref, nw_ref, sc_ref, gx_ref, gv_ref):
        @pl.when(pl.program_id(0) == 0)
        def _():
            gv_ref[...] = jnp.zeros_like(gv_ref)

        dh = (_dot_nt(dpa_ref[...], wa_ref[...]) + _dot_nt(dpl_ref[...], wl_ref[...])
              + _dot_nt(dpg_ref[...], wg_ref[...]))
        xv = x_ref[...]
        r = lax.rsqrt(_rowmean(xv * xv) + EPS)
        xh = xv * r
        nw = nw_ref[...]
        gv_ref[0:1, :] += _colsum(dh)
        gv_ref[1:2, :] += _colsum(dh * (xh * nw))
        dy = dh * (1.0 + sc_ref[...])
        gv_ref[2:3, :] += _colsum(dy * xh)
        dxh = dy * nw
        gx_ref[...] = dx2_ref[...] + r * (dxh - xh * _rowmean(dxh * xh))

    const = lambda shape: pl.BlockSpec(shape, lambda i: (0, 0))
    rowb = lambda w: pl.BlockSpec((ts, w), lambda i: (i, 0))
    side_in, side_out, side_shapes, side_sems = _scatter_operands(parts, False)
    outs = pl.pallas_call(
        _scatter_alongside(body, 10, 2, len(parts), s // ts - 1, _chip_scatter_copies), grid=(s // ts,),
        in_specs=[rowb(A_COLS), rowb(L_COLS), rowb(G_COLS), const((D, A_COLS)), const((D, L_COLS)),
                  const((D, G_COLS)), rowb(D), rowb(D), const((1, D)), const((1, D))] + side_in,
        out_specs=[rowb(D), const((8, D))] + side_out,
        out_shape=[jax.ShapeDtypeStruct((s, D), F32), jax.ShapeDtypeStruct((8, D), F32)] + side_shapes,
        scratch_shapes=side_sems,
        name="input_bwd", compiler_params=_params(("arbitrary",), VMEM_BIG))(
            dpa, dpl, dpg, wa, wl, wg, x, dx2, norm_w, scale, *parts)
    return outs[0], outs[1], list(outs[2:])


def _adamw_math(w, g, m, v):
    nm = ADAM_B1 * m + (1.0 - ADAM_B1) * g
    nv = ADAM_B2 * v + (1.0 - ADAM_B2) * (g * g)
    m_hat = nm / (1.0 - ADAM_B1 ** ADAM_STEP)
    v_hat = nv / (1.0 - ADAM_B2 ** ADAM_STEP)
    return -ADAM_LR * (m_hat / (jnp.sqrt(v_hat) + ADAM_EPS) + ADAM_WD * w), nm, nv


def _adamw(w, g, m, v, tr, name):
    lead, (rows, cols) = w.shape[:-2], w.shape[-2:]

    def body(w_ref, g_ref, m_ref, v_ref, d_ref, nm_ref, nv_ref):
        d_ref[...], nm_ref[...], nv_ref[...] = _adamw_math(w_ref[...], g_ref[...], m_ref[...], v_ref[...])

    blk = pl.BlockSpec((1,) * len(lead) + (tr, cols), lambda i: (0,) * len(lead) + (i, 0))
    shp = jax.ShapeDtypeStruct(w.shape, F32)
    return pl.pallas_call(
        body, grid=(rows // tr,), in_specs=[blk] * 4, out_specs=[blk] * 3, out_shape=[shp] * 3, name=name,
        compiler_params=_params(("parallel",), VMEM_BIG))(w, g.reshape(w.shape), m, v)


ROW_SHIFT, ROW_SCALE, ROW_NORM_W = 0, 1, 2
ROW_FINAL_NORM_W, ROW_GATE, ROW_LOSS = 8, 9, 10
ROW_LN_W, ROW_LN_B, ROW_CONV_B = 16, 17, 18
ROW_Q_NORM_W, ROW_KV_NORM_W = 24, 25
ROW_CONV_W = 32
SUM_ROWS = 64
VECTOR_ROWS = ((ROW_SHIFT, ROW_SCALE, ROW_GATE), (ROW_NORM_W,), (ROW_CONV_B,), (ROW_LN_W,), (ROW_LN_B,),
               (ROW_Q_NORM_W,), (ROW_KV_NORM_W,), (ROW_FINAL_NORM_W,))


def _small_finalize(gathered, vectors, conv, chip):
    n = len(vectors)
    cw = conv[0].shape[2]

    def body(chip_ref, g_ref, *refs):
        ins, outs = refs[:3 * n + 3], refs[3 * n + 3:]
        tot = g_ref[0]
        for k in range(1, N_DEV):
            tot = tot + g_ref[k]
        for p, rows in enumerate(VECTOR_ROWS):
            w_ref, m_ref, v_ref = ins[3 * p:3 * p + 3]
            g_out, d_out, nm_out, nv_out = outs[4 * p:4 * p + 4]
            width = w_ref.shape[1] // len(rows)
            for q, r in enumerate(rows):
                lanes = slice(q * width, (q + 1) * width)
                g = tot[r:r + 1, 0:width]
                g_out[:, lanes] = g
                d_out[:, lanes], nm_out[:, lanes], nv_out[:, lanes] = _adamw_math(
                    w_ref[:, lanes], g, m_ref[:, lanes], v_ref[:, lanes])
        cols = pl.ds(pl.multiple_of(chip_ref[0] * cw, LANE), cw)
        gc = g_ref[0, pl.ds(ROW_CONV_W, KC), cols]
        for k in range(1, N_DEV):
            gc = gc + g_ref[k, pl.ds(ROW_CONV_W, KC), cols]
        cw_ref, cm_ref, cv_ref = ins[3 * n:3 * n + 3]
        g_out, d_out, nm_out, nv_out, dmod_ref, loss_ref = outs[4 * n:]
        g_out[0] = gc
        d_out[0], nm_out[0], nv_out[0] = _adamw_math(cw_ref[0], gc, cm_ref[0], cv_ref[0])
        for k in range(N_DEV):
            for q, r in enumerate((ROW_SHIFT, ROW_SCALE, ROW_GATE)):
                dmod_ref[k:k + 1, q * D:(q + 1) * D] = g_ref[k, r:r + 1, :]
        loss_ref[...] = (0.5 / D) * jnp.sum(tot[ROW_LOSS:ROW_LOSS + 1, :], axis=-1, keepdims=True)

    flat_in = [a for triple in vectors for a in triple] + list(conv)
    shapes = [jax.ShapeDtypeStruct(w.shape, F32) for w, _, _ in vectors for _ in range(4)]
    shapes += [jax.ShapeDtypeStruct(conv[0].shape, F32)] * 4
    shapes += [jax.ShapeDtypeStruct((N_DEV, 3 * D), F32), jax.ShapeDtypeStruct((1, 1), F32)]
    whole = pl.BlockSpec(memory_space=pltpu.VMEM)
    return pl.pallas_call(
        body, out_shape=shapes,
        in_specs=[pl.BlockSpec(memory_space=pltpu.SMEM)] + [whole] * (1 + len(flat_in)),
        out_specs=[whole] * len(shapes), name="small_finalize")(chip, gathered, *flat_in)


def _ada_fwd(c_all, w_ada_shard, b_ada_shard):
    def body(c_ref, w_ref, b_ref, o_ref):
        cv = c_ref[...]
        o_ref[...] = jnp.dot(cv * _sigmoid(cv), w_ref[...], preferred_element_type=F32,
                             precision=lax.Precision.HIGHEST) + b_ref[...]

    return pl.pallas_call(
        body, out_shape=jax.ShapeDtypeStruct((N_DEV, w_ada_shard.shape[1]), F32), name="ada_fwd")(
            c_all, w_ada_shard, b_ada_shard)


def _ada_bwd(c_all_t, dmod_shard):
    def body(c_ref, d_ref, o_ref):
        cv = c_ref[...]
        o_ref[...] = jnp.dot(cv * _sigmoid(cv), d_ref[...], preferred_element_type=F32,
                             precision=lax.Precision.HIGHEST)

    return pl.pallas_call(
        body, out_shape=jax.ShapeDtypeStruct((D, dmod_shard.shape[1]), F32), name="ada_bwd")(c_all_t, dmod_shard)


def _sum_chip_slabs(arrived, part, place, tr, name, axis):
    n, rows, cols = arrived.shape
    per = rows // tr
    own_map = ((lambda i, pc: (pc[0], i, 0)) if part.shape[1] == rows
               else (lambda i, pc: (pc[0], pc[1] * per + i, 0)))

    def body(place_ref, a_ref, p_ref, o_ref):
        acc = p_ref[0].astype(F32)
        for k in range(n):
            acc = acc + a_ref[k].astype(F32)
        o_ref[...] = acc

    if axis == 1:
        whole, out_map = (2 * rows, cols), lambda i, pc: (pc[1] * per + i, 0)
    else:
        whole, out_map = (rows, 2 * cols), lambda i, pc: (i, pc[1])
    grid_spec = pltpu.PrefetchScalarGridSpec(
        num_scalar_prefetch=1, grid=(per,),
        in_specs=[pl.BlockSpec((n, tr, cols), lambda i, pc: (0, i, 0)),
                  pl.BlockSpec((1, tr, cols), own_map)],
        out_specs=pl.BlockSpec((tr, cols), out_map))
    return pl.pallas_call(
        body, grid_spec=grid_spec, out_shape=jax.ShapeDtypeStruct(whole, F32), name=name,
        compiler_params=_params(("parallel",)))(place, arrived, part)


def _sum_device_partials(arrived, parts, place):
    n = len(arrived)

    def body(place_ref, *refs):
        a_refs, p_refs, o_refs = refs[:n], refs[n:2 * n], refs[2 * n:]
        for a in range(n):
            acc = p_refs[a][0].astype(F32)
            for k in range(arrived[a].shape[0]):
                acc = acc + a_refs[a][k].astype(F32)
            o_refs[a][...] = acc

    grid_spec = pltpu.PrefetchScalarGridSpec(
        num_scalar_prefetch=1, grid=(1,),
        in_specs=[pl.BlockSpec(a.shape, lambda i, pc: (0, 0, 0)) for a in arrived]
        + [pl.BlockSpec((1,) + a.shape[1:], lambda i, pc: (pc[0], pc[1], 0)) for a in arrived],
        out_specs=[pl.BlockSpec(a.shape[1:], lambda i, pc: (pc[1], 0)) for a in arrived])
    return pl.pallas_call(
        body, grid_spec=grid_spec,
        out_shape=[jax.ShapeDtypeStruct((2 * a.shape[1], a.shape[2]), F32) for a in arrived],
        name="sum_device_partials", compiler_params=_params(("arbitrary",), VMEM_BIG))(place, *arrived, *parts)


def _adamw_many(ws, gs, ms, vs, tr):
    n = len(ws)
    rows = ws[0].shape[1]

    def body(*refs):
        ins, outs = refs[:4 * n], refs[4 * n:]
        for a in range(n):
            w_ref, g_ref, m_ref, v_ref = ins[4 * a:4 * a + 4]
            outs[3 * a][...], outs[3 * a + 1][...], outs[3 * a + 2][...] = _adamw_math(
                w_ref[...], g_ref[...], m_ref[...], v_ref[...])

    blk = lambda w: pl.BlockSpec((1, tr, w.shape[2]), lambda i: (0, i, 0))
    gs = [g.reshape(w.shape) for g, w in zip(gs, ws)]
    flat = [a for quad in zip(ws, gs, ms, vs) for a in quad]
    outs = pl.pallas_call(
        body, grid=(rows // tr,), in_specs=[blk(w) for w in ws for _ in range(4)],
        out_specs=[blk(w) for w in ws for _ in range(3)],
        out_shape=[jax.ShapeDtypeStruct(w.shape, F32) for w in ws for _ in range(3)], name="adamw_small_matrices",
        compiler_params=_params(("parallel",), VMEM_BIG))(*flat)
    return [(gs[a], outs[3 * a], outs[3 * a + 1], outs[3 * a + 2]) for a in range(n)]


def _add_own_half(full, other, core, tr, name, axis):
    n, rows, cols = other.shape
    per = rows // tr

    def body(c_ref, f_ref, o_ref, out_ref):
        out_ref[...] = (f_ref[...].astype(F32) + o_ref[...].astype(F32)).astype(BF16)

    full_map = (lambda k, i, c: (k, c[0] * per + i, 0)) if axis == 1 else (lambda k, i, c: (k, i, c[0]))
    grid_spec = pltpu.PrefetchScalarGridSpec(
        num_scalar_prefetch=1, grid=(n, per),
        in_specs=[pl.BlockSpec((1, tr, cols), full_map),
                  pl.BlockSpec((1, tr, cols), lambda k, i, c: (k, i, 0))],
        out_specs=pl.BlockSpec((1, tr, cols), lambda k, i, c: (k, i, 0)))
    return pl.pallas_call(
        body, grid_spec=grid_spec, out_shape=jax.ShapeDtypeStruct((n, rows, cols), BF16), name=name,
        compiler_params=_params(("parallel", "parallel"), VMEM_BIG))(core, full, other)


def _allgather8(block, src_rows, vmem, name):
    n = block.shape[1]
    m = src_rows
    sliced = block.shape[0] != m

    def body(x_ref, out_ref, send_sems, recv_sems, local_sem):
        x, y, c = _coords()
        me, sibling = (x, y, c), (x, y, 1 - c)
        chips = [(1 - x, y), (x, 1 - y), (1 - x, 1 - y)]
        src = x_ref.at[pl.ds(pl.multiple_of(c * m, 16), m), :] if sliced else x_ref

        def rows(px, py, pc):
            return out_ref.at[pl.ds(pl.multiple_of((4 * px + 2 * py + pc) * m, 8), m), :]

        def copy(k, blk, to, source=None):
            return pltpu.make_async_remote_copy(
                src_ref=rows(*blk) if source is None else source, dst_ref=rows(*blk),
                send_sem=send_sems.at[k], recv_sem=recv_sems.at[k], device_id=to, device_id_type=MESH)

        mine = pltpu.make_async_copy(src, rows(*me), local_sem)
        mine.start()
        first = [copy(0, me, sibling, source=src)]
        first += [copy(1 + j, me, (*chip, c), source=src) for j, chip in enumerate(chips)]
        for cp in first:
            cp.start()
        passed = [copy(4 + j, (*chip, c), sibling) for j, chip in enumerate(chips)]
        for j, chip in enumerate(chips):
            copy(1 + j, (*chip, c), me).wait_recv()
            passed[j].start()
        copy(0, sibling, me).wait_recv()
        for j, chip in enumerate(chips):
            copy(4 + j, (*chip, 1 - c), me).wait_recv()
        for cp in first + passed:
            cp.wait_send()
        mine.wait()

    space = pltpu.VMEM if vmem else pl.ANY
    return pl.pallas_call(
        body, out_shape=jax.ShapeDtypeStruct((N_DEV * m, n), block.dtype),
        in_specs=[pl.BlockSpec(memory_space=space)], out_specs=pl.BlockSpec(memory_space=space),
        scratch_shapes=[pltpu.SemaphoreType.DMA((7,)), pltpu.SemaphoreType.DMA((7,)), pltpu.SemaphoreType.DMA],
        name=name)(block)


def _gather_plan(x_refs, out_refs, send_sems, recv_sems, local_sems):
    n = len(x_refs)
    halves = [r.shape[0] // 2 for r in x_refs]
    x, y, c = _coords()
    me, sibling = (x, y, c), (x, y, 1 - c)
    chips = [(1 - x, y), (x, 1 - y), (1 - x, 1 - y)]

    def src(a):
        return x_refs[a].at[pl.ds(pl.multiple_of(c * halves[a], 16), halves[a]), :]

    def blk(a, px, py, pc):
        return out_refs[a].at[4 * px + 2 * py + pc]

    def copy(a, k, who, to, source=None):
        return pltpu.make_async_remote_copy(
            src_ref=blk(a, *who) if source is None else source, dst_ref=blk(a, *who),
            send_sem=send_sems.at[7 * a + k], recv_sem=recv_sems.at[7 * a + k], device_id=to, device_id_type=MESH)

    def mine(a):
        return pltpu.make_async_copy(src(a), blk(a, *me), local_sems.at[a])

    def first(a):
        return ([copy(a, 0, me, sibling, source=src(a))]
                + [copy(a, 1 + j, me, (*chip, c), source=src(a)) for j, chip in enumerate(chips)])

    def begin():
        for a in range(n):
            mine(a).start()
        for a in range(n):
            for cp in first(a):
                cp.start()

    def finish():
        onward = []
        for j, chip in enumerate(chips):
            for a in range(n):
                copy(a, 1 + j, (*chip, c), me).wait_recv()
                onward.append(copy(a, 4 + j, (*chip, c), sibling))
                onward[-1].start()
        for a in range(n):
            copy(a, 0, sibling, me).wait_recv()
        for j, chip in enumerate(chips):
            for a in range(n):
                copy(a, 4 + j, (*chip, 1 - c), me).wait_recv()
        for a in range(n):
            for cp in first(a):
                cp.wait_send()
        for cp in onward:
            cp.wait_send()
        for a in range(n):
            mine(a).wait()

    return begin, finish


def _gather_operands(shards):
    n = len(shards)
    shapes = [jax.ShapeDtypeStruct((N_DEV, a.shape[0] // 2, a.shape[1]), a.dtype) for a in shards]
    sems = [pltpu.SemaphoreType.DMA((7 * n,)), pltpu.SemaphoreType.DMA((7 * n,)), pltpu.SemaphoreType.DMA((n,))]
    return shapes, sems


def _as_chip_slabs(gathered, shards):
    return [o.reshape(N_CHIP, a.shape[0], a.shape[1]) for o, a in zip(gathered, shards)]


def _gather_alongside(body, n_in, n_out, n_shards, last_step):
    def wrapped(*refs):
        ins, shards = refs[:n_in], refs[n_in:n_in + n_shards]
        rest = refs[n_in + n_shards:]
        outs, gathered = rest[:n_out], rest[n_out:n_out + n_shards]
        scratch, sems = rest[n_out + n_shards:-3], rest[-3:]

        @pl.when(pl.program_id(0) == 0)
        def _():
            _gather_plan(shards, gathered, *sems)[0]()

        body(*ins, *outs, *scratch)

        @pl.when(pl.program_id(0) == last_step)
        def _():
            _gather_plan(shards, gathered, *sems)[1]()

    return wrapped


def _half(ref, axis, which, ndim):
    size = ref.shape[axis] // 2
    idx = [slice(None)] * ndim
    idx[axis] = pl.ds(pl.multiple_of(which * size, 8 if axis == ndim - 2 else LANE), size)
    return ref.at[tuple(idx)]


def _swap_halves_with_sibling(fulls, name, axes):
    n = len(fulls)

    def body(*refs):
        f_refs, got_refs = refs[:n], refs[n:2 * n]
        send_sems, recv_sems = refs[2 * n:]
        x, y, c = _coords()
        copies = []
        for a in range(n):
            copies.append(pltpu.make_async_remote_copy(
                src_ref=_half(f_refs[a], axes[a], 1 - c, 3), dst_ref=got_refs[a], send_sem=send_sems.at[a],
                recv_sem=recv_sems.at[a], device_id=(x, y, 1 - c), device_id_type=MESH))
        for cp in copies:
            cp.start()
        for cp in copies:
            cp.wait()

    def halved(a, axis):
        shape = list(a.shape)
        shape[axis] //= 2
        return jax.ShapeDtypeStruct(tuple(shape), a.dtype)

    return pl.pallas_call(
        body, out_shape=[halved(a, ax) for a, ax in zip(fulls, axes)],
        in_specs=[HBM_REF] * n, out_specs=[HBM_REF] * n,
        scratch_shapes=[pltpu.SemaphoreType.DMA((n,)), pltpu.SemaphoreType.DMA((n,))],
        name=name)(*fulls)


def _join_halves_with_sibling(wholes, axes, block):
    n = len(wholes)
    twice = jnp.concatenate([block, block], axis=0)
    gathered_shapes, gather_sems = _gather_operands([twice])

    def body(*refs):
        out_refs = refs[n + 1:2 * n + 1]
        send_sems, recv_sems = refs[2 * n + 2:2 * n + 4]
        begin, finish = _gather_plan([refs[n]], [refs[2 * n + 1]], *refs[2 * n + 4:])
        x, y, c = _coords()

        def push(a, core):
            half = _half(out_refs[a], axes[a] - 1, core, 2)
            return pltpu.make_async_remote_copy(
                src_ref=half, dst_ref=half, send_sem=send_sems.at[a], recv_sem=recv_sems.at[a],
                device_id=(x, y, 1 - c), device_id_type=MESH)

        begin()
        for a in range(n):
            push(a, c).start()
        finish()
        for a in range(n):
            push(a, 1 - c).wait_recv()
        for a in range(n):
            push(a, c).wait_send()

    outs = pl.pallas_call(
        body, out_shape=[jax.ShapeDtypeStruct(a.shape, a.dtype) for a in wholes] + gathered_shapes,
        in_specs=[HBM_REF] * (n + 1), out_specs=[HBM_REF] * (n + 1), input_output_aliases={a: a for a in range(n)},
        scratch_shapes=[pltpu.SemaphoreType.DMA((n,)), pltpu.SemaphoreType.DMA((n,))] + gather_sems,
        name="rs_pair_join")(*wholes, twice)
    return outs[:n], outs[n]


def _cols_to_slabs(g):
    rows, cols = g.shape
    return g.reshape(rows, N_CHIP, cols // N_CHIP).transpose(1, 0, 2)


def _slabs_to_cols(w):
    n, rows, cols = w.shape
    return w.transpose(1, 0, 2).reshape(rows, n * cols)


def _col_window(slabs, start, stop):
    n = slabs.shape[2]
    pieces = []
    for k in range(N_CHIP):
        lo, hi = max(start, k * n), min(stop, (k + 1) * n)
        if lo < hi:
            pieces.append(slabs[k][:, lo - k * n:hi - k * n])
    return pieces[0] if len(pieces) == 1 else jnp.concatenate(pieces, axis=1)


def _slabs_from_groups(groups, n):
    slabs = []
    for k in range(N_CHIP):
        pieces, off = [], 0
        for g in groups:
            lo, hi = max(k * n, off), min((k + 1) * n, off + g.shape[0])
            if lo < hi:
                pieces.append(g[lo - off:hi - off])
            off += g.shape[0]
        slabs.append(pieces[0] if len(pieces) == 1 else jnp.concatenate(pieces, axis=0))
    return jnp.stack(slabs)


def _uq_to_padded(w_uq):
    per = w_uq.reshape(RQ, H, DN + DR)
    nope = per[:, :, :DN].reshape(RQ, H * DN)
    rope = jnp.pad(per[:, :, DN:], ((0, 0), (0, 0), (0, LANE - DR))).reshape(RQ, H * LANE)
    return jnp.concatenate([nope, rope], axis=1)


def _uq_from_padded(g):
    nope = g[:, :H * DN].reshape(RQ, H, DN)
    rope = g[:, H * DN:].reshape(RQ, H, LANE)[:, :, :DR]
    return jnp.concatenate([nope, rope], axis=2).reshape(RQ, H * (DN + DR))


def _rope_tables(positions):
    inv_freq = ROPE_THETA ** (-jnp.arange(0, DR, 2, dtype=F32) / DR)
    ang = positions.astype(F32)[:, None] * inv_freq
    cos, sin = jnp.cos(ang), jnp.sin(ang)
    return jnp.tile(cos, (1, 4)), jnp.tile(jnp.concatenate([-sin, sin], axis=1), (1, 2))


def _pair_sums(fulls, core, tag, axes, tr):
    from_sibling = _swap_halves_with_sibling(fulls, f"rs_pair_swap_{tag}", axes)
    return [_add_own_half(f, o, core, min(tr, o.shape[1]), f"add_own_half_{tag}{n}", ax)
            for n, (f, o, ax) in enumerate(zip(fulls, from_sibling, axes))]


def _local_step(x, tgt, cos_t, sin_t, mod, weights, small, tiles, place):
    ts, ts_in, ts_mla, tm_nn, tm_tn, t_attn, chunk = tiles
    w_in_shard, later_shards, conv_w = weights
    norm_w, conv_b, ln_w, ln_b, q_norm_w, kv_norm_w, fnw = small
    shift, scale, gate = mod[:, 0:D], mod[:, D:2 * D], mod[:, 2 * D:3 * D]

    h, (g_in,) = _adaln_norm(x, norm_w, shift, scale, ts, [w_in_shard])
    wa = _col_window(g_in, 0, A_COLS)
    wl = jnp.pad(_col_window(g_in, A_COLS, A_COLS + L_COLS_RAW), ((0, 0), (0, L_COLS - L_COLS_RAW)))
    wg = _col_window(g_in, A_COLS + L_COLS_RAW, IN_COLS)
    proj_a = _mm_nn(h, wa, tm_nn, D, "proj_a")
    u0, u1, za, (g_uq, g_ukv, g_co, g_ao, g_o) = _conv_fwd(proj_a, conv_w, conv_b, ln_w, ln_b, ts, chunk, later_shards)
    w_uq2, w_ukv = _uq_to_padded(_slabs_to_cols(g_uq)), _slabs_to_cols(g_ukv)
    wco, wao, wo = g_co.reshape(D, D), g_ao.reshape(D, D), g_o.reshape(D, D)
    proj_l = _mm_nn(h, wl, tm_nn, L_COLS, "proj_l")
    proj_g = _mm_nn(h, wg, tm_nn, D, "proj_g")
    qn, kvn, q, k, v = _mla_prep(proj_l, q_norm_w, kv_norm_w, w_uq2, w_ukv, cos_t, sin_t, ts_mla)
    o, lse = _attn_fwd(q, k, v, t_attn)
    (dx2, dza, do, delta, dpg, zb, mg, dmo, dya, dyb, vec_mid) = _middle(
        za, o, proj_g, x, tgt, gate, fnw, wco, wao, wo, ts)
    g_wo = _mm_tn(mg, dmo, tm_tn, D, D, "grad_w_out", BF16)
    g_wco = _mm_tn(za, dya, tm_tn, D, D, "grad_w_conv_out", BF16)
    g_wao = _mm_tn(zb, dyb, tm_tn, D, D, "grad_w_attn_out", BF16)
    dq, dk, dv = _attn_bwd(q, k, v, do, lse, delta, t_attn)
    dpl, g_wuq2, g_wukv, vec_mla = _mla_prep_bwd(
        dq, dk, dv, proj_l, qn, kvn, q_norm_w, kv_norm_w, w_uq2, w_ukv, cos_t, sin_t, ts_mla)

    core = place[1:2]
    nr = D // N_CHIP
    early = [_cols_to_slabs(_uq_from_padded(g_wuq2)).astype(BF16), _cols_to_slabs(g_wukv).astype(BF16),
             g_wco.reshape(N_CHIP, nr, D), g_wao.reshape(N_CHIP, nr, D), g_wo.reshape(N_CHIP, nr, D)]
    dpa, g_conv_w, vec_conv, early_got = _conv_bwd(dza, proj_a, u0, u1, conv_w, ln_w, ln_b, ts, chunk, early)

    g_wa_t = _mm_tn(dpa, h, tm_tn, D, D, "grad_w_in_a", BF16)
    g_wl_t = _mm_tn(dpl, h, tm_tn, L_COLS, D, "grad_w_in_l", BF16)
    g_wg_t = _mm_tn(dpg, h, tm_tn, D, D, "grad_w_in_g", BF16)
    g_w_in_slabs = _slabs_from_groups([g_wa_t, g_wl_t[0:L_COLS_RAW], g_wg_t], IN_COLS // N_CHIP)
    late_sums = _pair_sums([g_w_in_slabs], core, "b", [2], IN_COLS // N_CHIP)
    grad_x, vec_in, late_got = _input_bwd(dpa, dpl, dpg, wa, wl, wg, x, dx2, norm_w, scale, ts_in, late_sums)

    col_sums = jnp.concatenate(
        [vec_in, vec_mid, vec_conv, jnp.pad(vec_mla, ((0, 0), (0, D - RQ))), g_conv_w], axis=0)
    wholes = ([_sum_chip_slabs(late_got[0], late_sums[0], place, W_IN_ROWS, "sum_chip_slabs_w_in", 2)]
              + list(_sum_device_partials(early_got, early, place)))
    shards, all_col_sums = _join_halves_with_sibling(wholes, [2] + [1] * len(early), col_sums)

    return grad_x, shards, all_col_sums


def kernel(x, c, positions, w_ada, b_ada, norm_w, w_in, conv_w, conv_b, conv_ln_w, conv_ln_b, w_conv_out, q_norm_w, w_uq, kv_norm_w, w_ukv, w_attn_out, w_out, final_norm_w, loss_target, m_w_ada, m_b_ada, m_norm_w, m_w_in, m_conv_w, m_conv_b, m_conv_ln_w, m_conv_ln_b, m_w_conv_out, m_q_norm_w, m_w_uq, m_kv_norm_w, m_w_ukv, m_w_attn_out, m_w_out, m_final_norm_w, v_w_ada, v_b_ada, v_norm_w, v_w_in, v_conv_w, v_conv_b, v_conv_ln_w, v_conv_ln_b, v_w_conv_out, v_q_norm_w, v_w_uq, v_kv_norm_w, v_w_ukv, v_w_attn_out, v_w_out, v_final_norm_w):
    ix, iy, ic = _coords()
    chip = 2 * ix + iy
    dev = 4 * ix + 2 * iy + ic
    s = x.shape[1]
    tiles = (256, 512, 512, 1024, 2048, 512, 32)

    conv_w_pad = jnp.pad(conv_w[0], ((0, HALO - KC), (0, 0)))
    small_in = jnp.concatenate([c.reshape(8, LANE), conv_w_pad.reshape(64, LANE)], axis=0)
    small_all = _allgather8(small_in, 72, True, "gather_c_conv").reshape(N_DEV, 72, LANE)
    c_all = small_all[:, 0:8].reshape(N_DEV, D)
    conv_full = jnp.concatenate(
        [small_all[2 * k, 8:72].reshape(HALO, D // N_CHIP) for k in range(N_CHIP)], axis=1)

    later_shards = [w[0].astype(BF16) for w in (w_uq, w_ukv, w_conv_out, w_attn_out, w_out)]
    weights = (w_in[0].astype(BF16), later_shards, conv_full)

    ada_cols = w_ada.shape[2]
    b_shard = lax.dynamic_slice(b_ada, (0, chip * ada_cols), (1, ada_cols))
    mod_part = _ada_fwd(c_all, w_ada[0], b_shard)
    mod_all = _allgather8(mod_part, N_DEV, True, "gather_mod").reshape(N_DEV, N_DEV, ada_cols)
    mod = jnp.concatenate(
        [lax.dynamic_slice(mod_all[2 * k], (dev, 0), (1, ada_cols)) for k in range(N_CHIP)], axis=1)

    cos_t, sin_t = _rope_tables(positions[0])
    small = (norm_w, conv_b, conv_ln_w, conv_ln_b, q_norm_w, kv_norm_w, final_norm_w.reshape(1, D))
    place = jnp.stack([chip, ic]).astype(jnp.int32)
    grad_x, shards, gathered = _local_step(x[0], loss_target[0], cos_t, sin_t, mod, weights, small, tiles, place)
    g_w_in_s, g_w_uq_s, g_w_ukv_s, g_wco_s, g_wao_s, g_wo_s = shards

    vec_names = ("b_ada", "norm_w", "conv_b", "conv_ln_w", "conv_ln_b", "q_norm_w", "kv_norm_w", "final_norm_w")
    row = lambda a: a.reshape(1, -1)
    vectors = [(row(b_ada), row(m_b_ada), row(v_b_ada)), (norm_w, m_norm_w, v_norm_w), (conv_b, m_conv_b, v_conv_b),
               (conv_ln_w, m_conv_ln_w, v_conv_ln_w), (conv_ln_b, m_conv_ln_b, v_conv_ln_b),
               (q_norm_w, m_q_norm_w, v_q_norm_w), (kv_norm_w, m_kv_norm_w, v_kv_norm_w),
               (row(final_norm_w), row(m_final_norm_w), row(v_final_norm_w))]
    fin = _small_finalize(gathered, vectors, (conv_w, m_conv_w, v_conv_w), place[0:1])
    res = {}
    for p, (name, (w, _, _)) in enumerate(zip(vec_names, vectors)):
        shape = final_norm_w.shape if name == "final_norm_w" else w.shape
        res[name] = tuple(a.reshape(shape) for a in fin[4 * p:4 * p + 4])
    res["conv_w"] = tuple(fin[4 * len(vectors):4 * len(vectors) + 4])
    dmod_all, loss = fin[-2], fin[-1].reshape(())
    dmod_shard = lax.dynamic_slice(dmod_all, (0, chip * ada_cols), (N_DEV, ada_cols))
    g_w_ada = _ada_bwd(c_all.T, dmod_shard).reshape(1, D, ada_cols)

    def big(w, g, m, v, tr, name):
        d, nm, nv = _adamw(w, g, m, v, tr, name)
        return g.reshape(w.shape), d, nm, nv

    res["w_ada"] = big(w_ada, g_w_ada[0], m_w_ada, v_w_ada, 256, "adamw_w_ada")
    t_in = [a[0].T for a in (w_in, m_w_in, v_w_in)]
    d_t, nm_t, nv_t = _adamw(t_in[0], g_w_in_s, t_in[1], t_in[2], W_IN_ROWS, "adamw_w_in")
    res["w_in"] = tuple(a.T[None] for a in (g_w_in_s, d_t, nm_t, nv_t))
    small = _adamw_many(
        [w_uq, w_ukv, w_conv_out, w_attn_out, w_out], [g_w_uq_s, g_w_ukv_s, g_wco_s, g_wao_s, g_wo_s],
        [m_w_uq, m_w_ukv, m_w_conv_out, m_w_attn_out, m_w_out], [v_w_uq, v_w_ukv, v_w_conv_out, v_w_attn_out, v_w_out],
        128)
    res["w_uq"], res["w_ukv"], res["w_conv_out"], res["w_attn_out"], res["w_out"] = small

    order = ("w_ada", "b_ada", "norm_w", "w_in", "conv_w", "conv_b", "conv_ln_w", "conv_ln_b", "w_conv_out",
             "q_norm_w", "w_uq", "kv_norm_w", "w_ukv", "w_attn_out", "w_out", "final_norm_w")
    outs = [loss, grad_x[None]]
    for slot in range(4):
        outs += [res[name][slot] for name in order]
    return tuple(outs)
```

```python
import functools

import numpy as np
import jax
import jax.numpy as jnp
from jax import lax
from jax.experimental import pallas as pl
from jax.experimental.pallas import tpu as pltpu

F32 = jnp.float32
BF16 = jnp.bfloat16
MESH = pl.DeviceIdType.MESH

D = 1024
H = 8
DN = 128
DR = 64
RQ = 256
KC = 31
HALO = 32
EPS = 1e-6
ROPE_THETA = 10000.0
N_CHIP = 4
N_DEV = 8
LANE = 128
VMEM_BIG = 56 * 1024 * 1024

ADAM_LR = 0.001
ADAM_B1 = 0.9
ADAM_B2 = 0.999
ADAM_EPS = 1e-08
ADAM_WD = 0.01
ADAM_STEP = 10

A_COLS = 3 * D
L_COLS_RAW = RQ + RQ + DR
L_COLS = 640
G_COLS = 3 * D
IN_COLS = A_COLS + L_COLS_RAW + G_COLS


def _params(sem=None, vmem=None):
    kw = {}
    if sem is not None:
        kw["dimension_semantics"] = sem
    if vmem is not None:
        kw["vmem_limit_bytes"] = vmem
    return pltpu.CompilerParams(**kw)


def _dot(a, b):
    return jnp.dot(a, b, preferred_element_type=F32)


def _dot_nt(a, b):
    return lax.dot_general(a, b, (((1,), (1,)), ((), ())), preferred_element_type=F32)


def _dot_tn(a, b):
    return lax.dot_general(a, b, (((0,), (0,)), ((), ())), preferred_element_type=F32)


def _colsum(v):
    return jnp.sum(v, axis=0, keepdims=True)


def _rowmean(v):
    return jnp.mean(v, axis=-1, keepdims=True)


def _sigmoid(v):
    return jax.nn.sigmoid(v)


def _dsilu(v, s):
    return s * (1.0 + v * (1.0 - s))


def _swap_halves(v, first_half):
    return jnp.where(first_half, pltpu.roll(v, 96, 1), pltpu.roll(v, 32, 1))


def _first_half_mask(rows):
    lane = lax.broadcasted_iota(jnp.int32, (rows, LANE), 1)
    return (lane % 64) < 32


def _adaln_norm(x, norm_w, c_all, w_ada_shard, b_ada_shard, dev, ts, shards):
    s = x.shape[0]
    cols = w_ada_shard.shape[1]

    def modulation(dev_ref, x_ref, nw_ref, c_ref, w_ref, b_ref, h_ref, mod_ref, part_sc, all_sc, *sems):
        cv = c_ref[...]
        part_sc[...] = jnp.dot(cv * _sigmoid(cv), w_ref[...], preferred_element_type=F32,
                               precision=lax.Precision.HIGHEST) + b_ref[...]
        _allgather8_run(part_sc, all_sc, *sems)
        for k in range(N_CHIP):
            mod_ref[:, cols * k:cols * (k + 1)] = all_sc[pl.ds(2 * N_DEV * k + dev_ref[0], 1), :]

    def body(dev_ref, x_ref, nw_ref, c_ref, w_ref, b_ref, h_ref, mod_ref, part_sc, all_sc, *sems):
        xv = x_ref[...]
        r = lax.rsqrt(_rowmean(xv * xv) + EPS)
        y = xv * r * nw_ref[...]
        h_ref[...] = (y * (1.0 + mod_ref[:, D:2 * D]) + mod_ref[:, 0:D]).astype(BF16)

    row = pl.BlockSpec((ts, D), lambda i: (i, 0))
    const = lambda shape: pl.BlockSpec(shape, lambda i: (0, 0))
    n = len(shards)
    gathered_shapes, sems = _gather_operands(shards)
    outs = pl.pallas_call(
        _gather_alongside(body, 6, 2, n, s // ts - 1, modulation), grid=(s // ts,),
        in_specs=[pl.BlockSpec(memory_space=pltpu.SMEM), row, const((1, D)), const(c_all.shape),
                  const(w_ada_shard.shape), const((1, cols))] + [HBM_REF] * n,
        out_specs=[row, const((1, 3 * D))] + [HBM_REF] * n,
        out_shape=[jax.ShapeDtypeStruct((s, D), BF16), jax.ShapeDtypeStruct((1, 3 * D), F32)] + gathered_shapes,
        scratch_shapes=[pltpu.VMEM((N_DEV, cols), F32), pltpu.VMEM((N_DEV * N_DEV, cols), F32)] + ALLGATHER8_SEMS + sems,
        name="adaln_norm", compiler_params=_params(("arbitrary",), VMEM_BIG))(
            dev, x, norm_w, c_all, w_ada_shard, b_ada_shard, *shards)
    return outs[0], outs[1], _as_chip_slabs(outs[2:], shards)


def _mm_nn(a, b, tm, tn, name):
    m, k = a.shape
    n = b.shape[1]

    def body(a_ref, b_ref, o_ref):
        o_ref[...] = _dot(a_ref[...], b_ref[...])

    return pl.pallas_call(
        body, grid=(n // tn, m // tm),
        in_specs=[pl.BlockSpec((tm, k), lambda j, i: (i, 0)), pl.BlockSpec((k, tn), lambda j, i: (0, j))],
        out_specs=pl.BlockSpec((tm, tn), lambda j, i: (i, j)),
        out_shape=jax.ShapeDtypeStruct((m, n), F32), name=name,
        compiler_params=_params(("parallel", "parallel"), VMEM_BIG))(a, b)


def _mm_tn(a, b, tm, tk, tn, name, out_dtype=F32):
    m, k = a.shape
    n = b.shape[1]
    steps = m // tm

    def body(a_ref, b_ref, o_ref, acc_ref):
        @pl.when(pl.program_id(2) == 0)
        def _():
            acc_ref[...] = jnp.zeros_like(acc_ref)
        acc_ref[...] += _dot_tn(a_ref[...], b_ref[...])

        @pl.when(pl.program_id(2) == steps - 1)
        def _():
            o_ref[...] = acc_ref[...].astype(out_dtype)

    return pl.pallas_call(
        body, grid=(k // tk, n // tn, steps),
        in_specs=[pl.BlockSpec((tm, tk), lambda r, j, i: (i, r)), pl.BlockSpec((tm, tn), lambda r, j, i: (i, j))],
        out_specs=pl.BlockSpec((tk, tn), lambda r, j, i: (r, j)),
        out_shape=jax.ShapeDtypeStruct((k, n), out_dtype), scratch_shapes=[pltpu.VMEM((tk, tn), F32)], name=name,
        compiler_params=_params(("parallel", "parallel", "arbitrary"), VMEM_BIG))(a, b)


def _coords():
    return lax.axis_index("x"), lax.axis_index("y"), lax.axis_index("c")


HBM_REF = pl.BlockSpec(memory_space=pl.ANY)


def _chip_scatter_copies(p_refs, got_refs, send_sems, recv_sems):
    x, y, c = _coords()
    copies = []
    for a in range(len(p_refs)):
        for j, (px, py) in enumerate([(1 - x, y), (x, 1 - y), (1 - x, 1 - y)]):
            copies.append(pltpu.make_async_remote_copy(
                src_ref=p_refs[a].at[2 * px + py], dst_ref=got_refs[a].at[j], send_sem=send_sems.at[3 * a + j],
                recv_sem=recv_sems.at[3 * a + j], device_id=(px, py, c), device_id_type=MESH))
    return copies


RELATIONS = [(dx, dy, dc) for dx in (0, 1) for dy in (0, 1) for dc in (0, 1)][1:]


def _device_scatter_copies(p_refs, got_refs, send_sems, recv_sems):
    x, y, c = _coords()
    copies = []
    for a in range(len(p_refs)):
        half = p_refs[a].shape[1] // 2
        for j, (dx, dy, dc) in enumerate(RELATIONS):
            px, py, pc = (1 - x if dx else x), (1 - y if dy else y), (1 - c if dc else c)
            src = p_refs[a].at[2 * px + py, pl.ds(pl.multiple_of(pc * half, 16), half), :]
            copies.append(pltpu.make_async_remote_copy(
                src_ref=src, dst_ref=got_refs[a].at[j], send_sem=send_sems.at[7 * a + j],
                recv_sem=recv_sems.at[7 * a + j], device_id=(px, py, pc), device_id_type=MESH))
    return copies


def _scatter_alongside(body, n_in, n_out, n_parts, last_step, make_copies):
    def wrapped(*refs):
        ins, parts = refs[:n_in], refs[n_in:n_in + n_parts]
        rest = refs[n_in + n_parts:]
        outs, got = rest[:n_out], rest[n_out:n_out + n_parts]
        scratch, (send_sems, recv_sems) = rest[n_out + n_parts:-2], rest[-2:]

        @pl.when(pl.program_id(0) == 0)
        def _():
            for cp in make_copies(parts, got, send_sems, recv_sems):
                cp.start()

        body(*ins, *outs, *scratch)

        @pl.when(pl.program_id(0) == last_step)
        def _():
            for cp in make_copies(parts, got, send_sems, recv_sems):
                cp.wait()

    return wrapped


def _scatter_operands(parts, per_device):
    n = len(parts)
    if per_device:
        slots, shapes = 7, [jax.ShapeDtypeStruct((7, a.shape[1] // 2, a.shape[2]), a.dtype) for a in parts]
    else:
        slots, shapes = 3, [jax.ShapeDtypeStruct((3,) + a.shape[1:], a.dtype) for a in parts]
    sems = [pltpu.SemaphoreType.DMA((slots * n,)), pltpu.SemaphoreType.DMA((slots * n,))]
    return [HBM_REF] * n, [HBM_REF] * n, shapes, sems


def _shifted_copies(win_ref, sh_ref, rows):
    for p in range(1, 8):
        sh_ref[p - 1, 0:rows, :] = win_ref[pl.ds(p, rows), :]


def _tap_rows(win_ref, sh_ref, start, rows):
    p = start % 8
    if p == 0:
        return win_ref[pl.ds(start, rows), :]
    return sh_ref[p - 1, pl.ds(start - p, rows), :]


def _conv_taps(win_ref, sh_ref, w_ref, rows, chunk, offset_of_tap):
    pieces = []
    for c0 in range(0, rows, chunk):
        acc = None
        for j in range(KC):
            term = w_ref[j:j + 1, :] * _tap_rows(win_ref, sh_ref, c0 + offset_of_tap(j), chunk)
            acc = term if acc is None else acc + term
        pieces.append(acc)
    return pieces


def _conv_fwd(proj_a, conv_w, conv_b, ln_w, ln_b, ts, chunk, shards):
    s = proj_a.shape[0]

    def body(av_ref, al_ref, ag_ref, w_ref, b_ref, lw_ref, lb_ref, u0_ref, u1_ref, za_ref, win_ref, sh_ref):
        @pl.when(pl.program_id(0) == 0)
        def _():
            win_ref[0:HALO, :] = jnp.zeros((HALO, D), F32)

        u0 = av_ref[...] * _sigmoid(al_ref[...])
        u0_ref[...] = u0
        win_ref[HALO:HALO + ts, :] = u0
        _shifted_copies(win_ref, sh_ref, ts + HALO - 8)
        pieces = _conv_taps(win_ref, sh_ref, w_ref, ts, chunk, lambda j: HALO - (KC - 1) + j)
        for n, acc in enumerate(pieces):
            u1_ref[n * chunk:(n + 1) * chunk, :] = acc + b_ref[...]
        win_ref[0:HALO, :] = win_ref[ts:ts + HALO, :]

        u1 = u1_ref[...]
        xc = u1 - _rowmean(u1)
        rstd = lax.rsqrt(_rowmean(xc * xc) + EPS)
        u2 = xc * rstd * lw_ref[...] + lb_ref[...]
        u3 = u2 * _sigmoid(u2)
        ag = ag_ref[...]
        za_ref[...] = (u3 * (ag * _sigmoid(ag))).astype(BF16)

    col = lambda c: pl.BlockSpec((ts, D), lambda i, c=c: (i, c))
    row = pl.BlockSpec((ts, D), lambda i: (i, 0))
    vec = pl.BlockSpec((1, D), lambda i: (0, 0))
    n = len(shards)
    gathered_shapes, sems = _gather_operands(shards)
    outs = pl.pallas_call(
        _gather_alongside(body, 7, 3, n, s // ts - 1), grid=(s // ts,),
        in_specs=[col(0), col(1), col(2), pl.BlockSpec((HALO, D), lambda i: (0, 0)), vec, vec, vec] + [HBM_REF] * n,
        out_specs=[row, row, row] + [HBM_REF] * n,
        out_shape=[jax.ShapeDtypeStruct((s, D), F32), jax.ShapeDtypeStruct((s, D), F32),
                   jax.ShapeDtypeStruct((s, D), BF16)] + gathered_shapes,
        scratch_shapes=[pltpu.VMEM((ts + HALO, D), F32), pltpu.VMEM((7, ts + HALO, D), F32)] + sems,
        name="conv_fwd", compiler_params=_params(("arbitrary",), VMEM_BIG))(
            proj_a, proj_a, proj_a, conv_w, conv_b, ln_w, ln_b, *shards)
    return outs[0], outs[1], outs[2], _as_chip_slabs(outs[3:], shards)


def _conv_bwd(dza, proj_a, u0, u1, conv_w, ln_w, ln_b, ts, chunk, parts):
    s = dza.shape[0]
    nt = s // ts
    per = ts // HALO

    def body(dza_ref, av_ref, al_ref, ag_ref, u0_ref, u0p_ref, u1_ref, w_ref, lw_ref, lb_ref,
             dpa_ref, gw_ref, gv_ref, dwin_ref, uwin_ref, du0_ref, gwp_ref, dsh_ref, ush_ref):
        step = pl.program_id(0)
        tile = nt - 1 - step

        @pl.when(step == 0)
        def _():
            dwin_ref[ts:ts + HALO, :] = jnp.zeros((HALO, D), F32)
            gwp_ref[...] = jnp.zeros_like(gwp_ref)
            gv_ref[...] = jnp.zeros_like(gv_ref)

        ag = ag_ref[...]
        sg = _sigmoid(ag)
        u1 = u1_ref[...]
        xc = u1 - _rowmean(u1)
        rstd = lax.rsqrt(_rowmean(xc * xc) + EPS)
        xh = xc * rstd
        u2 = xh * lw_ref[...] + lb_ref[...]
        s2 = _sigmoid(u2)
        dz = dza_ref[...]
        du3 = dz * (ag * sg)
        dpa_ref[:, 2 * D:3 * D] = (dz * (u2 * s2) * _dsilu(ag, sg)).astype(BF16)
        du2 = du3 * _dsilu(u2, s2)
        gv_ref[0:1, :] += _colsum(du2 * xh)
        gv_ref[1:2, :] += _colsum(du2)
        dxh = du2 * lw_ref[...]
        du1 = rstd * (dxh - _rowmean(dxh) - xh * _rowmean(dxh * xh))
        gv_ref[2:3, :] += _colsum(du1)
        dwin_ref[0:ts, :] = du1

        uwin_ref[0:HALO, :] = jnp.where(tile == 0, 0.0, u0p_ref[...])
        uwin_ref[HALO:HALO + ts, :] = u0_ref[...]

        _shifted_copies(dwin_ref, dsh_ref, ts + HALO - 8)
        _shifted_copies(uwin_ref, ush_ref, ts + HALO - 8)
        pieces = _conv_taps(dwin_ref, dsh_ref, w_ref, ts, chunk, lambda j: (KC - 1) - j)
        for n, acc in enumerate(pieces):
            du0_ref[n * chunk:(n + 1) * chunk, :] = acc
        for c0 in range(0, ts, chunk):
            dchunk = dwin_ref[c0:c0 + chunk, :]
            for j in range(KC):
                prod = dchunk * _tap_rows(uwin_ref, ush_ref, c0 + HALO - (KC - 1) + j, chunk)
                gwp_ref[8 * j:8 * j + 8, :] += jnp.sum(prod.reshape(chunk // 8, 8, D), axis=0)
        dwin_ref[ts:ts + HALO, :] = dwin_ref[0:HALO, :]

        du0 = du0_ref[...]
        al = al_ref[...]
        sl = _sigmoid(al)
        dpa_ref[:, 0:D] = (du0 * sl).astype(BF16)
        dpa_ref[:, D:2 * D] = (du0 * av_ref[...] * sl * (1.0 - sl)).astype(BF16)

        @pl.when(step == nt - 1)
        def _():
            for j in range(KC):
                gw_ref[j:j + 1, :] = _colsum(gwp_ref[8 * j:8 * j + 8, :])
            gw_ref[KC:HALO, :] = jnp.zeros((HALO - KC, D), F32)

    rev = lambda i: nt - 1 - i
    col = lambda c: pl.BlockSpec((ts, D), lambda i, c=c: (rev(i), c))
    row = pl.BlockSpec((ts, D), lambda i: (rev(i), 0))
    vec = pl.BlockSpec((1, D), lambda i: (0, 0))
    halo = pl.BlockSpec((HALO, D), lambda i: (jnp.maximum(rev(i) * per - 1, 0), 0))
    side_in, side_out, side_shapes, side_sems = _scatter_operands(parts, True)
    outs = pl.pallas_call(
        _scatter_alongside(body, 10, 3, len(parts), nt - 1, _device_scatter_copies), grid=(nt,),
        in_specs=[row, col(0), col(1), col(2), row, halo, row, pl.BlockSpec((HALO, D), lambda i: (0, 0)), vec, vec]
        + side_in,
        out_specs=[pl.BlockSpec((ts, A_COLS), lambda i: (rev(i), 0)),
                   pl.BlockSpec((HALO, D), lambda i: (0, 0)), pl.BlockSpec((8, D), lambda i: (0, 0))] + side_out,
        out_shape=[jax.ShapeDtypeStruct((s, A_COLS), BF16), jax.ShapeDtypeStruct((HALO, D), F32),
                   jax.ShapeDtypeStruct((8, D), F32)] + side_shapes,
        scratch_shapes=[pltpu.VMEM((ts + HALO, D), F32), pltpu.VMEM((ts + HALO, D), F32),
                        pltpu.VMEM((ts, D), F32), pltpu.VMEM((8 * HALO, D), F32),
                        pltpu.VMEM((7, ts + HALO, D), F32), pltpu.VMEM((7, ts + HALO, D), F32)] + side_sems,
        name="conv_bwd", compiler_params=_params(("arbitrary",), VMEM_BIG))(
            dza, proj_a, proj_a, proj_a, u0, u0, u1, conv_w, ln_w, ln_b, *parts)
    return outs[0], outs[1], outs[2], list(outs[3:])


def _mla_prep(proj_l, q_norm_w, kv_norm_w, w_uq2, w_ukv, cos_t, sin_t, ts):
    s = proj_l.shape[0]

    def body(pl_ref, qw_ref, kw_ref, wq_ref, wkv_ref, c_ref, s_ref, qn_ref, kvn_ref, q_ref, k_ref, v_ref):
        first = _first_half_mask(ts)
        cs = c_ref[...]
        sn = s_ref[...]

        def rms(v, w):
            return v * lax.rsqrt(_rowmean(v * v) + EPS) * w

        def rope(v):
            return v * cs + _swap_halves(v, first) * sn

        qn = rms(pl_ref[:, 0:RQ], qw_ref[...]).astype(BF16)
        kvn = rms(pl_ref[:, RQ:2 * RQ], kw_ref[...]).astype(BF16)
        qn_ref[...] = qn
        kvn_ref[...] = kvn
        q = _dot(qn, wq_ref[...])
        kv = _dot(kvn, wkv_ref[...])
        kr = rope(pl_ref[:, 2 * RQ:2 * RQ + LANE]).astype(BF16)
        for h in range(H):
            q_ref[h, :, 0:DN] = q[:, DN * h:DN * (h + 1)].astype(BF16)
            q_ref[h, :, DN:2 * DN] = rope(q[:, H * DN + LANE * h:H * DN + LANE * (h + 1)]).astype(BF16)
            k_ref[h, :, 0:DN] = kv[:, 2 * DN * h:2 * DN * h + DN].astype(BF16)
            k_ref[h, :, DN:2 * DN] = kr
            v_ref[h, :, 0:DN] = kv[:, 2 * DN * h + DN:2 * DN * (h + 1)].astype(BF16)
            v_ref[h, :, DN:2 * DN] = jnp.ones((ts, DN), BF16)

    const = lambda shape: pl.BlockSpec(shape, lambda i: (0,) * len(shape))
    rowb = lambda w: pl.BlockSpec((ts, w), lambda i: (i, 0))
    head = lambda w: pl.BlockSpec((H, ts, w), lambda i: (0, i, 0))
    return pl.pallas_call(
        body, grid=(s // ts,),
        in_specs=[rowb(L_COLS), const((1, RQ)), const((1, RQ)), const((RQ, 2 * H * DN)), const((RQ, 2 * H * DN)),
                  rowb(LANE), rowb(LANE)],
        out_specs=[rowb(RQ), rowb(RQ), head(2 * DN), head(2 * DN), head(2 * DN)],
        out_shape=[jax.ShapeDtypeStruct((s, RQ), BF16), jax.ShapeDtypeStruct((s, RQ), BF16),
                   jax.ShapeDtypeStruct((H, s, 2 * DN), BF16), jax.ShapeDtypeStruct((H, s, 2 * DN), BF16),
                   jax.ShapeDtypeStruct((H, s, 2 * DN), BF16)],
        name="mla_prep", compiler_params=_params(("parallel",), VMEM_BIG))(
            proj_l, q_norm_w, kv_norm_w, w_uq2, w_ukv, cos_t, sin_t)


def _mla_prep_bwd(dq, dk, dv, proj_l, qn, kvn, q_norm_w, kv_norm_w, w_uq2, w_ukv, cos_t, sin_t, ts):
    s = proj_l.shape[0]

    def body(dq_ref, dk_ref, dv_ref, pl_ref, qn_ref, kvn_ref, qw_ref, kw_ref, wq_ref, wkv_ref, c_ref, s_ref,
             dpl_ref, gwq_ref, gwkv_ref, gv_ref, dq2_ref, dkv2_ref):
        @pl.when(pl.program_id(0) == 0)
        def _():
            gwq_ref[...] = jnp.zeros_like(gwq_ref)
            gwkv_ref[...] = jnp.zeros_like(gwkv_ref)
            gv_ref[...] = jnp.zeros_like(gv_ref)

        first = _first_half_mask(ts)
        cs = c_ref[...] * ATT_SCALE
        sn = s_ref[...] * ATT_SCALE

        def rope_bwd(g):
            return g * cs + _swap_halves(g * sn, first)

        def rms_bwd(v, w, dy):
            r = lax.rsqrt(_rowmean(v * v) + EPS)
            vh = v * r
            dvh = dy * w
            return r * (dvh - vh * _rowmean(dvh * vh)), _colsum(dy * vh)

        dkr = None
        for h in range(H):
            dq2_ref[:, DN * h:DN * (h + 1)] = (dq_ref[h, :, 0:DN] * ATT_SCALE).astype(BF16)
            dq2_ref[:, H * DN + LANE * h:H * DN + LANE * (h + 1)] = rope_bwd(dq_ref[h, :, DN:2 * DN]).astype(BF16)
            dkv2_ref[:, 2 * DN * h:2 * DN * h + DN] = (dk_ref[h, :, 0:DN] * ATT_SCALE).astype(BF16)
            dkv2_ref[:, 2 * DN * h + DN:2 * DN * (h + 1)] = dv_ref[h].astype(BF16)
            part = dk_ref[h, :, DN:2 * DN]
            dkr = part if dkr is None else dkr + part

        dq2 = dq2_ref[...]
        dkv2 = dkv2_ref[...]
        gwq_ref[...] += _dot_tn(qn_ref[...], dq2)
        gwkv_ref[...] += _dot_tn(kvn_ref[...], dkv2)
        dcq, gq = rms_bwd(pl_ref[:, 0:RQ], qw_ref[...], _dot_nt(dq2, wq_ref[...]))
        dckv, gkv = rms_bwd(pl_ref[:, RQ:2 * RQ], kw_ref[...], _dot_nt(dkv2, wkv_ref[...]))
        gv_ref[0:1, :] += gq
        gv_ref[1:2, :] += gkv
        dpl_ref[:, 0:RQ] = dcq.astype(BF16)
        dpl_ref[:, RQ:2 * RQ] = dckv.astype(BF16)
        dpl_ref[:, 2 * RQ:2 * RQ + LANE] = rope_bwd(dkr).astype(BF16)

    const = lambda shape: pl.BlockSpec(shape, lambda i: (0,) * len(shape))
    rowb = lambda w: pl.BlockSpec((ts, w), lambda i: (i, 0))
    head = lambda w: pl.BlockSpec((H, ts, w), lambda i: (0, i, 0))
    return pl.pallas_call(
        body, grid=(s // ts,),
        in_specs=[head(2 * DN), head(2 * DN), head(DN), rowb(L_COLS), rowb(RQ), rowb(RQ), const((1, RQ)),
                  const((1, RQ)), const((RQ, 2 * H * DN)), const((RQ, 2 * H * DN)), rowb(LANE), rowb(LANE)],
        out_specs=[rowb(L_COLS), const((RQ, 2 * H * DN)), const((RQ, 2 * H * DN)), const((8, RQ))],
        out_shape=[jax.ShapeDtypeStruct((s, L_COLS), BF16), jax.ShapeDtypeStruct((RQ, 2 * H * DN), F32),
                   jax.ShapeDtypeStruct((RQ, 2 * H * DN), F32), jax.ShapeDtypeStruct((8, RQ), F32)],
        scratch_shapes=[pltpu.VMEM((ts, 2 * H * DN), BF16), pltpu.VMEM((ts, 2 * H * DN), BF16)],
        name="mla_prep_bwd", compiler_params=_params(("arbitrary",), VMEM_BIG))(
            dq, dk, dv, proj_l, qn, kvn, q_norm_w, kv_norm_w, w_uq2, w_ukv, cos_t, sin_t)


def _causal_pairs(n, by_key):
    if by_key:
        pairs = [(i, j) for j in range(n) for i in range(j, n)]
    else:
        pairs = [(i, j) for i in range(n) for j in range(i + 1)]
    return (jnp.asarray(np.array([p[0] for p in pairs], np.int32)),
            jnp.asarray(np.array([p[1] for p in pairs], np.int32)))


ATT_SCALE = float((DN + DR) ** -0.5)
LOG2E = 1.4426950408889634
LN2 = 0.6931471805599453
ATT_HEADS_FWD = 4
ATT_HEADS = 2
W_IN_ROWS = 336
ATT_ROWS = 64


def _diag_width(r0, t):
    return min(t, -(-(r0 + ATT_ROWS) // LANE) * LANE)


def _diag_mask_rows(r0, width):
    rows = r0 + lax.broadcasted_iota(jnp.int32, (ATT_ROWS, width), 0)
    cols = lax.broadcasted_iota(jnp.int32, (ATT_ROWS, width), 1)
    return cols <= rows


def _diag_mask(t):
    return lax.broadcasted_iota(jnp.int32, (t, t), 1) <= lax.broadcasted_iota(jnp.int32, (t, t), 0)


def _attn_fwd(q, k, v, t):
    s = q.shape[1]
    n = s // t
    scale2 = float((DN + DR) ** -0.5) * LOG2E
    qi, ki = _causal_pairs(n, by_key=False)

    def body(qi_ref, ki_ref, q_ref, k_ref, v_ref, o_ref, lse_ref, *scratch):
        per_head = [scratch[5 * h:5 * h + 5] for h in range(ATT_HEADS_FWD)]
        p = pl.program_id(1)
        i = qi_ref[p]
        j = ki_ref[p]

        @pl.when(j == 0)
        def _():
            for m_sc, acc_sc, _, _, _ in per_head:
                m_sc[...] = jnp.full_like(m_sc, -jnp.inf)
                acc_sc[...] = jnp.zeros_like(acc_sc)

        def scores(h, diag):
            sc = _dot_nt(q_ref[h], k_ref[h])
            if diag:
                sc = jnp.where(_diag_mask(t), sc, -jnp.inf)
            per_head[h][2][...] = sc

        def rowmax(h, rows):
            per_head[h][4][rows, :] = jnp.max(per_head[h][2][rows, :], axis=-1, keepdims=True)

        def stats(h):
            m_sc, acc_sc, _, _, mx_sc = per_head[h]
            m_prev = m_sc[...]
            m_new = jnp.maximum(m_prev, mx_sc[...] * scale2)
            m_sc[...] = m_new
            acc_sc[...] = jnp.exp2(m_prev - m_new) * acc_sc[...]

        def probs(h, rows):
            m_sc, _, s_sc, p_sc, _ = per_head[h]
            p_sc[rows, :] = jnp.exp2(s_sc[rows, :] * scale2 - m_sc[rows, :]).astype(BF16)

        def values(h):
            _, acc_sc, _, p_sc, _ = per_head[h]
            acc_sc[...] += _dot(p_sc[...], v_ref[h])

        def step(diag):
            blocks = [slice(r0, r0 + ATT_ROWS) for r0 in range(0, t, ATT_ROWS)]
            for h in range(ATT_HEADS_FWD):
                scores(h, diag)
            for rows in blocks:
                rowmax(0, rows)
            stats(0)
            for h in range(ATT_HEADS_FWD):
                for rows in blocks:
                    probs(h, rows)
                    if h + 1 < ATT_HEADS_FWD:
                        rowmax(h + 1, rows)
                if h + 1 < ATT_HEADS_FWD:
                    stats(h + 1)
                values(h)

        @pl.when(j < i)
        def _():
            step(False)

        @pl.when(j == i)
        def _():
            step(True)
            for h, (m_sc, acc_sc, _, _, _) in enumerate(per_head):
                l = acc_sc[:, DN:2 * DN]
                o_ref[:, DN * h:DN * (h + 1)] = acc_sc[:, 0:DN] / l
                lse_ref[h] = (m_sc[...] + jnp.log2(l[:, 0:1])) * LN2

    hb = ATT_HEADS_FWD
    grid_spec = pltpu.PrefetchScalarGridSpec(
        num_scalar_prefetch=2, grid=(H // hb, int(qi.shape[0])),
        in_specs=[pl.BlockSpec((hb, t, 2 * DN), lambda h, p, qi, ki: (h, qi[p], 0)),
                  pl.BlockSpec((hb, t, 2 * DN), lambda h, p, qi, ki: (h, ki[p], 0)),
                  pl.BlockSpec((hb, t, 2 * DN), lambda h, p, qi, ki: (h, ki[p], 0))],
        out_specs=[pl.BlockSpec((t, hb * DN), lambda h, p, qi, ki: (qi[p], h)),
                   pl.BlockSpec((hb, t, 1), lambda h, p, qi, ki: (h, qi[p], 0))],
        scratch_shapes=[pltpu.VMEM((t, 1), F32), pltpu.VMEM((t, 2 * DN), F32), pltpu.VMEM((t, t), F32),
                        pltpu.VMEM((t, t), BF16), pltpu.VMEM((t, 1), F32)] * hb)
    return pl.pallas_call(
        body, grid_spec=grid_spec,
        out_shape=[jax.ShapeDtypeStruct((s, H * DN), F32), jax.ShapeDtypeStruct((H, s, 1), F32)],
        name="attn_fwd", compiler_params=_params(("parallel", "arbitrary"), VMEM_BIG))(qi, ki, q, k, v)


def _attn_bwd(q, k, v, do, lse, delta, t):
    s = q.shape[1]
    n = s // t
    scale = ATT_SCALE
    qi, ki = _causal_pairs(n, by_key=True)

    def body(qi_ref, ki_ref, q_ref, k_ref, v_ref, do_ref, lse_ref, dl_ref, dq_ref, dk_ref, dv_ref,
             dk_sc, dv_sc, s_sc, dp_sc, p_sc, ds_sc):
        p = pl.program_id(1)
        i = qi_ref[p]
        j = ki_ref[p]

        @pl.when(p == 0)
        def _():
            dq_ref[...] = jnp.zeros_like(dq_ref)

        @pl.when(i == j)
        def _():
            dk_sc[...] = jnp.zeros_like(dk_sc)
            dv_sc[...] = jnp.zeros_like(dv_sc)

        def step(diag):
            for h in range(ATT_HEADS):
                s_sc[h] = _dot_nt(q_ref[h], k_ref[h])
                dp_sc[h] = _dot_nt(do_ref[:, DN * h:DN * (h + 1)], v_ref[h, :, 0:DN])
            for h in range(ATT_HEADS):
                for r0 in range(0, t, ATT_ROWS):
                    rows = slice(r0, r0 + ATT_ROWS)
                    width = _diag_width(r0, t) if diag else t
                    sc = s_sc[h, rows, 0:width] * (scale * LOG2E)
                    if diag:
                        sc = jnp.where(_diag_mask_rows(r0, width), sc, -jnp.inf)
                    pr = jnp.exp2(sc - lse_ref[h, rows, :] * LOG2E)
                    ds = pr * (dp_sc[h, rows, 0:width] - dl_ref[h, rows, :])
                    p_sc[h, rows, 0:width] = pr.astype(BF16)
                    ds_sc[h, rows, 0:width] = ds.astype(BF16)
                    if width < t:
                        p_sc[h, rows, width:t] = jnp.zeros((ATT_ROWS, t - width), BF16)
                        ds_sc[h, rows, width:t] = jnp.zeros((ATT_ROWS, t - width), BF16)
            q_rows = pl.ds(pl.multiple_of(i * t, t), t)
            for h in range(ATT_HEADS):
                dv_sc[h] += _dot_tn(p_sc[h], do_ref[:, DN * h:DN * (h + 1)])
                dk_sc[h] += _dot_tn(ds_sc[h], q_ref[h])
                dq_ref[h, q_rows, :] += _dot(ds_sc[h], k_ref[h])

        @pl.when(i > j)
        def _():
            step(False)

        @pl.when(i == j)
        def _():
            step(True)

        @pl.when(i == n - 1)
        def _():
            dk_ref[...] = dk_sc[...]
            dv_ref[...] = dv_sc[...]

    hb = ATT_HEADS
    grid_spec = pltpu.PrefetchScalarGridSpec(
        num_scalar_prefetch=2, grid=(H // hb, int(qi.shape[0])),
        in_specs=[pl.BlockSpec((hb, t, 2 * DN), lambda h, p, qi, ki: (h, qi[p], 0)),
                  pl.BlockSpec((hb, t, 2 * DN), lambda h, p, qi, ki: (h, ki[p], 0)),
                  pl.BlockSpec((hb, t, 2 * DN), lambda h, p, qi, ki: (h, ki[p], 0)),
                  pl.BlockSpec((t, hb * DN), lambda h, p, qi, ki: (qi[p], h)),
                  pl.BlockSpec((hb, t, 1), lambda h, p, qi, ki: (h, qi[p], 0)),
                  pl.BlockSpec((hb, t, 1), lambda h, p, qi, ki: (h, qi[p], 0))],
        out_specs=[pl.BlockSpec((hb, s, 2 * DN), lambda h, p, qi, ki: (h, 0, 0)),
                   pl.BlockSpec((hb, t, 2 * DN), lambda h, p, qi, ki: (h, ki[p], 0)),
                   pl.BlockSpec((hb, t, DN), lambda h, p, qi, ki: (h, ki[p], 0))],
        scratch_shapes=[pltpu.VMEM((hb, t, 2 * DN), F32), pltpu.VMEM((hb, t, DN), F32),
                        pltpu.VMEM((hb, t, t), F32), pltpu.VMEM((hb, t, t), F32),
                        pltpu.VMEM((hb, t, t), BF16), pltpu.VMEM((hb, t, t), BF16)])
    return pl.pallas_call(
        body, grid_spec=grid_spec,
        out_shape=[jax.ShapeDtypeStruct((H, s, 2 * DN), F32), jax.ShapeDtypeStruct((H, s, 2 * DN), F32),
                   jax.ShapeDtypeStruct((H, s, DN), F32)],
        name="attn_bwd", compiler_params=_params(("parallel", "arbitrary"), VMEM_BIG))(
            qi, ki, q, k, v, do, lse, delta)


def _middle(za, o, proj_g, x, tgt, gate, fnw, wco, wao, wo, ts):
    s = x.shape[0]
    inv_d = 1.0 / D

    def body(za_ref, o_ref, bg_ref, ga_ref, gb_ref, x_ref, t_ref, gate_ref, fnw_ref, wco_ref, wao_ref, wo_ref,
             dx2_ref, dza_ref, do_ref, dl_ref, dpg_ref, zb_ref, mg_ref, dmo_ref, dya_ref, dyb_ref, vec_ref):
        @pl.when(pl.program_id(0) == 0)
        def _():
            vec_ref[...] = jnp.zeros_like(vec_ref)

        ov = o_ref[...]
        bg = bg_ref[...]
        sb = _sigmoid(bg)
        silu_b = bg * sb
        zb = (ov * silu_b).astype(BF16)
        zb_ref[...] = zb
        ya = _dot(za_ref[...], wco_ref[...])
        yb = _dot(zb, wao_ref[...])
        sa = _sigmoid(ga_ref[...])
        sg = _sigmoid(gb_ref[...])
        mg = (sa * ya + sg * yb).astype(BF16)
        mg_ref[...] = mg
        mo = _dot(mg, wo_ref[...])
        gate_v = gate_ref[...]
        x2 = x_ref[...] + gate_v * mo
        r = lax.rsqrt(_rowmean(x2 * x2) + EPS)
        xh = x2 * r
        fw = fnw_ref[...]
        e = xh * fw - t_ref[...]
        vec_ref[2:3, :] += _colsum(e * e)
        dy = e * inv_d
        vec_ref[0:1, :] += _colsum(dy * xh)
        dxh = dy * fw
        dx2 = r * (dxh - xh * _rowmean(dxh * xh))
        dx2_ref[...] = dx2
        vec_ref[1:2, :] += _colsum(dx2 * mo)
        dmo = (gate_v * dx2).astype(BF16)
        dmo_ref[...] = dmo
        dmg = _dot_nt(dmo, wo_ref[...])
        dya = (sa * dmg).astype(BF16)
        dyb = (sg * dmg).astype(BF16)
        dya_ref[...] = dya
        dyb_ref[...] = dyb
        dpg_ref[:, D:2 * D] = (dmg * ya * (sa * (1.0 - sa))).astype(BF16)
        dpg_ref[:, 2 * D:3 * D] = (dmg * yb * (sg * (1.0 - sg))).astype(BF16)
        dza_ref[...] = _dot_nt(dya, wco_ref[...])
        dzb = _dot_nt(dyb, wao_ref[...])
        dov = dzb * silu_b
        do_ref[...] = dov.astype(BF16)
        dpg_ref[:, 0:D] = (dzb * ov * _dsilu(bg, sb)).astype(BF16)
        dprod = dov * ov
        for h in range(H):
            dl_ref[h] = jnp.sum(dprod[:, DN * h:DN * (h + 1)], axis=-1, keepdims=True)

    col = lambda c: pl.BlockSpec((ts, D), lambda i, c=c: (i, c))
    row = pl.BlockSpec((ts, D), lambda i: (i, 0))
    vec = pl.BlockSpec((1, D), lambda i: (0, 0))
    wsp = pl.BlockSpec((D, D), lambda i: (0, 0))
    bf = jax.ShapeDtypeStruct((s, D), BF16)
    ff = jax.ShapeDtypeStruct((s, D), F32)
    return pl.pallas_call(
        body, grid=(s // ts,),
        in_specs=[row, row, col(0), col(1), col(2), row, row, vec, vec, wsp, wsp, wsp],
        out_specs=[row, row, row, pl.BlockSpec((H, ts, 1), lambda i: (0, i, 0)),
                   pl.BlockSpec((ts, G_COLS), lambda i: (i, 0)), row, row, row, row, row,
                   pl.BlockSpec((8, D), lambda i: (0, 0))],
        out_shape=[ff, ff, bf, jax.ShapeDtypeStruct((H, s, 1), F32), jax.ShapeDtypeStruct((s, G_COLS), BF16),
                   bf, bf, bf, bf, bf, jax.ShapeDtypeStruct((8, D), F32)],
        name="middle", compiler_params=_params(("arbitrary",), VMEM_BIG))(
            za, o, proj_g, proj_g, proj_g, x, tgt, gate, fnw, wco, wao, wo)


def _input_bwd(dpa, dpl, dpg, wa, wl, wg, x, dx2, norm_w, scale, ts, parts):
    s = x.shape[0]

    def body(dpa_ref, dpl_ref, dpg_ref, wa_ref, wl_ref, wg_ref, x_ref, dx2_ref, nw_ref, sc_ref, gx_ref, gv_ref):
        @pl.when(pl.program_id(0) == 0)
        def _():
            gv_ref[...] = jnp.zeros_like(gv_ref)

        dh = (_dot_nt(dpa_ref[...], wa_ref[...]) + _dot_nt(dpl_ref[...], wl_ref[...])
              + _dot_nt(dpg_ref[...], wg_ref[...]))
        xv = x_ref[...]
        r = lax.rsqrt(_rowmean(xv * xv) + EPS)
        xh = xv * r
        nw = nw_ref[...]
        gv_ref[0:1, :] += _colsum(dh)
        gv_ref[1:2, :] += _colsum(dh * (xh * nw))
        dy = dh * (1.0 + sc_ref[...])
        gv_ref[2:3, :] += _colsum(dy * xh)
        dxh = dy * nw
        gx_ref[...] = dx2_ref[...] + r * (dxh - xh * _rowmean(dxh * xh))

    const = lambda shape: pl.BlockSpec(shape, lambda i: (0, 0))
    rowb = lambda w: pl.BlockSpec((ts, w), lambda i: (i, 0))
    side_in, side_out, side_shapes, side_sems = _scatter_operands(parts, False)
    outs = pl.pallas_call(
        _scatter_alongside(body, 10, 2, len(parts), s // ts - 1, _chip_scatter_copies), grid=(s // ts,),
        in_specs=[rowb(A_COLS), rowb(L_COLS), rowb(G_COLS), const((D, A_COLS)), const((D, L_COLS)),
                  const((D, G_COLS)), rowb(D), rowb(D), const((1, D)), const((1, D))] + side_in,
        out_specs=[rowb(D), const((8, D))] + side_out,
        out_shape=[jax.ShapeDtypeStruct((s, D), F32), jax.ShapeDtypeStruct((8, D), F32)] + side_shapes,
        scratch_shapes=side_sems,
        name="input_bwd", compiler_params=_params(("arbitrary",), VMEM_BIG))(
            dpa, dpl, dpg, wa, wl, wg, x, dx2, norm_w, scale, *parts)
    return outs[0], outs[1], list(outs[2:])


def _adamw_math(w, g, m, v):
    nm = ADAM_B1 * m + (1.0 - ADAM_B1) * g
    nv = ADAM_B2 * v + (1.0 - ADAM_B2) * (g * g)
    m_hat = nm / (1.0 - ADAM_B1 ** ADAM_STEP)
    v_hat = nv / (1.0 - ADAM_B2 ** ADAM_STEP)
    return -ADAM_LR * (m_hat / (jnp.sqrt(v_hat) + ADAM_EPS) + ADAM_WD * w), nm, nv


def _adamw(w, g, m, v, tr, name):
    lead, (rows, cols) = w.shape[:-2], w.shape[-2:]

    def body(w_ref, g_ref, m_ref, v_ref, d_ref, nm_ref, nv_ref):
        d_ref[...], nm_ref[...], nv_ref[...] = _adamw_math(w_ref[...], g_ref[...], m_ref[...], v_ref[...])

    blk = pl.BlockSpec((1,) * len(lead) + (tr, cols), lambda i: (0,) * len(lead) + (i, 0))
    shp = jax.ShapeDtypeStruct(w.shape, F32)
    return pl.pallas_call(
        body, grid=(rows // tr,), in_specs=[blk] * 4, out_specs=[blk] * 3, out_shape=[shp] * 3, name=name,
        compiler_params=_params(("parallel",), VMEM_BIG))(w, g.reshape(w.shape), m, v)


ROW_SHIFT, ROW_SCALE, ROW_NORM_W = 0, 1, 2
ROW_FINAL_NORM_W, ROW_GATE, ROW_LOSS = 8, 9, 10
ROW_LN_W, ROW_LN_B, ROW_CONV_B = 16, 17, 18
ROW_Q_NORM_W, ROW_KV_NORM_W = 24, 25
ROW_CONV_W = 32
SUM_ROWS = 64
VECTOR_ROWS = ((ROW_SHIFT, ROW_SCALE, ROW_GATE), (ROW_NORM_W,), (ROW_CONV_B,), (ROW_LN_W,), (ROW_LN_B,),
               (ROW_Q_NORM_W,), (ROW_KV_NORM_W,), (ROW_FINAL_NORM_W,))


def _small_finalize(gathered, vectors, conv, chip):
    n = len(vectors)
    cw = conv[0].shape[2]

    def body(chip_ref, g_ref, *refs):
        ins, outs = refs[:3 * n + 3], refs[3 * n + 3:]
        tot = g_ref[0]
        for k in range(1, N_DEV):
            tot = tot + g_ref[k]
        for p, rows in enumerate(VECTOR_ROWS):
            w_ref, m_ref, v_ref = ins[3 * p:3 * p + 3]
            g_out, d_out, nm_out, nv_out = outs[4 * p:4 * p + 4]
            width = w_ref.shape[1] // len(rows)
            for q, r in enumerate(rows):
                lanes = slice(q * width, (q + 1) * width)
                g = tot[r:r + 1, 0:width]
                g_out[:, lanes] = g
                d_out[:, lanes], nm_out[:, lanes], nv_out[:, lanes] = _adamw_math(
                    w_ref[:, lanes], g, m_ref[:, lanes], v_ref[:, lanes])
        cols = pl.ds(pl.multiple_of(chip_ref[0] * cw, LANE), cw)
        gc = g_ref[0, pl.ds(ROW_CONV_W, KC), cols]
        for k in range(1, N_DEV):
            gc = gc + g_ref[k, pl.ds(ROW_CONV_W, KC), cols]
        cw_ref, cm_ref, cv_ref = ins[3 * n:3 * n + 3]
        g_out, d_out, nm_out, nv_out, dmod_ref, loss_ref = outs[4 * n:]
        g_out[0] = gc
        d_out[0], nm_out[0], nv_out[0] = _adamw_math(cw_ref[0], gc, cm_ref[0], cv_ref[0])
        for k in range(N_DEV):
            for q, r in enumerate((ROW_SHIFT, ROW_SCALE, ROW_GATE)):
                dmod_ref[k:k + 1, q * D:(q + 1) * D] = g_ref[k, r:r + 1, :]
        loss_ref[...] = (0.5 / D) * jnp.sum(tot[ROW_LOSS:ROW_LOSS + 1, :], axis=-1, keepdims=True)

    flat_in = [a for triple in vectors for a in triple] + list(conv)
    shapes = [jax.ShapeDtypeStruct(w.shape, F32) for w, _, _ in vectors for _ in range(4)]
    shapes += [jax.ShapeDtypeStruct(conv[0].shape, F32)] * 4
    shapes += [jax.ShapeDtypeStruct((N_DEV, 3 * D), F32), jax.ShapeDtypeStruct((1, 1), F32)]
    whole = pl.BlockSpec(memory_space=pltpu.VMEM)
    return pl.pallas_call(
        body, out_shape=shapes,
        in_specs=[pl.BlockSpec(memory_space=pltpu.SMEM)] + [whole] * (1 + len(flat_in)),
        out_specs=[whole] * len(shapes), name="small_finalize")(chip, gathered, *flat_in)


def _ada_bwd(c_all_t, dmod_shard):
    def body(c_ref, d_ref, o_ref):
        cv = c_ref[...]
        o_ref[...] = jnp.dot(cv * _sigmoid(cv), d_ref[...], preferred_element_type=F32,
                             precision=lax.Precision.HIGHEST)

    return pl.pallas_call(
        body, out_shape=jax.ShapeDtypeStruct((D, dmod_shard.shape[1]), F32), name="ada_bwd")(c_all_t, dmod_shard)


def _sum_chip_slabs(arrived, part, place, tr, name, axis):
    n, rows, cols = arrived.shape
    per = rows // tr
    own_map = ((lambda i, pc: (pc[0], i, 0)) if part.shape[1] == rows
               else (lambda i, pc: (pc[0], pc[1] * per + i, 0)))

    def body(place_ref, a_ref, p_ref, o_ref):
        acc = p_ref[0].astype(F32)
        for k in range(n):
            acc = acc + a_ref[k].astype(F32)
        o_ref[...] = acc

    if axis == 1:
        whole, out_map = (2 * rows, cols), lambda i, pc: (pc[1] * per + i, 0)
    else:
        whole, out_map = (rows, 2 * cols), lambda i, pc: (i, pc[1])
    grid_spec = pltpu.PrefetchScalarGridSpec(
        num_scalar_prefetch=1, grid=(per,),
        in_specs=[pl.BlockSpec((n, tr, cols), lambda i, pc: (0, i, 0)),
                  pl.BlockSpec((1, tr, cols), own_map)],
        out_specs=pl.BlockSpec((tr, cols), out_map))
    return pl.pallas_call(
        body, grid_spec=grid_spec, out_shape=jax.ShapeDtypeStruct(whole, F32), name=name,
        compiler_params=_params(("parallel",)))(place, arrived, part)


def _sum_device_partials(arrived, parts, place):
    n = len(arrived)

    def body(place_ref, *refs):
        a_refs, p_refs, o_refs = refs[:n], refs[n:2 * n], refs[2 * n:]
        for a in range(n):
            acc = p_refs[a][0].astype(F32)
            for k in range(arrived[a].shape[0]):
                acc = acc + a_refs[a][k].astype(F32)
            o_refs[a][...] = acc

    grid_spec = pltpu.PrefetchScalarGridSpec(
        num_scalar_prefetch=1, grid=(1,),
        in_specs=[pl.BlockSpec(a.shape, lambda i, pc: (0, 0, 0)) for a in arrived]
        + [pl.BlockSpec((1,) + a.shape[1:], lambda i, pc: (pc[0], pc[1], 0)) for a in arrived],
        out_specs=[pl.BlockSpec(a.shape[1:], lambda i, pc: (pc[1], 0)) for a in arrived])
    return pl.pallas_call(
        body, grid_spec=grid_spec,
        out_shape=[jax.ShapeDtypeStruct((2 * a.shape[1], a.shape[2]), F32) for a in arrived],
        name="sum_device_partials", compiler_params=_params(("arbitrary",), VMEM_BIG))(place, *arrived, *parts)


def _adamw_many(ws, gs, ms, vs, tr):
    n = len(ws)
    rows = ws[0].shape[1]

    def body(*refs):
        ins, outs = refs[:4 * n], refs[4 * n:]
        for a in range(n):
            w_ref, g_ref, m_ref, v_ref = ins[4 * a:4 * a + 4]
            outs[3 * a][...], outs[3 * a + 1][...], outs[3 * a + 2][...] = _adamw_math(
                w_ref[...], g_ref[...], m_ref[...], v_ref[...])

    blk = lambda w: pl.BlockSpec((1, tr, w.shape[2]), lambda i: (0, i, 0))
    gs = [g.reshape(w.shape) for g, w in zip(gs, ws)]
    flat = [a for quad in zip(ws, gs, ms, vs) for a in quad]
    outs = pl.pallas_call(
        body, grid=(rows // tr,), in_specs=[blk(w) for w in ws for _ in range(4)],
        out_specs=[blk(w) for w in ws for _ in range(3)],
        out_shape=[jax.ShapeDtypeStruct(w.shape, F32) for w in ws for _ in range(3)], name="adamw_small_matrices",
        compiler_params=_params(("parallel",), VMEM_BIG))(*flat)
    return [(gs[a], outs[3 * a], outs[3 * a + 1], outs[3 * a + 2]) for a in range(n)]


def _add_own_half(full, other, core, tr, name, axis):
    n, rows, cols = other.shape
    per = rows // tr

    def body(c_ref, f_ref, o_ref, out_ref):
        out_ref[...] = (f_ref[...].astype(F32) + o_ref[...].astype(F32)).astype(BF16)

    full_map = (lambda k, i, c: (k, c[0] * per + i, 0)) if axis == 1 else (lambda k, i, c: (k, i, c[0]))
    grid_spec = pltpu.PrefetchScalarGridSpec(
        num_scalar_prefetch=1, grid=(n, per),
        in_specs=[pl.BlockSpec((1, tr, cols), full_map),
                  pl.BlockSpec((1, tr, cols), lambda k, i, c: (k, i, 0))],
        out_specs=pl.BlockSpec((1, tr, cols), lambda k, i, c: (k, i, 0)))
    return pl.pallas_call(
        body, grid_spec=grid_spec, out_shape=jax.ShapeDtypeStruct((n, rows, cols), BF16), name=name,
        compiler_params=_params(("parallel", "parallel"), VMEM_BIG))(core, full, other)


def _allgather8_run(x_ref, out_ref, send_sems, recv_sems, local_sem):
    m = x_ref.shape[0]
    x, y, c = _coords()
    me, sibling = (x, y, c), (x, y, 1 - c)
    chips = [(1 - x, y), (x, 1 - y), (1 - x, 1 - y)]

    def rows(px, py, pc):
        return out_ref.at[pl.ds(pl.multiple_of((4 * px + 2 * py + pc) * m, 8), m), :]

    def copy(k, blk, to, source=None):
        return pltpu.make_async_remote_copy(
            src_ref=rows(*blk) if source is None else source, dst_ref=rows(*blk),
            send_sem=send_sems.at[k], recv_sem=recv_sems.at[k], device_id=to, device_id_type=MESH)

    mine = pltpu.make_async_copy(x_ref, rows(*me), local_sem)
    mine.start()
    first = [copy(0, me, sibling, source=x_ref)]
    first += [copy(1 + j, me, (*chip, c), source=x_ref) for j, chip in enumerate(chips)]
    for cp in first:
        cp.start()
    passed = [copy(4 + j, (*chip, c), sibling) for j, chip in enumerate(chips)]
    for j, chip in enumerate(chips):
        copy(1 + j, (*chip, c), me).wait_recv()
        passed[j].start()
    copy(0, sibling, me).wait_recv()
    for j, chip in enumerate(chips):
        copy(4 + j, (*chip, 1 - c), me).wait_recv()
    for cp in first + passed:
        cp.wait_send()
    mine.wait()


ALLGATHER8_SEMS = [pltpu.SemaphoreType.DMA((7,)), pltpu.SemaphoreType.DMA((7,)), pltpu.SemaphoreType.DMA]


def _allgather8(block, name):
    m, n = block.shape
    whole = pl.BlockSpec(memory_space=pltpu.VMEM)
    return pl.pallas_call(
        functools.partial(_allgather8_run), out_shape=jax.ShapeDtypeStruct((N_DEV * m, n), block.dtype), in_specs=[whole],
        out_specs=whole, scratch_shapes=ALLGATHER8_SEMS, name=name)(block)


def _gather_plan(x_refs, out_refs, send_sems, recv_sems, local_sems):
    n = len(x_refs)
    halves = [r.shape[0] // 2 for r in x_refs]
    x, y, c = _coords()
    me, sibling = (x, y, c), (x, y, 1 - c)
    chips = [(1 - x, y), (x, 1 - y), (1 - x, 1 - y)]

    def src(a):
        return x_refs[a].at[pl.ds(pl.multiple_of(c * halves[a], 16), halves[a]), :]

    def blk(a, px, py, pc):
        return out_refs[a].at[4 * px + 2 * py + pc]

    def copy(a, k, who, to, source=None):
        return pltpu.make_async_remote_copy(
            src_ref=blk(a, *who) if source is None else source, dst_ref=blk(a, *who),
            send_sem=send_sems.at[7 * a + k], recv_sem=recv_sems.at[7 * a + k], device_id=to, device_id_type=MESH)

    def mine(a):
        return pltpu.make_async_copy(src(a), blk(a, *me), local_sems.at[a])

    def first(a):
        return ([copy(a, 0, me, sibling, source=src(a))]
                + [copy(a, 1 + j, me, (*chip, c), source=src(a)) for j, chip in enumerate(chips)])

    def begin():
        for a in range(n):
            mine(a).start()
        for a in range(n):
            for cp in first(a):
                cp.start()

    def finish():
        onward = []
        for j, chip in enumerate(chips):
            for a in range(n):
                copy(a, 1 + j, (*chip, c), me).wait_recv()
                onward.append(copy(a, 4 + j, (*chip, c), sibling))
                onward[-1].start()
        for a in range(n):
            copy(a, 0, sibling, me).wait_recv()
        for j, chip in enumerate(chips):
            for a in range(n):
                copy(a, 4 + j, (*chip, 1 - c), me).wait_recv()
        for a in range(n):
            for cp in first(a):
                cp.wait_send()
        for cp in onward:
            cp.wait_send()
        for a in range(n):
            mine(a).wait()

    return begin, finish


def _gather_operands(shards):
    n = len(shards)
    shapes = [jax.ShapeDtypeStruct((N_DEV, a.shape[0] // 2, a.shape[1]), a.dtype) for a in shards]
    sems = [pltpu.SemaphoreType.DMA((7 * n,)), pltpu.SemaphoreType.DMA((7 * n,)), pltpu.SemaphoreType.DMA((n,))]
    return shapes, sems


def _as_chip_slabs(gathered, shards):
    return [o.reshape(N_CHIP, a.shape[0], a.shape[1]) for o, a in zip(gathered, shards)]


def _gather_alongside(body, n_in, n_out, n_shards, last_step, first=None):
    def wrapped(*refs):
        ins, shards = refs[:n_in], refs[n_in:n_in + n_shards]
        rest = refs[n_in + n_shards:]
        outs, gathered = rest[:n_out], rest[n_out:n_out + n_shards]
        scratch, sems = rest[n_out + n_shards:-3], rest[-3:]

        @pl.when(pl.program_id(0) == 0)
        def _():
            if first is not None:
                first(*ins, *outs, *scratch)
            _gather_plan(shards, gathered, *sems)[0]()

        body(*ins, *outs, *scratch)

        @pl.when(pl.program_id(0) == last_step)
        def _():
            _gather_plan(shards, gathered, *sems)[1]()

    return wrapped


def _half(ref, axis, which, ndim):
    size = ref.shape[axis] // 2
    idx = [slice(None)] * ndim
    idx[axis] = pl.ds(pl.multiple_of(which * size, 8 if axis == ndim - 2 else LANE), size)
    return ref.at[tuple(idx)]


def _swap_halves_with_sibling(fulls, name, axes):
    n = len(fulls)

    def body(*refs):
        f_refs, got_refs = refs[:n], refs[n:2 * n]
        send_sems, recv_sems = refs[2 * n:]
        x, y, c = _coords()
        copies = []
        for a in range(n):
            copies.append(pltpu.make_async_remote_copy(
                src_ref=_half(f_refs[a], axes[a], 1 - c, 3), dst_ref=got_refs[a], send_sem=send_sems.at[a],
                recv_sem=recv_sems.at[a], device_id=(x, y, 1 - c), device_id_type=MESH))
        for cp in copies:
            cp.start()
        for cp in copies:
            cp.wait()

    def halved(a, axis):
        shape = list(a.shape)
        shape[axis] //= 2
        return jax.ShapeDtypeStruct(tuple(shape), a.dtype)

    return pl.pallas_call(
        body, out_shape=[halved(a, ax) for a, ax in zip(fulls, axes)],
        in_specs=[HBM_REF] * n, out_specs=[HBM_REF] * n,
        scratch_shapes=[pltpu.SemaphoreType.DMA((n,)), pltpu.SemaphoreType.DMA((n,))],
        name=name)(*fulls)


def _join_halves_with_sibling(wholes, axes, block):
    n = len(wholes)
    twice = jnp.concatenate([block, block], axis=0)
    gathered_shapes, gather_sems = _gather_operands([twice])

    def body(*refs):
        out_refs = refs[n + 1:2 * n + 1]
        send_sems, recv_sems = refs[2 * n + 2:2 * n + 4]
        begin, finish = _gather_plan([refs[n]], [refs[2 * n + 1]], *refs[2 * n + 4:])
        x, y, c = _coords()

        def push(a, core):
            half = _half(out_refs[a], axes[a] - 1, core, 2)
            return pltpu.make_async_remote_copy(
                src_ref=half, dst_ref=half, send_sem=send_sems.at[a], recv_sem=recv_sems.at[a],
                device_id=(x, y, 1 - c), device_id_type=MESH)

        begin()
        for a in range(n):
            push(a, c).start()
        finish()
        for a in range(n):
            push(a, 1 - c).wait_recv()
        for a in range(n):
            push(a, c).wait_send()

    outs = pl.pallas_call(
        body, out_shape=[jax.ShapeDtypeStruct(a.shape, a.dtype) for a in wholes] + gathered_shapes,
        in_specs=[HBM_REF] * (n + 1), out_specs=[HBM_REF] * (n + 1), input_output_aliases={a: a for a in range(n)},
        scratch_shapes=[pltpu.SemaphoreType.DMA((n,)), pltpu.SemaphoreType.DMA((n,))] + gather_sems,
        name="rs_pair_join")(*wholes, twice)
    return outs[:n], outs[n]


def _cols_to_slabs(g):
    rows, cols = g.shape
    return g.reshape(rows, N_CHIP, cols // N_CHIP).transpose(1, 0, 2)


def _slabs_to_cols(w):
    n, rows, cols = w.shape
    return w.transpose(1, 0, 2).reshape(rows, n * cols)


def _col_window(slabs, start, stop):
    n = slabs.shape[2]
    pieces = []
    for k in range(N_CHIP):
        lo, hi = max(start, k * n), min(stop, (k + 1) * n)
        if lo < hi:
            pieces.append(slabs[k][:, lo - k * n:hi - k * n])
    return pieces[0] if len(pieces) == 1 else jnp.concatenate(pieces, axis=1)


def _slabs_from_groups(groups, n):
    slabs = []
    for k in range(N_CHIP):
        pieces, off = [], 0
        for g in groups:
            lo, hi = max(k * n, off), min((k + 1) * n, off + g.shape[0])
            if lo < hi:
                pieces.append(g[lo - off:hi - off])
            off += g.shape[0]
        slabs.append(pieces[0] if len(pieces) == 1 else jnp.concatenate(pieces, axis=0))
    return jnp.stack(slabs)


def _uq_to_padded(w_uq):
    per = w_uq.reshape(RQ, H, DN + DR)
    nope = per[:, :, :DN].reshape(RQ, H * DN)
    rope = jnp.pad(per[:, :, DN:], ((0, 0), (0, 0), (0, LANE - DR))).reshape(RQ, H * LANE)
    return jnp.concatenate([nope, rope], axis=1)


def _uq_from_padded(g):
    nope = g[:, :H * DN].reshape(RQ, H, DN)
    rope = g[:, H * DN:].reshape(RQ, H, LANE)[:, :, :DR]
    return jnp.concatenate([nope, rope], axis=2).reshape(RQ, H * (DN + DR))


def _rope_tables(positions):
    inv_freq = ROPE_THETA ** (-jnp.arange(0, DR, 2, dtype=F32) / DR)
    ang = positions.astype(F32)[:, None] * inv_freq
    cos, sin = jnp.cos(ang), jnp.sin(ang)
    return jnp.tile(cos, (1, 4)), jnp.tile(jnp.concatenate([-sin, sin], axis=1), (1, 2))


def _pair_sums(fulls, core, tag, axes, tr):
    from_sibling = _swap_halves_with_sibling(fulls, f"rs_pair_swap_{tag}", axes)
    return [_add_own_half(f, o, core, min(tr, o.shape[1]), f"add_own_half_{tag}{n}", ax)
            for n, (f, o, ax) in enumerate(zip(fulls, from_sibling, axes))]


def _local_step(x, tgt, cos_t, sin_t, ada, weights, small, tiles, place):
    ts, ts_in, ts_mla, tm_nn, tm_tn, t_attn, chunk = tiles
    w_in_shard, later_shards, conv_w = weights
    norm_w, conv_b, ln_w, ln_b, q_norm_w, kv_norm_w, fnw = small
    h, mod, (g_in,) = _adaln_norm(x, norm_w, *ada, ts, [w_in_shard])
    scale, gate = mod[:, D:2 * D], mod[:, 2 * D:3 * D]
    wa = _col_window(g_in, 0, A_COLS)
    wl = jnp.pad(_col_window(g_in, A_COLS, A_COLS + L_COLS_RAW), ((0, 0), (0, L_COLS - L_COLS_RAW)))
    wg = _col_window(g_in, A_COLS + L_COLS_RAW, IN_COLS)
    proj_a = _mm_nn(h, wa, tm_nn, D, "proj_a")
    u0, u1, za, (g_uq, g_ukv, g_co, g_ao, g_o) = _conv_fwd(proj_a, conv_w, conv_b, ln_w, ln_b, ts, chunk, later_shards)
    w_uq2, w_ukv = _uq_to_padded(_slabs_to_cols(g_uq)), _slabs_to_cols(g_ukv)
    wco, wao, wo = g_co.reshape(D, D), g_ao.reshape(D, D), g_o.reshape(D, D)
    proj_l = _mm_nn(h, wl, tm_nn, L_COLS, "proj_l")
    proj_g = _mm_nn(h, wg, tm_nn, D, "proj_g")
    qn, kvn, q, k, v = _mla_prep(proj_l, q_norm_w, kv_norm_w, w_uq2, w_ukv, cos_t, sin_t, ts_mla)
    o, lse = _attn_fwd(q, k, v, t_attn)
    (dx2, dza, do, delta, dpg, zb, mg, dmo, dya, dyb, vec_mid) = _middle(
        za, o, proj_g, x, tgt, gate, fnw, wco, wao, wo, ts)
    g_wo = _mm_tn(mg, dmo, tm_tn, D, D, "grad_w_out", BF16)
    g_wco = _mm_tn(za, dya, tm_tn, D, D, "grad_w_conv_out", BF16)
    g_wao = _mm_tn(zb, dyb, tm_tn, D, D, "grad_w_attn_out", BF16)
    dq, dk, dv = _attn_bwd(q, k, v, do, lse, delta, t_attn)
    dpl, g_wuq2, g_wukv, vec_mla = _mla_prep_bwd(
        dq, dk, dv, proj_l, qn, kvn, q_norm_w, kv_norm_w, w_uq2, w_ukv, cos_t, sin_t, ts_mla)

    core = place[1:2]
    nr = D // N_CHIP
    early = [_cols_to_slabs(_uq_from_padded(g_wuq2)).astype(BF16), _cols_to_slabs(g_wukv).astype(BF16),
             g_wco.reshape(N_CHIP, nr, D), g_wao.reshape(N_CHIP, nr, D), g_wo.reshape(N_CHIP, nr, D)]
    dpa, g_conv_w, vec_conv, early_got = _conv_bwd(dza, proj_a, u0, u1, conv_w, ln_w, ln_b, ts, chunk, early)

    g_wa_t = _mm_tn(dpa, h, tm_tn, D, D, "grad_w_in_a", BF16)
    g_wl_t = _mm_tn(dpl, h, tm_tn, L_COLS, D, "grad_w_in_l", BF16)
    g_wg_t = _mm_tn(dpg, h, tm_tn, D, D, "grad_w_in_g", BF16)
    g_w_in_slabs = _slabs_from_groups([g_wa_t, g_wl_t[0:L_COLS_RAW], g_wg_t], IN_COLS // N_CHIP)
    late_sums = _pair_sums([g_w_in_slabs], core, "b", [2], IN_COLS // N_CHIP)
    grad_x, vec_in, late_got = _input_bwd(dpa, dpl, dpg, wa, wl, wg, x, dx2, norm_w, scale, ts_in, late_sums)

    col_sums = jnp.concatenate(
        [vec_in, vec_mid, vec_conv, jnp.pad(vec_mla, ((0, 0), (0, D - RQ))), g_conv_w], axis=0)
    wholes = ([_sum_chip_slabs(late_got[0], late_sums[0], place, W_IN_ROWS, "sum_chip_slabs_w_in", 2)]
              + list(_sum_device_partials(early_got, early, place)))
    shards, all_col_sums = _join_halves_with_sibling(wholes, [2] + [1] * len(early), col_sums)

    return grad_x, shards, all_col_sums


def kernel(x, c, positions, w_ada, b_ada, norm_w, w_in, conv_w, conv_b, conv_ln_w, conv_ln_b, w_conv_out, q_norm_w, w_uq, kv_norm_w, w_ukv, w_attn_out, w_out, final_norm_w, loss_target, m_w_ada, m_b_ada, m_norm_w, m_w_in, m_conv_w, m_conv_b, m_conv_ln_w, m_conv_ln_b, m_w_conv_out, m_q_norm_w, m_w_uq, m_kv_norm_w, m_w_ukv, m_w_attn_out, m_w_out, m_final_norm_w, v_w_ada, v_b_ada, v_norm_w, v_w_in, v_conv_w, v_conv_b, v_conv_ln_w, v_conv_ln_b, v_w_conv_out, v_q_norm_w, v_w_uq, v_kv_norm_w, v_w_ukv, v_w_attn_out, v_w_out, v_final_norm_w):
    ix, iy, ic = _coords()
    chip = 2 * ix + iy
    dev = 4 * ix + 2 * iy + ic
    s = x.shape[1]
    tiles = (256, 512, 512, 1024, 2048, 512, 32)

    conv_w_pad = jnp.pad(conv_w[0], ((0, HALO - KC), (0, 0)))
    small_in = jnp.concatenate([c.reshape(8, LANE), conv_w_pad.reshape(64, LANE)], axis=0)
    small_all = _allgather8(small_in, "gather_c_conv").reshape(N_DEV, 72, LANE)
    c_all = small_all[:, 0:8].reshape(N_DEV, D)
    conv_full = jnp.concatenate(
        [small_all[2 * k, 8:72].reshape(HALO, D // N_CHIP) for k in range(N_CHIP)], axis=1)

    later_shards = [w[0].astype(BF16) for w in (w_uq, w_ukv, w_conv_out, w_attn_out, w_out)]
    weights = (w_in[0].astype(BF16), later_shards, conv_full)

    ada_cols = w_ada.shape[2]
    b_shard = lax.dynamic_slice(b_ada, (0, chip * ada_cols), (1, ada_cols))
    ada = (c_all, w_ada[0], b_shard, dev.reshape(1).astype(jnp.int32))

    cos_t, sin_t = _rope_tables(positions[0])
    small = (norm_w, conv_b, conv_ln_w, conv_ln_b, q_norm_w, kv_norm_w, final_norm_w.reshape(1, D))
    place = jnp.stack([chip, ic]).astype(jnp.int32)
    grad_x, shards, gathered = _local_step(x[0], loss_target[0], cos_t, sin_t, ada, weights, small, tiles, place)
    g_w_in_s, g_w_uq_s, g_w_ukv_s, g_wco_s, g_wao_s, g_wo_s = shards

    vec_names = ("b_ada", "norm_w", "conv_b", "conv_ln_w", "conv_ln_b", "q_norm_w", "kv_norm_w", "final_norm_w")
    row = lambda a: a.reshape(1, -1)
    vectors = [(row(b_ada), row(m_b_ada), row(v_b_ada)), (norm_w, m_norm_w, v_norm_w), (conv_b, m_conv_b, v_conv_b),
               (conv_ln_w, m_conv_ln_w, v_conv_ln_w), (conv_ln_b, m_conv_ln_b, v_conv_ln_b),
               (q_norm_w, m_q_norm_w, v_q_norm_w), (kv_norm_w, m_kv_norm_w, v_kv_norm_w),
               (row(final_norm_w), row(m_final_norm_w), row(v_final_norm_w))]
    fin = _small_finalize(gathered, vectors, (conv_w, m_conv_w, v_conv_w), place[0:1])
    res = {}
    for p, (name, (w, _, _)) in enumerate(zip(vec_names, vectors)):
        shape = final_norm_w.shape if name == "final_norm_w" else w.shape
        res[name] = tuple(a.reshape(shape) for a in fin[4 * p:4 * p + 4])
    res["conv_w"] = tuple(fin[4 * len(vectors):4 * len(vectors) + 4])
    dmod_all, loss = fin[-2], fin[-1].reshape(())
    dmod_shard = lax.dynamic_slice(dmod_all, (0, chip * ada_cols), (N_DEV, ada_cols))
    g_w_ada = _ada_bwd(c_all.T, dmod_shard).reshape(1, D, ada_cols)

    def big(w, g, m, v, tr, name):
        d, nm, nv = _adamw(w, g, m, v, tr, name)
        return g.reshape(w.shape), d, nm, nv

    res["w_ada"] = big(w_ada, g_w_ada[0], m_w_ada, v_w_ada, 256, "adamw_w_ada")
    t_in = [a[0].T for a in (w_in, m_w_in, v_w_in)]
    d_t, nm_t, nv_t = _adamw(t_in[0], g_w_in_s, t_in[1], t_in[2], W_IN_ROWS, "adamw_w_in")
    res["w_in"] = tuple(a.T[None] for a in (g_w_in_s, d_t, nm_t, nv_t))
    small = _adamw_many(
        [w_uq, w_ukv, w_conv_out, w_attn_out, w_out], [g_w_uq_s, g_w_ukv_s, g_wco_s, g_wao_s, g_wo_s],
        [m_w_uq, m_w_ukv, m_w_conv_out, m_w_attn_out, m_w_out], [v_w_uq, v_w_ukv, v_w_conv_out, v_w_attn_out, v_w_out],
        128)
    res["w_uq"], res["w_ukv"], res["w_conv_out"], res["w_attn_out"], res["w_out"] = small

    order = ("w_ada", "b_ada", "norm_w", "w_in", "conv_w", "conv_b", "conv_ln_w", "conv_ln_b", "w_conv_out",
             "q_norm_w", "w_uq", "kv_norm_w", "w_ukv", "w_attn_out", "w_out", "final_norm_w")
    outs = [loss, grad_x[None]]
    for slot in range(4):
        outs += [res[name][slot] for name in order]
    return tuple(outs)
```

```python
import functools

import numpy as np
import jax
import jax.numpy as jnp
from jax import lax
from jax.experimental import pallas as pl
from jax.experimental.pallas import tpu as pltpu

F32 = jnp.float32
BF16 = jnp.bfloat16
MESH = pl.DeviceIdType.MESH

D = 1024
H = 8
DN = 128
DR = 64
RQ = 256
KC = 31
HALO = 32
EPS = 1e-6
ROPE_THETA = 10000.0
N_CHIP = 4
N_DEV = 8
LANE = 128
VMEM_BIG = 56 * 1024 * 1024

ADAM_LR = 0.001
ADAM_B1 = 0.9
ADAM_B2 = 0.999
ADAM_EPS = 1e-08
ADAM_WD = 0.01
ADAM_STEP = 10

A_COLS = 3 * D
L_COLS_RAW = RQ + RQ + DR
L_COLS = 640
G_COLS = 3 * D
IN_COLS = A_COLS + L_COLS_RAW + G_COLS


def _params(sem=None, vmem=None):
    kw = {}
    if sem is not None:
        kw["dimension_semantics"] = sem
    if vmem is not None:
        kw["vmem_limit_bytes"] = vmem
    return pltpu.CompilerParams(**kw)


def _dot(a, b):
    return jnp.dot(a, b, preferred_element_type=F32)


def _dot_nt(a, b):
    return lax.dot_general(a, b, (((1,), (1,)), ((), ())), preferred_element_type=F32)


def _dot_tn(a, b):
    return lax.dot_general(a, b, (((0,), (0,)), ((), ())), preferred_element_type=F32)


def _colsum(v):
    return jnp.sum(v, axis=0, keepdims=True)


def _rowmean(v):
    return jnp.mean(v, axis=-1, keepdims=True)


def _sigmoid(v):
    return jax.nn.sigmoid(v)


def _dsilu(v, s):
    return s * (1.0 + v * (1.0 - s))


def _swap_halves(v, first_half):
    return jnp.where(first_half, pltpu.roll(v, 96, 1), pltpu.roll(v, 32, 1))


def _first_half_mask(rows):
    lane = lax.broadcasted_iota(jnp.int32, (rows, LANE), 1)
    return (lane % 64) < 32


SMALL_IN_ROWS = 8 + HALO


def _adaln_norm(x, norm_w, small_in, w_ada_shard, b_ada_shard, dev, ts, shards):
    s = x.shape[0]
    cols = w_ada_shard.shape[1]
    taps = D // N_CHIP

    def modulation(dev_ref, x_ref, nw_ref, sm_ref, w_ref, b_ref, h_ref, mod_ref, c_ref, conv_ref,
                   part_sc, all_sc, small_sc, *sems):
        _allgather8_run(sm_ref, small_sc, *sems[0:3])
        for k in range(N_DEV):
            c_ref[k:k + 1, :] = small_sc[SMALL_IN_ROWS * k:SMALL_IN_ROWS * k + 1, :]
        for k in range(N_CHIP):
            base = SMALL_IN_ROWS * 2 * k + 8
            conv_ref[:, taps * k:taps * (k + 1)] = small_sc[base:base + HALO, 0:taps]
        cv = c_ref[...]
        part_sc[...] = jnp.dot(cv * _sigmoid(cv), w_ref[...], preferred_element_type=F32,
                               precision=lax.Precision.HIGHEST) + b_ref[...]
        _allgather8_run(part_sc, all_sc, *sems[3:6])
        for k in range(N_CHIP):
            mod_ref[:, cols * k:cols * (k + 1)] = all_sc[pl.ds(2 * N_DEV * k + dev_ref[0], 1), :]

    def body(dev_ref, x_ref, nw_ref, sm_ref, w_ref, b_ref, h_ref, mod_ref, *rest):
        xv = x_ref[...]
        r = lax.rsqrt(_rowmean(xv * xv) + EPS)
        y = xv * r * nw_ref[...]
        h_ref[...] = (y * (1.0 + mod_ref[:, D:2 * D]) + mod_ref[:, 0:D]).astype(BF16)

    row = pl.BlockSpec((ts, D), lambda i: (i, 0))
    const = lambda shape: pl.BlockSpec(shape, lambda i: (0, 0))
    n = len(shards)
    gathered_shapes, sems = _gather_operands(shards)
    outs = pl.pallas_call(
        _gather_alongside(body, 6, 4, n, s // ts - 1, modulation), grid=(s // ts,),
        in_specs=[pl.BlockSpec(memory_space=pltpu.SMEM), row, const((1, D)), const(small_in.shape),
                  const(w_ada_shard.shape), const((1, cols))] + [HBM_REF] * n,
        out_specs=[row, const((1, 3 * D)), const((N_DEV, D)), const((HALO, D))] + [HBM_REF] * n,
        out_shape=[jax.ShapeDtypeStruct((s, D), BF16), jax.ShapeDtypeStruct((1, 3 * D), F32),
                   jax.ShapeDtypeStruct((N_DEV, D), F32), jax.ShapeDtypeStruct((HALO, D), F32)] + gathered_shapes,
        scratch_shapes=[pltpu.VMEM((N_DEV, cols), F32), pltpu.VMEM((N_DEV * N_DEV, cols), F32),
                        pltpu.VMEM((N_DEV * SMALL_IN_ROWS, D), F32)] + ALLGATHER8_SEMS + ALLGATHER8_SEMS + sems,
        name="adaln_norm", compiler_params=_params(("arbitrary",), VMEM_BIG))(
            dev, x, norm_w, small_in, w_ada_shard, b_ada_shard, *shards)
    return outs[0], outs[1], outs[2], outs[3], _as_chip_slabs(outs[4:], shards)


def _mm_nn(a, b, tm, tn, name):
    m, k = a.shape
    n = b.shape[1]

    def body(a_ref, b_ref, o_ref):
        o_ref[...] = _dot(a_ref[...], b_ref[...])

    return pl.pallas_call(
        body, grid=(n // tn, m // tm),
        in_specs=[pl.BlockSpec((tm, k), lambda j, i: (i, 0)), pl.BlockSpec((k, tn), lambda j, i: (0, j))],
        out_specs=pl.BlockSpec((tm, tn), lambda j, i: (i, j)),
        out_shape=jax.ShapeDtypeStruct((m, n), F32), name=name,
        compiler_params=_params(("parallel", "parallel"), VMEM_BIG))(a, b)


def _mm_tn(a, b, tm, tk, tn, name, out_dtype=F32):
    m, k = a.shape
    n = b.shape[1]
    steps = m // tm

    def body(a_ref, b_ref, o_ref, acc_ref):
        @pl.when(pl.program_id(2) == 0)
        def _():
            acc_ref[...] = jnp.zeros_like(acc_ref)
        acc_ref[...] += _dot_tn(a_ref[...], b_ref[...])

        @pl.when(pl.program_id(2) == steps - 1)
        def _():
            o_ref[...] = acc_ref[...].astype(out_dtype)

    return pl.pallas_call(
        body, grid=(k // tk, n // tn, steps),
        in_specs=[pl.BlockSpec((tm, tk), lambda r, j, i: (i, r)), pl.BlockSpec((tm, tn), lambda r, j, i: (i, j))],
        out_specs=pl.BlockSpec((tk, tn), lambda r, j, i: (r, j)),
        out_shape=jax.ShapeDtypeStruct((k, n), out_dtype), scratch_shapes=[pltpu.VMEM((tk, tn), F32)], name=name,
        compiler_params=_params(("parallel", "parallel", "arbitrary"), VMEM_BIG))(a, b)


def _mm_tn_stack(a, b, tm, name):
    n_stack, m, k = a.shape
    n = b.shape[2]
    steps = m // tm

    def body(a_ref, b_ref, o_ref, acc_ref):
        @pl.when(pl.program_id(1) == 0)
        def _():
            acc_ref[...] = jnp.zeros_like(acc_ref)
        acc_ref[...] += _dot_tn(a_ref[0], b_ref[0])

        @pl.when(pl.program_id(1) == steps - 1)
        def _():
            o_ref[0] = acc_ref[...].astype(BF16)

    out = pl.pallas_call(
        body, grid=(n_stack, steps),
        in_specs=[pl.BlockSpec((1, tm, k), lambda g, i: (g, i, 0)), pl.BlockSpec((1, tm, n), lambda g, i: (g, i, 0))],
        out_specs=pl.BlockSpec((1, k, n), lambda g, i: (g, 0, 0)),
        out_shape=jax.ShapeDtypeStruct((n_stack, k, n), BF16), scratch_shapes=[pltpu.VMEM((k, n), F32)], name=name,
        compiler_params=_params(("parallel", "arbitrary"), VMEM_BIG))(a, b)
    return [out[g] for g in range(n_stack)]


def _coords():
    return lax.axis_index("x"), lax.axis_index("y"), lax.axis_index("c")


HBM_REF = pl.BlockSpec(memory_space=pl.ANY)


def _chip_scatter_copies(p_refs, got_refs, send_sems, recv_sems):
    x, y, c = _coords()
    copies = []
    for a in range(len(p_refs)):
        for j, (px, py) in enumerate([(1 - x, y), (x, 1 - y), (1 - x, 1 - y)]):
            copies.append(pltpu.make_async_remote_copy(
                src_ref=p_refs[a].at[2 * px + py], dst_ref=got_refs[a].at[j], send_sem=send_sems.at[3 * a + j],
                recv_sem=recv_sems.at[3 * a + j], device_id=(px, py, c), device_id_type=MESH))
    return copies


RELATIONS = [(dx, dy, dc) for dx in (0, 1) for dy in (0, 1) for dc in (0, 1)][1:]


def _device_scatter_copies(p_refs, got_refs, send_sems, recv_sems):
    x, y, c = _coords()
    copies = []
    for a in range(len(p_refs)):
        half = p_refs[a].shape[1] // 2
        for j, (dx, dy, dc) in enumerate(RELATIONS):
            px, py, pc = (1 - x if dx else x), (1 - y if dy else y), (1 - c if dc else c)
            src = p_refs[a].at[2 * px + py, pl.ds(pl.multiple_of(pc * half, 16), half), :]
            copies.append(pltpu.make_async_remote_copy(
                src_ref=src, dst_ref=got_refs[a].at[j], send_sem=send_sems.at[7 * a + j],
                recv_sem=recv_sems.at[7 * a + j], device_id=(px, py, pc), device_id_type=MESH))
    return copies


def _scatter_alongside(body, n_in, n_out, n_parts, last_step, make_copies):
    def wrapped(*refs):
        ins, parts = refs[:n_in], refs[n_in:n_in + n_parts]
        rest = refs[n_in + n_parts:]
        outs, got = rest[:n_out], rest[n_out:n_out + n_parts]
        scratch, (send_sems, recv_sems) = rest[n_out + n_parts:-2], rest[-2:]

        @pl.when(pl.program_id(0) == 0)
        def _():
            for cp in make_copies(parts, got, send_sems, recv_sems):
                cp.start()

        body(*ins, *outs, *scratch)

        @pl.when(pl.program_id(0) == last_step)
        def _():
            for cp in make_copies(parts, got, send_sems, recv_sems):
                cp.wait()

    return wrapped


def _scatter_operands(parts, per_device):
    n = len(parts)
    if per_device:
        slots, shapes = 7, [jax.ShapeDtypeStruct((7, a.shape[1] // 2, a.shape[2]), a.dtype) for a in parts]
    else:
        slots, shapes = 3, [jax.ShapeDtypeStruct((3,) + a.shape[1:], a.dtype) for a in parts]
    sems = [pltpu.SemaphoreType.DMA((slots * n,)), pltpu.SemaphoreType.DMA((slots * n,))]
    return [HBM_REF] * n, [HBM_REF] * n, shapes, sems


def _shifted_copies(win_ref, sh_ref, rows):
    for p in range(1, 8):
        sh_ref[p - 1, 0:rows, :] = win_ref[pl.ds(p, rows), :]


def _tap_rows(win_ref, sh_ref, start, rows):
    p = start % 8
    if p == 0:
        return win_ref[pl.ds(start, rows), :]
    return sh_ref[p - 1, pl.ds(start - p, rows), :]


def _conv_taps(win_ref, sh_ref, w_ref, rows, chunk, offset_of_tap):
    pieces = []
    for c0 in range(0, rows, chunk):
        acc = None
        for j in range(KC):
            term = w_ref[j:j + 1, :] * _tap_rows(win_ref, sh_ref, c0 + offset_of_tap(j), chunk)
            acc = term if acc is None else acc + term
        pieces.append(acc)
    return pieces


def _conv_fwd(proj_a, conv_w, conv_b, ln_w, ln_b, ts, chunk, shards):
    s = proj_a.shape[0]

    def body(av_ref, al_ref, ag_ref, w_ref, b_ref, lw_ref, lb_ref, u0_ref, u1_ref, za_ref, win_ref, sh_ref):
        @pl.when(pl.program_id(0) == 0)
        def _():
            win_ref[0:HALO, :] = jnp.zeros((HALO, D), F32)

        u0 = av_ref[...] * _sigmoid(al_ref[...])
        u0_ref[...] = u0
        win_ref[HALO:HALO + ts, :] = u0
        _shifted_copies(win_ref, sh_ref, ts + HALO - 8)
        pieces = _conv_taps(win_ref, sh_ref, w_ref, ts, chunk, lambda j: HALO - (KC - 1) + j)
        for n, acc in enumerate(pieces):
            u1_ref[n * chunk:(n + 1) * chunk, :] = acc + b_ref[...]
        win_ref[0:HALO, :] = win_ref[ts:ts + HALO, :]

        u1 = u1_ref[...]
        xc = u1 - _rowmean(u1)
        rstd = lax.rsqrt(_rowmean(xc * xc) + EPS)
        u2 = xc * rstd * lw_ref[...] + lb_ref[...]
        u3 = u2 * _sigmoid(u2)
        ag = ag_ref[...]
        za_ref[...] = (u3 * (ag * _sigmoid(ag))).astype(BF16)

    col = lambda c: pl.BlockSpec((ts, D), lambda i, c=c: (i, c))
    row = pl.BlockSpec((ts, D), lambda i: (i, 0))
    vec = pl.BlockSpec((1, D), lambda i: (0, 0))
    n = len(shards)
    gathered_shapes, sems = _gather_operands(shards)
    outs = pl.pallas_call(
        _gather_alongside(body, 7, 3, n, s // ts - 1), grid=(s // ts,),
        in_specs=[col(0), col(1), col(2), pl.BlockSpec((HALO, D), lambda i: (0, 0)), vec, vec, vec] + [HBM_REF] * n,
        out_specs=[row, row, row] + [HBM_REF] * n,
        out_shape=[jax.ShapeDtypeStruct((s, D), F32), jax.ShapeDtypeStruct((s, D), F32),
                   jax.ShapeDtypeStruct((s, D), BF16)] + gathered_shapes,
        scratch_shapes=[pltpu.VMEM((ts + HALO, D), F32), pltpu.VMEM((7, ts + HALO, D), F32)] + sems,
        name="conv_fwd", compiler_params=_params(("arbitrary",), VMEM_BIG))(
            proj_a, proj_a, proj_a, conv_w, conv_b, ln_w, ln_b, *shards)
    return outs[0], outs[1], outs[2], _as_chip_slabs(outs[3:], shards)


def _conv_bwd(dza, proj_a, u0, u1, conv_w, ln_w, ln_b, ts, chunk, parts):
    s = dza.shape[0]
    nt = s // ts
    per = ts // HALO

    def body(dza_ref, av_ref, al_ref, ag_ref, u0_ref, u0p_ref, u1_ref, w_ref, lw_ref, lb_ref,
             dpa_ref, gw_ref, gv_ref, dwin_ref, uwin_ref, du0_ref, gwp_ref, dsh_ref, ush_ref):
        step = pl.program_id(0)
        tile = nt - 1 - step

        @pl.when(step == 0)
        def _():
            dwin_ref[ts:ts + HALO, :] = jnp.zeros((HALO, D), F32)
            gwp_ref[...] = jnp.zeros_like(gwp_ref)
            gv_ref[...] = jnp.zeros_like(gv_ref)

        ag = ag_ref[...]
        sg = _sigmoid(ag)
        u1 = u1_ref[...]
        xc = u1 - _rowmean(u1)
        rstd = lax.rsqrt(_rowmean(xc * xc) + EPS)
        xh = xc * rstd
        u2 = xh * lw_ref[...] + lb_ref[...]
        s2 = _sigmoid(u2)
        dz = dza_ref[...]
        du3 = dz * (ag * sg)
        dpa_ref[:, 2 * D:3 * D] = (dz * (u2 * s2) * _dsilu(ag, sg)).astype(BF16)
        du2 = du3 * _dsilu(u2, s2)
        gv_ref[0:1, :] += _colsum(du2 * xh)
        gv_ref[1:2, :] += _colsum(du2)
        dxh = du2 * lw_ref[...]
        du1 = rstd * (dxh - _rowmean(dxh) - xh * _rowmean(dxh * xh))
        gv_ref[2:3, :] += _colsum(du1)
        dwin_ref[0:ts, :] = du1

        uwin_ref[0:HALO, :] = jnp.where(tile == 0, 0.0, u0p_ref[...])
        uwin_ref[HALO:HALO + ts, :] = u0_ref[...]

        _shifted_copies(dwin_ref, dsh_ref, ts + HALO - 8)
        _shifted_copies(uwin_ref, ush_ref, ts + HALO - 8)
        pieces = _conv_taps(dwin_ref, dsh_ref, w_ref, ts, chunk, lambda j: (KC - 1) - j)
        for n, acc in enumerate(pieces):
            du0_ref[n * chunk:(n + 1) * chunk, :] = acc
        for c0 in range(0, ts, chunk):
            dchunk = dwin_ref[c0:c0 + chunk, :]
            for j in range(KC):
                prod = dchunk * _tap_rows(uwin_ref, ush_ref, c0 + HALO - (KC - 1) + j, chunk)
                gwp_ref[8 * j:8 * j + 8, :] += jnp.sum(prod.reshape(chunk // 8, 8, D), axis=0)
        dwin_ref[ts:ts + HALO, :] = dwin_ref[0:HALO, :]

        du0 = du0_ref[...]
        al = al_ref[...]
        sl = _sigmoid(al)
        dpa_ref[:, 0:D] = (du0 * sl).astype(BF16)
        dpa_ref[:, D:2 * D] = (du0 * av_ref[...] * sl * (1.0 - sl)).astype(BF16)

        @pl.when(step == nt - 1)
        def _():
            for j in range(KC):
                gw_ref[j:j + 1, :] = _colsum(gwp_ref[8 * j:8 * j + 8, :])
            gw_ref[KC:HALO, :] = jnp.zeros((HALO - KC, D), F32)

    rev = lambda i: nt - 1 - i
    col = lambda c: pl.BlockSpec((ts, D), lambda i, c=c: (rev(i), c))
    row = pl.BlockSpec((ts, D), lambda i: (rev(i), 0))
    vec = pl.BlockSpec((1, D), lambda i: (0, 0))
    halo = pl.BlockSpec((HALO, D), lambda i: (jnp.maximum(rev(i) * per - 1, 0), 0))
    side_in, side_out, side_shapes, side_sems = _scatter_operands(parts, True)
    outs = pl.pallas_call(
        _scatter_alongside(body, 10, 3, len(parts), nt - 1, _device_scatter_copies), grid=(nt,),
        in_specs=[row, col(0), col(1), col(2), row, halo, row, pl.BlockSpec((HALO, D), lambda i: (0, 0)), vec, vec]
        + side_in,
        out_specs=[pl.BlockSpec((ts, A_COLS), lambda i: (rev(i), 0)),
                   pl.BlockSpec((HALO, D), lambda i: (0, 0)), pl.BlockSpec((8, D), lambda i: (0, 0))] + side_out,
        out_shape=[jax.ShapeDtypeStruct((s, A_COLS), BF16), jax.ShapeDtypeStruct((HALO, D), F32),
                   jax.ShapeDtypeStruct((8, D), F32)] + side_shapes,
        scratch_shapes=[pltpu.VMEM((ts + HALO, D), F32), pltpu.VMEM((ts + HALO, D), F32),
                        pltpu.VMEM((ts, D), F32), pltpu.VMEM((8 * HALO, D), F32),
                        pltpu.VMEM((7, ts + HALO, D), F32), pltpu.VMEM((7, ts + HALO, D), F32)] + side_sems,
        name="conv_bwd", compiler_params=_params(("arbitrary",), VMEM_BIG))(
            dza, proj_a, proj_a, proj_a, u0, u0, u1, conv_w, ln_w, ln_b, *parts)
    return outs[0], outs[1], outs[2], list(outs[3:])


def _mla_prep(proj_l, q_norm_w, kv_norm_w, w_uq2, w_ukv, cos_t, sin_t, ts):
    s = proj_l.shape[0]

    def body(pl_ref, qw_ref, kw_ref, wq_ref, wkv_ref, c_ref, s_ref, qn_ref, kvn_ref, q_ref, k_ref, v_ref):
        first = _first_half_mask(ts)
        cs = c_ref[...]
        sn = s_ref[...]

        def rms(v, w):
            return v * lax.rsqrt(_rowmean(v * v) + EPS) * w

        def rope(v):
            return v * cs + _swap_halves(v, first) * sn

        qn = rms(pl_ref[:, 0:RQ], qw_ref[...]).astype(BF16)
        kvn = rms(pl_ref[:, RQ:2 * RQ], kw_ref[...]).astype(BF16)
        qn_ref[...] = qn
        kvn_ref[...] = kvn
        q = _dot(qn, wq_ref[...])
        kv = _dot(kvn, wkv_ref[...])
        kr = rope(pl_ref[:, 2 * RQ:2 * RQ + LANE]).astype(BF16)
        for h in range(H):
            q_ref[h, :, 0:DN] = q[:, DN * h:DN * (h + 1)].astype(BF16)
            q_ref[h, :, DN:2 * DN] = rope(q[:, H * DN + LANE * h:H * DN + LANE * (h + 1)]).astype(BF16)
            k_ref[h, :, 0:DN] = kv[:, 2 * DN * h:2 * DN * h + DN].astype(BF16)
            k_ref[h, :, DN:2 * DN] = kr
            v_ref[h, :, 0:DN] = kv[:, 2 * DN * h + DN:2 * DN * (h + 1)].astype(BF16)
            v_ref[h, :, DN:2 * DN] = jnp.ones((ts, DN), BF16)

    const = lambda shape: pl.BlockSpec(shape, lambda i: (0,) * len(shape))
    rowb = lambda w: pl.BlockSpec((ts, w), lambda i: (i, 0))
    head = lambda w: pl.BlockSpec((H, ts, w), lambda i: (0, i, 0))
    return pl.pallas_call(
        body, grid=(s // ts,),
        in_specs=[rowb(L_COLS), const((1, RQ)), const((1, RQ)), const((RQ, 2 * H * DN)), const((RQ, 2 * H * DN)),
                  rowb(LANE), rowb(LANE)],
        out_specs=[rowb(RQ), rowb(RQ), head(2 * DN), head(2 * DN), head(2 * DN)],
        out_shape=[jax.ShapeDtypeStruct((s, RQ), BF16), jax.ShapeDtypeStruct((s, RQ), BF16),
                   jax.ShapeDtypeStruct((H, s, 2 * DN), BF16), jax.ShapeDtypeStruct((H, s, 2 * DN), BF16),
                   jax.ShapeDtypeStruct((H, s, 2 * DN), BF16)],
        name="mla_prep", compiler_params=_params(("parallel",), VMEM_BIG))(
            proj_l, q_norm_w, kv_norm_w, w_uq2, w_ukv, cos_t, sin_t)


def _mla_prep_bwd(dq, dk, dv, proj_l, qn, kvn, q_norm_w, kv_norm_w, w_uq2, w_ukv, cos_t, sin_t, ts):
    s = proj_l.shape[0]

    def body(dq_ref, dk_ref, dv_ref, pl_ref, qn_ref, kvn_ref, qw_ref, kw_ref, wq_ref, wkv_ref, c_ref, s_ref,
             dpl_ref, gwq_ref, gwkv_ref, gv_ref, dq2_ref, dkv2_ref):
        @pl.when(pl.program_id(0) == 0)
        def _():
            gwq_ref[...] = jnp.zeros_like(gwq_ref)
            gwkv_ref[...] = jnp.zeros_like(gwkv_ref)
            gv_ref[...] = jnp.zeros_like(gv_ref)

        first = _first_half_mask(ts)
        cs = c_ref[...] * ATT_SCALE
        sn = s_ref[...] * ATT_SCALE

        def rope_bwd(g):
            return g * cs + _swap_halves(g * sn, first)

        def rms_bwd(v, w, dy):
            r = lax.rsqrt(_rowmean(v * v) + EPS)
            vh = v * r
            dvh = dy * w
            return r * (dvh - vh * _rowmean(dvh * vh)), _colsum(dy * vh)

        dkr = None
        for h in range(H):
            dq2_ref[:, DN * h:DN * (h + 1)] = (dq_ref[h, :, 0:DN] * ATT_SCALE).astype(BF16)
            dq2_ref[:, H * DN + LANE * h:H * DN + LANE * (h + 1)] = rope_bwd(dq_ref[h, :, DN:2 * DN]).astype(BF16)
            dkv2_ref[:, 2 * DN * h:2 * DN * h + DN] = (dk_ref[h, :, 0:DN] * ATT_SCALE).astype(BF16)
            dkv2_ref[:, 2 * DN * h + DN:2 * DN * (h + 1)] = dv_ref[h].astype(BF16)
            part = dk_ref[h, :, DN:2 * DN]
            dkr = part if dkr is None else dkr + part

        dq2 = dq2_ref[...]
        dkv2 = dkv2_ref[...]
        gwq_ref[...] += _dot_tn(qn_ref[...], dq2)
        gwkv_ref[...] += _dot_tn(kvn_ref[...], dkv2)
        dcq, gq = rms_bwd(pl_ref[:, 0:RQ], qw_ref[...], _dot_nt(dq2, wq_ref[...]))
        dckv, gkv = rms_bwd(pl_ref[:, RQ:2 * RQ], kw_ref[...], _dot_nt(dkv2, wkv_ref[...]))
        gv_ref[0:1, :] += gq
        gv_ref[1:2, :] += gkv
        dpl_ref[:, 0:RQ] = dcq.astype(BF16)
        dpl_ref[:, RQ:2 * RQ] = dckv.astype(BF16)
        dpl_ref[:, 2 * RQ:2 * RQ + LANE] = rope_bwd(dkr).astype(BF16)

    const = lambda shape: pl.BlockSpec(shape, lambda i: (0,) * len(shape))
    rowb = lambda w: pl.BlockSpec((ts, w), lambda i: (i, 0))
    head = lambda w: pl.BlockSpec((H, ts, w), lambda i: (0, i, 0))
    return pl.pallas_call(
        body, grid=(s // ts,),
        in_specs=[head(2 * DN), head(2 * DN), head(DN), rowb(L_COLS), rowb(RQ), rowb(RQ), const((1, RQ)),
                  const((1, RQ)), const((RQ, 2 * H * DN)), const((RQ, 2 * H * DN)), rowb(LANE), rowb(LANE)],
        out_specs=[rowb(L_COLS), const((RQ, 2 * H * DN)), const((RQ, 2 * H * DN)), const((8, RQ))],
        out_shape=[jax.ShapeDtypeStruct((s, L_COLS), BF16), jax.ShapeDtypeStruct((RQ, 2 * H * DN), F32),
                   jax.ShapeDtypeStruct((RQ, 2 * H * DN), F32), jax.ShapeDtypeStruct((8, RQ), F32)],
        scratch_shapes=[pltpu.VMEM((ts, 2 * H * DN), BF16), pltpu.VMEM((ts, 2 * H * DN), BF16)],
        name="mla_prep_bwd", compiler_params=_params(("arbitrary",), VMEM_BIG))(
            dq, dk, dv, proj_l, qn, kvn, q_norm_w, kv_norm_w, w_uq2, w_ukv, cos_t, sin_t)


def _causal_pairs(n, by_key):
    if by_key:
        pairs = [(i, j) for j in range(n) for i in range(j, n)]
    else:
        pairs = [(i, j) for i in range(n) for j in range(i + 1)]
    return (jnp.asarray(np.array([p[0] for p in pairs], np.int32)),
            jnp.asarray(np.array([p[1] for p in pairs], np.int32)))


ATT_SCALE = float((DN + DR) ** -0.5)
LOG2E = 1.4426950408889634
LN2 = 0.6931471805599453
ATT_HEADS_FWD = 4
ATT_HEADS = 2
W_IN_ROWS = 336
ATT_ROWS = 64


def _diag_width(r0, t):
    return min(t, -(-(r0 + ATT_ROWS) // LANE) * LANE)


def _diag_mask_rows(r0, width):
    rows = r0 + lax.broadcasted_iota(jnp.int32, (ATT_ROWS, width), 0)
    cols = lax.broadcasted_iota(jnp.int32, (ATT_ROWS, width), 1)
    return cols <= rows


def _diag_mask(t):
    return lax.broadcasted_iota(jnp.int32, (t, t), 1) <= lax.broadcasted_iota(jnp.int32, (t, t), 0)


def _attn_fwd(q, k, v, t):
    s = q.shape[1]
    n = s // t
    scale2 = float((DN + DR) ** -0.5) * LOG2E
    qi, ki = _causal_pairs(n, by_key=False)

    def body(qi_ref, ki_ref, q_ref, k_ref, v_ref, o_ref, lse_ref, *scratch):
        per_head = [scratch[5 * h:5 * h + 5] for h in range(ATT_HEADS_FWD)]
        p = pl.program_id(1)
        i = qi_ref[p]
        j = ki_ref[p]

        @pl.when(j == 0)
        def _():
            for m_sc, acc_sc, _, _, _ in per_head:
                m_sc[...] = jnp.full_like(m_sc, -jnp.inf)
                acc_sc[...] = jnp.zeros_like(acc_sc)

        def scores(h, diag):
            sc = _dot_nt(q_ref[h], k_ref[h])
            if diag:
                sc = jnp.where(_diag_mask(t), sc, -jnp.inf)
            per_head[h][2][...] = sc

        def rowmax(h, rows):
            per_head[h][4][rows, :] = jnp.max(per_head[h][2][rows, :], axis=-1, keepdims=True)

        def stats(h):
            m_sc, acc_sc, _, _, mx_sc = per_head[h]
            m_prev = m_sc[...]
            m_new = jnp.maximum(m_prev, mx_sc[...] * scale2)
            m_sc[...] = m_new
            acc_sc[...] = jnp.exp2(m_prev - m_new) * acc_sc[...]

        def probs(h, rows):
            m_sc, _, s_sc, p_sc, _ = per_head[h]
            p_sc[rows, :] = jnp.exp2(s_sc[rows, :] * scale2 - m_sc[rows, :]).astype(BF16)

        def values(h):
            _, acc_sc, _, p_sc, _ = per_head[h]
            acc_sc[...] += _dot(p_sc[...], v_ref[h])

        def step(diag):
            blocks = [slice(r0, r0 + ATT_ROWS) for r0 in range(0, t, ATT_ROWS)]
            for h in range(ATT_HEADS_FWD):
                scores(h, diag)
            for rows in blocks:
                rowmax(0, rows)
            stats(0)
            for h in range(ATT_HEADS_FWD):
                for rows in blocks:
                    probs(h, rows)
                    if h + 1 < ATT_HEADS_FWD:
                        rowmax(h + 1, rows)
                if h + 1 < ATT_HEADS_FWD:
                    stats(h + 1)
                values(h)

        @pl.when(j < i)
        def _():
            step(False)

        @pl.when(j == i)
        def _():
            step(True)
            for h, (m_sc, acc_sc, _, _, _) in enumerate(per_head):
                l = acc_sc[:, DN:2 * DN]
                o_ref[:, DN * h:DN * (h + 1)] = acc_sc[:, 0:DN] / l
                lse_ref[h] = (m_sc[...] + jnp.log2(l[:, 0:1])) * LN2

    hb = ATT_HEADS_FWD
    grid_spec = pltpu.PrefetchScalarGridSpec(
        num_scalar_prefetch=2, grid=(H // hb, int(qi.shape[0])),
        in_specs=[pl.BlockSpec((hb, t, 2 * DN), lambda h, p, qi, ki: (h, qi[p], 0)),
                  pl.BlockSpec((hb, t, 2 * DN), lambda h, p, qi, ki: (h, ki[p], 0)),
                  pl.BlockSpec((hb, t, 2 * DN), lambda h, p, qi, ki: (h, ki[p], 0))],
        out_specs=[pl.BlockSpec((t, hb * DN), lambda h, p, qi, ki: (qi[p], h)),
                   pl.BlockSpec((hb, t, 1), lambda h, p, qi, ki: (h, qi[p], 0))],
        scratch_shapes=[pltpu.VMEM((t, 1), F32), pltpu.VMEM((t, 2 * DN), F32), pltpu.VMEM((t, t), F32),
                        pltpu.VMEM((t, t), BF16), pltpu.VMEM((t, 1), F32)] * hb)
    return pl.pallas_call(
        body, grid_spec=grid_spec,
        out_shape=[jax.ShapeDtypeStruct((s, H * DN), F32), jax.ShapeDtypeStruct((H, s, 1), F32)],
        name="attn_fwd", compiler_params=_params(("parallel", "arbitrary"), VMEM_BIG))(qi, ki, q, k, v)


def _attn_bwd(q, k, v, do, lse, delta, t):
    s = q.shape[1]
    n = s // t
    scale = ATT_SCALE
    qi, ki = _causal_pairs(n, by_key=True)

    def body(qi_ref, ki_ref, q_ref, k_ref, v_ref, do_ref, lse_ref, dl_ref, dq_ref, dk_ref, dv_ref,
             dk_sc, dv_sc, s_sc, dp_sc, p_sc, ds_sc):
        p = pl.program_id(1)
        i = qi_ref[p]
        j = ki_ref[p]

        @pl.when(p == 0)
        def _():
            dq_ref[...] = jnp.zeros_like(dq_ref)

        @pl.when(i == j)
        def _():
            dk_sc[...] = jnp.zeros_like(dk_sc)
            dv_sc[...] = jnp.zeros_like(dv_sc)

        def step(diag):
            for h in range(ATT_HEADS):
                s_sc[h] = _dot_nt(q_ref[h], k_ref[h])
                dp_sc[h] = _dot_nt(do_ref[:, DN * h:DN * (h + 1)], v_ref[h, :, 0:DN])
            for h in range(ATT_HEADS):
                for r0 in range(0, t, ATT_ROWS):
                    rows = slice(r0, r0 + ATT_ROWS)
                    width = _diag_width(r0, t) if diag else t
                    sc = s_sc[h, rows, 0:width] * (scale * LOG2E)
                    if diag:
                        sc = jnp.where(_diag_mask_rows(r0, width), sc, -jnp.inf)
                    pr = jnp.exp2(sc - lse_ref[h, rows, :] * LOG2E)
                    ds = pr * (dp_sc[h, rows, 0:width] - dl_ref[h, rows, :])
                    p_sc[h, rows, 0:width] = pr.astype(BF16)
                    ds_sc[h, rows, 0:width] = ds.astype(BF16)
                    if width < t:
                        p_sc[h, rows, width:t] = jnp.zeros((ATT_ROWS, t - width), BF16)
                        ds_sc[h, rows, width:t] = jnp.zeros((ATT_ROWS, t - width), BF16)
            q_rows = pl.ds(pl.multiple_of(i * t, t), t)
            for h in range(ATT_HEADS):
                dv_sc[h] += _dot_tn(p_sc[h], do_ref[:, DN * h:DN * (h + 1)])
                dk_sc[h] += _dot_tn(ds_sc[h], q_ref[h])
                dq_ref[h, q_rows, :] += _dot(ds_sc[h], k_ref[h])

        @pl.when(i > j)
        def _():
            step(False)

        @pl.when(i == j)
        def _():
            step(True)

        @pl.when(i == n - 1)
        def _():
            dk_ref[...] = dk_sc[...]
            dv_ref[...] = dv_sc[...]

    hb = ATT_HEADS
    grid_spec = pltpu.PrefetchScalarGridSpec(
        num_scalar_prefetch=2, grid=(H // hb, int(qi.shape[0])),
        in_specs=[pl.BlockSpec((hb, t, 2 * DN), lambda h, p, qi, ki: (h, qi[p], 0)),
                  pl.BlockSpec((hb, t, 2 * DN), lambda h, p, qi, ki: (h, ki[p], 0)),
                  pl.BlockSpec((hb, t, 2 * DN), lambda h, p, qi, ki: (h, ki[p], 0)),
                  pl.BlockSpec((t, hb * DN), lambda h, p, qi, ki: (qi[p], h)),
                  pl.BlockSpec((hb, t, 1), lambda h, p, qi, ki: (h, qi[p], 0)),
                  pl.BlockSpec((hb, t, 1), lambda h, p, qi, ki: (h, qi[p], 0))],
        out_specs=[pl.BlockSpec((hb, s, 2 * DN), lambda h, p, qi, ki: (h, 0, 0)),
                   pl.BlockSpec((hb, t, 2 * DN), lambda h, p, qi, ki: (h, ki[p], 0)),
                   pl.BlockSpec((hb, t, DN), lambda h, p, qi, ki: (h, ki[p], 0))],
        scratch_shapes=[pltpu.VMEM((hb, t, 2 * DN), F32), pltpu.VMEM((hb, t, DN), F32),
                        pltpu.VMEM((hb, t, t), F32), pltpu.VMEM((hb, t, t), F32),
                        pltpu.VMEM((hb, t, t), BF16), pltpu.VMEM((hb, t, t), BF16)])
    return pl.pallas_call(
        body, grid_spec=grid_spec,
        out_shape=[jax.ShapeDtypeStruct((H, s, 2 * DN), F32), jax.ShapeDtypeStruct((H, s, 2 * DN), F32),
                   jax.ShapeDtypeStruct((H, s, DN), F32)],
        name="attn_bwd", compiler_params=_params(("parallel", "arbitrary"), VMEM_BIG))(
            qi, ki, q, k, v, do, lse, delta)


def _middle(za, o, proj_g, x, tgt, gate, fnw, wco, wao, wo, ts):
    s = x.shape[0]
    inv_d = 1.0 / D

    def body(za_ref, o_ref, bg_ref, ga_ref, gb_ref, x_ref, t_ref, gate_ref, fnw_ref, wco_ref, wao_ref, wo_ref,
             dx2_ref, dza_ref, do_ref, dl_ref, dpg_ref, lhs_ref, rhs_ref, vec_ref):
        @pl.when(pl.program_id(0) == 0)
        def _():
            vec_ref[...] = jnp.zeros_like(vec_ref)

        ov = o_ref[...]
        bg = bg_ref[...]
        sb = _sigmoid(bg)
        silu_b = bg * sb
        zb = (ov * silu_b).astype(BF16)
        lhs_ref[0] = za_ref[...]
        lhs_ref[1] = zb
        ya = _dot(za_ref[...], wco_ref[...])
        yb = _dot(zb, wao_ref[...])
        sa = _sigmoid(ga_ref[...])
        sg = _sigmoid(gb_ref[...])
        mg = (sa * ya + sg * yb).astype(BF16)
        lhs_ref[2] = mg
        mo = _dot(mg, wo_ref[...])
        gate_v = gate_ref[...]
        x2 = x_ref[...] + gate_v * mo
        r = lax.rsqrt(_rowmean(x2 * x2) + EPS)
        xh = x2 * r
        fw = fnw_ref[...]
        e = xh * fw - t_ref[...]
        vec_ref[2:3, :] += _colsum(e * e)
        dy = e * inv_d
        vec_ref[0:1, :] += _colsum(dy * xh)
        dxh = dy * fw
        dx2 = r * (dxh - xh * _rowmean(dxh * xh))
        dx2_ref[...] = dx2
        vec_ref[1:2, :] += _colsum(dx2 * mo)
        dmo = (gate_v * dx2).astype(BF16)
        rhs_ref[2] = dmo
        dmg = _dot_nt(dmo, wo_ref[...])
        dya = (sa * dmg).astype(BF16)
        dyb = (sg * dmg).astype(BF16)
        rhs_ref[0] = dya
        rhs_ref[1] = dyb
        dpg_ref[:, D:2 * D] = (dmg * ya * (sa * (1.0 - sa))).astype(BF16)
        dpg_ref[:, 2 * D:3 * D] = (dmg * yb * (sg * (1.0 - sg))).astype(BF16)
        dza_ref[...] = _dot_nt(dya, wco_ref[...])
        dzb = _dot_nt(dyb, wao_ref[...])
        dov = dzb * silu_b
        do_ref[...] = dov.astype(BF16)
        dpg_ref[:, 0:D] = (dzb * ov * _dsilu(bg, sb)).astype(BF16)
        dprod = dov * ov
        for h in range(H):
            dl_ref[h] = jnp.sum(dprod[:, DN * h:DN * (h + 1)], axis=-1, keepdims=True)

    col = lambda c: pl.BlockSpec((ts, D), lambda i, c=c: (i, c))
    row = pl.BlockSpec((ts, D), lambda i: (i, 0))
    vec = pl.BlockSpec((1, D), lambda i: (0, 0))
    wsp = pl.BlockSpec((D, D), lambda i: (0, 0))
    stack = pl.BlockSpec((3, ts, D), lambda i: (0, i, 0))
    bf = jax.ShapeDtypeStruct((s, D), BF16)
    ff = jax.ShapeDtypeStruct((s, D), F32)
    return pl.pallas_call(
        body, grid=(s // ts,),
        in_specs=[row, row, col(0), col(1), col(2), row, row, vec, vec, wsp, wsp, wsp],
        out_specs=[row, row, row, pl.BlockSpec((H, ts, 1), lambda i: (0, i, 0)),
                   pl.BlockSpec((ts, G_COLS), lambda i: (i, 0)), stack, stack,
                   pl.BlockSpec((8, D), lambda i: (0, 0))],
        out_shape=[ff, ff, bf, jax.ShapeDtypeStruct((H, s, 1), F32), jax.ShapeDtypeStruct((s, G_COLS), BF16),
                   jax.ShapeDtypeStruct((3, s, D), BF16), jax.ShapeDtypeStruct((3, s, D), BF16),
                   jax.ShapeDtypeStruct((8, D), F32)],
        name="middle", compiler_params=_params(("arbitrary",), VMEM_BIG))(
            za, o, proj_g, proj_g, proj_g, x, tgt, gate, fnw, wco, wao, wo)


def _input_bwd(dpa, dpl, dpg, wa, wl, wg, x, dx2, norm_w, scale, ts, parts):
    s = x.shape[0]

    def body(dpa_ref, dpl_ref, dpg_ref, wa_ref, wl_ref, wg_ref, x_ref, dx2_ref, nw_ref, sc_ref, gx_ref, gv_ref):
        @pl.when(pl.program_id(0) == 0)
        def _():
            gv_ref[...] = jnp.zeros_like(gv_ref)

        dh = (_dot_nt(dpa_ref[...], wa_ref[...]) + _dot_nt(dpl_ref[...], wl_ref[...])
              + _dot_nt(dpg_ref[...], wg_ref[...]))
        xv = x_ref[...]
        r = lax.rsqrt(_rowmean(xv * xv) + EPS)
        xh = xv * r
        nw = nw_ref[...]
        gv_ref[0:1, :] += _colsum(dh)
        gv_ref[1:2, :] += _colsum(dh * (xh * nw))
        dy = dh * (1.0 + sc_ref[...])
        gv_ref[2:3, :] += _colsum(dy * xh)
        dxh = dy * nw
        gx_ref[...] = dx2_ref[...] + r * (dxh - xh * _rowmean(dxh * xh))

    const = lambda shape: pl.BlockSpec(shape, lambda i: (0, 0))
    rowb = lambda w: pl.BlockSpec((ts, w), lambda i: (i, 0))
    side_in, side_out, side_shapes, side_sems = _scatter_operands(parts, False)
    outs = pl.pallas_call(
        _scatter_alongside(body, 10, 2, len(parts), s // ts - 1, _chip_scatter_copies), grid=(s // ts,),
        in_specs=[rowb(A_COLS), rowb(L_COLS), rowb(G_COLS), const((D, A_COLS)), const((D, L_COLS)),
                  const((D, G_COLS)), rowb(D), rowb(D), const((1, D)), const((1, D))] + side_in,
        out_specs=[rowb(D), const((8, D))] + side_out,
        out_shape=[jax.ShapeDtypeStruct((s, D), F32), jax.ShapeDtypeStruct((8, D), F32)] + side_shapes,
        scratch_shapes=side_sems,
        name="input_bwd", compiler_params=_params(("arbitrary",), VMEM_BIG))(
            dpa, dpl, dpg, wa, wl, wg, x, dx2, norm_w, scale, *parts)
    return outs[0], outs[1], list(outs[2:])


def _adamw_math(w, g, m, v):
    nm = ADAM_B1 * m + (1.0 - ADAM_B1) * g
    nv = ADAM_B2 * v + (1.0 - ADAM_B2) * (g * g)
    m_hat = nm / (1.0 - ADAM_B1 ** ADAM_STEP)
    v_hat = nv / (1.0 - ADAM_B2 ** ADAM_STEP)
    return -ADAM_LR * (m_hat / (jnp.sqrt(v_hat) + ADAM_EPS) + ADAM_WD * w), nm, nv


def _adamw(w, g, m, v, tr, name):
    lead, (rows, cols) = w.shape[:-2], w.shape[-2:]

    def body(w_ref, g_ref, m_ref, v_ref, d_ref, nm_ref, nv_ref):
        d_ref[...], nm_ref[...], nv_ref[...] = _adamw_math(w_ref[...], g_ref[...], m_ref[...], v_ref[...])

    blk = pl.BlockSpec((1,) * len(lead) + (tr, cols), lambda i: (0,) * len(lead) + (i, 0))
    shp = jax.ShapeDtypeStruct(w.shape, F32)
    return pl.pallas_call(
        body, grid=(rows // tr,), in_specs=[blk] * 4, out_specs=[blk] * 3, out_shape=[shp] * 3, name=name,
        compiler_params=_params(("parallel",), VMEM_BIG))(w, g.reshape(w.shape), m, v)


ROW_SHIFT, ROW_SCALE, ROW_NORM_W = 0, 1, 2
ROW_FINAL_NORM_W, ROW_GATE, ROW_LOSS = 8, 9, 10
ROW_LN_W, ROW_LN_B, ROW_CONV_B = 16, 17, 18
ROW_Q_NORM_W, ROW_KV_NORM_W = 24, 25
ROW_CONV_W = 32
SUM_ROWS = 64
VECTOR_ROWS = ((ROW_SHIFT, ROW_SCALE, ROW_GATE), (ROW_NORM_W,), (ROW_CONV_B,), (ROW_LN_W,), (ROW_LN_B,),
               (ROW_Q_NORM_W,), (ROW_KV_NORM_W,), (ROW_FINAL_NORM_W,))


def _small_finalize(gathered, vectors, conv, chip):
    n = len(vectors)
    cw = conv[0].shape[2]

    def body(chip_ref, g_ref, *refs):
        ins, outs = refs[:3 * n + 3], refs[3 * n + 3:]
        tot = g_ref[0]
        for k in range(1, N_DEV):
            tot = tot + g_ref[k]
        for p, rows in enumerate(VECTOR_ROWS):
            w_ref, m_ref, v_ref = ins[3 * p:3 * p + 3]
            g_out, d_out, nm_out, nv_out = outs[4 * p:4 * p + 4]
            width = w_ref.shape[1] // len(rows)
            for q, r in enumerate(rows):
                lanes = slice(q * width, (q + 1) * width)
                g = tot[r:r + 1, 0:width]
                g_out[:, lanes] = g
                d_out[:, lanes], nm_out[:, lanes], nv_out[:, lanes] = _adamw_math(
                    w_ref[:, lanes], g, m_ref[:, lanes], v_ref[:, lanes])
        cols = pl.ds(pl.multiple_of(chip_ref[0] * cw, LANE), cw)
        gc = g_ref[0, pl.ds(ROW_CONV_W, KC), cols]
        for k in range(1, N_DEV):
            gc = gc + g_ref[k, pl.ds(ROW_CONV_W, KC), cols]
        cw_ref, cm_ref, cv_ref = ins[3 * n:3 * n + 3]
        g_out, d_out, nm_out, nv_out, dmod_ref, loss_ref = outs[4 * n:]
        g_out[0] = gc
        d_out[0], nm_out[0], nv_out[0] = _adamw_math(cw_ref[0], gc, cm_ref[0], cv_ref[0])
        for k in range(N_DEV):
            for q, r in enumerate((ROW_SHIFT, ROW_SCALE, ROW_GATE)):
                dmod_ref[k:k + 1, q * D:(q + 1) * D] = g_ref[k, r:r + 1, :]
        loss_ref[...] = (0.5 / D) * jnp.sum(tot[ROW_LOSS:ROW_LOSS + 1, :], axis=-1, keepdims=True)

    flat_in = [a for triple in vectors for a in triple] + list(conv)
    shapes = [jax.ShapeDtypeStruct(w.shape, F32) for w, _, _ in vectors for _ in range(4)]
    shapes += [jax.ShapeDtypeStruct(conv[0].shape, F32)] * 4
    shapes += [jax.ShapeDtypeStruct((N_DEV, 3 * D), F32), jax.ShapeDtypeStruct((1, 1), F32)]
    whole = pl.BlockSpec(memory_space=pltpu.VMEM)
    return pl.pallas_call(
        body, out_shape=shapes,
        in_specs=[pl.BlockSpec(memory_space=pltpu.SMEM)] + [whole] * (1 + len(flat_in)),
        out_specs=[whole] * len(shapes), name="small_finalize")(chip, gathered, *flat_in)


def _ada_bwd(c_all_t, dmod_shard):
    def body(c_ref, d_ref, o_ref):
        cv = c_ref[...]
        o_ref[...] = jnp.dot(cv * _sigmoid(cv), d_ref[...], preferred_element_type=F32,
                             precision=lax.Precision.HIGHEST)

    return pl.pallas_call(
        body, out_shape=jax.ShapeDtypeStruct((D, dmod_shard.shape[1]), F32), name="ada_bwd")(c_all_t, dmod_shard)


def _sum_chip_slabs(arrived, part, place, tr, name, axis):
    n, rows, cols = arrived.shape
    per = rows // tr
    own_map = ((lambda i, pc: (pc[0], i, 0)) if part.shape[1] == rows
               else (lambda i, pc: (pc[0], pc[1] * per + i, 0)))

    def body(place_ref, a_ref, p_ref, o_ref):
        acc = p_ref[0].astype(F32)
        for k in range(n):
            acc = acc + a_ref[k].astype(F32)
        o_ref[...] = acc

    if axis == 1:
        whole, out_map = (2 * rows, cols), lambda i, pc: (pc[1] * per + i, 0)
    else:
        whole, out_map = (rows, 2 * cols), lambda i, pc: (i, pc[1])
    grid_spec = pltpu.PrefetchScalarGridSpec(
        num_scalar_prefetch=1, grid=(per,),
        in_specs=[pl.BlockSpec((n, tr, cols), lambda i, pc: (0, i, 0)),
                  pl.BlockSpec((1, tr, cols), own_map)],
        out_specs=pl.BlockSpec((tr, cols), out_map))
    return pl.pallas_call(
        body, grid_spec=grid_spec, out_shape=jax.ShapeDtypeStruct(whole, F32), name=name,
        compiler_params=_params(("parallel",)))(place, arrived, part)


def _sum_device_partials(arrived, parts, place):
    n = len(arrived)

    def body(place_ref, *refs):
        a_refs, p_refs, o_refs = refs[:n], refs[n:2 * n], refs[2 * n:]
        for a in range(n):
            acc = p_refs[a][0].astype(F32)
            for k in range(arrived[a].shape[0]):
                acc = acc + a_refs[a][k].astype(F32)
            o_refs[a][...] = acc

    grid_spec = pltpu.PrefetchScalarGridSpec(
        num_scalar_prefetch=1, grid=(1,),
        in_specs=[pl.BlockSpec(a.shape, lambda i, pc: (0, 0, 0)) for a in arrived]
        + [pl.BlockSpec((1,) + a.shape[1:], lambda i, pc: (pc[0], pc[1], 0)) for a in arrived],
        out_specs=[pl.BlockSpec(a.shape[1:], lambda i, pc: (pc[1], 0)) for a in arrived])
    return pl.pallas_call(
        body, grid_spec=grid_spec,
        out_shape=[jax.ShapeDtypeStruct((2 * a.shape[1], a.shape[2]), F32) for a in arrived],
        name="sum_device_partials", compiler_params=_params(("arbitrary",), VMEM_BIG))(place, *arrived, *parts)


def _adamw_many(ws, gs, ms, vs, tr):
    n = len(ws)
    rows = ws[0].shape[1]

    def body(*refs):
        ins, outs = refs[:4 * n], refs[4 * n:]
        for a in range(n):
            w_ref, g_ref, m_ref, v_ref = ins[4 * a:4 * a + 4]
            outs[3 * a][...], outs[3 * a + 1][...], outs[3 * a + 2][...] = _adamw_math(
                w_ref[...], g_ref[...], m_ref[...], v_ref[...])

    blk = lambda w: pl.BlockSpec((1, tr, w.shape[2]), lambda i: (0, i, 0))
    gs = [g.reshape(w.shape) for g, w in zip(gs, ws)]
    flat = [a for quad in zip(ws, gs, ms, vs) for a in quad]
    outs = pl.pallas_call(
        body, grid=(rows // tr,), in_specs=[blk(w) for w in ws for _ in range(4)],
        out_specs=[blk(w) for w in ws for _ in range(3)],
        out_shape=[jax.ShapeDtypeStruct(w.shape, F32) for w in ws for _ in range(3)], name="adamw_small_matrices",
        compiler_params=_params(("parallel",), VMEM_BIG))(*flat)
    return [(gs[a], outs[3 * a], outs[3 * a + 1], outs[3 * a + 2]) for a in range(n)]


def _add_own_half(full, other, core, tr, name, axis):
    n, rows, cols = other.shape
    per = rows // tr

    def body(c_ref, f_ref, o_ref, out_ref):
        out_ref[...] = (f_ref[...].astype(F32) + o_ref[...].astype(F32)).astype(BF16)

    full_map = (lambda k, i, c: (k, c[0] * per + i, 0)) if axis == 1 else (lambda k, i, c: (k, i, c[0]))
    grid_spec = pltpu.PrefetchScalarGridSpec(
        num_scalar_prefetch=1, grid=(n, per),
        in_specs=[pl.BlockSpec((1, tr, cols), full_map),
                  pl.BlockSpec((1, tr, cols), lambda k, i, c: (k, i, 0))],
        out_specs=pl.BlockSpec((1, tr, cols), lambda k, i, c: (k, i, 0)))
    return pl.pallas_call(
        body, grid_spec=grid_spec, out_shape=jax.ShapeDtypeStruct((n, rows, cols), BF16), name=name,
        compiler_params=_params(("parallel", "parallel"), VMEM_BIG))(core, full, other)


def _allgather8_run(x_ref, out_ref, send_sems, recv_sems, local_sem):
    m = x_ref.shape[0]
    x, y, c = _coords()
    me, sibling = (x, y, c), (x, y, 1 - c)
    chips = [(1 - x, y), (x, 1 - y), (1 - x, 1 - y)]

    def rows(px, py, pc):
        return out_ref.at[pl.ds(pl.multiple_of((4 * px + 2 * py + pc) * m, 8), m), :]

    def copy(k, blk, to, source=None):
        return pltpu.make_async_remote_copy(
            src_ref=rows(*blk) if source is None else source, dst_ref=rows(*blk),
            send_sem=send_sems.at[k], recv_sem=recv_sems.at[k], device_id=to, device_id_type=MESH)

    mine = pltpu.make_async_copy(x_ref, rows(*me), local_sem)
    mine.start()
    first = [copy(0, me, sibling, source=x_ref)]
    first += [copy(1 + j, me, (*chip, c), source=x_ref) for j, chip in enumerate(chips)]
    for cp in first:
        cp.start()
    passed = [copy(4 + j, (*chip, c), sibling) for j, chip in enumerate(chips)]
    for j, chip in enumerate(chips):
        copy(1 + j, (*chip, c), me).wait_recv()
        passed[j].start()
    copy(0, sibling, me).wait_recv()
    for j, chip in enumerate(chips):
        copy(4 + j, (*chip, 1 - c), me).wait_recv()
    for cp in first + passed:
        cp.wait_send()
    mine.wait()


ALLGATHER8_SEMS = [pltpu.SemaphoreType.DMA((7,)), pltpu.SemaphoreType.DMA((7,)), pltpu.SemaphoreType.DMA]


def _gather_plan(x_refs, out_refs, send_sems, recv_sems, local_sems):
    n = len(x_refs)
    halves = [r.shape[0] // 2 for r in x_refs]
    x, y, c = _coords()
    me, sibling = (x, y, c), (x, y, 1 - c)
    chips = [(1 - x, y), (x, 1 - y), (1 - x, 1 - y)]

    def src(a):
        return x_refs[a].at[pl.ds(pl.multiple_of(c * halves[a], 16), halves[a]), :]

    def blk(a, px, py, pc):
        return out_refs[a].at[4 * px + 2 * py + pc]

    def copy(a, k, who, to, source=None):
        return pltpu.make_async_remote_copy(
            src_ref=blk(a, *who) if source is None else source, dst_ref=blk(a, *who),
            send_sem=send_sems.at[7 * a + k], recv_sem=recv_sems.at[7 * a + k], device_id=to, device_id_type=MESH)

    def mine(a):
        return pltpu.make_async_copy(src(a), blk(a, *me), local_sems.at[a])

    def first(a):
        return ([copy(a, 0, me, sibling, source=src(a))]
                + [copy(a, 1 + j, me, (*chip, c), source=src(a)) for j, chip in enumerate(chips)])

    def begin():
        for a in range(n):
            mine(a).start()
        for a in range(n):
            for cp in first(a):
                cp.start()

    def finish():
        onward = []
        for j, chip in enumerate(chips):
            for a in range(n):
                copy(a, 1 + j, (*chip, c), me).wait_recv()
                onward.append(copy(a, 4 + j, (*chip, c), sibling))
                onward[-1].start()
        for a in range(n):
            copy(a, 0, sibling, me).wait_recv()
        for j, chip in enumerate(chips):
            for a in range(n):
                copy(a, 4 + j, (*chip, 1 - c), me).wait_recv()
        for a in range(n):
            for cp in first(a):
                cp.wait_send()
        for cp in onward:
            cp.wait_send()
        for a in range(n):
            mine(a).wait()

    return begin, finish


def _gather_operands(shards):
    n = len(shards)
    shapes = [jax.ShapeDtypeStruct((N_DEV, a.shape[0] // 2, a.shape[1]), a.dtype) for a in shards]
    sems = [pltpu.SemaphoreType.DMA((7 * n,)), pltpu.SemaphoreType.DMA((7 * n,)), pltpu.SemaphoreType.DMA((n,))]
    return shapes, sems


def _as_chip_slabs(gathered, shards):
    return [o.reshape(N_CHIP, a.shape[0], a.shape[1]) for o, a in zip(gathered, shards)]


def _gather_alongside(body, n_in, n_out, n_shards, last_step, first=None):
    def wrapped(*refs):
        ins, shards = refs[:n_in], refs[n_in:n_in + n_shards]
        rest = refs[n_in + n_shards:]
        outs, gathered = rest[:n_out], rest[n_out:n_out + n_shards]
        scratch, sems = rest[n_out + n_shards:-3], rest[-3:]

        @pl.when(pl.program_id(0) == 0)
        def _():
            if first is not None:
                first(*ins, *outs, *scratch)
            _gather_plan(shards, gathered, *sems)[0]()

        body(*ins, *outs, *scratch)

        @pl.when(pl.program_id(0) == last_step)
        def _():
            _gather_plan(shards, gathered, *sems)[1]()

    return wrapped


def _half(ref, axis, which, ndim):
    size = ref.shape[axis] // 2
    idx = [slice(None)] * ndim
    idx[axis] = pl.ds(pl.multiple_of(which * size, 8 if axis == ndim - 2 else LANE), size)
    return ref.at[tuple(idx)]


def _swap_halves_with_sibling(fulls, name, axes):
    n = len(fulls)

    def body(*refs):
        f_refs, got_refs = refs[:n], refs[n:2 * n]
        send_sems, recv_sems = refs[2 * n:]
        x, y, c = _coords()
        copies = []
        for a in range(n):
            copies.append(pltpu.make_async_remote_copy(
                src_ref=_half(f_refs[a], axes[a], 1 - c, 3), dst_ref=got_refs[a], send_sem=send_sems.at[a],
                recv_sem=recv_sems.at[a], device_id=(x, y, 1 - c), device_id_type=MESH))
        for cp in copies:
            cp.start()
        for cp in copies:
            cp.wait()

    def halved(a, axis):
        shape = list(a.shape)
        shape[axis] //= 2
        return jax.ShapeDtypeStruct(tuple(shape), a.dtype)

    return pl.pallas_call(
        body, out_shape=[halved(a, ax) for a, ax in zip(fulls, axes)],
        in_specs=[HBM_REF] * n, out_specs=[HBM_REF] * n,
        scratch_shapes=[pltpu.SemaphoreType.DMA((n,)), pltpu.SemaphoreType.DMA((n,))],
        name=name)(*fulls)


def _join_halves_with_sibling(wholes, axes, block):
    n = len(wholes)
    twice = jnp.concatenate([block, block], axis=0)
    gathered_shapes, gather_sems = _gather_operands([twice])

    def body(*refs):
        out_refs = refs[n + 1:2 * n + 1]
        send_sems, recv_sems = refs[2 * n + 2:2 * n + 4]
        begin, finish = _gather_plan([refs[n]], [refs[2 * n + 1]], *refs[2 * n + 4:])
        x, y, c = _coords()

        def push(a, core):
            half = _half(out_refs[a], axes[a] - 1, core, 2)
            return pltpu.make_async_remote_copy(
                src_ref=half, dst_ref=half, send_sem=send_sems.at[a], recv_sem=recv_sems.at[a],
                device_id=(x, y, 1 - c), device_id_type=MESH)

        begin()
        for a in range(n):
            push(a, c).start()
        finish()
        for a in range(n):
            push(a, 1 - c).wait_recv()
        for a in range(n):
            push(a, c).wait_send()

    outs = pl.pallas_call(
        body, out_shape=[jax.ShapeDtypeStruct(a.shape, a.dtype) for a in wholes] + gathered_shapes,
        in_specs=[HBM_REF] * (n + 1), out_specs=[HBM_REF] * (n + 1), input_output_aliases={a: a for a in range(n)},
        scratch_shapes=[pltpu.SemaphoreType.DMA((n,)), pltpu.SemaphoreType.DMA((n,))] + gather_sems,
        name="rs_pair_join")(*wholes, twice)
    return outs[:n], outs[n]


def _cols_to_slabs(g):
    rows, cols = g.shape
    return g.reshape(rows, N_CHIP, cols // N_CHIP).transpose(1, 0, 2)


def _slabs_to_cols(w):
    n, rows, cols = w.shape
    return w.transpose(1, 0, 2).reshape(rows, n * cols)


def _col_window(slabs, start, stop):
    n = slabs.shape[2]
    pieces = []
    for k in range(N_CHIP):
        lo, hi = max(start, k * n), min(stop, (k + 1) * n)
        if lo < hi:
            pieces.append(slabs[k][:, lo - k * n:hi - k * n])
    return pieces[0] if len(pieces) == 1 else jnp.concatenate(pieces, axis=1)


def _slabs_from_groups(groups, n):
    slabs = []
    for k in range(N_CHIP):
        pieces, off = [], 0
        for g in groups:
            lo, hi = max(k * n, off), min((k + 1) * n, off + g.shape[0])
            if lo < hi:
                pieces.append(g[lo - off:hi - off])
            off += g.shape[0]
        slabs.append(pieces[0] if len(pieces) == 1 else jnp.concatenate(pieces, axis=0))
    return jnp.stack(slabs)


def _uq_to_padded(w_uq):
    per = w_uq.reshape(RQ, H, DN + DR)
    nope = per[:, :, :DN].reshape(RQ, H * DN)
    rope = jnp.pad(per[:, :, DN:], ((0, 0), (0, 0), (0, LANE - DR))).reshape(RQ, H * LANE)
    return jnp.concatenate([nope, rope], axis=1)


def _uq_from_padded(g):
    nope = g[:, :H * DN].reshape(RQ, H, DN)
    rope = g[:, H * DN:].reshape(RQ, H, LANE)[:, :, :DR]
    return jnp.concatenate([nope, rope], axis=2).reshape(RQ, H * (DN + DR))


def _rope_tables(positions):
    inv_freq = ROPE_THETA ** (-jnp.arange(0, DR, 2, dtype=F32) / DR)
    ang = positions.astype(F32)[:, None] * inv_freq
    cos, sin = jnp.cos(ang), jnp.sin(ang)
    return jnp.tile(cos, (1, 4)), jnp.tile(jnp.concatenate([-sin, sin], axis=1), (1, 2))


def _pair_sums(fulls, core, tag, axes, tr):
    from_sibling = _swap_halves_with_sibling(fulls, f"rs_pair_swap_{tag}", axes)
    return [_add_own_half(f, o, core, min(tr, o.shape[1]), f"add_own_half_{tag}{n}", ax)
            for n, (f, o, ax) in enumerate(zip(fulls, from_sibling, axes))]


def _local_step(x, tgt, cos_t, sin_t, ada, weights, small, tiles, place):
    ts, ts_in, ts_mla, tm_nn, tm_tn, t_attn, chunk = tiles
    w_in_shard, later_shards = weights
    norm_w, conv_b, ln_w, ln_b, q_norm_w, kv_norm_w, fnw = small
    h, mod, c_all, conv_w, (g_in,) = _adaln_norm(x, norm_w, *ada, ts, [w_in_shard])
    scale, gate = mod[:, D:2 * D], mod[:, 2 * D:3 * D]
    wa = _col_window(g_in, 0, A_COLS)
    wl = jnp.pad(_col_window(g_in, A_COLS, A_COLS + L_COLS_RAW), ((0, 0), (0, L_COLS - L_COLS_RAW)))
    wg = _col_window(g_in, A_COLS + L_COLS_RAW, IN_COLS)
    proj_a = _mm_nn(h, wa, tm_nn, D, "proj_a")
    u0, u1, za, (g_uq, g_ukv, g_co, g_ao, g_o) = _conv_fwd(proj_a, conv_w, conv_b, ln_w, ln_b, ts, chunk, later_shards)
    w_uq2, w_ukv = _uq_to_padded(_slabs_to_cols(g_uq)), _slabs_to_cols(g_ukv)
    wco, wao, wo = g_co.reshape(D, D), g_ao.reshape(D, D), g_o.reshape(D, D)
    proj_l = _mm_nn(h, wl, tm_nn, L_COLS, "proj_l")
    proj_g = _mm_nn(h, wg, tm_nn, D, "proj_g")
    qn, kvn, q, k, v = _mla_prep(proj_l, q_norm_w, kv_norm_w, w_uq2, w_ukv, cos_t, sin_t, ts_mla)
    o, lse = _attn_fwd(q, k, v, t_attn)
    dx2, dza, do, delta, dpg, lhs3, rhs3, vec_mid = _middle(za, o, proj_g, x, tgt, gate, fnw, wco, wao, wo, ts)
    g_wco, g_wao, g_wo = _mm_tn_stack(lhs3, rhs3, tm_tn, "grad_w_out3")
    dq, dk, dv = _attn_bwd(q, k, v, do, lse, delta, t_attn)
    dpl, g_wuq2, g_wukv, vec_mla = _mla_prep_bwd(
        dq, dk, dv, proj_l, qn, kvn, q_norm_w, kv_norm_w, w_uq2, w_ukv, cos_t, sin_t, ts_mla)

    core = place[1:2]
    nr = D // N_CHIP
    early = [_cols_to_slabs(_uq_from_padded(g_wuq2)).astype(BF16), _cols_to_slabs(g_wukv).astype(BF16),
             g_wco.reshape(N_CHIP, nr, D), g_wao.reshape(N_CHIP, nr, D), g_wo.reshape(N_CHIP, nr, D)]
    dpa, g_conv_w, vec_conv, early_got = _conv_bwd(dza, proj_a, u0, u1, conv_w, ln_w, ln_b, ts, chunk, early)

    g_wa_t = _mm_tn(dpa, h, tm_tn, D, D, "grad_w_in_a", BF16)
    g_wl_t = _mm_tn(dpl, h, tm_tn, L_COLS, D, "grad_w_in_l", BF16)
    g_wg_t = _mm_tn(dpg, h, tm_tn, D, D, "grad_w_in_g", BF16)
    g_w_in_slabs = _slabs_from_groups([g_wa_t, g_wl_t[0:L_COLS_RAW], g_wg_t], IN_COLS // N_CHIP)
    late_sums = _pair_sums([g_w_in_slabs], core, "b", [2], IN_COLS // N_CHIP)
    grad_x, vec_in, late_got = _input_bwd(dpa, dpl, dpg, wa, wl, wg, x, dx2, norm_w, scale, ts_in, late_sums)

    col_sums = jnp.concatenate(
        [vec_in, vec_mid, vec_conv, jnp.pad(vec_mla, ((0, 0), (0, D - RQ))), g_conv_w], axis=0)
    wholes = ([_sum_chip_slabs(late_got[0], late_sums[0], place, W_IN_ROWS, "sum_chip_slabs_w_in", 2)]
              + list(_sum_device_partials(early_got, early, place)))
    shards, all_col_sums = _join_halves_with_sibling(wholes, [2] + [1] * len(early), col_sums)

    return grad_x, shards, all_col_sums, c_all


def kernel(x, c, positions, w_ada, b_ada, norm_w, w_in, conv_w, conv_b, conv_ln_w, conv_ln_b, w_conv_out, q_norm_w, w_uq, kv_norm_w, w_ukv, w_attn_out, w_out, final_norm_w, loss_target, m_w_ada, m_b_ada, m_norm_w, m_w_in, m_conv_w, m_conv_b, m_conv_ln_w, m_conv_ln_b, m_w_conv_out, m_q_norm_w, m_w_uq, m_kv_norm_w, m_w_ukv, m_w_attn_out, m_w_out, m_final_norm_w, v_w_ada, v_b_ada, v_norm_w, v_w_in, v_conv_w, v_conv_b, v_conv_ln_w, v_conv_ln_b, v_w_conv_out, v_q_norm_w, v_w_uq, v_kv_norm_w, v_w_ukv, v_w_attn_out, v_w_out, v_final_norm_w):
    ix, iy, ic = _coords()
    chip = 2 * ix + iy
    dev = 4 * ix + 2 * iy + ic
    s = x.shape[1]
    tiles = (256, 512, 512, 1024, 2048, 512, 32)

    conv_w_pad = jnp.pad(conv_w[0], ((0, HALO - KC), (0, D - conv_w.shape[2])))
    small_in = jnp.concatenate([jnp.pad(c, ((0, 7), (0, 0))), conv_w_pad], axis=0)

    later_shards = [w[0].astype(BF16) for w in (w_uq, w_ukv, w_conv_out, w_attn_out, w_out)]
    weights = (w_in[0].astype(BF16), later_shards)

    ada_cols = w_ada.shape[2]
    b_shard = lax.dynamic_slice(b_ada, (0, chip * ada_cols), (1, ada_cols))
    ada = (small_in, w_ada[0], b_shard, dev.reshape(1).astype(jnp.int32))

    cos_t, sin_t = _rope_tables(positions[0])
    small = (norm_w, conv_b, conv_ln_w, conv_ln_b, q_norm_w, kv_norm_w, final_norm_w.reshape(1, D))
    place = jnp.stack([chip, ic]).astype(jnp.int32)
    grad_x, shards, gathered, c_all = _local_step(x[0], loss_target[0], cos_t, sin_t, ada, weights, small, tiles, place)
    g_w_in_s, g_w_uq_s, g_w_ukv_s, g_wco_s, g_wao_s, g_wo_s = shards

    vec_names = ("b_ada", "norm_w", "conv_b", "conv_ln_w", "conv_ln_b", "q_norm_w", "kv_norm_w", "final_norm_w")
    row = lambda a: a.reshape(1, -1)
    vectors = [(row(b_ada), row(m_b_ada), row(v_b_ada)), (norm_w, m_norm_w, v_norm_w), (conv_b, m_conv_b, v_conv_b),
               (conv_ln_w, m_conv_ln_w, v_conv_ln_w), (conv_ln_b, m_conv_ln_b, v_conv_ln_b),
               (q_norm_w, m_q_norm_w, v_q_norm_w), (kv_norm_w, m_kv_norm_w, v_kv_norm_w),
               (row(final_norm_w), row(m_final_norm_w), row(v_final_norm_w))]
    fin = _small_finalize(gathered, vectors, (conv_w, m_conv_w, v_conv_w), place[0:1])
    res = {}
    for p, (name, (w, _, _)) in enumerate(zip(vec_names, vectors)):
        shape = final_norm_w.shape if name == "final_norm_w" else w.shape
        res[name] = tuple(a.reshape(shape) for a in fin[4 * p:4 * p + 4])
    res["conv_w"] = tuple(fin[4 * len(vectors):4 * len(vectors) + 4])
    dmod_all, loss = fin[-2], fin[-1].reshape(())
    dmod_shard = lax.dynamic_slice(dmod_all, (0, chip * ada_cols), (N_DEV, ada_cols))
    g_w_ada = _ada_bwd(c_all.T, dmod_shard).reshape(1, D, ada_cols)

    def big(w, g, m, v, tr, name):
        d, nm, nv = _adamw(w, g, m, v, tr, name)
        return g.reshape(w.shape), d, nm, nv

    res["w_ada"] = big(w_ada, g_w_ada[0], m_w_ada, v_w_ada, 256, "adamw_w_ada")
    t_in = [a[0].T for a in (w_in, m_w_in, v_w_in)]
    d_t, nm_t, nv_t = _adamw(t_in[0], g_w_in_s, t_in[1], t_in[2], W_IN_ROWS, "adamw_w_in")
    res["w_in"] = tuple(a.T[None] for a in (g_w_in_s, d_t, nm_t, nv_t))
    small = _adamw_many(
        [w_uq, w_ukv, w_conv_out, w_attn_out, w_out], [g_w_uq_s, g_w_ukv_s, g_wco_s, g_wao_s, g_wo_s],
        [m_w_uq, m_w_ukv, m_w_conv_out, m_w_attn_out, m_w_out], [v_w_uq, v_w_ukv, v_w_conv_out, v_w_attn_out, v_w_out],
        128)
    res["w_uq"], res["w_ukv"], res["w_conv_out"], res["w_attn_out"], res["w_out"] = small

    order = ("w_ada", "b_ada", "norm_w", "w_in", "conv_w", "conv_b", "conv_ln_w", "conv_ln_b", "w_conv_out",
             "q_norm_w", "w_uq", "kv_norm_w", "w_ukv", "w_attn_out", "w_out", "final_norm_w")
    outs = [loss, grad_x[None]]
    for slot in range(4):
        outs += [res[name][slot] for name in order]
    return tuple(outs)
```

```python
import functools

import numpy as np
import jax
import jax.numpy as jnp
from jax import lax
from jax.experimental import pallas as pl
from jax.experimental.pallas import tpu as pltpu

F32 = jnp.float32
BF16 = jnp.bfloat16
MESH = pl.DeviceIdType.MESH

D = 1024
H = 8
DN = 128
DR = 64
RQ = 256
KC = 31
HALO = 32
EPS = 1e-6
ROPE_THETA = 10000.0
N_CHIP = 4
N_DEV = 8
LANE = 128
VMEM_BIG = 56 * 1024 * 1024

ADAM_LR = 0.001
ADAM_B1 = 0.9
ADAM_B2 = 0.999
ADAM_EPS = 1e-08
ADAM_WD = 0.01
ADAM_STEP = 10

A_COLS = 3 * D
L_COLS_RAW = RQ + RQ + DR
L_COLS = 640
G_COLS = 3 * D
IN_COLS = A_COLS + L_COLS_RAW + G_COLS


def _params(sem=None, vmem=None):
    kw = {}
    if sem is not None:
        kw["dimension_semantics"] = sem
    if vmem is not None:
        kw["vmem_limit_bytes"] = vmem
    return pltpu.CompilerParams(**kw)


def _dot(a, b):
    return jnp.dot(a, b, preferred_element_type=F32)


def _dot_nt(a, b):
    return lax.dot_general(a, b, (((1,), (1,)), ((), ())), preferred_element_type=F32)


def _dot_tn(a, b):
    return lax.dot_general(a, b, (((0,), (0,)), ((), ())), preferred_element_type=F32)


def _colsum(v):
    return jnp.sum(v, axis=0, keepdims=True)


def _rowmean(v):
    return jnp.mean(v, axis=-1, keepdims=True)


def _sigmoid(v):
    return jax.nn.sigmoid(v)


def _dsilu(v, s):
    return s * (1.0 + v * (1.0 - s))


def _swap_halves(v, first_half):
    return jnp.where(first_half, pltpu.roll(v, 96, 1), pltpu.roll(v, 32, 1))


def _first_half_mask(rows):
    lane = lax.broadcasted_iota(jnp.int32, (rows, LANE), 1)
    return (lane % 64) < 32


SMALL_IN_ROWS = 8 + HALO


def _adaln_norm(x, norm_w, small_in, w_ada_shard, b_ada_shard, dev, ts, shards):
    s = x.shape[0]
    cols = w_ada_shard.shape[1]
    taps = D // N_CHIP

    def modulation(dev_ref, x_ref, nw_ref, sm_ref, w_ref, b_ref, h_ref, mod_ref, c_ref, conv_ref,
                   part_sc, all_sc, small_sc, *sems):
        _allgather8_run(sm_ref, small_sc, *sems[0:3])
        for k in range(N_DEV):
            c_ref[k:k + 1, :] = small_sc[SMALL_IN_ROWS * k:SMALL_IN_ROWS * k + 1, :]
        for k in range(N_CHIP):
            base = SMALL_IN_ROWS * 2 * k + 8
            conv_ref[:, taps * k:taps * (k + 1)] = small_sc[base:base + HALO, 0:taps]
        cv = c_ref[...]
        part_sc[...] = jnp.dot(cv * _sigmoid(cv), w_ref[...], preferred_element_type=F32,
                               precision=lax.Precision.HIGHEST) + b_ref[...]
        _allgather8_run(part_sc, all_sc, *sems[3:6])
        for k in range(N_CHIP):
            mod_ref[:, cols * k:cols * (k + 1)] = all_sc[pl.ds(2 * N_DEV * k + dev_ref[0], 1), :]

    def body(dev_ref, x_ref, nw_ref, sm_ref, w_ref, b_ref, h_ref, mod_ref, *rest):
        xv = x_ref[...]
        r = lax.rsqrt(_rowmean(xv * xv) + EPS)
        y = xv * r * nw_ref[...]
        h_ref[...] = (y * (1.0 + mod_ref[:, D:2 * D]) + mod_ref[:, 0:D]).astype(BF16)

    row = pl.BlockSpec((ts, D), lambda i: (i, 0))
    const = lambda shape: pl.BlockSpec(shape, lambda i: (0, 0))
    n = len(shards)
    gathered_shapes, sems = _gather_operands(shards)
    outs = pl.pallas_call(
        _gather_alongside(body, 6, 4, n, s // ts - 1, modulation), grid=(s // ts,),
        in_specs=[pl.BlockSpec(memory_space=pltpu.SMEM), row, const((1, D)), const(small_in.shape),
                  const(w_ada_shard.shape), const((1, cols))] + [HBM_REF] * n,
        out_specs=[row, const((1, 3 * D)), const((N_DEV, D)), const((HALO, D))] + [HBM_REF] * n,
        out_shape=[jax.ShapeDtypeStruct((s, D), BF16), jax.ShapeDtypeStruct((1, 3 * D), F32),
                   jax.ShapeDtypeStruct((N_DEV, D), F32), jax.ShapeDtypeStruct((HALO, D), F32)] + gathered_shapes,
        scratch_shapes=[pltpu.VMEM((N_DEV, cols), F32), pltpu.VMEM((N_DEV * N_DEV, cols), F32),
                        pltpu.VMEM((N_DEV * SMALL_IN_ROWS, D), F32)] + ALLGATHER8_SEMS + ALLGATHER8_SEMS + sems,
        name="adaln_norm", compiler_params=_params(("arbitrary",), VMEM_BIG))(
            dev, x, norm_w, small_in, w_ada_shard, b_ada_shard, *shards)
    return outs[0], outs[1], outs[2], outs[3], _as_chip_slabs(outs[4:], shards)


def _mm_nn(a, b, tm, tn, name):
    m, k = a.shape
    n = b.shape[1]

    def body(a_ref, b_ref, o_ref):
        o_ref[...] = _dot(a_ref[...], b_ref[...])

    return pl.pallas_call(
        body, grid=(n // tn, m // tm),
        in_specs=[pl.BlockSpec((tm, k), lambda j, i: (i, 0)), pl.BlockSpec((k, tn), lambda j, i: (0, j))],
        out_specs=pl.BlockSpec((tm, tn), lambda j, i: (i, j)),
        out_shape=jax.ShapeDtypeStruct((m, n), F32), name=name,
        compiler_params=_params(("parallel", "parallel"), VMEM_BIG))(a, b)


def _mm_tn(a, b, tm, tk, tn, name, out_dtype=F32):
    m, k = a.shape
    n = b.shape[1]
    steps = m // tm

    def body(a_ref, b_ref, o_ref, acc_ref):
        @pl.when(pl.program_id(2) == 0)
        def _():
            acc_ref[...] = jnp.zeros_like(acc_ref)
        acc_ref[...] += _dot_tn(a_ref[...], b_ref[...])

        @pl.when(pl.program_id(2) == steps - 1)
        def _():
            o_ref[...] = acc_ref[...].astype(out_dtype)

    return pl.pallas_call(
        body, grid=(k // tk, n // tn, steps),
        in_specs=[pl.BlockSpec((tm, tk), lambda r, j, i: (i, r)), pl.BlockSpec((tm, tn), lambda r, j, i: (i, j))],
        out_specs=pl.BlockSpec((tk, tn), lambda r, j, i: (r, j)),
        out_shape=jax.ShapeDtypeStruct((k, n), out_dtype), scratch_shapes=[pltpu.VMEM((tk, tn), F32)], name=name,
        compiler_params=_params(("parallel", "parallel", "arbitrary"), VMEM_BIG))(a, b)


def _mm_tn_stack(a, b, tm, name):
    n_stack, m, k = a.shape
    n = b.shape[2]
    steps = m // tm

    def body(a_ref, b_ref, o_ref, acc_ref):
        @pl.when(pl.program_id(1) == 0)
        def _():
            acc_ref[...] = jnp.zeros_like(acc_ref)
        acc_ref[...] += _dot_tn(a_ref[0], b_ref[0])

        @pl.when(pl.program_id(1) == steps - 1)
        def _():
            o_ref[0] = acc_ref[...].astype(BF16)

    out = pl.pallas_call(
        body, grid=(n_stack, steps),
        in_specs=[pl.BlockSpec((1, tm, k), lambda g, i: (g, i, 0)), pl.BlockSpec((1, tm, n), lambda g, i: (g, i, 0))],
        out_specs=pl.BlockSpec((1, k, n), lambda g, i: (g, 0, 0)),
        out_shape=jax.ShapeDtypeStruct((n_stack, k, n), BF16), scratch_shapes=[pltpu.VMEM((k, n), F32)], name=name,
        compiler_params=_params(("parallel", "arbitrary"), VMEM_BIG))(a, b)
    return [out[g] for g in range(n_stack)]


def _coords():
    return lax.axis_index("x"), lax.axis_index("y"), lax.axis_index("c")


HBM_REF = pl.BlockSpec(memory_space=pl.ANY)


def _chip_scatter_copies(p_refs, got_refs, send_sems, recv_sems):
    x, y, c = _coords()
    copies = []
    for a in range(len(p_refs)):
        for j, (px, py) in enumerate([(1 - x, y), (x, 1 - y), (1 - x, 1 - y)]):
            copies.append(pltpu.make_async_remote_copy(
                src_ref=p_refs[a].at[2 * px + py], dst_ref=got_refs[a].at[j], send_sem=send_sems.at[3 * a + j],
                recv_sem=recv_sems.at[3 * a + j], device_id=(px, py, c), device_id_type=MESH))
    return copies


RELATIONS = [(dx, dy, dc) for dx in (0, 1) for dy in (0, 1) for dc in (0, 1)][1:]


def _device_scatter_copies(p_refs, got_refs, send_sems, recv_sems):
    x, y, c = _coords()
    copies = []
    for a in range(len(p_refs)):
        half = p_refs[a].shape[1] // 2
        for j, (dx, dy, dc) in enumerate(RELATIONS):
            px, py, pc = (1 - x if dx else x), (1 - y if dy else y), (1 - c if dc else c)
            src = p_refs[a].at[2 * px + py, pl.ds(pl.multiple_of(pc * half, 16), half), :]
            copies.append(pltpu.make_async_remote_copy(
                src_ref=src, dst_ref=got_refs[a].at[j], send_sem=send_sems.at[7 * a + j],
                recv_sem=recv_sems.at[7 * a + j], device_id=(px, py, pc), device_id_type=MESH))
    return copies


def _scatter_alongside(body, n_in, n_out, n_parts, last_step, make_copies):
    def wrapped(*refs):
        ins, parts = refs[:n_in], refs[n_in:n_in + n_parts]
        rest = refs[n_in + n_parts:]
        outs, got = rest[:n_out], rest[n_out:n_out + n_parts]
        scratch, (send_sems, recv_sems) = rest[n_out + n_parts:-2], rest[-2:]

        @pl.when(pl.program_id(0) == 0)
        def _():
            for cp in make_copies(parts, got, send_sems, recv_sems):
                cp.start()

        body(*ins, *outs, *scratch)

        @pl.when(pl.program_id(0) == last_step)
        def _():
            for cp in make_copies(parts, got, send_sems, recv_sems):
                cp.wait()

    return wrapped


def _scatter_operands(parts, per_device):
    n = len(parts)
    if per_device:
        slots, shapes = 7, [jax.ShapeDtypeStruct((7, a.shape[1] // 2, a.shape[2]), a.dtype) for a in parts]
    else:
        slots, shapes = 3, [jax.ShapeDtypeStruct((3,) + a.shape[1:], a.dtype) for a in parts]
    sems = [pltpu.SemaphoreType.DMA((slots * n,)), pltpu.SemaphoreType.DMA((slots * n,))]
    return [HBM_REF] * n, [HBM_REF] * n, shapes, sems


def _shifted_copies(win_ref, sh_ref, rows):
    for p in range(1, 8):
        sh_ref[p - 1, 0:rows, :] = win_ref[pl.ds(p, rows), :]


def _tap_rows(win_ref, sh_ref, start, rows):
    p = start % 8
    if p == 0:
        return win_ref[pl.ds(start, rows), :]
    return sh_ref[p - 1, pl.ds(start - p, rows), :]


def _conv_taps(win_ref, sh_ref, w_ref, rows, chunk, offset_of_tap):
    pieces = []
    for c0 in range(0, rows, chunk):
        acc = None
        for j in range(KC):
            term = w_ref[j:j + 1, :] * _tap_rows(win_ref, sh_ref, c0 + offset_of_tap(j), chunk)
            acc = term if acc is None else acc + term
        pieces.append(acc)
    return pieces


def _conv_fwd(proj_a, conv_w, conv_b, ln_w, ln_b, ts, chunk, shards):
    s = proj_a.shape[0]

    def body(av_ref, al_ref, ag_ref, w_ref, b_ref, lw_ref, lb_ref, u0_ref, u1_ref, za_ref, win_ref, sh_ref):
        @pl.when(pl.program_id(0) == 0)
        def _():
            win_ref[0:HALO, :] = jnp.zeros((HALO, D), F32)

        u0 = av_ref[...] * _sigmoid(al_ref[...])
        u0_ref[...] = u0
        win_ref[HALO:HALO + ts, :] = u0
        _shifted_copies(win_ref, sh_ref, ts + HALO - 8)
        pieces = _conv_taps(win_ref, sh_ref, w_ref, ts, chunk, lambda j: HALO - (KC - 1) + j)
        for n, acc in enumerate(pieces):
            u1_ref[n * chunk:(n + 1) * chunk, :] = acc + b_ref[...]
        win_ref[0:HALO, :] = win_ref[ts:ts + HALO, :]

        u1 = u1_ref[...]
        xc = u1 - _rowmean(u1)
        rstd = lax.rsqrt(_rowmean(xc * xc) + EPS)
        u2 = xc * rstd * lw_ref[...] + lb_ref[...]
        u3 = u2 * _sigmoid(u2)
        ag = ag_ref[...]
        za_ref[...] = (u3 * (ag * _sigmoid(ag))).astype(BF16)

    col = lambda c: pl.BlockSpec((ts, D), lambda i, c=c: (i, c))
    row = pl.BlockSpec((ts, D), lambda i: (i, 0))
    vec = pl.BlockSpec((1, D), lambda i: (0, 0))
    n = len(shards)
    gathered_shapes, sems = _gather_operands(shards)
    outs = pl.pallas_call(
        _gather_alongside(body, 7, 3, n, s // ts - 1), grid=(s // ts,),
        in_specs=[col(0), col(1), col(2), pl.BlockSpec((HALO, D), lambda i: (0, 0)), vec, vec, vec] + [HBM_REF] * n,
        out_specs=[row, row, row] + [HBM_REF] * n,
        out_shape=[jax.ShapeDtypeStruct((s, D), F32), jax.ShapeDtypeStruct((s, D), F32),
                   jax.ShapeDtypeStruct((s, D), BF16)] + gathered_shapes,
        scratch_shapes=[pltpu.VMEM((ts + HALO, D), F32), pltpu.VMEM((7, ts + HALO, D), F32)] + sems,
        name="conv_fwd", compiler_params=_params(("arbitrary",), VMEM_BIG))(
            proj_a, proj_a, proj_a, conv_w, conv_b, ln_w, ln_b, *shards)
    return outs[0], outs[1], outs[2], _as_chip_slabs(outs[3:], shards)


def _conv_bwd(dza, proj_a, u0, u1, conv_w, ln_w, ln_b, ts, chunk, parts):
    s = dza.shape[0]
    nt = s // ts
    per = ts // HALO

    def body(dza_ref, av_ref, al_ref, ag_ref, u0_ref, u0p_ref, u1_ref, w_ref, lw_ref, lb_ref,
             dpa_ref, gw_ref, gv_ref, dwin_ref, uwin_ref, du0_ref, gwp_ref, dsh_ref, ush_ref):
        step = pl.program_id(0)
        tile = nt - 1 - step

        @pl.when(step == 0)
        def _():
            dwin_ref[ts:ts + HALO, :] = jnp.zeros((HALO, D), F32)
            gwp_ref[...] = jnp.zeros_like(gwp_ref)
            gv_ref[...] = jnp.zeros_like(gv_ref)

        ag = ag_ref[...]
        sg = _sigmoid(ag)
        u1 = u1_ref[...]
        xc = u1 - _rowmean(u1)
        rstd = lax.rsqrt(_rowmean(xc * xc) + EPS)
        xh = xc * rstd
        u2 = xh * lw_ref[...] + lb_ref[...]
        s2 = _sigmoid(u2)
        dz = dza_ref[...]
        du3 = dz * (ag * sg)
        dpa_ref[:, 2 * D:3 * D] = (dz * (u2 * s2) * _dsilu(ag, sg)).astype(BF16)
        du2 = du3 * _dsilu(u2, s2)
        gv_ref[0:1, :] += _colsum(du2 * xh)
        gv_ref[1:2, :] += _colsum(du2)
        dxh = du2 * lw_ref[...]
        du1 = rstd * (dxh - _rowmean(dxh) - xh * _rowmean(dxh * xh))
        gv_ref[2:3, :] += _colsum(du1)
        dwin_ref[0:ts, :] = du1

        uwin_ref[0:HALO, :] = jnp.where(tile == 0, 0.0, u0p_ref[...])
        uwin_ref[HALO:HALO + ts, :] = u0_ref[...]

        _shifted_copies(dwin_ref, dsh_ref, ts + HALO - 8)
        _shifted_copies(uwin_ref, ush_ref, ts + HALO - 8)
        pieces = _conv_taps(dwin_ref, dsh_ref, w_ref, ts, chunk, lambda j: (KC - 1) - j)
        for n, acc in enumerate(pieces):
            du0_ref[n * chunk:(n + 1) * chunk, :] = acc
        for c0 in range(0, ts, chunk):
            dchunk = dwin_ref[c0:c0 + chunk, :]
            for j in range(KC):
                prod = dchunk * _tap_rows(uwin_ref, ush_ref, c0 + HALO - (KC - 1) + j, chunk)
                gwp_ref[8 * j:8 * j + 8, :] += jnp.sum(prod.reshape(chunk // 8, 8, D), axis=0)
        dwin_ref[ts:ts + HALO, :] = dwin_ref[0:HALO, :]

        du0 = du0_ref[...]
        al = al_ref[...]
        sl = _sigmoid(al)
        dpa_ref[:, 0:D] = (du0 * sl).astype(BF16)
        dpa_ref[:, D:2 * D] = (du0 * av_ref[...] * sl * (1.0 - sl)).astype(BF16)

        @pl.when(step == nt - 1)
        def _():
            for j in range(KC):
                gw_ref[j:j + 1, :] = _colsum(gwp_ref[8 * j:8 * j + 8, :])
            gw_ref[KC:HALO, :] = jnp.zeros((HALO - KC, D), F32)

    rev = lambda i: nt - 1 - i
    col = lambda c: pl.BlockSpec((ts, D), lambda i, c=c: (rev(i), c))
    row = pl.BlockSpec((ts, D), lambda i: (rev(i), 0))
    vec = pl.BlockSpec((1, D), lambda i: (0, 0))
    halo = pl.BlockSpec((HALO, D), lambda i: (jnp.maximum(rev(i) * per - 1, 0), 0))
    side_in, side_out, side_shapes, side_sems = _scatter_operands(parts, True)
    outs = pl.pallas_call(
        _scatter_alongside(body, 10, 3, len(parts), nt - 1, _device_scatter_copies), grid=(nt,),
        in_specs=[row, col(0), col(1), col(2), row, halo, row, pl.BlockSpec((HALO, D), lambda i: (0, 0)), vec, vec]
        + side_in,
        out_specs=[pl.BlockSpec((ts, A_COLS), lambda i: (rev(i), 0)),
                   pl.BlockSpec((HALO, D), lambda i: (0, 0)), pl.BlockSpec((8, D), lambda i: (0, 0))] + side_out,
        out_shape=[jax.ShapeDtypeStruct((s, A_COLS), BF16), jax.ShapeDtypeStruct((HALO, D), F32),
                   jax.ShapeDtypeStruct((8, D), F32)] + side_shapes,
        scratch_shapes=[pltpu.VMEM((ts + HALO, D), F32), pltpu.VMEM((ts + HALO, D), F32),
                        pltpu.VMEM((ts, D), F32), pltpu.VMEM((8 * HALO, D), F32),
                        pltpu.VMEM((7, ts + HALO, D), F32), pltpu.VMEM((7, ts + HALO, D), F32)] + side_sems,
        name="conv_bwd", compiler_params=_params(("arbitrary",), VMEM_BIG))(
            dza, proj_a, proj_a, proj_a, u0, u0, u1, conv_w, ln_w, ln_b, *parts)
    return outs[0], outs[1], outs[2], list(outs[3:])


def _mla_prep(proj_l, q_norm_w, kv_norm_w, w_uq2, w_ukv, cos_t, sin_t, ts):
    s = proj_l.shape[0]

    def body(pl_ref, qw_ref, kw_ref, wq_ref, wkv_ref, c_ref, s_ref, qn_ref, kvn_ref, q_ref, k_ref, v_ref):
        first = _first_half_mask(ts)
        cs = c_ref[...]
        sn = s_ref[...]

        def rms(v, w):
            return v * lax.rsqrt(_rowmean(v * v) + EPS) * w

        def rope(v):
            return v * cs + _swap_halves(v, first) * sn

        qn = rms(pl_ref[:, 0:RQ], qw_ref[...]).astype(BF16)
        kvn = rms(pl_ref[:, RQ:2 * RQ], kw_ref[...]).astype(BF16)
        qn_ref[...] = qn
        kvn_ref[...] = kvn
        q = _dot(qn, wq_ref[...])
        kv = _dot(kvn, wkv_ref[...])
        kr = rope(pl_ref[:, 2 * RQ:2 * RQ + LANE]).astype(BF16)
        for h in range(H):
            q_ref[h, :, 0:DN] = q[:, DN * h:DN * (h + 1)].astype(BF16)
            q_ref[h, :, DN:2 * DN] = rope(q[:, H * DN + LANE * h:H * DN + LANE * (h + 1)]).astype(BF16)
            k_ref[h, :, 0:DN] = kv[:, 2 * DN * h:2 * DN * h + DN].astype(BF16)
            k_ref[h, :, DN:2 * DN] = kr
            v_ref[h, :, 0:DN] = kv[:, 2 * DN * h + DN:2 * DN * (h + 1)].astype(BF16)
            v_ref[h, :, DN:2 * DN] = jnp.ones((ts, DN), BF16)

    const = lambda shape: pl.BlockSpec(shape, lambda i: (0,) * len(shape))
    rowb = lambda w: pl.BlockSpec((ts, w), lambda i: (i, 0))
    head = lambda w: pl.BlockSpec((H, ts, w), lambda i: (0, i, 0))
    return pl.pallas_call(
        body, grid=(s // ts,),
        in_specs=[rowb(L_COLS), const((1, RQ)), const((1, RQ)), const((RQ, 2 * H * DN)), const((RQ, 2 * H * DN)),
                  rowb(LANE), rowb(LANE)],
        out_specs=[rowb(RQ), rowb(RQ), head(2 * DN), head(2 * DN), head(2 * DN)],
        out_shape=[jax.ShapeDtypeStruct((s, RQ), BF16), jax.ShapeDtypeStruct((s, RQ), BF16),
                   jax.ShapeDtypeStruct((H, s, 2 * DN), BF16), jax.ShapeDtypeStruct((H, s, 2 * DN), BF16),
                   jax.ShapeDtypeStruct((H, s, 2 * DN), BF16)],
        name="mla_prep", compiler_params=_params(("parallel",), VMEM_BIG))(
            proj_l, q_norm_w, kv_norm_w, w_uq2, w_ukv, cos_t, sin_t)


def _mla_prep_bwd(dq, dk, dv, proj_l, qn, kvn, q_norm_w, kv_norm_w, w_uq2, w_ukv, cos_t, sin_t, ts):
    s = proj_l.shape[0]

    def body(dq_ref, dk_ref, dv_ref, pl_ref, qn_ref, kvn_ref, qw_ref, kw_ref, wq_ref, wkv_ref, c_ref, s_ref,
             dpl_ref, gwq_ref, gwkv_ref, gv_ref, dq2_ref, dkv2_ref):
        @pl.when(pl.program_id(0) == 0)
        def _():
            gwq_ref[...] = jnp.zeros_like(gwq_ref)
            gwkv_ref[...] = jnp.zeros_like(gwkv_ref)
            gv_ref[...] = jnp.zeros_like(gv_ref)

        first = _first_half_mask(ts)
        cs = c_ref[...] * ATT_SCALE
        sn = s_ref[...] * ATT_SCALE

        def rope_bwd(g):
            return g * cs + _swap_halves(g * sn, first)

        def rms_bwd(v, w, dy):
            r = lax.rsqrt(_rowmean(v * v) + EPS)
            vh = v * r
            dvh = dy * w
            return r * (dvh - vh * _rowmean(dvh * vh)), _colsum(dy * vh)

        dkr = None
        for h in range(H):
            dq2_ref[:, DN * h:DN * (h + 1)] = (dq_ref[h, :, 0:DN] * ATT_SCALE).astype(BF16)
            dq2_ref[:, H * DN + LANE * h:H * DN + LANE * (h + 1)] = rope_bwd(dq_ref[h, :, DN:2 * DN]).astype(BF16)
            dkv2_ref[:, 2 * DN * h:2 * DN * h + DN] = (dk_ref[h, :, 0:DN] * ATT_SCALE).astype(BF16)
            dkv2_ref[:, 2 * DN * h + DN:2 * DN * (h + 1)] = dv_ref[h].astype(BF16)
            part = dk_ref[h, :, DN:2 * DN]
            dkr = part if dkr is None else dkr + part

        dq2 = dq2_ref[...]
        dkv2 = dkv2_ref[...]
        gwq_ref[...] += _dot_tn(qn_ref[...], dq2)
        gwkv_ref[...] += _dot_tn(kvn_ref[...], dkv2)
        dcq, gq = rms_bwd(pl_ref[:, 0:RQ], qw_ref[...], _dot_nt(dq2, wq_ref[...]))
        dckv, gkv = rms_bwd(pl_ref[:, RQ:2 * RQ], kw_ref[...], _dot_nt(dkv2, wkv_ref[...]))
        gv_ref[0:1, :] += gq
        gv_ref[1:2, :] += gkv
        dpl_ref[:, 0:RQ] = dcq.astype(BF16)
        dpl_ref[:, RQ:2 * RQ] = dckv.astype(BF16)
        dpl_ref[:, 2 * RQ:2 * RQ + LANE] = rope_bwd(dkr).astype(BF16)

    const = lambda shape: pl.BlockSpec(shape, lambda i: (0,) * len(shape))
    rowb = lambda w: pl.BlockSpec((ts, w), lambda i: (i, 0))
    head = lambda w: pl.BlockSpec((H, ts, w), lambda i: (0, i, 0))
    return pl.pallas_call(
        body, grid=(s // ts,),
        in_specs=[head(2 * DN), head(2 * DN), head(DN), rowb(L_COLS), rowb(RQ), rowb(RQ), const((1, RQ)),
                  const((1, RQ)), const((RQ, 2 * H * DN)), const((RQ, 2 * H * DN)), rowb(LANE), rowb(LANE)],
        out_specs=[rowb(L_COLS), const((RQ, 2 * H * DN)), const((RQ, 2 * H * DN)), const((8, RQ))],
        out_shape=[jax.ShapeDtypeStruct((s, L_COLS), BF16), jax.ShapeDtypeStruct((RQ, 2 * H * DN), F32),
                   jax.ShapeDtypeStruct((RQ, 2 * H * DN), F32), jax.ShapeDtypeStruct((8, RQ), F32)],
        scratch_shapes=[pltpu.VMEM((ts, 2 * H * DN), BF16), pltpu.VMEM((ts, 2 * H * DN), BF16)],
        name="mla_prep_bwd", compiler_params=_params(("arbitrary",), VMEM_BIG))(
            dq, dk, dv, proj_l, qn, kvn, q_norm_w, kv_norm_w, w_uq2, w_ukv, cos_t, sin_t)


def _causal_pairs(n, by_key):
    if by_key:
        pairs = [(i, j) for j in range(n) for i in range(j, n)]
    else:
        pairs = [(i, j) for i in range(n) for j in range(i + 1)]
    return (jnp.asarray(np.array([p[0] for p in pairs], np.int32)),
            jnp.asarray(np.array([p[1] for p in pairs], np.int32)))


ATT_SCALE = float((DN + DR) ** -0.5)
LOG2E = 1.4426950408889634
LN2 = 0.6931471805599453
ATT_HEADS_FWD = 8
ATT_HEADS = 4
W_IN_ROWS = 336
ATT_ROWS = 16


def _diag_width(r0, t):
    return min(t, -(-(r0 + ATT_ROWS) // LANE) * LANE)


def _diag_mask_rows(r0, width):
    rows = r0 + lax.broadcasted_iota(jnp.int32, (ATT_ROWS, width), 0)
    cols = lax.broadcasted_iota(jnp.int32, (ATT_ROWS, width), 1)
    return cols <= rows


def _diag_mask(t):
    return lax.broadcasted_iota(jnp.int32, (t, t), 1) <= lax.broadcasted_iota(jnp.int32, (t, t), 0)


def _attn_fwd(q, k, v, t):
    s = q.shape[1]
    n = s // t
    scale2 = float((DN + DR) ** -0.5) * LOG2E
    qi, ki = _causal_pairs(n, by_key=False)

    def body(qi_ref, ki_ref, q_ref, k_ref, v_ref, o_ref, lse_ref, *scratch):
        per_head = [scratch[5 * h:5 * h + 5] for h in range(ATT_HEADS_FWD)]
        p = pl.program_id(1)
        i = qi_ref[p]
        j = ki_ref[p]

        @pl.when(j == 0)
        def _():
            for m_sc, acc_sc, _, _, _ in per_head:
                m_sc[...] = jnp.full_like(m_sc, -jnp.inf)
                acc_sc[...] = jnp.zeros_like(acc_sc)

        def scores(h, diag):
            sc = _dot_nt(q_ref[h], k_ref[h])
            if diag:
                sc = jnp.where(_diag_mask(t), sc, -jnp.inf)
            per_head[h][2][...] = sc

        def rowmax(h, rows):
            per_head[h][4][rows, :] = jnp.max(per_head[h][2][rows, :], axis=-1, keepdims=True)

        def stats(h):
            m_sc, acc_sc, _, _, mx_sc = per_head[h]
            m_prev = m_sc[...]
            m_new = jnp.maximum(m_prev, mx_sc[...] * scale2)
            m_sc[...] = m_new
            acc_sc[...] = jnp.exp2(m_prev - m_new) * acc_sc[...]

        def probs(h, rows):
            m_sc, _, s_sc, p_sc, _ = per_head[h]
            p_sc[rows, :] = jnp.exp2(s_sc[rows, :] * scale2 - m_sc[rows, :]).astype(BF16)

        def values(h):
            _, acc_sc, _, p_sc, _ = per_head[h]
            acc_sc[...] += _dot(p_sc[...], v_ref[h])

        def step(diag):
            blocks = [slice(r0, r0 + ATT_ROWS) for r0 in range(0, t, ATT_ROWS)]
            for h in range(ATT_HEADS_FWD):
                scores(h, diag)
            for rows in blocks:
                rowmax(0, rows)
            stats(0)
            for h in range(ATT_HEADS_FWD):
                for rows in blocks:
                    probs(h, rows)
                    if h + 1 < ATT_HEADS_FWD:
                        rowmax(h + 1, rows)
                if h + 1 < ATT_HEADS_FWD:
                    stats(h + 1)
                values(h)

        @pl.when(j < i)
        def _():
            step(False)

        @pl.when(j == i)
        def _():
            step(True)
            for h, (m_sc, acc_sc, _, _, _) in enumerate(per_head):
                l = acc_sc[:, DN:2 * DN]
                o_ref[:, DN * h:DN * (h + 1)] = acc_sc[:, 0:DN] / l
                lse_ref[h] = (m_sc[...] + jnp.log2(l[:, 0:1])) * LN2

    hb = ATT_HEADS_FWD
    grid_spec = pltpu.PrefetchScalarGridSpec(
        num_scalar_prefetch=2, grid=(H // hb, int(qi.shape[0])),
        in_specs=[pl.BlockSpec((hb, t, 2 * DN), lambda h, p, qi, ki: (h, qi[p], 0)),
                  pl.BlockSpec((hb, t, 2 * DN), lambda h, p, qi, ki: (h, ki[p], 0)),
                  pl.BlockSpec((hb, t, 2 * DN), lambda h, p, qi, ki: (h, ki[p], 0))],
        out_specs=[pl.BlockSpec((t, hb * DN), lambda h, p, qi, ki: (qi[p], h)),
                   pl.BlockSpec((hb, t, 1), lambda h, p, qi, ki: (h, qi[p], 0))],
        scratch_shapes=[pltpu.VMEM((t, 1), F32), pltpu.VMEM((t, 2 * DN), F32), pltpu.VMEM((t, t), F32),
                        pltpu.VMEM((t, t), BF16), pltpu.VMEM((t, 1), F32)] * hb)
    return pl.pallas_call(
        body, grid_spec=grid_spec,
        out_shape=[jax.ShapeDtypeStruct((s, H * DN), F32), jax.ShapeDtypeStruct((H, s, 1), F32)],
        name="attn_fwd", compiler_params=_params(("parallel", "arbitrary"), VMEM_BIG))(qi, ki, q, k, v)


def _attn_bwd(q, k, v, do, lse, delta, t):
    s = q.shape[1]
    n = s // t
    scale = ATT_SCALE
    qi, ki = _causal_pairs(n, by_key=True)

    def body(qi_ref, ki_ref, q_ref, k_ref, v_ref, do_ref, lse_ref, dl_ref, dq_ref, dk_ref, dv_ref,
             dk_sc, dv_sc, s_sc, dp_sc, p_sc, ds_sc):
        p = pl.program_id(1)
        i = qi_ref[p]
        j = ki_ref[p]

        @pl.when(p == 0)
        def _():
            dq_ref[...] = jnp.zeros_like(dq_ref)

        @pl.when(i == j)
        def _():
            dk_sc[...] = jnp.zeros_like(dk_sc)
            dv_sc[...] = jnp.zeros_like(dv_sc)

        def step(diag):
            for h in range(ATT_HEADS):
                s_sc[h] = _dot_nt(q_ref[h], k_ref[h])
                dp_sc[h] = _dot_nt(do_ref[:, DN * h:DN * (h + 1)], v_ref[h, :, 0:DN])
            for h in range(ATT_HEADS):
                for r0 in range(0, t, ATT_ROWS):
                    rows = slice(r0, r0 + ATT_ROWS)
                    width = _diag_width(r0, t) if diag else t
                    sc = s_sc[h, rows, 0:width] * (scale * LOG2E)
                    if diag:
                        sc = jnp.where(_diag_mask_rows(r0, width), sc, -jnp.inf)
                    pr = jnp.exp2(sc - lse_ref[h, rows, :] * LOG2E)
                    ds = pr * (dp_sc[h, rows, 0:width] - dl_ref[h, rows, :])
                    p_sc[h, rows, 0:width] = pr.astype(BF16)
                    ds_sc[h, rows, 0:width] = ds.astype(BF16)
                    if width < t:
                        p_sc[h, rows, width:t] = jnp.zeros((ATT_ROWS, t - width), BF16)
                        ds_sc[h, rows, width:t] = jnp.zeros((ATT_ROWS, t - width), BF16)
            q_rows = pl.ds(pl.multiple_of(i * t, t), t)
            for h in range(ATT_HEADS):
                dv_sc[h] += _dot_tn(p_sc[h], do_ref[:, DN * h:DN * (h + 1)])
                dk_sc[h] += _dot_tn(ds_sc[h], q_ref[h])
                dq_ref[h, q_rows, :] += _dot(ds_sc[h], k_ref[h])

        @pl.when(i > j)
        def _():
            step(False)

        @pl.when(i == j)
        def _():
            step(True)

        @pl.when(i == n - 1)
        def _():
            dk_ref[...] = dk_sc[...]
            dv_ref[...] = dv_sc[...]

    hb = ATT_HEADS
    grid_spec = pltpu.PrefetchScalarGridSpec(
        num_scalar_prefetch=2, grid=(H // hb, int(qi.shape[0])),
        in_specs=[pl.BlockSpec((hb, t, 2 * DN), lambda h, p, qi, ki: (h, qi[p], 0)),
                  pl.BlockSpec((hb, t, 2 * DN), lambda h, p, qi, ki: (h, ki[p], 0)),
                  pl.BlockSpec((hb, t, 2 * DN), lambda h, p, qi, ki: (h, ki[p], 0)),
                  pl.BlockSpec((t, hb * DN), lambda h, p, qi, ki: (qi[p], h)),
                  pl.BlockSpec((hb, t, 1), lambda h, p, qi, ki: (h, qi[p], 0)),
                  pl.BlockSpec((hb, t, 1), lambda h, p, qi, ki: (h, qi[p], 0))],
        out_specs=[pl.BlockSpec((hb, s, 2 * DN), lambda h, p, qi, ki: (h, 0, 0), pipeline_mode=pl.Buffered(1)),
                   pl.BlockSpec((hb, t, 2 * DN), lambda h, p, qi, ki: (h, ki[p], 0)),
                   pl.BlockSpec((hb, t, DN), lambda h, p, qi, ki: (h, ki[p], 0))],
        scratch_shapes=[pltpu.VMEM((hb, t, 2 * DN), F32), pltpu.VMEM((hb, t, DN), F32),
                        pltpu.VMEM((hb, t, t), F32), pltpu.VMEM((hb, t, t), F32),
                        pltpu.VMEM((hb, t, t), BF16), pltpu.VMEM((hb, t, t), BF16)])
    return pl.pallas_call(
        body, grid_spec=grid_spec,
        out_shape=[jax.ShapeDtypeStruct((H, s, 2 * DN), F32), jax.ShapeDtypeStruct((H, s, 2 * DN), F32),
                   jax.ShapeDtypeStruct((H, s, DN), F32)],
        name="attn_bwd", compiler_params=_params(("parallel", "arbitrary"), VMEM_BIG))(
            qi, ki, q, k, v, do, lse, delta)


def _middle(za, o, proj_g, x, tgt, gate, fnw, wco, wao, wo, ts):
    s = x.shape[0]
    inv_d = 1.0 / D

    def body(za_ref, o_ref, bg_ref, ga_ref, gb_ref, x_ref, t_ref, gate_ref, fnw_ref, wco_ref, wao_ref, wo_ref,
             dx2_ref, dza_ref, do_ref, dl_ref, dpg_ref, lhs_ref, rhs_ref, vec_ref):
        @pl.when(pl.program_id(0) == 0)
        def _():
            vec_ref[...] = jnp.zeros_like(vec_ref)

        ov = o_ref[...]
        bg = bg_ref[...]
        sb = _sigmoid(bg)
        silu_b = bg * sb
        zb = (ov * silu_b).astype(BF16)
        lhs_ref[0] = za_ref[...]
        lhs_ref[1] = zb
        ya = _dot(za_ref[...], wco_ref[...])
        yb = _dot(zb, wao_ref[...])
        sa = _sigmoid(ga_ref[...])
        sg = _sigmoid(gb_ref[...])
        mg = (sa * ya + sg * yb).astype(BF16)
        lhs_ref[2] = mg
        mo = _dot(mg, wo_ref[...])
        gate_v = gate_ref[...]
        x2 = x_ref[...] + gate_v * mo
        r = lax.rsqrt(_rowmean(x2 * x2) + EPS)
        xh = x2 * r
        fw = fnw_ref[...]
        e = xh * fw - t_ref[...]
        vec_ref[2:3, :] += _colsum(e * e)
        dy = e * inv_d
        vec_ref[0:1, :] += _colsum(dy * xh)
        dxh = dy * fw
        dx2 = r * (dxh - xh * _rowmean(dxh * xh))
        dx2_ref[...] = dx2
        vec_ref[1:2, :] += _colsum(dx2 * mo)
        dmo = (gate_v * dx2).astype(BF16)
        rhs_ref[2] = dmo
        dmg = _dot_nt(dmo, wo_ref[...])
        dya = (sa * dmg).astype(BF16)
        dyb = (sg * dmg).astype(BF16)
        rhs_ref[0] = dya
        rhs_ref[1] = dyb
        dpg_ref[:, D:2 * D] = (dmg * ya * (sa * (1.0 - sa))).astype(BF16)
        dpg_ref[:, 2 * D:3 * D] = (dmg * yb * (sg * (1.0 - sg))).astype(BF16)
        dza_ref[...] = _dot_nt(dya, wco_ref[...])
        dzb = _dot_nt(dyb, wao_ref[...])
        dov = dzb * silu_b
        do_ref[...] = dov.astype(BF16)
        dpg_ref[:, 0:D] = (dzb * ov * _dsilu(bg, sb)).astype(BF16)
        dprod = dov * ov
        for h in range(H):
            dl_ref[h] = jnp.sum(dprod[:, DN * h:DN * (h + 1)], axis=-1, keepdims=True)

    col = lambda c: pl.BlockSpec((ts, D), lambda i, c=c: (i, c))
    row = pl.BlockSpec((ts, D), lambda i: (i, 0))
    vec = pl.BlockSpec((1, D), lambda i: (0, 0))
    wsp = pl.BlockSpec((D, D), lambda i: (0, 0))
    stack = pl.BlockSpec((3, ts, D), lambda i: (0, i, 0))
    bf = jax.ShapeDtypeStruct((s, D), BF16)
    ff = jax.ShapeDtypeStruct((s, D), F32)
    return pl.pallas_call(
        body, grid=(s // ts,),
        in_specs=[row, row, col(0), col(1), col(2), row, row, vec, vec, wsp, wsp, wsp],
        out_specs=[row, row, row, pl.BlockSpec((H, ts, 1), lambda i: (0, i, 0)),
                   pl.BlockSpec((ts, G_COLS), lambda i: (i, 0)), stack, stack,
                   pl.BlockSpec((8, D), lambda i: (0, 0))],
        out_shape=[ff, ff, bf, jax.ShapeDtypeStruct((H, s, 1), F32), jax.ShapeDtypeStruct((s, G_COLS), BF16),
                   jax.ShapeDtypeStruct((3, s, D), BF16), jax.ShapeDtypeStruct((3, s, D), BF16),
                   jax.ShapeDtypeStruct((8, D), F32)],
        name="middle", compiler_params=_params(("arbitrary",), VMEM_BIG))(
            za, o, proj_g, proj_g, proj_g, x, tgt, gate, fnw, wco, wao, wo)


def _input_bwd(dpa, dpl, dpg, wa, wl, wg, x, dx2, norm_w, scale, ts, parts):
    s = x.shape[0]

    def body(dpa_ref, dpl_ref, dpg_ref, wa_ref, wl_ref, wg_ref, x_ref, dx2_ref, nw_ref, sc_ref, gx_ref, gv_ref):
        @pl.when(pl.program_id(0) == 0)
        def _():
            gv_ref[...] = jnp.zeros_like(gv_ref)

        dh = (_dot_nt(dpa_ref[...], wa_ref[...]) + _dot_nt(dpl_ref[...], wl_ref[...])
              + _dot_nt(dpg_ref[...], wg_ref[...]))
        xv = x_ref[...]
        r = lax.rsqrt(_rowmean(xv * xv) + EPS)
        xh = xv * r
        nw = nw_ref[...]
        gv_ref[0:1, :] += _colsum(dh)
        gv_ref[1:2, :] += _colsum(dh * (xh * nw))
        dy = dh * (1.0 + sc_ref[...])
        gv_ref[2:3, :] += _colsum(dy * xh)
        dxh = dy * nw
        gx_ref[...] = dx2_ref[...] + r * (dxh - xh * _rowmean(dxh * xh))

    const = lambda shape: pl.BlockSpec(shape, lambda i: (0, 0))
    rowb = lambda w: pl.BlockSpec((ts, w), lambda i: (i, 0))
    side_in, side_out, side_shapes, side_sems = _scatter_operands(parts, False)
    outs = pl.pallas_call(
        _scatter_alongside(body, 10, 2, len(parts), s // ts - 1, _chip_scatter_copies), grid=(s // ts,),
        in_specs=[rowb(A_COLS), rowb(L_COLS), rowb(G_COLS), const((D, A_COLS)), const((D, L_COLS)),
                  const((D, G_COLS)), rowb(D), rowb(D), const((1, D)), const((1, D))] + side_in,
        out_specs=[rowb(D), const((8, D))] + side_out,
        out_shape=[jax.ShapeDtypeStruct((s, D), F32), jax.ShapeDtypeStruct((8, D), F32)] + side_shapes,
        scratch_shapes=side_sems,
        name="input_bwd", compiler_params=_params(("arbitrary",), VMEM_BIG))(
            dpa, dpl, dpg, wa, wl, wg, x, dx2, norm_w, scale, *parts)
    return outs[0], outs[1], list(outs[2:])


def _adamw_math(w, g, m, v):
    nm = ADAM_B1 * m + (1.0 - ADAM_B1) * g
    nv = ADAM_B2 * v + (1.0 - ADAM_B2) * (g * g)
    m_hat = nm / (1.0 - ADAM_B1 ** ADAM_STEP)
    v_hat = nv / (1.0 - ADAM_B2 ** ADAM_STEP)
    return -ADAM_LR * (m_hat / (jnp.sqrt(v_hat) + ADAM_EPS) + ADAM_WD * w), nm, nv


def _adamw(w, g, m, v, tr, name):
    lead, (rows, cols) = w.shape[:-2], w.shape[-2:]

    def body(w_ref, g_ref, m_ref, v_ref, d_ref, nm_ref, nv_ref):
        d_ref[...], nm_ref[...], nv_ref[...] = _adamw_math(w_ref[...], g_ref[...], m_ref[...], v_ref[...])

    blk = pl.BlockSpec((1,) * len(lead) + (tr, cols), lambda i: (0,) * len(lead) + (i, 0))
    shp = jax.ShapeDtypeStruct(w.shape, F32)
    return pl.pallas_call(
        body, grid=(rows // tr,), in_specs=[blk] * 4, out_specs=[blk] * 3, out_shape=[shp] * 3, name=name,
        compiler_params=_params(("parallel",), VMEM_BIG))(w, g.reshape(w.shape), m, v)


ROW_SHIFT, ROW_SCALE, ROW_NORM_W = 0, 1, 2
ROW_FINAL_NORM_W, ROW_GATE, ROW_LOSS = 8, 9, 10
ROW_LN_W, ROW_LN_B, ROW_CONV_B = 16, 17, 18
ROW_Q_NORM_W, ROW_KV_NORM_W = 24, 25
ROW_CONV_W = 32
SUM_ROWS = 64
VECTOR_ROWS = ((ROW_SHIFT, ROW_SCALE, ROW_GATE), (ROW_NORM_W,), (ROW_CONV_B,), (ROW_LN_W,), (ROW_LN_B,),
               (ROW_Q_NORM_W,), (ROW_KV_NORM_W,), (ROW_FINAL_NORM_W,))


def _small_finalize(gathered, vectors, conv, chip):
    n = len(vectors)
    cw = conv[0].shape[2]

    def body(chip_ref, g_ref, *refs):
        ins, outs = refs[:3 * n + 3], refs[3 * n + 3:]
        tot = g_ref[0]
        for k in range(1, N_DEV):
            tot = tot + g_ref[k]
        for p, rows in enumerate(VECTOR_ROWS):
            w_ref, m_ref, v_ref = ins[3 * p:3 * p + 3]
            g_out, d_out, nm_out, nv_out = outs[4 * p:4 * p + 4]
            width = w_ref.shape[1] // len(rows)
            for q, r in enumerate(rows):
                lanes = slice(q * width, (q + 1) * width)
                g = tot[r:r + 1, 0:width]
                g_out[:, lanes] = g
                d_out[:, lanes], nm_out[:, lanes], nv_out[:, lanes] = _adamw_math(
                    w_ref[:, lanes], g, m_ref[:, lanes], v_ref[:, lanes])
        cols = pl.ds(pl.multiple_of(chip_ref[0] * cw, LANE), cw)
        gc = g_ref[0, pl.ds(ROW_CONV_W, KC), cols]
        for k in range(1, N_DEV):
            gc = gc + g_ref[k, pl.ds(ROW_CONV_W, KC), cols]
        cw_ref, cm_ref, cv_ref = ins[3 * n:3 * n + 3]
        g_out, d_out, nm_out, nv_out, dmod_ref, loss_ref = outs[4 * n:]
        g_out[0] = gc
        d_out[0], nm_out[0], nv_out[0] = _adamw_math(cw_ref[0], gc, cm_ref[0], cv_ref[0])
        for k in range(N_DEV):
            for q, r in enumerate((ROW_SHIFT, ROW_SCALE, ROW_GATE)):
                dmod_ref[k:k + 1, q * D:(q + 1) * D] = g_ref[k, r:r + 1, :]
        loss_ref[...] = (0.5 / D) * jnp.sum(tot[ROW_LOSS:ROW_LOSS + 1, :], axis=-1, keepdims=True)

    flat_in = [a for triple in vectors for a in triple] + list(conv)
    shapes = [jax.ShapeDtypeStruct(w.shape, F32) for w, _, _ in vectors for _ in range(4)]
    shapes += [jax.ShapeDtypeStruct(conv[0].shape, F32)] * 4
    shapes += [jax.ShapeDtypeStruct((N_DEV, 3 * D), F32), jax.ShapeDtypeStruct((1, 1), F32)]
    whole = pl.BlockSpec(memory_space=pltpu.VMEM)
    return pl.pallas_call(
        body, out_shape=shapes,
        in_specs=[pl.BlockSpec(memory_space=pltpu.SMEM)] + [whole] * (1 + len(flat_in)),
        out_specs=[whole] * len(shapes), name="small_finalize")(chip, gathered, *flat_in)


def _ada_bwd(c_all_t, dmod_shard):
    def body(c_ref, d_ref, o_ref):
        cv = c_ref[...]
        o_ref[...] = jnp.dot(cv * _sigmoid(cv), d_ref[...], preferred_element_type=F32,
                             precision=lax.Precision.HIGHEST)

    return pl.pallas_call(
        body, out_shape=jax.ShapeDtypeStruct((D, dmod_shard.shape[1]), F32), name="ada_bwd")(c_all_t, dmod_shard)


def _sum_chip_slabs(arrived, part, place, tr, name, axis):
    n, rows, cols = arrived.shape
    per = rows // tr
    own_map = ((lambda i, pc: (pc[0], i, 0)) if part.shape[1] == rows
               else (lambda i, pc: (pc[0], pc[1] * per + i, 0)))

    def body(place_ref, a_ref, p_ref, o_ref):
        acc = p_ref[0].astype(F32)
        for k in range(n):
            acc = acc + a_ref[k].astype(F32)
        o_ref[...] = acc

    if axis == 1:
        whole, out_map = (2 * rows, cols), lambda i, pc: (pc[1] * per + i, 0)
    else:
        whole, out_map = (rows, 2 * cols), lambda i, pc: (i, pc[1])
    grid_spec = pltpu.PrefetchScalarGridSpec(
        num_scalar_prefetch=1, grid=(per,),
        in_specs=[pl.BlockSpec((n, tr, cols), lambda i, pc: (0, i, 0)),
                  pl.BlockSpec((1, tr, cols), own_map)],
        out_specs=pl.BlockSpec((tr, cols), out_map))
    return pl.pallas_call(
        body, grid_spec=grid_spec, out_shape=jax.ShapeDtypeStruct(whole, F32), name=name,
        compiler_params=_params(("parallel",)))(place, arrived, part)


def _sum_device_partials(arrived, parts, place):
    n = len(arrived)

    def body(place_ref, *refs):
        a_refs, p_refs, o_refs = refs[:n], refs[n:2 * n], refs[2 * n:]
        for a in range(n):
            acc = p_refs[a][0].astype(F32)
            for k in range(arrived[a].shape[0]):
                acc = acc + a_refs[a][k].astype(F32)
            o_refs[a][...] = acc

    grid_spec = pltpu.PrefetchScalarGridSpec(
        num_scalar_prefetch=1, grid=(1,),
        in_specs=[pl.BlockSpec(a.shape, lambda i, pc: (0, 0, 0)) for a in arrived]
        + [pl.BlockSpec((1,) + a.shape[1:], lambda i, pc: (pc[0], pc[1], 0)) for a in arrived],
        out_specs=[pl.BlockSpec(a.shape[1:], lambda i, pc: (pc[1], 0)) for a in arrived])
    return pl.pallas_call(
        body, grid_spec=grid_spec,
        out_shape=[jax.ShapeDtypeStruct((2 * a.shape[1], a.shape[2]), F32) for a in arrived],
        name="sum_device_partials", compiler_params=_params(("arbitrary",), VMEM_BIG))(place, *arrived, *parts)


def _adamw_many(ws, gs, ms, vs, tr):
    n = len(ws)
    rows = ws[0].shape[1]

    def body(*refs):
        ins, outs = refs[:4 * n], refs[4 * n:]
        for a in range(n):
            w_ref, g_ref, m_ref, v_ref = ins[4 * a:4 * a + 4]
            outs[3 * a][...], outs[3 * a + 1][...], outs[3 * a + 2][...] = _adamw_math(
                w_ref[...], g_ref[...], m_ref[...], v_ref[...])

    blk = lambda w: pl.BlockSpec((1, tr, w.shape[2]), lambda i: (0, i, 0))
    gs = [g.reshape(w.shape) for g, w in zip(gs, ws)]
    flat = [a for quad in zip(ws, gs, ms, vs) for a in quad]
    outs = pl.pallas_call(
        body, grid=(rows // tr,), in_specs=[blk(w) for w in ws for _ in range(4)],
        out_specs=[blk(w) for w in ws for _ in range(3)],
        out_shape=[jax.ShapeDtypeStruct(w.shape, F32) for w in ws for _ in range(3)], name="adamw_small_matrices",
        compiler_params=_params(("parallel",), VMEM_BIG))(*flat)
    return [(gs[a], outs[3 * a], outs[3 * a + 1], outs[3 * a + 2]) for a in range(n)]


def _add_own_half(full, other, core, tr, name, axis):
    n, rows, cols = other.shape
    per = rows // tr

    def body(c_ref, f_ref, o_ref, out_ref):
        out_ref[...] = (f_ref[...].astype(F32) + o_ref[...].astype(F32)).astype(BF16)

    full_map = (lambda k, i, c: (k, c[0] * per + i, 0)) if axis == 1 else (lambda k, i, c: (k, i, c[0]))
    grid_spec = pltpu.PrefetchScalarGridSpec(
        num_scalar_prefetch=1, grid=(n, per),
        in_specs=[pl.BlockSpec((1, tr, cols), full_map),
                  pl.BlockSpec((1, tr, cols), lambda k, i, c: (k, i, 0))],
        out_specs=pl.BlockSpec((1, tr, cols), lambda k, i, c: (k, i, 0)))
    return pl.pallas_call(
        body, grid_spec=grid_spec, out_shape=jax.ShapeDtypeStruct((n, rows, cols), BF16), name=name,
        compiler_params=_params(("parallel", "parallel"), VMEM_BIG))(core, full, other)


def _allgather8_run(x_ref, out_ref, send_sems, recv_sems, local_sem):
    m = x_ref.shape[0]
    x, y, c = _coords()
    me, sibling = (x, y, c), (x, y, 1 - c)
    chips = [(1 - x, y), (x, 1 - y), (1 - x, 1 - y)]

    def rows(px, py, pc):
        return out_ref.at[pl.ds(pl.multiple_of((4 * px + 2 * py + pc) * m, 8), m), :]

    def copy(k, blk, to, source=None):
        return pltpu.make_async_remote_copy(
            src_ref=rows(*blk) if source is None else source, dst_ref=rows(*blk),
            send_sem=send_sems.at[k], recv_sem=recv_sems.at[k], device_id=to, device_id_type=MESH)

    mine = pltpu.make_async_copy(x_ref, rows(*me), local_sem)
    mine.start()
    first = [copy(0, me, sibling, source=x_ref)]
    first += [copy(1 + j, me, (*chip, c), source=x_ref) for j, chip in enumerate(chips)]
    for cp in first:
        cp.start()
    passed = [copy(4 + j, (*chip, c), sibling) for j, chip in enumerate(chips)]
    for j, chip in enumerate(chips):
        copy(1 + j, (*chip, c), me).wait_recv()
        passed[j].start()
    copy(0, sibling, me).wait_recv()
    for j, chip in enumerate(chips):
        copy(4 + j, (*chip, 1 - c), me).wait_recv()
    for cp in first + passed:
        cp.wait_send()
    mine.wait()


ALLGATHER8_SEMS = [pltpu.SemaphoreType.DMA((7,)), pltpu.SemaphoreType.DMA((7,)), pltpu.SemaphoreType.DMA]


def _gather_plan(x_refs, out_refs, send_sems, recv_sems, local_sems):
    n = len(x_refs)
    halves = [r.shape[0] // 2 for r in x_refs]
    x, y, c = _coords()
    me, sibling = (x, y, c), (x, y, 1 - c)
    chips = [(1 - x, y), (x, 1 - y), (1 - x, 1 - y)]

    def src(a):
        return x_refs[a].at[pl.ds(pl.multiple_of(c * halves[a], 16), halves[a]), :]

    def blk(a, px, py, pc):
        return out_refs[a].at[4 * px + 2 * py + pc]

    def copy(a, k, who, to, source=None):
        return pltpu.make_async_remote_copy(
            src_ref=blk(a, *who) if source is None else source, dst_ref=blk(a, *who),
            send_sem=send_sems.at[7 * a + k], recv_sem=recv_sems.at[7 * a + k], device_id=to, device_id_type=MESH)

    def mine(a):
        return pltpu.make_async_copy(src(a), blk(a, *me), local_sems.at[a])

    def first(a):
        return ([copy(a, 0, me, sibling, source=src(a))]
                + [copy(a, 1 + j, me, (*chip, c), source=src(a)) for j, chip in enumerate(chips)])

    def begin():
        for a in range(n):
            mine(a).start()
        for a in range(n):
            for cp in first(a):
                cp.start()

    def finish():
        onward = []
        for j, chip in enumerate(chips):
            for a in range(n):
                copy(a, 1 + j, (*chip, c), me).wait_recv()
                onward.append(copy(a, 4 + j, (*chip, c), sibling))
                onward[-1].start()
        for a in range(n):
            copy(a, 0, sibling, me).wait_recv()
        for j, chip in enumerate(chips):
            for a in range(n):
                copy(a, 4 + j, (*chip, 1 - c), me).wait_recv()
        for a in range(n):
            for cp in first(a):
                cp.wait_send()
        for cp in onward:
            cp.wait_send()
        for a in range(n):
            mine(a).wait()

    return begin, finish


def _gather_operands(shards):
    n = len(shards)
    shapes = [jax.ShapeDtypeStruct((N_DEV, a.shape[0] // 2, a.shape[1]), a.dtype) for a in shards]
    sems = [pltpu.SemaphoreType.DMA((7 * n,)), pltpu.SemaphoreType.DMA((7 * n,)), pltpu.SemaphoreType.DMA((n,))]
    return shapes, sems


def _as_chip_slabs(gathered, shards):
    return [o.reshape(N_CHIP, a.shape[0], a.shape[1]) for o, a in zip(gathered, shards)]


def _gather_alongside(body, n_in, n_out, n_shards, last_step, first=None):
    def wrapped(*refs):
        ins, shards = refs[:n_in], refs[n_in:n_in + n_shards]
        rest = refs[n_in + n_shards:]
        outs, gathered = rest[:n_out], rest[n_out:n_out + n_shards]
        scratch, sems = rest[n_out + n_shards:-3], rest[-3:]

        @pl.when(pl.program_id(0) == 0)
        def _():
            if first is not None:
                first(*ins, *outs, *scratch)
            _gather_plan(shards, gathered, *sems)[0]()

        body(*ins, *outs, *scratch)

        @pl.when(pl.program_id(0) == last_step)
        def _():
            _gather_plan(shards, gathered, *sems)[1]()

    return wrapped


def _half(ref, axis, which, ndim):
    size = ref.shape[axis] // 2
    idx = [slice(None)] * ndim
    idx[axis] = pl.ds(pl.multiple_of(which * size, 8 if axis == ndim - 2 else LANE), size)
    return ref.at[tuple(idx)]


def _swap_halves_with_sibling(fulls, name, axes):
    n = len(fulls)

    def body(*refs):
        f_refs, got_refs = refs[:n], refs[n:2 * n]
        send_sems, recv_sems = refs[2 * n:]
        x, y, c = _coords()
        copies = []
        for a in range(n):
            copies.append(pltpu.make_async_remote_copy(
                src_ref=_half(f_refs[a], axes[a], 1 - c, 3), dst_ref=got_refs[a], send_sem=send_sems.at[a],
                recv_sem=recv_sems.at[a], device_id=(x, y, 1 - c), device_id_type=MESH))
        for cp in copies:
            cp.start()
        for cp in copies:
            cp.wait()

    def halved(a, axis):
        shape = list(a.shape)
        shape[axis] //= 2
        return jax.ShapeDtypeStruct(tuple(shape), a.dtype)

    return pl.pallas_call(
        body, out_shape=[halved(a, ax) for a, ax in zip(fulls, axes)],
        in_specs=[HBM_REF] * n, out_specs=[HBM_REF] * n,
        scratch_shapes=[pltpu.SemaphoreType.DMA((n,)), pltpu.SemaphoreType.DMA((n,))],
        name=name)(*fulls)


def _join_halves_with_sibling(wholes, axes, block):
    n = len(wholes)
    twice = jnp.concatenate([block, block], axis=0)
    gathered_shapes, gather_sems = _gather_operands([twice])

    def body(*refs):
        out_refs = refs[n + 1:2 * n + 1]
        send_sems, recv_sems = refs[2 * n + 2:2 * n + 4]
        begin, finish = _gather_plan([refs[n]], [refs[2 * n + 1]], *refs[2 * n + 4:])
        x, y, c = _coords()

        def push(a, core):
            half = _half(out_refs[a], axes[a] - 1, core, 2)
            return pltpu.make_async_remote_copy(
                src_ref=half, dst_ref=half, send_sem=send_sems.at[a], recv_sem=recv_sems.at[a],
                device_id=(x, y, 1 - c), device_id_type=MESH)

        begin()
        for a in range(n):
            push(a, c).start()
        finish()
        for a in range(n):
            push(a, 1 - c).wait_recv()
        for a in range(n):
            push(a, c).wait_send()

    outs = pl.pallas_call(
        body, out_shape=[jax.ShapeDtypeStruct(a.shape, a.dtype) for a in wholes] + gathered_shapes,
        in_specs=[HBM_REF] * (n + 1), out_specs=[HBM_REF] * (n + 1), input_output_aliases={a: a for a in range(n)},
        scratch_shapes=[pltpu.SemaphoreType.DMA((n,)), pltpu.SemaphoreType.DMA((n,))] + gather_sems,
        name="rs_pair_join")(*wholes, twice)
    return outs[:n], outs[n]


def _cols_to_slabs(g):
    rows, cols = g.shape
    return g.reshape(rows, N_CHIP, cols // N_CHIP).transpose(1, 0, 2)


def _slabs_to_cols(w):
    n, rows, cols = w.shape
    return w.transpose(1, 0, 2).reshape(rows, n * cols)


def _col_window(slabs, start, stop):
    n = slabs.shape[2]
    pieces = []
    for k in range(N_CHIP):
        lo, hi = max(start, k * n), min(stop, (k + 1) * n)
        if lo < hi:
            pieces.append(slabs[k][:, lo - k * n:hi - k * n])
    return pieces[0] if len(pieces) == 1 else jnp.concatenate(pieces, axis=1)


def _slabs_from_groups(groups, n):
    slabs = []
    for k in range(N_CHIP):
        pieces, off = [], 0
        for g in groups:
            lo, hi = max(k * n, off), min((k + 1) * n, off + g.shape[0])
            if lo < hi:
                pieces.append(g[lo - off:hi - off])
            off += g.shape[0]
        slabs.append(pieces[0] if len(pieces) == 1 else jnp.concatenate(pieces, axis=0))
    return jnp.stack(slabs)


def _uq_to_padded(w_uq):
    per = w_uq.reshape(RQ, H, DN + DR)
    nope = per[:, :, :DN].reshape(RQ, H * DN)
    rope = jnp.pad(per[:, :, DN:], ((0, 0), (0, 0), (0, LANE - DR))).reshape(RQ, H * LANE)
    return jnp.concatenate([nope, rope], axis=1)


def _uq_from_padded(g):
    nope = g[:, :H * DN].reshape(RQ, H, DN)
    rope = g[:, H * DN:].reshape(RQ, H, LANE)[:, :, :DR]
    return jnp.concatenate([nope, rope], axis=2).reshape(RQ, H * (DN + DR))


def _rope_tables(positions):
    inv_freq = ROPE_THETA ** (-jnp.arange(0, DR, 2, dtype=F32) / DR)
    ang = positions.astype(F32)[:, None] * inv_freq
    cos, sin = jnp.cos(ang), jnp.sin(ang)
    return jnp.tile(cos, (1, 4)), jnp.tile(jnp.concatenate([-sin, sin], axis=1), (1, 2))


def _pair_sums(fulls, core, tag, axes, tr):
    from_sibling = _swap_halves_with_sibling(fulls, f"rs_pair_swap_{tag}", axes)
    return [_add_own_half(f, o, core, min(tr, o.shape[1]), f"add_own_half_{tag}{n}", ax)
            for n, (f, o, ax) in enumerate(zip(fulls, from_sibling, axes))]


def _local_step(x, tgt, cos_t, sin_t, ada, weights, small, tiles, place):
    ts, ts_in, ts_mla, tm_nn, tm_tn, t_attn, chunk = tiles
    w_in_shard, later_shards = weights
    norm_w, conv_b, ln_w, ln_b, q_norm_w, kv_norm_w, fnw = small
    h, mod, c_all, conv_w, (g_in,) = _adaln_norm(x, norm_w, *ada, ts, [w_in_shard])
    scale, gate = mod[:, D:2 * D], mod[:, 2 * D:3 * D]
    wa = _col_window(g_in, 0, A_COLS)
    wl = jnp.pad(_col_window(g_in, A_COLS, A_COLS + L_COLS_RAW), ((0, 0), (0, L_COLS - L_COLS_RAW)))
    wg = _col_window(g_in, A_COLS + L_COLS_RAW, IN_COLS)
    proj_a = _mm_nn(h, wa, tm_nn, D, "proj_a")
    u0, u1, za, (g_uq, g_ukv, g_co, g_ao, g_o) = _conv_fwd(proj_a, conv_w, conv_b, ln_w, ln_b, ts, chunk, later_shards)
    w_uq2, w_ukv = _uq_to_padded(_slabs_to_cols(g_uq)), _slabs_to_cols(g_ukv)
    wco, wao, wo = g_co.reshape(D, D), g_ao.reshape(D, D), g_o.reshape(D, D)
    proj_l = _mm_nn(h, wl, tm_nn, L_COLS, "proj_l")
    proj_g = _mm_nn(h, wg, tm_nn, D, "proj_g")
    qn, kvn, q, k, v = _mla_prep(proj_l, q_norm_w, kv_norm_w, w_uq2, w_ukv, cos_t, sin_t, ts_mla)
    o, lse = _attn_fwd(q, k, v, t_attn)
    dx2, dza, do, delta, dpg, lhs3, rhs3, vec_mid = _middle(za, o, proj_g, x, tgt, gate, fnw, wco, wao, wo, ts)
    g_wco, g_wao, g_wo = _mm_tn_stack(lhs3, rhs3, tm_tn, "grad_w_out3")
    dq, dk, dv = _attn_bwd(q, k, v, do, lse, delta, t_attn)
    dpl, g_wuq2, g_wukv, vec_mla = _mla_prep_bwd(
        dq, dk, dv, proj_l, qn, kvn, q_norm_w, kv_norm_w, w_uq2, w_ukv, cos_t, sin_t, ts_mla)

    core = place[1:2]
    nr = D // N_CHIP
    early = [_cols_to_slabs(_uq_from_padded(g_wuq2)).astype(BF16), _cols_to_slabs(g_wukv).astype(BF16),
             g_wco.reshape(N_CHIP, nr, D), g_wao.reshape(N_CHIP, nr, D), g_wo.reshape(N_CHIP, nr, D)]
    dpa, g_conv_w, vec_conv, early_got = _conv_bwd(dza, proj_a, u0, u1, conv_w, ln_w, ln_b, ts, chunk, early)

    g_wa_t = _mm_tn(dpa, h, tm_tn, D, D, "grad_w_in_a", BF16)
    g_wl_t = _mm_tn(dpl, h, tm_tn, L_COLS, D, "grad_w_in_l", BF16)
    g_wg_t = _mm_tn(dpg, h, tm_tn, D, D, "grad_w_in_g", BF16)
    g_w_in_slabs = _slabs_from_groups([g_wa_t, g_wl_t[0:L_COLS_RAW], g_wg_t], IN_COLS // N_CHIP)
    late_sums = _pair_sums([g_w_in_slabs], core, "b", [2], IN_COLS // N_CHIP)
    grad_x, vec_in, late_got = _input_bwd(dpa, dpl, dpg, wa, wl, wg, x, dx2, norm_w, scale, ts_in, late_sums)

    col_sums = jnp.concatenate(
        [vec_in, vec_mid, vec_conv, jnp.pad(vec_mla, ((0, 0), (0, D - RQ))), g_conv_w], axis=0)
    wholes = ([_sum_chip_slabs(late_got[0], late_sums[0], place, W_IN_ROWS, "sum_chip_slabs_w_in", 2)]
              + list(_sum_device_partials(early_got, early, place)))
    shards, all_col_sums = _join_halves_with_sibling(wholes, [2] + [1] * len(early), col_sums)

    return grad_x, shards, all_col_sums, c_all


def kernel(x, c, positions, w_ada, b_ada, norm_w, w_in, conv_w, conv_b, conv_ln_w, conv_ln_b, w_conv_out, q_norm_w, w_uq, kv_norm_w, w_ukv, w_attn_out, w_out, final_norm_w, loss_target, m_w_ada, m_b_ada, m_norm_w, m_w_in, m_conv_w, m_conv_b, m_conv_ln_w, m_conv_ln_b, m_w_conv_out, m_q_norm_w, m_w_uq, m_kv_norm_w, m_w_ukv, m_w_attn_out, m_w_out, m_final_norm_w, v_w_ada, v_b_ada, v_norm_w, v_w_in, v_conv_w, v_conv_b, v_conv_ln_w, v_conv_ln_b, v_w_conv_out, v_q_norm_w, v_w_uq, v_kv_norm_w, v_w_ukv, v_w_attn_out, v_w_out, v_final_norm_w):
    ix, iy, ic = _coords()
    chip = 2 * ix + iy
    dev = 4 * ix + 2 * iy + ic
    s = x.shape[1]
    tiles = (256, 512, 512, 1024, 2048, 512, 32)

    conv_w_pad = jnp.pad(conv_w[0], ((0, HALO - KC), (0, D - conv_w.shape[2])))
    small_in = jnp.concatenate([jnp.pad(c, ((0, 7), (0, 0))), conv_w_pad], axis=0)

    later_shards = [w[0].astype(BF16) for w in (w_uq, w_ukv, w_conv_out, w_attn_out, w_out)]
    weights = (w_in[0].astype(BF16), later_shards)

    ada_cols = w_ada.shape[2]
    b_shard = lax.dynamic_slice(b_ada, (0, chip * ada_cols), (1, ada_cols))
    ada = (small_in, w_ada[0], b_shard, dev.reshape(1).astype(jnp.int32))

    cos_t, sin_t = _rope_tables(positions[0])
    small = (norm_w, conv_b, conv_ln_w, conv_ln_b, q_norm_w, kv_norm_w, final_norm_w.reshape(1, D))
    place = jnp.stack([chip, ic]).astype(jnp.int32)
    grad_x, shards, gathered, c_all = _local_step(x[0], loss_target[0], cos_t, sin_t, ada, weights, small, tiles, place)
    g_w_in_s, g_w_uq_s, g_w_ukv_s, g_wco_s, g_wao_s, g_wo_s = shards

    vec_names = ("b_ada", "norm_w", "conv_b", "conv_ln_w", "conv_ln_b", "q_norm_w", "kv_norm_w", "final_norm_w")
    row = lambda a: a.reshape(1, -1)
    vectors = [(row(b_ada), row(m_b_ada), row(v_b_ada)), (norm_w, m_norm_w, v_norm_w), (conv_b, m_conv_b, v_conv_b),
               (conv_ln_w, m_conv_ln_w, v_conv_ln_w), (conv_ln_b, m_conv_ln_b, v_conv_ln_b),
               (q_norm_w, m_q_norm_w, v_q_norm_w), (kv_norm_w, m_kv_norm_w, v_kv_norm_w),
               (row(final_norm_w), row(m_final_norm_w), row(v_final_norm_w))]
    fin = _small_finalize(gathered, vectors, (conv_w, m_conv_w, v_conv_w), place[0:1])
    res = {}
    for p, (name, (w, _, _)) in enumerate(zip(vec_names, vectors)):
        shape = final_norm_w.shape if name == "final_norm_w" else w.shape
        res[name] = tuple(a.reshape(shape) for a in fin[4 * p:4 * p + 4])
    res["conv_w"] = tuple(fin[4 * len(vectors):4 * len(vectors) + 4])
    dmod_all, loss = fin[-2], fin[-1].reshape(())
    dmod_shard = lax.dynamic_slice(dmod_all, (0, chip * ada_cols), (N_DEV, ada_cols))
    g_w_ada = _ada_bwd(c_all.T, dmod_shard).reshape(1, D, ada_cols)

    def big(w, g, m, v, tr, name):
        d, nm, nv = _adamw(w, g, m, v, tr, name)
        return g.reshape(w.shape), d, nm, nv

    res["w_ada"] = big(w_ada, g_w_ada[0], m_w_ada, v_w_ada, 256, "adamw_w_ada")
    t_in = [a[0].T for a in (w_in, m_w_in, v_w_in)]
    d_t, nm_t, nv_t = _adamw(t_in[0], g_w_in_s, t_in[1], t_in[2], W_IN_ROWS, "adamw_w_in")
    res["w_in"] = tuple(a.T[None] for a in (g_w_in_s, d_t, nm_t, nv_t))
    small = _adamw_many(
        [w_uq, w_ukv, w_conv_out, w_attn_out, w_out], [g_w_uq_s, g_w_ukv_s, g_wco_s, g_wao_s, g_wo_s],
        [m_w_uq, m_w_ukv, m_w_conv_out, m_w_attn_out, m_w_out], [v_w_uq, v_w_ukv, v_w_conv_out, v_w_attn_out, v_w_out],
        128)
    res["w_uq"], res["w_ukv"], res["w_conv_out"], res["w_attn_out"], res["w_out"] = small

    order = ("w_ada", "b_ada", "norm_w", "w_in", "conv_w", "conv_b", "conv_ln_w", "conv_ln_b", "w_conv_out",
             "q_norm_w", "w_uq", "kv_norm_w", "w_ukv", "w_attn_out", "w_out", "final_norm_w")
    outs = [loss, grad_x[None]]
    for slot in range(4):
        outs += [res[name][slot] for name in order]
    return tuple(outs)
```

```python
import functools

import numpy as np
import jax
import jax.numpy as jnp
from jax import lax
from jax.experimental import pallas as pl
from jax.experimental.pallas import tpu as pltpu

F32 = jnp.float32
BF16 = jnp.bfloat16
MESH = pl.DeviceIdType.MESH

D = 1024
H = 8
DN = 128
DR = 64
RQ = 256
KC = 31
HALO = 32
EPS = 1e-6
ROPE_THETA = 10000.0
N_CHIP = 4
N_DEV = 8
LANE = 128
VMEM_BIG = 56 * 1024 * 1024

ADAM_LR = 0.001
ADAM_B1 = 0.9
ADAM_B2 = 0.999
ADAM_EPS = 1e-08
ADAM_WD = 0.01
ADAM_STEP = 10

A_COLS = 3 * D
L_COLS_RAW = RQ + RQ + DR
L_COLS = 640
G_COLS = 3 * D
IN_COLS = A_COLS + L_COLS_RAW + G_COLS


def _params(sem=None, vmem=None):
    kw = {}
    if sem is not None:
        kw["dimension_semantics"] = sem
    if vmem is not None:
        kw["vmem_limit_bytes"] = vmem
    return pltpu.CompilerParams(**kw)


def _dot(a, b):
    return jnp.dot(a, b, preferred_element_type=F32)


def _dot_nt(a, b):
    return lax.dot_general(a, b, (((1,), (1,)), ((), ())), preferred_element_type=F32)


def _dot_tn(a, b):
    return lax.dot_general(a, b, (((0,), (0,)), ((), ())), preferred_element_type=F32)


def _colsum(v):
    return jnp.sum(v, axis=0, keepdims=True)


def _rowmean(v):
    return jnp.mean(v, axis=-1, keepdims=True)


def _sigmoid(v):
    return jax.nn.sigmoid(v)


def _dsilu(v, s):
    return s * (1.0 + v * (1.0 - s))


def _swap_halves(v, first_half):
    return jnp.where(first_half, pltpu.roll(v, 96, 1), pltpu.roll(v, 32, 1))


def _first_half_mask(rows):
    lane = lax.broadcasted_iota(jnp.int32, (rows, LANE), 1)
    return (lane % 64) < 32


SMALL_IN_ROWS = 8 + HALO


def _adaln_norm(x, norm_w, small_in, w_ada_shard, b_ada_shard, dev, ts, shards):
    s = x.shape[0]
    cols = w_ada_shard.shape[1]
    taps = D // N_CHIP

    def modulation(dev_ref, x_ref, nw_ref, sm_ref, w_ref, b_ref, h_ref, mod_ref, c_ref, conv_ref,
                   part_sc, all_sc, small_sc, *sems):
        _allgather8_run(sm_ref, small_sc, *sems[0:3])
        for k in range(N_DEV):
            c_ref[k:k + 1, :] = small_sc[SMALL_IN_ROWS * k:SMALL_IN_ROWS * k + 1, :]
        for k in range(N_CHIP):
            base = SMALL_IN_ROWS * 2 * k + 8
            conv_ref[:, taps * k:taps * (k + 1)] = small_sc[base:base + HALO, 0:taps]
        cv = c_ref[...]
        part_sc[...] = jnp.dot(cv * _sigmoid(cv), w_ref[...], preferred_element_type=F32,
                               precision=lax.Precision.HIGHEST) + b_ref[...]
        _allgather8_run(part_sc, all_sc, *sems[3:6])
        for k in range(N_CHIP):
            mod_ref[:, cols * k:cols * (k + 1)] = all_sc[pl.ds(2 * N_DEV * k + dev_ref[0], 1), :]

    def body(dev_ref, x_ref, nw_ref, sm_ref, w_ref, b_ref, h_ref, mod_ref, *rest):
        xv = x_ref[...]
        r = lax.rsqrt(_rowmean(xv * xv) + EPS)
        y = xv * r * nw_ref[...]
        h_ref[...] = (y * (1.0 + mod_ref[:, D:2 * D]) + mod_ref[:, 0:D]).astype(BF16)

    row = pl.BlockSpec((ts, D), lambda i: (i, 0))
    const = lambda shape: pl.BlockSpec(shape, lambda i: (0, 0))
    n = len(shards)
    gathered_shapes, sems = _gather_operands(shards)
    outs = pl.pallas_call(
        _gather_alongside(body, 6, 4, n, s // ts - 1, modulation), grid=(s // ts,),
        in_specs=[pl.BlockSpec(memory_space=pltpu.SMEM), row, const((1, D)), const(small_in.shape),
                  const(w_ada_shard.shape), const((1, cols))] + [HBM_REF] * n,
        out_specs=[row, const((1, 3 * D)), const((N_DEV, D)), const((HALO, D))] + [HBM_REF] * n,
        out_shape=[jax.ShapeDtypeStruct((s, D), BF16), jax.ShapeDtypeStruct((1, 3 * D), F32),
                   jax.ShapeDtypeStruct((N_DEV, D), F32), jax.ShapeDtypeStruct((HALO, D), F32)] + gathered_shapes,
        scratch_shapes=[pltpu.VMEM((N_DEV, cols), F32), pltpu.VMEM((N_DEV * N_DEV, cols), F32),
                        pltpu.VMEM((N_DEV * SMALL_IN_ROWS, D), F32)] + ALLGATHER8_SEMS + ALLGATHER8_SEMS + sems,
        name="adaln_norm", compiler_params=_params(("arbitrary",), VMEM_BIG))(
            dev, x, norm_w, small_in, w_ada_shard, b_ada_shard, *shards)
    return outs[0], outs[1], outs[2], outs[3], _as_chip_slabs(outs[4:], shards)


def _mm_nn(a, b, tm, tn, name):
    m, k = a.shape
    n = b.shape[1]

    def body(a_ref, b_ref, o_ref):
        o_ref[...] = _dot(a_ref[...], b_ref[...])

    return pl.pallas_call(
        body, grid=(n // tn, m // tm),
        in_specs=[pl.BlockSpec((tm, k), lambda j, i: (i, 0)), pl.BlockSpec((k, tn), lambda j, i: (0, j))],
        out_specs=pl.BlockSpec((tm, tn), lambda j, i: (i, j)),
        out_shape=jax.ShapeDtypeStruct((m, n), F32), name=name,
        compiler_params=_params(("parallel", "parallel"), VMEM_BIG))(a, b)


def _mm_tn(a, b, tm, tk, tn, name, out_dtype=F32):
    m, k = a.shape
    n = b.shape[1]
    steps = m // tm

    def body(a_ref, b_ref, o_ref, acc_ref):
        @pl.when(pl.program_id(2) == 0)
        def _():
            acc_ref[...] = jnp.zeros_like(acc_ref)
        acc_ref[...] += _dot_tn(a_ref[...], b_ref[...])

        @pl.when(pl.program_id(2) == steps - 1)
        def _():
            o_ref[...] = acc_ref[...].astype(out_dtype)

    return pl.pallas_call(
        body, grid=(k // tk, n // tn, steps),
        in_specs=[pl.BlockSpec((tm, tk), lambda r, j, i: (i, r)), pl.BlockSpec((tm, tn), lambda r, j, i: (i, j))],
        out_specs=pl.BlockSpec((tk, tn), lambda r, j, i: (r, j)),
        out_shape=jax.ShapeDtypeStruct((k, n), out_dtype), scratch_shapes=[pltpu.VMEM((tk, tn), F32)], name=name,
        compiler_params=_params(("parallel", "parallel", "arbitrary"), VMEM_BIG))(a, b)


def _mm_tn_stack(a, b, tm, name):
    n_stack, m, k = a.shape
    n = b.shape[2]
    steps = m // tm

    def body(a_ref, b_ref, o_ref, acc_ref):
        @pl.when(pl.program_id(1) == 0)
        def _():
            acc_ref[...] = jnp.zeros_like(acc_ref)
        acc_ref[...] += _dot_tn(a_ref[0], b_ref[0])

        @pl.when(pl.program_id(1) == steps - 1)
        def _():
            o_ref[0] = acc_ref[...].astype(BF16)

    out = pl.pallas_call(
        body, grid=(n_stack, steps),
        in_specs=[pl.BlockSpec((1, tm, k), lambda g, i: (g, i, 0)), pl.BlockSpec((1, tm, n), lambda g, i: (g, i, 0))],
        out_specs=pl.BlockSpec((1, k, n), lambda g, i: (g, 0, 0)),
        out_shape=jax.ShapeDtypeStruct((n_stack, k, n), BF16), scratch_shapes=[pltpu.VMEM((k, n), F32)], name=name,
        compiler_params=_params(("parallel", "arbitrary"), VMEM_BIG))(a, b)
    return [out[g] for g in range(n_stack)]


def _coords():
    return lax.axis_index("x"), lax.axis_index("y"), lax.axis_index("c")


HBM_REF = pl.BlockSpec(memory_space=pl.ANY)


def _chip_scatter_copies(p_refs, got_refs, send_sems, recv_sems):
    x, y, c = _coords()
    copies = []
    for a in range(len(p_refs)):
        for j, (px, py) in enumerate([(1 - x, y), (x, 1 - y), (1 - x, 1 - y)]):
            copies.append(pltpu.make_async_remote_copy(
                src_ref=p_refs[a].at[2 * px + py], dst_ref=got_refs[a].at[j], send_sem=send_sems.at[3 * a + j],
                recv_sem=recv_sems.at[3 * a + j], device_id=(px, py, c), device_id_type=MESH))
    return copies


RELATIONS = [(dx, dy, dc) for dx in (0, 1) for dy in (0, 1) for dc in (0, 1)][1:]


def _device_scatter_copies(p_refs, got_refs, send_sems, recv_sems):
    x, y, c = _coords()
    copies = []
    for a in range(len(p_refs)):
        half = p_refs[a].shape[1] // 2
        for j, (dx, dy, dc) in enumerate(RELATIONS):
            px, py, pc = (1 - x if dx else x), (1 - y if dy else y), (1 - c if dc else c)
            src = p_refs[a].at[2 * px + py, pl.ds(pl.multiple_of(pc * half, 16), half), :]
            copies.append(pltpu.make_async_remote_copy(
                src_ref=src, dst_ref=got_refs[a].at[j], send_sem=send_sems.at[7 * a + j],
                recv_sem=recv_sems.at[7 * a + j], device_id=(px, py, pc), device_id_type=MESH))
    return copies


def _scatter_alongside(body, n_in, n_out, n_parts, last_step, make_copies):
    def wrapped(*refs):
        ins, parts = refs[:n_in], refs[n_in:n_in + n_parts]
        rest = refs[n_in + n_parts:]
        outs, got = rest[:n_out], rest[n_out:n_out + n_parts]
        scratch, (send_sems, recv_sems) = rest[n_out + n_parts:-2], rest[-2:]

        @pl.when(pl.program_id(0) == 0)
        def _():
            for cp in make_copies(parts, got, send_sems, recv_sems):
                cp.start()

        body(*ins, *outs, *scratch)

        @pl.when(pl.program_id(0) == last_step)
        def _():
            for cp in make_copies(parts, got, send_sems, recv_sems):
                cp.wait()

    return wrapped


def _scatter_operands(parts, per_device):
    n = len(parts)
    if per_device:
        slots, shapes = 7, [jax.ShapeDtypeStruct((7, a.shape[1] // 2, a.shape[2]), a.dtype) for a in parts]
    else:
        slots, shapes = 3, [jax.ShapeDtypeStruct((3,) + a.shape[1:], a.dtype) for a in parts]
    sems = [pltpu.SemaphoreType.DMA((slots * n,)), pltpu.SemaphoreType.DMA((slots * n,))]
    return [HBM_REF] * n, [HBM_REF] * n, shapes, sems


def _shifted_copies(win_ref, sh_ref, rows):
    for p in range(1, 8):
        sh_ref[p - 1, 0:rows, :] = win_ref[pl.ds(p, rows), :]


def _tap_rows(win_ref, sh_ref, start, rows):
    p = start % 8
    if p == 0:
        return win_ref[pl.ds(start, rows), :]
    return sh_ref[p - 1, pl.ds(start - p, rows), :]


def _conv_taps(win_ref, sh_ref, w_ref, rows, chunk, offset_of_tap):
    pieces = []
    for c0 in range(0, rows, chunk):
        acc = None
        for j in range(KC):
            term = w_ref[j:j + 1, :] * _tap_rows(win_ref, sh_ref, c0 + offset_of_tap(j), chunk)
            acc = term if acc is None else acc + term
        pieces.append(acc)
    return pieces


def _conv_fwd(proj_a, conv_w, conv_b, ln_w, ln_b, ts, chunk, shards):
    s = proj_a.shape[0]

    def body(av_ref, al_ref, ag_ref, w_ref, b_ref, lw_ref, lb_ref, u0_ref, u1_ref, za_ref, win_ref, sh_ref):
        @pl.when(pl.program_id(0) == 0)
        def _():
            win_ref[0:HALO, :] = jnp.zeros((HALO, D), F32)

        u0 = av_ref[...] * _sigmoid(al_ref[...])
        u0_ref[...] = u0
        win_ref[HALO:HALO + ts, :] = u0
        _shifted_copies(win_ref, sh_ref, ts + HALO - 8)
        pieces = _conv_taps(win_ref, sh_ref, w_ref, ts, chunk, lambda j: HALO - (KC - 1) + j)
        for n, acc in enumerate(pieces):
            u1_ref[n * chunk:(n + 1) * chunk, :] = acc + b_ref[...]
        win_ref[0:HALO, :] = win_ref[ts:ts + HALO, :]

        u1 = u1_ref[...]
        xc = u1 - _rowmean(u1)
        rstd = lax.rsqrt(_rowmean(xc * xc) + EPS)
        u2 = xc * rstd * lw_ref[...] + lb_ref[...]
        u3 = u2 * _sigmoid(u2)
        ag = ag_ref[...]
        za_ref[...] = (u3 * (ag * _sigmoid(ag))).astype(BF16)

    col = lambda c: pl.BlockSpec((ts, D), lambda i, c=c: (i, c))
    row = pl.BlockSpec((ts, D), lambda i: (i, 0))
    vec = pl.BlockSpec((1, D), lambda i: (0, 0))
    n = len(shards)
    gathered_shapes, sems = _gather_operands(shards)
    outs = pl.pallas_call(
        _gather_alongside(body, 7, 3, n, s // ts - 1), grid=(s // ts,),
        in_specs=[col(0), col(1), col(2), pl.BlockSpec((HALO, D), lambda i: (0, 0)), vec, vec, vec] + [HBM_REF] * n,
        out_specs=[row, row, row] + [HBM_REF] * n,
        out_shape=[jax.ShapeDtypeStruct((s, D), F32), jax.ShapeDtypeStruct((s, D), F32),
                   jax.ShapeDtypeStruct((s, D), BF16)] + gathered_shapes,
        scratch_shapes=[pltpu.VMEM((ts + HALO, D), F32), pltpu.VMEM((7, ts + HALO, D), F32)] + sems,
        name="conv_fwd", compiler_params=_params(("arbitrary",), VMEM_BIG))(
            proj_a, proj_a, proj_a, conv_w, conv_b, ln_w, ln_b, *shards)
    return outs[0], outs[1], outs[2], _as_chip_slabs(outs[3:], shards)


def _conv_bwd(dza, proj_a, u0, u1, conv_w, ln_w, ln_b, ts, chunk, parts):
    s = dza.shape[0]
    nt = s // ts
    per = ts // HALO

    def body(dza_ref, av_ref, al_ref, ag_ref, u0_ref, u0p_ref, u1_ref, w_ref, lw_ref, lb_ref,
             dpa_ref, gw_ref, gv_ref, dwin_ref, uwin_ref, du0_ref, gwp_ref, dsh_ref, ush_ref):
        step = pl.program_id(0)
        tile = nt - 1 - step

        @pl.when(step == 0)
        def _():
            dwin_ref[ts:ts + HALO, :] = jnp.zeros((HALO, D), F32)
            gwp_ref[...] = jnp.zeros_like(gwp_ref)
            gv_ref[...] = jnp.zeros_like(gv_ref)

        ag = ag_ref[...]
        sg = _sigmoid(ag)
        u1 = u1_ref[...]
        xc = u1 - _rowmean(u1)
        rstd = lax.rsqrt(_rowmean(xc * xc) + EPS)
        xh = xc * rstd
        u2 = xh * lw_ref[...] + lb_ref[...]
        s2 = _sigmoid(u2)
        dz = dza_ref[...]
        du3 = dz * (ag * sg)
        dpa_ref[:, 2 * D:3 * D] = (dz * (u2 * s2) * _dsilu(ag, sg)).astype(BF16)
        du2 = du3 * _dsilu(u2, s2)
        gv_ref[0:1, :] += _colsum(du2 * xh)
        gv_ref[1:2, :] += _colsum(du2)
        dxh = du2 * lw_ref[...]
        du1 = rstd * (dxh - _rowmean(dxh) - xh * _rowmean(dxh * xh))
        gv_ref[2:3, :] += _colsum(du1)
        dwin_ref[0:ts, :] = du1

        uwin_ref[0:HALO, :] = jnp.where(tile == 0, 0.0, u0p_ref[...])
        uwin_ref[HALO:HALO + ts, :] = u0_ref[...]

        _shifted_copies(dwin_ref, dsh_ref, ts + HALO - 8)
        _shifted_copies(uwin_ref, ush_ref, ts + HALO - 8)
        pieces = _conv_taps(dwin_ref, dsh_ref, w_ref, ts, chunk, lambda j: (KC - 1) - j)
        for n, acc in enumerate(pieces):
            du0_ref[n * chunk:(n + 1) * chunk, :] = acc
        for c0 in range(0, ts, chunk):
            dchunk = dwin_ref[c0:c0 + chunk, :]
            for j in range(KC):
                prod = dchunk * _tap_rows(uwin_ref, ush_ref, c0 + HALO - (KC - 1) + j, chunk)
                gwp_ref[8 * j:8 * j + 8, :] += jnp.sum(prod.reshape(chunk // 8, 8, D), axis=0)
        dwin_ref[ts:ts + HALO, :] = dwin_ref[0:HALO, :]

        du0 = du0_ref[...]
        al = al_ref[...]
        sl = _sigmoid(al)
        dpa_ref[:, 0:D] = (du0 * sl).astype(BF16)
        dpa_ref[:, D:2 * D] = (du0 * av_ref[...] * sl * (1.0 - sl)).astype(BF16)

        @pl.when(step == nt - 1)
        def _():
            for j in range(KC):
                gw_ref[j:j + 1, :] = _colsum(gwp_ref[8 * j:8 * j + 8, :])
            gw_ref[KC:HALO, :] = jnp.zeros((HALO - KC, D), F32)

    rev = lambda i: nt - 1 - i
    col = lambda c: pl.BlockSpec((ts, D), lambda i, c=c: (rev(i), c))
    row = pl.BlockSpec((ts, D), lambda i: (rev(i), 0))
    vec = pl.BlockSpec((1, D), lambda i: (0, 0))
    halo = pl.BlockSpec((HALO, D), lambda i: (jnp.maximum(rev(i) * per - 1, 0), 0))
    side_in, side_out, side_shapes, side_sems = _scatter_operands(parts, True)
    outs = pl.pallas_call(
        _scatter_alongside(body, 10, 3, len(parts), nt - 1, _device_scatter_copies), grid=(nt,),
        in_specs=[row, col(0), col(1), col(2), row, halo, row, pl.BlockSpec((HALO, D), lambda i: (0, 0)), vec, vec]
        + side_in,
        out_specs=[pl.BlockSpec((ts, A_COLS), lambda i: (rev(i), 0)),
                   pl.BlockSpec((HALO, D), lambda i: (0, 0)), pl.BlockSpec((8, D), lambda i: (0, 0))] + side_out,
        out_shape=[jax.ShapeDtypeStruct((s, A_COLS), BF16), jax.ShapeDtypeStruct((HALO, D), F32),
                   jax.ShapeDtypeStruct((8, D), F32)] + side_shapes,
        scratch_shapes=[pltpu.VMEM((ts + HALO, D), F32), pltpu.VMEM((ts + HALO, D), F32),
                        pltpu.VMEM((ts, D), F32), pltpu.VMEM((8 * HALO, D), F32),
                        pltpu.VMEM((7, ts + HALO, D), F32), pltpu.VMEM((7, ts + HALO, D), F32)] + side_sems,
        name="conv_bwd", compiler_params=_params(("arbitrary",), VMEM_BIG))(
            dza, proj_a, proj_a, proj_a, u0, u0, u1, conv_w, ln_w, ln_b, *parts)
    return outs[0], outs[1], outs[2], list(outs[3:])


def _mla_prep(proj_l, q_norm_w, kv_norm_w, w_uq2, w_ukv, cos_t, sin_t, ts):
    s = proj_l.shape[0]

    def body(pl_ref, qw_ref, kw_ref, wq_ref, wkv_ref, c_ref, s_ref, qn_ref, kvn_ref, q_ref, k_ref, v_ref):
        first = _first_half_mask(ts)
        cs = c_ref[...]
        sn = s_ref[...]

        def rms(v, w):
            return v * lax.rsqrt(_rowmean(v * v) + EPS) * w

        def rope(v):
            return v * cs + _swap_halves(v, first) * sn

        qn = rms(pl_ref[:, 0:RQ], qw_ref[...]).astype(BF16)
        kvn = rms(pl_ref[:, RQ:2 * RQ], kw_ref[...]).astype(BF16)
        qn_ref[...] = qn
        kvn_ref[...] = kvn
        q = _dot(qn, wq_ref[...])
        kv = _dot(kvn, wkv_ref[...])
        kr = rope(pl_ref[:, 2 * RQ:2 * RQ + LANE]).astype(BF16)
        for h in range(H):
            q_ref[h, :, 0:DN] = q[:, DN * h:DN * (h + 1)].astype(BF16)
            q_ref[h, :, DN:2 * DN] = rope(q[:, H * DN + LANE * h:H * DN + LANE * (h + 1)]).astype(BF16)
            k_ref[h, :, 0:DN] = kv[:, 2 * DN * h:2 * DN * h + DN].astype(BF16)
            k_ref[h, :, DN:2 * DN] = kr
            v_ref[h, :, 0:DN] = kv[:, 2 * DN * h + DN:2 * DN * (h + 1)].astype(BF16)
            v_ref[h, :, DN:2 * DN] = jnp.ones((ts, DN), BF16)

    const = lambda shape: pl.BlockSpec(shape, lambda i: (0,) * len(shape))
    rowb = lambda w: pl.BlockSpec((ts, w), lambda i: (i, 0))
    head = lambda w: pl.BlockSpec((H, ts, w), lambda i: (0, i, 0))
    return pl.pallas_call(
        body, grid=(s // ts,),
        in_specs=[rowb(L_COLS), const((1, RQ)), const((1, RQ)), const((RQ, 2 * H * DN)), const((RQ, 2 * H * DN)),
                  rowb(LANE), rowb(LANE)],
        out_specs=[rowb(RQ), rowb(RQ), head(2 * DN), head(2 * DN), head(2 * DN)],
        out_shape=[jax.ShapeDtypeStruct((s, RQ), BF16), jax.ShapeDtypeStruct((s, RQ), BF16),
                   jax.ShapeDtypeStruct((H, s, 2 * DN), BF16), jax.ShapeDtypeStruct((H, s, 2 * DN), BF16),
                   jax.ShapeDtypeStruct((H, s, 2 * DN), BF16)],
        name="mla_prep", compiler_params=_params(("parallel",), VMEM_BIG))(
            proj_l, q_norm_w, kv_norm_w, w_uq2, w_ukv, cos_t, sin_t)


def _mla_prep_bwd(dq, dk, dv, proj_l, qn, kvn, q_norm_w, kv_norm_w, w_uq2, w_ukv, cos_t, sin_t, ts):
    s = proj_l.shape[0]

    def body(dq_ref, dk_ref, dv_ref, pl_ref, qn_ref, kvn_ref, qw_ref, kw_ref, wq_ref, wkv_ref, c_ref, s_ref,
             dpl_ref, gwq_ref, gwkv_ref, gv_ref, dq2_ref, dkv2_ref):
        @pl.when(pl.program_id(0) == 0)
        def _():
            gwq_ref[...] = jnp.zeros_like(gwq_ref)
            gwkv_ref[...] = jnp.zeros_like(gwkv_ref)
            gv_ref[...] = jnp.zeros_like(gv_ref)

        first = _first_half_mask(ts)
        cs = c_ref[...] * ATT_SCALE
        sn = s_ref[...] * ATT_SCALE

        def rope_bwd(g):
            return g * cs + _swap_halves(g * sn, first)

        def rms_bwd(v, w, dy):
            r = lax.rsqrt(_rowmean(v * v) + EPS)
            vh = v * r
            dvh = dy * w
            return r * (dvh - vh * _rowmean(dvh * vh)), _colsum(dy * vh)

        dkr = None
        for h in range(H):
            dq2_ref[:, DN * h:DN * (h + 1)] = (dq_ref[h, :, 0:DN] * ATT_SCALE).astype(BF16)
            dq2_ref[:, H * DN + LANE * h:H * DN + LANE * (h + 1)] = rope_bwd(dq_ref[h, :, DN:2 * DN]).astype(BF16)
            dkv2_ref[:, 2 * DN * h:2 * DN * h + DN] = (dk_ref[h, :, 0:DN] * ATT_SCALE).astype(BF16)
            dkv2_ref[:, 2 * DN * h + DN:2 * DN * (h + 1)] = dv_ref[h].astype(BF16)
            part = dk_ref[h, :, DN:2 * DN]
            dkr = part if dkr is None else dkr + part

        dq2 = dq2_ref[...]
        dkv2 = dkv2_ref[...]
        gwq_ref[...] += _dot_tn(qn_ref[...], dq2)
        gwkv_ref[...] += _dot_tn(kvn_ref[...], dkv2)
        dcq, gq = rms_bwd(pl_ref[:, 0:RQ], qw_ref[...], _dot_nt(dq2, wq_ref[...]))
        dckv, gkv = rms_bwd(pl_ref[:, RQ:2 * RQ], kw_ref[...], _dot_nt(dkv2, wkv_ref[...]))
        gv_ref[0:1, :] += gq
        gv_ref[1:2, :] += gkv
        dpl_ref[:, 0:RQ] = dcq.astype(BF16)
        dpl_ref[:, RQ:2 * RQ] = dckv.astype(BF16)
        dpl_ref[:, 2 * RQ:2 * RQ + LANE] = rope_bwd(dkr).astype(BF16)

    const = lambda shape: pl.BlockSpec(shape, lambda i: (0,) * len(shape))
    rowb = lambda w: pl.BlockSpec((ts, w), lambda i: (i, 0))
    head = lambda w: pl.BlockSpec((H, ts, w), lambda i: (0, i, 0))
    return pl.pallas_call(
        body, grid=(s // ts,),
        in_specs=[head(2 * DN), head(2 * DN), head(DN), rowb(L_COLS), rowb(RQ), rowb(RQ), const((1, RQ)),
                  const((1, RQ)), const((RQ, 2 * H * DN)), const((RQ, 2 * H * DN)), rowb(LANE), rowb(LANE)],
        out_specs=[rowb(L_COLS), const((RQ, 2 * H * DN)), const((RQ, 2 * H * DN)), const((8, RQ))],
        out_shape=[jax.ShapeDtypeStruct((s, L_COLS), BF16), jax.ShapeDtypeStruct((RQ, 2 * H * DN), F32),
                   jax.ShapeDtypeStruct((RQ, 2 * H * DN), F32), jax.ShapeDtypeStruct((8, RQ), F32)],
        scratch_shapes=[pltpu.VMEM((ts, 2 * H * DN), BF16), pltpu.VMEM((ts, 2 * H * DN), BF16)],
        name="mla_prep_bwd", compiler_params=_params(("arbitrary",), VMEM_BIG))(
            dq, dk, dv, proj_l, qn, kvn, q_norm_w, kv_norm_w, w_uq2, w_ukv, cos_t, sin_t)


def _causal_pairs(n, by_key):
    if by_key:
        pairs = [(i, j) for j in range(n) for i in range(j, n)]
    else:
        pairs = [(i, j) for i in range(n) for j in range(i + 1)]
    return (jnp.asarray(np.array([p[0] for p in pairs], np.int32)),
            jnp.asarray(np.array([p[1] for p in pairs], np.int32)))


ATT_SCALE = float((DN + DR) ** -0.5)
LOG2E = 1.4426950408889634
LN2 = 0.6931471805599453
ATT_HEADS_FWD = 8
ATT_HEADS = 4
W_IN_ROWS = 336
ATT_ROWS = 16


def _diag_width(r0, t):
    return min(t, -(-(r0 + ATT_ROWS) // LANE) * LANE)


def _diag_mask_rows(r0, width):
    rows = r0 + lax.broadcasted_iota(jnp.int32, (ATT_ROWS, width), 0)
    cols = lax.broadcasted_iota(jnp.int32, (ATT_ROWS, width), 1)
    return cols <= rows


def _diag_mask(t):
    return lax.broadcasted_iota(jnp.int32, (t, t), 1) <= lax.broadcasted_iota(jnp.int32, (t, t), 0)


def _attn_fwd(q, k, v, t):
    s = q.shape[1]
    n = s // t
    scale2 = float((DN + DR) ** -0.5) * LOG2E
    qi, ki = _causal_pairs(n, by_key=False)

    def body(qi_ref, ki_ref, q_ref, k_ref, v_ref, o_ref, lse_ref, *scratch):
        per_head = [scratch[5 * h:5 * h + 5] for h in range(ATT_HEADS_FWD)]
        p = pl.program_id(1)
        i = qi_ref[p]
        j = ki_ref[p]

        @pl.when(j == 0)
        def _():
            for m_sc, acc_sc, _, _, _ in per_head:
                m_sc[...] = jnp.full_like(m_sc, -jnp.inf)
                acc_sc[...] = jnp.zeros_like(acc_sc)

        def scores(h, diag):
            sc = _dot_nt(q_ref[h], k_ref[h])
            if diag:
                sc = jnp.where(_diag_mask(t), sc, -jnp.inf)
            per_head[h][2][...] = sc

        def rowmax(h, rows):
            per_head[h][4][rows, :] = jnp.max(per_head[h][2][rows, :], axis=-1, keepdims=True)

        def stats(h):
            m_sc, acc_sc, _, _, mx_sc = per_head[h]
            m_prev = m_sc[...]
            m_new = jnp.maximum(m_prev, mx_sc[...] * scale2)
            m_sc[...] = m_new
            acc_sc[...] = jnp.exp2(m_prev - m_new) * acc_sc[...]

        def probs(h, rows):
            m_sc, _, s_sc, p_sc, _ = per_head[h]
            p_sc[rows, :] = jnp.exp2(s_sc[rows, :] * scale2 - m_sc[rows, :]).astype(BF16)

        def values(h):
            _, acc_sc, _, p_sc, _ = per_head[h]
            acc_sc[...] += _dot(p_sc[...], v_ref[h])

        def step(diag):
            blocks = [slice(r0, r0 + ATT_ROWS) for r0 in range(0, t, ATT_ROWS)]
            for h in range(ATT_HEADS_FWD):
                scores(h, diag)
            for rows in blocks:
                rowmax(0, rows)
            stats(0)
            for h in range(ATT_HEADS_FWD):
                for rows in blocks:
                    probs(h, rows)
                    if h + 1 < ATT_HEADS_FWD:
                        rowmax(h + 1, rows)
                if h + 1 < ATT_HEADS_FWD:
                    stats(h + 1)
                values(h)

        @pl.when(j < i)
        def _():
            step(False)

        @pl.when(j == i)
        def _():
            step(True)
            for h, (m_sc, acc_sc, _, _, _) in enumerate(per_head):
                l = acc_sc[:, DN:2 * DN]
                o_ref[:, DN * h:DN * (h + 1)] = acc_sc[:, 0:DN] / l
                lse_ref[h] = (m_sc[...] + jnp.log2(l[:, 0:1])) * LN2

    hb = ATT_HEADS_FWD
    grid_spec = pltpu.PrefetchScalarGridSpec(
        num_scalar_prefetch=2, grid=(H // hb, int(qi.shape[0])),
        in_specs=[pl.BlockSpec((hb, t, 2 * DN), lambda h, p, qi, ki: (h, qi[p], 0)),
                  pl.BlockSpec((hb, t, 2 * DN), lambda h, p, qi, ki: (h, ki[p], 0)),
                  pl.BlockSpec((hb, t, 2 * DN), lambda h, p, qi, ki: (h, ki[p], 0))],
        out_specs=[pl.BlockSpec((t, hb * DN), lambda h, p, qi, ki: (qi[p], h)),
                   pl.BlockSpec((hb, t, 1), lambda h, p, qi, ki: (h, qi[p], 0))],
        scratch_shapes=[pltpu.VMEM((t, 1), F32), pltpu.VMEM((t, 2 * DN), F32), pltpu.VMEM((t, t), F32),
                        pltpu.VMEM((t, t), BF16), pltpu.VMEM((t, 1), F32)] * hb)
    return pl.pallas_call(
        body, grid_spec=grid_spec,
        out_shape=[jax.ShapeDtypeStruct((s, H * DN), F32), jax.ShapeDtypeStruct((H, s, 1), F32)],
        name="attn_fwd", compiler_params=_params(("parallel", "arbitrary"), VMEM_BIG))(qi, ki, q, k, v)


def _attn_bwd(q, k, v, do, lse, delta, t):
    s = q.shape[1]
    n = s // t
    scale = ATT_SCALE
    qi, ki = _causal_pairs(n, by_key=True)

    def body(qi_ref, ki_ref, q_ref, k_ref, v_ref, do_ref, lse_ref, dl_ref, dq_ref, dk_ref, dv_ref,
             dk_sc, dv_sc, s_sc, dp_sc, p_sc, ds_sc):
        p = pl.program_id(1)
        i = qi_ref[p]
        j = ki_ref[p]

        @pl.when(p == 0)
        def _():
            dq_ref[...] = jnp.zeros_like(dq_ref)

        @pl.when(i == j)
        def _():
            dk_sc[...] = jnp.zeros_like(dk_sc)
            dv_sc[...] = jnp.zeros_like(dv_sc)

        def step(diag):
            for h in range(ATT_HEADS):
                s_sc[h] = _dot_nt(q_ref[h], k_ref[h])
                dp_sc[h] = _dot_nt(do_ref[:, DN * h:DN * (h + 1)], v_ref[h, :, 0:DN])
            for h in range(ATT_HEADS):
                for r0 in range(0, t, ATT_ROWS):
                    rows = slice(r0, r0 + ATT_ROWS)
                    width = _diag_width(r0, t) if diag else t
                    sc = s_sc[h, rows, 0:width] * (scale * LOG2E)
                    if diag:
                        sc = jnp.where(_diag_mask_rows(r0, width), sc, -jnp.inf)
                    pr = jnp.exp2(sc - lse_ref[h, rows, :] * LOG2E)
                    ds = pr * (dp_sc[h, rows, 0:width] - dl_ref[h, rows, :])
                    p_sc[h, rows, 0:width] = pr.astype(BF16)
                    ds_sc[h, rows, 0:width] = ds.astype(BF16)
                    if width < t:
                        p_sc[h, rows, width:t] = jnp.zeros((ATT_ROWS, t - width), BF16)
                        ds_sc[h, rows, width:t] = jnp.zeros((ATT_ROWS, t - width), BF16)
            q_rows = pl.ds(pl.multiple_of(i * t, t), t)
            for h in range(ATT_HEADS):
                dv_sc[h] += _dot_tn(p_sc[h], do_ref[:, DN * h:DN * (h + 1)])
                dk_sc[h] += _dot_tn(ds_sc[h], q_ref[h])
                dq_ref[h, q_rows, :] += _dot(ds_sc[h], k_ref[h])

        @pl.when(i > j)
        def _():
            step(False)

        @pl.when(i == j)
        def _():
            step(True)

        @pl.when(i == n - 1)
        def _():
            dk_ref[...] = dk_sc[...]
            dv_ref[...] = dv_sc[...]

    hb = ATT_HEADS
    grid_spec = pltpu.PrefetchScalarGridSpec(
        num_scalar_prefetch=2, grid=(H // hb, int(qi.shape[0])),
        in_specs=[pl.BlockSpec((hb, t, 2 * DN), lambda h, p, qi, ki: (h, qi[p], 0)),
                  pl.BlockSpec((hb, t, 2 * DN), lambda h, p, qi, ki: (h, ki[p], 0)),
                  pl.BlockSpec((hb, t, 2 * DN), lambda h, p, qi, ki: (h, ki[p], 0)),
                  pl.BlockSpec((t, hb * DN), lambda h, p, qi, ki: (qi[p], h)),
                  pl.BlockSpec((hb, t, 1), lambda h, p, qi, ki: (h, qi[p], 0)),
                  pl.BlockSpec((hb, t, 1), lambda h, p, qi, ki: (h, qi[p], 0))],
        out_specs=[pl.BlockSpec((hb, s, 2 * DN), lambda h, p, qi, ki: (h, 0, 0), pipeline_mode=pl.Buffered(1)),
                   pl.BlockSpec((hb, t, 2 * DN), lambda h, p, qi, ki: (h, ki[p], 0)),
                   pl.BlockSpec((hb, t, DN), lambda h, p, qi, ki: (h, ki[p], 0))],
        scratch_shapes=[pltpu.VMEM((hb, t, 2 * DN), F32), pltpu.VMEM((hb, t, DN), F32),
                        pltpu.VMEM((hb, t, t), F32), pltpu.VMEM((hb, t, t), F32),
                        pltpu.VMEM((hb, t, t), BF16), pltpu.VMEM((hb, t, t), BF16)])
    return pl.pallas_call(
        body, grid_spec=grid_spec,
        out_shape=[jax.ShapeDtypeStruct((H, s, 2 * DN), F32), jax.ShapeDtypeStruct((H, s, 2 * DN), F32),
                   jax.ShapeDtypeStruct((H, s, DN), F32)],
        name="attn_bwd", compiler_params=_params(("parallel", "arbitrary"), VMEM_BIG))(
            qi, ki, q, k, v, do, lse, delta)


def _middle(za, o, proj_g, x, tgt, gate, fnw, wco, wao, wo, ts):
    s = x.shape[0]
    inv_d = 1.0 / D

    def body(za_ref, o_ref, bg_ref, ga_ref, gb_ref, x_ref, t_ref, gate_ref, fnw_ref, wco_ref, wao_ref, wo_ref,
             dx2_ref, dza_ref, do_ref, dl_ref, dpg_ref, lhs_ref, rhs_ref, vec_ref):
        @pl.when(pl.program_id(0) == 0)
        def _():
            vec_ref[...] = jnp.zeros_like(vec_ref)

        ov = o_ref[...]
        bg = bg_ref[...]
        sb = _sigmoid(bg)
        silu_b = bg * sb
        zb = (ov * silu_b).astype(BF16)
        lhs_ref[0] = za_ref[...]
        lhs_ref[1] = zb
        ya = _dot(za_ref[...], wco_ref[...])
        yb = _dot(zb, wao_ref[...])
        sa = _sigmoid(ga_ref[...])
        sg = _sigmoid(gb_ref[...])
        mg = (sa * ya + sg * yb).astype(BF16)
        lhs_ref[2] = mg
        mo = _dot(mg, wo_ref[...])
        gate_v = gate_ref[...]
        x2 = x_ref[...] + gate_v * mo
        r = lax.rsqrt(_rowmean(x2 * x2) + EPS)
        xh = x2 * r
        fw = fnw_ref[...]
        e = xh * fw - t_ref[...]
        vec_ref[2:3, :] += _colsum(e * e)
        dy = e * inv_d
        vec_ref[0:1, :] += _colsum(dy * xh)
        dxh = dy * fw
        dx2 = r * (dxh - xh * _rowmean(dxh * xh))
        dx2_ref[...] = dx2
        vec_ref[1:2, :] += _colsum(dx2 * mo)
        dmo = (gate_v * dx2).astype(BF16)
        rhs_ref[2] = dmo
        dmg = _dot_nt(dmo, wo_ref[...])
        dya = (sa * dmg).astype(BF16)
        dyb = (sg * dmg).astype(BF16)
        rhs_ref[0] = dya
        rhs_ref[1] = dyb
        dpg_ref[:, D:2 * D] = (dmg * ya * (sa * (1.0 - sa))).astype(BF16)
        dpg_ref[:, 2 * D:3 * D] = (dmg * yb * (sg * (1.0 - sg))).astype(BF16)
        dza_ref[...] = _dot_nt(dya, wco_ref[...])
        dzb = _dot_nt(dyb, wao_ref[...])
        dov = dzb * silu_b
        do_ref[...] = dov.astype(BF16)
        dpg_ref[:, 0:D] = (dzb * ov * _dsilu(bg, sb)).astype(BF16)
        dprod = dov * ov
        for h in range(H):
            dl_ref[h] = jnp.sum(dprod[:, DN * h:DN * (h + 1)], axis=-1, keepdims=True)

    col = lambda c: pl.BlockSpec((ts, D), lambda i, c=c: (i, c))
    row = pl.BlockSpec((ts, D), lambda i: (i, 0))
    vec = pl.BlockSpec((1, D), lambda i: (0, 0))
    wsp = pl.BlockSpec((D, D), lambda i: (0, 0))
    stack = pl.BlockSpec((3, ts, D), lambda i: (0, i, 0))
    bf = jax.ShapeDtypeStruct((s, D), BF16)
    ff = jax.ShapeDtypeStruct((s, D), F32)
    return pl.pallas_call(
        body, grid=(s // ts,),
        in_specs=[row, row, col(0), col(1), col(2), row, row, vec, vec, wsp, wsp, wsp],
        out_specs=[row, row, row, pl.BlockSpec((H, ts, 1), lambda i: (0, i, 0)),
                   pl.BlockSpec((ts, G_COLS), lambda i: (i, 0)), stack, stack,
                   pl.BlockSpec((8, D), lambda i: (0, 0))],
        out_shape=[ff, ff, bf, jax.ShapeDtypeStruct((H, s, 1), F32), jax.ShapeDtypeStruct((s, G_COLS), BF16),
                   jax.ShapeDtypeStruct((3, s, D), BF16), jax.ShapeDtypeStruct((3, s, D), BF16),
                   jax.ShapeDtypeStruct((8, D), F32)],
        name="middle", compiler_params=_params(("arbitrary",), VMEM_BIG))(
            za, o, proj_g, proj_g, proj_g, x, tgt, gate, fnw, wco, wao, wo)


def _input_bwd(dpa, dpl, dpg, wa, wl, wg, x, dx2, norm_w, scale, ts, parts):
    s = x.shape[0]

    def body(dpa_ref, dpl_ref, dpg_ref, wa_ref, wl_ref, wg_ref, x_ref, dx2_ref, nw_ref, sc_ref, gx_ref, gv_ref):
        @pl.when(pl.program_id(0) == 0)
        def _():
            gv_ref[...] = jnp.zeros_like(gv_ref)

        dh = (_dot_nt(dpa_ref[...], wa_ref[...]) + _dot_nt(dpl_ref[...], wl_ref[...])
              + _dot_nt(dpg_ref[...], wg_ref[...]))
        xv = x_ref[...]
        r = lax.rsqrt(_rowmean(xv * xv) + EPS)
        xh = xv * r
        nw = nw_ref[...]
        gv_ref[0:1, :] += _colsum(dh)
        gv_ref[1:2, :] += _colsum(dh * (xh * nw))
        dy = dh * (1.0 + sc_ref[...])
        gv_ref[2:3, :] += _colsum(dy * xh)
        dxh = dy * nw
        gx_ref[...] = dx2_ref[...] + r * (dxh - xh * _rowmean(dxh * xh))

    const = lambda shape: pl.BlockSpec(shape, lambda i: (0, 0))
    rowb = lambda w: pl.BlockSpec((ts, w), lambda i: (i, 0))
    side_in, side_out, side_shapes, side_sems = _scatter_operands(parts, False)
    outs = pl.pallas_call(
        _scatter_alongside(body, 10, 2, len(parts), s // ts - 1, _chip_scatter_copies), grid=(s // ts,),
        in_specs=[rowb(A_COLS), rowb(L_COLS), rowb(G_COLS), const((D, A_COLS)), const((D, L_COLS)),
                  const((D, G_COLS)), rowb(D), rowb(D), const((1, D)), const((1, D))] + side_in,
        out_specs=[rowb(D), const((8, D))] + side_out,
        out_shape=[jax.ShapeDtypeStruct((s, D), F32), jax.ShapeDtypeStruct((8, D), F32)] + side_shapes,
        scratch_shapes=side_sems,
        name="input_bwd", compiler_params=_params(("arbitrary",), VMEM_BIG))(
            dpa, dpl, dpg, wa, wl, wg, x, dx2, norm_w, scale, *parts)
    return outs[0], outs[1], list(outs[2:])


def _adamw_math(w, g, m, v):
    nm = ADAM_B1 * m + (1.0 - ADAM_B1) * g
    nv = ADAM_B2 * v + (1.0 - ADAM_B2) * (g * g)
    m_hat = nm / (1.0 - ADAM_B1 ** ADAM_STEP)
    v_hat = nv / (1.0 - ADAM_B2 ** ADAM_STEP)
    return -ADAM_LR * (m_hat / (jnp.sqrt(v_hat) + ADAM_EPS) + ADAM_WD * w), nm, nv


def _adamw(w, g, m, v, tr, name):
    lead, (rows, cols) = w.shape[:-2], w.shape[-2:]

    def body(w_ref, g_ref, m_ref, v_ref, d_ref, nm_ref, nv_ref):
        d_ref[...], nm_ref[...], nv_ref[...] = _adamw_math(w_ref[...], g_ref[...], m_ref[...], v_ref[...])

    blk = pl.BlockSpec((1,) * len(lead) + (tr, cols), lambda i: (0,) * len(lead) + (i, 0))
    shp = jax.ShapeDtypeStruct(w.shape, F32)
    return pl.pallas_call(
        body, grid=(rows // tr,), in_specs=[blk] * 4, out_specs=[blk] * 3, out_shape=[shp] * 3, name=name,
        compiler_params=_params(("parallel",), VMEM_BIG))(w, g.reshape(w.shape), m, v)


ROW_SHIFT, ROW_SCALE, ROW_NORM_W = 0, 1, 2
ROW_FINAL_NORM_W, ROW_GATE, ROW_LOSS = 8, 9, 10
ROW_LN_W, ROW_LN_B, ROW_CONV_B = 16, 17, 18
ROW_Q_NORM_W, ROW_KV_NORM_W = 24, 25
ROW_CONV_W = 32
SUM_ROWS = 64
VECTOR_ROWS = ((ROW_SHIFT, ROW_SCALE, ROW_GATE), (ROW_NORM_W,), (ROW_CONV_B,), (ROW_LN_W,), (ROW_LN_B,),
               (ROW_Q_NORM_W,), (ROW_KV_NORM_W,), (ROW_FINAL_NORM_W,))


def _small_finalize(gathered, vectors, conv, chip):
    n = len(vectors)
    cw = conv[0].shape[2]

    def body(chip_ref, g_ref, *refs):
        ins, outs = refs[:3 * n + 3], refs[3 * n + 3:]
        tot = g_ref[0]
        for k in range(1, N_DEV):
            tot = tot + g_ref[k]
        for p, rows in enumerate(VECTOR_ROWS):
            w_ref, m_ref, v_ref = ins[3 * p:3 * p + 3]
            g_out, d_out, nm_out, nv_out = outs[4 * p:4 * p + 4]
            width = w_ref.shape[1] // len(rows)
            for q, r in enumerate(rows):
                lanes = slice(q * width, (q + 1) * width)
                g = tot[r:r + 1, 0:width]
                g_out[:, lanes] = g
                d_out[:, lanes], nm_out[:, lanes], nv_out[:, lanes] = _adamw_math(
                    w_ref[:, lanes], g, m_ref[:, lanes], v_ref[:, lanes])
        cols = pl.ds(pl.multiple_of(chip_ref[0] * cw, LANE), cw)
        gc = g_ref[0, pl.ds(ROW_CONV_W, KC), cols]
        for k in range(1, N_DEV):
            gc = gc + g_ref[k, pl.ds(ROW_CONV_W, KC), cols]
        cw_ref, cm_ref, cv_ref = ins[3 * n:3 * n + 3]
        g_out, d_out, nm_out, nv_out, dmod_ref, loss_ref = outs[4 * n:]
        g_out[0] = gc
        d_out[0], nm_out[0], nv_out[0] = _adamw_math(cw_ref[0], gc, cm_ref[0], cv_ref[0])
        for k in range(N_DEV):
            for q, r in enumerate((ROW_SHIFT, ROW_SCALE, ROW_GATE)):
                dmod_ref[k:k + 1, q * D:(q + 1) * D] = g_ref[k, r:r + 1, :]
        loss_ref[...] = (0.5 / D) * jnp.sum(tot[ROW_LOSS:ROW_LOSS + 1, :], axis=-1, keepdims=True)

    flat_in = [a for triple in vectors for a in triple] + list(conv)
    shapes = [jax.ShapeDtypeStruct(w.shape, F32) for w, _, _ in vectors for _ in range(4)]
    shapes += [jax.ShapeDtypeStruct(conv[0].shape, F32)] * 4
    shapes += [jax.ShapeDtypeStruct((N_DEV, 3 * D), F32), jax.ShapeDtypeStruct((1, 1), F32)]
    whole = pl.BlockSpec(memory_space=pltpu.VMEM)
    return pl.pallas_call(
        body, out_shape=shapes,
        in_specs=[pl.BlockSpec(memory_space=pltpu.SMEM)] + [whole] * (1 + len(flat_in)),
        out_specs=[whole] * len(shapes), name="small_finalize")(chip, gathered, *flat_in)


def _ada_bwd(c_all_t, dmod_shard):
    def body(c_ref, d_ref, o_ref):
        cv = c_ref[...]
        o_ref[...] = jnp.dot(cv * _sigmoid(cv), d_ref[...], preferred_element_type=F32,
                             precision=lax.Precision.HIGHEST)

    return pl.pallas_call(
        body, out_shape=jax.ShapeDtypeStruct((D, dmod_shard.shape[1]), F32), name="ada_bwd")(c_all_t, dmod_shard)


def _sum_chip_slabs(arrived, part, place, tr, name, axis):
    n, rows, cols = arrived.shape
    per = rows // tr
    own_map = ((lambda i, pc: (pc[0], i, 0)) if part.shape[1] == rows
               else (lambda i, pc: (pc[0], pc[1] * per + i, 0)))

    def body(place_ref, a_ref, p_ref, o_ref):
        acc = p_ref[0].astype(F32)
        for k in range(n):
            acc = acc + a_ref[k].astype(F32)
        o_ref[...] = acc

    if axis == 1:
        whole, out_map = (2 * rows, cols), lambda i, pc: (pc[1] * per + i, 0)
    else:
        whole, out_map = (rows, 2 * cols), lambda i, pc: (i, pc[1])
    grid_spec = pltpu.PrefetchScalarGridSpec(
        num_scalar_prefetch=1, grid=(per,),
        in_specs=[pl.BlockSpec((n, tr, cols), lambda i, pc: (0, i, 0)),
                  pl.BlockSpec((1, tr, cols), own_map)],
        out_specs=pl.BlockSpec((tr, cols), out_map))
    return pl.pallas_call(
        body, grid_spec=grid_spec, out_shape=jax.ShapeDtypeStruct(whole, F32), name=name,
        compiler_params=_params(("parallel",)))(place, arrived, part)


def _sum_device_partials(arrived, parts, place):
    n = len(arrived)

    def body(place_ref, *refs):
        a_refs, p_refs, o_refs = refs[:n], refs[n:2 * n], refs[2 * n:]
        for a in range(n):
            acc = p_refs[a][0].astype(F32)
            for k in range(arrived[a].shape[0]):
                acc = acc + a_refs[a][k].astype(F32)
            o_refs[a][...] = acc

    grid_spec = pltpu.PrefetchScalarGridSpec(
        num_scalar_prefetch=1, grid=(1,),
        in_specs=[pl.BlockSpec(a.shape, lambda i, pc: (0, 0, 0)) for a in arrived]
        + [pl.BlockSpec((1,) + a.shape[1:], lambda i, pc: (pc[0], pc[1], 0)) for a in arrived],
        out_specs=[pl.BlockSpec(a.shape[1:], lambda i, pc: (pc[1], 0)) for a in arrived])
    return pl.pallas_call(
        body, grid_spec=grid_spec,
        out_shape=[jax.ShapeDtypeStruct((2 * a.shape[1], a.shape[2]), F32) for a in arrived],
        name="sum_device_partials", compiler_params=_params(("arbitrary",), VMEM_BIG))(place, *arrived, *parts)


def _adamw_many(ws, gs, ms, vs, tr):
    n = len(ws)
    rows = ws[0].shape[1]

    def body(*refs):
        ins, outs = refs[:4 * n], refs[4 * n:]
        for a in range(n):
            w_ref, g_ref, m_ref, v_ref = ins[4 * a:4 * a + 4]
            outs[3 * a][...], outs[3 * a + 1][...], outs[3 * a + 2][...] = _adamw_math(
                w_ref[...], g_ref[...], m_ref[...], v_ref[...])

    blk = lambda w: pl.BlockSpec((1, tr, w.shape[2]), lambda i: (0, i, 0))
    gs = [g.reshape(w.shape) for g, w in zip(gs, ws)]
    flat = [a for quad in zip(ws, gs, ms, vs) for a in quad]
    outs = pl.pallas_call(
        body, grid=(rows // tr,), in_specs=[blk(w) for w in ws for _ in range(4)],
        out_specs=[blk(w) for w in ws for _ in range(3)],
        out_shape=[jax.ShapeDtypeStruct(w.shape, F32) for w in ws for _ in range(3)], name="adamw_small_matrices",
        compiler_params=_params(("parallel",), VMEM_BIG))(*flat)
    return [(gs[a], outs[3 * a], outs[3 * a + 1], outs[3 * a + 2]) for a in range(n)]


def _add_own_half(full, other, core, tr, name, axis):
    n, rows, cols = other.shape
    per = rows // tr

    def body(c_ref, f_ref, o_ref, out_ref):
        out_ref[...] = (f_ref[...].astype(F32) + o_ref[...].astype(F32)).astype(BF16)

    full_map = (lambda k, i, c: (k, c[0] * per + i, 0)) if axis == 1 else (lambda k, i, c: (k, i, c[0]))
    grid_spec = pltpu.PrefetchScalarGridSpec(
        num_scalar_prefetch=1, grid=(n, per),
        in_specs=[pl.BlockSpec((1, tr, cols), full_map),
                  pl.BlockSpec((1, tr, cols), lambda k, i, c: (k, i, 0))],
        out_specs=pl.BlockSpec((1, tr, cols), lambda k, i, c: (k, i, 0)))
    return pl.pallas_call(
        body, grid_spec=grid_spec, out_shape=jax.ShapeDtypeStruct((n, rows, cols), BF16), name=name,
        compiler_params=_params(("parallel", "parallel"), VMEM_BIG))(core, full, other)


def _allgather8_run(x_ref, out_ref, send_sems, recv_sems, local_sem):
    m = x_ref.shape[0]
    x, y, c = _coords()
    me, sibling = (x, y, c), (x, y, 1 - c)
    chips = [(1 - x, y), (x, 1 - y), (1 - x, 1 - y)]

    def rows(px, py, pc):
        return out_ref.at[pl.ds(pl.multiple_of((4 * px + 2 * py + pc) * m, 8), m), :]

    def copy(k, blk, to, source=None):
        return pltpu.make_async_remote_copy(
            src_ref=rows(*blk) if source is None else source, dst_ref=rows(*blk),
            send_sem=send_sems.at[k], recv_sem=recv_sems.at[k], device_id=to, device_id_type=MESH)

    mine = pltpu.make_async_copy(x_ref, rows(*me), local_sem)
    mine.start()
    first = [copy(0, me, sibling, source=x_ref)]
    first += [copy(1 + j, me, (*chip, c), source=x_ref) for j, chip in enumerate(chips)]
    for cp in first:
        cp.start()
    passed = [copy(4 + j, (*chip, c), sibling) for j, chip in enumerate(chips)]
    for j, chip in enumerate(chips):
        copy(1 + j, (*chip, c), me).wait_recv()
        passed[j].start()
    copy(0, sibling, me).wait_recv()
    for j, chip in enumerate(chips):
        copy(4 + j, (*chip, 1 - c), me).wait_recv()
    for cp in first + passed:
        cp.wait_send()
    mine.wait()


ALLGATHER8_SEMS = [pltpu.SemaphoreType.DMA((7,)), pltpu.SemaphoreType.DMA((7,)), pltpu.SemaphoreType.DMA]


def _gather_plan(x_refs, out_refs, send_sems, recv_sems, local_sems):
    n = len(x_refs)
    halves = [r.shape[0] // 2 for r in x_refs]
    x, y, c = _coords()
    me, sibling = (x, y, c), (x, y, 1 - c)
    chips = [(1 - x, y), (x, 1 - y), (1 - x, 1 - y)]

    def src(a):
        return x_refs[a].at[pl.ds(pl.multiple_of(c * halves[a], 16), halves[a]), :]

    def blk(a, px, py, pc):
        return out_refs[a].at[4 * px + 2 * py + pc]

    def copy(a, k, who, to, source=None):
        return pltpu.make_async_remote_copy(
            src_ref=blk(a, *who) if source is None else source, dst_ref=blk(a, *who),
            send_sem=send_sems.at[7 * a + k], recv_sem=recv_sems.at[7 * a + k], device_id=to, device_id_type=MESH)

    def mine(a):
        return pltpu.make_async_copy(src(a), blk(a, *me), local_sems.at[a])

    def first(a):
        return ([copy(a, 0, me, sibling, source=src(a))]
                + [copy(a, 1 + j, me, (*chip, c), source=src(a)) for j, chip in enumerate(chips)])

    def begin():
        for a in range(n):
            mine(a).start()
        for a in range(n):
            for cp in first(a):
                cp.start()

    def finish():
        onward = []
        for j, chip in enumerate(chips):
            for a in range(n):
                copy(a, 1 + j, (*chip, c), me).wait_recv()
                onward.append(copy(a, 4 + j, (*chip, c), sibling))
                onward[-1].start()
        for a in range(n):
            copy(a, 0, sibling, me).wait_recv()
        for j, chip in enumerate(chips):
            for a in range(n):
                copy(a, 4 + j, (*chip, 1 - c), me).wait_recv()
        for a in range(n):
            for cp in first(a):
                cp.wait_send()
        for cp in onward:
            cp.wait_send()
        for a in range(n):
            mine(a).wait()

    return begin, finish


def _gather_operands(shards):
    n = len(shards)
    shapes = [jax.ShapeDtypeStruct((N_DEV, a.shape[0] // 2, a.shape[1]), a.dtype) for a in shards]
    sems = [pltpu.SemaphoreType.DMA((7 * n,)), pltpu.SemaphoreType.DMA((7 * n,)), pltpu.SemaphoreType.DMA((n,))]
    return shapes, sems


def _as_chip_slabs(gathered, shards):
    return [o.reshape(N_CHIP, a.shape[0], a.shape[1]) for o, a in zip(gathered, shards)]


def _gather_alongside(body, n_in, n_out, n_shards, last_step, first=None):
    def wrapped(*refs):
        ins, shards = refs[:n_in], refs[n_in:n_in + n_shards]
        rest = refs[n_in + n_shards:]
        outs, gathered = rest[:n_out], rest[n_out:n_out + n_shards]
        scratch, sems = rest[n_out + n_shards:-3], rest[-3:]

        @pl.when(pl.program_id(0) == 0)
        def _():
            if first is not None:
                first(*ins, *outs, *scratch)
            _gather_plan(shards, gathered, *sems)[0]()

        body(*ins, *outs, *scratch)

        @pl.when(pl.program_id(0) == last_step)
        def _():
            _gather_plan(shards, gathered, *sems)[1]()

    return wrapped


def _half(ref, axis, which, ndim):
    size = ref.shape[axis] // 2
    idx = [slice(None)] * ndim
    idx[axis] = pl.ds(pl.multiple_of(which * size, 8 if axis == ndim - 2 else LANE), size)
    return ref.at[tuple(idx)]


def _swap_halves_with_sibling(fulls, name, axes):
    n = len(fulls)

    def body(*refs):
        f_refs, got_refs = refs[:n], refs[n:2 * n]
        send_sems, recv_sems = refs[2 * n:]
        x, y, c = _coords()
        copies = []
        for a in range(n):
            copies.append(pltpu.make_async_remote_copy(
                src_ref=_half(f_refs[a], axes[a], 1 - c, 3), dst_ref=got_refs[a], send_sem=send_sems.at[a],
                recv_sem=recv_sems.at[a], device_id=(x, y, 1 - c), device_id_type=MESH))
        for cp in copies:
            cp.start()
        for cp in copies:
            cp.wait()

    def halved(a, axis):
        shape = list(a.shape)
        shape[axis] //= 2
        return jax.ShapeDtypeStruct(tuple(shape), a.dtype)

    return pl.pallas_call(
        body, out_shape=[halved(a, ax) for a, ax in zip(fulls, axes)],
        in_specs=[HBM_REF] * n, out_specs=[HBM_REF] * n,
        scratch_shapes=[pltpu.SemaphoreType.DMA((n,)), pltpu.SemaphoreType.DMA((n,))],
        name=name)(*fulls)


def _join_halves_with_sibling(wholes, axes, block):
    n = len(wholes)
    twice = jnp.concatenate([block, block], axis=0)
    gathered_shapes, gather_sems = _gather_operands([twice])

    def body(*refs):
        out_refs = refs[n + 1:2 * n + 1]
        send_sems, recv_sems = refs[2 * n + 2:2 * n + 4]
        begin, finish = _gather_plan([refs[n]], [refs[2 * n + 1]], *refs[2 * n + 4:])
        x, y, c = _coords()

        def push(a, core):
            half = _half(out_refs[a], axes[a] - 1, core, 2)
            return pltpu.make_async_remote_copy(
                src_ref=half, dst_ref=half, send_sem=send_sems.at[a], recv_sem=recv_sems.at[a],
                device_id=(x, y, 1 - c), device_id_type=MESH)

        begin()
        for a in range(n):
            push(a, c).start()
        finish()
        for a in range(n):
            push(a, 1 - c).wait_recv()
        for a in range(n):
            push(a, c).wait_send()

    outs = pl.pallas_call(
        body, out_shape=[jax.ShapeDtypeStruct(a.shape, a.dtype) for a in wholes] + gathered_shapes,
        in_specs=[HBM_REF] * (n + 1), out_specs=[HBM_REF] * (n + 1), input_output_aliases={a: a for a in range(n)},
        scratch_shapes=[pltpu.SemaphoreType.DMA((n,)), pltpu.SemaphoreType.DMA((n,))] + gather_sems,
        name="rs_pair_join")(*wholes, twice)
    return outs[:n], outs[n]


def _cols_to_slabs(g):
    rows, cols = g.shape
    return g.reshape(rows, N_CHIP, cols // N_CHIP).transpose(1, 0, 2)


def _slabs_to_cols(w):
    n, rows, cols = w.shape
    return w.transpose(1, 0, 2).reshape(rows, n * cols)


def _col_window(slabs, start, stop):
    n = slabs.shape[2]
    pieces = []
    for k in range(N_CHIP):
        lo, hi = max(start, k * n), min(stop, (k + 1) * n)
        if lo < hi:
            pieces.append(slabs[k][:, lo - k * n:hi - k * n])
    return pieces[0] if len(pieces) == 1 else jnp.concatenate(pieces, axis=1)


def _slabs_from_groups(groups, n):
    slabs = []
    for k in range(N_CHIP):
        pieces, off = [], 0
        for g in groups:
            lo, hi = max(k * n, off), min((k + 1) * n, off + g.shape[0])
            if lo < hi:
                pieces.append(g[lo - off:hi - off])
            off += g.shape[0]
        slabs.append(pieces[0] if len(pieces) == 1 else jnp.concatenate(pieces, axis=0))
    return jnp.stack(slabs)


def _uq_to_padded(w_uq):
    per = w_uq.reshape(RQ, H, DN + DR)
    nope = per[:, :, :DN].reshape(RQ, H * DN)
    rope = jnp.pad(per[:, :, DN:], ((0, 0), (0, 0), (0, LANE - DR))).reshape(RQ, H * LANE)
    return jnp.concatenate([nope, rope], axis=1)


def _uq_from_padded(g):
    nope = g[:, :H * DN].reshape(RQ, H, DN)
    rope = g[:, H * DN:].reshape(RQ, H, LANE)[:, :, :DR]
    return jnp.concatenate([nope, rope], axis=2).reshape(RQ, H * (DN + DR))


def _rope_tables(positions):
    inv_freq = ROPE_THETA ** (-jnp.arange(0, DR, 2, dtype=F32) / DR)
    ang = positions.astype(F32)[:, None] * inv_freq
    cos, sin = jnp.cos(ang), jnp.sin(ang)
    return jnp.tile(cos, (1, 4)), jnp.tile(jnp.concatenate([-sin, sin], axis=1), (1, 2))


def _pair_sums(fulls, core, tag, axes, tr):
    from_sibling = _swap_halves_with_sibling(fulls, f"rs_pair_swap_{tag}", axes)
    return [_add_own_half(f, o, core, min(tr, o.shape[1]), f"add_own_half_{tag}{n}", ax)
            for n, (f, o, ax) in enumerate(zip(fulls, from_sibling, axes))]


def _local_step(x, tgt, cos_t, sin_t, ada, weights, small, tiles, place):
    ts, ts_in, ts_mla, tm_nn, tm_tn, t_attn, chunk = tiles
    w_in_shard, later_shards = weights
    norm_w, conv_b, ln_w, ln_b, q_norm_w, kv_norm_w, fnw = small
    h, mod, c_all, conv_w, (g_in,) = _adaln_norm(x, norm_w, *ada, ts, [w_in_shard])
    scale, gate = mod[:, D:2 * D], mod[:, 2 * D:3 * D]
    wa = _col_window(g_in, 0, A_COLS)
    wl = jnp.pad(_col_window(g_in, A_COLS, A_COLS + L_COLS_RAW), ((0, 0), (0, L_COLS - L_COLS_RAW)))
    wg = _col_window(g_in, A_COLS + L_COLS_RAW, IN_COLS)
    proj_a = _mm_nn(h, wa, tm_nn, D, "proj_a")
    u0, u1, za, (g_uq, g_ukv, g_co, g_ao, g_o) = _conv_fwd(proj_a, conv_w, conv_b, ln_w, ln_b, ts, chunk, later_shards)
    w_uq2, w_ukv = _uq_to_padded(_slabs_to_cols(g_uq)), _slabs_to_cols(g_ukv)
    wco, wao, wo = g_co.reshape(D, D), g_ao.reshape(D, D), g_o.reshape(D, D)
    proj_l = _mm_nn(h, wl, tm_nn, L_COLS, "proj_l")
    proj_g = _mm_nn(h, wg, tm_nn, D, "proj_g")
    qn, kvn, q, k, v = _mla_prep(proj_l, q_norm_w, kv_norm_w, w_uq2, w_ukv, cos_t, sin_t, ts_mla)
    o, lse = _attn_fwd(q, k, v, t_attn)
    dx2, dza, do, delta, dpg, lhs3, rhs3, vec_mid = _middle(za, o, proj_g, x, tgt, gate, fnw, wco, wao, wo, ts)
    g_wco, g_wao, g_wo = _mm_tn_stack(lhs3, rhs3, tm_tn, "grad_w_out3")
    dq, dk, dv = _attn_bwd(q, k, v, do, lse, delta, t_attn)
    dpl, g_wuq2, g_wukv, vec_mla = _mla_prep_bwd(
        dq, dk, dv, proj_l, qn, kvn, q_norm_w, kv_norm_w, w_uq2, w_ukv, cos_t, sin_t, ts_mla)

    core = place[1:2]
    nr = D // N_CHIP
    early = [_cols_to_slabs(_uq_from_padded(g_wuq2)).astype(BF16), _cols_to_slabs(g_wukv).astype(BF16),
             g_wco.reshape(N_CHIP, nr, D), g_wao.reshape(N_CHIP, nr, D), g_wo.reshape(N_CHIP, nr, D)]
    dpa, g_conv_w, vec_conv, early_got = _conv_bwd(dza, proj_a, u0, u1, conv_w, ln_w, ln_b, ts, chunk, early)

    g_wa_t = _mm_tn(dpa, h, tm_tn, D, D, "grad_w_in_a", BF16)
    g_wl_t = _mm_tn(dpl, h, tm_tn, L_COLS, D, "grad_w_in_l", BF16)
    g_wg_t = _mm_tn(dpg, h, tm_tn, D, D, "grad_w_in_g", BF16)
    g_w_in_slabs = _slabs_from_groups([g_wa_t, g_wl_t[0:L_COLS_RAW], g_wg_t], IN_COLS // N_CHIP)
    late_sums = _pair_sums([g_w_in_slabs], core, "b", [2], IN_COLS // N_CHIP)
    grad_x, vec_in, late_got = _input_bwd(dpa, dpl, dpg, wa, wl, wg, x, dx2, norm_w, scale, ts_in, late_sums)

    col_sums = jnp.concatenate(
        [vec_in, vec_mid, vec_conv, jnp.pad(vec_mla, ((0, 0), (0, D - RQ))), g_conv_w], axis=0)
    wholes = ([_sum_chip_slabs(late_got[0], late_sums[0], place, W_IN_ROWS, "sum_chip_slabs_w_in", 2)]
              + list(_sum_device_partials(early_got, early, place)))
    shards, all_col_sums = _join_halves_with_sibling(wholes, [2] + [1] * len(early), col_sums)

    return grad_x, shards, all_col_sums, c_all


def kernel(x, c, positions, w_ada, b_ada, norm_w, w_in, conv_w, conv_b, conv_ln_w, conv_ln_b, w_conv_out, q_norm_w, w_uq, kv_norm_w, w_ukv, w_attn_out, w_out, final_norm_w, loss_target, m_w_ada, m_b_ada, m_norm_w, m_w_in, m_conv_w, m_conv_b, m_conv_ln_w, m_conv_ln_b, m_w_conv_out, m_q_norm_w, m_w_uq, m_kv_norm_w, m_w_ukv, m_w_attn_out, m_w_out, m_final_norm_w, v_w_ada, v_b_ada, v_norm_w, v_w_in, v_conv_w, v_conv_b, v_conv_ln_w, v_conv_ln_b, v_w_conv_out, v_q_norm_w, v_w_uq, v_kv_norm_w, v_w_ukv, v_w_attn_out, v_w_out, v_final_norm_w):
    ix, iy, ic = _coords()
    chip = 2 * ix + iy
    dev = 4 * ix + 2 * iy + ic
    s = x.shape[1]
    tiles = (256, 512, 512, 2048, 4096, 512, 32)

    conv_w_pad = jnp.pad(conv_w[0], ((0, HALO - KC), (0, D - conv_w.shape[2])))
    small_in = jnp.concatenate([jnp.pad(c, ((0, 7), (0, 0))), conv_w_pad], axis=0)

    later_shards = [w[0].astype(BF16) for w in (w_uq, w_ukv, w_conv_out, w_attn_out, w_out)]
    weights = (w_in[0].astype(BF16), later_shards)

    ada_cols = w_ada.shape[2]
    b_shard = lax.dynamic_slice(b_ada, (0, chip * ada_cols), (1, ada_cols))
    ada = (small_in, w_ada[0], b_shard, dev.reshape(1).astype(jnp.int32))

    cos_t, sin_t = _rope_tables(positions[0])
    small = (norm_w, conv_b, conv_ln_w, conv_ln_b, q_norm_w, kv_norm_w, final_norm_w.reshape(1, D))
    place = jnp.stack([chip, ic]).astype(jnp.int32)
    grad_x, shards, gathered, c_all = _local_step(x[0], loss_target[0], cos_t, sin_t, ada, weights, small, tiles, place)
    g_w_in_s, g_w_uq_s, g_w_ukv_s, g_wco_s, g_wao_s, g_wo_s = shards

    vec_names = ("b_ada", "norm_w", "conv_b", "conv_ln_w", "conv_ln_b", "q_norm_w", "kv_norm_w", "final_norm_w")
    row = lambda a: a.reshape(1, -1)
    vectors = [(row(b_ada), row(m_b_ada), row(v_b_ada)), (norm_w, m_norm_w, v_norm_w), (conv_b, m_conv_b, v_conv_b),
               (conv_ln_w, m_conv_ln_w, v_conv_ln_w), (conv_ln_b, m_conv_ln_b, v_conv_ln_b),
               (q_norm_w, m_q_norm_w, v_q_norm_w), (kv_norm_w, m_kv_norm_w, v_kv_norm_w),
               (row(final_norm_w), row(m_final_norm_w), row(v_final_norm_w))]
    fin = _small_finalize(gathered, vectors, (conv_w, m_conv_w, v_conv_w), place[0:1])
    res = {}
    for p, (name, (w, _, _)) in enumerate(zip(vec_names, vectors)):
        shape = final_norm_w.shape if name == "final_norm_w" else w.shape
        res[name] = tuple(a.reshape(shape) for a in fin[4 * p:4 * p + 4])
    res["conv_w"] = tuple(fin[4 * len(vectors):4 * len(vectors) + 4])
    dmod_all, loss = fin[-2], fin[-1].reshape(())
    dmod_shard = lax.dynamic_slice(dmod_all, (0, chip * ada_cols), (N_DEV, ada_cols))
    g_w_ada = _ada_bwd(c_all.T, dmod_shard).reshape(1, D, ada_cols)

    def big(w, g, m, v, tr, name):
        d, nm, nv = _adamw(w, g, m, v, tr, name)
        return g.reshape(w.shape), d, nm, nv

    res["w_ada"] = big(w_ada, g_w_ada[0], m_w_ada, v_w_ada, 256, "adamw_w_ada")
    t_in = [a[0].T for a in (w_in, m_w_in, v_w_in)]
    d_t, nm_t, nv_t = _adamw(t_in[0], g_w_in_s, t_in[1], t_in[2], W_IN_ROWS, "adamw_w_in")
    res["w_in"] = tuple(a.T[None] for a in (g_w_in_s, d_t, nm_t, nv_t))
    small = _adamw_many(
        [w_uq, w_ukv, w_conv_out, w_attn_out, w_out], [g_w_uq_s, g_w_ukv_s, g_wco_s, g_wao_s, g_wo_s],
        [m_w_uq, m_w_ukv, m_w_conv_out, m_w_attn_out, m_w_out], [v_w_uq, v_w_ukv, v_w_conv_out, v_w_attn_out, v_w_out],
        128)
    res["w_uq"], res["w_ukv"], res["w_conv_out"], res["w_attn_out"], res["w_out"] = small

    order = ("w_ada", "b_ada", "norm_w", "w_in", "conv_w", "conv_b", "conv_ln_w", "conv_ln_b", "w_conv_out",
             "q_norm_w", "w_uq", "kv_norm_w", "w_ukv", "w_attn_out", "w_out", "final_norm_w")
    outs = [loss, grad_x[None]]
    for slot in range(4):
        outs += [res[name][slot] for name in order]
    return tuple(outs)
```

```python
import functools

import numpy as np
import jax
import jax.numpy as jnp
from jax import lax
from jax.experimental import pallas as pl
from jax.experimental.pallas import tpu as pltpu

F32 = jnp.float32
BF16 = jnp.bfloat16
MESH = pl.DeviceIdType.MESH

D = 1024
H = 8
DN = 128
DR = 64
RQ = 256
KC = 31
HALO = 32
EPS = 1e-6
ROPE_THETA = 10000.0
N_CHIP = 4
N_DEV = 8
LANE = 128
VMEM_BIG = 56 * 1024 * 1024

ADAM_LR = 0.001
ADAM_B1 = 0.9
ADAM_B2 = 0.999
ADAM_EPS = 1e-08
ADAM_WD = 0.01
ADAM_STEP = 10

A_COLS = 3 * D
L_COLS_RAW = RQ + RQ + DR
L_COLS = 640
G_COLS = 3 * D
IN_COLS = A_COLS + L_COLS_RAW + G_COLS


def _params(sem=None, vmem=None):
    kw = {}
    if sem is not None:
        kw["dimension_semantics"] = sem
    if vmem is not None:
        kw["vmem_limit_bytes"] = vmem
    return pltpu.CompilerParams(**kw)


def _dot(a, b):
    return jnp.dot(a, b, preferred_element_type=F32)


def _dot_nt(a, b):
    return lax.dot_general(a, b, (((1,), (1,)), ((), ())), preferred_element_type=F32)


def _dot_tn(a, b):
    return lax.dot_general(a, b, (((0,), (0,)), ((), ())), preferred_element_type=F32)


def _colsum(v):
    return jnp.sum(v, axis=0, keepdims=True)


def _rowmean(v):
    return jnp.mean(v, axis=-1, keepdims=True)


def _sigmoid(v):
    return jax.nn.sigmoid(v)


def _dsilu(v, s):
    return s * (1.0 + v * (1.0 - s))


def _swap_halves(v, first_half):
    return jnp.where(first_half, pltpu.roll(v, 96, 1), pltpu.roll(v, 32, 1))


def _first_half_mask(rows):
    lane = lax.broadcasted_iota(jnp.int32, (rows, LANE), 1)
    return (lane % 64) < 32


SMALL_IN_ROWS = 8 + HALO


def _adaln_norm(x, norm_w, small_in, w_ada_shard, b_ada_shard, dev, ts, shards):
    s = x.shape[0]
    cols = w_ada_shard.shape[1]
    taps = D // N_CHIP

    def modulation(dev_ref, x_ref, nw_ref, sm_ref, w_ref, b_ref, h_ref, mod_ref, c_ref, conv_ref,
                   part_sc, all_sc, small_sc, *sems):
        _allgather8_run(sm_ref, small_sc, *sems[0:3])
        for k in range(N_DEV):
            c_ref[k:k + 1, :] = small_sc[SMALL_IN_ROWS * k:SMALL_IN_ROWS * k + 1, :]
        for k in range(N_CHIP):
            base = SMALL_IN_ROWS * 2 * k + 8
            conv_ref[:, taps * k:taps * (k + 1)] = small_sc[base:base + HALO, 0:taps]
        cv = c_ref[...]
        part_sc[...] = jnp.dot(cv * _sigmoid(cv), w_ref[...], preferred_element_type=F32,
                               precision=lax.Precision.HIGHEST) + b_ref[...]
        _allgather8_run(part_sc, all_sc, *sems[3:6])
        for k in range(N_CHIP):
            mod_ref[:, cols * k:cols * (k + 1)] = all_sc[pl.ds(2 * N_DEV * k + dev_ref[0], 1), :]

    def body(dev_ref, x_ref, nw_ref, sm_ref, w_ref, b_ref, h_ref, mod_ref, *rest):
        xv = x_ref[...]
        r = lax.rsqrt(_rowmean(xv * xv) + EPS)
        y = xv * r * nw_ref[...]
        h_ref[...] = (y * (1.0 + mod_ref[:, D:2 * D]) + mod_ref[:, 0:D]).astype(BF16)

    row = pl.BlockSpec((ts, D), lambda i: (i, 0))
    const = lambda shape: pl.BlockSpec(shape, lambda i: (0, 0))
    n = len(shards)
    gathered_shapes, sems = _gather_operands(shards)
    outs = pl.pallas_call(
        _gather_alongside(body, 6, 4, n, s // ts - 1, modulation), grid=(s // ts,),
        in_specs=[pl.BlockSpec(memory_space=pltpu.SMEM), row, const((1, D)), const(small_in.shape),
                  const(w_ada_shard.shape), const((1, cols))] + [HBM_REF] * n,
        out_specs=[row, const((1, 3 * D)), const((N_DEV, D)), const((HALO, D))] + [HBM_REF] * n,
        out_shape=[jax.ShapeDtypeStruct((s, D), BF16), jax.ShapeDtypeStruct((1, 3 * D), F32),
                   jax.ShapeDtypeStruct((N_DEV, D), F32), jax.ShapeDtypeStruct((HALO, D), F32)] + gathered_shapes,
        scratch_shapes=[pltpu.VMEM((N_DEV, cols), F32), pltpu.VMEM((N_DEV * N_DEV, cols), F32),
                        pltpu.VMEM((N_DEV * SMALL_IN_ROWS, D), F32)] + ALLGATHER8_SEMS + ALLGATHER8_SEMS + sems,
        name="adaln_norm", compiler_params=_params(("arbitrary",), VMEM_BIG))(
            dev, x, norm_w, small_in, w_ada_shard, b_ada_shard, *shards)
    return outs[0], outs[1], outs[2], outs[3], _as_chip_slabs(outs[4:], shards)


def _mm_nn(a, b, tm, tn, name):
    m, k = a.shape
    n = b.shape[1]

    def body(a_ref, b_ref, o_ref):
        o_ref[...] = _dot(a_ref[...], b_ref[...])

    return pl.pallas_call(
        body, grid=(n // tn, m // tm),
        in_specs=[pl.BlockSpec((tm, k), lambda j, i: (i, 0)), pl.BlockSpec((k, tn), lambda j, i: (0, j))],
        out_specs=pl.BlockSpec((tm, tn), lambda j, i: (i, j)),
        out_shape=jax.ShapeDtypeStruct((m, n), F32), name=name,
        compiler_params=_params(("parallel", "parallel"), VMEM_BIG))(a, b)


def _mm_tn(a, b, tm, tk, tn, name, out_dtype=F32):
    m, k = a.shape
    n = b.shape[1]
    steps = m // tm

    def body(a_ref, b_ref, o_ref, acc_ref):
        @pl.when(pl.program_id(2) == 0)
        def _():
            acc_ref[...] = jnp.zeros_like(acc_ref)
        acc_ref[...] += _dot_tn(a_ref[...], b_ref[...])

        @pl.when(pl.program_id(2) == steps - 1)
        def _():
            o_ref[...] = acc_ref[...].astype(out_dtype)

    return pl.pallas_call(
        body, grid=(k // tk, n // tn, steps),
        in_specs=[pl.BlockSpec((tm, tk), lambda r, j, i: (i, r)), pl.BlockSpec((tm, tn), lambda r, j, i: (i, j))],
        out_specs=pl.BlockSpec((tk, tn), lambda r, j, i: (r, j)),
        out_shape=jax.ShapeDtypeStruct((k, n), out_dtype), scratch_shapes=[pltpu.VMEM((tk, tn), F32)], name=name,
        compiler_params=_params(("parallel", "parallel", "arbitrary"), VMEM_BIG))(a, b)


def _mm_tn_stack(a, b, tm, name):
    n_stack, m, k = a.shape
    n = b.shape[2]
    steps = m // tm

    def body(a_ref, b_ref, o_ref, acc_ref):
        @pl.when(pl.program_id(1) == 0)
        def _():
            acc_ref[...] = jnp.zeros_like(acc_ref)
        acc_ref[...] += _dot_tn(a_ref[0], b_ref[0])

        @pl.when(pl.program_id(1) == steps - 1)
        def _():
            o_ref[0] = acc_ref[...].astype(BF16)

    out = pl.pallas_call(
        body, grid=(n_stack, steps),
        in_specs=[pl.BlockSpec((1, tm, k), lambda g, i: (g, i, 0)), pl.BlockSpec((1, tm, n), lambda g, i: (g, i, 0))],
        out_specs=pl.BlockSpec((1, k, n), lambda g, i: (g, 0, 0)),
        out_shape=jax.ShapeDtypeStruct((n_stack, k, n), BF16), scratch_shapes=[pltpu.VMEM((k, n), F32)], name=name,
        compiler_params=_params(("parallel", "arbitrary"), VMEM_BIG))(a, b)
    return [out[g] for g in range(n_stack)]


def _coords():
    return lax.axis_index("x"), lax.axis_index("y"), lax.axis_index("c")


HBM_REF = pl.BlockSpec(memory_space=pl.ANY)


def _chip_scatter_copies(p_refs, got_refs, send_sems, recv_sems):
    x, y, c = _coords()
    copies = []
    for a in range(len(p_refs)):
        for j, (px, py) in enumerate([(1 - x, y), (x, 1 - y), (1 - x, 1 - y)]):
            copies.append(pltpu.make_async_remote_copy(
                src_ref=p_refs[a].at[2 * px + py], dst_ref=got_refs[a].at[j], send_sem=send_sems.at[3 * a + j],
                recv_sem=recv_sems.at[3 * a + j], device_id=(px, py, c), device_id_type=MESH))
    return copies


RELATIONS = [(dx, dy, dc) for dx in (0, 1) for dy in (0, 1) for dc in (0, 1)][1:]


def _device_scatter_copies(p_refs, got_refs, send_sems, recv_sems):
    x, y, c = _coords()
    copies = []
    for a in range(len(p_refs)):
        half = p_refs[a].shape[1] // 2
        for j, (dx, dy, dc) in enumerate(RELATIONS):
            px, py, pc = (1 - x if dx else x), (1 - y if dy else y), (1 - c if dc else c)
            src = p_refs[a].at[2 * px + py, pl.ds(pl.multiple_of(pc * half, 16), half), :]
            copies.append(pltpu.make_async_remote_copy(
                src_ref=src, dst_ref=got_refs[a].at[j], send_sem=send_sems.at[7 * a + j],
                recv_sem=recv_sems.at[7 * a + j], device_id=(px, py, pc), device_id_type=MESH))
    return copies


def _scatter_alongside(body, n_in, n_out, n_parts, last_step, make_copies):
    def wrapped(*refs):
        ins, parts = refs[:n_in], refs[n_in:n_in + n_parts]
        rest = refs[n_in + n_parts:]
        outs, got = rest[:n_out], rest[n_out:n_out + n_parts]
        scratch, (send_sems, recv_sems) = rest[n_out + n_parts:-2], rest[-2:]

        @pl.when(pl.program_id(0) == 0)
        def _():
            for cp in make_copies(parts, got, send_sems, recv_sems):
                cp.start()

        body(*ins, *outs, *scratch)

        @pl.when(pl.program_id(0) == last_step)
        def _():
            for cp in make_copies(parts, got, send_sems, recv_sems):
                cp.wait()

    return wrapped


def _scatter_operands(parts, per_device):
    n = len(parts)
    if per_device:
        slots, shapes = 7, [jax.ShapeDtypeStruct((7, a.shape[1] // 2, a.shape[2]), a.dtype) for a in parts]
    else:
        slots, shapes = 3, [jax.ShapeDtypeStruct((3,) + a.shape[1:], a.dtype) for a in parts]
    sems = [pltpu.SemaphoreType.DMA((slots * n,)), pltpu.SemaphoreType.DMA((slots * n,))]
    return [HBM_REF] * n, [HBM_REF] * n, shapes, sems


def _shifted_copies(win_ref, sh_ref, rows):
    for p in range(1, 8):
        sh_ref[p - 1, 0:rows, :] = win_ref[pl.ds(p, rows), :]


def _tap_rows(win_ref, sh_ref, start, rows):
    p = start % 8
    if p == 0:
        return win_ref[pl.ds(start, rows), :]
    return sh_ref[p - 1, pl.ds(start - p, rows), :]


def _conv_taps(win_ref, sh_ref, w_ref, rows, chunk, offset_of_tap):
    pieces = []
    for c0 in range(0, rows, chunk):
        acc = None
        for j in range(KC):
            term = w_ref[j:j + 1, :] * _tap_rows(win_ref, sh_ref, c0 + offset_of_tap(j), chunk)
            acc = term if acc is None else acc + term
        pieces.append(acc)
    return pieces


def _conv_fwd(proj_a, conv_w, conv_b, ln_w, ln_b, ts, chunk, shards):
    s = proj_a.shape[0]

    def body(av_ref, al_ref, ag_ref, w_ref, b_ref, lw_ref, lb_ref, u0_ref, u1_ref, za_ref, win_ref, sh_ref):
        @pl.when(pl.program_id(0) == 0)
        def _():
            win_ref[0:HALO, :] = jnp.zeros((HALO, D), F32)

        u0 = av_ref[...] * _sigmoid(al_ref[...])
        u0_ref[...] = u0
        win_ref[HALO:HALO + ts, :] = u0
        _shifted_copies(win_ref, sh_ref, ts + HALO - 8)
        pieces = _conv_taps(win_ref, sh_ref, w_ref, ts, chunk, lambda j: HALO - (KC - 1) + j)
        for n, acc in enumerate(pieces):
            u1_ref[n * chunk:(n + 1) * chunk, :] = acc + b_ref[...]
        win_ref[0:HALO, :] = win_ref[ts:ts + HALO, :]

        u1 = u1_ref[...]
        xc = u1 - _rowmean(u1)
        rstd = lax.rsqrt(_rowmean(xc * xc) + EPS)
        u2 = xc * rstd * lw_ref[...] + lb_ref[...]
        u3 = u2 * _sigmoid(u2)
        ag = ag_ref[...]
        za_ref[...] = (u3 * (ag * _sigmoid(ag))).astype(BF16)

    col = lambda c: pl.BlockSpec((ts, D), lambda i, c=c: (i, c))
    row = pl.BlockSpec((ts, D), lambda i: (i, 0))
    vec = pl.BlockSpec((1, D), lambda i: (0, 0))
    n = len(shards)
    gathered_shapes, sems = _gather_operands(shards)
    outs = pl.pallas_call(
        _gather_alongside(body, 7, 3, n, s // ts - 1), grid=(s // ts,),
        in_specs=[col(0), col(1), col(2), pl.BlockSpec((HALO, D), lambda i: (0, 0)), vec, vec, vec] + [HBM_REF] * n,
        out_specs=[row, row, row] + [HBM_REF] * n,
        out_shape=[jax.ShapeDtypeStruct((s, D), F32), jax.ShapeDtypeStruct((s, D), F32),
                   jax.ShapeDtypeStruct((s, D), BF16)] + gathered_shapes,
        scratch_shapes=[pltpu.VMEM((ts + HALO, D), F32), pltpu.VMEM((7, ts + HALO, D), F32)] + sems,
        name="conv_fwd", compiler_params=_params(("arbitrary",), VMEM_BIG))(
            proj_a, proj_a, proj_a, conv_w, conv_b, ln_w, ln_b, *shards)
    return outs[0], outs[1], outs[2], _as_chip_slabs(outs[3:], shards)


def _conv_bwd(dza, proj_a, u0, u1, conv_w, ln_w, ln_b, ts, chunk, parts):
    s = dza.shape[0]
    nt = s // ts
    per = ts // HALO

    def body(dza_ref, av_ref, al_ref, ag_ref, u0_ref, u0p_ref, u1_ref, w_ref, lw_ref, lb_ref,
             dpa_ref, gw_ref, gv_ref, dwin_ref, uwin_ref, du0_ref, gwp_ref, dsh_ref, ush_ref):
        step = pl.program_id(0)
        tile = nt - 1 - step

        @pl.when(step == 0)
        def _():
            dwin_ref[ts:ts + HALO, :] = jnp.zeros((HALO, D), F32)
            gwp_ref[...] = jnp.zeros_like(gwp_ref)
            gv_ref[...] = jnp.zeros_like(gv_ref)

        ag = ag_ref[...]
        sg = _sigmoid(ag)
        u1 = u1_ref[...]
        xc = u1 - _rowmean(u1)
        rstd = lax.rsqrt(_rowmean(xc * xc) + EPS)
        xh = xc * rstd
        u2 = xh * lw_ref[...] + lb_ref[...]
        s2 = _sigmoid(u2)
        dz = dza_ref[...]
        du3 = dz * (ag * sg)
        dpa_ref[:, 2 * D:3 * D] = (dz * (u2 * s2) * _dsilu(ag, sg)).astype(BF16)
        du2 = du3 * _dsilu(u2, s2)
        gv_ref[0:1, :] += _colsum(du2 * xh)
        gv_ref[1:2, :] += _colsum(du2)
        dxh = du2 * lw_ref[...]
        du1 = rstd * (dxh - _rowmean(dxh) - xh * _rowmean(dxh * xh))
        gv_ref[2:3, :] += _colsum(du1)
        dwin_ref[0:ts, :] = du1

        uwin_ref[0:HALO, :] = jnp.where(tile == 0, 0.0, u0p_ref[...])
        uwin_ref[HALO:HALO + ts, :] = u0_ref[...]

        _shifted_copies(dwin_ref, dsh_ref, ts + HALO - 8)
        _shifted_copies(uwin_ref, ush_ref, ts + HALO - 8)
        pieces = _conv_taps(dwin_ref, dsh_ref, w_ref, ts, chunk, lambda j: (KC - 1) - j)
        for n, acc in enumerate(pieces):
            du0_ref[n * chunk:(n + 1) * chunk, :] = acc
        for c0 in range(0, ts, chunk):
            dchunk = dwin_ref[c0:c0 + chunk, :]
            for j in range(KC):
                prod = dchunk * _tap_rows(uwin_ref, ush_ref, c0 + HALO - (KC - 1) + j, chunk)
                gwp_ref[8 * j:8 * j + 8, :] += jnp.sum(prod.reshape(chunk // 8, 8, D), axis=0)
        dwin_ref[ts:ts + HALO, :] = dwin_ref[0:HALO, :]

        du0 = du0_ref[...]
        al = al_ref[...]
        sl = _sigmoid(al)
        dpa_ref[:, 0:D] = (du0 * sl).astype(BF16)
        dpa_ref[:, D:2 * D] = (du0 * av_ref[...] * sl * (1.0 - sl)).astype(BF16)

        @pl.when(step == nt - 1)
        def _():
            for j in range(KC):
                gw_ref[j:j + 1, :] = _colsum(gwp_ref[8 * j:8 * j + 8, :])
            gw_ref[KC:HALO, :] = jnp.zeros((HALO - KC, D), F32)

    rev = lambda i: nt - 1 - i
    col = lambda c: pl.BlockSpec((ts, D), lambda i, c=c: (rev(i), c))
    row = pl.BlockSpec((ts, D), lambda i: (rev(i), 0))
    vec = pl.BlockSpec((1, D), lambda i: (0, 0))
    halo = pl.BlockSpec((HALO, D), lambda i: (jnp.maximum(rev(i) * per - 1, 0), 0))
    side_in, side_out, side_shapes, side_sems = _scatter_operands(parts, True)
    outs = pl.pallas_call(
        _scatter_alongside(body, 10, 3, len(parts), nt - 1, _device_scatter_copies), grid=(nt,),
        in_specs=[row, col(0), col(1), col(2), row, halo, row, pl.BlockSpec((HALO, D), lambda i: (0, 0)), vec, vec]
        + side_in,
        out_specs=[pl.BlockSpec((ts, A_COLS), lambda i: (rev(i), 0)),
                   pl.BlockSpec((HALO, D), lambda i: (0, 0)), pl.BlockSpec((8, D), lambda i: (0, 0))] + side_out,
        out_shape=[jax.ShapeDtypeStruct((s, A_COLS), BF16), jax.ShapeDtypeStruct((HALO, D), F32),
                   jax.ShapeDtypeStruct((8, D), F32)] + side_shapes,
        scratch_shapes=[pltpu.VMEM((ts + HALO, D), F32), pltpu.VMEM((ts + HALO, D), F32),
                        pltpu.VMEM((ts, D), F32), pltpu.VMEM((8 * HALO, D), F32),
                        pltpu.VMEM((7, ts + HALO, D), F32), pltpu.VMEM((7, ts + HALO, D), F32)] + side_sems,
        name="conv_bwd", compiler_params=_params(("arbitrary",), VMEM_BIG))(
            dza, proj_a, proj_a, proj_a, u0, u0, u1, conv_w, ln_w, ln_b, *parts)
    return outs[0], outs[1], outs[2], list(outs[3:])


def _mla_prep(proj_l, q_norm_w, kv_norm_w, w_uq2, w_ukv, cos_t, sin_t, ts):
    s = proj_l.shape[0]

    def body(pl_ref, qw_ref, kw_ref, wq_ref, wkv_ref, c_ref, s_ref, qn_ref, kvn_ref, q_ref, k_ref, v_ref):
        first = _first_half_mask(ts)
        cs = c_ref[...]
        sn = s_ref[...]

        def rms(v, w):
            return v * lax.rsqrt(_rowmean(v * v) + EPS) * w

        def rope(v):
            return v * cs + _swap_halves(v, first) * sn

        qn = rms(pl_ref[:, 0:RQ], qw_ref[...]).astype(BF16)
        kvn = rms(pl_ref[:, RQ:2 * RQ], kw_ref[...]).astype(BF16)
        qn_ref[...] = qn
        kvn_ref[...] = kvn
        q = _dot(qn, wq_ref[...])
        kv = _dot(kvn, wkv_ref[...])
        kr = rope(pl_ref[:, 2 * RQ:2 * RQ + LANE]).astype(BF16)
        for h in range(H):
            q_ref[h, :, 0:DN] = q[:, DN * h:DN * (h + 1)].astype(BF16)
            q_ref[h, :, DN:2 * DN] = rope(q[:, H * DN + LANE * h:H * DN + LANE * (h + 1)]).astype(BF16)
            k_ref[h, :, 0:DN] = kv[:, 2 * DN * h:2 * DN * h + DN].astype(BF16)
            k_ref[h, :, DN:2 * DN] = kr
            v_ref[h, :, 0:DN] = kv[:, 2 * DN * h + DN:2 * DN * (h + 1)].astype(BF16)
            v_ref[h, :, DN:2 * DN] = jnp.ones((ts, DN), BF16)

    const = lambda shape: pl.BlockSpec(shape, lambda i: (0,) * len(shape))
    rowb = lambda w: pl.BlockSpec((ts, w), lambda i: (i, 0))
    head = lambda w: pl.BlockSpec((H, ts, w), lambda i: (0, i, 0))
    return pl.pallas_call(
        body, grid=(s // ts,),
        in_specs=[rowb(L_COLS), const((1, RQ)), const((1, RQ)), const((RQ, 2 * H * DN)), const((RQ, 2 * H * DN)),
                  rowb(LANE), rowb(LANE)],
        out_specs=[rowb(RQ), rowb(RQ), head(2 * DN), head(2 * DN), head(2 * DN)],
        out_shape=[jax.ShapeDtypeStruct((s, RQ), BF16), jax.ShapeDtypeStruct((s, RQ), BF16),
                   jax.ShapeDtypeStruct((H, s, 2 * DN), BF16), jax.ShapeDtypeStruct((H, s, 2 * DN), BF16),
                   jax.ShapeDtypeStruct((H, s, 2 * DN), BF16)],
        name="mla_prep", compiler_params=_params(("parallel",), VMEM_BIG))(
            proj_l, q_norm_w, kv_norm_w, w_uq2, w_ukv, cos_t, sin_t)


def _mla_prep_bwd(dq, dk, dv, proj_l, qn, kvn, q_norm_w, kv_norm_w, w_uq2, w_ukv, cos_t, sin_t, ts):
    s = proj_l.shape[0]

    def body(dq_ref, dk_ref, dv_ref, pl_ref, qn_ref, kvn_ref, qw_ref, kw_ref, wq_ref, wkv_ref, c_ref, s_ref,
             dpl_ref, gwq_ref, gwkv_ref, gv_ref, dq2_ref, dkv2_ref):
        @pl.when(pl.program_id(0) == 0)
        def _():
            gwq_ref[...] = jnp.zeros_like(gwq_ref)
            gwkv_ref[...] = jnp.zeros_like(gwkv_ref)
            gv_ref[...] = jnp.zeros_like(gv_ref)

        first = _first_half_mask(ts)
        cs = c_ref[...] * ATT_SCALE
        sn = s_ref[...] * ATT_SCALE

        def rope_bwd(g):
            return g * cs + _swap_halves(g * sn, first)

        def rms_bwd(v, w, dy):
            r = lax.rsqrt(_rowmean(v * v) + EPS)
            vh = v * r
            dvh = dy * w
            return r * (dvh - vh * _rowmean(dvh * vh)), _colsum(dy * vh)

        dkr = None
        for h in range(H):
            dq2_ref[:, DN * h:DN * (h + 1)] = (dq_ref[h, :, 0:DN] * ATT_SCALE).astype(BF16)
            dq2_ref[:, H * DN + LANE * h:H * DN + LANE * (h + 1)] = rope_bwd(dq_ref[h, :, DN:2 * DN]).astype(BF16)
            dkv2_ref[:, 2 * DN * h:2 * DN * h + DN] = (dk_ref[h, :, 0:DN] * ATT_SCALE).astype(BF16)
            dkv2_ref[:, 2 * DN * h + DN:2 * DN * (h + 1)] = dv_ref[h].astype(BF16)
            part = dk_ref[h, :, DN:2 * DN]
            dkr = part if dkr is None else dkr + part

        dq2 = dq2_ref[...]
        dkv2 = dkv2_ref[...]
        gwq_ref[...] += _dot_tn(qn_ref[...], dq2)
        gwkv_ref[...] += _dot_tn(kvn_ref[...], dkv2)
        dcq, gq = rms_bwd(pl_ref[:, 0:RQ], qw_ref[...], _dot_nt(dq2, wq_ref[...]))
        dckv, gkv = rms_bwd(pl_ref[:, RQ:2 * RQ], kw_ref[...], _dot_nt(dkv2, wkv_ref[...]))
        gv_ref[0:1, :] += gq
        gv_ref[1:2, :] += gkv
        dpl_ref[:, 0:RQ] = dcq.astype(BF16)
        dpl_ref[:, RQ:2 * RQ] = dckv.astype(BF16)
        dpl_ref[:, 2 * RQ:2 * RQ + LANE] = rope_bwd(dkr).astype(BF16)

    const = lambda shape: pl.BlockSpec(shape, lambda i: (0,) * len(shape))
    rowb = lambda w: pl.BlockSpec((ts, w), lambda i: (i, 0))
    head = lambda w: pl.BlockSpec((H, ts, w), lambda i: (0, i, 0))
    return pl.pallas_call(
        body, grid=(s // ts,),
        in_specs=[head(2 * DN), head(2 * DN), head(DN), rowb(L_COLS), rowb(RQ), rowb(RQ), const((1, RQ)),
                  const((1, RQ)), const((RQ, 2 * H * DN)), const((RQ, 2 * H * DN)), rowb(LANE), rowb(LANE)],
        out_specs=[rowb(L_COLS), const((RQ, 2 * H * DN)), const((RQ, 2 * H * DN)), const((8, RQ))],
        out_shape=[jax.ShapeDtypeStruct((s, L_COLS), BF16), jax.ShapeDtypeStruct((RQ, 2 * H * DN), F32),
                   jax.ShapeDtypeStruct((RQ, 2 * H * DN), F32), jax.ShapeDtypeStruct((8, RQ), F32)],
        scratch_shapes=[pltpu.VMEM((ts, 2 * H * DN), BF16), pltpu.VMEM((ts, 2 * H * DN), BF16)],
        name="mla_prep_bwd", compiler_params=_params(("arbitrary",), VMEM_BIG))(
            dq, dk, dv, proj_l, qn, kvn, q_norm_w, kv_norm_w, w_uq2, w_ukv, cos_t, sin_t)


def _causal_pairs(n, by_key):
    if by_key:
        pairs = [(i, j) for j in range(n) for i in range(j, n)]
    else:
        pairs = [(i, j) for i in range(n) for j in range(i + 1)]
    return (jnp.asarray(np.array([p[0] for p in pairs], np.int32)),
            jnp.asarray(np.array([p[1] for p in pairs], np.int32)))


ATT_SCALE = float((DN + DR) ** -0.5)
LOG2E = 1.4426950408889634
LN2 = 0.6931471805599453
ATT_HEADS_FWD = 8
ATT_HEADS = 4
W_IN_ROWS = 336
ATT_ROWS = 16


def _diag_width(r0, t):
    return min(t, -(-(r0 + ATT_ROWS) // LANE) * LANE)


def _diag_mask_rows(r0, width):
    rows = r0 + lax.broadcasted_iota(jnp.int32, (ATT_ROWS, width), 0)
    cols = lax.broadcasted_iota(jnp.int32, (ATT_ROWS, width), 1)
    return cols <= rows


def _diag_mask(t):
    return lax.broadcasted_iota(jnp.int32, (t, t), 1) <= lax.broadcasted_iota(jnp.int32, (t, t), 0)


def _attn_fwd(q, k, v, t):
    s = q.shape[1]
    n = s // t
    scale2 = float((DN + DR) ** -0.5) * LOG2E
    qi, ki = _causal_pairs(n, by_key=False)

    def body(qi_ref, ki_ref, q_ref, k_ref, v_ref, o_ref, lse_ref, *scratch):
        per_head = [scratch[5 * h:5 * h + 5] for h in range(ATT_HEADS_FWD)]
        p = pl.program_id(1)
        i = qi_ref[p]
        j = ki_ref[p]

        @pl.when(j == 0)
        def _():
            for m_sc, acc_sc, _, _, _ in per_head:
                m_sc[...] = jnp.full_like(m_sc, -jnp.inf)
                acc_sc[...] = jnp.zeros_like(acc_sc)

        def scores(h, diag):
            sc = _dot_nt(q_ref[h], k_ref[h])
            if diag:
                sc = jnp.where(_diag_mask(t), sc, -jnp.inf)
            per_head[h][2][...] = sc

        def rowmax(h, rows):
            per_head[h][4][rows, :] = jnp.max(per_head[h][2][rows, :], axis=-1, keepdims=True)

        def stats(h):
            m_sc, acc_sc, _, _, mx_sc = per_head[h]
            m_prev = m_sc[...]
            m_new = jnp.maximum(m_prev, mx_sc[...] * scale2)
            m_sc[...] = m_new
            acc_sc[...] = jnp.exp2(m_prev - m_new) * acc_sc[...]

        def probs(h, rows):
            m_sc, _, s_sc, p_sc, _ = per_head[h]
            p_sc[rows, :] = jnp.exp2(s_sc[rows, :] * scale2 - m_sc[rows, :]).astype(BF16)

        def values(h):
            _, acc_sc, _, p_sc, _ = per_head[h]
            acc_sc[...] += _dot(p_sc[...], v_ref[h])

        def step(diag):
            blocks = [slice(r0, r0 + ATT_ROWS) for r0 in range(0, t, ATT_ROWS)]
            for h in range(ATT_HEADS_FWD):
                scores(h, diag)
            for rows in blocks:
                rowmax(0, rows)
            stats(0)
            for h in range(ATT_HEADS_FWD):
                for rows in blocks:
                    probs(h, rows)
                    if h + 1 < ATT_HEADS_FWD:
                        rowmax(h + 1, rows)
                if h + 1 < ATT_HEADS_FWD:
                    stats(h + 1)
                values(h)

        @pl.when(j < i)
        def _():
            step(False)

        @pl.when(j == i)
        def _():
            step(True)
            for h, (m_sc, acc_sc, _, _, _) in enumerate(per_head):
                l = acc_sc[:, DN:2 * DN]
                o_ref[:, DN * h:DN * (h + 1)] = acc_sc[:, 0:DN] / l
                lse_ref[h] = (m_sc[...] + jnp.log2(l[:, 0:1])) * LN2

    hb = ATT_HEADS_FWD
    grid_spec = pltpu.PrefetchScalarGridSpec(
        num_scalar_prefetch=2, grid=(H // hb, int(qi.shape[0])),
        in_specs=[pl.BlockSpec((hb, t, 2 * DN), lambda h, p, qi, ki: (h, qi[p], 0)),
                  pl.BlockSpec((hb, t, 2 * DN), lambda h, p, qi, ki: (h, ki[p], 0)),
                  pl.BlockSpec((hb, t, 2 * DN), lambda h, p, qi, ki: (h, ki[p], 0))],
        out_specs=[pl.BlockSpec((t, hb * DN), lambda h, p, qi, ki: (qi[p], h)),
                   pl.BlockSpec((hb, t, 1), lambda h, p, qi, ki: (h, qi[p], 0))],
        scratch_shapes=[pltpu.VMEM((t, 1), F32), pltpu.VMEM((t, 2 * DN), F32), pltpu.VMEM((t, t), F32),
                        pltpu.VMEM((t, t), BF16), pltpu.VMEM((t, 1), F32)] * hb)
    return pl.pallas_call(
        body, grid_spec=grid_spec,
        out_shape=[jax.ShapeDtypeStruct((s, H * DN), F32), jax.ShapeDtypeStruct((H, s, 1), F32)],
        name="attn_fwd", compiler_params=_params(("parallel", "arbitrary"), VMEM_BIG))(qi, ki, q, k, v)


def _attn_bwd(q, k, v, do, lse, delta, t):
    s = q.shape[1]
    n = s // t
    scale = ATT_SCALE
    qi, ki = _causal_pairs(n, by_key=True)

    def body(qi_ref, ki_ref, q_ref, k_ref, v_ref, do_ref, lse_ref, dl_ref, dq_ref, dk_ref, dv_ref,
             dk_sc, dv_sc, s_sc, dp_sc, p_sc, ds_sc):
        p = pl.program_id(1)
        i = qi_ref[p]
        j = ki_ref[p]

        @pl.when(p == 0)
        def _():
            dq_ref[...] = jnp.zeros_like(dq_ref)

        @pl.when(i == j)
        def _():
            dk_sc[...] = jnp.zeros_like(dk_sc)
            dv_sc[...] = jnp.zeros_like(dv_sc)

        def step(diag):
            for h in range(ATT_HEADS):
                s_sc[h] = _dot_nt(q_ref[h], k_ref[h])
                dp_sc[h] = _dot_nt(do_ref[:, DN * h:DN * (h + 1)], v_ref[h, :, 0:DN])
            for h in range(ATT_HEADS):
                for r0 in range(0, t, ATT_ROWS):
                    rows = slice(r0, r0 + ATT_ROWS)
                    width = _diag_width(r0, t) if diag else t
                    sc = s_sc[h, rows, 0:width] * (scale * LOG2E)
                    if diag:
                        sc = jnp.where(_diag_mask_rows(r0, width), sc, -jnp.inf)
                    pr = jnp.exp2(sc - lse_ref[h, rows, :] * LOG2E)
                    ds = pr * (dp_sc[h, rows, 0:width] - dl_ref[h, rows, :])
                    p_sc[h, rows, 0:width] = pr.astype(BF16)
                    ds_sc[h, rows, 0:width] = ds.astype(BF16)
                    if width < t:
                        p_sc[h, rows, width:t] = jnp.zeros((ATT_ROWS, t - width), BF16)
                        ds_sc[h, rows, width:t] = jnp.zeros((ATT_ROWS, t - width), BF16)
            q_rows = pl.ds(pl.multiple_of(i * t, t), t)
            for h in range(ATT_HEADS):
                dv_sc[h] += _dot_tn(p_sc[h], do_ref[:, DN * h:DN * (h + 1)])
                dk_sc[h] += _dot_tn(ds_sc[h], q_ref[h])
                dq_ref[h, q_rows, :] += _dot(ds_sc[h], k_ref[h])

        @pl.when(i > j)
        def _():
            step(False)

        @pl.when(i == j)
        def _():
            step(True)

        @pl.when(i == n - 1)
        def _():
            dk_ref[...] = dk_sc[...]
            dv_ref[...] = dv_sc[...]

    hb = ATT_HEADS
    grid_spec = pltpu.PrefetchScalarGridSpec(
        num_scalar_prefetch=2, grid=(H // hb, int(qi.shape[0])),
        in_specs=[pl.BlockSpec((hb, t, 2 * DN), lambda h, p, qi, ki: (h, qi[p], 0)),
                  pl.BlockSpec((hb, t, 2 * DN), lambda h, p, qi, ki: (h, ki[p], 0)),
                  pl.BlockSpec((hb, t, 2 * DN), lambda h, p, qi, ki: (h, ki[p], 0)),
                  pl.BlockSpec((t, hb * DN), lambda h, p, qi, ki: (qi[p], h)),
                  pl.BlockSpec((hb, t, 1), lambda h, p, qi, ki: (h, qi[p], 0)),
                  pl.BlockSpec((hb, t, 1), lambda h, p, qi, ki: (h, qi[p], 0))],
        out_specs=[pl.BlockSpec((hb, s, 2 * DN), lambda h, p, qi, ki: (h, 0, 0), pipeline_mode=pl.Buffered(1)),
                   pl.BlockSpec((hb, t, 2 * DN), lambda h, p, qi, ki: (h, ki[p], 0)),
                   pl.BlockSpec((hb, t, DN), lambda h, p, qi, ki: (h, ki[p], 0))],
        scratch_shapes=[pltpu.VMEM((hb, t, 2 * DN), F32), pltpu.VMEM((hb, t, DN), F32),
                        pltpu.VMEM((hb, t, t), F32), pltpu.VMEM((hb, t, t), F32),
                        pltpu.VMEM((hb, t, t), BF16), pltpu.VMEM((hb, t, t), BF16)])
    return pl.pallas_call(
        body, grid_spec=grid_spec,
        out_shape=[jax.ShapeDtypeStruct((H, s, 2 * DN), F32), jax.ShapeDtypeStruct((H, s, 2 * DN), F32),
                   jax.ShapeDtypeStruct((H, s, DN), F32)],
        name="attn_bwd", compiler_params=_params(("parallel", "arbitrary"), VMEM_BIG))(
            qi, ki, q, k, v, do, lse, delta)


def _middle(za, o, proj_g, x, tgt, gate, fnw, wco, wao, wo, ts):
    s = x.shape[0]
    inv_d = 1.0 / D

    def body(za_ref, o_ref, bg_ref, ga_ref, gb_ref, x_ref, t_ref, gate_ref, fnw_ref, wco_ref, wao_ref, wo_ref,
             dx2_ref, dza_ref, do_ref, dl_ref, dpg_ref, lhs_ref, rhs_ref, vec_ref):
        @pl.when(pl.program_id(0) == 0)
        def _():
            vec_ref[...] = jnp.zeros_like(vec_ref)

        ov = o_ref[...]
        bg = bg_ref[...]
        sb = _sigmoid(bg)
        silu_b = bg * sb
        zb = (ov * silu_b).astype(BF16)
        lhs_ref[0] = za_ref[...]
        lhs_ref[1] = zb
        ya = _dot(za_ref[...], wco_ref[...])
        yb = _dot(zb, wao_ref[...])
        sa = _sigmoid(ga_ref[...])
        sg = _sigmoid(gb_ref[...])
        mg = (sa * ya + sg * yb).astype(BF16)
        lhs_ref[2] = mg
        mo = _dot(mg, wo_ref[...])
        gate_v = gate_ref[...]
        x2 = x_ref[...] + gate_v * mo
        r = lax.rsqrt(_rowmean(x2 * x2) + EPS)
        xh = x2 * r
        fw = fnw_ref[...]
        e = xh * fw - t_ref[...]
        vec_ref[2:3, :] += _colsum(e * e)
        dy = e * inv_d
        vec_ref[0:1, :] += _colsum(dy * xh)
        dxh = dy * fw
        dx2 = r * (dxh - xh * _rowmean(dxh * xh))
        dx2_ref[...] = dx2
        vec_ref[1:2, :] += _colsum(dx2 * mo)
        dmo = (gate_v * dx2).astype(BF16)
        rhs_ref[2] = dmo
        dmg = _dot_nt(dmo, wo_ref[...])
        dya = (sa * dmg).astype(BF16)
        dyb = (sg * dmg).astype(BF16)
        rhs_ref[0] = dya
        rhs_ref[1] = dyb
        dpg_ref[:, D:2 * D] = (dmg * ya * (sa * (1.0 - sa))).astype(BF16)
        dpg_ref[:, 2 * D:3 * D] = (dmg * yb * (sg * (1.0 - sg))).astype(BF16)
        dza_ref[...] = _dot_nt(dya, wco_ref[...])
        dzb = _dot_nt(dyb, wao_ref[...])
        dov = dzb * silu_b
        do_ref[...] = dov.astype(BF16)
        dpg_ref[:, 0:D] = (dzb * ov * _dsilu(bg, sb)).astype(BF16)
        dprod = dov * ov
        for h in range(H):
            dl_ref[h] = jnp.sum(dprod[:, DN * h:DN * (h + 1)], axis=-1, keepdims=True)

    col = lambda c: pl.BlockSpec((ts, D), lambda i, c=c: (i, c))
    row = pl.BlockSpec((ts, D), lambda i: (i, 0))
    vec = pl.BlockSpec((1, D), lambda i: (0, 0))
    wsp = pl.BlockSpec((D, D), lambda i: (0, 0))
    stack = pl.BlockSpec((3, ts, D), lambda i: (0, i, 0))
    bf = jax.ShapeDtypeStruct((s, D), BF16)
    ff = jax.ShapeDtypeStruct((s, D), F32)
    return pl.pallas_call(
        body, grid=(s // ts,),
        in_specs=[row, row, col(0), col(1), col(2), row, row, vec, vec, wsp, wsp, wsp],
        out_specs=[row, row, row, pl.BlockSpec((H, ts, 1), lambda i: (0, i, 0)),
                   pl.BlockSpec((ts, G_COLS), lambda i: (i, 0)), stack, stack,
                   pl.BlockSpec((8, D), lambda i: (0, 0))],
        out_shape=[ff, ff, bf, jax.ShapeDtypeStruct((H, s, 1), F32), jax.ShapeDtypeStruct((s, G_COLS), BF16),
                   jax.ShapeDtypeStruct((3, s, D), BF16), jax.ShapeDtypeStruct((3, s, D), BF16),
                   jax.ShapeDtypeStruct((8, D), F32)],
        name="middle", compiler_params=_params(("arbitrary",), VMEM_BIG))(
            za, o, proj_g, proj_g, proj_g, x, tgt, gate, fnw, wco, wao, wo)


def _input_bwd(dpa, dpl, dpg, wa, wl, wg, x, dx2, norm_w, scale, ts, parts):
    s = x.shape[0]

    def body(dpa_ref, dpl_ref, dpg_ref, wa_ref, wl_ref, wg_ref, x_ref, dx2_ref, nw_ref, sc_ref, gx_ref, gv_ref):
        @pl.when(pl.program_id(0) == 0)
        def _():
            gv_ref[...] = jnp.zeros_like(gv_ref)

        dh = (_dot_nt(dpa_ref[...], wa_ref[...]) + _dot_nt(dpl_ref[...], wl_ref[...])
              + _dot_nt(dpg_ref[...], wg_ref[...]))
        xv = x_ref[...]
        r = lax.rsqrt(_rowmean(xv * xv) + EPS)
        xh = xv * r
        nw = nw_ref[...]
        gv_ref[0:1, :] += _colsum(dh)
        gv_ref[1:2, :] += _colsum(dh * (xh * nw))
        dy = dh * (1.0 + sc_ref[...])
        gv_ref[2:3, :] += _colsum(dy * xh)
        dxh = dy * nw
        gx_ref[...] = dx2_ref[...] + r * (dxh - xh * _rowmean(dxh * xh))

    const = lambda shape: pl.BlockSpec(shape, lambda i: (0, 0))
    rowb = lambda w: pl.BlockSpec((ts, w), lambda i: (i, 0))
    side_in, side_out, side_shapes, side_sems = _scatter_operands(parts, False)
    outs = pl.pallas_call(
        _scatter_alongside(body, 10, 2, len(parts), s // ts - 1, _chip_scatter_copies), grid=(s // ts,),
        in_specs=[rowb(A_COLS), rowb(L_COLS), rowb(G_COLS), const((D, A_COLS)), const((D, L_COLS)),
                  const((D, G_COLS)), rowb(D), rowb(D), const((1, D)), const((1, D))] + side_in,
        out_specs=[rowb(D), const((8, D))] + side_out,
        out_shape=[jax.ShapeDtypeStruct((s, D), F32), jax.ShapeDtypeStruct((8, D), F32)] + side_shapes,
        scratch_shapes=side_sems,
        name="input_bwd", compiler_params=_params(("arbitrary",), VMEM_BIG))(
            dpa, dpl, dpg, wa, wl, wg, x, dx2, norm_w, scale, *parts)
    return outs[0], outs[1], list(outs[2:])


def _adamw_math(w, g, m, v):
    nm = ADAM_B1 * m + (1.0 - ADAM_B1) * g
    nv = ADAM_B2 * v + (1.0 - ADAM_B2) * (g * g)
    m_hat = nm / (1.0 - ADAM_B1 ** ADAM_STEP)
    v_hat = nv / (1.0 - ADAM_B2 ** ADAM_STEP)
    return -ADAM_LR * (m_hat / (jnp.sqrt(v_hat) + ADAM_EPS) + ADAM_WD * w), nm, nv


def _adamw(w, g, m, v, tr, name):
    lead, (rows, cols) = w.shape[:-2], w.shape[-2:]

    def body(w_ref, g_ref, m_ref, v_ref, d_ref, nm_ref, nv_ref):
        d_ref[...], nm_ref[...], nv_ref[...] = _adamw_math(w_ref[...], g_ref[...], m_ref[...], v_ref[...])

    blk = pl.BlockSpec((1,) * len(lead) + (tr, cols), lambda i: (0,) * len(lead) + (i, 0))
    shp = jax.ShapeDtypeStruct(w.shape, F32)
    return pl.pallas_call(
        body, grid=(rows // tr,), in_specs=[blk] * 4, out_specs=[blk] * 3, out_shape=[shp] * 3, name=name,
        compiler_params=_params(("parallel",), VMEM_BIG))(w, g.reshape(w.shape), m, v)


ROW_SHIFT, ROW_SCALE, ROW_NORM_W = 0, 1, 2
ROW_FINAL_NORM_W, ROW_GATE, ROW_LOSS = 8, 9, 10
ROW_LN_W, ROW_LN_B, ROW_CONV_B = 16, 17, 18
ROW_Q_NORM_W, ROW_KV_NORM_W = 24, 25
ROW_CONV_W = 32
SUM_ROWS = 64
VECTOR_ROWS = ((ROW_SHIFT, ROW_SCALE, ROW_GATE), (ROW_NORM_W,), (ROW_CONV_B,), (ROW_LN_W,), (ROW_LN_B,),
               (ROW_Q_NORM_W,), (ROW_KV_NORM_W,), (ROW_FINAL_NORM_W,))


def _small_finalize(gathered, vectors, conv, chip):
    n = len(vectors)
    cw = conv[0].shape[2]

    def body(chip_ref, g_ref, *refs):
        ins, outs = refs[:3 * n + 3], refs[3 * n + 3:]
        tot = g_ref[0]
        for k in range(1, N_DEV):
            tot = tot + g_ref[k]
        for p, rows in enumerate(VECTOR_ROWS):
            w_ref, m_ref, v_ref = ins[3 * p:3 * p + 3]
            g_out, d_out, nm_out, nv_out = outs[4 * p:4 * p + 4]
            width = w_ref.shape[1] // len(rows)
            for q, r in enumerate(rows):
                lanes = slice(q * width, (q + 1) * width)
                g = tot[r:r + 1, 0:width]
                g_out[:, lanes] = g
                d_out[:, lanes], nm_out[:, lanes], nv_out[:, lanes] = _adamw_math(
                    w_ref[:, lanes], g, m_ref[:, lanes], v_ref[:, lanes])
        cols = pl.ds(pl.multiple_of(chip_ref[0] * cw, LANE), cw)
        gc = g_ref[0, pl.ds(ROW_CONV_W, KC), cols]
        for k in range(1, N_DEV):
            gc = gc + g_ref[k, pl.ds(ROW_CONV_W, KC), cols]
        cw_ref, cm_ref, cv_ref = ins[3 * n:3 * n + 3]
        g_out, d_out, nm_out, nv_out, dmod_ref, loss_ref = outs[4 * n:]
        g_out[0] = gc
        d_out[0], nm_out[0], nv_out[0] = _adamw_math(cw_ref[0], gc, cm_ref[0], cv_ref[0])
        for k in range(N_DEV):
            for q, r in enumerate((ROW_SHIFT, ROW_SCALE, ROW_GATE)):
                dmod_ref[k:k + 1, q * D:(q + 1) * D] = g_ref[k, r:r + 1, :]
        loss_ref[...] = (0.5 / D) * jnp.sum(tot[ROW_LOSS:ROW_LOSS + 1, :], axis=-1, keepdims=True)

    flat_in = [a for triple in vectors for a in triple] + list(conv)
    shapes = [jax.ShapeDtypeStruct(w.shape, F32) for w, _, _ in vectors for _ in range(4)]
    shapes += [jax.ShapeDtypeStruct(conv[0].shape, F32)] * 4
    shapes += [jax.ShapeDtypeStruct((N_DEV, 3 * D), F32), jax.ShapeDtypeStruct((1, 1), F32)]
    whole = pl.BlockSpec(memory_space=pltpu.VMEM)
    return pl.pallas_call(
        body, out_shape=shapes,
        in_specs=[pl.BlockSpec(memory_space=pltpu.SMEM)] + [whole] * (1 + len(flat_in)),
        out_specs=[whole] * len(shapes), name="small_finalize")(chip, gathered, *flat_in)


def _ada_bwd(c_all_t, dmod_shard):
    def body(c_ref, d_ref, o_ref):
        cv = c_ref[...]
        o_ref[...] = jnp.dot(cv * _sigmoid(cv), d_ref[...], preferred_element_type=F32,
                             precision=lax.Precision.HIGHEST)

    return pl.pallas_call(
        body, out_shape=jax.ShapeDtypeStruct((D, dmod_shard.shape[1]), F32), name="ada_bwd")(c_all_t, dmod_shard)


def _sum_chip_slabs(arrived, part, place, tr, name, axis):
    n, rows, cols = arrived.shape
    per = rows // tr
    own_map = ((lambda i, pc: (pc[0], i, 0)) if part.shape[1] == rows
               else (lambda i, pc: (pc[0], pc[1] * per + i, 0)))

    def body(place_ref, a_ref, p_ref, o_ref):
        acc = p_ref[0].astype(F32)
        for k in range(n):
            acc = acc + a_ref[k].astype(F32)
        o_ref[...] = acc

    if axis == 1:
        whole, out_map = (2 * rows, cols), lambda i, pc: (pc[1] * per + i, 0)
    else:
        whole, out_map = (rows, 2 * cols), lambda i, pc: (i, pc[1])
    grid_spec = pltpu.PrefetchScalarGridSpec(
        num_scalar_prefetch=1, grid=(per,),
        in_specs=[pl.BlockSpec((n, tr, cols), lambda i, pc: (0, i, 0)),
                  pl.BlockSpec((1, tr, cols), own_map)],
        out_specs=pl.BlockSpec((tr, cols), out_map))
    return pl.pallas_call(
        body, grid_spec=grid_spec, out_shape=jax.ShapeDtypeStruct(whole, F32), name=name,
        compiler_params=_params(("parallel",)))(place, arrived, part)


def _sum_device_partials(arrived, parts, place):
    n = len(arrived)

    def body(place_ref, *refs):
        a_refs, p_refs, o_refs = refs[:n], refs[n:2 * n], refs[2 * n:]
        for a in range(n):
            acc = p_refs[a][0].astype(F32)
            for k in range(arrived[a].shape[0]):
                acc = acc + a_refs[a][k].astype(F32)
            o_refs[a][...] = acc

    grid_spec = pltpu.PrefetchScalarGridSpec(
        num_scalar_prefetch=1, grid=(1,),
        in_specs=[pl.BlockSpec(a.shape, lambda i, pc: (0, 0, 0)) for a in arrived]
        + [pl.BlockSpec((1,) + a.shape[1:], lambda i, pc: (pc[0], pc[1], 0)) for a in arrived],
        out_specs=[pl.BlockSpec(a.shape[1:], lambda i, pc: (pc[1], 0)) for a in arrived])
    return pl.pallas_call(
        body, grid_spec=grid_spec,
        out_shape=[jax.ShapeDtypeStruct((2 * a.shape[1], a.shape[2]), F32) for a in arrived],
        name="sum_device_partials", compiler_params=_params(("arbitrary",), VMEM_BIG))(place, *arrived, *parts)


def _adamw_many(ws, gs, ms, vs, tr):
    n = len(ws)
    rows = ws[0].shape[1]

    def body(*refs):
        ins, outs = refs[:4 * n], refs[4 * n:]
        for a in range(n):
            w_ref, g_ref, m_ref, v_ref = ins[4 * a:4 * a + 4]
            outs[3 * a][...], outs[3 * a + 1][...], outs[3 * a + 2][...] = _adamw_math(
                w_ref[...], g_ref[...], m_ref[...], v_ref[...])

    blk = lambda w: pl.BlockSpec((1, tr, w.shape[2]), lambda i: (0, i, 0))
    gs = [g.reshape(w.shape) for g, w in zip(gs, ws)]
    flat = [a for quad in zip(ws, gs, ms, vs) for a in quad]
    outs = pl.pallas_call(
        body, grid=(rows // tr,), in_specs=[blk(w) for w in ws for _ in range(4)],
        out_specs=[blk(w) for w in ws for _ in range(3)],
        out_shape=[jax.ShapeDtypeStruct(w.shape, F32) for w in ws for _ in range(3)], name="adamw_small_matrices",
        compiler_params=_params(("parallel",), VMEM_BIG))(*flat)
    return [(gs[a], outs[3 * a], outs[3 * a + 1], outs[3 * a + 2]) for a in range(n)]


def _add_own_half(full, other, core, tr, name, axis):
    n, rows, cols = other.shape
    per = rows // tr

    def body(c_ref, f_ref, o_ref, out_ref):
        out_ref[...] = (f_ref[...].astype(F32) + o_ref[...].astype(F32)).astype(BF16)

    full_map = (lambda k, i, c: (k, c[0] * per + i, 0)) if axis == 1 else (lambda k, i, c: (k, i, c[0]))
    grid_spec = pltpu.PrefetchScalarGridSpec(
        num_scalar_prefetch=1, grid=(n, per),
        in_specs=[pl.BlockSpec((1, tr, cols), full_map),
                  pl.BlockSpec((1, tr, cols), lambda k, i, c: (k, i, 0))],
        out_specs=pl.BlockSpec((1, tr, cols), lambda k, i, c: (k, i, 0)))
    return pl.pallas_call(
        body, grid_spec=grid_spec, out_shape=jax.ShapeDtypeStruct((n, rows, cols), BF16), name=name,
        compiler_params=_params(("parallel", "parallel"), VMEM_BIG))(core, full, other)


def _allgather8_run(x_ref, out_ref, send_sems, recv_sems, local_sem):
    m = x_ref.shape[0]
    x, y, c = _coords()
    me, sibling = (x, y, c), (x, y, 1 - c)
    chips = [(1 - x, y), (x, 1 - y), (1 - x, 1 - y)]

    def rows(px, py, pc):
        return out_ref.at[pl.ds(pl.multiple_of((4 * px + 2 * py + pc) * m, 8), m), :]

    def copy(k, blk, to, source=None):
        return pltpu.make_async_remote_copy(
            src_ref=rows(*blk) if source is None else source, dst_ref=rows(*blk),
            send_sem=send_sems.at[k], recv_sem=recv_sems.at[k], device_id=to, device_id_type=MESH)

    mine = pltpu.make_async_copy(x_ref, rows(*me), local_sem)
    mine.start()
    first = [copy(0, me, sibling, source=x_ref)]
    first += [copy(1 + j, me, (*chip, c), source=x_ref) for j, chip in enumerate(chips)]
    for cp in first:
        cp.start()
    passed = [copy(4 + j, (*chip, c), sibling) for j, chip in enumerate(chips)]
    for j, chip in enumerate(chips):
        copy(1 + j, (*chip, c), me).wait_recv()
        passed[j].start()
    copy(0, sibling, me).wait_recv()
    for j, chip in enumerate(chips):
        copy(4 + j, (*chip, 1 - c), me).wait_recv()
    for cp in first + passed:
        cp.wait_send()
    mine.wait()


ALLGATHER8_SEMS = [pltpu.SemaphoreType.DMA((7,)), pltpu.SemaphoreType.DMA((7,)), pltpu.SemaphoreType.DMA]


def _gather_plan(x_refs, out_refs, send_sems, recv_sems, local_sems):
    n = len(x_refs)
    halves = [r.shape[0] // 2 for r in x_refs]
    x, y, c = _coords()
    me, sibling = (x, y, c), (x, y, 1 - c)
    chips = [(1 - x, y), (x, 1 - y), (1 - x, 1 - y)]

    def src(a):
        return x_refs[a].at[pl.ds(pl.multiple_of(c * halves[a], 16), halves[a]), :]

    def blk(a, px, py, pc):
        return out_refs[a].at[4 * px + 2 * py + pc]

    def copy(a, k, who, to, source=None):
        return pltpu.make_async_remote_copy(
            src_ref=blk(a, *who) if source is None else source, dst_ref=blk(a, *who),
            send_sem=send_sems.at[7 * a + k], recv_sem=recv_sems.at[7 * a + k], device_id=to, device_id_type=MESH)

    def mine(a):
        return pltpu.make_async_copy(src(a), blk(a, *me), local_sems.at[a])

    def first(a):
        return ([copy(a, 0, me, sibling, source=src(a))]
                + [copy(a, 1 + j, me, (*chip, c), source=src(a)) for j, chip in enumerate(chips)])

    def begin():
        for a in range(n):
            mine(a).start()
        for a in range(n):
            for cp in first(a):
                cp.start()

    def finish():
        onward = []
        for j, chip in enumerate(chips):
            for a in range(n):
                copy(a, 1 + j, (*chip, c), me).wait_recv()
                onward.append(copy(a, 4 + j, (*chip, c), sibling))
                onward[-1].start()
        for a in range(n):
            copy(a, 0, sibling, me).wait_recv()
        for j, chip in enumerate(chips):
            for a in range(n):
                copy(a, 4 + j, (*chip, 1 - c), me).wait_recv()
        for a in range(n):
            for cp in first(a):
                cp.wait_send()
        for cp in onward:
            cp.wait_send()
        for a in range(n):
            mine(a).wait()

    return begin, finish


def _gather_operands(shards):
    n = len(shards)
    shapes = [jax.ShapeDtypeStruct((N_DEV, a.shape[0] // 2, a.shape[1]), a.dtype) for a in shards]
    sems = [pltpu.SemaphoreType.DMA((7 * n,)), pltpu.SemaphoreType.DMA((7 * n,)), pltpu.SemaphoreType.DMA((n,))]
    return shapes, sems


def _as_chip_slabs(gathered, shards):
    return [o.reshape(N_CHIP, a.shape[0], a.shape[1]) for o, a in zip(gathered, shards)]


def _gather_alongside(body, n_in, n_out, n_shards, last_step, first=None):
    def wrapped(*refs):
        ins, shards = refs[:n_in], refs[n_in:n_in + n_shards]
        rest = refs[n_in + n_shards:]
        outs, gathered = rest[:n_out], rest[n_out:n_out + n_shards]
        scratch, sems = rest[n_out + n_shards:-3], rest[-3:]

        @pl.when(pl.program_id(0) == 0)
        def _():
            if first is not None:
                first(*ins, *outs, *scratch)
            _gather_plan(shards, gathered, *sems)[0]()

        body(*ins, *outs, *scratch)

        @pl.when(pl.program_id(0) == last_step)
        def _():
            _gather_plan(shards, gathered, *sems)[1]()

    return wrapped


def _half(ref, axis, which, ndim):
    size = ref.shape[axis] // 2
    idx = [slice(None)] * ndim
    idx[axis] = pl.ds(pl.multiple_of(which * size, 8 if axis == ndim - 2 else LANE), size)
    return ref.at[tuple(idx)]


def _swap_halves_with_sibling(fulls, name, axes):
    n = len(fulls)

    def body(*refs):
        f_refs, got_refs = refs[:n], refs[n:2 * n]
        send_sems, recv_sems = refs[2 * n:]
        x, y, c = _coords()
        copies = []
        for a in range(n):
            copies.append(pltpu.make_async_remote_copy(
                src_ref=_half(f_refs[a], axes[a], 1 - c, 3), dst_ref=got_refs[a], send_sem=send_sems.at[a],
                recv_sem=recv_sems.at[a], device_id=(x, y, 1 - c), device_id_type=MESH))
        for cp in copies:
            cp.start()
        for cp in copies:
            cp.wait()

    def halved(a, axis):
        shape = list(a.shape)
        shape[axis] //= 2
        return jax.ShapeDtypeStruct(tuple(shape), a.dtype)

    return pl.pallas_call(
        body, out_shape=[halved(a, ax) for a, ax in zip(fulls, axes)],
        in_specs=[HBM_REF] * n, out_specs=[HBM_REF] * n,
        scratch_shapes=[pltpu.SemaphoreType.DMA((n,)), pltpu.SemaphoreType.DMA((n,))],
        name=name)(*fulls)


def _join_halves_with_sibling(wholes, axes, block):
    n = len(wholes)
    twice = jnp.concatenate([block, block], axis=0)
    gathered_shapes, gather_sems = _gather_operands([twice])

    def body(*refs):
        out_refs = refs[n + 1:2 * n + 1]
        send_sems, recv_sems = refs[2 * n + 2:2 * n + 4]
        begin, finish = _gather_plan([refs[n]], [refs[2 * n + 1]], *refs[2 * n + 4:])
        x, y, c = _coords()

        def push(a, core):
            half = _half(out_refs[a], axes[a] - 1, core, 2)
            return pltpu.make_async_remote_copy(
                src_ref=half, dst_ref=half, send_sem=send_sems.at[a], recv_sem=recv_sems.at[a],
                device_id=(x, y, 1 - c), device_id_type=MESH)

        begin()
        for a in range(n):
            push(a, c).start()
        finish()
        for a in range(n):
            push(a, 1 - c).wait_recv()
        for a in range(n):
            push(a, c).wait_send()

    outs = pl.pallas_call(
        body, out_shape=[jax.ShapeDtypeStruct(a.shape, a.dtype) for a in wholes] + gathered_shapes,
        in_specs=[HBM_REF] * (n + 1), out_specs=[HBM_REF] * (n + 1), input_output_aliases={a: a for a in range(n)},
        scratch_shapes=[pltpu.SemaphoreType.DMA((n,)), pltpu.SemaphoreType.DMA((n,))] + gather_sems,
        name="rs_pair_join")(*wholes, twice)
    return outs[:n], outs[n]


def _cols_to_slabs(g):
    rows, cols = g.shape
    return g.reshape(rows, N_CHIP, cols // N_CHIP).transpose(1, 0, 2)


def _slabs_to_cols(w):
    n, rows, cols = w.shape
    return w.transpose(1, 0, 2).reshape(rows, n * cols)


def _col_window(slabs, start, stop):
    n = slabs.shape[2]
    pieces = []
    for k in range(N_CHIP):
        lo, hi = max(start, k * n), min(stop, (k + 1) * n)
        if lo < hi:
            pieces.append(slabs[k][:, lo - k * n:hi - k * n])
    return pieces[0] if len(pieces) == 1 else jnp.concatenate(pieces, axis=1)


def _slabs_from_groups(groups, n):
    slabs = []
    for k in range(N_CHIP):
        pieces, off = [], 0
        for g in groups:
            lo, hi = max(k * n, off), min((k + 1) * n, off + g.shape[0])
            if lo < hi:
                pieces.append(g[lo - off:hi - off])
            off += g.shape[0]
        slabs.append(pieces[0] if len(pieces) == 1 else jnp.concatenate(pieces, axis=0))
    return jnp.stack(slabs)


def _uq_to_padded(w_uq):
    per = w_uq.reshape(RQ, H, DN + DR)
    nope = per[:, :, :DN].reshape(RQ, H * DN)
    rope = jnp.pad(per[:, :, DN:], ((0, 0), (0, 0), (0, LANE - DR))).reshape(RQ, H * LANE)
    return jnp.concatenate([nope, rope], axis=1)


def _uq_from_padded(g):
    nope = g[:, :H * DN].reshape(RQ, H, DN)
    rope = g[:, H * DN:].reshape(RQ, H, LANE)[:, :, :DR]
    return jnp.concatenate([nope, rope], axis=2).reshape(RQ, H * (DN + DR))


def _rope_tables(positions):
    inv_freq = ROPE_THETA ** (-jnp.arange(0, DR, 2, dtype=F32) / DR)
    ang = positions.astype(F32)[:, None] * inv_freq
    cos, sin = jnp.cos(ang), jnp.sin(ang)
    return jnp.tile(cos, (1, 4)), jnp.tile(jnp.concatenate([-sin, sin], axis=1), (1, 2))


def _pair_sums(fulls, core, tag, axes, tr):
    from_sibling = _swap_halves_with_sibling(fulls, f"rs_pair_swap_{tag}", axes)
    return [_add_own_half(f, o, core, min(tr, o.shape[1]), f"add_own_half_{tag}{n}", ax)
            for n, (f, o, ax) in enumerate(zip(fulls, from_sibling, axes))]


def _local_step(x, tgt, cos_t, sin_t, ada, weights, small, tiles, place):
    ts, ts_in, ts_mla, tm_nn, tm_tn, t_attn, chunk = tiles
    w_in_shard, later_shards = weights
    norm_w, conv_b, ln_w, ln_b, q_norm_w, kv_norm_w, fnw = small
    h, mod, c_all, conv_w, (g_in,) = _adaln_norm(x, norm_w, *ada, ts, [w_in_shard])
    scale, gate = mod[:, D:2 * D], mod[:, 2 * D:3 * D]
    wa = _col_window(g_in, 0, A_COLS)
    wl = jnp.pad(_col_window(g_in, A_COLS, A_COLS + L_COLS_RAW), ((0, 0), (0, L_COLS - L_COLS_RAW)))
    wg = _col_window(g_in, A_COLS + L_COLS_RAW, IN_COLS)
    proj_a = _mm_nn(h, wa, tm_nn, D, "proj_a")
    u0, u1, za, (g_uq, g_ukv, g_co, g_ao, g_o) = _conv_fwd(proj_a, conv_w, conv_b, ln_w, ln_b, ts, chunk, later_shards)
    w_uq2, w_ukv = _uq_to_padded(_slabs_to_cols(g_uq)), _slabs_to_cols(g_ukv)
    wco, wao, wo = g_co.reshape(D, D), g_ao.reshape(D, D), g_o.reshape(D, D)
    proj_l = _mm_nn(h, wl, tm_nn, L_COLS, "proj_l")
    proj_g = _mm_nn(h, wg, tm_nn, D, "proj_g")
    qn, kvn, q, k, v = _mla_prep(proj_l, q_norm_w, kv_norm_w, w_uq2, w_ukv, cos_t, sin_t, ts_mla)
    o, lse = _attn_fwd(q, k, v, t_attn)
    dx2, dza, do, delta, dpg, lhs3, rhs3, vec_mid = _middle(za, o, proj_g, x, tgt, gate, fnw, wco, wao, wo, ts)
    g_wco, g_wao, g_wo = _mm_tn_stack(lhs3, rhs3, tm_tn, "grad_w_out3")
    dq, dk, dv = _attn_bwd(q, k, v, do, lse, delta, t_attn)
    dpl, g_wuq2, g_wukv, vec_mla = _mla_prep_bwd(
        dq, dk, dv, proj_l, qn, kvn, q_norm_w, kv_norm_w, w_uq2, w_ukv, cos_t, sin_t, ts_mla)

    core = place[1:2]
    nr = D // N_CHIP
    early = [_cols_to_slabs(_uq_from_padded(g_wuq2)).astype(BF16), _cols_to_slabs(g_wukv).astype(BF16),
             g_wco.reshape(N_CHIP, nr, D), g_wao.reshape(N_CHIP, nr, D), g_wo.reshape(N_CHIP, nr, D)]
    dpa, g_conv_w, vec_conv, early_got = _conv_bwd(dza, proj_a, u0, u1, conv_w, ln_w, ln_b, ts, chunk, early)

    g_wa_t = _mm_tn(dpa, h, tm_tn, D, D, "grad_w_in_a", BF16)
    g_wl_t = _mm_tn(dpl, h, tm_tn, L_COLS, D, "grad_w_in_l", BF16)
    g_wg_t = _mm_tn(dpg, h, tm_tn, D, D, "grad_w_in_g", BF16)
    g_w_in_slabs = _slabs_from_groups([g_wa_t, g_wl_t[0:L_COLS_RAW], g_wg_t], IN_COLS // N_CHIP)
    late_sums = _pair_sums([g_w_in_slabs], core, "b", [2], IN_COLS // N_CHIP)
    grad_x, vec_in, late_got = _input_bwd(dpa, dpl, dpg, wa, wl, wg, x, dx2, norm_w, scale, ts_in, late_sums)

    col_sums = jnp.concatenate(
        [vec_in, vec_mid, vec_conv, jnp.pad(vec_mla, ((0, 0), (0, D - RQ))), g_conv_w], axis=0)
    wholes = ([_sum_chip_slabs(late_got[0], late_sums[0], place, W_IN_ROWS, "sum_chip_slabs_w_in", 2)]
              + list(_sum_device_partials(early_got, early, place)))
    shards, all_col_sums = _join_halves_with_sibling(wholes, [2] + [1] * len(early), col_sums)

    return grad_x, shards, all_col_sums, c_all


def kernel(x, c, positions, w_ada, b_ada, norm_w, w_in, conv_w, conv_b, conv_ln_w, conv_ln_b, w_conv_out, q_norm_w, w_uq, kv_norm_w, w_ukv, w_attn_out, w_out, final_norm_w, loss_target, m_w_ada, m_b_ada, m_norm_w, m_w_in, m_conv_w, m_conv_b, m_conv_ln_w, m_conv_ln_b, m_w_conv_out, m_q_norm_w, m_w_uq, m_kv_norm_w, m_w_ukv, m_w_attn_out, m_w_out, m_final_norm_w, v_w_ada, v_b_ada, v_norm_w, v_w_in, v_conv_w, v_conv_b, v_conv_ln_w, v_conv_ln_b, v_w_conv_out, v_q_norm_w, v_w_uq, v_kv_norm_w, v_w_ukv, v_w_attn_out, v_w_out, v_final_norm_w):
    ix, iy, ic = _coords()
    chip = 2 * ix + iy
    dev = 4 * ix + 2 * iy + ic
    s = x.shape[1]
    tiles = (256, 256, 512, 1024, 2048, 512, 32)

    conv_w_pad = jnp.pad(conv_w[0], ((0, HALO - KC), (0, D - conv_w.shape[2])))
    small_in = jnp.concatenate([jnp.pad(c, ((0, 7), (0, 0))), conv_w_pad], axis=0)

    later_shards = [w[0].astype(BF16) for w in (w_uq, w_ukv, w_conv_out, w_attn_out, w_out)]
    weights = (w_in[0].astype(BF16), later_shards)

    ada_cols = w_ada.shape[2]
    b_shard = lax.dynamic_slice(b_ada, (0, chip * ada_cols), (1, ada_cols))
    ada = (small_in, w_ada[0], b_shard, dev.reshape(1).astype(jnp.int32))

    cos_t, sin_t = _rope_tables(positions[0])
    small = (norm_w, conv_b, conv_ln_w, conv_ln_b, q_norm_w, kv_norm_w, final_norm_w.reshape(1, D))
    place = jnp.stack([chip, ic]).astype(jnp.int32)
    grad_x, shards, gathered, c_all = _local_step(x[0], loss_target[0], cos_t, sin_t, ada, weights, small, tiles, place)
    g_w_in_s, g_w_uq_s, g_w_ukv_s, g_wco_s, g_wao_s, g_wo_s = shards

    vec_names = ("b_ada", "norm_w", "conv_b", "conv_ln_w", "conv_ln_b", "q_norm_w", "kv_norm_w", "final_norm_w")
    row = lambda a: a.reshape(1, -1)
    vectors = [(row(b_ada), row(m_b_ada), row(v_b_ada)), (norm_w, m_norm_w, v_norm_w), (conv_b, m_conv_b, v_conv_b),
               (conv_ln_w, m_conv_ln_w, v_conv_ln_w), (conv_ln_b, m_conv_ln_b, v_conv_ln_b),
               (q_norm_w, m_q_norm_w, v_q_norm_w), (kv_norm_w, m_kv_norm_w, v_kv_norm_w),
               (row(final_norm_w), row(m_final_norm_w), row(v_final_norm_w))]
    fin = _small_finalize(gathered, vectors, (conv_w, m_conv_w, v_conv_w), place[0:1])
    res = {}
    for p, (name, (w, _, _)) in enumerate(zip(vec_names, vectors)):
        shape = final_norm_w.shape if name == "final_norm_w" else w.shape
        res[name] = tuple(a.reshape(shape) for a in fin[4 * p:4 * p + 4])
    res["conv_w"] = tuple(fin[4 * len(vectors):4 * len(vectors) + 4])
    dmod_all, loss = fin[-2], fin[-1].reshape(())
    dmod_shard = lax.dynamic_slice(dmod_all, (0, chip * ada_cols), (N_DEV, ada_cols))
    g_w_ada = _ada_bwd(c_all.T, dmod_shard).reshape(1, D, ada_cols)

    def big(w, g, m, v, tr, name):
        d, nm, nv = _adamw(w, g, m, v, tr, name)
        return g.reshape(w.shape), d, nm, nv

    res["w_ada"] = big(w_ada, g_w_ada[0], m_w_ada, v_w_ada, 256, "adamw_w_ada")
    t_in = [a[0].T for a in (w_in, m_w_in, v_w_in)]
    d_t, nm_t, nv_t = _adamw(t_in[0], g_w_in_s, t_in[1], t_in[2], W_IN_ROWS, "adamw_w_in")
    res["w_in"] = tuple(a.T[None] for a in (g_w_in_s, d_t, nm_t, nv_t))
    small = _adamw_many(
        [w_uq, w_ukv, w_conv_out, w_attn_out, w_out], [g_w_uq_s, g_w_ukv_s, g_wco_s, g_wao_s, g_wo_s],
        [m_w_uq, m_w_ukv, m_w_conv_out, m_w_attn_out, m_w_out], [v_w_uq, v_w_ukv, v_w_conv_out, v_w_attn_out, v_w_out],
        128)
    res["w_uq"], res["w_ukv"], res["w_conv_out"], res["w_attn_out"], res["w_out"] = small

    order = ("w_ada", "b_ada", "norm_w", "w_in", "conv_w", "conv_b", "conv_ln_w", "conv_ln_b", "w_conv_out",
             "q_norm_w", "w_uq", "kv_norm_w", "w_ukv", "w_attn_out", "w_out", "final_norm_w")
    outs = [loss, grad_x[None]]
    for slot in range(4):
        outs += [res[name][slot] for name in order]
    return tuple(outs)
```

```python
import functools

import numpy as np
import jax
import jax.numpy as jnp
from jax import lax
from jax.experimental import pallas as pl
from jax.experimental.pallas import tpu as pltpu

F32 = jnp.float32
BF16 = jnp.bfloat16
MESH = pl.DeviceIdType.MESH

D = 1024
H = 8
DN = 128
DR = 64
RQ = 256
KC = 31
HALO = 32
EPS = 1e-6
ROPE_THETA = 10000.0
N_CHIP = 4
N_DEV = 8
LANE = 128
VMEM_BIG = 56 * 1024 * 1024

ADAM_LR = 0.001
ADAM_B1 = 0.9
ADAM_B2 = 0.999
ADAM_EPS = 1e-08
ADAM_WD = 0.01
ADAM_STEP = 10

A_COLS = 3 * D
L_COLS_RAW = RQ + RQ + DR
L_COLS = 640
G_COLS = 3 * D
IN_COLS = A_COLS + L_COLS_RAW + G_COLS


def _params(sem=None, vmem=None):
    kw = {}
    if sem is not None:
        kw["dimension_semantics"] = sem
    if vmem is not None:
        kw["vmem_limit_bytes"] = vmem
    return pltpu.CompilerParams(**kw)


def _dot(a, b):
    return jnp.dot(a, b, preferred_element_type=F32)


def _dot_nt(a, b):
    return lax.dot_general(a, b, (((1,), (1,)), ((), ())), preferred_element_type=F32)


def _dot_tn(a, b):
    return lax.dot_general(a, b, (((0,), (0,)), ((), ())), preferred_element_type=F32)


def _colsum(v):
    return jnp.sum(v, axis=0, keepdims=True)


def _rowmean(v):
    return jnp.mean(v, axis=-1, keepdims=True)


def _sigmoid(v):
    return jax.nn.sigmoid(v)


def _dsilu(v, s):
    return s * (1.0 + v * (1.0 - s))


def _swap_halves(v, first_half):
    return jnp.where(first_half, pltpu.roll(v, 96, 1), pltpu.roll(v, 32, 1))


def _first_half_mask(rows):
    lane = lax.broadcasted_iota(jnp.int32, (rows, LANE), 1)
    return (lane % 64) < 32


SMALL_IN_ROWS = 8 + HALO


def _adaln_norm(x, norm_w, small_in, w_ada_shard, b_ada_shard, dev, ts, shards):
    s = x.shape[0]
    cols = w_ada_shard.shape[1]
    taps = D // N_CHIP

    def modulation(dev_ref, x_ref, nw_ref, sm_ref, w_ref, b_ref, h_ref, mod_ref, c_ref, conv_ref,
                   part_sc, all_sc, small_sc, *sems):
        _allgather8_run(sm_ref, small_sc, *sems[0:3])
        for k in range(N_DEV):
            c_ref[k:k + 1, :] = small_sc[SMALL_IN_ROWS * k:SMALL_IN_ROWS * k + 1, :]
        for k in range(N_CHIP):
            base = SMALL_IN_ROWS * 2 * k + 8
            conv_ref[:, taps * k:taps * (k + 1)] = small_sc[base:base + HALO, 0:taps]
        cv = c_ref[...]
        part_sc[...] = jnp.dot(cv * _sigmoid(cv), w_ref[...], preferred_element_type=F32,
                               precision=lax.Precision.HIGHEST) + b_ref[...]
        _allgather8_run(part_sc, all_sc, *sems[3:6])
        for k in range(N_CHIP):
            mod_ref[:, cols * k:cols * (k + 1)] = all_sc[pl.ds(2 * N_DEV * k + dev_ref[0], 1), :]

    def body(dev_ref, x_ref, nw_ref, sm_ref, w_ref, b_ref, h_ref, mod_ref, *rest):
        xv = x_ref[...]
        r = lax.rsqrt(_rowmean(xv * xv) + EPS)
        y = xv * r * nw_ref[...]
        h_ref[...] = (y * (1.0 + mod_ref[:, D:2 * D]) + mod_ref[:, 0:D]).astype(BF16)

    row = pl.BlockSpec((ts, D), lambda i: (i, 0))
    const = lambda shape: pl.BlockSpec(shape, lambda i: (0, 0))
    n = len(shards)
    gathered_shapes, sems = _gather_operands(shards)
    outs = pl.pallas_call(
        _gather_alongside(body, 6, 4, n, s // ts - 1, modulation), grid=(s // ts,),
        in_specs=[pl.BlockSpec(memory_space=pltpu.SMEM), row, const((1, D)), const(small_in.shape),
                  const(w_ada_shard.shape), const((1, cols))] + [HBM_REF] * n,
        out_specs=[row, const((1, 3 * D)), const((N_DEV, D)), const((HALO, D))] + [HBM_REF] * n,
        out_shape=[jax.ShapeDtypeStruct((s, D), BF16), jax.ShapeDtypeStruct((1, 3 * D), F32),
                   jax.ShapeDtypeStruct((N_DEV, D), F32), jax.ShapeDtypeStruct((HALO, D), F32)] + gathered_shapes,
        scratch_shapes=[pltpu.VMEM((N_DEV, cols), F32), pltpu.VMEM((N_DEV * N_DEV, cols), F32),
                        pltpu.VMEM((N_DEV * SMALL_IN_ROWS, D), F32)] + ALLGATHER8_SEMS + ALLGATHER8_SEMS + sems,
        name="adaln_norm", compiler_params=_params(("arbitrary",), VMEM_BIG))(
            dev, x, norm_w, small_in, w_ada_shard, b_ada_shard, *shards)
    return outs[0], outs[1], outs[2], outs[3], _as_chip_slabs(outs[4:], shards)


def _mm_nn(a, b, tm, tn, name):
    m, k = a.shape
    n = b.shape[1]

    def body(a_ref, b_ref, o_ref):
        o_ref[...] = _dot(a_ref[...], b_ref[...])

    return pl.pallas_call(
        body, grid=(n // tn, m // tm),
        in_specs=[pl.BlockSpec((tm, k), lambda j, i: (i, 0)), pl.BlockSpec((k, tn), lambda j, i: (0, j))],
        out_specs=pl.BlockSpec((tm, tn), lambda j, i: (i, j)),
        out_shape=jax.ShapeDtypeStruct((m, n), F32), name=name,
        compiler_params=_params(("parallel", "parallel"), VMEM_BIG))(a, b)


def _mm_tn(a, b, tm, tk, tn, name, out_dtype=F32):
    m, k = a.shape
    n = b.shape[1]
    steps = m // tm

    def body(a_ref, b_ref, o_ref, acc_ref):
        @pl.when(pl.program_id(2) == 0)
        def _():
            acc_ref[...] = jnp.zeros_like(acc_ref)
        acc_ref[...] += _dot_tn(a_ref[...], b_ref[...])

        @pl.when(pl.program_id(2) == steps - 1)
        def _():
            o_ref[...] = acc_ref[...].astype(out_dtype)

    return pl.pallas_call(
        body, grid=(k // tk, n // tn, steps),
        in_specs=[pl.BlockSpec((tm, tk), lambda r, j, i: (i, r)), pl.BlockSpec((tm, tn), lambda r, j, i: (i, j))],
        out_specs=pl.BlockSpec((tk, tn), lambda r, j, i: (r, j)),
        out_shape=jax.ShapeDtypeStruct((k, n), out_dtype), scratch_shapes=[pltpu.VMEM((tk, tn), F32)], name=name,
        compiler_params=_params(("parallel", "parallel", "arbitrary"), VMEM_BIG))(a, b)


def _mm_tn_stack(a, b, tm, name):
    n_stack, m, k = a.shape
    n = b.shape[2]
    steps = m // tm

    def body(a_ref, b_ref, o_ref, acc_ref):
        @pl.when(pl.program_id(1) == 0)
        def _():
            acc_ref[...] = jnp.zeros_like(acc_ref)
        acc_ref[...] += _dot_tn(a_ref[0], b_ref[0])

        @pl.when(pl.program_id(1) == steps - 1)
        def _():
            o_ref[0] = acc_ref[...].astype(BF16)

    out = pl.pallas_call(
        body, grid=(n_stack, steps),
        in_specs=[pl.BlockSpec((1, tm, k), lambda g, i: (g, i, 0)), pl.BlockSpec((1, tm, n), lambda g, i: (g, i, 0))],
        out_specs=pl.BlockSpec((1, k, n), lambda g, i: (g, 0, 0)),
        out_shape=jax.ShapeDtypeStruct((n_stack, k, n), BF16), scratch_shapes=[pltpu.VMEM((k, n), F32)], name=name,
        compiler_params=_params(("parallel", "arbitrary"), VMEM_BIG))(a, b)
    return [out[g] for g in range(n_stack)]


def _coords():
    return lax.axis_index("x"), lax.axis_index("y"), lax.axis_index("c")


HBM_REF = pl.BlockSpec(memory_space=pl.ANY)


def _chip_scatter_copies(p_refs, got_refs, send_sems, recv_sems):
    x, y, c = _coords()
    copies = []
    for a in range(len(p_refs)):
        for j, (px, py) in enumerate([(1 - x, y), (x, 1 - y), (1 - x, 1 - y)]):
            copies.append(pltpu.make_async_remote_copy(
                src_ref=p_refs[a].at[2 * px + py], dst_ref=got_refs[a].at[j], send_sem=send_sems.at[3 * a + j],
                recv_sem=recv_sems.at[3 * a + j], device_id=(px, py, c), device_id_type=MESH))
    return copies


RELATIONS = [(dx, dy, dc) for dx in (0, 1) for dy in (0, 1) for dc in (0, 1)][1:]


def _device_scatter_copies(p_refs, got_refs, send_sems, recv_sems):
    x, y, c = _coords()
    copies = []
    for a in range(len(p_refs)):
        half = p_refs[a].shape[1] // 2
        for j, (dx, dy, dc) in enumerate(RELATIONS):
            px, py, pc = (1 - x if dx else x), (1 - y if dy else y), (1 - c if dc else c)
            src = p_refs[a].at[2 * px + py, pl.ds(pl.multiple_of(pc * half, 16), half), :]
            copies.append(pltpu.make_async_remote_copy(
                src_ref=src, dst_ref=got_refs[a].at[j], send_sem=send_sems.at[7 * a + j],
                recv_sem=recv_sems.at[7 * a + j], device_id=(px, py, pc), device_id_type=MESH))
    return copies


def _scatter_alongside(body, n_in, n_out, n_parts, last_step, make_copies):
    def wrapped(*refs):
        ins, parts = refs[:n_in], refs[n_in:n_in + n_parts]
        rest = refs[n_in + n_parts:]
        outs, got = rest[:n_out], rest[n_out:n_out + n_parts]
        scratch, (send_sems, recv_sems) = rest[n_out + n_parts:-2], rest[-2:]

        @pl.when(pl.program_id(0) == 0)
        def _():
            for cp in make_copies(parts, got, send_sems, recv_sems):
                cp.start()

        body(*ins, *outs, *scratch)

        @pl.when(pl.program_id(0) == last_step)
        def _():
            for cp in make_copies(parts, got, send_sems, recv_sems):
                cp.wait()

    return wrapped


def _scatter_operands(parts, per_device):
    n = len(parts)
    if per_device:
        slots, shapes = 7, [jax.ShapeDtypeStruct((7, a.shape[1] // 2, a.shape[2]), a.dtype) for a in parts]
    else:
        slots, shapes = 3, [jax.ShapeDtypeStruct((3,) + a.shape[1:], a.dtype) for a in parts]
    sems = [pltpu.SemaphoreType.DMA((slots * n,)), pltpu.SemaphoreType.DMA((slots * n,))]
    return [HBM_REF] * n, [HBM_REF] * n, shapes, sems


def _shifted_copies(win_ref, sh_ref, rows):
    for p in range(1, 8):
        sh_ref[p - 1, 0:rows, :] = win_ref[pl.ds(p, rows), :]


def _tap_rows(win_ref, sh_ref, start, rows):
    p = start % 8
    if p == 0:
        return win_ref[pl.ds(start, rows), :]
    return sh_ref[p - 1, pl.ds(start - p, rows), :]


def _conv_taps(win_ref, sh_ref, w_ref, rows, chunk, offset_of_tap):
    pieces = []
    for c0 in range(0, rows, chunk):
        acc = None
        for j in range(KC):
            term = w_ref[j:j + 1, :] * _tap_rows(win_ref, sh_ref, c0 + offset_of_tap(j), chunk)
            acc = term if acc is None else acc + term
        pieces.append(acc)
    return pieces


def _conv_fwd(proj_a, conv_w, conv_b, ln_w, ln_b, ts, chunk, shards):
    s = proj_a.shape[0]

    def body(av_ref, al_ref, ag_ref, w_ref, b_ref, lw_ref, lb_ref, u0_ref, u1_ref, za_ref, win_ref, sh_ref):
        @pl.when(pl.program_id(0) == 0)
        def _():
            win_ref[0:HALO, :] = jnp.zeros((HALO, D), F32)

        u0 = av_ref[...] * _sigmoid(al_ref[...])
        u0_ref[...] = u0
        win_ref[HALO:HALO + ts, :] = u0
        _shifted_copies(win_ref, sh_ref, ts + HALO - 8)
        pieces = _conv_taps(win_ref, sh_ref, w_ref, ts, chunk, lambda j: HALO - (KC - 1) + j)
        for n, acc in enumerate(pieces):
            u1_ref[n * chunk:(n + 1) * chunk, :] = acc + b_ref[...]
        win_ref[0:HALO, :] = win_ref[ts:ts + HALO, :]

        u1 = u1_ref[...]
        xc = u1 - _rowmean(u1)
        rstd = lax.rsqrt(_rowmean(xc * xc) + EPS)
        u2 = xc * rstd * lw_ref[...] + lb_ref[...]
        u3 = u2 * _sigmoid(u2)
        ag = ag_ref[...]
        za_ref[...] = (u3 * (ag * _sigmoid(ag))).astype(BF16)

    col = lambda c: pl.BlockSpec((ts, D), lambda i, c=c: (i, c))
    row = pl.BlockSpec((ts, D), lambda i: (i, 0))
    vec = pl.BlockSpec((1, D), lambda i: (0, 0))
    n = len(shards)
    gathered_shapes, sems = _gather_operands(shards)
    outs = pl.pallas_call(
        _gather_alongside(body, 7, 3, n, s // ts - 1), grid=(s // ts,),
        in_specs=[col(0), col(1), col(2), pl.BlockSpec((HALO, D), lambda i: (0, 0)), vec, vec, vec] + [HBM_REF] * n,
        out_specs=[row, row, row] + [HBM_REF] * n,
        out_shape=[jax.ShapeDtypeStruct((s, D), F32), jax.ShapeDtypeStruct((s, D), F32),
                   jax.ShapeDtypeStruct((s, D), BF16)] + gathered_shapes,
        scratch_shapes=[pltpu.VMEM((ts + HALO, D), F32), pltpu.VMEM((7, ts + HALO, D), F32)] + sems,
        name="conv_fwd", compiler_params=_params(("arbitrary",), VMEM_BIG))(
            proj_a, proj_a, proj_a, conv_w, conv_b, ln_w, ln_b, *shards)
    return outs[0], outs[1], outs[2], _as_chip_slabs(outs[3:], shards)


def _conv_bwd(dza, proj_a, u0, u1, conv_w, ln_w, ln_b, ts, chunk, parts):
    s = dza.shape[0]
    nt = s // ts
    per = ts // HALO

    def body(dza_ref, av_ref, al_ref, ag_ref, u0_ref, u0p_ref, u1_ref, w_ref, lw_ref, lb_ref,
             dpa_ref, gw_ref, gv_ref, dwin_ref, uwin_ref, du0_ref, gwp_ref, dsh_ref, ush_ref):
        step = pl.program_id(0)
        tile = nt - 1 - step

        @pl.when(step == 0)
        def _():
            dwin_ref[ts:ts + HALO, :] = jnp.zeros((HALO, D), F32)
            gwp_ref[...] = jnp.zeros_like(gwp_ref)
            gv_ref[...] = jnp.zeros_like(gv_ref)

        ag = ag_ref[...]
        sg = _sigmoid(ag)
        u1 = u1_ref[...]
        xc = u1 - _rowmean(u1)
        rstd = lax.rsqrt(_rowmean(xc * xc) + EPS)
        xh = xc * rstd
        u2 = xh * lw_ref[...] + lb_ref[...]
        s2 = _sigmoid(u2)
        dz = dza_ref[...]
        du3 = dz * (ag * sg)
        dpa_ref[:, 2 * D:3 * D] = (dz * (u2 * s2) * _dsilu(ag, sg)).astype(BF16)
        du2 = du3 * _dsilu(u2, s2)
        gv_ref[0:1, :] += _colsum(du2 * xh)
        gv_ref[1:2, :] += _colsum(du2)
        dxh = du2 * lw_ref[...]
        du1 = rstd * (dxh - _rowmean(dxh) - xh * _rowmean(dxh * xh))
        gv_ref[2:3, :] += _colsum(du1)
        dwin_ref[0:ts, :] = du1

        uwin_ref[0:HALO, :] = jnp.where(tile == 0, 0.0, u0p_ref[...])
        uwin_ref[HALO:HALO + ts, :] = u0_ref[...]

        _shifted_copies(dwin_ref, dsh_ref, ts + HALO - 8)
        _shifted_copies(uwin_ref, ush_ref, ts + HALO - 8)
        pieces = _conv_taps(dwin_ref, dsh_ref, w_ref, ts, chunk, lambda j: (KC - 1) - j)
        for n, acc in enumerate(pieces):
            du0_ref[n * chunk:(n + 1) * chunk, :] = acc
        for c0 in range(0, ts, chunk):
            dchunk = dwin_ref[c0:c0 + chunk, :]
            for j in range(KC):
                prod = dchunk * _tap_rows(uwin_ref, ush_ref, c0 + HALO - (KC - 1) + j, chunk)
                gwp_ref[8 * j:8 * j + 8, :] += jnp.sum(prod.reshape(chunk // 8, 8, D), axis=0)
        dwin_ref[ts:ts + HALO, :] = dwin_ref[0:HALO, :]

        du0 = du0_ref[...]
        al = al_ref[...]
        sl = _sigmoid(al)
        dpa_ref[:, 0:D] = (du0 * sl).astype(BF16)
        dpa_ref[:, D:2 * D] = (du0 * av_ref[...] * sl * (1.0 - sl)).astype(BF16)

        @pl.when(step == nt - 1)
        def _():
            for j in range(KC):
                gw_ref[j:j + 1, :] = _colsum(gwp_ref[8 * j:8 * j + 8, :])
            gw_ref[KC:HALO, :] = jnp.zeros((HALO - KC, D), F32)

    rev = lambda i: nt - 1 - i
    col = lambda c: pl.BlockSpec((ts, D), lambda i, c=c: (rev(i), c))
    row = pl.BlockSpec((ts, D), lambda i: (rev(i), 0))
    vec = pl.BlockSpec((1, D), lambda i: (0, 0))
    halo = pl.BlockSpec((HALO, D), lambda i: (jnp.maximum(rev(i) * per - 1, 0), 0))
    side_in, side_out, side_shapes, side_sems = _scatter_operands(parts, True)
    outs = pl.pallas_call(
        _scatter_alongside(body, 10, 3, len(parts), nt - 1, _device_scatter_copies), grid=(nt,),
        in_specs=[row, col(0), col(1), col(2), row, halo, row, pl.BlockSpec((HALO, D), lambda i: (0, 0)), vec, vec]
        + side_in,
        out_specs=[pl.BlockSpec((ts, A_COLS), lambda i: (rev(i), 0)),
                   pl.BlockSpec((HALO, D), lambda i: (0, 0)), pl.BlockSpec((8, D), lambda i: (0, 0))] + side_out,
        out_shape=[jax.ShapeDtypeStruct((s, A_COLS), BF16), jax.ShapeDtypeStruct((HALO, D), F32),
                   jax.ShapeDtypeStruct((8, D), F32)] + side_shapes,
        scratch_shapes=[pltpu.VMEM((ts + HALO, D), F32), pltpu.VMEM((ts + HALO, D), F32),
                        pltpu.VMEM((ts, D), F32), pltpu.VMEM((8 * HALO, D), F32),
                        pltpu.VMEM((7, ts + HALO, D), F32), pltpu.VMEM((7, ts + HALO, D), F32)] + side_sems,
        name="conv_bwd", compiler_params=_params(("arbitrary",), VMEM_BIG))(
            dza, proj_a, proj_a, proj_a, u0, u0, u1, conv_w, ln_w, ln_b, *parts)
    return outs[0], outs[1], outs[2], list(outs[3:])


def _mla_prep(proj_l, q_norm_w, kv_norm_w, w_uq2, w_ukv, cos_t, sin_t, ts):
    s = proj_l.shape[0]

    def body(pl_ref, qw_ref, kw_ref, wq_ref, wkv_ref, c_ref, s_ref, qn_ref, kvn_ref, q_ref, k_ref, v_ref):
        first = _first_half_mask(ts)
        cs = c_ref[...]
        sn = s_ref[...]

        def rms(v, w):
            return v * lax.rsqrt(_rowmean(v * v) + EPS) * w

        def rope(v):
            return v * cs + _swap_halves(v, first) * sn

        qn = rms(pl_ref[:, 0:RQ], qw_ref[...]).astype(BF16)
        kvn = rms(pl_ref[:, RQ:2 * RQ], kw_ref[...]).astype(BF16)
        qn_ref[...] = qn
        kvn_ref[...] = kvn
        q = _dot(qn, wq_ref[...])
        kv = _dot(kvn, wkv_ref[...])
        kr = rope(pl_ref[:, 2 * RQ:2 * RQ + LANE]).astype(BF16)
        for h in range(H):
            q_ref[h, :, 0:DN] = q[:, DN * h:DN * (h + 1)].astype(BF16)
            q_ref[h, :, DN:2 * DN] = rope(q[:, H * DN + LANE * h:H * DN + LANE * (h + 1)]).astype(BF16)
            k_ref[h, :, 0:DN] = kv[:, 2 * DN * h:2 * DN * h + DN].astype(BF16)
            k_ref[h, :, DN:2 * DN] = kr
            v_ref[h, :, 0:DN] = kv[:, 2 * DN * h + DN:2 * DN * (h + 1)].astype(BF16)
            v_ref[h, :, DN:2 * DN] = jnp.ones((ts, DN), BF16)

    const = lambda shape: pl.BlockSpec(shape, lambda i: (0,) * len(shape))
    rowb = lambda w: pl.BlockSpec((ts, w), lambda i: (i, 0))
    head = lambda w: pl.BlockSpec((H, ts, w), lambda i: (0, i, 0))
    return pl.pallas_call(
        body, grid=(s // ts,),
        in_specs=[rowb(L_COLS), const((1, RQ)), const((1, RQ)), const((RQ, 2 * H * DN)), const((RQ, 2 * H * DN)),
                  rowb(LANE), rowb(LANE)],
        out_specs=[rowb(RQ), rowb(RQ), head(2 * DN), head(2 * DN), head(2 * DN)],
        out_shape=[jax.ShapeDtypeStruct((s, RQ), BF16), jax.ShapeDtypeStruct((s, RQ), BF16),
                   jax.ShapeDtypeStruct((H, s, 2 * DN), BF16), jax.ShapeDtypeStruct((H, s, 2 * DN), BF16),
                   jax.ShapeDtypeStruct((H, s, 2 * DN), BF16)],
        name="mla_prep", compiler_params=_params(("parallel",), VMEM_BIG))(
            proj_l, q_norm_w, kv_norm_w, w_uq2, w_ukv, cos_t, sin_t)


def _mla_prep_bwd(dq, dk, dv, proj_l, qn, kvn, q_norm_w, kv_norm_w, w_uq2, w_ukv, cos_t, sin_t, ts):
    s = proj_l.shape[0]

    def body(dq_ref, dk_ref, dv_ref, pl_ref, qn_ref, kvn_ref, qw_ref, kw_ref, wq_ref, wkv_ref, c_ref, s_ref,
             dpl_ref, gwq_ref, gwkv_ref, gv_ref, dq2_ref, dkv2_ref):
        @pl.when(pl.program_id(0) == 0)
        def _():
            gwq_ref[...] = jnp.zeros_like(gwq_ref)
            gwkv_ref[...] = jnp.zeros_like(gwkv_ref)
            gv_ref[...] = jnp.zeros_like(gv_ref)

        first = _first_half_mask(ts)
        cs = c_ref[...] * ATT_SCALE
        sn = s_ref[...] * ATT_SCALE

        def rope_bwd(g):
            return g * cs + _swap_halves(g * sn, first)

        def rms_bwd(v, w, dy):
            r = lax.rsqrt(_rowmean(v * v) + EPS)
            vh = v * r
            dvh = dy * w
            return r * (dvh - vh * _rowmean(dvh * vh)), _colsum(dy * vh)

        dkr = None
        for h in range(H):
            dq2_ref[:, DN * h:DN * (h + 1)] = (dq_ref[h, :, 0:DN] * ATT_SCALE).astype(BF16)
            dq2_ref[:, H * DN + LANE * h:H * DN + LANE * (h + 1)] = rope_bwd(dq_ref[h, :, DN:2 * DN]).astype(BF16)
            dkv2_ref[:, 2 * DN * h:2 * DN * h + DN] = (dk_ref[h, :, 0:DN] * ATT_SCALE).astype(BF16)
            dkv2_ref[:, 2 * DN * h + DN:2 * DN * (h + 1)] = dv_ref[h].astype(BF16)
            part = dk_ref[h, :, DN:2 * DN]
            dkr = part if dkr is None else dkr + part

        dq2 = dq2_ref[...]
        dkv2 = dkv2_ref[...]
        gwq_ref[...] += _dot_tn(qn_ref[...], dq2)
        gwkv_ref[...] += _dot_tn(kvn_ref[...], dkv2)
        dcq, gq = rms_bwd(pl_ref[:, 0:RQ], qw_ref[...], _dot_nt(dq2, wq_ref[...]))
        dckv, gkv = rms_bwd(pl_ref[:, RQ:2 * RQ], kw_ref[...], _dot_nt(dkv2, wkv_ref[...]))
        gv_ref[0:1, :] += gq
        gv_ref[1:2, :] += gkv
        dpl_ref[:, 0:RQ] = dcq.astype(BF16)
        dpl_ref[:, RQ:2 * RQ] = dckv.astype(BF16)
        dpl_ref[:, 2 * RQ:2 * RQ + LANE] = rope_bwd(dkr).astype(BF16)

    const = lambda shape: pl.BlockSpec(shape, lambda i: (0,) * len(shape))
    rowb = lambda w: pl.BlockSpec((ts, w), lambda i: (i, 0))
    head = lambda w: pl.BlockSpec((H, ts, w), lambda i: (0, i, 0))
    return pl.pallas_call(
        body, grid=(s // ts,),
        in_specs=[head(2 * DN), head(2 * DN), head(DN), rowb(L_COLS), rowb(RQ), rowb(RQ), const((1, RQ)),
                  const((1, RQ)), const((RQ, 2 * H * DN)), const((RQ, 2 * H * DN)), rowb(LANE), rowb(LANE)],
        out_specs=[rowb(L_COLS), const((RQ, 2 * H * DN)), const((RQ, 2 * H * DN)), const((8, RQ))],
        out_shape=[jax.ShapeDtypeStruct((s, L_COLS), BF16), jax.ShapeDtypeStruct((RQ, 2 * H * DN), F32),
                   jax.ShapeDtypeStruct((RQ, 2 * H * DN), F32), jax.ShapeDtypeStruct((8, RQ), F32)],
        scratch_shapes=[pltpu.VMEM((ts, 2 * H * DN), BF16), pltpu.VMEM((ts, 2 * H * DN), BF16)],
        name="mla_prep_bwd", compiler_params=_params(("arbitrary",), VMEM_BIG))(
            dq, dk, dv, proj_l, qn, kvn, q_norm_w, kv_norm_w, w_uq2, w_ukv, cos_t, sin_t)


def _causal_pairs(n, by_key):
    if by_key:
        pairs = [(i, j) for j in range(n) for i in range(j, n)]
    else:
        pairs = [(i, j) for i in range(n) for j in range(i + 1)]
    return (jnp.asarray(np.array([p[0] for p in pairs], np.int32)),
            jnp.asarray(np.array([p[1] for p in pairs], np.int32)))


ATT_SCALE = float((DN + DR) ** -0.5)
LOG2E = 1.4426950408889634
LN2 = 0.6931471805599453
ATT_HEADS_FWD = 8
ATT_HEADS = 4
W_IN_ROWS = 336
ATT_ROWS = 16


def _diag_width(r0, t):
    return min(t, -(-(r0 + ATT_ROWS) // LANE) * LANE)


def _diag_mask_rows(r0, width):
    rows = r0 + lax.broadcasted_iota(jnp.int32, (ATT_ROWS, width), 0)
    cols = lax.broadcasted_iota(jnp.int32, (ATT_ROWS, width), 1)
    return cols <= rows


def _diag_mask(t):
    return lax.broadcasted_iota(jnp.int32, (t, t), 1) <= lax.broadcasted_iota(jnp.int32, (t, t), 0)


def _attn_fwd(q, k, v, t):
    s = q.shape[1]
    n = s // t
    scale2 = float((DN + DR) ** -0.5) * LOG2E
    qi, ki = _causal_pairs(n, by_key=False)

    def body(qi_ref, ki_ref, q_ref, k_ref, v_ref, o_ref, lse_ref, *scratch):
        per_head = [scratch[5 * h:5 * h + 5] for h in range(ATT_HEADS_FWD)]
        p = pl.program_id(1)
        i = qi_ref[p]
        j = ki_ref[p]

        @pl.when(j == 0)
        def _():
            for m_sc, acc_sc, _, _, _ in per_head:
                m_sc[...] = jnp.full_like(m_sc, -jnp.inf)
                acc_sc[...] = jnp.zeros_like(acc_sc)

        def scores(h, diag):
            sc = _dot_nt(q_ref[h], k_ref[h])
            if diag:
                sc = jnp.where(_diag_mask(t), sc, -jnp.inf)
            per_head[h][2][...] = sc

        def rowmax(h, rows):
            per_head[h][4][rows, :] = jnp.max(per_head[h][2][rows, :], axis=-1, keepdims=True)

        def stats(h):
            m_sc, acc_sc, _, _, mx_sc = per_head[h]
            m_prev = m_sc[...]
            m_new = jnp.maximum(m_prev, mx_sc[...] * scale2)
            m_sc[...] = m_new
            acc_sc[...] = jnp.exp2(m_prev - m_new) * acc_sc[...]

        def probs(h, rows):
            m_sc, _, s_sc, p_sc, _ = per_head[h]
            p_sc[rows, :] = jnp.exp2(s_sc[rows, :] * scale2 - m_sc[rows, :]).astype(BF16)

        def values(h):
            _, acc_sc, _, p_sc, _ = per_head[h]
            acc_sc[...] += _dot(p_sc[...], v_ref[h])

        def step(diag):
            blocks = [slice(r0, r0 + ATT_ROWS) for r0 in range(0, t, ATT_ROWS)]
            for h in range(ATT_HEADS_FWD):
                scores(h, diag)
            for rows in blocks:
                rowmax(0, rows)
            stats(0)
            for h in range(ATT_HEADS_FWD):
                for rows in blocks:
                    probs(h, rows)
                    if h + 1 < ATT_HEADS_FWD:
                        rowmax(h + 1, rows)
                if h + 1 < ATT_HEADS_FWD:
                    stats(h + 1)
                values(h)

        @pl.when(j < i)
        def _():
            step(False)

        @pl.when(j == i)
        def _():
            step(True)
            for h, (m_sc, acc_sc, _, _, _) in enumerate(per_head):
                l = acc_sc[:, DN:2 * DN]
                o_ref[:, DN * h:DN * (h + 1)] = acc_sc[:, 0:DN] / l
                lse_ref[h] = (m_sc[...] + jnp.log2(l[:, 0:1])) * LN2

    hb = ATT_HEADS_FWD
    grid_spec = pltpu.PrefetchScalarGridSpec(
        num_scalar_prefetch=2, grid=(H // hb, int(qi.shape[0])),
        in_specs=[pl.BlockSpec((hb, t, 2 * DN), lambda h, p, qi, ki: (h, qi[p], 0)),
                  pl.BlockSpec((hb, t, 2 * DN), lambda h, p, qi, ki: (h, ki[p], 0)),
                  pl.BlockSpec((hb, t, 2 * DN), lambda h, p, qi, ki: (h, ki[p], 0))],
        out_specs=[pl.BlockSpec((t, hb * DN), lambda h, p, qi, ki: (qi[p], h)),
                   pl.BlockSpec((hb, t, 1), lambda h, p, qi, ki: (h, qi[p], 0))],
        scratch_shapes=[pltpu.VMEM((t, 1), F32), pltpu.VMEM((t, 2 * DN), F32), pltpu.VMEM((t, t), F32),
                        pltpu.VMEM((t, t), BF16), pltpu.VMEM((t, 1), F32)] * hb)
    return pl.pallas_call(
        body, grid_spec=grid_spec,
        out_shape=[jax.ShapeDtypeStruct((s, H * DN), F32), jax.ShapeDtypeStruct((H, s, 1), F32)],
        name="attn_fwd", compiler_params=_params(("parallel", "arbitrary"), VMEM_BIG))(qi, ki, q, k, v)


def _attn_bwd(q, k, v, do, lse, delta, t):
    s = q.shape[1]
    n = s // t
    scale = ATT_SCALE
    qi, ki = _causal_pairs(n, by_key=True)

    def body(qi_ref, ki_ref, q_ref, k_ref, v_ref, do_ref, lse_ref, dl_ref, dq_ref, dk_ref, dv_ref,
             dk_sc, dv_sc, s_sc, dp_sc, p_sc, ds_sc):
        p = pl.program_id(1)
        i = qi_ref[p]
        j = ki_ref[p]

        @pl.when(p == 0)
        def _():
            dq_ref[...] = jnp.zeros_like(dq_ref)

        @pl.when(i == j)
        def _():
            dk_sc[...] = jnp.zeros_like(dk_sc)
            dv_sc[...] = jnp.zeros_like(dv_sc)

        def step(diag):
            for h in range(ATT_HEADS):
                s_sc[h] = _dot_nt(q_ref[h], k_ref[h])
                dp_sc[h] = _dot_nt(do_ref[:, DN * h:DN * (h + 1)], v_ref[h, :, 0:DN])
            for h in range(ATT_HEADS):
                for r0 in range(0, t, ATT_ROWS):
                    rows = slice(r0, r0 + ATT_ROWS)
                    width = _diag_width(r0, t) if diag else t
                    sc = s_sc[h, rows, 0:width] * (scale * LOG2E)
                    if diag:
                        sc = jnp.where(_diag_mask_rows(r0, width), sc, -jnp.inf)
                    pr = jnp.exp2(sc - lse_ref[h, rows, :] * LOG2E)
                    ds = pr * (dp_sc[h, rows, 0:width] - dl_ref[h, rows, :])
                    p_sc[h, rows, 0:width] = pr.astype(BF16)
                    ds_sc[h, rows, 0:width] = ds.astype(BF16)
                    if width < t:
                        p_sc[h, rows, width:t] = jnp.zeros((ATT_ROWS, t - width), BF16)
                        ds_sc[h, rows, width:t] = jnp.zeros((ATT_ROWS, t - width), BF16)
            q_rows = pl.ds(pl.multiple_of(i * t, t), t)
            for h in range(ATT_HEADS):
                dv_sc[h] += _dot_tn(p_sc[h], do_ref[:, DN * h:DN * (h + 1)])
                dk_sc[h] += _dot_tn(ds_sc[h], q_ref[h])
                dq_ref[h, q_rows, :] += _dot(ds_sc[h], k_ref[h])

        @pl.when(i > j)
        def _():
            step(False)

        @pl.when(i == j)
        def _():
            step(True)

        @pl.when(i == n - 1)
        def _():
            dk_ref[...] = dk_sc[...]
            dv_ref[...] = dv_sc[...]

    hb = ATT_HEADS
    grid_spec = pltpu.PrefetchScalarGridSpec(
        num_scalar_prefetch=2, grid=(H // hb, int(qi.shape[0])),
        in_specs=[pl.BlockSpec((hb, t, 2 * DN), lambda h, p, qi, ki: (h, qi[p], 0)),
                  pl.BlockSpec((hb, t, 2 * DN), lambda h, p, qi, ki: (h, ki[p], 0)),
                  pl.BlockSpec((hb, t, 2 * DN), lambda h, p, qi, ki: (h, ki[p], 0)),
                  pl.BlockSpec((t, hb * DN), lambda h, p, qi, ki: (qi[p], h)),
                  pl.BlockSpec((hb, t, 1), lambda h, p, qi, ki: (h, qi[p], 0)),
                  pl.BlockSpec((hb, t, 1), lambda h, p, qi, ki: (h, qi[p], 0))],
        out_specs=[pl.BlockSpec((hb, s, 2 * DN), lambda h, p, qi, ki: (h, 0, 0), pipeline_mode=pl.Buffered(1)),
                   pl.BlockSpec((hb, t, 2 * DN), lambda h, p, qi, ki: (h, ki[p], 0)),
                   pl.BlockSpec((hb, t, DN), lambda h, p, qi, ki: (h, ki[p], 0))],
        scratch_shapes=[pltpu.VMEM((hb, t, 2 * DN), F32), pltpu.VMEM((hb, t, DN), F32),
                        pltpu.VMEM((hb, t, t), F32), pltpu.VMEM((hb, t, t), F32),
                        pltpu.VMEM((hb, t, t), BF16), pltpu.VMEM((hb, t, t), BF16)])
    return pl.pallas_call(
        body, grid_spec=grid_spec,
        out_shape=[jax.ShapeDtypeStruct((H, s, 2 * DN), F32), jax.ShapeDtypeStruct((H, s, 2 * DN), F32),
                   jax.ShapeDtypeStruct((H, s, DN), F32)],
        name="attn_bwd", compiler_params=_params(("parallel", "arbitrary"), VMEM_BIG))(
            qi, ki, q, k, v, do, lse, delta)


def _middle(za, o, proj_g, x, tgt, gate, fnw, wco, wao, wo, ts):
    s = x.shape[0]
    inv_d = 1.0 / D

    def body(za_ref, o_ref, bg_ref, ga_ref, gb_ref, x_ref, t_ref, gate_ref, fnw_ref, wco_ref, wao_ref, wo_ref,
             dx2_ref, dza_ref, do_ref, dl_ref, dpg_ref, lhs_ref, rhs_ref, vec_ref):
        @pl.when(pl.program_id(0) == 0)
        def _():
            vec_ref[...] = jnp.zeros_like(vec_ref)

        ov = o_ref[...]
        bg = bg_ref[...]
        sb = _sigmoid(bg)
        silu_b = bg * sb
        zb = (ov * silu_b).astype(BF16)
        lhs_ref[0] = za_ref[...]
        lhs_ref[1] = zb
        ya = _dot(za_ref[...], wco_ref[...])
        yb = _dot(zb, wao_ref[...])
        sa = _sigmoid(ga_ref[...])
        sg = _sigmoid(gb_ref[...])
        mg = (sa * ya + sg * yb).astype(BF16)
        lhs_ref[2] = mg
        mo = _dot(mg, wo_ref[...])
        gate_v = gate_ref[...]
        x2 = x_ref[...] + gate_v * mo
        r = lax.rsqrt(_rowmean(x2 * x2) + EPS)
        xh = x2 * r
        fw = fnw_ref[...]
        e = xh * fw - t_ref[...]
        vec_ref[2:3, :] += _colsum(e * e)
        dy = e * inv_d
        vec_ref[0:1, :] += _colsum(dy * xh)
        dxh = dy * fw
        dx2 = r * (dxh - xh * _rowmean(dxh * xh))
        dx2_ref[...] = dx2
        vec_ref[1:2, :] += _colsum(dx2 * mo)
        dmo = (gate_v * dx2).astype(BF16)
        rhs_ref[2] = dmo
        dmg = _dot_nt(dmo, wo_ref[...])
        dya = (sa * dmg).astype(BF16)
        dyb = (sg * dmg).astype(BF16)
        rhs_ref[0] = dya
        rhs_ref[1] = dyb
        dpg_ref[:, D:2 * D] = (dmg * ya * (sa * (1.0 - sa))).astype(BF16)
        dpg_ref[:, 2 * D:3 * D] = (dmg * yb * (sg * (1.0 - sg))).astype(BF16)
        dza_ref[...] = _dot_nt(dya, wco_ref[...])
        dzb = _dot_nt(dyb, wao_ref[...])
        dov = dzb * silu_b
        do_ref[...] = dov.astype(BF16)
        dpg_ref[:, 0:D] = (dzb * ov * _dsilu(bg, sb)).astype(BF16)
        dprod = dov * ov
        for h in range(H):
            dl_ref[h] = jnp.sum(dprod[:, DN * h:DN * (h + 1)], axis=-1, keepdims=True)

    col = lambda c: pl.BlockSpec((ts, D), lambda i, c=c: (i, c))
    row = pl.BlockSpec((ts, D), lambda i: (i, 0))
    vec = pl.BlockSpec((1, D), lambda i: (0, 0))
    wsp = pl.BlockSpec((D, D), lambda i: (0, 0))
    stack = pl.BlockSpec((3, ts, D), lambda i: (0, i, 0))
    bf = jax.ShapeDtypeStruct((s, D), BF16)
    ff = jax.ShapeDtypeStruct((s, D), F32)
    return pl.pallas_call(
        body, grid=(s // ts,),
        in_specs=[row, row, col(0), col(1), col(2), row, row, vec, vec, wsp, wsp, wsp],
        out_specs=[row, row, row, pl.BlockSpec((H, ts, 1), lambda i: (0, i, 0)),
                   pl.BlockSpec((ts, G_COLS), lambda i: (i, 0)), stack, stack,
                   pl.BlockSpec((8, D), lambda i: (0, 0))],
        out_shape=[ff, ff, bf, jax.ShapeDtypeStruct((H, s, 1), F32), jax.ShapeDtypeStruct((s, G_COLS), BF16),
                   jax.ShapeDtypeStruct((3, s, D), BF16), jax.ShapeDtypeStruct((3, s, D), BF16),
                   jax.ShapeDtypeStruct((8, D), F32)],
        name="middle", compiler_params=_params(("arbitrary",), VMEM_BIG))(
            za, o, proj_g, proj_g, proj_g, x, tgt, gate, fnw, wco, wao, wo)


def _input_bwd(dpa, dpl, dpg, wa, wl, wg, x, dx2, norm_w, scale, ts, parts):
    s = x.shape[0]

    def body(dpa_ref, dpl_ref, dpg_ref, wa_ref, wl_ref, wg_ref, x_ref, dx2_ref, nw_ref, sc_ref, gx_ref, gv_ref):
        @pl.when(pl.program_id(0) == 0)
        def _():
            gv_ref[...] = jnp.zeros_like(gv_ref)

        dh = (_dot_nt(dpa_ref[...], wa_ref[...]) + _dot_nt(dpl_ref[...], wl_ref[...])
              + _dot_nt(dpg_ref[...], wg_ref[...]))
        xv = x_ref[...]
        r = lax.rsqrt(_rowmean(xv * xv) + EPS)
        xh = xv * r
        nw = nw_ref[...]
        gv_ref[0:1, :] += _colsum(dh)
        gv_ref[1:2, :] += _colsum(dh * (xh * nw))
        dy = dh * (1.0 + sc_ref[...])
        gv_ref[2:3, :] += _colsum(dy * xh)
        dxh = dy * nw
        gx_ref[...] = dx2_ref[...] + r * (dxh - xh * _rowmean(dxh * xh))

    const = lambda shape: pl.BlockSpec(shape, lambda i: (0, 0))
    rowb = lambda w: pl.BlockSpec((ts, w), lambda i: (i, 0))
    side_in, side_out, side_shapes, side_sems = _scatter_operands(parts, False)
    outs = pl.pallas_call(
        _scatter_alongside(body, 10, 2, len(parts), s // ts - 1, _chip_scatter_copies), grid=(s // ts,),
        in_specs=[rowb(A_COLS), rowb(L_COLS), rowb(G_COLS), const((D, A_COLS)), const((D, L_COLS)),
                  const((D, G_COLS)), rowb(D), rowb(D), const((1, D)), const((1, D))] + side_in,
        out_specs=[rowb(D), const((8, D))] + side_out,
        out_shape=[jax.ShapeDtypeStruct((s, D), F32), jax.ShapeDtypeStruct((8, D), F32)] + side_shapes,
        scratch_shapes=side_sems,
        name="input_bwd", compiler_params=_params(("arbitrary",), VMEM_BIG))(
            dpa, dpl, dpg, wa, wl, wg, x, dx2, norm_w, scale, *parts)
    return outs[0], outs[1], list(outs[2:])


def _adamw_math(w, g, m, v):
    nm = ADAM_B1 * m + (1.0 - ADAM_B1) * g
    nv = ADAM_B2 * v + (1.0 - ADAM_B2) * (g * g)
    m_hat = nm / (1.0 - ADAM_B1 ** ADAM_STEP)
    v_hat = nv / (1.0 - ADAM_B2 ** ADAM_STEP)
    return -ADAM_LR * (m_hat / (jnp.sqrt(v_hat) + ADAM_EPS) + ADAM_WD * w), nm, nv


def _adamw(w, g, m, v, tr, name):
    lead, (rows, cols) = w.shape[:-2], w.shape[-2:]

    def body(w_ref, g_ref, m_ref, v_ref, d_ref, nm_ref, nv_ref):
        d_ref[...], nm_ref[...], nv_ref[...] = _adamw_math(w_ref[...], g_ref[...], m_ref[...], v_ref[...])

    blk = pl.BlockSpec((1,) * len(lead) + (tr, cols), lambda i: (0,) * len(lead) + (i, 0))
    shp = jax.ShapeDtypeStruct(w.shape, F32)
    return pl.pallas_call(
        body, grid=(rows // tr,), in_specs=[blk] * 4, out_specs=[blk] * 3, out_shape=[shp] * 3, name=name,
        compiler_params=_params(("parallel",), VMEM_BIG))(w, g.reshape(w.shape), m, v)


ROW_SHIFT, ROW_SCALE, ROW_NORM_W = 0, 1, 2
ROW_FINAL_NORM_W, ROW_GATE, ROW_LOSS = 8, 9, 10
ROW_LN_W, ROW_LN_B, ROW_CONV_B = 16, 17, 18
ROW_Q_NORM_W, ROW_KV_NORM_W = 24, 25
ROW_CONV_W = 32
SUM_ROWS = 64
VECTOR_ROWS = ((ROW_SHIFT, ROW_SCALE, ROW_GATE), (ROW_NORM_W,), (ROW_CONV_B,), (ROW_LN_W,), (ROW_LN_B,),
               (ROW_Q_NORM_W,), (ROW_KV_NORM_W,), (ROW_FINAL_NORM_W,))


def _small_finalize(gathered, vectors, conv, chip):
    n = len(vectors)
    cw = conv[0].shape[2]

    def body(chip_ref, g_ref, *refs):
        ins, outs = refs[:3 * n + 3], refs[3 * n + 3:]
        tot = g_ref[0]
        for k in range(1, N_DEV):
            tot = tot + g_ref[k]
        for p, rows in enumerate(VECTOR_ROWS):
            w_ref, m_ref, v_ref = ins[3 * p:3 * p + 3]
            g_out, d_out, nm_out, nv_out = outs[4 * p:4 * p + 4]
            width = w_ref.shape[1] // len(rows)
            for q, r in enumerate(rows):
                lanes = slice(q * width, (q + 1) * width)
                g = tot[r:r + 1, 0:width]
                g_out[:, lanes] = g
                d_out[:, lanes], nm_out[:, lanes], nv_out[:, lanes] = _adamw_math(
                    w_ref[:, lanes], g, m_ref[:, lanes], v_ref[:, lanes])
        cols = pl.ds(pl.multiple_of(chip_ref[0] * cw, LANE), cw)
        gc = g_ref[0, pl.ds(ROW_CONV_W, KC), cols]
        for k in range(1, N_DEV):
            gc = gc + g_ref[k, pl.ds(ROW_CONV_W, KC), cols]
        cw_ref, cm_ref, cv_ref = ins[3 * n:3 * n + 3]
        g_out, d_out, nm_out, nv_out, dmod_ref, loss_ref = outs[4 * n:]
        g_out[0] = gc
        d_out[0], nm_out[0], nv_out[0] = _adamw_math(cw_ref[0], gc, cm_ref[0], cv_ref[0])
        for k in range(N_DEV):
            for q, r in enumerate((ROW_SHIFT, ROW_SCALE, ROW_GATE)):
                dmod_ref[k:k + 1, q * D:(q + 1) * D] = g_ref[k, r:r + 1, :]
        loss_ref[...] = (0.5 / D) * jnp.sum(tot[ROW_LOSS:ROW_LOSS + 1, :], axis=-1, keepdims=True)

    flat_in = [a for triple in vectors for a in triple] + list(conv)
    shapes = [jax.ShapeDtypeStruct(w.shape, F32) for w, _, _ in vectors for _ in range(4)]
    shapes += [jax.ShapeDtypeStruct(conv[0].shape, F32)] * 4
    shapes += [jax.ShapeDtypeStruct((N_DEV, 3 * D), F32), jax.ShapeDtypeStruct((1, 1), F32)]
    whole = pl.BlockSpec(memory_space=pltpu.VMEM)
    return pl.pallas_call(
        body, out_shape=shapes,
        in_specs=[pl.BlockSpec(memory_space=pltpu.SMEM)] + [whole] * (1 + len(flat_in)),
        out_specs=[whole] * len(shapes), name="small_finalize")(chip, gathered, *flat_in)


def _ada_bwd(c_all_t, dmod_shard):
    def body(c_ref, d_ref, o_ref):
        cv = c_ref[...]
        o_ref[...] = jnp.dot(cv * _sigmoid(cv), d_ref[...], preferred_element_type=F32,
                             precision=lax.Precision.HIGHEST)

    return pl.pallas_call(
        body, out_shape=jax.ShapeDtypeStruct((D, dmod_shard.shape[1]), F32), name="ada_bwd")(c_all_t, dmod_shard)


def _sum_chip_slabs(arrived, part, place, tr, name, axis):
    n, rows, cols = arrived.shape
    per = rows // tr
    own_map = ((lambda i, pc: (pc[0], i, 0)) if part.shape[1] == rows
               else (lambda i, pc: (pc[0], pc[1] * per + i, 0)))

    def body(place_ref, a_ref, p_ref, o_ref):
        acc = p_ref[0].astype(F32)
        for k in range(n):
            acc = acc + a_ref[k].astype(F32)
        o_ref[...] = acc

    if axis == 1:
        whole, out_map = (2 * rows, cols), lambda i, pc: (pc[1] * per + i, 0)
    else:
        whole, out_map = (rows, 2 * cols), lambda i, pc: (i, pc[1])
    grid_spec = pltpu.PrefetchScalarGridSpec(
        num_scalar_prefetch=1, grid=(per,),
        in_specs=[pl.BlockSpec((n, tr, cols), lambda i, pc: (0, i, 0)),
                  pl.BlockSpec((1, tr, cols), own_map)],
        out_specs=pl.BlockSpec((tr, cols), out_map))
    return pl.pallas_call(
        body, grid_spec=grid_spec, out_shape=jax.ShapeDtypeStruct(whole, F32), name=name,
        compiler_params=_params(("parallel",)))(place, arrived, part)


def _sum_device_partials(arrived, parts, place):
    n = len(arrived)

    def body(place_ref, *refs):
        a_refs, p_refs, o_refs = refs[:n], refs[n:2 * n], refs[2 * n:]
        for a in range(n):
            acc = p_refs[a][0].astype(F32)
            for k in range(arrived[a].shape[0]):
                acc = acc + a_refs[a][k].astype(F32)
            o_refs[a][...] = acc

    grid_spec = pltpu.PrefetchScalarGridSpec(
        num_scalar_prefetch=1, grid=(1,),
        in_specs=[pl.BlockSpec(a.shape, lambda i, pc: (0, 0, 0)) for a in arrived]
        + [pl.BlockSpec((1,) + a.shape[1:], lambda i, pc: (pc[0], pc[1], 0)) for a in arrived],
        out_specs=[pl.BlockSpec(a.shape[1:], lambda i, pc: (pc[1], 0)) for a in arrived])
    return pl.pallas_call(
        body, grid_spec=grid_spec,
        out_shape=[jax.ShapeDtypeStruct((2 * a.shape[1], a.shape[2]), F32) for a in arrived],
        name="sum_device_partials", compiler_params=_params(("arbitrary",), VMEM_BIG))(place, *arrived, *parts)


def _adamw_many(ws, gs, ms, vs, tr):
    n = len(ws)
    rows = ws[0].shape[1]

    def body(*refs):
        ins, outs = refs[:4 * n], refs[4 * n:]
        for a in range(n):
            w_ref, g_ref, m_ref, v_ref = ins[4 * a:4 * a + 4]
            outs[3 * a][...], outs[3 * a + 1][...], outs[3 * a + 2][...] = _adamw_math(
                w_ref[...], g_ref[...], m_ref[...], v_ref[...])

    blk = lambda w: pl.BlockSpec((1, tr, w.shape[2]), lambda i: (0, i, 0))
    gs = [g.reshape(w.shape) for g, w in zip(gs, ws)]
    flat = [a for quad in zip(ws, gs, ms, vs) for a in quad]
    outs = pl.pallas_call(
        body, grid=(rows // tr,), in_specs=[blk(w) for w in ws for _ in range(4)],
        out_specs=[blk(w) for w in ws for _ in range(3)],
        out_shape=[jax.ShapeDtypeStruct(w.shape, F32) for w in ws for _ in range(3)], name="adamw_small_matrices",
        compiler_params=_params(("parallel",), VMEM_BIG))(*flat)
    return [(gs[a], outs[3 * a], outs[3 * a + 1], outs[3 * a + 2]) for a in range(n)]


def _add_own_half(full, other, core, tr, name, axis):
    n, rows, cols = other.shape
    per = rows // tr

    def body(c_ref, f_ref, o_ref, out_ref):
        out_ref[...] = (f_ref[...].astype(F32) + o_ref[...].astype(F32)).astype(BF16)

    full_map = (lambda k, i, c: (k, c[0] * per + i, 0)) if axis == 1 else (lambda k, i, c: (k, i, c[0]))
    grid_spec = pltpu.PrefetchScalarGridSpec(
        num_scalar_prefetch=1, grid=(n, per),
        in_specs=[pl.BlockSpec((1, tr, cols), full_map),
                  pl.BlockSpec((1, tr, cols), lambda k, i, c: (k, i, 0))],
        out_specs=pl.BlockSpec((1, tr, cols), lambda k, i, c: (k, i, 0)))
    return pl.pallas_call(
        body, grid_spec=grid_spec, out_shape=jax.ShapeDtypeStruct((n, rows, cols), BF16), name=name,
        compiler_params=_params(("parallel", "parallel"), VMEM_BIG))(core, full, other)


def _allgather8_run(x_ref, out_ref, send_sems, recv_sems, local_sem):
    m = x_ref.shape[0]
    x, y, c = _coords()
    me, sibling = (x, y, c), (x, y, 1 - c)
    chips = [(1 - x, y), (x, 1 - y), (1 - x, 1 - y)]

    def rows(px, py, pc):
        return out_ref.at[pl.ds(pl.multiple_of((4 * px + 2 * py + pc) * m, 8), m), :]

    def copy(k, blk, to, source=None):
        return pltpu.make_async_remote_copy(
            src_ref=rows(*blk) if source is None else source, dst_ref=rows(*blk),
            send_sem=send_sems.at[k], recv_sem=recv_sems.at[k], device_id=to, device_id_type=MESH)

    mine = pltpu.make_async_copy(x_ref, rows(*me), local_sem)
    mine.start()
    first = [copy(0, me, sibling, source=x_ref)]
    first += [copy(1 + j, me, (*chip, c), source=x_ref) for j, chip in enumerate(chips)]
    for cp in first:
        cp.start()
    passed = [copy(4 + j, (*chip, c), sibling) for j, chip in enumerate(chips)]
    for j, chip in enumerate(chips):
        copy(1 + j, (*chip, c), me).wait_recv()
        passed[j].start()
    copy(0, sibling, me).wait_recv()
    for j, chip in enumerate(chips):
        copy(4 + j, (*chip, 1 - c), me).wait_recv()
    for cp in first + passed:
        cp.wait_send()
    mine.wait()


ALLGATHER8_SEMS = [pltpu.SemaphoreType.DMA((7,)), pltpu.SemaphoreType.DMA((7,)), pltpu.SemaphoreType.DMA]


def _gather_plan(x_refs, out_refs, send_sems, recv_sems, local_sems):
    n = len(x_refs)
    halves = [r.shape[0] // 2 for r in x_refs]
    x, y, c = _coords()
    me, sibling = (x, y, c), (x, y, 1 - c)
    chips = [(1 - x, y), (x, 1 - y), (1 - x, 1 - y)]

    def src(a):
        return x_refs[a].at[pl.ds(pl.multiple_of(c * halves[a], 16), halves[a]), :]

    def blk(a, px, py, pc):
        return out_refs[a].at[4 * px + 2 * py + pc]

    def copy(a, k, who, to, source=None):
        return pltpu.make_async_remote_copy(
            src_ref=blk(a, *who) if source is None else source, dst_ref=blk(a, *who),
            send_sem=send_sems.at[7 * a + k], recv_sem=recv_sems.at[7 * a + k], device_id=to, device_id_type=MESH)

    def mine(a):
        return pltpu.make_async_copy(src(a), blk(a, *me), local_sems.at[a])

    def first(a):
        return ([copy(a, 0, me, sibling, source=src(a))]
                + [copy(a, 1 + j, me, (*chip, c), source=src(a)) for j, chip in enumerate(chips)])

    def begin():
        for a in range(n):
            mine(a).start()
        for a in range(n):
            for cp in first(a):
                cp.start()

    def finish():
        onward = []
        for j, chip in enumerate(chips):
            for a in range(n):
                copy(a, 1 + j, (*chip, c), me).wait_recv()
                onward.append(copy(a, 4 + j, (*chip, c), sibling))
                onward[-1].start()
        for a in range(n):
            copy(a, 0, sibling, me).wait_recv()
        for j, chip in enumerate(chips):
            for a in range(n):
                copy(a, 4 + j, (*chip, 1 - c), me).wait_recv()
        for a in range(n):
            for cp in first(a):
                cp.wait_send()
        for cp in onward:
            cp.wait_send()
        for a in range(n):
            mine(a).wait()

    return begin, finish


def _gather_operands(shards):
    n = len(shards)
    shapes = [jax.ShapeDtypeStruct((N_DEV, a.shape[0] // 2, a.shape[1]), a.dtype) for a in shards]
    sems = [pltpu.SemaphoreType.DMA((7 * n,)), pltpu.SemaphoreType.DMA((7 * n,)), pltpu.SemaphoreType.DMA((n,))]
    return shapes, sems


def _as_chip_slabs(gathered, shards):
    return [o.reshape(N_CHIP, a.shape[0], a.shape[1]) for o, a in zip(gathered, shards)]


def _gather_alongside(body, n_in, n_out, n_shards, last_step, first=None):
    def wrapped(*refs):
        ins, shards = refs[:n_in], refs[n_in:n_in + n_shards]
        rest = refs[n_in + n_shards:]
        outs, gathered = rest[:n_out], rest[n_out:n_out + n_shards]
        scratch, sems = rest[n_out + n_shards:-3], rest[-3:]

        @pl.when(pl.program_id(0) == 0)
        def _():
            if first is not None:
                first(*ins, *outs, *scratch)
            _gather_plan(shards, gathered, *sems)[0]()

        body(*ins, *outs, *scratch)

        @pl.when(pl.program_id(0) == last_step)
        def _():
            _gather_plan(shards, gathered, *sems)[1]()

    return wrapped


def _half(ref, axis, which, ndim):
    size = ref.shape[axis] // 2
    idx = [slice(None)] * ndim
    idx[axis] = pl.ds(pl.multiple_of(which * size, 8 if axis == ndim - 2 else LANE), size)
    return ref.at[tuple(idx)]


def _swap_halves_with_sibling(fulls, name, axes):
    n = len(fulls)

    def body(*refs):
        f_refs, got_refs = refs[:n], refs[n:2 * n]
        send_sems, recv_sems = refs[2 * n:]
        x, y, c = _coords()
        copies = []
        for a in range(n):
            copies.append(pltpu.make_async_remote_copy(
                src_ref=_half(f_refs[a], axes[a], 1 - c, 3), dst_ref=got_refs[a], send_sem=send_sems.at[a],
                recv_sem=recv_sems.at[a], device_id=(x, y, 1 - c), device_id_type=MESH))
        for cp in copies:
            cp.start()
        for cp in copies:
            cp.wait()

    def halved(a, axis):
        shape = list(a.shape)
        shape[axis] //= 2
        return jax.ShapeDtypeStruct(tuple(shape), a.dtype)

    return pl.pallas_call(
        body, out_shape=[halved(a, ax) for a, ax in zip(fulls, axes)],
        in_specs=[HBM_REF] * n, out_specs=[HBM_REF] * n,
        scratch_shapes=[pltpu.SemaphoreType.DMA((n,)), pltpu.SemaphoreType.DMA((n,))],
        name=name)(*fulls)


def _join_halves_with_sibling(wholes, axes, block):
    n = len(wholes)
    twice = jnp.concatenate([block, block], axis=0)
    gathered_shapes, gather_sems = _gather_operands([twice])

    def body(*refs):
        out_refs = refs[n + 1:2 * n + 1]
        send_sems, recv_sems = refs[2 * n + 2:2 * n + 4]
        begin, finish = _gather_plan([refs[n]], [refs[2 * n + 1]], *refs[2 * n + 4:])
        x, y, c = _coords()

        def push(a, core):
            half = _half(out_refs[a], axes[a] - 1, core, 2)
            return pltpu.make_async_remote_copy(
                src_ref=half, dst_ref=half, send_sem=send_sems.at[a], recv_sem=recv_sems.at[a],
                device_id=(x, y, 1 - c), device_id_type=MESH)

        begin()
        for a in range(n):
            push(a, c).start()
        finish()
        for a in range(n):
            push(a, 1 - c).wait_recv()
        for a in range(n):
            push(a, c).wait_send()

    outs = pl.pallas_call(
        body, out_shape=[jax.ShapeDtypeStruct(a.shape, a.dtype) for a in wholes] + gathered_shapes,
        in_specs=[HBM_REF] * (n + 1), out_specs=[HBM_REF] * (n + 1), input_output_aliases={a: a for a in range(n)},
        scratch_shapes=[pltpu.SemaphoreType.DMA((n,)), pltpu.SemaphoreType.DMA((n,))] + gather_sems,
        name="rs_pair_join")(*wholes, twice)
    return outs[:n], outs[n]


def _cols_to_slabs(g):
    rows, cols = g.shape
    return g.reshape(rows, N_CHIP, cols // N_CHIP).transpose(1, 0, 2)


def _slabs_to_cols(w):
    n, rows, cols = w.shape
    return w.transpose(1, 0, 2).reshape(rows, n * cols)


def _col_window(slabs, start, stop):
    n = slabs.shape[2]
    pieces = []
    for k in range(N_CHIP):
        lo, hi = max(start, k * n), min(stop, (k + 1) * n)
        if lo < hi:
            pieces.append(slabs[k][:, lo - k * n:hi - k * n])
    return pieces[0] if len(pieces) == 1 else jnp.concatenate(pieces, axis=1)


def _slabs_from_groups(groups, n):
    slabs = []
    for k in range(N_CHIP):
        pieces, off = [], 0
        for g in groups:
            lo, hi = max(k * n, off), min((k + 1) * n, off + g.shape[0])
            if lo < hi:
                pieces.append(g[lo - off:hi - off])
            off += g.shape[0]
        slabs.append(pieces[0] if len(pieces) == 1 else jnp.concatenate(pieces, axis=0))
    return jnp.stack(slabs)


def _uq_to_padded(w_uq):
    per = w_uq.reshape(RQ, H, DN + DR)
    nope = per[:, :, :DN].reshape(RQ, H * DN)
    rope = jnp.pad(per[:, :, DN:], ((0, 0), (0, 0), (0, LANE - DR))).reshape(RQ, H * LANE)
    return jnp.concatenate([nope, rope], axis=1)


def _uq_from_padded(g):
    nope = g[:, :H * DN].reshape(RQ, H, DN)
    rope = g[:, H * DN:].reshape(RQ, H, LANE)[:, :, :DR]
    return jnp.concatenate([nope, rope], axis=2).reshape(RQ, H * (DN + DR))


def _rope_tables(positions):
    inv_freq = ROPE_THETA ** (-jnp.arange(0, DR, 2, dtype=F32) / DR)
    ang = positions.astype(F32)[:, None] * inv_freq
    cos, sin = jnp.cos(ang), jnp.sin(ang)
    return jnp.tile(cos, (1, 4)), jnp.tile(jnp.concatenate([-sin, sin], axis=1), (1, 2))


def _pair_sums(fulls, core, tag, axes, tr):
    from_sibling = _swap_halves_with_sibling(fulls, f"rs_pair_swap_{tag}", axes)
    return [_add_own_half(f, o, core, min(tr, o.shape[1]), f"add_own_half_{tag}{n}", ax)
            for n, (f, o, ax) in enumerate(zip(fulls, from_sibling, axes))]


def _local_step(x, tgt, cos_t, sin_t, ada, weights, small, tiles, place):
    ts, ts_in, ts_mla, tm_nn, tm_tn, t_attn, chunk = tiles
    w_in_shard, later_shards = weights
    norm_w, conv_b, ln_w, ln_b, q_norm_w, kv_norm_w, fnw = small
    h, mod, c_all, conv_w, (g_in,) = _adaln_norm(x, norm_w, *ada, ts, [w_in_shard])
    scale, gate = mod[:, D:2 * D], mod[:, 2 * D:3 * D]
    wa = _col_window(g_in, 0, A_COLS)
    wl = jnp.pad(_col_window(g_in, A_COLS, A_COLS + L_COLS_RAW), ((0, 0), (0, L_COLS - L_COLS_RAW)))
    wg = _col_window(g_in, A_COLS + L_COLS_RAW, IN_COLS)
    proj_a = _mm_nn(h, wa, tm_nn, D, "proj_a")
    u0, u1, za, (g_uq, g_ukv, g_co, g_ao, g_o) = _conv_fwd(proj_a, conv_w, conv_b, ln_w, ln_b, ts, chunk, later_shards)
    w_uq2, w_ukv = _uq_to_padded(_slabs_to_cols(g_uq)), _slabs_to_cols(g_ukv)
    wco, wao, wo = g_co.reshape(D, D), g_ao.reshape(D, D), g_o.reshape(D, D)
    proj_l = _mm_nn(h, wl, tm_nn, L_COLS, "proj_l")
    proj_g = _mm_nn(h, wg, tm_nn, D, "proj_g")
    qn, kvn, q, k, v = _mla_prep(proj_l, q_norm_w, kv_norm_w, w_uq2, w_ukv, cos_t, sin_t, ts_mla)
    o, lse = _attn_fwd(q, k, v, t_attn)
    dx2, dza, do, delta, dpg, lhs3, rhs3, vec_mid = _middle(za, o, proj_g, x, tgt, gate, fnw, wco, wao, wo, ts)
    g_wco, g_wao, g_wo = _mm_tn_stack(lhs3, rhs3, tm_tn, "grad_w_out3")
    dq, dk, dv = _attn_bwd(q, k, v, do, lse, delta, t_attn)
    dpl, g_wuq2, g_wukv, vec_mla = _mla_prep_bwd(
        dq, dk, dv, proj_l, qn, kvn, q_norm_w, kv_norm_w, w_uq2, w_ukv, cos_t, sin_t, ts_mla)

    core = place[1:2]
    nr = D // N_CHIP
    early = [_cols_to_slabs(_uq_from_padded(g_wuq2)).astype(BF16), _cols_to_slabs(g_wukv).astype(BF16),
             g_wco.reshape(N_CHIP, nr, D), g_wao.reshape(N_CHIP, nr, D), g_wo.reshape(N_CHIP, nr, D)]
    dpa, g_conv_w, vec_conv, early_got = _conv_bwd(dza, proj_a, u0, u1, conv_w, ln_w, ln_b, ts, chunk // 2, early)

    g_wa_t = _mm_tn(dpa, h, tm_tn, D, D, "grad_w_in_a", BF16)
    g_wl_t = _mm_tn(dpl, h, tm_tn, L_COLS, D, "grad_w_in_l", BF16)
    g_wg_t = _mm_tn(dpg, h, tm_tn, D, D, "grad_w_in_g", BF16)
    g_w_in_slabs = _slabs_from_groups([g_wa_t, g_wl_t[0:L_COLS_RAW], g_wg_t], IN_COLS // N_CHIP)
    late_sums = _pair_sums([g_w_in_slabs], core, "b", [2], IN_COLS // N_CHIP)
    grad_x, vec_in, late_got = _input_bwd(dpa, dpl, dpg, wa, wl, wg, x, dx2, norm_w, scale, ts_in, late_sums)

    col_sums = jnp.concatenate(
        [vec_in, vec_mid, vec_conv, jnp.pad(vec_mla, ((0, 0), (0, D - RQ))), g_conv_w], axis=0)
    wholes = ([_sum_chip_slabs(late_got[0], late_sums[0], place, W_IN_ROWS, "sum_chip_slabs_w_in", 2)]
              + list(_sum_device_partials(early_got, early, place)))
    shards, all_col_sums = _join_halves_with_sibling(wholes, [2] + [1] * len(early), col_sums)

    return grad_x, shards, all_col_sums, c_all


def kernel(x, c, positions, w_ada, b_ada, norm_w, w_in, conv_w, conv_b, conv_ln_w, conv_ln_b, w_conv_out, q_norm_w, w_uq, kv_norm_w, w_ukv, w_attn_out, w_out, final_norm_w, loss_target, m_w_ada, m_b_ada, m_norm_w, m_w_in, m_conv_w, m_conv_b, m_conv_ln_w, m_conv_ln_b, m_w_conv_out, m_q_norm_w, m_w_uq, m_kv_norm_w, m_w_ukv, m_w_attn_out, m_w_out, m_final_norm_w, v_w_ada, v_b_ada, v_norm_w, v_w_in, v_conv_w, v_conv_b, v_conv_ln_w, v_conv_ln_b, v_w_conv_out, v_q_norm_w, v_w_uq, v_kv_norm_w, v_w_ukv, v_w_attn_out, v_w_out, v_final_norm_w):
    ix, iy, ic = _coords()
    chip = 2 * ix + iy
    dev = 4 * ix + 2 * iy + ic
    s = x.shape[1]
    tiles = (256, 256, 512, 1024, 2048, 512, 32)

    conv_w_pad = jnp.pad(conv_w[0], ((0, HALO - KC), (0, D - conv_w.shape[2])))
    small_in = jnp.concatenate([jnp.pad(c, ((0, 7), (0, 0))), conv_w_pad], axis=0)

    later_shards = [w[0].astype(BF16) for w in (w_uq, w_ukv, w_conv_out, w_attn_out, w_out)]
    weights = (w_in[0].astype(BF16), later_shards)

    ada_cols = w_ada.shape[2]
    b_shard = lax.dynamic_slice(b_ada, (0, chip * ada_cols), (1, ada_cols))
    ada = (small_in, w_ada[0], b_shard, dev.reshape(1).astype(jnp.int32))

    cos_t, sin_t = _rope_tables(positions[0])
    small = (norm_w, conv_b, conv_ln_w, conv_ln_b, q_norm_w, kv_norm_w, final_norm_w.reshape(1, D))
    place = jnp.stack([chip, ic]).astype(jnp.int32)
    grad_x, shards, gathered, c_all = _local_step(x[0], loss_target[0], cos_t, sin_t, ada, weights, small, tiles, place)
    g_w_in_s, g_w_uq_s, g_w_ukv_s, g_wco_s, g_wao_s, g_wo_s = shards

    vec_names = ("b_ada", "norm_w", "conv_b", "conv_ln_w", "conv_ln_b", "q_norm_w", "kv_norm_w", "final_norm_w")
    row = lambda a: a.reshape(1, -1)
    vectors = [(row(b_ada), row(m_b_ada), row(v_b_ada)), (norm_w, m_norm_w, v_norm_w), (conv_b, m_conv_b, v_conv_b),
               (conv_ln_w, m_conv_ln_w, v_conv_ln_w), (conv_ln_b, m_conv_ln_b, v_conv_ln_b),
               (q_norm_w, m_q_norm_w, v_q_norm_w), (kv_norm_w, m_kv_norm_w, v_kv_norm_w),
               (row(final_norm_w), row(m_final_norm_w), row(v_final_norm_w))]
    fin = _small_finalize(gathered, vectors, (conv_w, m_conv_w, v_conv_w), place[0:1])
    res = {}
    for p, (name, (w, _, _)) in enumerate(zip(vec_names, vectors)):
        shape = final_norm_w.shape if name == "final_norm_w" else w.shape
        res[name] = tuple(a.reshape(shape) for a in fin[4 * p:4 * p + 4])
    res["conv_w"] = tuple(fin[4 * len(vectors):4 * len(vectors) + 4])
    dmod_all, loss = fin[-2], fin[-1].reshape(())
    dmod_shard = lax.dynamic_slice(dmod_all, (0, chip * ada_cols), (N_DEV, ada_cols))
    g_w_ada = _ada_bwd(c_all.T, dmod_shard).reshape(1, D, ada_cols)

    def big(w, g, m, v, tr, name):
        d, nm, nv = _adamw(w, g, m, v, tr, name)
        return g.reshape(w.shape), d, nm, nv

    res["w_ada"] = big(w_ada, g_w_ada[0], m_w_ada, v_w_ada, 256, "adamw_w_ada")
    t_in = [a[0].T for a in (w_in, m_w_in, v_w_in)]
    d_t, nm_t, nv_t = _adamw(t_in[0], g_w_in_s, t_in[1], t_in[2], W_IN_ROWS, "adamw_w_in")
    res["w_in"] = tuple(a.T[None] for a in (g_w_in_s, d_t, nm_t, nv_t))
    small = _adamw_many(
        [w_uq, w_ukv, w_conv_out, w_attn_out, w_out], [g_w_uq_s, g_w_ukv_s, g_wco_s, g_wao_s, g_wo_s],
        [m_w_uq, m_w_ukv, m_w_conv_out, m_w_attn_out, m_w_out], [v_w_uq, v_w_ukv, v_w_conv_out, v_w_attn_out, v_w_out],
        128)
    res["w_uq"], res["w_ukv"], res["w_conv_out"], res["w_attn_out"], res["w_out"] = small

    order = ("w_ada", "b_ada", "norm_w", "w_in", "conv_w", "conv_b", "conv_ln_w", "conv_ln_b", "w_conv_out",
             "q_norm_w", "w_uq", "kv_norm_w", "w_ukv", "w_attn_out", "w_out", "final_norm_w")
    outs = [loss, grad_x[None]]
    for slot in range(4):
        outs += [res[name][slot] for name in order]
    return tuple(outs)
```

```python
import functools

import numpy as np
import jax
import jax.numpy as jnp
from jax import lax
from jax.experimental import pallas as pl
from jax.experimental.pallas import tpu as pltpu

F32 = jnp.float32
BF16 = jnp.bfloat16
MESH = pl.DeviceIdType.MESH

D = 1024
H = 8
DN = 128
DR = 64
RQ = 256
KC = 31
HALO = 32
EPS = 1e-6
ROPE_THETA = 10000.0
N_CHIP = 4
N_DEV = 8
LANE = 128
VMEM_BIG = 56 * 1024 * 1024

ADAM_LR = 0.001
ADAM_B1 = 0.9
ADAM_B2 = 0.999
ADAM_EPS = 1e-08
ADAM_WD = 0.01
ADAM_STEP = 10

A_COLS = 3 * D
L_COLS_RAW = RQ + RQ + DR
L_COLS = 640
G_COLS = 3 * D
IN_COLS = A_COLS + L_COLS_RAW + G_COLS


def _params(sem=None, vmem=None):
    kw = {}
    if sem is not None:
        kw["dimension_semantics"] = sem
    if vmem is not None:
        kw["vmem_limit_bytes"] = vmem
    return pltpu.CompilerParams(**kw)


def _dot(a, b):
    return jnp.dot(a, b, preferred_element_type=F32)


def _dot_nt(a, b):
    return lax.dot_general(a, b, (((1,), (1,)), ((), ())), preferred_element_type=F32)


def _dot_tn(a, b):
    return lax.dot_general(a, b, (((0,), (0,)), ((), ())), preferred_element_type=F32)


def _colsum(v):
    return jnp.sum(v, axis=0, keepdims=True)


def _rowmean(v):
    return jnp.mean(v, axis=-1, keepdims=True)


def _sigmoid(v):
    return jax.nn.sigmoid(v)


def _dsilu(v, s):
    return s * (1.0 + v * (1.0 - s))


def _swap_halves(v, first_half):
    return jnp.where(first_half, pltpu.roll(v, 96, 1), pltpu.roll(v, 32, 1))


def _first_half_mask(rows):
    lane = lax.broadcasted_iota(jnp.int32, (rows, LANE), 1)
    return (lane % 64) < 32


SMALL_IN_ROWS = 8 + HALO


def _adaln_norm(x, norm_w, small_in, w_ada_shard, b_ada_shard, dev, ts, shards):
    s = x.shape[0]
    cols = w_ada_shard.shape[1]
    taps = D // N_CHIP

    def modulation(dev_ref, x_ref, nw_ref, sm_ref, w_ref, b_ref, h_ref, mod_ref, c_ref, conv_ref,
                   part_sc, all_sc, small_sc, *sems):
        _allgather8_run(sm_ref, small_sc, *sems[0:3])
        for k in range(N_DEV):
            c_ref[k:k + 1, :] = small_sc[SMALL_IN_ROWS * k:SMALL_IN_ROWS * k + 1, :]
        for k in range(N_CHIP):
            base = SMALL_IN_ROWS * 2 * k + 8
            conv_ref[:, taps * k:taps * (k + 1)] = small_sc[base:base + HALO, 0:taps]
        cv = c_ref[...]
        part_sc[...] = jnp.dot(cv * _sigmoid(cv), w_ref[...], preferred_element_type=F32,
                               precision=lax.Precision.HIGHEST) + b_ref[...]
        _allgather8_run(part_sc, all_sc, *sems[3:6])
        for k in range(N_CHIP):
            mod_ref[:, cols * k:cols * (k + 1)] = all_sc[pl.ds(2 * N_DEV * k + dev_ref[0], 1), :]

    def body(dev_ref, x_ref, nw_ref, sm_ref, w_ref, b_ref, h_ref, mod_ref, *rest):
        xv = x_ref[...]
        r = lax.rsqrt(_rowmean(xv * xv) + EPS)
        y = xv * r * nw_ref[...]
        h_ref[...] = (y * (1.0 + mod_ref[:, D:2 * D]) + mod_ref[:, 0:D]).astype(BF16)

    row = pl.BlockSpec((ts, D), lambda i: (i, 0))
    const = lambda shape: pl.BlockSpec(shape, lambda i: (0, 0))
    n = len(shards)
    gathered_shapes, sems = _gather_operands(shards)
    outs = pl.pallas_call(
        _gather_alongside(body, 6, 4, n, s // ts - 1, modulation), grid=(s // ts,),
        in_specs=[pl.BlockSpec(memory_space=pltpu.SMEM), row, const((1, D)), const(small_in.shape),
                  const(w_ada_shard.shape), const((1, cols))] + [HBM_REF] * n,
        out_specs=[row, const((1, 3 * D)), const((N_DEV, D)), const((HALO, D))] + [HBM_REF] * n,
        out_shape=[jax.ShapeDtypeStruct((s, D), BF16), jax.ShapeDtypeStruct((1, 3 * D), F32),
                   jax.ShapeDtypeStruct((N_DEV, D), F32), jax.ShapeDtypeStruct((HALO, D), F32)] + gathered_shapes,
        scratch_shapes=[pltpu.VMEM((N_DEV, cols), F32), pltpu.VMEM((N_DEV * N_DEV, cols), F32),
                        pltpu.VMEM((N_DEV * SMALL_IN_ROWS, D), F32)] + ALLGATHER8_SEMS + ALLGATHER8_SEMS + sems,
        name="adaln_norm", compiler_params=_params(("arbitrary",), VMEM_BIG))(
            dev, x, norm_w, small_in, w_ada_shard, b_ada_shard, *shards)
    return outs[0], outs[1], outs[2], outs[3], _as_chip_slabs(outs[4:], shards)


def _mm_nn(a, b, tm, tn, name):
    m, k = a.shape
    n = b.shape[1]

    def body(a_ref, b_ref, o_ref):
        o_ref[...] = _dot(a_ref[...], b_ref[...])

    return pl.pallas_call(
        body, grid=(n // tn, m // tm),
        in_specs=[pl.BlockSpec((tm, k), lambda j, i: (i, 0)), pl.BlockSpec((k, tn), lambda j, i: (0, j))],
        out_specs=pl.BlockSpec((tm, tn), lambda j, i: (i, j)),
        out_shape=jax.ShapeDtypeStruct((m, n), F32), name=name,
        compiler_params=_params(("parallel", "parallel"), VMEM_BIG))(a, b)


def _mm_tn(a, b, tm, tk, tn, name, out_dtype=F32):
    m, k = a.shape
    n = b.shape[1]
    steps = m // tm

    def body(a_ref, b_ref, o_ref, acc_ref):
        @pl.when(pl.program_id(2) == 0)
        def _():
            acc_ref[...] = jnp.zeros_like(acc_ref)
        acc_ref[...] += _dot_tn(a_ref[...], b_ref[...])

        @pl.when(pl.program_id(2) == steps - 1)
        def _():
            o_ref[...] = acc_ref[...].astype(out_dtype)

    return pl.pallas_call(
        body, grid=(k // tk, n // tn, steps),
        in_specs=[pl.BlockSpec((tm, tk), lambda r, j, i: (i, r)), pl.BlockSpec((tm, tn), lambda r, j, i: (i, j))],
        out_specs=pl.BlockSpec((tk, tn), lambda r, j, i: (r, j)),
        out_shape=jax.ShapeDtypeStruct((k, n), out_dtype), scratch_shapes=[pltpu.VMEM((tk, tn), F32)], name=name,
        compiler_params=_params(("parallel", "parallel", "arbitrary"), VMEM_BIG))(a, b)


def _mm_tn_stack(a, b, tm, name):
    n_stack, m, k = a.shape
    n = b.shape[2]
    steps = m // tm

    def body(a_ref, b_ref, o_ref, acc_ref):
        @pl.when(pl.program_id(1) == 0)
        def _():
            acc_ref[...] = jnp.zeros_like(acc_ref)
        acc_ref[...] += _dot_tn(a_ref[0], b_ref[0])

        @pl.when(pl.program_id(1) == steps - 1)
        def _():
            o_ref[0] = acc_ref[...].astype(BF16)

    out = pl.pallas_call(
        body, grid=(n_stack, steps),
        in_specs=[pl.BlockSpec((1, tm, k), lambda g, i: (g, i, 0)), pl.BlockSpec((1, tm, n), lambda g, i: (g, i, 0))],
        out_specs=pl.BlockSpec((1, k, n), lambda g, i: (g, 0, 0)),
        out_shape=jax.ShapeDtypeStruct((n_stack, k, n), BF16), scratch_shapes=[pltpu.VMEM((k, n), F32)], name=name,
        compiler_params=_params(("parallel", "arbitrary"), VMEM_BIG))(a, b)
    return [out[g] for g in range(n_stack)]


def _coords():
    return lax.axis_index("x"), lax.axis_index("y"), lax.axis_index("c")


HBM_REF = pl.BlockSpec(memory_space=pl.ANY)


def _chip_scatter_copies(p_refs, got_refs, send_sems, recv_sems):
    x, y, c = _coords()
    copies = []
    for a in range(len(p_refs)):
        for j, (px, py) in enumerate([(1 - x, y), (x, 1 - y), (1 - x, 1 - y)]):
            copies.append(pltpu.make_async_remote_copy(
                src_ref=p_refs[a].at[2 * px + py], dst_ref=got_refs[a].at[j], send_sem=send_sems.at[3 * a + j],
                recv_sem=recv_sems.at[3 * a + j], device_id=(px, py, c), device_id_type=MESH))
    return copies


RELATIONS = [(dx, dy, dc) for dx in (0, 1) for dy in (0, 1) for dc in (0, 1)][1:]


def _device_scatter_copies(p_refs, got_refs, send_sems, recv_sems):
    x, y, c = _coords()
    copies = []
    for a in range(len(p_refs)):
        half = p_refs[a].shape[1] // 2
        for j, (dx, dy, dc) in enumerate(RELATIONS):
            px, py, pc = (1 - x if dx else x), (1 - y if dy else y), (1 - c if dc else c)
            src = p_refs[a].at[2 * px + py, pl.ds(pl.multiple_of(pc * half, 16), half), :]
            copies.append(pltpu.make_async_remote_copy(
                src_ref=src, dst_ref=got_refs[a].at[j], send_sem=send_sems.at[7 * a + j],
                recv_sem=recv_sems.at[7 * a + j], device_id=(px, py, pc), device_id_type=MESH))
    return copies


def _scatter_alongside(body, n_in, n_out, n_parts, last_step, make_copies):
    def wrapped(*refs):
        ins, parts = refs[:n_in], refs[n_in:n_in + n_parts]
        rest = refs[n_in + n_parts:]
        outs, got = rest[:n_out], rest[n_out:n_out + n_parts]
        scratch, (send_sems, recv_sems) = rest[n_out + n_parts:-2], rest[-2:]

        @pl.when(pl.program_id(0) == 0)
        def _():
            for cp in make_copies(parts, got, send_sems, recv_sems):
                cp.start()

        body(*ins, *outs, *scratch)

        @pl.when(pl.program_id(0) == last_step)
        def _():
            for cp in make_copies(parts, got, send_sems, recv_sems):
                cp.wait()

    return wrapped


def _scatter_operands(parts, per_device):
    n = len(parts)
    if per_device:
        slots, shapes = 7, [jax.ShapeDtypeStruct((7, a.shape[1] // 2, a.shape[2]), a.dtype) for a in parts]
    else:
        slots, shapes = 3, [jax.ShapeDtypeStruct((3,) + a.shape[1:], a.dtype) for a in parts]
    sems = [pltpu.SemaphoreType.DMA((slots * n,)), pltpu.SemaphoreType.DMA((slots * n,))]
    return [HBM_REF] * n, [HBM_REF] * n, shapes, sems


def _shifted_copies(win_ref, sh_ref, rows):
    for p in range(1, 8):
        sh_ref[p - 1, 0:rows, :] = win_ref[pl.ds(p, rows), :]


def _tap_rows(win_ref, sh_ref, start, rows):
    p = start % 8
    if p == 0:
        return win_ref[pl.ds(start, rows), :]
    return sh_ref[p - 1, pl.ds(start - p, rows), :]


def _conv_taps(win_ref, sh_ref, w_ref, rows, chunk, offset_of_tap):
    pieces = []
    for c0 in range(0, rows, chunk):
        acc = None
        for j in range(KC):
            term = w_ref[j:j + 1, :] * _tap_rows(win_ref, sh_ref, c0 + offset_of_tap(j), chunk)
            acc = term if acc is None else acc + term
        pieces.append(acc)
    return pieces


def _conv_fwd(proj_a, conv_w, conv_b, ln_w, ln_b, ts, chunk, shards):
    s = proj_a.shape[0]

    def body(av_ref, al_ref, ag_ref, w_ref, b_ref, lw_ref, lb_ref, u0_ref, u1_ref, za_ref, win_ref, sh_ref):
        @pl.when(pl.program_id(0) == 0)
        def _():
            win_ref[0:HALO, :] = jnp.zeros((HALO, D), F32)

        u0 = av_ref[...] * _sigmoid(al_ref[...])
        u0_ref[...] = u0
        win_ref[HALO:HALO + ts, :] = u0
        _shifted_copies(win_ref, sh_ref, ts + HALO - 8)
        pieces = _conv_taps(win_ref, sh_ref, w_ref, ts, chunk, lambda j: HALO - (KC - 1) + j)
        for n, acc in enumerate(pieces):
            u1_ref[n * chunk:(n + 1) * chunk, :] = acc + b_ref[...]
        win_ref[0:HALO, :] = win_ref[ts:ts + HALO, :]

        u1 = u1_ref[...]
        xc = u1 - _rowmean(u1)
        rstd = lax.rsqrt(_rowmean(xc * xc) + EPS)
        u2 = xc * rstd * lw_ref[...] + lb_ref[...]
        u3 = u2 * _sigmoid(u2)
        ag = ag_ref[...]
        za_ref[...] = (u3 * (ag * _sigmoid(ag))).astype(BF16)

    col = lambda c: pl.BlockSpec((ts, D), lambda i, c=c: (i, c))
    row = pl.BlockSpec((ts, D), lambda i: (i, 0))
    vec = pl.BlockSpec((1, D), lambda i: (0, 0))
    n = len(shards)
    gathered_shapes, sems = _gather_operands(shards)
    outs = pl.pallas_call(
        _gather_alongside(body, 7, 3, n, s // ts - 1), grid=(s // ts,),
        in_specs=[col(0), col(1), col(2), pl.BlockSpec((HALO, D), lambda i: (0, 0)), vec, vec, vec] + [HBM_REF] * n,
        out_specs=[row, row, row] + [HBM_REF] * n,
        out_shape=[jax.ShapeDtypeStruct((s, D), F32), jax.ShapeDtypeStruct((s, D), F32),
                   jax.ShapeDtypeStruct((s, D), BF16)] + gathered_shapes,
        scratch_shapes=[pltpu.VMEM((ts + HALO, D), F32), pltpu.VMEM((7, ts + HALO, D), F32)] + sems,
        name="conv_fwd", compiler_params=_params(("arbitrary",), VMEM_BIG))(
            proj_a, proj_a, proj_a, conv_w, conv_b, ln_w, ln_b, *shards)
    return outs[0], outs[1], outs[2], _as_chip_slabs(outs[3:], shards)


def _conv_bwd(dza, proj_a, u0, u1, conv_w, ln_w, ln_b, ts, chunk, parts):
    s = dza.shape[0]
    nt = s // ts
    per = ts // HALO

    def body(dza_ref, av_ref, al_ref, ag_ref, u0_ref, u0p_ref, u1_ref, w_ref, lw_ref, lb_ref,
             dpa_ref, gw_ref, gv_ref, dwin_ref, uwin_ref, du0_ref, gwp_ref, dsh_ref, ush_ref):
        step = pl.program_id(0)
        tile = nt - 1 - step

        @pl.when(step == 0)
        def _():
            dwin_ref[ts:ts + HALO, :] = jnp.zeros((HALO, D), F32)
            gwp_ref[...] = jnp.zeros_like(gwp_ref)
            gv_ref[...] = jnp.zeros_like(gv_ref)

        ag = ag_ref[...]
        sg = _sigmoid(ag)
        u1 = u1_ref[...]
        xc = u1 - _rowmean(u1)
        rstd = lax.rsqrt(_rowmean(xc * xc) + EPS)
        xh = xc * rstd
        u2 = xh * lw_ref[...] + lb_ref[...]
        s2 = _sigmoid(u2)
        dz = dza_ref[...]
        du3 = dz * (ag * sg)
        dpa_ref[:, 2 * D:3 * D] = (dz * (u2 * s2) * _dsilu(ag, sg)).astype(BF16)
        du2 = du3 * _dsilu(u2, s2)
        gv_ref[0:1, :] += _colsum(du2 * xh)
        gv_ref[1:2, :] += _colsum(du2)
        dxh = du2 * lw_ref[...]
        du1 = rstd * (dxh - _rowmean(dxh) - xh * _rowmean(dxh * xh))
        gv_ref[2:3, :] += _colsum(du1)
        dwin_ref[0:ts, :] = du1

        uwin_ref[0:HALO, :] = jnp.where(tile == 0, 0.0, u0p_ref[...])
        uwin_ref[HALO:HALO + ts, :] = u0_ref[...]

        _shifted_copies(dwin_ref, dsh_ref, ts + HALO - 8)
        _shifted_copies(uwin_ref, ush_ref, ts + HALO - 8)
        pieces = _conv_taps(dwin_ref, dsh_ref, w_ref, ts, chunk, lambda j: (KC - 1) - j)
        for n, acc in enumerate(pieces):
            du0_ref[n * chunk:(n + 1) * chunk, :] = acc
        for c0 in range(0, ts, chunk):
            dchunk = dwin_ref[c0:c0 + chunk, :]
            for j in range(KC):
                prod = dchunk * _tap_rows(uwin_ref, ush_ref, c0 + HALO - (KC - 1) + j, chunk)
                gwp_ref[8 * j:8 * j + 8, :] += jnp.sum(prod.reshape(chunk // 8, 8, D), axis=0)
        dwin_ref[ts:ts + HALO, :] = dwin_ref[0:HALO, :]

        du0 = du0_ref[...]
        al = al_ref[...]
        sl = _sigmoid(al)
        dpa_ref[:, 0:D] = (du0 * sl).astype(BF16)
        dpa_ref[:, D:2 * D] = (du0 * av_ref[...] * sl * (1.0 - sl)).astype(BF16)

        @pl.when(step == nt - 1)
        def _():
            for j in range(KC):
                gw_ref[j:j + 1, :] = _colsum(gwp_ref[8 * j:8 * j + 8, :])
            gw_ref[KC:HALO, :] = jnp.zeros((HALO - KC, D), F32)

    rev = lambda i: nt - 1 - i
    col = lambda c: pl.BlockSpec((ts, D), lambda i, c=c: (rev(i), c))
    row = pl.BlockSpec((ts, D), lambda i: (rev(i), 0))
    vec = pl.BlockSpec((1, D), lambda i: (0, 0))
    halo = pl.BlockSpec((HALO, D), lambda i: (jnp.maximum(rev(i) * per - 1, 0), 0))
    side_in, side_out, side_shapes, side_sems = _scatter_operands(parts, True)
    outs = pl.pallas_call(
        _scatter_alongside(body, 10, 3, len(parts), nt - 1, _device_scatter_copies), grid=(nt,),
        in_specs=[row, col(0), col(1), col(2), row, halo, row, pl.BlockSpec((HALO, D), lambda i: (0, 0)), vec, vec]
        + side_in,
        out_specs=[pl.BlockSpec((ts, A_COLS), lambda i: (rev(i), 0)),
                   pl.BlockSpec((HALO, D), lambda i: (0, 0)), pl.BlockSpec((8, D), lambda i: (0, 0))] + side_out,
        out_shape=[jax.ShapeDtypeStruct((s, A_COLS), BF16), jax.ShapeDtypeStruct((HALO, D), F32),
                   jax.ShapeDtypeStruct((8, D), F32)] + side_shapes,
        scratch_shapes=[pltpu.VMEM((ts + HALO, D), F32), pltpu.VMEM((ts + HALO, D), F32),
                        pltpu.VMEM((ts, D), F32), pltpu.VMEM((8 * HALO, D), F32),
                        pltpu.VMEM((7, ts + HALO, D), F32), pltpu.VMEM((7, ts + HALO, D), F32)] + side_sems,
        name="conv_bwd", compiler_params=_params(("arbitrary",), VMEM_BIG))(
            dza, proj_a, proj_a, proj_a, u0, u0, u1, conv_w, ln_w, ln_b, *parts)
    return outs[0], outs[1], outs[2], list(outs[3:])


def _mla_prep(proj_l, q_norm_w, kv_norm_w, w_uq2, w_ukv, cos_t, sin_t, ts):
    s = proj_l.shape[0]

    def body(pl_ref, qw_ref, kw_ref, wq_ref, wkv_ref, c_ref, s_ref, qn_ref, kvn_ref, q_ref, k_ref, v_ref):
        first = _first_half_mask(ts)
        cs = c_ref[...]
        sn = s_ref[...]

        def rms(v, w):
            return v * lax.rsqrt(_rowmean(v * v) + EPS) * w

        def rope(v):
            return v * cs + _swap_halves(v, first) * sn

        qn = rms(pl_ref[:, 0:RQ], qw_ref[...]).astype(BF16)
        kvn = rms(pl_ref[:, RQ:2 * RQ], kw_ref[...]).astype(BF16)
        qn_ref[...] = qn
        kvn_ref[...] = kvn
        q = _dot(qn, wq_ref[...])
        kv = _dot(kvn, wkv_ref[...])
        kr = rope(pl_ref[:, 2 * RQ:2 * RQ + LANE]).astype(BF16)
        for h in range(H):
            q_ref[h, :, 0:DN] = q[:, DN * h:DN * (h + 1)].astype(BF16)
            q_ref[h, :, DN:2 * DN] = rope(q[:, H * DN + LANE * h:H * DN + LANE * (h + 1)]).astype(BF16)
            k_ref[h, :, 0:DN] = kv[:, 2 * DN * h:2 * DN * h + DN].astype(BF16)
            k_ref[h, :, DN:2 * DN] = kr
            v_ref[h, :, 0:DN] = kv[:, 2 * DN * h + DN:2 * DN * (h + 1)].astype(BF16)
            v_ref[h, :, DN:2 * DN] = jnp.ones((ts, DN), BF16)

    const = lambda shape: pl.BlockSpec(shape, lambda i: (0,) * len(shape))
    rowb = lambda w: pl.BlockSpec((ts, w), lambda i: (i, 0))
    head = lambda w: pl.BlockSpec((H, ts, w), lambda i: (0, i, 0))
    return pl.pallas_call(
        body, grid=(s // ts,),
        in_specs=[rowb(L_COLS), const((1, RQ)), const((1, RQ)), const((RQ, 2 * H * DN)), const((RQ, 2 * H * DN)),
                  rowb(LANE), rowb(LANE)],
        out_specs=[rowb(RQ), rowb(RQ), head(2 * DN), head(2 * DN), head(2 * DN)],
        out_shape=[jax.ShapeDtypeStruct((s, RQ), BF16), jax.ShapeDtypeStruct((s, RQ), BF16),
                   jax.ShapeDtypeStruct((H, s, 2 * DN), BF16), jax.ShapeDtypeStruct((H, s, 2 * DN), BF16),
                   jax.ShapeDtypeStruct((H, s, 2 * DN), BF16)],
        name="mla_prep", compiler_params=_params(("parallel",), VMEM_BIG))(
            proj_l, q_norm_w, kv_norm_w, w_uq2, w_ukv, cos_t, sin_t)


def _mla_prep_bwd(dq, dk, dv, proj_l, qn, kvn, q_norm_w, kv_norm_w, w_uq2, w_ukv, cos_t, sin_t, ts):
    s = proj_l.shape[0]

    def body(dq_ref, dk_ref, dv_ref, pl_ref, qn_ref, kvn_ref, qw_ref, kw_ref, wq_ref, wkv_ref, c_ref, s_ref,
             dpl_ref, gwq_ref, gwkv_ref, gv_ref, dq2_ref, dkv2_ref):
        @pl.when(pl.program_id(0) == 0)
        def _():
            gwq_ref[...] = jnp.zeros_like(gwq_ref)
            gwkv_ref[...] = jnp.zeros_like(gwkv_ref)
            gv_ref[...] = jnp.zeros_like(gv_ref)

        first = _first_half_mask(ts)
        cs = c_ref[...] * ATT_SCALE
        sn = s_ref[...] * ATT_SCALE

        def rope_bwd(g):
            return g * cs + _swap_halves(g * sn, first)

        def rms_bwd(v, w, dy):
            r = lax.rsqrt(_rowmean(v * v) + EPS)
            vh = v * r
            dvh = dy * w
            return r * (dvh - vh * _rowmean(dvh * vh)), _colsum(dy * vh)

        dkr = None
        for h in range(H):
            dq2_ref[:, DN * h:DN * (h + 1)] = (dq_ref[h, :, 0:DN] * ATT_SCALE).astype(BF16)
            dq2_ref[:, H * DN + LANE * h:H * DN + LANE * (h + 1)] = rope_bwd(dq_ref[h, :, DN:2 * DN]).astype(BF16)
            dkv2_ref[:, 2 * DN * h:2 * DN * h + DN] = (dk_ref[h, :, 0:DN] * ATT_SCALE).astype(BF16)
            dkv2_ref[:, 2 * DN * h + DN:2 * DN * (h + 1)] = dv_ref[h].astype(BF16)
            part = dk_ref[h, :, DN:2 * DN]
            dkr = part if dkr is None else dkr + part

        dq2 = dq2_ref[...]
        dkv2 = dkv2_ref[...]
        gwq_ref[...] += _dot_tn(qn_ref[...], dq2)
        gwkv_ref[...] += _dot_tn(kvn_ref[...], dkv2)
        dcq, gq = rms_bwd(pl_ref[:, 0:RQ], qw_ref[...], _dot_nt(dq2, wq_ref[...]))
        dckv, gkv = rms_bwd(pl_ref[:, RQ:2 * RQ], kw_ref[...], _dot_nt(dkv2, wkv_ref[...]))
        gv_ref[0:1, :] += gq
        gv_ref[1:2, :] += gkv
        dpl_ref[:, 0:RQ] = dcq.astype(BF16)
        dpl_ref[:, RQ:2 * RQ] = dckv.astype(BF16)
        dpl_ref[:, 2 * RQ:2 * RQ + LANE] = rope_bwd(dkr).astype(BF16)

    const = lambda shape: pl.BlockSpec(shape, lambda i: (0,) * len(shape))
    rowb = lambda w: pl.BlockSpec((ts, w), lambda i: (i, 0))
    head = lambda w: pl.BlockSpec((H, ts, w), lambda i: (0, i, 0))
    return pl.pallas_call(
        body, grid=(s // ts,),
        in_specs=[head(2 * DN), head(2 * DN), head(DN), rowb(L_COLS), rowb(RQ), rowb(RQ), const((1, RQ)),
                  const((1, RQ)), const((RQ, 2 * H * DN)), const((RQ, 2 * H * DN)), rowb(LANE), rowb(LANE)],
        out_specs=[rowb(L_COLS), const((RQ, 2 * H * DN)), const((RQ, 2 * H * DN)), const((8, RQ))],
        out_shape=[jax.ShapeDtypeStruct((s, L_COLS), BF16), jax.ShapeDtypeStruct((RQ, 2 * H * DN), F32),
                   jax.ShapeDtypeStruct((RQ, 2 * H * DN), F32), jax.ShapeDtypeStruct((8, RQ), F32)],
        scratch_shapes=[pltpu.VMEM((ts, 2 * H * DN), BF16), pltpu.VMEM((ts, 2 * H * DN), BF16)],
        name="mla_prep_bwd", compiler_params=_params(("arbitrary",), VMEM_BIG))(
            dq, dk, dv, proj_l, qn, kvn, q_norm_w, kv_norm_w, w_uq2, w_ukv, cos_t, sin_t)


def _causal_pairs(n, by_key):
    if by_key:
        pairs = [(i, j) for j in range(n) for i in range(j, n)]
    else:
        pairs = [(i, j) for i in range(n) for j in range(i + 1)]
    return (jnp.asarray(np.array([p[0] for p in pairs], np.int32)),
            jnp.asarray(np.array([p[1] for p in pairs], np.int32)))


ATT_SCALE = float((DN + DR) ** -0.5)
LOG2E = 1.4426950408889634
LN2 = 0.6931471805599453
ATT_HEADS_FWD = 8
ATT_HEADS = 4
W_IN_ROWS = 560
ATT_ROWS = 16


def _diag_width(r0, t):
    return min(t, -(-(r0 + ATT_ROWS) // LANE) * LANE)


def _diag_mask_rows(r0, width):
    rows = r0 + lax.broadcasted_iota(jnp.int32, (ATT_ROWS, width), 0)
    cols = lax.broadcasted_iota(jnp.int32, (ATT_ROWS, width), 1)
    return cols <= rows


def _diag_mask(t):
    return lax.broadcasted_iota(jnp.int32, (t, t), 1) <= lax.broadcasted_iota(jnp.int32, (t, t), 0)


def _attn_fwd(q, k, v, t):
    s = q.shape[1]
    n = s // t
    scale2 = float((DN + DR) ** -0.5) * LOG2E
    qi, ki = _causal_pairs(n, by_key=False)

    def body(qi_ref, ki_ref, q_ref, k_ref, v_ref, o_ref, lse_ref, *scratch):
        per_head = [scratch[5 * h:5 * h + 5] for h in range(ATT_HEADS_FWD)]
        p = pl.program_id(1)
        i = qi_ref[p]
        j = ki_ref[p]

        @pl.when(j == 0)
        def _():
            for m_sc, acc_sc, _, _, _ in per_head:
                m_sc[...] = jnp.full_like(m_sc, -jnp.inf)
                acc_sc[...] = jnp.zeros_like(acc_sc)

        def scores(h, diag):
            sc = _dot_nt(q_ref[h], k_ref[h])
            if diag:
                sc = jnp.where(_diag_mask(t), sc, -jnp.inf)
            per_head[h][2][...] = sc

        def rowmax(h, rows):
            per_head[h][4][rows, :] = jnp.max(per_head[h][2][rows, :], axis=-1, keepdims=True)

        def stats(h):
            m_sc, acc_sc, _, _, mx_sc = per_head[h]
            m_prev = m_sc[...]
            m_new = jnp.maximum(m_prev, mx_sc[...] * scale2)
            m_sc[...] = m_new
            acc_sc[...] = jnp.exp2(m_prev - m_new) * acc_sc[...]

        def probs(h, rows):
            m_sc, _, s_sc, p_sc, _ = per_head[h]
            p_sc[rows, :] = jnp.exp2(s_sc[rows, :] * scale2 - m_sc[rows, :]).astype(BF16)

        def values(h):
            _, acc_sc, _, p_sc, _ = per_head[h]
            acc_sc[...] += _dot(p_sc[...], v_ref[h])

        def step(diag):
            blocks = [slice(r0, r0 + ATT_ROWS) for r0 in range(0, t, ATT_ROWS)]
            for h in range(ATT_HEADS_FWD):
                scores(h, diag)
            for rows in blocks:
                rowmax(0, rows)
            stats(0)
            for h in range(ATT_HEADS_FWD):
                for rows in blocks:
                    probs(h, rows)
                    if h + 1 < ATT_HEADS_FWD:
                        rowmax(h + 1, rows)
                if h + 1 < ATT_HEADS_FWD:
                    stats(h + 1)
                values(h)

        @pl.when(j < i)
        def _():
            step(False)

        @pl.when(j == i)
        def _():
            step(True)
            for h, (m_sc, acc_sc, _, _, _) in enumerate(per_head):
                l = acc_sc[:, DN:2 * DN]
                o_ref[:, DN * h:DN * (h + 1)] = acc_sc[:, 0:DN] / l
                lse_ref[h] = (m_sc[...] + jnp.log2(l[:, 0:1])) * LN2

    hb = ATT_HEADS_FWD
    grid_spec = pltpu.PrefetchScalarGridSpec(
        num_scalar_prefetch=2, grid=(H // hb, int(qi.shape[0])),
        in_specs=[pl.BlockSpec((hb, t, 2 * DN), lambda h, p, qi, ki: (h, qi[p], 0)),
                  pl.BlockSpec((hb, t, 2 * DN), lambda h, p, qi, ki: (h, ki[p], 0)),
                  pl.BlockSpec((hb, t, 2 * DN), lambda h, p, qi, ki: (h, ki[p], 0))],
        out_specs=[pl.BlockSpec((t, hb * DN), lambda h, p, qi, ki: (qi[p], h)),
                   pl.BlockSpec((hb, t, 1), lambda h, p, qi, ki: (h, qi[p], 0))],
        scratch_shapes=[pltpu.VMEM((t, 1), F32), pltpu.VMEM((t, 2 * DN), F32), pltpu.VMEM((t, t), F32),
                        pltpu.VMEM((t, t), BF16), pltpu.VMEM((t, 1), F32)] * hb)
    return pl.pallas_call(
        body, grid_spec=grid_spec,
        out_shape=[jax.ShapeDtypeStruct((s, H * DN), F32), jax.ShapeDtypeStruct((H, s, 1), F32)],
        name="attn_fwd", compiler_params=_params(("parallel", "arbitrary"), VMEM_BIG))(qi, ki, q, k, v)


def _attn_bwd(q, k, v, do, lse, delta, t):
    s = q.shape[1]
    n = s // t
    scale = ATT_SCALE
    qi, ki = _causal_pairs(n, by_key=True)

    def body(qi_ref, ki_ref, q_ref, k_ref, v_ref, do_ref, lse_ref, dl_ref, dq_ref, dk_ref, dv_ref,
             dk_sc, dv_sc, s_sc, dp_sc, p_sc, ds_sc):
        p = pl.program_id(1)
        i = qi_ref[p]
        j = ki_ref[p]

        @pl.when(p == 0)
        def _():
            dq_ref[...] = jnp.zeros_like(dq_ref)

        @pl.when(i == j)
        def _():
            dk_sc[...] = jnp.zeros_like(dk_sc)
            dv_sc[...] = jnp.zeros_like(dv_sc)

        def step(diag):
            for h in range(ATT_HEADS):
                s_sc[h] = _dot_nt(q_ref[h], k_ref[h])
                dp_sc[h] = _dot_nt(do_ref[:, DN * h:DN * (h + 1)], v_ref[h, :, 0:DN])
            for h in range(ATT_HEADS):
                for r0 in range(0, t, ATT_ROWS):
                    rows = slice(r0, r0 + ATT_ROWS)
                    width = _diag_width(r0, t) if diag else t
                    sc = s_sc[h, rows, 0:width] * (scale * LOG2E)
                    if diag:
                        sc = jnp.where(_diag_mask_rows(r0, width), sc, -jnp.inf)
                    pr = jnp.exp2(sc - lse_ref[h, rows, :] * LOG2E)
                    ds = pr * (dp_sc[h, rows, 0:width] - dl_ref[h, rows, :])
                    p_sc[h, rows, 0:width] = pr.astype(BF16)
                    ds_sc[h, rows, 0:width] = ds.astype(BF16)
                    if width < t:
                        p_sc[h, rows, width:t] = jnp.zeros((ATT_ROWS, t - width), BF16)
                        ds_sc[h, rows, width:t] = jnp.zeros((ATT_ROWS, t - width), BF16)
            q_rows = pl.ds(pl.multiple_of(i * t, t), t)
            for h in range(ATT_HEADS):
                dv_sc[h] += _dot_tn(p_sc[h], do_ref[:, DN * h:DN * (h + 1)])
                dk_sc[h] += _dot_tn(ds_sc[h], q_ref[h])
                dq_ref[h, q_rows, :] += _dot(ds_sc[h], k_ref[h])

        @pl.when(i > j)
        def _():
            step(False)

        @pl.when(i == j)
        def _():
            step(True)

        @pl.when(i == n - 1)
        def _():
            dk_ref[...] = dk_sc[...]
            dv_ref[...] = dv_sc[...]

    hb = ATT_HEADS
    grid_spec = pltpu.PrefetchScalarGridSpec(
        num_scalar_prefetch=2, grid=(H // hb, int(qi.shape[0])),
        in_specs=[pl.BlockSpec((hb, t, 2 * DN), lambda h, p, qi, ki: (h, qi[p], 0)),
                  pl.BlockSpec((hb, t, 2 * DN), lambda h, p, qi, ki: (h, ki[p], 0)),
                  pl.BlockSpec((hb, t, 2 * DN), lambda h, p, qi, ki: (h, ki[p], 0)),
                  pl.BlockSpec((t, hb * DN), lambda h, p, qi, ki: (qi[p], h)),
                  pl.BlockSpec((hb, t, 1), lambda h, p, qi, ki: (h, qi[p], 0)),
                  pl.BlockSpec((hb, t, 1), lambda h, p, qi, ki: (h, qi[p], 0))],
        out_specs=[pl.BlockSpec((hb, s, 2 * DN), lambda h, p, qi, ki: (h, 0, 0), pipeline_mode=pl.Buffered(1)),
                   pl.BlockSpec((hb, t, 2 * DN), lambda h, p, qi, ki: (h, ki[p], 0)),
                   pl.BlockSpec((hb, t, DN), lambda h, p, qi, ki: (h, ki[p], 0))],
        scratch_shapes=[pltpu.VMEM((hb, t, 2 * DN), F32), pltpu.VMEM((hb, t, DN), F32),
                        pltpu.VMEM((hb, t, t), F32), pltpu.VMEM((hb, t, t), F32),
                        pltpu.VMEM((hb, t, t), BF16), pltpu.VMEM((hb, t, t), BF16)])
    return pl.pallas_call(
        body, grid_spec=grid_spec,
        out_shape=[jax.ShapeDtypeStruct((H, s, 2 * DN), F32), jax.ShapeDtypeStruct((H, s, 2 * DN), F32),
                   jax.ShapeDtypeStruct((H, s, DN), F32)],
        name="attn_bwd", compiler_params=_params(("parallel", "arbitrary"), VMEM_BIG))(
            qi, ki, q, k, v, do, lse, delta)


def _middle(za, o, proj_g, x, tgt, gate, fnw, wco, wao, wo, ts):
    s = x.shape[0]
    inv_d = 1.0 / D

    def body(za_ref, o_ref, bg_ref, ga_ref, gb_ref, x_ref, t_ref, gate_ref, fnw_ref, wco_ref, wao_ref, wo_ref,
             dx2_ref, dza_ref, do_ref, dl_ref, dpg_ref, lhs_ref, rhs_ref, vec_ref):
        @pl.when(pl.program_id(0) == 0)
        def _():
            vec_ref[...] = jnp.zeros_like(vec_ref)

        ov = o_ref[...]
        bg = bg_ref[...]
        sb = _sigmoid(bg)
        silu_b = bg * sb
        zb = (ov * silu_b).astype(BF16)
        lhs_ref[0] = za_ref[...]
        lhs_ref[1] = zb
        ya = _dot(za_ref[...], wco_ref[...])
        yb = _dot(zb, wao_ref[...])
        sa = _sigmoid(ga_ref[...])
        sg = _sigmoid(gb_ref[...])
        mg = (sa * ya + sg * yb).astype(BF16)
        lhs_ref[2] = mg
        mo = _dot(mg, wo_ref[...])
        gate_v = gate_ref[...]
        x2 = x_ref[...] + gate_v * mo
        r = lax.rsqrt(_rowmean(x2 * x2) + EPS)
        xh = x2 * r
        fw = fnw_ref[...]
        e = xh * fw - t_ref[...]
        vec_ref[2:3, :] += _colsum(e * e)
        dy = e * inv_d
        vec_ref[0:1, :] += _colsum(dy * xh)
        dxh = dy * fw
        dx2 = r * (dxh - xh * _rowmean(dxh * xh))
        dx2_ref[...] = dx2
        vec_ref[1:2, :] += _colsum(dx2 * mo)
        dmo = (gate_v * dx2).astype(BF16)
        rhs_ref[2] = dmo
        dmg = _dot_nt(dmo, wo_ref[...])
        dya = (sa * dmg).astype(BF16)
        dyb = (sg * dmg).astype(BF16)
        rhs_ref[0] = dya
        rhs_ref[1] = dyb
        dpg_ref[:, D:2 * D] = (dmg * ya * (sa * (1.0 - sa))).astype(BF16)
        dpg_ref[:, 2 * D:3 * D] = (dmg * yb * (sg * (1.0 - sg))).astype(BF16)
        dza_ref[...] = _dot_nt(dya, wco_ref[...])
        dzb = _dot_nt(dyb, wao_ref[...])
        dov = dzb * silu_b
        do_ref[...] = dov.astype(BF16)
        dpg_ref[:, 0:D] = (dzb * ov * _dsilu(bg, sb)).astype(BF16)
        dprod = dov * ov
        for h in range(H):
            dl_ref[h] = jnp.sum(dprod[:, DN * h:DN * (h + 1)], axis=-1, keepdims=True)

    col = lambda c: pl.BlockSpec((ts, D), lambda i, c=c: (i, c))
    row = pl.BlockSpec((ts, D), lambda i: (i, 0))
    vec = pl.BlockSpec((1, D), lambda i: (0, 0))
    wsp = pl.BlockSpec((D, D), lambda i: (0, 0))
    stack = pl.BlockSpec((3, ts, D), lambda i: (0, i, 0))
    bf = jax.ShapeDtypeStruct((s, D), BF16)
    ff = jax.ShapeDtypeStruct((s, D), F32)
    return pl.pallas_call(
        body, grid=(s // ts,),
        in_specs=[row, row, col(0), col(1), col(2), row, row, vec, vec, wsp, wsp, wsp],
        out_specs=[row, row, row, pl.BlockSpec((H, ts, 1), lambda i: (0, i, 0)),
                   pl.BlockSpec((ts, G_COLS), lambda i: (i, 0)), stack, stack,
                   pl.BlockSpec((8, D), lambda i: (0, 0))],
        out_shape=[ff, ff, bf, jax.ShapeDtypeStruct((H, s, 1), F32), jax.ShapeDtypeStruct((s, G_COLS), BF16),
                   jax.ShapeDtypeStruct((3, s, D), BF16), jax.ShapeDtypeStruct((3, s, D), BF16),
                   jax.ShapeDtypeStruct((8, D), F32)],
        name="middle", compiler_params=_params(("arbitrary",), VMEM_BIG))(
            za, o, proj_g, proj_g, proj_g, x, tgt, gate, fnw, wco, wao, wo)


def _input_bwd(dpa, dpl, dpg, wa, wl, wg, x, dx2, norm_w, scale, ts, parts):
    s = x.shape[0]

    def body(dpa_ref, dpl_ref, dpg_ref, wa_ref, wl_ref, wg_ref, x_ref, dx2_ref, nw_ref, sc_ref, gx_ref, gv_ref):
        @pl.when(pl.program_id(0) == 0)
        def _():
            gv_ref[...] = jnp.zeros_like(gv_ref)

        dh = (_dot_nt(dpa_ref[...], wa_ref[...]) + _dot_nt(dpl_ref[...], wl_ref[...])
              + _dot_nt(dpg_ref[...], wg_ref[...]))
        xv = x_ref[...]
        r = lax.rsqrt(_rowmean(xv * xv) + EPS)
        xh = xv * r
        nw = nw_ref[...]
        gv_ref[0:1, :] += _colsum(dh)
        gv_ref[1:2, :] += _colsum(dh * (xh * nw))
        dy = dh * (1.0 + sc_ref[...])
        gv_ref[2:3, :] += _colsum(dy * xh)
        dxh = dy * nw
        gx_ref[...] = dx2_ref[...] + r * (dxh - xh * _rowmean(dxh * xh))

    const = lambda shape: pl.BlockSpec(shape, lambda i: (0, 0))
    rowb = lambda w: pl.BlockSpec((ts, w), lambda i: (i, 0))
    side_in, side_out, side_shapes, side_sems = _scatter_operands(parts, False)
    outs = pl.pallas_call(
        _scatter_alongside(body, 10, 2, len(parts), s // ts - 1, _chip_scatter_copies), grid=(s // ts,),
        in_specs=[rowb(A_COLS), rowb(L_COLS), rowb(G_COLS), const((D, A_COLS)), const((D, L_COLS)),
                  const((D, G_COLS)), rowb(D), rowb(D), const((1, D)), const((1, D))] + side_in,
        out_specs=[rowb(D), const((8, D))] + side_out,
        out_shape=[jax.ShapeDtypeStruct((s, D), F32), jax.ShapeDtypeStruct((8, D), F32)] + side_shapes,
        scratch_shapes=side_sems,
        name="input_bwd", compiler_params=_params(("arbitrary",), VMEM_BIG))(
            dpa, dpl, dpg, wa, wl, wg, x, dx2, norm_w, scale, *parts)
    return outs[0], outs[1], list(outs[2:])


def _adamw_math(w, g, m, v):
    nm = ADAM_B1 * m + (1.0 - ADAM_B1) * g
    nv = ADAM_B2 * v + (1.0 - ADAM_B2) * (g * g)
    m_hat = nm / (1.0 - ADAM_B1 ** ADAM_STEP)
    v_hat = nv / (1.0 - ADAM_B2 ** ADAM_STEP)
    return -ADAM_LR * (m_hat / (jnp.sqrt(v_hat) + ADAM_EPS) + ADAM_WD * w), nm, nv


def _adamw(w, g, m, v, tr, name):
    lead, (rows, cols) = w.shape[:-2], w.shape[-2:]

    def body(w_ref, g_ref, m_ref, v_ref, d_ref, nm_ref, nv_ref):
        d_ref[...], nm_ref[...], nv_ref[...] = _adamw_math(w_ref[...], g_ref[...], m_ref[...], v_ref[...])

    blk = pl.BlockSpec((1,) * len(lead) + (tr, cols), lambda i: (0,) * len(lead) + (i, 0))
    shp = jax.ShapeDtypeStruct(w.shape, F32)
    return pl.pallas_call(
        body, grid=(rows // tr,), in_specs=[blk] * 4, out_specs=[blk] * 3, out_shape=[shp] * 3, name=name,
        compiler_params=_params(("parallel",), VMEM_BIG))(w, g.reshape(w.shape), m, v)


ROW_SHIFT, ROW_SCALE, ROW_NORM_W = 0, 1, 2
ROW_FINAL_NORM_W, ROW_GATE, ROW_LOSS = 8, 9, 10
ROW_LN_W, ROW_LN_B, ROW_CONV_B = 16, 17, 18
ROW_Q_NORM_W, ROW_KV_NORM_W = 24, 25
ROW_CONV_W = 32
SUM_ROWS = 64
VECTOR_ROWS = ((ROW_SHIFT, ROW_SCALE, ROW_GATE), (ROW_NORM_W,), (ROW_CONV_B,), (ROW_LN_W,), (ROW_LN_B,),
               (ROW_Q_NORM_W,), (ROW_KV_NORM_W,), (ROW_FINAL_NORM_W,))


def _small_finalize(gathered, vectors, conv, chip):
    n = len(vectors)
    cw = conv[0].shape[2]

    def body(chip_ref, g_ref, *refs):
        ins, outs = refs[:3 * n + 3], refs[3 * n + 3:]
        tot = g_ref[0]
        for k in range(1, N_DEV):
            tot = tot + g_ref[k]
        for p, rows in enumerate(VECTOR_ROWS):
            w_ref, m_ref, v_ref = ins[3 * p:3 * p + 3]
            g_out, d_out, nm_out, nv_out = outs[4 * p:4 * p + 4]
            width = w_ref.shape[1] // len(rows)
            for q, r in enumerate(rows):
                lanes = slice(q * width, (q + 1) * width)
                g = tot[r:r + 1, 0:width]
                g_out[:, lanes] = g
                d_out[:, lanes], nm_out[:, lanes], nv_out[:, lanes] = _adamw_math(
                    w_ref[:, lanes], g, m_ref[:, lanes], v_ref[:, lanes])
        cols = pl.ds(pl.multiple_of(chip_ref[0] * cw, LANE), cw)
        gc = g_ref[0, pl.ds(ROW_CONV_W, KC), cols]
        for k in range(1, N_DEV):
            gc = gc + g_ref[k, pl.ds(ROW_CONV_W, KC), cols]
        cw_ref, cm_ref, cv_ref = ins[3 * n:3 * n + 3]
        g_out, d_out, nm_out, nv_out, dmod_ref, loss_ref = outs[4 * n:]
        g_out[0] = gc
        d_out[0], nm_out[0], nv_out[0] = _adamw_math(cw_ref[0], gc, cm_ref[0], cv_ref[0])
        for k in range(N_DEV):
            for q, r in enumerate((ROW_SHIFT, ROW_SCALE, ROW_GATE)):
                dmod_ref[k:k + 1, q * D:(q + 1) * D] = g_ref[k, r:r + 1, :]
        loss_ref[...] = (0.5 / D) * jnp.sum(tot[ROW_LOSS:ROW_LOSS + 1, :], axis=-1, keepdims=True)

    flat_in = [a for triple in vectors for a in triple] + list(conv)
    shapes = [jax.ShapeDtypeStruct(w.shape, F32) for w, _, _ in vectors for _ in range(4)]
    shapes += [jax.ShapeDtypeStruct(conv[0].shape, F32)] * 4
    shapes += [jax.ShapeDtypeStruct((N_DEV, 3 * D), F32), jax.ShapeDtypeStruct((1, 1), F32)]
    whole = pl.BlockSpec(memory_space=pltpu.VMEM)
    return pl.pallas_call(
        body, out_shape=shapes,
        in_specs=[pl.BlockSpec(memory_space=pltpu.SMEM)] + [whole] * (1 + len(flat_in)),
        out_specs=[whole] * len(shapes), name="small_finalize")(chip, gathered, *flat_in)


def _ada_bwd(c_all_t, dmod_shard):
    def body(c_ref, d_ref, o_ref):
        cv = c_ref[...]
        o_ref[...] = jnp.dot(cv * _sigmoid(cv), d_ref[...], preferred_element_type=F32,
                             precision=lax.Precision.HIGHEST)

    return pl.pallas_call(
        body, out_shape=jax.ShapeDtypeStruct((D, dmod_shard.shape[1]), F32), name="ada_bwd")(c_all_t, dmod_shard)


def _sum_chip_slabs(arrived, part, place, tr, name, axis):
    n, rows, cols = arrived.shape
    per = rows // tr
    own_map = ((lambda i, pc: (pc[0], i, 0)) if part.shape[1] == rows
               else (lambda i, pc: (pc[0], pc[1] * per + i, 0)))

    def body(place_ref, a_ref, p_ref, o_ref):
        acc = p_ref[0].astype(F32)
        for k in range(n):
            acc = acc + a_ref[k].astype(F32)
        o_ref[...] = acc

    if axis == 1:
        whole, out_map = (2 * rows, cols), lambda i, pc: (pc[1] * per + i, 0)
    else:
        whole, out_map = (rows, 2 * cols), lambda i, pc: (i, pc[1])
    grid_spec = pltpu.PrefetchScalarGridSpec(
        num_scalar_prefetch=1, grid=(per,),
        in_specs=[pl.BlockSpec((n, tr, cols), lambda i, pc: (0, i, 0)),
                  pl.BlockSpec((1, tr, cols), own_map)],
        out_specs=pl.BlockSpec((tr, cols), out_map))
    return pl.pallas_call(
        body, grid_spec=grid_spec, out_shape=jax.ShapeDtypeStruct(whole, F32), name=name,
        compiler_params=_params(("parallel",)))(place, arrived, part)


def _sum_device_partials(arrived, parts, place):
    n = len(arrived)

    def body(place_ref, *refs):
        a_refs, p_refs, o_refs = refs[:n], refs[n:2 * n], refs[2 * n:]
        for a in range(n):
            acc = p_refs[a][0].astype(F32)
            for k in range(arrived[a].shape[0]):
                acc = acc + a_refs[a][k].astype(F32)
            o_refs[a][...] = acc

    grid_spec = pltpu.PrefetchScalarGridSpec(
        num_scalar_prefetch=1, grid=(1,),
        in_specs=[pl.BlockSpec(a.shape, lambda i, pc: (0, 0, 0)) for a in arrived]
        + [pl.BlockSpec((1,) + a.shape[1:], lambda i, pc: (pc[0], pc[1], 0)) for a in arrived],
        out_specs=[pl.BlockSpec(a.shape[1:], lambda i, pc: (pc[1], 0)) for a in arrived])
    return pl.pallas_call(
        body, grid_spec=grid_spec,
        out_shape=[jax.ShapeDtypeStruct((2 * a.shape[1], a.shape[2]), F32) for a in arrived],
        name="sum_device_partials", compiler_params=_params(("arbitrary",), VMEM_BIG))(place, *arrived, *parts)


def _adamw_many(ws, gs, ms, vs, tr):
    n = len(ws)
    rows = ws[0].shape[1]

    def body(*refs):
        ins, outs = refs[:4 * n], refs[4 * n:]
        for a in range(n):
            w_ref, g_ref, m_ref, v_ref = ins[4 * a:4 * a + 4]
            outs[3 * a][...], outs[3 * a + 1][...], outs[3 * a + 2][...] = _adamw_math(
                w_ref[...], g_ref[...], m_ref[...], v_ref[...])

    blk = lambda w: pl.BlockSpec((1, tr, w.shape[2]), lambda i: (0, i, 0))
    gs = [g.reshape(w.shape) for g, w in zip(gs, ws)]
    flat = [a for quad in zip(ws, gs, ms, vs) for a in quad]
    outs = pl.pallas_call(
        body, grid=(rows // tr,), in_specs=[blk(w) for w in ws for _ in range(4)],
        out_specs=[blk(w) for w in ws for _ in range(3)],
        out_shape=[jax.ShapeDtypeStruct(w.shape, F32) for w in ws for _ in range(3)], name="adamw_small_matrices",
        compiler_params=_params(("parallel",), VMEM_BIG))(*flat)
    return [(gs[a], outs[3 * a], outs[3 * a + 1], outs[3 * a + 2]) for a in range(n)]


def _add_own_half(full, other, core, tr, name, axis):
    n, rows, cols = other.shape
    per = rows // tr

    def body(c_ref, f_ref, o_ref, out_ref):
        out_ref[...] = (f_ref[...].astype(F32) + o_ref[...].astype(F32)).astype(BF16)

    full_map = (lambda k, i, c: (k, c[0] * per + i, 0)) if axis == 1 else (lambda k, i, c: (k, i, c[0]))
    grid_spec = pltpu.PrefetchScalarGridSpec(
        num_scalar_prefetch=1, grid=(n, per),
        in_specs=[pl.BlockSpec((1, tr, cols), full_map),
                  pl.BlockSpec((1, tr, cols), lambda k, i, c: (k, i, 0))],
        out_specs=pl.BlockSpec((1, tr, cols), lambda k, i, c: (k, i, 0)))
    return pl.pallas_call(
        body, grid_spec=grid_spec, out_shape=jax.ShapeDtypeStruct((n, rows, cols), BF16), name=name,
        compiler_params=_params(("parallel", "parallel"), VMEM_BIG))(core, full, other)


def _allgather8_run(x_ref, out_ref, send_sems, recv_sems, local_sem):
    m = x_ref.shape[0]
    x, y, c = _coords()
    me, sibling = (x, y, c), (x, y, 1 - c)
    chips = [(1 - x, y), (x, 1 - y), (1 - x, 1 - y)]

    def rows(px, py, pc):
        return out_ref.at[pl.ds(pl.multiple_of((4 * px + 2 * py + pc) * m, 8), m), :]

    def copy(k, blk, to, source=None):
        return pltpu.make_async_remote_copy(
            src_ref=rows(*blk) if source is None else source, dst_ref=rows(*blk),
            send_sem=send_sems.at[k], recv_sem=recv_sems.at[k], device_id=to, device_id_type=MESH)

    mine = pltpu.make_async_copy(x_ref, rows(*me), local_sem)
    mine.start()
    first = [copy(0, me, sibling, source=x_ref)]
    first += [copy(1 + j, me, (*chip, c), source=x_ref) for j, chip in enumerate(chips)]
    for cp in first:
        cp.start()
    passed = [copy(4 + j, (*chip, c), sibling) for j, chip in enumerate(chips)]
    for j, chip in enumerate(chips):
        copy(1 + j, (*chip, c), me).wait_recv()
        passed[j].start()
    copy(0, sibling, me).wait_recv()
    for j, chip in enumerate(chips):
        copy(4 + j, (*chip, 1 - c), me).wait_recv()
    for cp in first + passed:
        cp.wait_send()
    mine.wait()


ALLGATHER8_SEMS = [pltpu.SemaphoreType.DMA((7,)), pltpu.SemaphoreType.DMA((7,)), pltpu.SemaphoreType.DMA]


def _gather_plan(x_refs, out_refs, send_sems, recv_sems, local_sems):
    n = len(x_refs)
    halves = [r.shape[0] // 2 for r in x_refs]
    x, y, c = _coords()
    me, sibling = (x, y, c), (x, y, 1 - c)
    chips = [(1 - x, y), (x, 1 - y), (1 - x, 1 - y)]

    def src(a):
        return x_refs[a].at[pl.ds(pl.multiple_of(c * halves[a], 16), halves[a]), :]

    def blk(a, px, py, pc):
        return out_refs[a].at[4 * px + 2 * py + pc]

    def copy(a, k, who, to, source=None):
        return pltpu.make_async_remote_copy(
            src_ref=blk(a, *who) if source is None else source, dst_ref=blk(a, *who),
            send_sem=send_sems.at[7 * a + k], recv_sem=recv_sems.at[7 * a + k], device_id=to, device_id_type=MESH)

    def mine(a):
        return pltpu.make_async_copy(src(a), blk(a, *me), local_sems.at[a])

    def first(a):
        return ([copy(a, 0, me, sibling, source=src(a))]
                + [copy(a, 1 + j, me, (*chip, c), source=src(a)) for j, chip in enumerate(chips)])

    def begin():
        for a in range(n):
            mine(a).start()
        for a in range(n):
            for cp in first(a):
                cp.start()

    def finish():
        onward = []
        for j, chip in enumerate(chips):
            for a in range(n):
                copy(a, 1 + j, (*chip, c), me).wait_recv()
                onward.append(copy(a, 4 + j, (*chip, c), sibling))
                onward[-1].start()
        for a in range(n):
            copy(a, 0, sibling, me).wait_recv()
        for j, chip in enumerate(chips):
            for a in range(n):
                copy(a, 4 + j, (*chip, 1 - c), me).wait_recv()
        for a in range(n):
            for cp in first(a):
                cp.wait_send()
        for cp in onward:
            cp.wait_send()
        for a in range(n):
            mine(a).wait()

    return begin, finish


def _gather_operands(shards):
    n = len(shards)
    shapes = [jax.ShapeDtypeStruct((N_DEV, a.shape[0] // 2, a.shape[1]), a.dtype) for a in shards]
    sems = [pltpu.SemaphoreType.DMA((7 * n,)), pltpu.SemaphoreType.DMA((7 * n,)), pltpu.SemaphoreType.DMA((n,))]
    return shapes, sems


def _as_chip_slabs(gathered, shards):
    return [o.reshape(N_CHIP, a.shape[0], a.shape[1]) for o, a in zip(gathered, shards)]


def _gather_alongside(body, n_in, n_out, n_shards, last_step, first=None):
    def wrapped(*refs):
        ins, shards = refs[:n_in], refs[n_in:n_in + n_shards]
        rest = refs[n_in + n_shards:]
        outs, gathered = rest[:n_out], rest[n_out:n_out + n_shards]
        scratch, sems = rest[n_out + n_shards:-3], rest[-3:]

        @pl.when(pl.program_id(0) == 0)
        def _():
            if first is not None:
                first(*ins, *outs, *scratch)
            _gather_plan(shards, gathered, *sems)[0]()

        body(*ins, *outs, *scratch)

        @pl.when(pl.program_id(0) == last_step)
        def _():
            _gather_plan(shards, gathered, *sems)[1]()

    return wrapped


def _half(ref, axis, which, ndim):
    size = ref.shape[axis] // 2
    idx = [slice(None)] * ndim
    idx[axis] = pl.ds(pl.multiple_of(which * size, 8 if axis == ndim - 2 else LANE), size)
    return ref.at[tuple(idx)]


def _swap_halves_with_sibling(fulls, name, axes):
    n = len(fulls)

    def body(*refs):
        f_refs, got_refs = refs[:n], refs[n:2 * n]
        send_sems, recv_sems = refs[2 * n:]
        x, y, c = _coords()
        copies = []
        for a in range(n):
            copies.append(pltpu.make_async_remote_copy(
                src_ref=_half(f_refs[a], axes[a], 1 - c, 3), dst_ref=got_refs[a], send_sem=send_sems.at[a],
                recv_sem=recv_sems.at[a], device_id=(x, y, 1 - c), device_id_type=MESH))
        for cp in copies:
            cp.start()
        for cp in copies:
            cp.wait()

    def halved(a, axis):
        shape = list(a.shape)
        shape[axis] //= 2
        return jax.ShapeDtypeStruct(tuple(shape), a.dtype)

    return pl.pallas_call(
        body, out_shape=[halved(a, ax) for a, ax in zip(fulls, axes)],
        in_specs=[HBM_REF] * n, out_specs=[HBM_REF] * n,
        scratch_shapes=[pltpu.SemaphoreType.DMA((n,)), pltpu.SemaphoreType.DMA((n,))],
        name=name)(*fulls)


def _join_halves_with_sibling(wholes, axes, block):
    n = len(wholes)
    twice = jnp.concatenate([block, block], axis=0)
    gathered_shapes, gather_sems = _gather_operands([twice])

    def body(*refs):
        out_refs = refs[n + 1:2 * n + 1]
        send_sems, recv_sems = refs[2 * n + 2:2 * n + 4]
        begin, finish = _gather_plan([refs[n]], [refs[2 * n + 1]], *refs[2 * n + 4:])
        x, y, c = _coords()

        def push(a, core):
            half = _half(out_refs[a], axes[a] - 1, core, 2)
            return pltpu.make_async_remote_copy(
                src_ref=half, dst_ref=half, send_sem=send_sems.at[a], recv_sem=recv_sems.at[a],
                device_id=(x, y, 1 - c), device_id_type=MESH)

        begin()
        for a in range(n):
            push(a, c).start()
        finish()
        for a in range(n):
            push(a, 1 - c).wait_recv()
        for a in range(n):
            push(a, c).wait_send()

    outs = pl.pallas_call(
        body, out_shape=[jax.ShapeDtypeStruct(a.shape, a.dtype) for a in wholes] + gathered_shapes,
        in_specs=[HBM_REF] * (n + 1), out_specs=[HBM_REF] * (n + 1), input_output_aliases={a: a for a in range(n)},
        scratch_shapes=[pltpu.SemaphoreType.DMA((n,)), pltpu.SemaphoreType.DMA((n,))] + gather_sems,
        name="rs_pair_join")(*wholes, twice)
    return outs[:n], outs[n]


def _cols_to_slabs(g):
    rows, cols = g.shape
    return g.reshape(rows, N_CHIP, cols // N_CHIP).transpose(1, 0, 2)


def _slabs_to_cols(w):
    n, rows, cols = w.shape
    return w.transpose(1, 0, 2).reshape(rows, n * cols)


def _col_window(slabs, start, stop):
    n = slabs.shape[2]
    pieces = []
    for k in range(N_CHIP):
        lo, hi = max(start, k * n), min(stop, (k + 1) * n)
        if lo < hi:
            pieces.append(slabs[k][:, lo - k * n:hi - k * n])
    return pieces[0] if len(pieces) == 1 else jnp.concatenate(pieces, axis=1)


def _slabs_from_groups(groups, n):
    slabs = []
    for k in range(N_CHIP):
        pieces, off = [], 0
        for g in groups:
            lo, hi = max(k * n, off), min((k + 1) * n, off + g.shape[0])
            if lo < hi:
                pieces.append(g[lo - off:hi - off])
            off += g.shape[0]
        slabs.append(pieces[0] if len(pieces) == 1 else jnp.concatenate(pieces, axis=0))
    return jnp.stack(slabs)


def _uq_to_padded(w_uq):
    per = w_uq.reshape(RQ, H, DN + DR)
    nope = per[:, :, :DN].reshape(RQ, H * DN)
    rope = jnp.pad(per[:, :, DN:], ((0, 0), (0, 0), (0, LANE - DR))).reshape(RQ, H * LANE)
    return jnp.concatenate([nope, rope], axis=1)


def _uq_from_padded(g):
    nope = g[:, :H * DN].reshape(RQ, H, DN)
    rope = g[:, H * DN:].reshape(RQ, H, LANE)[:, :, :DR]
    return jnp.concatenate([nope, rope], axis=2).reshape(RQ, H * (DN + DR))


def _rope_tables(positions):
    inv_freq = ROPE_THETA ** (-jnp.arange(0, DR, 2, dtype=F32) / DR)
    ang = positions.astype(F32)[:, None] * inv_freq
    cos, sin = jnp.cos(ang), jnp.sin(ang)
    return jnp.tile(cos, (1, 4)), jnp.tile(jnp.concatenate([-sin, sin], axis=1), (1, 2))


def _pair_sums(fulls, core, tag, axes, tr):
    from_sibling = _swap_halves_with_sibling(fulls, f"rs_pair_swap_{tag}", axes)
    return [_add_own_half(f, o, core, min(tr, o.shape[1]), f"add_own_half_{tag}{n}", ax)
            for n, (f, o, ax) in enumerate(zip(fulls, from_sibling, axes))]


def _local_step(x, tgt, cos_t, sin_t, ada, weights, small, tiles, place):
    ts, ts_in, ts_mla, tm_nn, tm_tn, t_attn, chunk = tiles
    w_in_shard, later_shards = weights
    norm_w, conv_b, ln_w, ln_b, q_norm_w, kv_norm_w, fnw = small
    h, mod, c_all, conv_w, (g_in,) = _adaln_norm(x, norm_w, *ada, ts, [w_in_shard])
    scale, gate = mod[:, D:2 * D], mod[:, 2 * D:3 * D]
    wa = _col_window(g_in, 0, A_COLS)
    wl = jnp.pad(_col_window(g_in, A_COLS, A_COLS + L_COLS_RAW), ((0, 0), (0, L_COLS - L_COLS_RAW)))
    wg = _col_window(g_in, A_COLS + L_COLS_RAW, IN_COLS)
    proj_a = _mm_nn(h, wa, tm_nn, D, "proj_a")
    u0, u1, za, (g_uq, g_ukv, g_co, g_ao, g_o) = _conv_fwd(proj_a, conv_w, conv_b, ln_w, ln_b, ts, chunk, later_shards)
    w_uq2, w_ukv = _uq_to_padded(_slabs_to_cols(g_uq)), _slabs_to_cols(g_ukv)
    wco, wao, wo = g_co.reshape(D, D), g_ao.reshape(D, D), g_o.reshape(D, D)
    proj_l = _mm_nn(h, wl, tm_nn, L_COLS, "proj_l")
    proj_g = _mm_nn(h, wg, tm_nn, D, "proj_g")
    qn, kvn, q, k, v = _mla_prep(proj_l, q_norm_w, kv_norm_w, w_uq2, w_ukv, cos_t, sin_t, ts_mla)
    o, lse = _attn_fwd(q, k, v, t_attn)
    dx2, dza, do, delta, dpg, lhs3, rhs3, vec_mid = _middle(za, o, proj_g, x, tgt, gate, fnw, wco, wao, wo, ts)
    g_wco, g_wao, g_wo = _mm_tn_stack(lhs3, rhs3, tm_tn, "grad_w_out3")
    dq, dk, dv = _attn_bwd(q, k, v, do, lse, delta, t_attn)
    dpl, g_wuq2, g_wukv, vec_mla = _mla_prep_bwd(
        dq, dk, dv, proj_l, qn, kvn, q_norm_w, kv_norm_w, w_uq2, w_ukv, cos_t, sin_t, ts_mla)

    core = place[1:2]
    nr = D // N_CHIP
    early = [_cols_to_slabs(_uq_from_padded(g_wuq2)).astype(BF16), _cols_to_slabs(g_wukv).astype(BF16),
             g_wco.reshape(N_CHIP, nr, D), g_wao.reshape(N_CHIP, nr, D), g_wo.reshape(N_CHIP, nr, D)]
    dpa, g_conv_w, vec_conv, early_got = _conv_bwd(dza, proj_a, u0, u1, conv_w, ln_w, ln_b, ts, chunk // 2, early)

    g_wa_t = _mm_tn(dpa, h, tm_tn, D, D, "grad_w_in_a", BF16)
    g_wl_t = _mm_tn(dpl, h, tm_tn, L_COLS, D, "grad_w_in_l", BF16)
    g_wg_t = _mm_tn(dpg, h, tm_tn, D, D, "grad_w_in_g", BF16)
    g_w_in_slabs = _slabs_from_groups([g_wa_t, g_wl_t[0:L_COLS_RAW], g_wg_t], IN_COLS // N_CHIP)
    late_sums = _pair_sums([g_w_in_slabs], core, "b", [2], IN_COLS // N_CHIP)
    grad_x, vec_in, late_got = _input_bwd(dpa, dpl, dpg, wa, wl, wg, x, dx2, norm_w, scale, ts_in, late_sums)

    col_sums = jnp.concatenate(
        [vec_in, vec_mid, vec_conv, jnp.pad(vec_mla, ((0, 0), (0, D - RQ))), g_conv_w], axis=0)
    wholes = ([_sum_chip_slabs(late_got[0], late_sums[0], place, W_IN_ROWS, "sum_chip_slabs_w_in", 2)]
              + list(_sum_device_partials(early_got, early, place)))
    shards, all_col_sums = _join_halves_with_sibling(wholes, [2] + [1] * len(early), col_sums)

    return grad_x, shards, all_col_sums, c_all


def kernel(x, c, positions, w_ada, b_ada, norm_w, w_in, conv_w, conv_b, conv_ln_w, conv_ln_b, w_conv_out, q_norm_w, w_uq, kv_norm_w, w_ukv, w_attn_out, w_out, final_norm_w, loss_target, m_w_ada, m_b_ada, m_norm_w, m_w_in, m_conv_w, m_conv_b, m_conv_ln_w, m_conv_ln_b, m_w_conv_out, m_q_norm_w, m_w_uq, m_kv_norm_w, m_w_ukv, m_w_attn_out, m_w_out, m_final_norm_w, v_w_ada, v_b_ada, v_norm_w, v_w_in, v_conv_w, v_conv_b, v_conv_ln_w, v_conv_ln_b, v_w_conv_out, v_q_norm_w, v_w_uq, v_kv_norm_w, v_w_ukv, v_w_attn_out, v_w_out, v_final_norm_w):
    ix, iy, ic = _coords()
    chip = 2 * ix + iy
    dev = 4 * ix + 2 * iy + ic
    s = x.shape[1]
    tiles = (256, 256, 512, 1024, 2048, 512, 32)

    conv_w_pad = jnp.pad(conv_w[0], ((0, HALO - KC), (0, D - conv_w.shape[2])))
    small_in = jnp.concatenate([jnp.pad(c, ((0, 7), (0, 0))), conv_w_pad], axis=0)

    later_shards = [w[0].astype(BF16) for w in (w_uq, w_ukv, w_conv_out, w_attn_out, w_out)]
    weights = (w_in[0].astype(BF16), later_shards)

    ada_cols = w_ada.shape[2]
    b_shard = lax.dynamic_slice(b_ada, (0, chip * ada_cols), (1, ada_cols))
    ada = (small_in, w_ada[0], b_shard, dev.reshape(1).astype(jnp.int32))

    cos_t, sin_t = _rope_tables(positions[0])
    small = (norm_w, conv_b, conv_ln_w, conv_ln_b, q_norm_w, kv_norm_w, final_norm_w.reshape(1, D))
    place = jnp.stack([chip, ic]).astype(jnp.int32)
    grad_x, shards, gathered, c_all = _local_step(x[0], loss_target[0], cos_t, sin_t, ada, weights, small, tiles, place)
    g_w_in_s, g_w_uq_s, g_w_ukv_s, g_wco_s, g_wao_s, g_wo_s = shards

    vec_names = ("b_ada", "norm_w", "conv_b", "conv_ln_w", "conv_ln_b", "q_norm_w", "kv_norm_w", "final_norm_w")
    row = lambda a: a.reshape(1, -1)
    vectors = [(row(b_ada), row(m_b_ada), row(v_b_ada)), (norm_w, m_norm_w, v_norm_w), (conv_b, m_conv_b, v_conv_b),
               (conv_ln_w, m_conv_ln_w, v_conv_ln_w), (conv_ln_b, m_conv_ln_b, v_conv_ln_b),
               (q_norm_w, m_q_norm_w, v_q_norm_w), (kv_norm_w, m_kv_norm_w, v_kv_norm_w),
               (row(final_norm_w), row(m_final_norm_w), row(v_final_norm_w))]
    fin = _small_finalize(gathered, vectors, (conv_w, m_conv_w, v_conv_w), place[0:1])
    res = {}
    for p, (name, (w, _, _)) in enumerate(zip(vec_names, vectors)):
        shape = final_norm_w.shape if name == "final_norm_w" else w.shape
        res[name] = tuple(a.reshape(shape) for a in fin[4 * p:4 * p + 4])
    res["conv_w"] = tuple(fin[4 * len(vectors):4 * len(vectors) + 4])
    dmod_all, loss = fin[-2], fin[-1].reshape(())
    dmod_shard = lax.dynamic_slice(dmod_all, (0, chip * ada_cols), (N_DEV, ada_cols))
    g_w_ada = _ada_bwd(c_all.T, dmod_shard).reshape(1, D, ada_cols)

    def big(w, g, m, v, tr, name):
        d, nm, nv = _adamw(w, g, m, v, tr, name)
        return g.reshape(w.shape), d, nm, nv

    res["w_ada"] = big(w_ada, g_w_ada[0], m_w_ada, v_w_ada, 256, "adamw_w_ada")
    t_in = [a[0].T for a in (w_in, m_w_in, v_w_in)]
    d_t, nm_t, nv_t = _adamw(t_in[0], g_w_in_s, t_in[1], t_in[2], W_IN_ROWS, "adamw_w_in")
    res["w_in"] = tuple(a.T[None] for a in (g_w_in_s, d_t, nm_t, nv_t))
    small = _adamw_many(
        [w_uq, w_ukv, w_conv_out, w_attn_out, w_out], [g_w_uq_s, g_w_ukv_s, g_wco_s, g_wao_s, g_wo_s],
        [m_w_uq, m_w_ukv, m_w_conv_out, m_w_attn_out, m_w_out], [v_w_uq, v_w_ukv, v_w_conv_out, v_w_attn_out, v_w_out],
        128)
    res["w_uq"], res["w_ukv"], res["w_conv_out"], res["w_attn_out"], res["w_out"] = small

    order = ("w_ada", "b_ada", "norm_w", "w_in", "conv_w", "conv_b", "conv_ln_w", "conv_ln_b", "w_conv_out",
             "q_norm_w", "w_uq", "kv_norm_w", "w_ukv", "w_attn_out", "w_out", "final_norm_w")
    outs = [loss, grad_x[None]]
    for slot in range(4):
        outs += [res[name][slot] for name in order]
    return tuple(outs)
```
